```python
import math
import jax, jax.numpy as jnp
from jax import lax
import numpy as np

D_MODEL = 1024
BATCH = 4
SEQ = 4096
DEPTH = 1

A_HEADS = 8
A_HEAD_DIM = 64
A_W = A_HEADS * A_HEAD_DIM
IDX_HEADS = 8
IDX_DIM = 64
IDX_SCALE = (IDX_HEADS * IDX_DIM) ** -0.5
TOPK_MAX = 256
SB_HEADS = 8
SB_HEAD_DIM = 64
SB_W = SB_HEADS * SB_HEAD_DIM
Q_BLOCK = 128
N_BUCKETS = 32
MAX_DISTANCE = 128
N_EXPERTS = 32
TOP_K = 4
D_EXPERT = 1024
SWIGLU_LIMIT = 7.0
SWIGLU_ALPHA = 1.702
EXPERT_BLOCK = 128
LN_EPS = 1e-5
DEEPNORM_ALPHA = (2 * DEPTH) ** 0.25
DEEPNORM_BETA = (8 * DEPTH) ** -0.25
SPLIT_SIZES = (A_W, A_W, A_W, IDX_HEADS * IDX_DIM, IDX_DIM, IDX_HEADS, SB_W, SB_W, SB_W, D_MODEL, D_MODEL)
IN_W = sum(SPLIT_SIZES)

kernel_name = "hybrid_dsa_stickbreak_moe_deepnorm"


def layer_norm(x, g, b):
    xf = x.astype(jnp.float32)
    mu = jnp.mean(xf, axis=-1, keepdims=True)
    var = jnp.mean(jnp.square(xf - mu), axis=-1, keepdims=True)
    y = (xf - mu) * lax.rsqrt(var + LN_EPS)
    return (y * g.astype(jnp.float32) + b.astype(jnp.float32)).astype(x.dtype)


def t5_bucket(rel):
    n = jnp.maximum(rel, 0)
    max_exact = N_BUCKETS // 2
    nf = jnp.maximum(n, 1).astype(jnp.float32)
    large = max_exact + (jnp.log(nf / max_exact) / math.log(MAX_DISTANCE / max_exact) * (N_BUCKETS - max_exact)).astype(jnp.int32)
    large = jnp.minimum(large, N_BUCKETS - 1)
    return jnp.where(n < max_exact, n, large)


def dsa_attention(q, k, v, q_idx, k_idx, w_idx, rel_bias):
    B, S, H, Dh = q.shape
    n_sel = min(TOPK_MAX, S // 4)
    key_pos = jnp.arange(S, dtype=jnp.int32)

    def block(i):
        start = i * Q_BLOCK
        qb = lax.dynamic_slice_in_dim(q, start, Q_BLOCK, axis=1)
        qib = lax.dynamic_slice_in_dim(q_idx, start, Q_BLOCK, axis=1)
        wb = lax.dynamic_slice_in_dim(w_idx, start, Q_BLOCK, axis=1)
        t_pos = start + jnp.arange(Q_BLOCK, dtype=jnp.int32)
        head_scores = jax.nn.relu(jnp.einsum('bthd,bsd->bths', qib, k_idx))
        score = jnp.einsum('bth,bths->bts', wb, head_scores).astype(jnp.float32) * IDX_SCALE
        causal = key_pos[None, :] <= t_pos[:, None]
        score = jnp.where(causal[None], score, -jnp.inf)
        _, sel = lax.top_k(score, n_sel)
        k_sel = jax.vmap(lambda kk, ii: kk[ii])(k, sel)
        v_sel = jax.vmap(lambda vv, ii: vv[ii])(v, sel)
        logits = jnp.einsum('bthd,btkhd->bthk', qb, k_sel).astype(jnp.float32) * (Dh ** -0.5)
        bias = rel_bias[t5_bucket(t_pos[None, :, None] - sel)]
        logits = logits + jnp.moveaxis(bias, -1, 2).astype(jnp.float32)
        valid = (sel <= t_pos[None, :, None])[:, :, None, :]
        logits = jnp.where(valid, logits, -jnp.inf)
        p = jax.nn.softmax(logits, axis=-1).astype(v.dtype)
        return jnp.einsum('bthk,btkhd->bthd', p, v_sel)

    out = lax.map(block, jnp.arange(S // Q_BLOCK, dtype=jnp.int32))
    return jnp.moveaxis(out, 0, 1).reshape(B, S, H * Dh)


def stick_breaking(q, k, v):
    B, S, H, Dh = q.shape
    key_pos = jnp.arange(S, dtype=jnp.int32)

    def block(i):
        start = i * Q_BLOCK
        qb = lax.dynamic_slice_in_dim(q, start, Q_BLOCK, axis=1)
        t_pos = start + jnp.arange(Q_BLOCK, dtype=jnp.int32)
        z = jnp.einsum('bthd,bshd->bhts', qb, k).astype(jnp.float32) * (Dh ** -0.5)
        strict = key_pos[None, :] < t_pos[:, None]
        log_1m = jnp.where(strict, jax.nn.log_sigmoid(-z), 0.0)
        after = lax.cumsum(log_1m, axis=3, reverse=True) - log_1m
        w = jnp.where(strict, jnp.exp(jax.nn.log_sigmoid(z) + after), 0.0)
        return jnp.einsum('bhts,bshd->bthd', w.astype(v.dtype), v)

    out = lax.map(block, jnp.arange(S // Q_BLOCK, dtype=jnp.int32))
    return jnp.moveaxis(out, 0, 1).reshape(B, S, H * Dh)


def token_mixer(h, w_in, w_branch_a, w_branch_b, w_out, rel_bias):
    B, S, _ = h.shape
    proj = h @ w_in
    qa, ka, va, qi, ki, wi, qb, kb, vb, ga, gb = jnp.split(proj, np.cumsum(SPLIT_SIZES)[:-1].tolist(), axis=-1)
    qa = qa.reshape(B, S, A_HEADS, A_HEAD_DIM)
    ka = ka.reshape(B, S, A_HEADS, A_HEAD_DIM)
    va = va.reshape(B, S, A_HEADS, A_HEAD_DIM)
    qi = qi.reshape(B, S, IDX_HEADS, IDX_DIM)
    qb = qb.reshape(B, S, SB_HEADS, SB_HEAD_DIM)
    kb = kb.reshape(B, S, SB_HEADS, SB_HEAD_DIM)
    vb = vb.reshape(B, S, SB_HEADS, SB_HEAD_DIM)
    ya = dsa_attention(qa, ka, va, qi, ki, wi, rel_bias)
    yb = stick_breaking(qb, kb, vb)
    merged = jax.nn.sigmoid(ga) * (ya @ w_branch_a) + jax.nn.sigmoid(gb) * (yb @ w_branch_b)
    return merged @ w_out


def moe(h, w_router, b_router, w_gu, b_gu, w_dn, b_dn):
    Bsz, S, D = h.shape
    T = Bsz * S
    xf = h.reshape(T, D)
    logits = (xf @ w_router + b_router).astype(jnp.float32)
    top_v, top_e = lax.top_k(logits, TOP_K)
    gate = jax.nn.softmax(top_v, axis=-1)
    N = T * TOP_K
    flat_e = top_e.reshape(N).astype(jnp.int32)
    flat_tok = jnp.arange(N, dtype=jnp.int32) // TOP_K
    flat_g = gate.reshape(N)
    order = jnp.argsort(flat_e)
    sorted_e = flat_e[order]
    counts = jnp.zeros((N_EXPERTS,), jnp.int32).at[flat_e].add(1)
    padded = (counts + EXPERT_BLOCK - 1) // EXPERT_BLOCK * EXPERT_BLOCK
    offs = jnp.cumsum(counts) - counts
    pends = jnp.cumsum(padded)
    poffs = pends - padded
    r = jnp.arange(N, dtype=jnp.int32)
    dest = poffs[sorted_e] + (r - offs[sorted_e])
    P = N + N_EXPERTS * EXPERT_BLOCK
    nb = P // EXPERT_BLOCK
    row_tok = jnp.zeros((P,), jnp.int32).at[dest].set(flat_tok[order])
    row_g = jnp.zeros((P,), jnp.float32).at[dest].set(flat_g[order])
    blk_start = jnp.arange(nb, dtype=jnp.int32) * EXPERT_BLOCK
    blk_e = jnp.minimum(jnp.searchsorted(pends, blk_start, side='right'), N_EXPERTS - 1).astype(jnp.int32)

    def block_fn(args):
        tok, g, e = args
        xb = xf[tok]
        hgu = xb @ w_gu[e] + b_gu[e]
        a = jnp.minimum(hgu[:, :D_EXPERT], SWIGLU_LIMIT)
        u = jnp.clip(hgu[:, D_EXPERT:], -SWIGLU_LIMIT, SWIGLU_LIMIT)
        glu = a * jax.nn.sigmoid(a * SWIGLU_ALPHA)
        y = ((u + 1.0) * glu) @ w_dn[e] + b_dn[e]
        return y * g[:, None].astype(y.dtype)

    ys = lax.map(block_fn, (row_tok.reshape(nb, EXPERT_BLOCK), row_g.reshape(nb, EXPERT_BLOCK), blk_e))
    out = jnp.zeros((T, D), h.dtype).at[row_tok].add(ys.reshape(P, D).astype(h.dtype))
    return out.reshape(Bsz, S, D)


def setup_inputs(seed: int = 0) -> dict:
    key = jax.random.key(seed)
    ks = jax.random.split(key, 16)
    L = DEPTH
    E = N_EXPERTS
    F = D_EXPERT

    def nrm(k, shape, scale):
        return jax.random.normal(k, shape, jnp.float32) * scale

    return {
        "x": nrm(ks[0], (BATCH, SEQ, D_MODEL), 1.0),
        "w_in": nrm(ks[1], (L, D_MODEL, IN_W), D_MODEL ** -0.5),
        "w_branch_a": nrm(ks[2], (L, A_W, D_MODEL), A_W ** -0.5),
        "w_branch_b": nrm(ks[3], (L, SB_W, D_MODEL), SB_W ** -0.5),
        "w_out": nrm(ks[4], (L, D_MODEL, D_MODEL), D_MODEL ** -0.5 * DEEPNORM_BETA),
        "rel_bias": nrm(ks[5], (N_BUCKETS, A_HEADS), 0.2),
        "ln1_g": 1.0 + nrm(ks[6], (L, D_MODEL), 0.02),
        "ln1_b": nrm(ks[7], (L, D_MODEL), 0.02),
        "w_router": nrm(ks[8], (L, D_MODEL, E), D_MODEL ** -0.5),
        "b_router": nrm(ks[9], (L, E), 0.01),
        "w_gate_up": nrm(ks[10], (L, E, D_MODEL, 2 * F), D_MODEL ** -0.5),
        "b_gate_up": nrm(ks[11], (L, E, 2 * F), 0.01),
        "w_down": nrm(ks[12], (L, E, F, D_MODEL), F ** -0.5 * DEEPNORM_BETA),
        "b_down": nrm(ks[13], (L, E, D_MODEL), 0.01),
        "ln2_g": 1.0 + nrm(ks[14], (L, D_MODEL), 0.02),
        "ln2_b": nrm(ks[15], (L, D_MODEL), 0.02),
    }


def reference(x, w_in, w_branch_a, w_branch_b, w_out, rel_bias, ln1_g, ln1_b, w_router, b_router, w_gate_up, b_gate_up, w_down, b_down, ln2_g, ln2_b):
    h = x
    for l in range(DEPTH):
        m = token_mixer(h, w_in[l], w_branch_a[l], w_branch_b[l], w_out[l], rel_bias)
        h = layer_norm(DEEPNORM_ALPHA * h + m, ln1_g[l], ln1_b[l])
        f = moe(h, w_router[l], b_router[l], w_gate_up[l], b_gate_up[l], w_down[l], b_down[l])
        h = layer_norm(DEEPNORM_ALPHA * h + f, ln2_g[l], ln2_b[l])
    return h
```

```python
import functools
import math

import numpy as np
import jax
import jax.numpy as jnp
from jax import lax
from jax.experimental import pallas as pl
from jax.experimental.pallas import tpu as pltpu

F32 = jnp.float32
BF16 = jnp.bfloat16
I32 = jnp.int32

A_HEADS = 8
HEAD_DIM = 64
ATT_W = A_HEADS * HEAD_DIM
IDX_HEADS = 8
IDX_DIM = 64
IDX_SCALE = (IDX_HEADS * IDX_DIM) ** -0.5
TOPK_MAX = 256
N_BUCKETS = 32
MAX_DISTANCE = 128
N_EXPERTS = 32
TOP_K = 4
SWIGLU_LIMIT = 7.0
SWIGLU_ALPHA = 1.702
LN_EPS = 1e-5
QK_SCALE = HEAD_DIM ** -0.5

LANES = 128
HALF = LANES // 2
N_PAIRS = A_HEADS // 2
VMEM_LIMIT = 56 * 1024 * 1024

DSA_TQ = 128
DSA_CK = 256
SB_T = 256
MERGE_TM = 512
MOE_BLK = 256
COMB_TM = 256
BISECT_CAP = 48
NEG = -1e30

COL_QA, COL_KA, COL_VA, COL_QI, COL_QB, COL_KB, COL_VB = (g * ATT_W for g in range(7))
COL_KK = 7 * ATT_W
COL_WI = COL_KK + LANES
ATT_COLS = 3840


def _params(sem, vmem=VMEM_LIMIT):
    return pltpu.CompilerParams(dimension_semantics=sem, vmem_limit_bytes=vmem)


def _mm_kernel(x_ref, w_ref, o_ref, xb_ref):
    @pl.when(pl.program_id(1) == 0)
    def _cast():
        xb_ref[...] = x_ref[...].astype(BF16)

    o_ref[...] = jnp.dot(xb_ref[...], w_ref[...], preferred_element_type=F32).astype(o_ref.dtype)


def _matmul(x, w, out_dtype, tm, tn):
    M, K = x.shape
    N = w.shape[1]
    return pl.pallas_call(
        _mm_kernel,
        grid=(M // tm, N // tn),
        in_specs=[pl.BlockSpec((tm, K), lambda i, j: (i, 0)),
                  pl.BlockSpec((K, tn), lambda i, j: (0, j))],
        out_specs=pl.BlockSpec((tm, tn), lambda i, j: (i, j)),
        out_shape=jax.ShapeDtypeStruct((M, N), out_dtype),
        scratch_shapes=[pltpu.VMEM((tm, K), BF16)],
        compiler_params=_params(("arbitrary", "arbitrary")),
    )(x, w)


def _t5_bucket_np(n):
    n = np.maximum(n, 0)
    max_exact = N_BUCKETS // 2
    nf = np.maximum(n, 1).astype(np.float32)
    large = max_exact + (np.log(nf / max_exact) / math.log(MAX_DISTANCE / max_exact)
                         * (N_BUCKETS - max_exact)).astype(np.int32)
    large = np.minimum(large, N_BUCKETS - 1)
    return np.where(n < max_exact, n, large).astype(np.int32)


def _dsa_bucket_tiles(tq, ck):
    n_off = ck // LANES + 2
    i = np.arange(tq)[None, :, None]
    j = np.arange(ck)[None, None, :]
    o = np.arange(n_off)[:, None, None]
    return _t5_bucket_np(i - j + ck + LANES - LANES * o)


def _dsa_kernel(bucket_ref, relb_ref, q_ref, k_ref, v_ref, qi_ref, kk_ref, wi_ref, o_ref,
                sc_ref, bias_ref, m_ref, l_ref, acc_ref, *, tq, ck, n_sel, n_off, idx_bits):
    b = pl.program_id(0)
    i = pl.program_id(1)
    q0 = i * tq
    nck = (q0 + tq + ck - 1) // ck
    sub = ck // LANES

    @pl.when(jnp.logical_and(b == 0, i == 0))
    def _build_bias():
        def head_body(h, _):
            for o in range(n_off):
                for cb in range(sub):
                    bk = bucket_ref[o, :, cb * LANES:(cb + 1) * LANES]

                    def bucket_body(n, acc):
                        return jnp.where(bk == n, relb_ref[n, h], acc)

                    bias_ref[h, o, :, cb * LANES:(cb + 1) * LANES] = lax.fori_loop(
                        0, N_BUCKETS, bucket_body, jnp.zeros((tq, LANES), F32))
            return 0

        lax.fori_loop(0, A_HEADS, head_body, 0)

    lane = lax.broadcasted_iota(I32, (tq, LANES), 1)
    lo_half = lane < HALF
    row_pos = q0 + lax.broadcasted_iota(I32, (tq, 1), 0)

    wi = wi_ref[0].astype(F32)
    wcol = [wi[:, h:h + 1] * IDX_SCALE for h in range(IDX_HEADS)]
    qi_m = []
    for p in range(N_PAIRS):
        qi2 = qi_ref[0, :, p * LANES:(p + 1) * LANES].astype(F32)
        qi_m.append(jnp.where(lo_half, qi2, 0.0).astype(BF16))
        qi_m.append(jnp.where(lo_half, 0.0, qi2).astype(BF16))

    def score_chunk(c, _):
        c0 = pl.multiple_of(c * ck, ck)
        kk = kk_ref[0, pl.ds(c0, ck), :]
        acc = jnp.zeros((tq, ck), F32)
        for h in range(IDX_HEADS):
            s = lax.dot_general(qi_m[h], kk, (((1,), (1,)), ((), ())), preferred_element_type=F32)
            acc = acc + wcol[h] * jnp.maximum(s, 0.0)
        col = c0 + lax.broadcasted_iota(I32, (tq, ck), 1)
        sc_ref[c] = jnp.where(col <= row_pos, acc, -jnp.inf)
        return 0

    lax.fori_loop(0, nck, score_chunk, 0)

    kt = jnp.minimum(row_pos + 1, n_sel).astype(F32)

    def fold(fn, init):
        def body(c, acc):
            for j in range(sub):
                s = sc_ref[c, :, j * LANES:(j + 1) * LANES]
                idx = c * ck + j * LANES + lane
                acc = fn(acc, s, idx)
            return acc
        return lax.fori_loop(0, nck, body, init)

    def lanes(v):
        return jnp.broadcast_to(v, (tq, LANES))

    zeros = jnp.zeros((tq, LANES), F32)
    pinf = jnp.full((tq, LANES), jnp.inf, F32)

    def count_ge(th):
        thb = lanes(th)
        acc = fold(lambda a, s, idx: a + jnp.where(s >= thb, 1.0, 0.0), zeros)
        return jnp.sum(acc, axis=1, keepdims=True)

    mn, mx = fold(lambda a, s, idx: (jnp.minimum(a[0], jnp.where(s == -jnp.inf, jnp.inf, s)),
                                     jnp.maximum(a[1], s)),
                  (pinf, -pinf))
    rmin = jnp.min(mn, axis=1, keepdims=True)
    rmax = jnp.max(mx, axis=1, keepdims=True)

    def bis_cond(st):
        it, lo, hi, clo = st
        return jnp.logical_and(it < BISECT_CAP, jnp.max(jnp.abs(clo - kt)) > 0.0)

    def bis_body(st):
        it, lo, hi, clo = st
        mid = 0.5 * lo + 0.5 * hi
        c = count_ge(mid)
        active = clo != kt
        up = jnp.logical_and(active, c >= kt)
        dn = jnp.logical_and(active, c < kt)
        return (it + 1, jnp.where(up, mid, lo), jnp.where(dn, mid, hi), jnp.where(up, c, clo))

    _, lo, _, _ = lax.while_loop(
        bis_cond, bis_body, (jnp.int32(0), rmin, rmax + 1.0, (row_pos + 1).astype(F32)))

    def stats(lo_):
        lob = lanes(lo_)
        a_ = jnp.min(fold(lambda a, s, idx: jnp.minimum(a, jnp.where(s >= lob, s, jnp.inf)), pinf),
                     axis=1, keepdims=True)
        ab = lanes(a_)
        cg, ct, nx = fold(
            lambda a, s, idx: (a[0] + jnp.where(s > ab, 1.0, 0.0),
                               a[1] + jnp.where(s == ab, 1.0, 0.0),
                               jnp.minimum(a[2], jnp.where(s > ab, s, jnp.inf))),
            (zeros, zeros, pinf))
        return (a_, jnp.sum(cg, axis=1, keepdims=True), jnp.sum(ct, axis=1, keepdims=True),
                jnp.min(nx, axis=1, keepdims=True))

    def fin_cond(st):
        return st[0]

    def fin_body(st):
        _, lo_, _, _ = st
        a_, cgt_, nt_, nxt_ = stats(lo_)
        bad = cgt_ >= kt
        return (jnp.max(jnp.where(bad, 1.0, 0.0)) > 0.0, jnp.where(bad, nxt_, a_), cgt_, nt_)

    _, a, cgt, nties = lax.while_loop(fin_cond, fin_body, (jnp.bool_(True), lo, kt, kt))
    need = kt - cgt
    excess = jnp.max(jnp.where(nties > need, 1.0, 0.0)) > 0.0
    ab = lanes(a)

    def tie_search():
        ans = jnp.zeros((tq, 1), I32)
        for bit in reversed(range(idx_bits)):
            cand = ans + (1 << bit)
            cb_ = lanes(cand)
            cnt = jnp.sum(fold(lambda acc, s, idx: acc + jnp.where(
                jnp.logical_and(s == ab, idx < cb_), 1.0, 0.0), zeros), axis=1, keepdims=True)
            ans = jnp.where(cnt < need, cand, ans)
        return ans

    jstar = lax.cond(excess, tie_search, lambda: jnp.full((tq, 1), (1 << idx_bits) - 1, I32))
    jb = lanes(jstar)

    def mask_chunk(c, _):
        for j in range(sub):
            s = sc_ref[c, :, j * LANES:(j + 1) * LANES]
            idx = c * ck + j * LANES + lane
            sel = jnp.logical_or(s > ab, jnp.logical_and(s == ab, idx <= jb))
            sc_ref[c, :, j * LANES:(j + 1) * LANES] = jnp.where(sel, 0.0, NEG)
        return 0

    lax.fori_loop(0, nck, mask_chunk, 0)

    m_ref[...] = jnp.full(m_ref.shape, NEG, F32)
    l_ref[...] = jnp.zeros(l_ref.shape, F32)
    acc_ref[...] = jnp.zeros(acc_ref.shape, F32)

    q_m = []
    for p in range(N_PAIRS):
        q2 = q_ref[0, :, p * LANES:(p + 1) * LANES].astype(F32) * QK_SCALE
        q_m.append(jnp.where(lo_half, q2, 0.0).astype(BF16))
        q_m.append(jnp.where(lo_half, 0.0, q2).astype(BF16))

    def att_chunk(c, _):
        c0 = pl.multiple_of(c * ck, ck)
        madd = sc_ref[c]
        o_idx = jnp.maximum((c0 - q0) // LANES + (n_off - 1), 0)
        for p in range(N_PAIRS):
            k2 = k_ref[0, pl.ds(c0, ck), p * LANES:(p + 1) * LANES]
            v2 = v_ref[0, pl.ds(c0, ck), p * LANES:(p + 1) * LANES]
            for hh in range(2):
                h = 2 * p + hh
                s = lax.dot_general(q_m[h], k2, (((1,), (1,)), ((), ())), preferred_element_type=F32)
                s = s + bias_ref[h, o_idx] + madd
                m_old = m_ref[h]
                m_new = jnp.maximum(m_old, jnp.max(s, axis=1, keepdims=True))
                alpha = jnp.exp(m_old - m_new)
                pexp = jnp.exp(s - m_new)
                l_ref[h] = alpha * l_ref[h] + jnp.sum(pexp, axis=1, keepdims=True)
                acc_ref[h] = alpha * acc_ref[h] + jnp.dot(pexp.astype(BF16), v2,
                                                          preferred_element_type=F32)
                m_ref[h] = m_new
        return 0

    lax.fori_loop(0, nck, att_chunk, 0)

    for p in range(N_PAIRS):
        oa = acc_ref[2 * p] / l_ref[2 * p]
        ob = acc_ref[2 * p + 1] / l_ref[2 * p + 1]
        o_ref[0, :, p * LANES:(p + 1) * LANES] = jnp.where(lo_half, oa, ob).astype(o_ref.dtype)


def _dsa(proj3, rel_bias):
    B, S, _ = proj3.shape
    tq, ck = DSA_TQ, DSA_CK
    n_off = ck // LANES + 2
    n_sel = min(TOPK_MAX, S // 4)
    idx_bits = max(1, int(math.ceil(math.log2(S))))
    bucket = jnp.asarray(_dsa_bucket_tiles(tq, ck))
    wcols = ATT_W // LANES
    kern = functools.partial(_dsa_kernel, tq=tq, ck=ck, n_sel=n_sel, n_off=n_off, idx_bits=idx_bits)
    return pl.pallas_call(
        kern,
        grid=(B, S // tq),
        in_specs=[
            pl.BlockSpec((n_off, tq, ck), lambda b, i: (0, 0, 0)),
            pl.BlockSpec(memory_space=pltpu.SMEM),
            pl.BlockSpec((1, tq, ATT_W), lambda b, i: (b, i, COL_QA // ATT_W)),
            pl.BlockSpec((1, S, ATT_W), lambda b, i: (b, 0, COL_KA // ATT_W)),
            pl.BlockSpec((1, S, ATT_W), lambda b, i: (b, 0, COL_VA // ATT_W)),
            pl.BlockSpec((1, tq, ATT_W), lambda b, i: (b, i, COL_QI // ATT_W)),
            pl.BlockSpec((1, S, LANES), lambda b, i: (b, 0, COL_KK // LANES)),
            pl.BlockSpec((1, tq, LANES), lambda b, i: (b, i, COL_WI // LANES)),
        ],
        out_specs=pl.BlockSpec((1, tq, ATT_W), lambda b, i: (b, i, 0)),
        out_shape=jax.ShapeDtypeStruct((B, S, ATT_W), BF16),
        scratch_shapes=[
            pltpu.VMEM((S // ck, tq, ck), F32),
            pltpu.VMEM((A_HEADS, n_off, tq, ck), F32),
            pltpu.VMEM((A_HEADS, tq, 1), F32),
            pltpu.VMEM((A_HEADS, tq, 1), F32),
            pltpu.VMEM((A_HEADS, tq, LANES), F32),
        ],
        compiler_params=_params(("arbitrary", "arbitrary")),
    )(bucket, rel_bias, proj3, proj3, proj3, proj3, proj3, proj3)


def _sb_kernel(q_ref, k_ref, v_ref, o_ref, *, t):
    i = pl.program_id(2)
    lane = lax.broadcasted_iota(I32, (t, LANES), 1)
    lo_half = lane < HALF
    q2 = q_ref[0].astype(F32) * QK_SCALE
    q_m = (jnp.where(lo_half, q2, 0.0).astype(BF16), jnp.where(lo_half, 0.0, q2).astype(BF16))
    r = lax.broadcasted_iota(I32, (t, t), 0)
    cidx = lax.broadcasted_iota(I32, (t, t), 1)
    later = (r > cidx).astype(BF16)
    diff = cidx - r

    def chunk(jj, carry):
        j = i - jj
        c0 = pl.multiple_of(j * t, t)
        k2 = k_ref[0, pl.ds(c0, t), :]
        v2 = v_ref[0, pl.ds(c0, t), :]
        keep = diff < jnp.where(jj == 0, 0, t)
        out = []
        for hh in range(2):
            car, acc = carry[hh]
            z = lax.dot_general(q_m[hh], k2, (((1,), (1,)), ((), ())), preferred_element_type=F32)
            sp = jnp.maximum(z, 0.0) + jnp.log1p(jnp.exp(-jnp.abs(z)))
            lm = jnp.where(keep, -sp, 0.0)
            hi = lm.astype(BF16)
            lo = (lm - hi.astype(F32)).astype(BF16)
            after = (jnp.dot(hi, later, preferred_element_type=F32)
                     + jnp.dot(lo, later, preferred_element_type=F32))
            w = jnp.where(keep, jnp.exp((z - sp) + after + car), 0.0)
            acc = acc + jnp.dot(w.astype(BF16), v2, preferred_element_type=F32)
            car = car + jnp.sum(lm, axis=1, keepdims=True)
            out.append((car, acc))
        return tuple(out)

    z1 = jnp.zeros((t, 1), F32)
    za = jnp.zeros((t, LANES), F32)
    (_, acc_a), (_, acc_b) = lax.fori_loop(0, i + 1, chunk, ((z1, za), (z1, za)))
    o_ref[0] = jnp.where(lo_half, acc_a, acc_b).astype(o_ref.dtype)


def _stick_breaking(proj3):
    B, S, _ = proj3.shape
    t = SB_T
    qb, kb, vb = COL_QB // LANES, COL_KB // LANES, COL_VB // LANES
    return pl.pallas_call(
        functools.partial(_sb_kernel, t=t),
        grid=(B, N_PAIRS, S // t),
        in_specs=[
            pl.BlockSpec((1, t, LANES), lambda b, p, i: (b, i, qb + p)),
            pl.BlockSpec((1, S, LANES), lambda b, p, i: (b, 0, kb + p)),
            pl.BlockSpec((1, S, LANES), lambda b, p, i: (b, 0, vb + p)),
        ],
        out_specs=pl.BlockSpec((1, t, LANES), lambda b, p, i: (b, i, p)),
        out_shape=jax.ShapeDtypeStruct((B, S, ATT_W), BF16),
        compiler_params=_params(("arbitrary", "arbitrary", "arbitrary")),
    )(proj3, proj3, proj3)


def _layer_norm(r, g, b):
    mu = jnp.mean(r, axis=-1, keepdims=True)
    d = r - mu
    var = jnp.mean(d * d, axis=-1, keepdims=True)
    return d * lax.rsqrt(var + LN_EPS) * g + b


def _split_bf16(v):
    hi = v.astype(BF16)
    return hi, (v - hi.astype(F32)).astype(BF16)


def _merge_kernel(x_ref, ya_ref, yb_ref, gate_ref, wa_ref, wb_ref, wo_ref, g_ref, b_ref,
                  wr_ref, br_ref, h_ref, e_ref, p_ref, *, alpha, d):
    pa = jnp.dot(ya_ref[...], wa_ref[...], preferred_element_type=F32)
    pb = jnp.dot(yb_ref[...], wb_ref[...], preferred_element_type=F32)
    merged = jax.nn.sigmoid(gate_ref[:, :d]) * pa + jax.nn.sigmoid(gate_ref[:, d:]) * pb
    m = jnp.dot(merged.astype(BF16), wo_ref[...], preferred_element_type=F32)
    h = _layer_norm(alpha * x_ref[...] + m, g_ref[...], b_ref[...])
    h_ref[...] = h

    nt = (((1,), (1,)), ((), ()))
    h_hi, h_lo = _split_bf16(h)
    w_hi, w_lo = _split_bf16(wr_ref[...])
    logit = (lax.dot_general(w_hi, h_hi, nt, preferred_element_type=F32)
             + lax.dot_general(w_hi, h_lo, nt, preferred_element_type=F32)
             + lax.dot_general(w_lo, h_hi, nt, preferred_element_type=F32)) + br_ref[...]
    eid = lax.broadcasted_iota(I32, logit.shape, 0)
    vals, ids = [], []
    for _ in range(TOP_K):
        mx = jnp.max(logit, axis=0, keepdims=True)
        am = jnp.min(jnp.where(logit == mx, eid, N_EXPERTS), axis=0, keepdims=True)
        vals.append(mx)
        ids.append(am)
        logit = jnp.where(eid == am, -jnp.inf, logit)
    ex = [jnp.exp(v - vals[0]) for v in vals]
    den = ex[0] + ex[1] + ex[2] + ex[3]
    for k in range(TOP_K):
        e_ref[k:k + 1, :] = ids[k]
        p_ref[k:k + 1, :] = ex[k] / den


def _merge(x2, ya, yb, gates, wa, wb, wo, g, b, wr_t, br, alpha):
    T, D = x2.shape
    tm = MERGE_TM
    row = lambda i: (i, 0)
    fixed = lambda i: (0, 0)
    return pl.pallas_call(
        functools.partial(_merge_kernel, alpha=alpha, d=D),
        grid=(T // tm,),
        in_specs=[
            pl.BlockSpec((tm, D), row),
            pl.BlockSpec((tm, ATT_W), row),
            pl.BlockSpec((tm, ATT_W), row),
            pl.BlockSpec((tm, 2 * D), row),
            pl.BlockSpec((ATT_W, D), fixed),
            pl.BlockSpec((ATT_W, D), fixed),
            pl.BlockSpec((D, D), fixed),
            pl.BlockSpec((1, D), fixed),
            pl.BlockSpec((1, D), fixed),
            pl.BlockSpec((N_EXPERTS, D), fixed),
            pl.BlockSpec((N_EXPERTS, 1), fixed),
        ],
        out_specs=[
            pl.BlockSpec((tm, D), row),
            pl.BlockSpec((TOP_K, tm), lambda i: (0, i)),
            pl.BlockSpec((TOP_K, tm), lambda i: (0, i)),
        ],
        out_shape=[
            jax.ShapeDtypeStruct((T, D), F32),
            jax.ShapeDtypeStruct((TOP_K, T), I32),
            jax.ShapeDtypeStruct((TOP_K, T), F32),
        ],
        compiler_params=_params(("arbitrary",)),
    )(x2, ya, yb, gates, wa, wb, wo, g, b, wr_t, br)


def _row_gather_start(src_hbm, idx_of, dst_of, sem, n):
    def body(r, _):
        pltpu.make_async_copy(src_hbm.at[pl.ds(idx_of(r), 1), :], dst_of(r), sem).start()
        return 0
    lax.fori_loop(0, n, body, 0, unroll=8)


def _moe_kernel(blk_e_ref, nused_ref, tok_cur_ref, tok_nxt_ref, g_ref, h_hbm, wgu_ref, bgu_ref,
                wdn_ref, bdn_ref, o_ref, xbuf, wgu_s, wdn_s, sem, *, blk, f):
    i = pl.program_id(0)
    nused = nused_ref[0]
    slot = lax.rem(i, 2)

    def issue(tok_ref, s):
        _row_gather_start(h_hbm, lambda r: tok_ref[0, 0, r],
                          lambda r: xbuf.at[s, pl.ds(r, 1), :], sem.at[s], blk)

    @pl.when(i == 0)
    def _first():
        issue(tok_cur_ref, 0)

    @pl.when(i + 1 < nused)
    def _prefetch():
        issue(tok_nxt_ref, 1 - slot)

    @pl.when(i < nused)
    def _compute():
        pltpu.make_async_copy(h_hbm.at[pl.ds(0, blk), :], xbuf.at[slot], sem.at[slot]).wait()
        changed = jnp.logical_or(i == 0, blk_e_ref[i] != blk_e_ref[jnp.maximum(i - 1, 0)])

        @pl.when(changed)
        def _cast_weights():
            wgu_s[...] = wgu_ref[0].astype(BF16)
            wdn_s[...] = wdn_ref[0].astype(BF16)

        x = xbuf[slot].astype(BF16)
        hgu = jnp.dot(x, wgu_s[...], preferred_element_type=F32) + bgu_ref[0]
        a = jnp.minimum(hgu[:, :f], SWIGLU_LIMIT)
        u = jnp.clip(hgu[:, f:], -SWIGLU_LIMIT, SWIGLU_LIMIT)
        glu = a * jax.nn.sigmoid(a * SWIGLU_ALPHA)
        y = jnp.dot(((u + 1.0) * glu).astype(BF16), wdn_s[...], preferred_element_type=F32) + bdn_ref[0]
        o_ref[...] = y * g_ref[...]

    @pl.when(i >= nused)
    def _unused_block():
        o_ref[...] = jnp.zeros(o_ref.shape, o_ref.dtype)


def _moe_ffn(h, blk_e, nused, row_tok, row_g, w_gu, b_gu, w_dn, b_dn):
    T, D = h.shape
    E, _, F2 = w_gu.shape
    f = F2 // 2
    blk = MOE_BLK
    nb = row_tok.shape[0] // blk
    tok3 = row_tok.reshape(nb, 1, blk)
    grid_spec = pltpu.PrefetchScalarGridSpec(
        num_scalar_prefetch=2,
        grid=(nb,),
        in_specs=[
            pl.BlockSpec((1, 1, blk), lambda i, be, nu: (i, 0, 0), memory_space=pltpu.SMEM),
            pl.BlockSpec((1, 1, blk), lambda i, be, nu: (jnp.minimum(i + 1, nb - 1), 0, 0),
                         memory_space=pltpu.SMEM),
            pl.BlockSpec((blk, 1), lambda i, be, nu: (i, 0)),
            pl.BlockSpec(memory_space=pl.ANY),
            pl.BlockSpec((1, D, F2), lambda i, be, nu: (be[i], 0, 0)),
            pl.BlockSpec((1, 1, F2), lambda i, be, nu: (be[i], 0, 0)),
            pl.BlockSpec((1, f, D), lambda i, be, nu: (be[i], 0, 0)),
            pl.BlockSpec((1, 1, D), lambda i, be, nu: (be[i], 0, 0)),
        ],
        out_specs=pl.BlockSpec((blk, D), lambda i, be, nu: (i, 0)),
        scratch_shapes=[
            pltpu.VMEM((2, blk, D), F32),
            pltpu.VMEM((D, F2), BF16),
            pltpu.VMEM((f, D), BF16),
            pltpu.SemaphoreType.DMA((2,)),
        ],
    )
    return pl.pallas_call(
        functools.partial(_moe_kernel, blk=blk, f=f),
        grid_spec=grid_spec,
        out_shape=jax.ShapeDtypeStruct((nb * blk, D), F32),
        compiler_params=_params(("arbitrary",)),
    )(blk_e, nused, tok3, tok3, row_g.reshape(-1, 1), h, w_gu, b_gu.reshape(E, 1, F2),
      w_dn, b_dn.reshape(E, 1, D))


def _comb_kernel(pos_cur_ref, pos_nxt_ref, h_ref, y_hbm, g_ref, b_ref, o_ref, ybuf, sem, *, tm, alpha):
    i = pl.program_id(0)
    n = pl.num_programs(0)
    slot = lax.rem(i, 2)

    def issue(pos_ref, s):
        for k in range(TOP_K):
            _row_gather_start(y_hbm, lambda r: pos_ref[0, k, r],
                              lambda r: ybuf.at[s, k, pl.ds(r, 1), :], sem.at[s], tm)

    @pl.when(i == 0)
    def _first():
        issue(pos_cur_ref, 0)

    @pl.when(i + 1 < n)
    def _prefetch():
        issue(pos_nxt_ref, 1 - slot)

    for k in range(TOP_K):
        pltpu.make_async_copy(y_hbm.at[pl.ds(0, tm), :], ybuf.at[slot, k], sem.at[slot]).wait()
    fsum = (ybuf[slot, 0] + ybuf[slot, 1]) + (ybuf[slot, 2] + ybuf[slot, 3])
    o_ref[...] = _layer_norm(alpha * h_ref[...] + fsum, g_ref[...], b_ref[...])


def _combine(h, ys, pos, g, b, alpha):
    T, D = h.shape
    tm = COMB_TM
    nt = T // tm
    pos3 = pos.reshape(TOP_K, nt, tm).transpose(1, 0, 2)
    return pl.pallas_call(
        functools.partial(_comb_kernel, tm=tm, alpha=alpha),
        grid=(nt,),
        in_specs=[
            pl.BlockSpec((1, TOP_K, tm), lambda i: (i, 0, 0), memory_space=pltpu.SMEM),
            pl.BlockSpec((1, TOP_K, tm), lambda i: (jnp.minimum(i + 1, nt - 1), 0, 0),
                         memory_space=pltpu.SMEM),
            pl.BlockSpec((tm, D), lambda i: (i, 0)),
            pl.BlockSpec(memory_space=pl.ANY),
            pl.BlockSpec((1, D), lambda i: (0, 0)),
            pl.BlockSpec((1, D), lambda i: (0, 0)),
        ],
        out_specs=pl.BlockSpec((tm, D), lambda i: (i, 0)),
        out_shape=jax.ShapeDtypeStruct((T, D), F32),
        scratch_shapes=[pltpu.VMEM((2, TOP_K, tm, D), F32), pltpu.SemaphoreType.DMA((2,))],
        compiler_params=_params(("arbitrary",)),
    )(pos3, pos3, h, ys, g, b)


def _route(top_e, top_p, blk):
    K, T = top_e.shape
    N = K * T
    flat_e = top_e.reshape(N)
    flat_p = top_p.reshape(N)
    tok = jnp.arange(N, dtype=I32) % T
    order = jnp.argsort(flat_e)
    sorted_e = flat_e[order]
    counts = jnp.sum(flat_e[:, None] == jnp.arange(N_EXPERTS, dtype=I32)[None, :], axis=0, dtype=I32)
    padded = (counts + blk - 1) // blk * blk
    pends = jnp.cumsum(padded)
    poffs = pends - padded
    offs = jnp.cumsum(counts) - counts
    dest = poffs[sorted_e] + (jnp.arange(N, dtype=I32) - offs[sorted_e])
    P = N + N_EXPERTS * blk
    nb = P // blk
    row_tok = jnp.zeros((P,), I32).at[dest].set(tok[order])
    row_g = jnp.zeros((P,), F32).at[dest].set(flat_p[order])
    pos = jnp.zeros((N,), I32).at[order].set(dest)
    blk_start = jnp.arange(nb, dtype=I32) * blk
    blk_e = jnp.minimum(jnp.searchsorted(pends, blk_start, side='right'), N_EXPERTS - 1).astype(I32)
    nused = (pends[-1:] // blk).astype(I32)
    return blk_e, nused, row_tok, row_g, pos.reshape(K, T)


def _attention_weight(w_in_l):
    sizes = (ATT_W, ATT_W, ATT_W, IDX_HEADS * IDX_DIM, IDX_DIM, IDX_HEADS, ATT_W, ATT_W, ATT_W)
    offs = np.concatenate([[0], np.cumsum(sizes)])
    qa, ka, va, qi, ki, wi, qb, kb, vb = (w_in_l[:, offs[n]:offs[n + 1]] for n in range(9))
    d = w_in_l.shape[0]
    pad_wi = jnp.zeros((d, LANES - IDX_HEADS), w_in_l.dtype)
    used = 7 * ATT_W + 2 * LANES
    pad = jnp.zeros((d, ATT_COLS - used), w_in_l.dtype)
    w = jnp.concatenate([qa, ka, va, qi, qb, kb, vb, ki, ki, wi, pad_wi, pad], axis=1)
    return w.astype(BF16), int(offs[9])


def kernel(x, w_in, w_branch_a, w_branch_b, w_out, rel_bias, ln1_g, ln1_b, w_router, b_router,
           w_gate_up, b_gate_up, w_down, b_down, ln2_g, ln2_b):
    B, S, D = x.shape
    depth = w_in.shape[0]
    alpha = (2 * depth) ** 0.25
    T = B * S
    h = x.reshape(T, D)
    for l in range(depth):
        w_att, gate_off = _attention_weight(w_in[l])
        w_gate = w_in[l][:, gate_off:].astype(BF16)
        proj = _matmul(h, w_att, BF16, 1024, 768).reshape(B, S, ATT_COLS)
        gates = _matmul(h, w_gate, F32, 1024, 1024)
        ya = _dsa(proj, rel_bias).reshape(T, ATT_W)
        yb = _stick_breaking(proj).reshape(T, ATT_W)
        h1, top_e, top_p = _merge(
            h, ya, yb, gates, w_branch_a[l].astype(BF16), w_branch_b[l].astype(BF16),
            w_out[l].astype(BF16), ln1_g[l].reshape(1, D), ln1_b[l].reshape(1, D),
            w_router[l].T, b_router[l].reshape(N_EXPERTS, 1), alpha)
        blk_e, nused, row_tok, row_g, pos = _route(top_e, top_p, MOE_BLK)
        ys = _moe_ffn(h1, blk_e, nused, row_tok, row_g, w_gate_up[l], b_gate_up[l], w_down[l], b_down[l])
        h = _combine(h1, ys, pos, ln2_g[l].reshape(1, D), ln2_b[l].reshape(1, D), alpha)
    return h.reshape(B, S, D)
```

```python
import functools
import math

import numpy as np
import jax
import jax.numpy as jnp
from jax import lax
from jax.experimental import pallas as pl
from jax.experimental.pallas import tpu as pltpu

F32 = jnp.float32
BF16 = jnp.bfloat16
I32 = jnp.int32

A_HEADS = 8
HEAD_DIM = 64
ATT_W = A_HEADS * HEAD_DIM
IDX_HEADS = 8
IDX_DIM = 64
IDX_SCALE = (IDX_HEADS * IDX_DIM) ** -0.5
TOPK_MAX = 256
N_BUCKETS = 32
MAX_DISTANCE = 128
N_EXPERTS = 32
TOP_K = 4
SWIGLU_LIMIT = 7.0
SWIGLU_ALPHA = 1.702
LN_EPS = 1e-5
QK_SCALE = HEAD_DIM ** -0.5

LANES = 128
SUBLANES = 8
HALF = LANES // 2
N_PAIRS = A_HEADS // 2
VMEM_LIMIT = 56 * 1024 * 1024

DSA_BLK = 256
SB_T = 256
MERGE_TM = 512
MOE_BLK = 256
COMB_TM = 256
BISECT_CAP = 48
NEG = -1e30

COL_QA, COL_KA, COL_QI, COL_QB, COL_KB, COL_VB = (g * ATT_W for g in range(6))
COL_KK = 6 * ATT_W
COL_WI = COL_KK + LANES
ATT_COLS = COL_WI + LANES
PROJ_TN = ATT_COLS // 2

NT_DIMS = (((1,), (1,)), ((), ()))


def _params(sem, vmem=VMEM_LIMIT):
    return pltpu.CompilerParams(dimension_semantics=sem, vmem_limit_bytes=vmem)


def _mm_kernel(x_ref, w_ref, o_ref, xb_ref):
    @pl.when(pl.program_id(1) == 0)
    def _cast():
        xb_ref[...] = x_ref[...].astype(BF16)

    o_ref[...] = jnp.dot(xb_ref[...], w_ref[...], preferred_element_type=F32).astype(o_ref.dtype)


def _matmul(x, w, out_dtype, tm, tn):
    M, K = x.shape
    N = w.shape[1]
    return pl.pallas_call(
        _mm_kernel,
        grid=(M // tm, N // tn),
        in_specs=[pl.BlockSpec((tm, K), lambda i, j: (i, 0)),
                  pl.BlockSpec((K, tn), lambda i, j: (0, j))],
        out_specs=pl.BlockSpec((tm, tn), lambda i, j: (i, j)),
        out_shape=jax.ShapeDtypeStruct((M, N), out_dtype),
        scratch_shapes=[pltpu.VMEM((tm, K), BF16)],
        compiler_params=_params(("arbitrary", "arbitrary")),
    )(x, w)


def _mm_t_kernel(x_ref, w_ref, o_ref):
    o_ref[0] = lax.dot_general(w_ref[...], x_ref[...].astype(BF16), NT_DIMS,
                               preferred_element_type=F32).astype(o_ref.dtype)


def _matmul_t(x, w_t, tm):
    M, K = x.shape
    N = w_t.shape[0]
    return pl.pallas_call(
        _mm_t_kernel,
        grid=(M // tm,),
        in_specs=[pl.BlockSpec((tm, K), lambda i: (i, 0)),
                  pl.BlockSpec((N, K), lambda i: (0, 0))],
        out_specs=pl.BlockSpec((1, N, tm), lambda i: (i, 0, 0)),
        out_shape=jax.ShapeDtypeStruct((M // tm, N, tm), BF16),
        compiler_params=_params(("arbitrary",)),
    )(x, w_t)


def _t5_bucket_np(n):
    n = np.maximum(n, 0)
    max_exact = N_BUCKETS // 2
    nf = np.maximum(n, 1).astype(np.float32)
    large = max_exact + (np.log(nf / max_exact) / math.log(MAX_DISTANCE / max_exact)
                         * (N_BUCKETS - max_exact)).astype(np.int32)
    large = np.minimum(large, N_BUCKETS - 1)
    return np.where(n < max_exact, n, large).astype(np.int32)


def _dsa_n_off(blk):
    return 2 + -(-MAX_DISTANCE // blk)


def _dsa_bucket_tiles(blk):
    n_off = _dsa_n_off(blk)
    j = np.arange(blk)[None, :, None]
    i = np.arange(blk)[None, None, :]
    o = np.arange(n_off)[:, None, None]
    return _t5_bucket_np(i - j + blk * (n_off - 1 - o))


def _dsa_kernel(bucket_ref, relb_ref, q_ref, k_ref, vt_ref, qi_ref, kk_ref, wi_ref, o_ref,
                sc_ref, bias_ref, m_ref, l_ref, acc_ref, *, blk, n_sel, n_off, idx_bits):
    b = pl.program_id(0)
    i = pl.program_id(1)
    q0 = i * blk
    nck = i + 1
    groups = blk // SUBLANES

    @pl.when(jnp.logical_and(b == 0, i == 0))
    def _build_bias():
        def head_body(h, _):
            for o in range(n_off):
                for rb in range(blk // LANES):
                    for cb in range(blk // LANES):
                        rs = slice(rb * LANES, (rb + 1) * LANES)
                        cs = slice(cb * LANES, (cb + 1) * LANES)
                        bk = bucket_ref[o, rs, cs]

                        def bucket_body(n, acc):
                            return jnp.where(bk == n, relb_ref[n, h], acc)

                        bias_ref[h, o, rs, cs] = lax.fori_loop(
                            0, N_BUCKETS, bucket_body, jnp.zeros((LANES, LANES), F32))
            return 0

        lax.fori_loop(0, A_HEADS, head_body, 0)

    lane = lax.broadcasted_iota(I32, (blk, LANES), 1)
    lo_half = lane < HALF
    krow = lax.broadcasted_iota(I32, (blk, blk), 0)
    qpos = q0 + lax.broadcasted_iota(I32, (1, blk), 1)

    def pair_split(ref, scale):
        out = []
        for p in range(N_PAIRS):
            v = ref[0, :, p * LANES:(p + 1) * LANES].astype(F32)
            if scale != 1.0:
                v = v * scale
            out.append(jnp.where(lo_half, v, 0.0).astype(BF16))
            out.append(jnp.where(lo_half, 0.0, v).astype(BF16))
        return out

    wi_t = wi_ref[0].astype(F32).T
    wrow = [wi_t[h:h + 1, :] * IDX_SCALE for h in range(IDX_HEADS)]
    qi_m = pair_split(qi_ref, 1.0)

    def score_chunk(c, _):
        c0 = pl.multiple_of(c * blk, blk)
        kk = kk_ref[0, pl.ds(c0, blk), :]
        acc = jnp.zeros((blk, blk), F32)
        for h in range(IDX_HEADS):
            s = lax.dot_general(kk, qi_m[h], NT_DIMS, preferred_element_type=F32)
            acc = acc + wrow[h] * jnp.maximum(s, 0.0)
        sc_ref[c] = jnp.where(c0 + krow <= qpos, acc, -jnp.inf)
        return 0

    lax.fori_loop(0, nck, score_chunk, 0)

    kt = jnp.minimum(qpos + 1, n_sel).astype(F32)

    def fold(fn, init):
        def body(c, acc):
            return fn(acc, sc_ref[c], c * blk + krow)
        return lax.fori_loop(0, nck, body, init)

    def part(x, op):
        return op(x.reshape(groups, SUBLANES, blk), axis=0)

    def fin(x, op):
        return op(x, axis=0, keepdims=True)

    zeros8 = jnp.zeros((SUBLANES, blk), F32)
    pinf8 = jnp.full((SUBLANES, blk), jnp.inf, F32)

    def count_ge(th):
        return fin(fold(lambda a, s, idx: a + part(jnp.where(s >= th, 1.0, 0.0), jnp.sum), zeros8),
                   jnp.sum)

    mn, mx = fold(lambda a, s, idx: (
        jnp.minimum(a[0], part(jnp.where(s == -jnp.inf, jnp.inf, s), jnp.min)),
        jnp.maximum(a[1], part(s, jnp.max))), (pinf8, -pinf8))
    rmin = fin(mn, jnp.min)
    rmax = fin(mx, jnp.max)

    def bis_cond(st):
        it, lo, hi, clo = st
        return jnp.logical_and(it < BISECT_CAP, jnp.max(jnp.abs(clo - kt)) > 0.0)

    def bis_body(st):
        it, lo, hi, clo = st
        mid = 0.5 * lo + 0.5 * hi
        c = count_ge(mid)
        active = clo != kt
        up = jnp.logical_and(active, c >= kt)
        dn = jnp.logical_and(active, c < kt)
        return (it + 1, jnp.where(up, mid, lo), jnp.where(dn, mid, hi), jnp.where(up, c, clo))

    _, lo, _, _ = lax.while_loop(
        bis_cond, bis_body, (jnp.int32(0), rmin, rmax + 1.0, (qpos + 1).astype(F32)))

    def stats(lo_):
        a_ = fin(fold(lambda a, s, idx: jnp.minimum(a, part(jnp.where(s >= lo_, s, jnp.inf), jnp.min)),
                      pinf8), jnp.min)
        cg, ct, nx = fold(
            lambda a, s, idx: (a[0] + part(jnp.where(s > a_, 1.0, 0.0), jnp.sum),
                               a[1] + part(jnp.where(s == a_, 1.0, 0.0), jnp.sum),
                               jnp.minimum(a[2], part(jnp.where(s > a_, s, jnp.inf), jnp.min))),
            (zeros8, zeros8, pinf8))
        return a_, fin(cg, jnp.sum), fin(ct, jnp.sum), fin(nx, jnp.min)

    def fin_cond(st):
        return st[0]

    def fin_body(st):
        _, lo_, _, _ = st
        a_, cgt_, nt_, nxt_ = stats(lo_)
        bad = cgt_ >= kt
        return (jnp.max(jnp.where(bad, 1.0, 0.0)) > 0.0, jnp.where(bad, nxt_, a_), cgt_, nt_)

    _, a, cgt, nties = lax.while_loop(fin_cond, fin_body, (jnp.bool_(True), lo, kt, kt))
    need = kt - cgt
    excess = jnp.max(jnp.where(nties > need, 1.0, 0.0)) > 0.0

    def tie_search():
        ans = jnp.zeros((1, blk), I32)
        for bit in reversed(range(idx_bits)):
            cand = ans + (1 << bit)
            cnt = fin(fold(lambda acc, s, idx: acc + part(jnp.where(
                jnp.logical_and(s == a, idx < cand), 1.0, 0.0), jnp.sum), zeros8), jnp.sum)
            ans = jnp.where(cnt < need, cand, ans)
        return ans

    jstar = lax.cond(excess, tie_search, lambda: jnp.full((1, blk), (1 << idx_bits) - 1, I32))

    def mask_chunk(c, _):
        s = sc_ref[c]
        idx = c * blk + krow
        sel = jnp.logical_or(s > a, jnp.logical_and(s == a, idx <= jstar))
        sc_ref[c] = jnp.where(sel, 0.0, NEG)
        return 0

    lax.fori_loop(0, nck, mask_chunk, 0)

    m_ref[...] = jnp.full(m_ref.shape, NEG, F32)
    l_ref[...] = jnp.zeros(l_ref.shape, F32)
    acc_ref[...] = jnp.zeros(acc_ref.shape, F32)
    q_m = pair_split(q_ref, QK_SCALE)

    def att_chunk(c, _):
        c0 = pl.multiple_of(c * blk, blk)
        madd = sc_ref[c]
        o_idx = jnp.maximum(c - i + (n_off - 1), 0)
        for p in range(N_PAIRS):
            k2 = k_ref[0, pl.ds(c0, blk), p * LANES:(p + 1) * LANES]
            vt2 = vt_ref[0, c, p * LANES:(p + 1) * LANES, :]
            for hh in range(2):
                h = 2 * p + hh
                s = lax.dot_general(k2, q_m[h], NT_DIMS, preferred_element_type=F32)
                s = s + bias_ref[h, o_idx] + madd
                m_old = m_ref[h]
                m_new = jnp.maximum(m_old, jnp.max(s, axis=0, keepdims=True))
                alpha = jnp.exp(m_old - m_new)
                pexp = jnp.exp(s - m_new)
                l_ref[h] = alpha * l_ref[h] + jnp.sum(pexp, axis=0, keepdims=True)
                acc_ref[h] = alpha * acc_ref[h] + jnp.dot(vt2, pexp.astype(BF16),
                                                          preferred_element_type=F32)
                m_ref[h] = m_new
        return 0

    lax.fori_loop(0, nck, att_chunk, 0)

    lo_rows = lax.broadcasted_iota(I32, (LANES, blk), 0) < HALF
    for p in range(N_PAIRS):
        oa = acc_ref[2 * p] / l_ref[2 * p]
        ob = acc_ref[2 * p + 1] / l_ref[2 * p + 1]
        o_ref[0, :, p * LANES:(p + 1) * LANES] = jnp.where(lo_rows, oa, ob).T.astype(o_ref.dtype)


def _dsa(proj3, vt4, rel_bias):
    B, S, _ = proj3.shape
    blk = DSA_BLK
    n_off = _dsa_n_off(blk)
    n_sel = min(TOPK_MAX, S // 4)
    idx_bits = max(1, int(math.ceil(math.log2(S))))
    bucket = jnp.asarray(_dsa_bucket_tiles(blk))
    kern = functools.partial(_dsa_kernel, blk=blk, n_sel=n_sel, n_off=n_off, idx_bits=idx_bits)
    return pl.pallas_call(
        kern,
        grid=(B, S // blk),
        in_specs=[
            pl.BlockSpec((n_off, blk, blk), lambda b, i: (0, 0, 0)),
            pl.BlockSpec(memory_space=pltpu.SMEM),
            pl.BlockSpec((1, blk, ATT_W), lambda b, i: (b, i, COL_QA // ATT_W)),
            pl.BlockSpec((1, S, ATT_W), lambda b, i: (b, 0, COL_KA // ATT_W)),
            pl.BlockSpec((1, S // blk, ATT_W, blk), lambda b, i: (b, 0, 0, 0)),
            pl.BlockSpec((1, blk, ATT_W), lambda b, i: (b, i, COL_QI // ATT_W)),
            pl.BlockSpec((1, S, LANES), lambda b, i: (b, 0, COL_KK // LANES)),
            pl.BlockSpec((1, blk, LANES), lambda b, i: (b, i, COL_WI // LANES)),
        ],
        out_specs=pl.BlockSpec((1, blk, ATT_W), lambda b, i: (b, i, 0)),
        out_shape=jax.ShapeDtypeStruct((B, S, ATT_W), BF16),
        scratch_shapes=[
            pltpu.VMEM((S // blk, blk, blk), F32),
            pltpu.VMEM((A_HEADS, n_off, blk, blk), F32),
            pltpu.VMEM((A_HEADS, 1, blk), F32),
            pltpu.VMEM((A_HEADS, 1, blk), F32),
            pltpu.VMEM((A_HEADS, LANES, blk), F32),
        ],
        compiler_params=_params(("arbitrary", "arbitrary")),
    )(bucket, rel_bias, proj3, proj3, vt4, proj3, proj3, proj3)


def _sb_kernel(q_ref, k_ref, v_ref, o_ref, *, t):
    i = pl.program_id(2)
    lane = lax.broadcasted_iota(I32, (t, LANES), 1)
    lo_half = lane < HALF
    q2 = q_ref[0].astype(F32) * QK_SCALE
    q_m = (jnp.where(lo_half, q2, 0.0).astype(BF16), jnp.where(lo_half, 0.0, q2).astype(BF16))
    r = lax.broadcasted_iota(I32, (t, t), 0)
    cidx = lax.broadcasted_iota(I32, (t, t), 1)
    later = (r > cidx).astype(BF16)
    strict = cidx < r

    def tile(j, carry, diagonal):
        c0 = pl.multiple_of(j * t, t)
        k2 = k_ref[0, pl.ds(c0, t), :]
        v2 = v_ref[0, pl.ds(c0, t), :]
        out = []
        for hh in range(2):
            car, acc = carry[hh]
            z = lax.dot_general(q_m[hh], k2, NT_DIMS, preferred_element_type=F32)
            sp = jnp.maximum(z, 0.0) + jnp.log(1.0 + jnp.exp(-jnp.abs(z)))
            lm = -sp
            if diagonal:
                lm = jnp.where(strict, lm, 0.0)
            hi = lm.astype(BF16)
            lo = (lm - hi.astype(F32)).astype(BF16)
            after = (jnp.dot(hi, later, preferred_element_type=F32)
                     + jnp.dot(lo, later, preferred_element_type=F32))
            w = jnp.exp((z - sp) + after + car)
            if diagonal:
                w = jnp.where(strict, w, 0.0)
            acc = acc + jnp.dot(w.astype(BF16), v2, preferred_element_type=F32)
            car = car + jnp.sum(lm, axis=1, keepdims=True)
            out.append((car, acc))
        return tuple(out)

    z1 = jnp.zeros((t, 1), F32)
    za = jnp.zeros((t, LANES), F32)
    carry = tile(i, ((z1, za), (z1, za)), True)
    (_, acc_a), (_, acc_b) = lax.fori_loop(0, i, lambda jj, c: tile(i - 1 - jj, c, False), carry)
    o_ref[0] = jnp.where(lo_half, acc_a, acc_b).astype(o_ref.dtype)


def _stick_breaking(proj3):
    B, S, _ = proj3.shape
    t = SB_T
    qb, kb, vb = COL_QB // LANES, COL_KB // LANES, COL_VB // LANES
    return pl.pallas_call(
        functools.partial(_sb_kernel, t=t),
        grid=(B, N_PAIRS, S // t),
        in_specs=[
            pl.BlockSpec((1, t, LANES), lambda b, p, i: (b, i, qb + p)),
            pl.BlockSpec((1, S, LANES), lambda b, p, i: (b, 0, kb + p)),
            pl.BlockSpec((1, S, LANES), lambda b, p, i: (b, 0, vb + p)),
        ],
        out_specs=pl.BlockSpec((1, t, LANES), lambda b, p, i: (b, i, p)),
        out_shape=jax.ShapeDtypeStruct((B, S, ATT_W), BF16),
        compiler_params=_params(("arbitrary", "arbitrary", "arbitrary")),
    )(proj3, proj3, proj3)


def _layer_norm(r, g, b):
    mu = jnp.mean(r, axis=-1, keepdims=True)
    d = r - mu
    var = jnp.mean(d * d, axis=-1, keepdims=True)
    return d * lax.rsqrt(var + LN_EPS) * g + b


def _split_bf16(v):
    hi = v.astype(BF16)
    return hi, (v - hi.astype(F32)).astype(BF16)


def _merge_kernel(x_ref, ya_ref, yb_ref, gate_ref, wa_ref, wb_ref, wo_ref, g_ref, b_ref,
                  wr_ref, br_ref, h_ref, e_ref, p_ref, *, alpha, d):
    pa = jnp.dot(ya_ref[...], wa_ref[...], preferred_element_type=F32)
    pb = jnp.dot(yb_ref[...], wb_ref[...], preferred_element_type=F32)
    merged = jax.nn.sigmoid(gate_ref[:, :d]) * pa + jax.nn.sigmoid(gate_ref[:, d:]) * pb
    m = jnp.dot(merged.astype(BF16), wo_ref[...], preferred_element_type=F32)
    h = _layer_norm(alpha * x_ref[...] + m, g_ref[...], b_ref[...])
    h_ref[...] = h

    h_hi, h_lo = _split_bf16(h)
    w_hi, w_lo = _split_bf16(wr_ref[...])
    logit = (lax.dot_general(w_hi, h_hi, NT_DIMS, preferred_element_type=F32)
             + lax.dot_general(w_hi, h_lo, NT_DIMS, preferred_element_type=F32)
             + lax.dot_general(w_lo, h_hi, NT_DIMS, preferred_element_type=F32)) + br_ref[...]
    eid = lax.broadcasted_iota(I32, logit.shape, 0)
    vals, ids = [], []
    for _ in range(TOP_K):
        mx = jnp.max(logit, axis=0, keepdims=True)
        am = jnp.min(jnp.where(logit == mx, eid, N_EXPERTS), axis=0, keepdims=True)
        vals.append(mx)
        ids.append(am)
        logit = jnp.where(eid == am, -jnp.inf, logit)
    ex = [jnp.exp(v - vals[0]) for v in vals]
    den = ex[0] + ex[1] + ex[2] + ex[3]
    for k in range(TOP_K):
        e_ref[k:k + 1, :] = ids[k]
        p_ref[k:k + 1, :] = ex[k] / den


def _merge(x2, ya, yb, gates, wa, wb, wo, g, b, wr_t, br, alpha):
    T, D = x2.shape
    tm = MERGE_TM
    row = lambda i: (i, 0)
    fixed = lambda i: (0, 0)
    return pl.pallas_call(
        functools.partial(_merge_kernel, alpha=alpha, d=D),
        grid=(T // tm,),
        in_specs=[
            pl.BlockSpec((tm, D), row),
            pl.BlockSpec((tm, ATT_W), row),
            pl.BlockSpec((tm, ATT_W), row),
            pl.BlockSpec((tm, 2 * D), row),
            pl.BlockSpec((ATT_W, D), fixed),
            pl.BlockSpec((ATT_W, D), fixed),
            pl.BlockSpec((D, D), fixed),
            pl.BlockSpec((1, D), fixed),
            pl.BlockSpec((1, D), fixed),
            pl.BlockSpec((N_EXPERTS, D), fixed),
            pl.BlockSpec((N_EXPERTS, 1), fixed),
        ],
        out_specs=[
            pl.BlockSpec((tm, D), row),
            pl.BlockSpec((TOP_K, tm), lambda i: (0, i)),
            pl.BlockSpec((TOP_K, tm), lambda i: (0, i)),
        ],
        out_shape=[
            jax.ShapeDtypeStruct((T, D), F32),
            jax.ShapeDtypeStruct((TOP_K, T), I32),
            jax.ShapeDtypeStruct((TOP_K, T), F32),
        ],
        compiler_params=_params(("arbitrary",)),
    )(x2, ya, yb, gates, wa, wb, wo, g, b, wr_t, br)


def _row_gather_start(src_hbm, idx_of, dst_of, sem, n):
    def body(r, _):
        pltpu.make_async_copy(src_hbm.at[pl.ds(idx_of(r), 1), :], dst_of(r), sem).start()
        return 0
    lax.fori_loop(0, n, body, 0, unroll=8)


def _moe_kernel(blk_e_ref, nused_ref, tok_cur_ref, tok_nxt_ref, g_ref, h_hbm, wgu_ref, bgu_ref,
                wdn_ref, bdn_ref, o_ref, xbuf, wgu_s, wdn_s, sem, *, blk, f):
    i = pl.program_id(0)
    nused = nused_ref[0]
    slot = lax.rem(i, 2)

    def issue(tok_ref, s):
        _row_gather_start(h_hbm, lambda r: tok_ref[0, 0, r],
                          lambda r: xbuf.at[s, pl.ds(r, 1), :], sem.at[s], blk)

    @pl.when(i == 0)
    def _first():
        issue(tok_cur_ref, 0)

    @pl.when(i + 1 < nused)
    def _prefetch():
        issue(tok_nxt_ref, 1 - slot)

    @pl.when(i < nused)
    def _compute():
        pltpu.make_async_copy(h_hbm.at[pl.ds(0, blk), :], xbuf.at[slot], sem.at[slot]).wait()
        changed = jnp.logical_or(i == 0, blk_e_ref[i] != blk_e_ref[jnp.maximum(i - 1, 0)])

        @pl.when(changed)
        def _cast_weights():
            wgu_s[...] = wgu_ref[0].astype(BF16)
            wdn_s[...] = wdn_ref[0].astype(BF16)

        x = xbuf[slot].astype(BF16)
        hgu = jnp.dot(x, wgu_s[...], preferred_element_type=F32) + bgu_ref[0]
        a = jnp.minimum(hgu[:, :f], SWIGLU_LIMIT)
        u = jnp.clip(hgu[:, f:], -SWIGLU_LIMIT, SWIGLU_LIMIT)
        glu = a * jax.nn.sigmoid(a * SWIGLU_ALPHA)
        y = jnp.dot(((u + 1.0) * glu).astype(BF16), wdn_s[...], preferred_element_type=F32) + bdn_ref[0]
        o_ref[...] = y * g_ref[...]

    @pl.when(i >= nused)
    def _unused_block():
        o_ref[...] = jnp.zeros(o_ref.shape, o_ref.dtype)


def _moe_ffn(h, blk_e, nused, row_tok, row_g, w_gu, b_gu, w_dn, b_dn):
    T, D = h.shape
    E, _, F2 = w_gu.shape
    f = F2 // 2
    blk = MOE_BLK
    nb = row_tok.shape[0] // blk
    tok3 = row_tok.reshape(nb, 1, blk)
    grid_spec = pltpu.PrefetchScalarGridSpec(
        num_scalar_prefetch=2,
        grid=(nb,),
        in_specs=[
            pl.BlockSpec((1, 1, blk), lambda i, be, nu: (i, 0, 0), memory_space=pltpu.SMEM),
            pl.BlockSpec((1, 1, blk), lambda i, be, nu: (jnp.minimum(i + 1, nb - 1), 0, 0),
                         memory_space=pltpu.SMEM),
            pl.BlockSpec((blk, 1), lambda i, be, nu: (i, 0)),
            pl.BlockSpec(memory_space=pl.ANY),
            pl.BlockSpec((1, D, F2), lambda i, be, nu: (be[i], 0, 0)),
            pl.BlockSpec((1, 1, F2), lambda i, be, nu: (be[i], 0, 0)),
            pl.BlockSpec((1, f, D), lambda i, be, nu: (be[i], 0, 0)),
            pl.BlockSpec((1, 1, D), lambda i, be, nu: (be[i], 0, 0)),
        ],
        out_specs=pl.BlockSpec((blk, D), lambda i, be, nu: (i, 0)),
        scratch_shapes=[
            pltpu.VMEM((2, blk, D), F32),
            pltpu.VMEM((D, F2), BF16),
            pltpu.VMEM((f, D), BF16),
            pltpu.SemaphoreType.DMA((2,)),
        ],
    )
    return pl.pallas_call(
        functools.partial(_moe_kernel, blk=blk, f=f),
        grid_spec=grid_spec,
        out_shape=jax.ShapeDtypeStruct((nb * blk, D), F32),
        compiler_params=_params(("arbitrary",)),
    )(blk_e, nused, tok3, tok3, row_g.reshape(-1, 1), h, w_gu, b_gu.reshape(E, 1, F2),
      w_dn, b_dn.reshape(E, 1, D))


def _comb_kernel(pos_cur_ref, pos_nxt_ref, h_ref, y_hbm, g_ref, b_ref, o_ref, ybuf, sem, *, tm, alpha):
    i = pl.program_id(0)
    n = pl.num_programs(0)
    slot = lax.rem(i, 2)

    def issue(pos_ref, s):
        for k in range(TOP_K):
            _row_gather_start(y_hbm, lambda r: pos_ref[0, k, r],
                              lambda r: ybuf.at[s, k, pl.ds(r, 1), :], sem.at[s], tm)

    @pl.when(i == 0)
    def _first():
        issue(pos_cur_ref, 0)

    @pl.when(i + 1 < n)
    def _prefetch():
        issue(pos_nxt_ref, 1 - slot)

    for k in range(TOP_K):
        pltpu.make_async_copy(y_hbm.at[pl.ds(0, tm), :], ybuf.at[slot, k], sem.at[slot]).wait()
    fsum = (ybuf[slot, 0] + ybuf[slot, 1]) + (ybuf[slot, 2] + ybuf[slot, 3])
    o_ref[...] = _layer_norm(alpha * h_ref[...] + fsum, g_ref[...], b_ref[...])


def _combine(h, ys, pos, g, b, alpha):
    T, D = h.shape
    tm = COMB_TM
    nt = T // tm
    pos3 = pos.reshape(TOP_K, nt, tm).transpose(1, 0, 2)
    return pl.pallas_call(
        functools.partial(_comb_kernel, tm=tm, alpha=alpha),
        grid=(nt,),
        in_specs=[
            pl.BlockSpec((1, TOP_K, tm), lambda i: (i, 0, 0), memory_space=pltpu.SMEM),
            pl.BlockSpec((1, TOP_K, tm), lambda i: (jnp.minimum(i + 1, nt - 1), 0, 0),
                         memory_space=pltpu.SMEM),
            pl.BlockSpec((tm, D), lambda i: (i, 0)),
            pl.BlockSpec(memory_space=pl.ANY),
            pl.BlockSpec((1, D), lambda i: (0, 0)),
            pl.BlockSpec((1, D), lambda i: (0, 0)),
        ],
        out_specs=pl.BlockSpec((tm, D), lambda i: (i, 0)),
        out_shape=jax.ShapeDtypeStruct((T, D), F32),
        scratch_shapes=[pltpu.VMEM((2, TOP_K, tm, D), F32), pltpu.SemaphoreType.DMA((2,))],
        compiler_params=_params(("arbitrary",)),
    )(pos3, pos3, h, ys, g, b)


def _route(top_e, top_p, blk):
    K, T = top_e.shape
    N = K * T
    flat_e = top_e.reshape(N)
    flat_p = top_p.reshape(N)
    experts = jnp.arange(N_EXPERTS, dtype=I32)
    order = jnp.argsort(flat_e, stable=True).astype(I32)
    inv = jnp.argsort(order).astype(I32)
    counts = jnp.sum(flat_e[:, None] == experts[None, :], axis=0, dtype=I32)
    padded = (counts + blk - 1) // blk * blk
    pends = jnp.cumsum(padded)
    offs = jnp.cumsum(counts) - counts
    shift = (pends - padded) - offs
    pos = inv + shift[flat_e]
    P = N + N_EXPERTS * blk
    nb = P // blk
    blk_start = jnp.arange(nb, dtype=I32) * blk
    blk_e = jnp.minimum(jnp.sum(pends[None, :] <= blk_start[:, None], axis=1, dtype=I32), N_EXPERTS - 1)
    e_row = jnp.repeat(blk_e, blk)
    j = jnp.arange(P, dtype=I32) - shift[e_row]
    valid = j < (offs + counts)[e_row]
    src = order[jnp.clip(j, 0, N - 1)]
    row_tok = jnp.where(valid, src % T, 0)
    row_g = jnp.where(valid, flat_p[src], 0.0)
    nused = (pends[-1:] // blk).astype(I32)
    return blk_e, nused, row_tok, row_g, pos.reshape(K, T)


def _projection_weights(w_in_l):
    sizes = (ATT_W, ATT_W, ATT_W, IDX_HEADS * IDX_DIM, IDX_DIM, IDX_HEADS, ATT_W, ATT_W, ATT_W)
    offs = np.concatenate([[0], np.cumsum(sizes)])
    qa, ka, va, qi, ki, wi, qb, kb, vb = (w_in_l[:, offs[n]:offs[n + 1]] for n in range(9))
    pad_wi = jnp.zeros((w_in_l.shape[0], LANES - IDX_HEADS), w_in_l.dtype)
    w_att = jnp.concatenate([qa, ka, qi, qb, kb, vb, ki, ki, wi, pad_wi], axis=1).astype(BF16)
    w_va_t = va.T.astype(BF16)
    w_gate = w_in_l[:, offs[9]:].astype(BF16)
    return w_att, w_va_t, w_gate


def kernel(x, w_in, w_branch_a, w_branch_b, w_out, rel_bias, ln1_g, ln1_b, w_router, b_router,
           w_gate_up, b_gate_up, w_down, b_down, ln2_g, ln2_b):
    B, S, D = x.shape
    depth = w_in.shape[0]
    alpha = (2 * depth) ** 0.25
    T = B * S
    h = x.reshape(T, D)
    for l in range(depth):
        w_att, w_va_t, w_gate = _projection_weights(w_in[l])
        proj = _matmul(h, w_att, BF16, min(T, 1024), PROJ_TN).reshape(B, S, ATT_COLS)
        vt = _matmul_t(h, w_va_t, DSA_BLK).reshape(B, S // DSA_BLK, ATT_W, DSA_BLK)
        gates = _matmul(h, w_gate, F32, min(T, 1024), 1024)
        ya = _dsa(proj, vt, rel_bias).reshape(T, ATT_W)
        yb = _stick_breaking(proj).reshape(T, ATT_W)
        h1, top_e, top_p = _merge(
            h, ya, yb, gates, w_branch_a[l].astype(BF16), w_branch_b[l].astype(BF16),
            w_out[l].astype(BF16), ln1_g[l].reshape(1, D), ln1_b[l].reshape(1, D),
            w_router[l].T, b_router[l].reshape(N_EXPERTS, 1), alpha)
        blk_e, nused, row_tok, row_g, pos = _route(top_e, top_p, MOE_BLK)
        ys = _moe_ffn(h1, blk_e, nused, row_tok, row_g, w_gate_up[l], b_gate_up[l], w_down[l], b_down[l])
        h = _combine(h1, ys, pos, ln2_g[l].reshape(1, D), ln2_b[l].reshape(1, D), alpha)
    return h.reshape(B, S, D)
```

```python
import functools
import math

import numpy as np
import jax
import jax.numpy as jnp
from jax import lax
from jax.experimental import pallas as pl
from jax.experimental.pallas import tpu as pltpu

F32 = jnp.float32
BF16 = jnp.bfloat16
I32 = jnp.int32

A_HEADS = 8
HEAD_DIM = 64
ATT_W = A_HEADS * HEAD_DIM
IDX_HEADS = 8
IDX_DIM = 64
IDX_SCALE = (IDX_HEADS * IDX_DIM) ** -0.5
TOPK_MAX = 256
N_BUCKETS = 32
MAX_DISTANCE = 128
N_EXPERTS = 32
TOP_K = 4
SWIGLU_LIMIT = 7.0
SWIGLU_ALPHA = 1.702
LN_EPS = 1e-5
QK_SCALE = HEAD_DIM ** -0.5

LANES = 128
SUBLANES = 8
HALF = LANES // 2
N_PAIRS = A_HEADS // 2
VMEM_LIMIT = 56 * 1024 * 1024

DSA_BLK = 256
SB_T = 256
MERGE_TM = 512
MOE_BLK = 256
COMB_TM = 256
BISECT_CAP = 48
NEG = -1e30

COL_QA, COL_KA, COL_QI, COL_QB, COL_KB, COL_VB = (g * ATT_W for g in range(6))
COL_KK = 6 * ATT_W
COL_WI = COL_KK + LANES
ATT_COLS = COL_WI + LANES
PROJ_TN = ATT_COLS // 2

NT_DIMS = (((1,), (1,)), ((), ()))


def _params(sem, vmem=VMEM_LIMIT):
    return pltpu.CompilerParams(dimension_semantics=sem, vmem_limit_bytes=vmem)


def _mm_kernel(x_ref, w_ref, o_ref, xb_ref):
    @pl.when(pl.program_id(1) == 0)
    def _cast():
        xb_ref[...] = x_ref[...].astype(BF16)

    o_ref[...] = jnp.dot(xb_ref[...], w_ref[...], preferred_element_type=F32).astype(o_ref.dtype)


def _matmul(x, w, out_dtype, tm, tn):
    M, K = x.shape
    N = w.shape[1]
    return pl.pallas_call(
        _mm_kernel,
        grid=(M // tm, N // tn),
        in_specs=[pl.BlockSpec((tm, K), lambda i, j: (i, 0)),
                  pl.BlockSpec((K, tn), lambda i, j: (0, j))],
        out_specs=pl.BlockSpec((tm, tn), lambda i, j: (i, j)),
        out_shape=jax.ShapeDtypeStruct((M, N), out_dtype),
        scratch_shapes=[pltpu.VMEM((tm, K), BF16)],
        compiler_params=_params(("arbitrary", "arbitrary")),
    )(x, w)


def _mm_t_kernel(x_ref, w_ref, o_ref):
    o_ref[0] = lax.dot_general(w_ref[...], x_ref[...].astype(BF16), NT_DIMS,
                               preferred_element_type=F32).astype(o_ref.dtype)


def _matmul_t(x, w_t, tm):
    M, K = x.shape
    N = w_t.shape[0]
    return pl.pallas_call(
        _mm_t_kernel,
        grid=(M // tm,),
        in_specs=[pl.BlockSpec((tm, K), lambda i: (i, 0)),
                  pl.BlockSpec((N, K), lambda i: (0, 0))],
        out_specs=pl.BlockSpec((1, N, tm), lambda i: (i, 0, 0)),
        out_shape=jax.ShapeDtypeStruct((M // tm, N, tm), BF16),
        compiler_params=_params(("arbitrary",)),
    )(x, w_t)


def _t5_bucket_np(n):
    n = np.maximum(n, 0)
    max_exact = N_BUCKETS // 2
    nf = np.maximum(n, 1).astype(np.float32)
    large = max_exact + (np.log(nf / max_exact) / math.log(MAX_DISTANCE / max_exact)
                         * (N_BUCKETS - max_exact)).astype(np.int32)
    large = np.minimum(large, N_BUCKETS - 1)
    return np.where(n < max_exact, n, large).astype(np.int32)


def _dsa_n_off(blk):
    return 2 + -(-MAX_DISTANCE // blk)


def _dsa_bucket_tiles(blk):
    n_off = _dsa_n_off(blk)
    j = np.arange(blk)[None, :, None]
    i = np.arange(blk)[None, None, :]
    o = np.arange(n_off)[:, None, None]
    return _t5_bucket_np(i - j + blk * (n_off - 1 - o))


def _dsa_kernel(bucket_ref, relb_ref, q_ref, k_ref, vt_ref, qi_ref, kk_ref, wi_ref, o_ref,
                sc_ref, bias_ref, m_ref, l_ref, acc_ref, st_ref, mx_ref,
                *, blk, n_chunks, n_sel, n_off, idx_bits):
    b = pl.program_id(0)
    i = pl.program_id(1)
    q0 = i * blk
    nck = i + 1
    groups = blk // SUBLANES

    @pl.when(jnp.logical_and(b == 0, i == 0))
    def _build_bias():
        def head_body(h, _):
            for o in range(n_off):
                for rb in range(blk // LANES):
                    for cb in range(blk // LANES):
                        rs = slice(rb * LANES, (rb + 1) * LANES)
                        cs = slice(cb * LANES, (cb + 1) * LANES)
                        bk = bucket_ref[o, rs, cs]

                        def bucket_body(n, acc):
                            return jnp.where(bk == n, relb_ref[n, h], acc)

                        bias_ref[h, o, rs, cs] = lax.fori_loop(
                            0, N_BUCKETS, bucket_body, jnp.zeros((LANES, LANES), F32))
            return 0

        lax.fori_loop(0, A_HEADS, head_body, 0)

    lane = lax.broadcasted_iota(I32, (blk, LANES), 1)
    lo_half = lane < HALF
    krow = lax.broadcasted_iota(I32, (blk, blk), 0)
    qpos = q0 + lax.broadcasted_iota(I32, (1, blk), 1)

    def pair_split(ref, scale):
        out = []
        for p in range(N_PAIRS):
            v = ref[0, :, p * LANES:(p + 1) * LANES].astype(F32)
            if scale != 1.0:
                v = v * scale
            out.append(jnp.where(lo_half, v, 0.0).astype(BF16))
            out.append(jnp.where(lo_half, 0.0, v).astype(BF16))
        return out

    wi_t = wi_ref[0].astype(F32).T
    wrow = [wi_t[h:h + 1, :] * IDX_SCALE for h in range(IDX_HEADS)]
    qi_m = pair_split(qi_ref, 1.0)

    def score_chunk(c, _):
        c0 = pl.multiple_of(c * blk, blk)
        kk = kk_ref[0, pl.ds(c0, blk), :]
        acc = jnp.zeros((blk, blk), F32)
        for h in range(IDX_HEADS):
            s = lax.dot_general(kk, qi_m[h], NT_DIMS, preferred_element_type=F32)
            acc = acc + wrow[h] * jnp.maximum(s, 0.0)
        sc_ref[c] = jnp.where(c0 + krow <= qpos, acc, -jnp.inf)
        return 0

    lax.fori_loop(0, nck, score_chunk, 0)

    kt = jnp.minimum(qpos + 1, n_sel).astype(F32)

    def fold(fn, init):
        def body(c, acc):
            return fn(acc, sc_ref[c], c * blk + krow)
        return lax.fori_loop(0, nck, body, init)

    def part(x, op):
        return op(x.reshape(groups, SUBLANES, blk), axis=0)

    def fin(x, op):
        return op(x, axis=0, keepdims=True)

    zeros8 = jnp.zeros((SUBLANES, blk), F32)
    pinf8 = jnp.full((SUBLANES, blk), jnp.inf, F32)

    def count_ge(th):
        return fin(fold(lambda a, s, idx: a + part(jnp.where(s >= th, 1.0, 0.0), jnp.sum), zeros8),
                   jnp.sum)

    mn, mx = fold(lambda a, s, idx: (
        jnp.minimum(a[0], part(jnp.where(s == -jnp.inf, jnp.inf, s), jnp.min)),
        jnp.maximum(a[1], part(s, jnp.max))), (pinf8, -pinf8))
    rmin = fin(mn, jnp.min)
    rmax = fin(mx, jnp.max)

    def bis_cond(st):
        it, lo, hi, clo = st
        return jnp.logical_and(it < BISECT_CAP, jnp.max(jnp.abs(clo - kt)) > 0.0)

    def bis_body(st):
        it, lo, hi, clo = st
        mid = 0.5 * lo + 0.5 * hi
        c = count_ge(mid)
        active = clo != kt
        up = jnp.logical_and(active, c >= kt)
        dn = jnp.logical_and(active, c < kt)
        return (it + 1, jnp.where(up, mid, lo), jnp.where(dn, mid, hi), jnp.where(up, c, clo))

    _, lo, _, _ = lax.while_loop(
        bis_cond, bis_body, (jnp.int32(0), rmin, rmax + 1.0, (qpos + 1).astype(F32)))

    def stats(lo_):
        a_ = fin(fold(lambda a, s, idx: jnp.minimum(a, part(jnp.where(s >= lo_, s, jnp.inf), jnp.min)),
                      pinf8), jnp.min)
        cg, ct, nx = fold(
            lambda a, s, idx: (a[0] + part(jnp.where(s > a_, 1.0, 0.0), jnp.sum),
                               a[1] + part(jnp.where(s == a_, 1.0, 0.0), jnp.sum),
                               jnp.minimum(a[2], part(jnp.where(s > a_, s, jnp.inf), jnp.min))),
            (zeros8, zeros8, pinf8))
        return a_, fin(cg, jnp.sum), fin(ct, jnp.sum), fin(nx, jnp.min)

    def fin_cond(st):
        return st[0]

    def fin_body(st):
        _, lo_, _, _ = st
        a_, cgt_, nt_, nxt_ = stats(lo_)
        bad = cgt_ >= kt
        return (jnp.max(jnp.where(bad, 1.0, 0.0)) > 0.0, jnp.where(bad, nxt_, a_), cgt_, nt_)

    _, a, cgt, nties = lax.while_loop(fin_cond, fin_body, (jnp.bool_(True), lo, kt, kt))
    need = kt - cgt
    excess = jnp.max(jnp.where(nties > need, 1.0, 0.0)) > 0.0

    def tie_search():
        ans = jnp.zeros((1, blk), I32)
        for bit in reversed(range(idx_bits)):
            cand = ans + (1 << bit)
            cnt = fin(fold(lambda acc, s, idx: acc + part(jnp.where(
                jnp.logical_and(s == a, idx < cand), 1.0, 0.0), jnp.sum), zeros8), jnp.sum)
            ans = jnp.where(cnt < need, cand, ans)
        return ans

    jstar = lax.cond(excess, tie_search, lambda: jnp.full((1, blk), (1 << idx_bits) - 1, I32))

    def mask_chunk(c, _):
        s = sc_ref[c]
        idx = c * blk + krow
        sel = jnp.logical_or(s > a, jnp.logical_and(s == a, idx <= jstar))
        sc_ref[c] = jnp.where(sel, 0.0, NEG)
        return 0

    lax.fori_loop(0, nck, mask_chunk, 0)

    odd = lax.rem(nck, 2)
    npair = (nck + odd) // 2

    @pl.when(odd == 1)
    def _mask_extra_chunk():
        sc_ref[nck] = jnp.full((blk, blk), NEG, F32)

    m_ref[...] = jnp.full(m_ref.shape, NEG, F32)
    l_ref[...] = jnp.zeros(l_ref.shape, F32)
    acc_ref[...] = jnp.zeros(acc_ref.shape, F32)
    q_m = pair_split(q_ref, QK_SCALE)

    def stage_logits(c, slot):
        c = jnp.minimum(c, n_chunks - 1)
        c0 = pl.multiple_of(c * blk, blk)
        madd = sc_ref[c]
        o_idx = jnp.clip(c - i + (n_off - 1), 0, n_off - 1)
        for p in range(N_PAIRS):
            k2 = k_ref[0, pl.ds(c0, blk), p * LANES:(p + 1) * LANES]
            for hh in range(2):
                h = 2 * p + hh
                s = lax.dot_general(k2, q_m[h], NT_DIMS, preferred_element_type=F32)
                s = s + bias_ref[h, o_idx] + madd
                st_ref[slot, h] = s
                mx_ref[slot, h] = jnp.max(s, axis=0, keepdims=True)

    def stage_values(c, slot):
        for p in range(N_PAIRS):
            vt2 = vt_ref[0, c, p * LANES:(p + 1) * LANES, :]
            for hh in range(2):
                h = 2 * p + hh
                m_old = m_ref[h]
                m_new = jnp.maximum(m_old, mx_ref[slot, h])
                alpha = jnp.exp(m_old - m_new)
                pexp = jnp.exp(st_ref[slot, h] - m_new)
                l_ref[h] = alpha * l_ref[h] + jnp.sum(pexp, axis=0, keepdims=True)
                acc_ref[h] = alpha * acc_ref[h] + jnp.dot(vt2, pexp.astype(BF16),
                                                          preferred_element_type=F32)
                m_ref[h] = m_new

    stage_logits(0, 0)

    def att_pair(pp, _):
        c = 2 * pp
        stage_logits(c + 1, 1)
        stage_values(c, 0)
        stage_logits(c + 2, 0)
        stage_values(c + 1, 1)
        return 0

    lax.fori_loop(0, npair, att_pair, 0)

    lo_rows = lax.broadcasted_iota(I32, (LANES, blk), 0) < HALF
    for p in range(N_PAIRS):
        oa = acc_ref[2 * p] / l_ref[2 * p]
        ob = acc_ref[2 * p + 1] / l_ref[2 * p + 1]
        o_ref[0, :, p * LANES:(p + 1) * LANES] = jnp.where(lo_rows, oa, ob).T.astype(o_ref.dtype)


def _dsa(proj3, vt4, rel_bias):
    B, S, _ = proj3.shape
    blk = DSA_BLK
    n_off = _dsa_n_off(blk)
    n_sel = min(TOPK_MAX, S // 4)
    idx_bits = max(1, int(math.ceil(math.log2(S))))
    bucket = jnp.asarray(_dsa_bucket_tiles(blk))
    n_chunks = S // blk
    assert S % blk == 0 and n_chunks % 2 == 0, "DSA consumes key chunks in pairs"
    kern = functools.partial(_dsa_kernel, blk=blk, n_chunks=n_chunks, n_sel=n_sel, n_off=n_off,
                             idx_bits=idx_bits)
    return pl.pallas_call(
        kern,
        grid=(B, S // blk),
        in_specs=[
            pl.BlockSpec((n_off, blk, blk), lambda b, i: (0, 0, 0)),
            pl.BlockSpec(memory_space=pltpu.SMEM),
            pl.BlockSpec((1, blk, ATT_W), lambda b, i: (b, i, COL_QA // ATT_W)),
            pl.BlockSpec((1, S, ATT_W), lambda b, i: (b, 0, COL_KA // ATT_W)),
            pl.BlockSpec((1, S // blk, ATT_W, blk), lambda b, i: (b, 0, 0, 0)),
            pl.BlockSpec((1, blk, ATT_W), lambda b, i: (b, i, COL_QI // ATT_W)),
            pl.BlockSpec((1, S, LANES), lambda b, i: (b, 0, COL_KK // LANES)),
            pl.BlockSpec((1, blk, LANES), lambda b, i: (b, i, COL_WI // LANES)),
        ],
        out_specs=pl.BlockSpec((1, blk, ATT_W), lambda b, i: (b, i, 0)),
        out_shape=jax.ShapeDtypeStruct((B, S, ATT_W), BF16),
        scratch_shapes=[
            pltpu.VMEM((S // blk, blk, blk), F32),
            pltpu.VMEM((A_HEADS, n_off, blk, blk), F32),
            pltpu.VMEM((A_HEADS, 1, blk), F32),
            pltpu.VMEM((A_HEADS, 1, blk), F32),
            pltpu.VMEM((A_HEADS, LANES, blk), F32),
            pltpu.VMEM((2, A_HEADS, blk, blk), F32),
            pltpu.VMEM((2, A_HEADS, 1, blk), F32),
        ],
        compiler_params=_params(("arbitrary", "arbitrary")),
    )(bucket, rel_bias, proj3, proj3, vt4, proj3, proj3, proj3)


def _sb_kernel(q_ref, k_ref, v_ref, o_ref, hl_ref, ls_ref, rs_ref, *, t, n_chunks):
    i = pl.program_id(2)
    n = i + 1
    odd = lax.rem(n, 2)
    top = i + odd
    npair = (n + odd) // 2
    lane = lax.broadcasted_iota(I32, (t, LANES), 1)
    lo_half = lane < HALF
    q2 = q_ref[0].astype(F32) * QK_SCALE
    q_m = (jnp.where(lo_half, q2, 0.0).astype(BF16), jnp.where(lo_half, 0.0, q2).astype(BF16))
    r = lax.broadcasted_iota(I32, (t, t), 0)
    cidx = lax.broadcasted_iota(I32, (t, t), 1)
    later = (r > cidx).astype(BF16)
    diff = cidx - r

    def stage_terms(step, slot, masked):
        chunk = top - step
        c0 = pl.multiple_of(jnp.clip(chunk, 0, n_chunks - 1) * t, t)
        k2 = k_ref[0, pl.ds(c0, t), :]
        if masked:
            keep = diff < (i - chunk) * t
        for hh in range(2):
            z = lax.dot_general(q_m[hh], k2, NT_DIMS, preferred_element_type=F32)
            sp = jnp.maximum(z, 0.0) + jnp.log(1.0 + jnp.exp(-jnp.abs(z)))
            lm = -sp
            ls = z - sp
            if masked:
                lm = jnp.where(keep, lm, 0.0)
                ls = jnp.where(keep, ls, NEG)
            hi = lm.astype(BF16)
            hl_ref[slot, 2 * hh] = hi
            hl_ref[slot, 2 * hh + 1] = (lm - hi.astype(F32)).astype(BF16)
            ls_ref[slot, hh] = ls
            rs_ref[slot, hh] = jnp.sum(lm, axis=1, keepdims=True)

    def stage_apply(step, slot, carry):
        c0 = pl.multiple_of(jnp.minimum(top - step, n_chunks - 1) * t, t)
        v2 = v_ref[0, pl.ds(c0, t), :]
        after4 = jnp.dot(hl_ref[slot].reshape(4 * t, t), later, preferred_element_type=F32)
        out = []
        for hh in range(2):
            car, acc = carry[hh]
            after = after4[(2 * hh) * t:(2 * hh + 1) * t] + after4[(2 * hh + 1) * t:(2 * hh + 2) * t]
            w = jnp.exp(ls_ref[slot, hh] + after + car)
            acc = acc + jnp.dot(w.astype(BF16), v2, preferred_element_type=F32)
            out.append((car + rs_ref[slot, hh], acc))
        return tuple(out)

    z1 = jnp.zeros((t, 1), F32)
    za = jnp.zeros((t, LANES), F32)
    stage_terms(0, 0, True)
    stage_terms(1, 1, True)
    carry = stage_apply(0, 0, ((z1, za), (z1, za)))
    stage_terms(2, 0, False)
    carry = stage_apply(1, 1, carry)

    def pair_body(pp, carry):
        step = 2 * pp
        stage_terms(step + 1, 1, False)
        carry = stage_apply(step, 0, carry)
        stage_terms(step + 2, 0, False)
        return stage_apply(step + 1, 1, carry)

    (_, acc_a), (_, acc_b) = lax.fori_loop(1, npair, pair_body, carry)
    o_ref[0] = jnp.where(lo_half, acc_a, acc_b).astype(o_ref.dtype)


def _stick_breaking(proj3):
    B, S, _ = proj3.shape
    t = SB_T
    qb, kb, vb = COL_QB // LANES, COL_KB // LANES, COL_VB // LANES
    return pl.pallas_call(
        functools.partial(_sb_kernel, t=t, n_chunks=S // t),
        grid=(B, N_PAIRS, S // t),
        in_specs=[
            pl.BlockSpec((1, t, LANES), lambda b, p, i: (b, i, qb + p)),
            pl.BlockSpec((1, S, LANES), lambda b, p, i: (b, 0, kb + p)),
            pl.BlockSpec((1, S, LANES), lambda b, p, i: (b, 0, vb + p)),
        ],
        out_specs=pl.BlockSpec((1, t, LANES), lambda b, p, i: (b, i, p)),
        out_shape=jax.ShapeDtypeStruct((B, S, ATT_W), BF16),
        scratch_shapes=[
            pltpu.VMEM((2, 4, t, t), BF16),
            pltpu.VMEM((2, 2, t, t), F32),
            pltpu.VMEM((2, 2, t, 1), F32),
        ],
        compiler_params=_params(("arbitrary", "arbitrary", "arbitrary")),
    )(proj3, proj3, proj3)


def _layer_norm(r, g, b):
    mu = jnp.mean(r, axis=-1, keepdims=True)
    d = r - mu
    var = jnp.mean(d * d, axis=-1, keepdims=True)
    return d * lax.rsqrt(var + LN_EPS) * g + b


def _split_bf16(v):
    hi = v.astype(BF16)
    return hi, (v - hi.astype(F32)).astype(BF16)


def _merge_kernel(x_ref, ya_ref, yb_ref, gate_ref, wa_ref, wb_ref, wo_ref, g_ref, b_ref,
                  wr_ref, br_ref, h_ref, e_ref, p_ref, *, alpha, d):
    pa = jnp.dot(ya_ref[...], wa_ref[...], preferred_element_type=F32)
    pb = jnp.dot(yb_ref[...], wb_ref[...], preferred_element_type=F32)
    merged = jax.nn.sigmoid(gate_ref[:, :d]) * pa + jax.nn.sigmoid(gate_ref[:, d:]) * pb
    m = jnp.dot(merged.astype(BF16), wo_ref[...], preferred_element_type=F32)
    h = _layer_norm(alpha * x_ref[...] + m, g_ref[...], b_ref[...])
    h_ref[...] = h

    h_hi, h_lo = _split_bf16(h)
    w_hi, w_lo = _split_bf16(wr_ref[...])
    logit = (lax.dot_general(w_hi, h_hi, NT_DIMS, preferred_element_type=F32)
             + lax.dot_general(w_hi, h_lo, NT_DIMS, preferred_element_type=F32)
             + lax.dot_general(w_lo, h_hi, NT_DIMS, preferred_element_type=F32)) + br_ref[...]
    eid = lax.broadcasted_iota(I32, logit.shape, 0)
    vals, ids = [], []
    for _ in range(TOP_K):
        mx = jnp.max(logit, axis=0, keepdims=True)
        am = jnp.min(jnp.where(logit == mx, eid, N_EXPERTS), axis=0, keepdims=True)
        vals.append(mx)
        ids.append(am)
        logit = jnp.where(eid == am, -jnp.inf, logit)
    ex = [jnp.exp(v - vals[0]) for v in vals]
    den = ex[0] + ex[1] + ex[2] + ex[3]
    for k in range(TOP_K):
        e_ref[k:k + 1, :] = ids[k]
        p_ref[k:k + 1, :] = ex[k] / den


def _merge(x2, ya, yb, gates, wa, wb, wo, g, b, wr_t, br, alpha):
    T, D = x2.shape
    tm = MERGE_TM
    row = lambda i: (i, 0)
    fixed = lambda i: (0, 0)
    return pl.pallas_call(
        functools.partial(_merge_kernel, alpha=alpha, d=D),
        grid=(T // tm,),
        in_specs=[
            pl.BlockSpec((tm, D), row),
            pl.BlockSpec((tm, ATT_W), row),
            pl.BlockSpec((tm, ATT_W), row),
            pl.BlockSpec((tm, 2 * D), row),
            pl.BlockSpec((ATT_W, D), fixed),
            pl.BlockSpec((ATT_W, D), fixed),
            pl.BlockSpec((D, D), fixed),
            pl.BlockSpec((1, D), fixed),
            pl.BlockSpec((1, D), fixed),
            pl.BlockSpec((N_EXPERTS, D), fixed),
            pl.BlockSpec((N_EXPERTS, 1), fixed),
        ],
        out_specs=[
            pl.BlockSpec((tm, D), row),
            pl.BlockSpec((TOP_K, tm), lambda i: (0, i)),
            pl.BlockSpec((TOP_K, tm), lambda i: (0, i)),
        ],
        out_shape=[
            jax.ShapeDtypeStruct((T, D), F32),
            jax.ShapeDtypeStruct((TOP_K, T), I32),
            jax.ShapeDtypeStruct((TOP_K, T), F32),
        ],
        compiler_params=_params(("arbitrary",)),
    )(x2, ya, yb, gates, wa, wb, wo, g, b, wr_t, br)


def _row_gather_start(src_hbm, idx_of, dst_of, sem, n):
    def body(r, _):
        pltpu.make_async_copy(src_hbm.at[pl.ds(idx_of(r), 1), :], dst_of(r), sem).start()
        return 0
    lax.fori_loop(0, n, body, 0, unroll=8)


def _moe_kernel(blk_e_ref, nused_ref, tok_cur_ref, tok_nxt_ref, g_ref, h_hbm, wgu_ref, bgu_ref,
                wdn_ref, bdn_ref, o_ref, xbuf, wgu_s, wdn_s, sem, *, blk, f):
    i = pl.program_id(0)
    nused = nused_ref[0]
    slot = lax.rem(i, 2)

    def issue(tok_ref, s):
        _row_gather_start(h_hbm, lambda r: tok_ref[0, 0, r],
                          lambda r: xbuf.at[s, pl.ds(r, 1), :], sem.at[s], blk)

    @pl.when(i == 0)
    def _first():
        issue(tok_cur_ref, 0)

    @pl.when(i + 1 < nused)
    def _prefetch():
        issue(tok_nxt_ref, 1 - slot)

    @pl.when(i < nused)
    def _compute():
        pltpu.make_async_copy(h_hbm.at[pl.ds(0, blk), :], xbuf.at[slot], sem.at[slot]).wait()
        changed = jnp.logical_or(i == 0, blk_e_ref[i] != blk_e_ref[jnp.maximum(i - 1, 0)])

        @pl.when(changed)
        def _cast_weights():
            wgu_s[...] = wgu_ref[0].astype(BF16)
            wdn_s[...] = wdn_ref[0].astype(BF16)

        x = xbuf[slot].astype(BF16)
        hgu = jnp.dot(x, wgu_s[...], preferred_element_type=F32) + bgu_ref[0]
        a = jnp.minimum(hgu[:, :f], SWIGLU_LIMIT)
        u = jnp.clip(hgu[:, f:], -SWIGLU_LIMIT, SWIGLU_LIMIT)
        glu = a * jax.nn.sigmoid(a * SWIGLU_ALPHA)
        y = jnp.dot(((u + 1.0) * glu).astype(BF16), wdn_s[...], preferred_element_type=F32) + bdn_ref[0]
        o_ref[...] = y * g_ref[...]

    @pl.when(i >= nused)
    def _unused_block():
        o_ref[...] = jnp.zeros(o_ref.shape, o_ref.dtype)


def _moe_ffn(h, blk_e, nused, row_tok, row_g, w_gu, b_gu, w_dn, b_dn):
    T, D = h.shape
    E, _, F2 = w_gu.shape
    f = F2 // 2
    blk = MOE_BLK
    nb = row_tok.shape[0] // blk
    tok3 = row_tok.reshape(nb, 1, blk)
    grid_spec = pltpu.PrefetchScalarGridSpec(
        num_scalar_prefetch=2,
        grid=(nb,),
        in_specs=[
            pl.BlockSpec((1, 1, blk), lambda i, be, nu: (i, 0, 0), memory_space=pltpu.SMEM),
            pl.BlockSpec((1, 1, blk), lambda i, be, nu: (jnp.minimum(i + 1, nb - 1), 0, 0),
                         memory_space=pltpu.SMEM),
            pl.BlockSpec((blk, 1), lambda i, be, nu: (i, 0)),
            pl.BlockSpec(memory_space=pl.ANY),
            pl.BlockSpec((1, D, F2), lambda i, be, nu: (be[i], 0, 0)),
            pl.BlockSpec((1, 1, F2), lambda i, be, nu: (be[i], 0, 0)),
            pl.BlockSpec((1, f, D), lambda i, be, nu: (be[i], 0, 0)),
            pl.BlockSpec((1, 1, D), lambda i, be, nu: (be[i], 0, 0)),
        ],
        out_specs=pl.BlockSpec((blk, D), lambda i, be, nu: (i, 0)),
        scratch_shapes=[
            pltpu.VMEM((2, blk, D), F32),
            pltpu.VMEM((D, F2), BF16),
            pltpu.VMEM((f, D), BF16),
            pltpu.SemaphoreType.DMA((2,)),
        ],
    )
    return pl.pallas_call(
        functools.partial(_moe_kernel, blk=blk, f=f),
        grid_spec=grid_spec,
        out_shape=jax.ShapeDtypeStruct((nb * blk, D), F32),
        compiler_params=_params(("arbitrary",)),
    )(blk_e, nused, tok3, tok3, row_g.reshape(-1, 1), h, w_gu, b_gu.reshape(E, 1, F2),
      w_dn, b_dn.reshape(E, 1, D))


def _comb_kernel(pos_cur_ref, pos_nxt_ref, h_ref, y_hbm, g_ref, b_ref, o_ref, ybuf, sem, *, tm, alpha):
    i = pl.program_id(0)
    n = pl.num_programs(0)
    slot = lax.rem(i, 2)

    def issue(pos_ref, s):
        for k in range(TOP_K):
            _row_gather_start(y_hbm, lambda r: pos_ref[0, k, r],
                              lambda r: ybuf.at[s, k, pl.ds(r, 1), :], sem.at[s], tm)

    @pl.when(i == 0)
    def _first():
        issue(pos_cur_ref, 0)

    @pl.when(i + 1 < n)
    def _prefetch():
        issue(pos_nxt_ref, 1 - slot)

    for k in range(TOP_K):
        pltpu.make_async_copy(y_hbm.at[pl.ds(0, tm), :], ybuf.at[slot, k], sem.at[slot]).wait()
    fsum = (ybuf[slot, 0] + ybuf[slot, 1]) + (ybuf[slot, 2] + ybuf[slot, 3])
    o_ref[...] = _layer_norm(alpha * h_ref[...] + fsum, g_ref[...], b_ref[...])


def _combine(h, ys, pos, g, b, alpha):
    T, D = h.shape
    tm = COMB_TM
    nt = T // tm
    pos3 = pos.reshape(TOP_K, nt, tm).transpose(1, 0, 2)
    return pl.pallas_call(
        functools.partial(_comb_kernel, tm=tm, alpha=alpha),
        grid=(nt,),
        in_specs=[
            pl.BlockSpec((1, TOP_K, tm), lambda i: (i, 0, 0), memory_space=pltpu.SMEM),
            pl.BlockSpec((1, TOP_K, tm), lambda i: (jnp.minimum(i + 1, nt - 1), 0, 0),
                         memory_space=pltpu.SMEM),
            pl.BlockSpec((tm, D), lambda i: (i, 0)),
            pl.BlockSpec(memory_space=pl.ANY),
            pl.BlockSpec((1, D), lambda i: (0, 0)),
            pl.BlockSpec((1, D), lambda i: (0, 0)),
        ],
        out_specs=pl.BlockSpec((tm, D), lambda i: (i, 0)),
        out_shape=jax.ShapeDtypeStruct((T, D), F32),
        scratch_shapes=[pltpu.VMEM((2, TOP_K, tm, D), F32), pltpu.SemaphoreType.DMA((2,))],
        compiler_params=_params(("arbitrary",)),
    )(pos3, pos3, h, ys, g, b)


def _route(top_e, top_p, blk):
    K, T = top_e.shape
    N = K * T
    flat_e = top_e.reshape(N)
    flat_p = top_p.reshape(N)
    experts = jnp.arange(N_EXPERTS, dtype=I32)
    order = jnp.argsort(flat_e, stable=True).astype(I32)
    inv = jnp.argsort(order).astype(I32)
    counts = jnp.sum(flat_e[:, None] == experts[None, :], axis=0, dtype=I32)
    padded = (counts + blk - 1) // blk * blk
    pends = jnp.cumsum(padded)
    offs = jnp.cumsum(counts) - counts
    shift = (pends - padded) - offs
    pos = inv + shift[flat_e]
    P = N + N_EXPERTS * blk
    nb = P // blk
    blk_start = jnp.arange(nb, dtype=I32) * blk
    blk_e = jnp.minimum(jnp.sum(pends[None, :] <= blk_start[:, None], axis=1, dtype=I32), N_EXPERTS - 1)
    e_row = jnp.repeat(blk_e, blk)
    j = jnp.arange(P, dtype=I32) - shift[e_row]
    valid = j < (offs + counts)[e_row]
    src = order[jnp.clip(j, 0, N - 1)]
    row_tok = jnp.where(valid, src % T, 0)
    row_g = jnp.where(valid, flat_p[src], 0.0)
    nused = (pends[-1:] // blk).astype(I32)
    return blk_e, nused, row_tok, row_g, pos.reshape(K, T)


def _projection_weights(w_in_l):
    sizes = (ATT_W, ATT_W, ATT_W, IDX_HEADS * IDX_DIM, IDX_DIM, IDX_HEADS, ATT_W, ATT_W, ATT_W)
    offs = np.concatenate([[0], np.cumsum(sizes)])
    qa, ka, va, qi, ki, wi, qb, kb, vb = (w_in_l[:, offs[n]:offs[n + 1]] for n in range(9))
    pad_wi = jnp.zeros((w_in_l.shape[0], LANES - IDX_HEADS), w_in_l.dtype)
    w_att = jnp.concatenate([qa, ka, qi, qb, kb, vb, ki, ki, wi, pad_wi], axis=1).astype(BF16)
    w_va_t = va.T.astype(BF16)
    w_gate = w_in_l[:, offs[9]:].astype(BF16)
    return w_att, w_va_t, w_gate


def kernel(x, w_in, w_branch_a, w_branch_b, w_out, rel_bias, ln1_g, ln1_b, w_router, b_router,
           w_gate_up, b_gate_up, w_down, b_down, ln2_g, ln2_b):
    B, S, D = x.shape
    depth = w_in.shape[0]
    alpha = (2 * depth) ** 0.25
    T = B * S
    h = x.reshape(T, D)
    for l in range(depth):
        w_att, w_va_t, w_gate = _projection_weights(w_in[l])
        proj = _matmul(h, w_att, BF16, min(T, 1024), PROJ_TN).reshape(B, S, ATT_COLS)
        vt = _matmul_t(h, w_va_t, DSA_BLK).reshape(B, S // DSA_BLK, ATT_W, DSA_BLK)
        gates = _matmul(h, w_gate, F32, min(T, 1024), 1024)
        ya = _dsa(proj, vt, rel_bias).reshape(T, ATT_W)
        yb = _stick_breaking(proj).reshape(T, ATT_W)
        h1, top_e, top_p = _merge(
            h, ya, yb, gates, w_branch_a[l].astype(BF16), w_branch_b[l].astype(BF16),
            w_out[l].astype(BF16), ln1_g[l].reshape(1, D), ln1_b[l].reshape(1, D),
            w_router[l].T, b_router[l].reshape(N_EXPERTS, 1), alpha)
        blk_e, nused, row_tok, row_g, pos = _route(top_e, top_p, MOE_BLK)
        ys = _moe_ffn(h1, blk_e, nused, row_tok, row_g, w_gate_up[l], b_gate_up[l], w_down[l], b_down[l])
        h = _combine(h1, ys, pos, ln2_g[l].reshape(1, D), ln2_b[l].reshape(1, D), alpha)
    return h.reshape(B, S, D)
```

```python
import functools
import math

import numpy as np
import jax
import jax.numpy as jnp
from jax import lax
from jax.experimental import pallas as pl
from jax.experimental.pallas import tpu as pltpu

F32 = jnp.float32
BF16 = jnp.bfloat16
I32 = jnp.int32

A_HEADS = 8
HEAD_DIM = 64
ATT_W = A_HEADS * HEAD_DIM
IDX_HEADS = 8
IDX_DIM = 64
IDX_SCALE = (IDX_HEADS * IDX_DIM) ** -0.5
TOPK_MAX = 256
N_BUCKETS = 32
MAX_DISTANCE = 128
N_EXPERTS = 32
TOP_K = 4
SWIGLU_LIMIT = 7.0
SWIGLU_ALPHA = 1.702
LN_EPS = 1e-5
QK_SCALE = HEAD_DIM ** -0.5

LANES = 128
SUBLANES = 8
HALF = LANES // 2
N_PAIRS = A_HEADS // 2
VMEM_LIMIT = 56 * 1024 * 1024

DSA_BLK = 256
SB_T = 256
MERGE_TM = 512
MOE_BLK = 256
MOE_ROW_SPLIT = 2
COMB_TM = 256
BISECT_CAP = 48
NEG = -1e30

COL_QA, COL_KA, COL_QI, COL_QB, COL_KB, COL_VB = (g * ATT_W for g in range(6))
COL_KK = 6 * ATT_W
COL_WI = COL_KK + LANES
ATT_COLS = COL_WI + LANES
PROJ_TN = ATT_COLS // 2

NT_DIMS = (((1,), (1,)), ((), ()))


def _params(sem, vmem=VMEM_LIMIT):
    return pltpu.CompilerParams(dimension_semantics=sem, vmem_limit_bytes=vmem)


def _mm_kernel(x_ref, w_ref, o_ref, xb_ref):
    @pl.when(pl.program_id(1) == 0)
    def _cast():
        xb_ref[...] = x_ref[...].astype(BF16)

    o_ref[...] = jnp.dot(xb_ref[...], w_ref[...], preferred_element_type=F32).astype(o_ref.dtype)


def _matmul(x, w, out_dtype, tm, tn):
    M, K = x.shape
    N = w.shape[1]
    return pl.pallas_call(
        _mm_kernel,
        grid=(M // tm, N // tn),
        in_specs=[pl.BlockSpec((tm, K), lambda i, j: (i, 0)),
                  pl.BlockSpec((K, tn), lambda i, j: (0, j))],
        out_specs=pl.BlockSpec((tm, tn), lambda i, j: (i, j)),
        out_shape=jax.ShapeDtypeStruct((M, N), out_dtype),
        scratch_shapes=[pltpu.VMEM((tm, K), BF16)],
        compiler_params=_params(("arbitrary", "arbitrary")),
    )(x, w)


def _mm_t_kernel(x_ref, w_ref, o_ref):
    o_ref[0] = lax.dot_general(w_ref[...], x_ref[...].astype(BF16), NT_DIMS,
                               preferred_element_type=F32).astype(o_ref.dtype)


def _matmul_t(x, w_t, tm):
    M, K = x.shape
    N = w_t.shape[0]
    return pl.pallas_call(
        _mm_t_kernel,
        grid=(M // tm,),
        in_specs=[pl.BlockSpec((tm, K), lambda i: (i, 0)),
                  pl.BlockSpec((N, K), lambda i: (0, 0))],
        out_specs=pl.BlockSpec((1, N, tm), lambda i: (i, 0, 0)),
        out_shape=jax.ShapeDtypeStruct((M // tm, N, tm), BF16),
        compiler_params=_params(("arbitrary",)),
    )(x, w_t)


def _t5_bucket_np(n):
    n = np.maximum(n, 0)
    max_exact = N_BUCKETS // 2
    nf = np.maximum(n, 1).astype(np.float32)
    large = max_exact + (np.log(nf / max_exact) / math.log(MAX_DISTANCE / max_exact)
                         * (N_BUCKETS - max_exact)).astype(np.int32)
    large = np.minimum(large, N_BUCKETS - 1)
    return np.where(n < max_exact, n, large).astype(np.int32)


def _dsa_n_off(blk):
    return 2 + -(-MAX_DISTANCE // blk)


def _dsa_bucket_tiles(blk):
    n_off = _dsa_n_off(blk)
    j = np.arange(blk)[None, :, None]
    i = np.arange(blk)[None, None, :]
    o = np.arange(n_off)[:, None, None]
    return _t5_bucket_np(i - j + blk * (n_off - 1 - o))


def _dsa_kernel(bucket_ref, relb_ref, q_ref, k_ref, vt_ref, qi_ref, kk_ref, wi_ref, o_ref,
                sc_ref, bias_ref, m_ref, l_ref, acc_ref, st_ref, mx_ref,
                *, blk, n_chunks, n_sel, n_off, idx_bits):
    b = pl.program_id(0)
    i = pl.program_id(1)
    q0 = i * blk
    nck = i + 1
    groups = blk // SUBLANES

    @pl.when(jnp.logical_and(b == 0, i == 0))
    def _build_bias():
        def head_body(h, _):
            for o in range(n_off):
                for rb in range(blk // LANES):
                    for cb in range(blk // LANES):
                        rs = slice(rb * LANES, (rb + 1) * LANES)
                        cs = slice(cb * LANES, (cb + 1) * LANES)
                        bk = bucket_ref[o, rs, cs]

                        def bucket_body(n, acc):
                            return jnp.where(bk == n, relb_ref[n, h], acc)

                        bias_ref[h, o, rs, cs] = lax.fori_loop(
                            0, N_BUCKETS, bucket_body, jnp.zeros((LANES, LANES), F32))
            return 0

        lax.fori_loop(0, A_HEADS, head_body, 0)

    lane = lax.broadcasted_iota(I32, (blk, LANES), 1)
    lo_half = lane < HALF
    krow = lax.broadcasted_iota(I32, (blk, blk), 0)
    qpos = q0 + lax.broadcasted_iota(I32, (1, blk), 1)

    def pair_split(ref, scale):
        out = []
        for p in range(N_PAIRS):
            v = ref[0, :, p * LANES:(p + 1) * LANES].astype(F32)
            if scale != 1.0:
                v = v * scale
            out.append(jnp.where(lo_half, v, 0.0).astype(BF16))
            out.append(jnp.where(lo_half, 0.0, v).astype(BF16))
        return out

    wi_t = wi_ref[0].astype(F32).T
    wrow = [wi_t[h:h + 1, :] * IDX_SCALE for h in range(IDX_HEADS)]
    qi_m = pair_split(qi_ref, 1.0)

    def score_chunk(c, _):
        c0 = pl.multiple_of(c * blk, blk)
        kk = kk_ref[0, pl.ds(c0, blk), :]
        acc = jnp.zeros((blk, blk), F32)
        for h in range(IDX_HEADS):
            s = lax.dot_general(kk, qi_m[h], NT_DIMS, preferred_element_type=F32)
            acc = acc + wrow[h] * jnp.maximum(s, 0.0)
        sc_ref[c] = jnp.where(c0 + krow <= qpos, acc, -jnp.inf)
        return 0

    lax.fori_loop(0, nck, score_chunk, 0)

    kt = jnp.minimum(qpos + 1, n_sel).astype(F32)

    def fold(fn, init):
        def body(c, acc):
            return fn(acc, sc_ref[c], c * blk + krow)
        return lax.fori_loop(0, nck, body, init)

    def part(x, op):
        return op(x.reshape(groups, SUBLANES, blk), axis=0)

    def fin(x, op):
        return op(x, axis=0, keepdims=True)

    zeros8 = jnp.zeros((SUBLANES, blk), F32)
    pinf8 = jnp.full((SUBLANES, blk), jnp.inf, F32)

    def count_ge(th):
        return fin(fold(lambda a, s, idx: a + part(jnp.where(s >= th, 1.0, 0.0), jnp.sum), zeros8),
                   jnp.sum)

    mn, mx = fold(lambda a, s, idx: (
        jnp.minimum(a[0], part(jnp.where(s == -jnp.inf, jnp.inf, s), jnp.min)),
        jnp.maximum(a[1], part(s, jnp.max))), (pinf8, -pinf8))
    rmin = fin(mn, jnp.min)
    rmax = fin(mx, jnp.max)

    def bis_cond(st):
        it, lo, hi, clo = st
        return jnp.logical_and(it < BISECT_CAP, jnp.max(jnp.abs(clo - kt)) > 0.0)

    def bis_body(st):
        it, lo, hi, clo = st
        mid = 0.5 * lo + 0.5 * hi
        c = count_ge(mid)
        active = clo != kt
        up = jnp.logical_and(active, c >= kt)
        dn = jnp.logical_and(active, c < kt)
        return (it + 1, jnp.where(up, mid, lo), jnp.where(dn, mid, hi), jnp.where(up, c, clo))

    _, lo, _, _ = lax.while_loop(
        bis_cond, bis_body, (jnp.int32(0), rmin, rmax + 1.0, (qpos + 1).astype(F32)))

    def stats(lo_):
        a_ = fin(fold(lambda a, s, idx: jnp.minimum(a, part(jnp.where(s >= lo_, s, jnp.inf), jnp.min)),
                      pinf8), jnp.min)
        cg, ct, nx = fold(
            lambda a, s, idx: (a[0] + part(jnp.where(s > a_, 1.0, 0.0), jnp.sum),
                               a[1] + part(jnp.where(s == a_, 1.0, 0.0), jnp.sum),
                               jnp.minimum(a[2], part(jnp.where(s > a_, s, jnp.inf), jnp.min))),
            (zeros8, zeros8, pinf8))
        return a_, fin(cg, jnp.sum), fin(ct, jnp.sum), fin(nx, jnp.min)

    def fin_cond(st):
        return st[0]

    def fin_body(st):
        _, lo_, _, _ = st
        a_, cgt_, nt_, nxt_ = stats(lo_)
        bad = cgt_ >= kt
        return (jnp.max(jnp.where(bad, 1.0, 0.0)) > 0.0, jnp.where(bad, nxt_, a_), cgt_, nt_)

    _, a, cgt, nties = lax.while_loop(fin_cond, fin_body, (jnp.bool_(True), lo, kt, kt))
    need = kt - cgt
    excess = jnp.max(jnp.where(nties > need, 1.0, 0.0)) > 0.0

    def tie_search():
        ans = jnp.zeros((1, blk), I32)
        for bit in reversed(range(idx_bits)):
            cand = ans + (1 << bit)
            cnt = fin(fold(lambda acc, s, idx: acc + part(jnp.where(
                jnp.logical_and(s == a, idx < cand), 1.0, 0.0), jnp.sum), zeros8), jnp.sum)
            ans = jnp.where(cnt < need, cand, ans)
        return ans

    jstar = lax.cond(excess, tie_search, lambda: jnp.full((1, blk), (1 << idx_bits) - 1, I32))

    def mask_chunk(c, _):
        s = sc_ref[c]
        idx = c * blk + krow
        sel = jnp.logical_or(s > a, jnp.logical_and(s == a, idx <= jstar))
        sc_ref[c] = jnp.where(sel, 0.0, NEG)
        return 0

    lax.fori_loop(0, nck, mask_chunk, 0)

    odd = lax.rem(nck, 2)
    npair = (nck + odd) // 2

    @pl.when(odd == 1)
    def _mask_extra_chunk():
        sc_ref[nck] = jnp.full((blk, blk), NEG, F32)

    m_ref[...] = jnp.full(m_ref.shape, NEG, F32)
    l_ref[...] = jnp.zeros(l_ref.shape, F32)
    acc_ref[...] = jnp.zeros(acc_ref.shape, F32)
    q_m = pair_split(q_ref, QK_SCALE)

    def stage_logits(c, slot):
        c = jnp.minimum(c, n_chunks - 1)
        c0 = pl.multiple_of(c * blk, blk)
        madd = sc_ref[c]
        o_idx = jnp.clip(c - i + (n_off - 1), 0, n_off - 1)
        for p in range(N_PAIRS):
            k2 = k_ref[0, pl.ds(c0, blk), p * LANES:(p + 1) * LANES]
            for hh in range(2):
                h = 2 * p + hh
                s = lax.dot_general(k2, q_m[h], NT_DIMS, preferred_element_type=F32)
                s = s + bias_ref[h, o_idx] + madd
                st_ref[slot, h] = s
                mx_ref[slot, h] = jnp.max(s, axis=0, keepdims=True)

    def stage_values(c, slot):
        for p in range(N_PAIRS):
            vt2 = vt_ref[0, c, p * LANES:(p + 1) * LANES, :]
            for hh in range(2):
                h = 2 * p + hh
                m_old = m_ref[h]
                m_new = jnp.maximum(m_old, mx_ref[slot, h])
                alpha = jnp.exp(m_old - m_new)
                pexp = jnp.exp(st_ref[slot, h] - m_new)
                l_ref[h] = alpha * l_ref[h] + jnp.sum(pexp, axis=0, keepdims=True)
                acc_ref[h] = alpha * acc_ref[h] + jnp.dot(vt2, pexp.astype(BF16),
                                                          preferred_element_type=F32)
                m_ref[h] = m_new

    stage_logits(0, 0)

    def att_pair(pp, _):
        c = 2 * pp
        stage_logits(c + 1, 1)
        stage_values(c, 0)
        stage_logits(c + 2, 0)
        stage_values(c + 1, 1)
        return 0

    lax.fori_loop(0, npair, att_pair, 0)

    lo_rows = lax.broadcasted_iota(I32, (LANES, blk), 0) < HALF
    for p in range(N_PAIRS):
        oa = acc_ref[2 * p] / l_ref[2 * p]
        ob = acc_ref[2 * p + 1] / l_ref[2 * p + 1]
        o_ref[0, :, p * LANES:(p + 1) * LANES] = jnp.where(lo_rows, oa, ob).T.astype(o_ref.dtype)


def _dsa(proj3, vt4, rel_bias):
    B, S, _ = proj3.shape
    blk = DSA_BLK
    n_off = _dsa_n_off(blk)
    n_sel = min(TOPK_MAX, S // 4)
    idx_bits = max(1, int(math.ceil(math.log2(S))))
    bucket = jnp.asarray(_dsa_bucket_tiles(blk))
    n_chunks = S // blk
    assert S % blk == 0 and n_chunks % 2 == 0, "DSA consumes key chunks in pairs"
    kern = functools.partial(_dsa_kernel, blk=blk, n_chunks=n_chunks, n_sel=n_sel, n_off=n_off,
                             idx_bits=idx_bits)
    return pl.pallas_call(
        kern,
        grid=(B, S // blk),
        in_specs=[
            pl.BlockSpec((n_off, blk, blk), lambda b, i: (0, 0, 0)),
            pl.BlockSpec(memory_space=pltpu.SMEM),
            pl.BlockSpec((1, blk, ATT_W), lambda b, i: (b, i, COL_QA // ATT_W)),
            pl.BlockSpec((1, S, ATT_W), lambda b, i: (b, 0, COL_KA // ATT_W)),
            pl.BlockSpec((1, S // blk, ATT_W, blk), lambda b, i: (b, 0, 0, 0)),
            pl.BlockSpec((1, blk, ATT_W), lambda b, i: (b, i, COL_QI // ATT_W)),
            pl.BlockSpec((1, S, LANES), lambda b, i: (b, 0, COL_KK // LANES)),
            pl.BlockSpec((1, blk, LANES), lambda b, i: (b, i, COL_WI // LANES)),
        ],
        out_specs=pl.BlockSpec((1, blk, ATT_W), lambda b, i: (b, i, 0)),
        out_shape=jax.ShapeDtypeStruct((B, S, ATT_W), BF16),
        scratch_shapes=[
            pltpu.VMEM((S // blk, blk, blk), F32),
            pltpu.VMEM((A_HEADS, n_off, blk, blk), F32),
            pltpu.VMEM((A_HEADS, 1, blk), F32),
            pltpu.VMEM((A_HEADS, 1, blk), F32),
            pltpu.VMEM((A_HEADS, LANES, blk), F32),
            pltpu.VMEM((2, A_HEADS, blk, blk), F32),
            pltpu.VMEM((2, A_HEADS, 1, blk), F32),
        ],
        compiler_params=_params(("arbitrary", "arbitrary")),
    )(bucket, rel_bias, proj3, proj3, vt4, proj3, proj3, proj3)


def _sb_kernel(q_ref, k_ref, v_ref, o_ref, hl_ref, ls_ref, rs_ref, *, t, n_chunks):
    i = pl.program_id(2)
    n = i + 1
    odd = lax.rem(n, 2)
    top = i + odd
    npair = (n + odd) // 2
    lane = lax.broadcasted_iota(I32, (t, LANES), 1)
    lo_half = lane < HALF
    q2 = q_ref[0].astype(F32) * QK_SCALE
    q_m = (jnp.where(lo_half, q2, 0.0).astype(BF16), jnp.where(lo_half, 0.0, q2).astype(BF16))
    r = lax.broadcasted_iota(I32, (t, t), 0)
    cidx = lax.broadcasted_iota(I32, (t, t), 1)
    later = (r > cidx).astype(BF16)
    diff = cidx - r

    def stage_terms(step, slot, masked):
        chunk = top - step
        c0 = pl.multiple_of(jnp.clip(chunk, 0, n_chunks - 1) * t, t)
        k2 = k_ref[0, pl.ds(c0, t), :]
        if masked:
            keep = diff < (i - chunk) * t
        for hh in range(2):
            z = lax.dot_general(q_m[hh], k2, NT_DIMS, preferred_element_type=F32)
            sp = jnp.maximum(z, 0.0) + jnp.log(1.0 + jnp.exp(-jnp.abs(z)))
            lm = -sp
            ls = z - sp
            if masked:
                lm = jnp.where(keep, lm, 0.0)
                ls = jnp.where(keep, ls, NEG)
            hi = lm.astype(BF16)
            hl_ref[slot, 2 * hh] = hi
            hl_ref[slot, 2 * hh + 1] = (lm - hi.astype(F32)).astype(BF16)
            ls_ref[slot, hh] = ls
            rs_ref[slot, hh] = jnp.sum(lm, axis=1, keepdims=True)

    def stage_apply(step, slot, carry):
        c0 = pl.multiple_of(jnp.minimum(top - step, n_chunks - 1) * t, t)
        v2 = v_ref[0, pl.ds(c0, t), :]
        after4 = jnp.dot(hl_ref[slot].reshape(4 * t, t), later, preferred_element_type=F32)
        out = []
        for hh in range(2):
            car, acc = carry[hh]
            after = after4[(2 * hh) * t:(2 * hh + 1) * t] + after4[(2 * hh + 1) * t:(2 * hh + 2) * t]
            w = jnp.exp(ls_ref[slot, hh] + after + car)
            acc = acc + jnp.dot(w.astype(BF16), v2, preferred_element_type=F32)
            out.append((car + rs_ref[slot, hh], acc))
        return tuple(out)

    z1 = jnp.zeros((t, 1), F32)
    za = jnp.zeros((t, LANES), F32)
    stage_terms(0, 0, True)
    stage_terms(1, 1, True)
    carry = stage_apply(0, 0, ((z1, za), (z1, za)))
    stage_terms(2, 0, False)
    carry = stage_apply(1, 1, carry)

    def pair_body(pp, carry):
        step = 2 * pp
        stage_terms(step + 1, 1, False)
        carry = stage_apply(step, 0, carry)
        stage_terms(step + 2, 0, False)
        return stage_apply(step + 1, 1, carry)

    (_, acc_a), (_, acc_b) = lax.fori_loop(1, npair, pair_body, carry)
    o_ref[0] = jnp.where(lo_half, acc_a, acc_b).astype(o_ref.dtype)


def _stick_breaking(proj3):
    B, S, _ = proj3.shape
    t = SB_T
    qb, kb, vb = COL_QB // LANES, COL_KB // LANES, COL_VB // LANES
    return pl.pallas_call(
        functools.partial(_sb_kernel, t=t, n_chunks=S // t),
        grid=(B, N_PAIRS, S // t),
        in_specs=[
            pl.BlockSpec((1, t, LANES), lambda b, p, i: (b, i, qb + p)),
            pl.BlockSpec((1, S, LANES), lambda b, p, i: (b, 0, kb + p)),
            pl.BlockSpec((1, S, LANES), lambda b, p, i: (b, 0, vb + p)),
        ],
        out_specs=pl.BlockSpec((1, t, LANES), lambda b, p, i: (b, i, p)),
        out_shape=jax.ShapeDtypeStruct((B, S, ATT_W), BF16),
        scratch_shapes=[
            pltpu.VMEM((2, 4, t, t), BF16),
            pltpu.VMEM((2, 2, t, t), F32),
            pltpu.VMEM((2, 2, t, 1), F32),
        ],
        compiler_params=_params(("arbitrary", "arbitrary", "arbitrary")),
    )(proj3, proj3, proj3)


def _layer_norm(r, g, b):
    mu = jnp.mean(r, axis=-1, keepdims=True)
    d = r - mu
    var = jnp.mean(d * d, axis=-1, keepdims=True)
    return d * lax.rsqrt(var + LN_EPS) * g + b


def _split_bf16(v):
    hi = v.astype(BF16)
    return hi, (v - hi.astype(F32)).astype(BF16)


def _merge_kernel(x_ref, ya_ref, yb_ref, gate_ref, wa_ref, wb_ref, wo_ref, g_ref, b_ref,
                  wr_ref, br_ref, h_ref, e_ref, p_ref, *, alpha, d):
    pa = jnp.dot(ya_ref[...], wa_ref[...], preferred_element_type=F32)
    pb = jnp.dot(yb_ref[...], wb_ref[...], preferred_element_type=F32)
    merged = jax.nn.sigmoid(gate_ref[:, :d]) * pa + jax.nn.sigmoid(gate_ref[:, d:]) * pb
    m = jnp.dot(merged.astype(BF16), wo_ref[...], preferred_element_type=F32)
    h = _layer_norm(alpha * x_ref[...] + m, g_ref[...], b_ref[...])
    h_ref[...] = h

    h_hi, h_lo = _split_bf16(h)
    w_hi, w_lo = _split_bf16(wr_ref[...])
    logit = (lax.dot_general(w_hi, h_hi, NT_DIMS, preferred_element_type=F32)
             + lax.dot_general(w_hi, h_lo, NT_DIMS, preferred_element_type=F32)
             + lax.dot_general(w_lo, h_hi, NT_DIMS, preferred_element_type=F32)) + br_ref[...]
    eid = lax.broadcasted_iota(I32, logit.shape, 0)
    vals, ids = [], []
    for _ in range(TOP_K):
        mx = jnp.max(logit, axis=0, keepdims=True)
        am = jnp.min(jnp.where(logit == mx, eid, N_EXPERTS), axis=0, keepdims=True)
        vals.append(mx)
        ids.append(am)
        logit = jnp.where(eid == am, -jnp.inf, logit)
    ex = [jnp.exp(v - vals[0]) for v in vals]
    den = ex[0] + ex[1] + ex[2] + ex[3]
    for k in range(TOP_K):
        e_ref[k:k + 1, :] = ids[k]
        p_ref[k:k + 1, :] = ex[k] / den


def _merge(x2, ya, yb, gates, wa, wb, wo, g, b, wr_t, br, alpha):
    T, D = x2.shape
    tm = MERGE_TM
    row = lambda i: (i, 0)
    fixed = lambda i: (0, 0)
    return pl.pallas_call(
        functools.partial(_merge_kernel, alpha=alpha, d=D),
        grid=(T // tm,),
        in_specs=[
            pl.BlockSpec((tm, D), row),
            pl.BlockSpec((tm, ATT_W), row),
            pl.BlockSpec((tm, ATT_W), row),
            pl.BlockSpec((tm, 2 * D), row),
            pl.BlockSpec((ATT_W, D), fixed),
            pl.BlockSpec((ATT_W, D), fixed),
            pl.BlockSpec((D, D), fixed),
            pl.BlockSpec((1, D), fixed),
            pl.BlockSpec((1, D), fixed),
            pl.BlockSpec((N_EXPERTS, D), fixed),
            pl.BlockSpec((N_EXPERTS, 1), fixed),
        ],
        out_specs=[
            pl.BlockSpec((tm, D), row),
            pl.BlockSpec((TOP_K, tm), lambda i: (0, i)),
            pl.BlockSpec((TOP_K, tm), lambda i: (0, i)),
        ],
        out_shape=[
            jax.ShapeDtypeStruct((T, D), F32),
            jax.ShapeDtypeStruct((TOP_K, T), I32),
            jax.ShapeDtypeStruct((TOP_K, T), F32),
        ],
        compiler_params=_params(("arbitrary",)),
    )(x2, ya, yb, gates, wa, wb, wo, g, b, wr_t, br)


def _row_gather_start(src_hbm, idx_of, dst_of, sem, n):
    def body(g, _):
        base = pl.multiple_of(g * SUBLANES, SUBLANES)
        for u in range(SUBLANES):
            r = base + u
            pltpu.make_async_copy(src_hbm.at[pl.ds(idx_of(r), 1), :], dst_of(r), sem).start()
        return 0
    lax.fori_loop(0, n // SUBLANES, body, 0)


def _moe_kernel(blk_e_ref, nused_ref, tok_cur_ref, tok_nxt_ref, h_hbm, wgu_ref, bgu_ref,
                wdn_ref, bdn_ref, o_ref, xbuf, wgu_s, wdn_s, sem, *, blk, f):
    i = pl.program_id(0)
    nused = nused_ref[0]
    slot = lax.rem(i, 2)

    def issue(tok_ref, s):
        _row_gather_start(h_hbm, lambda r: tok_ref[0, 0, r],
                          lambda r: xbuf.at[s, pl.ds(r, 1), :], sem.at[s], blk)

    @pl.when(i == 0)
    def _first():
        issue(tok_cur_ref, 0)

    @pl.when(i + 1 < nused)
    def _prefetch():
        issue(tok_nxt_ref, 1 - slot)

    @pl.when(i < nused)
    def _compute():
        pltpu.make_async_copy(h_hbm.at[pl.ds(0, blk), :], xbuf.at[slot], sem.at[slot]).wait()
        changed = jnp.logical_or(i == 0, blk_e_ref[i] != blk_e_ref[jnp.maximum(i - 1, 0)])

        @pl.when(changed)
        def _cast_weights():
            wgu_s[...] = wgu_ref[0].astype(BF16)
            wdn_s[...] = wdn_ref[0].astype(BF16)

        rows = blk // MOE_ROW_SPLIT
        for part in range(MOE_ROW_SPLIT):
            rs = pl.ds(part * rows, rows)
            x = xbuf[slot, rs, :].astype(BF16)
            hgu = jnp.dot(x, wgu_s[...], preferred_element_type=F32) + bgu_ref[0]
            a = jnp.minimum(hgu[:, :f], SWIGLU_LIMIT)
            u = jnp.clip(hgu[:, f:], -SWIGLU_LIMIT, SWIGLU_LIMIT)
            glu = a * jax.nn.sigmoid(a * SWIGLU_ALPHA)
            o_ref[rs, :] = jnp.dot(((u + 1.0) * glu).astype(BF16), wdn_s[...],
                                   preferred_element_type=F32) + bdn_ref[0]

    @pl.when(i >= nused)
    def _unused_block():
        o_ref[...] = jnp.zeros(o_ref.shape, o_ref.dtype)


def _moe_ffn(h, blk_e, nused, row_tok, w_gu, b_gu, w_dn, b_dn):
    T, D = h.shape
    E, _, F2 = w_gu.shape
    f = F2 // 2
    blk = MOE_BLK
    nb = row_tok.shape[0] // blk
    tok3 = row_tok.reshape(nb, 1, blk)
    grid_spec = pltpu.PrefetchScalarGridSpec(
        num_scalar_prefetch=2,
        grid=(nb,),
        in_specs=[
            pl.BlockSpec((1, 1, blk), lambda i, be, nu: (i, 0, 0), memory_space=pltpu.SMEM),
            pl.BlockSpec((1, 1, blk), lambda i, be, nu: (jnp.minimum(i + 1, nb - 1), 0, 0),
                         memory_space=pltpu.SMEM),
            pl.BlockSpec(memory_space=pl.ANY),
            pl.BlockSpec((1, D, F2), lambda i, be, nu: (be[i], 0, 0)),
            pl.BlockSpec((1, 1, F2), lambda i, be, nu: (be[i], 0, 0)),
            pl.BlockSpec((1, f, D), lambda i, be, nu: (be[i], 0, 0)),
            pl.BlockSpec((1, 1, D), lambda i, be, nu: (be[i], 0, 0)),
        ],
        out_specs=pl.BlockSpec((blk, D), lambda i, be, nu: (i, 0)),
        scratch_shapes=[
            pltpu.VMEM((2, blk, D), F32),
            pltpu.VMEM((D, F2), BF16),
            pltpu.VMEM((f, D), BF16),
            pltpu.SemaphoreType.DMA((2,)),
        ],
    )
    return pl.pallas_call(
        functools.partial(_moe_kernel, blk=blk, f=f),
        grid_spec=grid_spec,
        out_shape=jax.ShapeDtypeStruct((nb * blk, D), F32),
        compiler_params=_params(("arbitrary",)),
    )(blk_e, nused, tok3, tok3, h, w_gu, b_gu.reshape(E, 1, F2),
      w_dn, b_dn.reshape(E, 1, D))


def _comb_kernel(pos_cur_ref, pos_nxt_ref, h_ref, p_ref, y_hbm, g_ref, b_ref, o_ref, ybuf, sem,
                 *, tm, alpha):
    i = pl.program_id(0)
    n = pl.num_programs(0)
    slot = lax.rem(i, 2)

    def issue(pos_ref, s):
        for k in range(TOP_K):
            _row_gather_start(y_hbm, lambda r: pos_ref[0, k, r],
                              lambda r: ybuf.at[s, k, pl.ds(r, 1), :], sem.at[s], tm)

    @pl.when(i == 0)
    def _first():
        issue(pos_cur_ref, 0)

    @pl.when(i + 1 < n)
    def _prefetch():
        issue(pos_nxt_ref, 1 - slot)

    for k in range(TOP_K):
        pltpu.make_async_copy(y_hbm.at[pl.ds(0, tm), :], ybuf.at[slot, k], sem.at[slot]).wait()
    gate = p_ref[...]
    fsum = ((ybuf[slot, 0] * gate[:, 0:1] + ybuf[slot, 1] * gate[:, 1:2])
            + (ybuf[slot, 2] * gate[:, 2:3] + ybuf[slot, 3] * gate[:, 3:4]))
    o_ref[...] = _layer_norm(alpha * h_ref[...] + fsum, g_ref[...], b_ref[...])


def _combine(h, ys, pos, top_p, g, b, alpha):
    T, D = h.shape
    tm = COMB_TM
    nt = T // tm
    pos3 = pos.reshape(TOP_K, nt, tm).transpose(1, 0, 2)
    return pl.pallas_call(
        functools.partial(_comb_kernel, tm=tm, alpha=alpha),
        grid=(nt,),
        in_specs=[
            pl.BlockSpec((1, TOP_K, tm), lambda i: (i, 0, 0), memory_space=pltpu.SMEM),
            pl.BlockSpec((1, TOP_K, tm), lambda i: (jnp.minimum(i + 1, nt - 1), 0, 0),
                         memory_space=pltpu.SMEM),
            pl.BlockSpec((tm, D), lambda i: (i, 0)),
            pl.BlockSpec((tm, TOP_K), lambda i: (i, 0)),
            pl.BlockSpec(memory_space=pl.ANY),
            pl.BlockSpec((1, D), lambda i: (0, 0)),
            pl.BlockSpec((1, D), lambda i: (0, 0)),
        ],
        out_specs=pl.BlockSpec((tm, D), lambda i: (i, 0)),
        out_shape=jax.ShapeDtypeStruct((T, D), F32),
        scratch_shapes=[pltpu.VMEM((2, TOP_K, tm, D), F32), pltpu.SemaphoreType.DMA((2,))],
        compiler_params=_params(("arbitrary",)),
    )(pos3, pos3, h, top_p.T, ys, g, b)


def _route(top_e, blk):
    K, T = top_e.shape
    N = K * T
    flat_e = top_e.reshape(N)
    experts = jnp.arange(N_EXPERTS, dtype=I32)
    order = jnp.argsort(flat_e, stable=True).astype(I32)
    inv = jnp.argsort(order).astype(I32)
    onehot = flat_e[:, None] == experts[None, :]
    counts = jnp.sum(onehot, axis=0, dtype=I32)
    padded = (counts + blk - 1) // blk * blk
    pends = jnp.cumsum(padded)
    offs = jnp.cumsum(counts) - counts
    shift = (pends - padded) - offs
    pos = inv + jnp.sum(jnp.where(onehot, shift[None, :], 0), axis=1, dtype=I32)
    P = N + N_EXPERTS * blk
    nb = P // blk
    blk_start = jnp.arange(nb, dtype=I32) * blk
    blk_e = jnp.minimum(jnp.sum(pends[None, :] <= blk_start[:, None], axis=1, dtype=I32), N_EXPERTS - 1)
    j = (blk_start - shift[blk_e])[:, None] + jnp.arange(blk, dtype=I32)[None, :]
    valid = j < (offs + counts)[blk_e][:, None]
    src = order[jnp.clip(j, 0, N - 1)]
    row_tok = jnp.where(valid, src % T, 0).reshape(P)
    nused = (pends[-1:] // blk).astype(I32)
    return blk_e, nused, row_tok, pos.reshape(K, T)


def _projection_weights(w_in_l):
    sizes = (ATT_W, ATT_W, ATT_W, IDX_HEADS * IDX_DIM, IDX_DIM, IDX_HEADS, ATT_W, ATT_W, ATT_W)
    offs = np.concatenate([[0], np.cumsum(sizes)])
    qa, ka, va, qi, ki, wi, qb, kb, vb = (w_in_l[:, offs[n]:offs[n + 1]] for n in range(9))
    pad_wi = jnp.zeros((w_in_l.shape[0], LANES - IDX_HEADS), w_in_l.dtype)
    w_att = jnp.concatenate([qa, ka, qi, qb, kb, vb, ki, ki, wi, pad_wi], axis=1).astype(BF16)
    w_va_t = va.T.astype(BF16)
    w_gate = w_in_l[:, offs[9]:].astype(BF16)
    return w_att, w_va_t, w_gate


def kernel(x, w_in, w_branch_a, w_branch_b, w_out, rel_bias, ln1_g, ln1_b, w_router, b_router,
           w_gate_up, b_gate_up, w_down, b_down, ln2_g, ln2_b):
    B, S, D = x.shape
    depth = w_in.shape[0]
    alpha = (2 * depth) ** 0.25
    T = B * S
    h = x.reshape(T, D)
    for l in range(depth):
        w_att, w_va_t, w_gate = _projection_weights(w_in[l])
        proj = _matmul(h, w_att, BF16, min(T, 1024), PROJ_TN).reshape(B, S, ATT_COLS)
        vt = _matmul_t(h, w_va_t, DSA_BLK).reshape(B, S // DSA_BLK, ATT_W, DSA_BLK)
        gates = _matmul(h, w_gate, F32, min(T, 1024), 1024)
        ya = _dsa(proj, vt, rel_bias).reshape(T, ATT_W)
        yb = _stick_breaking(proj).reshape(T, ATT_W)
        h1, top_e, top_p = _merge(
            h, ya, yb, gates, w_branch_a[l].astype(BF16), w_branch_b[l].astype(BF16),
            w_out[l].astype(BF16), ln1_g[l].reshape(1, D), ln1_b[l].reshape(1, D),
            w_router[l].T, b_router[l].reshape(N_EXPERTS, 1), alpha)
        blk_e, nused, row_tok, pos = _route(top_e, MOE_BLK)
        ys = _moe_ffn(h1, blk_e, nused, row_tok, w_gate_up[l], b_gate_up[l], w_down[l], b_down[l])
        h = _combine(h1, ys, pos, top_p, ln2_g[l].reshape(1, D), ln2_b[l].reshape(1, D), alpha)
    return h.reshape(B, S, D)
```

```python
import functools
import math

import numpy as np
import jax
import jax.numpy as jnp
from jax import lax
from jax.experimental import pallas as pl
from jax.experimental.pallas import tpu as pltpu

F32 = jnp.float32
BF16 = jnp.bfloat16
I32 = jnp.int32

A_HEADS = 8
HEAD_DIM = 64
ATT_W = A_HEADS * HEAD_DIM
IDX_HEADS = 8
IDX_DIM = 64
IDX_SCALE = (IDX_HEADS * IDX_DIM) ** -0.5
TOPK_MAX = 256
N_BUCKETS = 32
MAX_DISTANCE = 128
N_EXPERTS = 32
TOP_K = 4
SWIGLU_LIMIT = 7.0
SWIGLU_ALPHA = 1.702
LN_EPS = 1e-5
QK_SCALE = HEAD_DIM ** -0.5

LANES = 128
SUBLANES = 8
HALF = LANES // 2
N_PAIRS = A_HEADS // 2
VMEM_LIMIT = 56 * 1024 * 1024

DSA_BLK = 256
SB_T = 256
MERGE_TM = 512
MOE_BLK = 512
COMB_TM = 256
BISECT_CAP = 24
NEG = -1e30

COL_QA, COL_KA, COL_QI, COL_QB, COL_KB, COL_VB = (g * ATT_W for g in range(6))
COL_KK = 6 * ATT_W
COL_WI = COL_KK + LANES
ATT_COLS = COL_WI + LANES
PROJ_TN = ATT_COLS // 2

NT_DIMS = (((1,), (1,)), ((), ()))


def _params(sem, vmem=VMEM_LIMIT):
    return pltpu.CompilerParams(dimension_semantics=sem, vmem_limit_bytes=vmem)


def _mm_kernel(x_ref, w_ref, o_ref, xb_ref):
    @pl.when(pl.program_id(1) == 0)
    def _cast():
        xb_ref[...] = x_ref[...].astype(BF16)

    o_ref[...] = jnp.dot(xb_ref[...], w_ref[...], preferred_element_type=F32).astype(o_ref.dtype)


def _matmul(x, w, out_dtype, tm, tn):
    M, K = x.shape
    N = w.shape[1]
    return pl.pallas_call(
        _mm_kernel,
        grid=(M // tm, N // tn),
        in_specs=[pl.BlockSpec((tm, K), lambda i, j: (i, 0)),
                  pl.BlockSpec((K, tn), lambda i, j: (0, j))],
        out_specs=pl.BlockSpec((tm, tn), lambda i, j: (i, j)),
        out_shape=jax.ShapeDtypeStruct((M, N), out_dtype),
        scratch_shapes=[pltpu.VMEM((tm, K), BF16)],
        compiler_params=_params(("arbitrary", "arbitrary")),
    )(x, w)


def _mm_t_kernel(x_ref, w_ref, o_ref):
    o_ref[0] = lax.dot_general(w_ref[...], x_ref[...].astype(BF16), NT_DIMS,
                               preferred_element_type=F32).astype(o_ref.dtype)


def _matmul_t(x, w_t, tm):
    M, K = x.shape
    N = w_t.shape[0]
    return pl.pallas_call(
        _mm_t_kernel,
        grid=(M // tm,),
        in_specs=[pl.BlockSpec((tm, K), lambda i: (i, 0)),
                  pl.BlockSpec((N, K), lambda i: (0, 0))],
        out_specs=pl.BlockSpec((1, N, tm), lambda i: (i, 0, 0)),
        out_shape=jax.ShapeDtypeStruct((M // tm, N, tm), BF16),
        compiler_params=_params(("arbitrary",)),
    )(x, w_t)


def _t5_bucket_np(n):
    n = np.maximum(n, 0)
    max_exact = N_BUCKETS // 2
    nf = np.maximum(n, 1).astype(np.float32)
    large = max_exact + (np.log(nf / max_exact) / math.log(MAX_DISTANCE / max_exact)
                         * (N_BUCKETS - max_exact)).astype(np.int32)
    large = np.minimum(large, N_BUCKETS - 1)
    return np.where(n < max_exact, n, large).astype(np.int32)


def _dsa_n_off(blk):
    return 2 + -(-MAX_DISTANCE // blk)


def _dsa_bucket_tiles(blk):
    n_off = _dsa_n_off(blk)
    j = np.arange(blk)[None, :, None]
    i = np.arange(blk)[None, None, :]
    o = np.arange(n_off)[:, None, None]
    return _t5_bucket_np(i - j + blk * (n_off - 1 - o))


def _dsa_kernel(bucket_ref, relb_ref, q_ref, k_ref, vt_ref, qi_ref, kk_ref, wi_ref, o_ref,
                sc_ref, bias_ref, m_ref, l_ref, acc_ref, st_ref, mx_ref,
                *, blk, n_chunks, n_sel, n_off, idx_bits):
    b = pl.program_id(0)
    i = pl.program_id(1)
    q0 = i * blk
    nck = i + 1
    groups = blk // SUBLANES

    @pl.when(jnp.logical_and(b == 0, i == 0))
    def _build_bias():
        def head_body(h, _):
            for o in range(n_off):
                for rb in range(blk // LANES):
                    for cb in range(blk // LANES):
                        rs = slice(rb * LANES, (rb + 1) * LANES)
                        cs = slice(cb * LANES, (cb + 1) * LANES)
                        bk = bucket_ref[o, rs, cs]

                        def bucket_body(n, acc):
                            return jnp.where(bk == n, relb_ref[n, h], acc)

                        bias_ref[h, o, rs, cs] = lax.fori_loop(
                            0, N_BUCKETS, bucket_body, jnp.zeros((LANES, LANES), F32))
            return 0

        lax.fori_loop(0, A_HEADS, head_body, 0)

    lane = lax.broadcasted_iota(I32, (blk, LANES), 1)
    lo_half = lane < HALF
    krow = lax.broadcasted_iota(I32, (blk, blk), 0)
    qpos = q0 + lax.broadcasted_iota(I32, (1, blk), 1)

    def pair_split(ref, scale):
        out = []
        for p in range(N_PAIRS):
            v = ref[0, :, p * LANES:(p + 1) * LANES].astype(F32)
            if scale != 1.0:
                v = v * scale
            out.append(jnp.where(lo_half, v, 0.0).astype(BF16))
            out.append(jnp.where(lo_half, 0.0, v).astype(BF16))
        return out

    wi_t = wi_ref[0].astype(F32).T
    wrow = [wi_t[h:h + 1, :] * IDX_SCALE for h in range(IDX_HEADS)]
    qi_m = pair_split(qi_ref, 1.0)

    def score_chunk(c, _):
        c0 = pl.multiple_of(c * blk, blk)
        kk = kk_ref[0, pl.ds(c0, blk), :]
        acc = jnp.zeros((blk, blk), F32)
        for h in range(IDX_HEADS):
            s = lax.dot_general(kk, qi_m[h], NT_DIMS, preferred_element_type=F32)
            acc = acc + wrow[h] * jnp.maximum(s, 0.0)
        sc_ref[c] = jnp.where(c0 + krow <= qpos, acc, -jnp.inf)
        return 0

    lax.fori_loop(0, nck, score_chunk, 0)

    kt = jnp.minimum(qpos + 1, n_sel).astype(F32)

    def fold(fn, init):
        def body(c, acc):
            return fn(acc, sc_ref[c], c * blk + krow)
        return lax.fori_loop(0, nck, body, init)

    def part(x, op):
        return op(x.reshape(groups, SUBLANES, blk), axis=0)

    def fin(x, op):
        return op(x, axis=0, keepdims=True)

    zeros8 = jnp.zeros((SUBLANES, blk), F32)
    pinf8 = jnp.full((SUBLANES, blk), jnp.inf, F32)

    def count_ge(th):
        return fin(fold(lambda a, s, idx: a + part(jnp.where(s >= th, 1.0, 0.0), jnp.sum), zeros8),
                   jnp.sum)

    mn, mx = fold(lambda a, s, idx: (
        jnp.minimum(a[0], part(jnp.where(s == -jnp.inf, jnp.inf, s), jnp.min)),
        jnp.maximum(a[1], part(s, jnp.max))), (pinf8, -pinf8))
    rmin = fin(mn, jnp.min)
    rmax = fin(mx, jnp.max)

    def bis_cond(st):
        it, lo, hi, clo = st
        return jnp.logical_and(it < BISECT_CAP, jnp.max(jnp.abs(clo - kt)) > 0.0)

    def bis_body(st):
        it, lo, hi, clo = st
        mid = 0.5 * lo + 0.5 * hi
        c = count_ge(mid)
        active = clo != kt
        up = jnp.logical_and(active, c >= kt)
        dn = jnp.logical_and(active, c < kt)
        return (it + 1, jnp.where(up, mid, lo), jnp.where(dn, mid, hi), jnp.where(up, c, clo))

    _, lo, _, _ = lax.while_loop(
        bis_cond, bis_body, (jnp.int32(0), rmin, rmax + 1.0, (qpos + 1).astype(F32)))

    def stats(lo_):
        a_ = fin(fold(lambda a, s, idx: jnp.minimum(a, part(jnp.where(s >= lo_, s, jnp.inf), jnp.min)),
                      pinf8), jnp.min)
        cg, ct, nx = fold(
            lambda a, s, idx: (a[0] + part(jnp.where(s > a_, 1.0, 0.0), jnp.sum),
                               a[1] + part(jnp.where(s == a_, 1.0, 0.0), jnp.sum),
                               jnp.minimum(a[2], part(jnp.where(s > a_, s, jnp.inf), jnp.min))),
            (zeros8, zeros8, pinf8))
        return a_, fin(cg, jnp.sum), fin(ct, jnp.sum), fin(nx, jnp.min)

    def fin_cond(st):
        return st[0]

    def fin_body(st):
        _, lo_, _, _ = st
        a_, cgt_, nt_, nxt_ = stats(lo_)
        bad = cgt_ >= kt
        return (jnp.max(jnp.where(bad, 1.0, 0.0)) > 0.0, jnp.where(bad, nxt_, a_), cgt_, nt_)

    _, a, cgt, nties = lax.while_loop(fin_cond, fin_body, (jnp.bool_(True), lo, kt, kt))
    need = kt - cgt
    excess = jnp.max(jnp.where(nties > need, 1.0, 0.0)) > 0.0

    def tie_search():
        ans = jnp.zeros((1, blk), I32)
        for bit in reversed(range(idx_bits)):
            cand = ans + (1 << bit)
            cnt = fin(fold(lambda acc, s, idx: acc + part(jnp.where(
                jnp.logical_and(s == a, idx < cand), 1.0, 0.0), jnp.sum), zeros8), jnp.sum)
            ans = jnp.where(cnt < need, cand, ans)
        return ans

    jstar = lax.cond(excess, tie_search, lambda: jnp.full((1, blk), (1 << idx_bits) - 1, I32))

    def mask_chunk(c, _):
        s = sc_ref[c]
        idx = c * blk + krow
        sel = jnp.logical_or(s > a, jnp.logical_and(s == a, idx <= jstar))
        sc_ref[c] = jnp.where(sel, 0.0, NEG)
        return 0

    lax.fori_loop(0, nck, mask_chunk, 0)

    odd = lax.rem(nck, 2)
    npair = (nck + odd) // 2

    @pl.when(odd == 1)
    def _mask_extra_chunk():
        sc_ref[nck] = jnp.full((blk, blk), NEG, F32)

    m_ref[...] = jnp.full(m_ref.shape, NEG, F32)
    l_ref[...] = jnp.zeros(l_ref.shape, F32)
    acc_ref[...] = jnp.zeros(acc_ref.shape, F32)
    q_m = pair_split(q_ref, QK_SCALE)

    def stage_logits(c, slot):
        c = jnp.minimum(c, n_chunks - 1)
        c0 = pl.multiple_of(c * blk, blk)
        madd = sc_ref[c]
        o_idx = jnp.clip(c - i + (n_off - 1), 0, n_off - 1)
        for p in range(N_PAIRS):
            k2 = k_ref[0, pl.ds(c0, blk), p * LANES:(p + 1) * LANES]
            for hh in range(2):
                h = 2 * p + hh
                s = lax.dot_general(k2, q_m[h], NT_DIMS, preferred_element_type=F32)
                s = s + bias_ref[h, o_idx] + madd
                st_ref[slot, h] = s
                mx_ref[slot, h] = jnp.max(s, axis=0, keepdims=True)

    def stage_values(c, slot):
        for p in range(N_PAIRS):
            vt2 = vt_ref[0, c, p * LANES:(p + 1) * LANES, :]
            for hh in range(2):
                h = 2 * p + hh
                m_old = m_ref[h]
                m_new = jnp.maximum(m_old, mx_ref[slot, h])
                alpha = jnp.exp(m_old - m_new)
                pexp = jnp.exp(st_ref[slot, h] - m_new)
                l_ref[h] = alpha * l_ref[h] + jnp.sum(pexp, axis=0, keepdims=True)
                acc_ref[h] = alpha * acc_ref[h] + jnp.dot(vt2, pexp.astype(BF16),
                                                          preferred_element_type=F32)
                m_ref[h] = m_new

    stage_logits(0, 0)

    def att_pair(pp, _):
        c = 2 * pp
        stage_logits(c + 1, 1)
        stage_values(c, 0)
        stage_logits(c + 2, 0)
        stage_values(c + 1, 1)
        return 0

    lax.fori_loop(0, npair, att_pair, 0)

    lo_rows = lax.broadcasted_iota(I32, (LANES, blk), 0) < HALF
    for p in range(N_PAIRS):
        oa = acc_ref[2 * p] / l_ref[2 * p]
        ob = acc_ref[2 * p + 1] / l_ref[2 * p + 1]
        o_ref[0, :, p * LANES:(p + 1) * LANES] = jnp.where(lo_rows, oa, ob).T.astype(o_ref.dtype)


def _dsa(proj3, vt4, rel_bias):
    B, S, _ = proj3.shape
    blk = DSA_BLK
    n_off = _dsa_n_off(blk)
    n_sel = min(TOPK_MAX, S // 4)
    idx_bits = max(1, int(math.ceil(math.log2(S))))
    bucket = jnp.asarray(_dsa_bucket_tiles(blk))
    n_chunks = S // blk
    assert S % blk == 0 and n_chunks % 2 == 0, "DSA consumes key chunks in pairs"
    kern = functools.partial(_dsa_kernel, blk=blk, n_chunks=n_chunks, n_sel=n_sel, n_off=n_off,
                             idx_bits=idx_bits)
    return pl.pallas_call(
        kern,
        grid=(B, S // blk),
        in_specs=[
            pl.BlockSpec((n_off, blk, blk), lambda b, i: (0, 0, 0)),
            pl.BlockSpec(memory_space=pltpu.SMEM),
            pl.BlockSpec((1, blk, ATT_W), lambda b, i: (b, i, COL_QA // ATT_W)),
            pl.BlockSpec((1, S, ATT_W), lambda b, i: (b, 0, COL_KA // ATT_W)),
            pl.BlockSpec((1, S // blk, ATT_W, blk), lambda b, i: (b, 0, 0, 0)),
            pl.BlockSpec((1, blk, ATT_W), lambda b, i: (b, i, COL_QI // ATT_W)),
            pl.BlockSpec((1, S, LANES), lambda b, i: (b, 0, COL_KK // LANES)),
            pl.BlockSpec((1, blk, LANES), lambda b, i: (b, i, COL_WI // LANES)),
        ],
        out_specs=pl.BlockSpec((1, blk, ATT_W), lambda b, i: (b, i, 0)),
        out_shape=jax.ShapeDtypeStruct((B, S, ATT_W), BF16),
        scratch_shapes=[
            pltpu.VMEM((S // blk, blk, blk), F32),
            pltpu.VMEM((A_HEADS, n_off, blk, blk), F32),
            pltpu.VMEM((A_HEADS, 1, blk), F32),
            pltpu.VMEM((A_HEADS, 1, blk), F32),
            pltpu.VMEM((A_HEADS, LANES, blk), F32),
            pltpu.VMEM((2, A_HEADS, blk, blk), F32),
            pltpu.VMEM((2, A_HEADS, 1, blk), F32),
        ],
        compiler_params=_params(("arbitrary", "arbitrary")),
    )(bucket, rel_bias, proj3, proj3, vt4, proj3, proj3, proj3)


def _sb_kernel(q_ref, k_ref, v_ref, o_ref, hl_ref, ls_ref, rs_ref, *, t, n_chunks):
    i = pl.program_id(2)
    n = i + 1
    odd = lax.rem(n, 2)
    top = i + odd
    npair = (n + odd) // 2
    lane = lax.broadcasted_iota(I32, (t, LANES), 1)
    lo_half = lane < HALF
    q2 = q_ref[0].astype(F32) * QK_SCALE
    q_m = (jnp.where(lo_half, q2, 0.0).astype(BF16), jnp.where(lo_half, 0.0, q2).astype(BF16))
    r = lax.broadcasted_iota(I32, (t, t), 0)
    cidx = lax.broadcasted_iota(I32, (t, t), 1)
    later = (r > cidx).astype(BF16)
    diff = cidx - r

    def stage_terms(step, slot, masked):
        chunk = top - step
        c0 = pl.multiple_of(jnp.clip(chunk, 0, n_chunks - 1) * t, t)
        k2 = k_ref[0, pl.ds(c0, t), :]
        if masked:
            keep = diff < (i - chunk) * t
        for hh in range(2):
            z = lax.dot_general(q_m[hh], k2, NT_DIMS, preferred_element_type=F32)
            sp = jnp.maximum(z, 0.0) + jnp.log(1.0 + jnp.exp(-jnp.abs(z)))
            lm = -sp
            ls = z - sp
            if masked:
                lm = jnp.where(keep, lm, 0.0)
                ls = jnp.where(keep, ls, NEG)
            hi = lm.astype(BF16)
            hl_ref[slot, 2 * hh] = hi
            hl_ref[slot, 2 * hh + 1] = (lm - hi.astype(F32)).astype(BF16)
            ls_ref[slot, hh] = ls
            rs_ref[slot, hh] = jnp.sum(lm, axis=1, keepdims=True)

    def stage_apply(step, slot, carry):
        c0 = pl.multiple_of(jnp.minimum(top - step, n_chunks - 1) * t, t)
        v2 = v_ref[0, pl.ds(c0, t), :]
        after4 = jnp.dot(hl_ref[slot].reshape(4 * t, t), later, preferred_element_type=F32)
        out = []
        for hh in range(2):
            car, acc = carry[hh]
            after = after4[(2 * hh) * t:(2 * hh + 1) * t] + after4[(2 * hh + 1) * t:(2 * hh + 2) * t]
            w = jnp.exp(ls_ref[slot, hh] + after + car)
            acc = acc + jnp.dot(w.astype(BF16), v2, preferred_element_type=F32)
            out.append((car + rs_ref[slot, hh], acc))
        return tuple(out)

    z1 = jnp.zeros((t, 1), F32)
    za = jnp.zeros((t, LANES), F32)
    stage_terms(0, 0, True)
    stage_terms(1, 1, True)
    carry = stage_apply(0, 0, ((z1, za), (z1, za)))
    stage_terms(2, 0, False)
    carry = stage_apply(1, 1, carry)

    def pair_body(pp, carry):
        step = 2 * pp
        stage_terms(step + 1, 1, False)
        carry = stage_apply(step, 0, carry)
        stage_terms(step + 2, 0, False)
        return stage_apply(step + 1, 1, carry)

    (_, acc_a), (_, acc_b) = lax.fori_loop(1, npair, pair_body, carry)
    o_ref[0] = jnp.where(lo_half, acc_a, acc_b).astype(o_ref.dtype)


def _stick_breaking(proj3):
    B, S, _ = proj3.shape
    t = SB_T
    qb, kb, vb = COL_QB // LANES, COL_KB // LANES, COL_VB // LANES
    return pl.pallas_call(
        functools.partial(_sb_kernel, t=t, n_chunks=S // t),
        grid=(B, N_PAIRS, S // t),
        in_specs=[
            pl.BlockSpec((1, t, LANES), lambda b, p, i: (b, i, qb + p)),
            pl.BlockSpec((1, S, LANES), lambda b, p, i: (b, 0, kb + p)),
            pl.BlockSpec((1, S, LANES), lambda b, p, i: (b, 0, vb + p)),
        ],
        out_specs=pl.BlockSpec((1, t, LANES), lambda b, p, i: (b, i, p)),
        out_shape=jax.ShapeDtypeStruct((B, S, ATT_W), BF16),
        scratch_shapes=[
            pltpu.VMEM((2, 4, t, t), BF16),
            pltpu.VMEM((2, 2, t, t), F32),
            pltpu.VMEM((2, 2, t, 1), F32),
        ],
        compiler_params=_params(("arbitrary", "arbitrary", "arbitrary")),
    )(proj3, proj3, proj3)


def _layer_norm(r, g, b):
    mu = jnp.mean(r, axis=-1, keepdims=True)
    d = r - mu
    var = jnp.mean(d * d, axis=-1, keepdims=True)
    return d * lax.rsqrt(var + LN_EPS) * g + b


def _split_bf16(v):
    hi = v.astype(BF16)
    return hi, (v - hi.astype(F32)).astype(BF16)


def _merge_kernel(x_ref, ya_ref, yb_ref, gate_ref, wa_ref, wb_ref, wo_ref, g_ref, b_ref,
                  wr_ref, br_ref, h_ref, e_ref, p_ref, *, alpha, d):
    pa = jnp.dot(ya_ref[...], wa_ref[...], preferred_element_type=F32)
    pb = jnp.dot(yb_ref[...], wb_ref[...], preferred_element_type=F32)
    merged = jax.nn.sigmoid(gate_ref[:, :d]) * pa + jax.nn.sigmoid(gate_ref[:, d:]) * pb
    m = jnp.dot(merged.astype(BF16), wo_ref[...], preferred_element_type=F32)
    h = _layer_norm(alpha * x_ref[...] + m, g_ref[...], b_ref[...])
    h_ref[...] = h

    h_hi, h_lo = _split_bf16(h)
    w_hi, w_lo = _split_bf16(wr_ref[...])
    logit = (lax.dot_general(w_hi, h_hi, NT_DIMS, preferred_element_type=F32)
             + lax.dot_general(w_hi, h_lo, NT_DIMS, preferred_element_type=F32)
             + lax.dot_general(w_lo, h_hi, NT_DIMS, preferred_element_type=F32)) + br_ref[...]
    eid = lax.broadcasted_iota(I32, logit.shape, 0)
    vals, ids = [], []
    for _ in range(TOP_K):
        mx = jnp.max(logit, axis=0, keepdims=True)
        am = jnp.min(jnp.where(logit == mx, eid, N_EXPERTS), axis=0, keepdims=True)
        vals.append(mx)
        ids.append(am)
        logit = jnp.where(eid == am, -jnp.inf, logit)
    ex = [jnp.exp(v - vals[0]) for v in vals]
    den = ex[0] + ex[1] + ex[2] + ex[3]
    for k in range(TOP_K):
        e_ref[k:k + 1, :] = ids[k]
        p_ref[k:k + 1, :] = ex[k] / den


def _merge(x2, ya, yb, gates, wa, wb, wo, g, b, wr_t, br, alpha):
    T, D = x2.shape
    tm = MERGE_TM
    row = lambda i: (i, 0)
    fixed = lambda i: (0, 0)
    return pl.pallas_call(
        functools.partial(_merge_kernel, alpha=alpha, d=D),
        grid=(T // tm,),
        in_specs=[
            pl.BlockSpec((tm, D), row),
            pl.BlockSpec((tm, ATT_W), row),
            pl.BlockSpec((tm, ATT_W), row),
            pl.BlockSpec((tm, 2 * D), row),
            pl.BlockSpec((ATT_W, D), fixed),
            pl.BlockSpec((ATT_W, D), fixed),
            pl.BlockSpec((D, D), fixed),
            pl.BlockSpec((1, D), fixed),
            pl.BlockSpec((1, D), fixed),
            pl.BlockSpec((N_EXPERTS, D), fixed),
            pl.BlockSpec((N_EXPERTS, 1), fixed),
        ],
        out_specs=[
            pl.BlockSpec((tm, D), row),
            pl.BlockSpec((TOP_K, tm), lambda i: (0, i)),
            pl.BlockSpec((TOP_K, tm), lambda i: (0, i)),
        ],
        out_shape=[
            jax.ShapeDtypeStruct((T, D), F32),
            jax.ShapeDtypeStruct((TOP_K, T), I32),
            jax.ShapeDtypeStruct((TOP_K, T), F32),
        ],
        compiler_params=_params(("arbitrary",)),
    )(x2, ya, yb, gates, wa, wb, wo, g, b, wr_t, br)


def _row_gather_start(src_hbm, idx_of, dst_of, sem, n):
    def body(r, _):
        pltpu.make_async_copy(src_hbm.at[pl.ds(idx_of(r), 1), :], dst_of(r), sem).start()
        return 0
    lax.fori_loop(0, n, body, 0, unroll=8)


def _moe_kernel(blk_e_ref, nused_ref, tok_cur_ref, tok_nxt_ref, h_hbm, wgu_ref, bgu_ref,
                wdn_ref, bdn_ref, o_ref, xbuf, wgu_s, wdn_s, sem, *, blk, f):
    i = pl.program_id(0)
    nused = nused_ref[0]
    slot = lax.rem(i, 2)

    def issue(tok_ref, s):
        _row_gather_start(h_hbm, lambda r: tok_ref[0, 0, r],
                          lambda r: xbuf.at[s, pl.ds(r, 1), :], sem.at[s], blk)

    @pl.when(i == 0)
    def _first():
        issue(tok_cur_ref, 0)

    @pl.when(i + 1 < nused)
    def _prefetch():
        issue(tok_nxt_ref, 1 - slot)

    @pl.when(i < nused)
    def _compute():
        pltpu.make_async_copy(h_hbm.at[pl.ds(0, blk), :], xbuf.at[slot], sem.at[slot]).wait()
        changed = jnp.logical_or(i == 0, blk_e_ref[i] != blk_e_ref[jnp.maximum(i - 1, 0)])

        @pl.when(changed)
        def _cast_weights():
            wgu_s[...] = wgu_ref[0].astype(BF16)
            wdn_s[...] = wdn_ref[0].astype(BF16)

        x = xbuf[slot].astype(BF16)
        hgu = jnp.dot(x, wgu_s[...], preferred_element_type=F32) + bgu_ref[0]
        a = jnp.minimum(hgu[:, :f], SWIGLU_LIMIT)
        u = jnp.clip(hgu[:, f:], -SWIGLU_LIMIT, SWIGLU_LIMIT)
        glu = a * jax.nn.sigmoid(a * SWIGLU_ALPHA)
        o_ref[...] = jnp.dot(((u + 1.0) * glu).astype(BF16), wdn_s[...],
                             preferred_element_type=F32) + bdn_ref[0]

    @pl.when(i >= nused)
    def _unused_block():
        o_ref[...] = jnp.zeros(o_ref.shape, o_ref.dtype)


def _moe_ffn(h, blk_e, nused, row_tok, w_gu, b_gu, w_dn, b_dn):
    T, D = h.shape
    E, _, F2 = w_gu.shape
    f = F2 // 2
    blk = MOE_BLK
    nb = row_tok.shape[0] // blk
    tok3 = row_tok.reshape(nb, 1, blk)
    grid_spec = pltpu.PrefetchScalarGridSpec(
        num_scalar_prefetch=2,
        grid=(nb,),
        in_specs=[
            pl.BlockSpec((1, 1, blk), lambda i, be, nu: (i, 0, 0), memory_space=pltpu.SMEM),
            pl.BlockSpec((1, 1, blk), lambda i, be, nu: (jnp.minimum(i + 1, nb - 1), 0, 0),
                         memory_space=pltpu.SMEM),
            pl.BlockSpec(memory_space=pl.ANY),
            pl.BlockSpec((1, D, F2), lambda i, be, nu: (be[i], 0, 0)),
            pl.BlockSpec((1, 1, F2), lambda i, be, nu: (be[i], 0, 0)),
            pl.BlockSpec((1, f, D), lambda i, be, nu: (be[i], 0, 0)),
            pl.BlockSpec((1, 1, D), lambda i, be, nu: (be[i], 0, 0)),
        ],
        out_specs=pl.BlockSpec((blk, D), lambda i, be, nu: (i, 0)),
        scratch_shapes=[
            pltpu.VMEM((2, blk, D), F32),
            pltpu.VMEM((D, F2), BF16),
            pltpu.VMEM((f, D), BF16),
            pltpu.SemaphoreType.DMA((2,)),
        ],
    )
    return pl.pallas_call(
        functools.partial(_moe_kernel, blk=blk, f=f),
        grid_spec=grid_spec,
        out_shape=jax.ShapeDtypeStruct((nb * blk, D), F32),
        compiler_params=_params(("arbitrary",)),
    )(blk_e, nused, tok3, tok3, h, w_gu, b_gu.reshape(E, 1, F2),
      w_dn, b_dn.reshape(E, 1, D))


def _comb_kernel(pos_cur_ref, pos_nxt_ref, h_ref, p_ref, y_hbm, g_ref, b_ref, o_ref, ybuf, sem,
                 *, tm, alpha):
    i = pl.program_id(0)
    n = pl.num_programs(0)
    slot = lax.rem(i, 2)

    def issue(pos_ref, s):
        for k in range(TOP_K):
            _row_gather_start(y_hbm, lambda r: pos_ref[0, k, r],
                              lambda r: ybuf.at[s, k, pl.ds(r, 1), :], sem.at[s], tm)

    @pl.when(i == 0)
    def _first():
        issue(pos_cur_ref, 0)

    @pl.when(i + 1 < n)
    def _prefetch():
        issue(pos_nxt_ref, 1 - slot)

    for k in range(TOP_K):
        pltpu.make_async_copy(y_hbm.at[pl.ds(0, tm), :], ybuf.at[slot, k], sem.at[slot]).wait()
    gate = p_ref[...]
    fsum = ((ybuf[slot, 0] * gate[:, 0:1] + ybuf[slot, 1] * gate[:, 1:2])
            + (ybuf[slot, 2] * gate[:, 2:3] + ybuf[slot, 3] * gate[:, 3:4]))
    o_ref[...] = _layer_norm(alpha * h_ref[...] + fsum, g_ref[...], b_ref[...])


def _combine(h, ys, pos, top_p, g, b, alpha):
    T, D = h.shape
    tm = COMB_TM
    nt = T // tm
    pos3 = pos.reshape(TOP_K, nt, tm).transpose(1, 0, 2)
    return pl.pallas_call(
        functools.partial(_comb_kernel, tm=tm, alpha=alpha),
        grid=(nt,),
        in_specs=[
            pl.BlockSpec((1, TOP_K, tm), lambda i: (i, 0, 0), memory_space=pltpu.SMEM),
            pl.BlockSpec((1, TOP_K, tm), lambda i: (jnp.minimum(i + 1, nt - 1), 0, 0),
                         memory_space=pltpu.SMEM),
            pl.BlockSpec((tm, D), lambda i: (i, 0)),
            pl.BlockSpec((tm, TOP_K), lambda i: (i, 0)),
            pl.BlockSpec(memory_space=pl.ANY),
            pl.BlockSpec((1, D), lambda i: (0, 0)),
            pl.BlockSpec((1, D), lambda i: (0, 0)),
        ],
        out_specs=pl.BlockSpec((tm, D), lambda i: (i, 0)),
        out_shape=jax.ShapeDtypeStruct((T, D), F32),
        scratch_shapes=[pltpu.VMEM((2, TOP_K, tm, D), F32), pltpu.SemaphoreType.DMA((2,))],
        compiler_params=_params(("arbitrary",)),
    )(pos3, pos3, h, top_p.T, ys, g, b)


def _route(top_e, blk):
    K, T = top_e.shape
    N = K * T
    flat_e = top_e.reshape(N)
    experts = jnp.arange(N_EXPERTS, dtype=I32)
    order = jnp.argsort(flat_e, stable=True).astype(I32)
    inv = jnp.argsort(order).astype(I32)
    onehot = flat_e[:, None] == experts[None, :]
    counts = jnp.sum(onehot, axis=0, dtype=I32)
    padded = (counts + blk - 1) // blk * blk
    pends = jnp.cumsum(padded)
    offs = jnp.cumsum(counts) - counts
    shift = (pends - padded) - offs
    pos = inv + jnp.sum(jnp.where(onehot, shift[None, :], 0), axis=1, dtype=I32)
    P = N + N_EXPERTS * blk
    nb = P // blk
    blk_start = jnp.arange(nb, dtype=I32) * blk
    blk_e = jnp.minimum(jnp.sum(pends[None, :] <= blk_start[:, None], axis=1, dtype=I32), N_EXPERTS - 1)
    j = (blk_start - shift[blk_e])[:, None] + jnp.arange(blk, dtype=I32)[None, :]
    valid = j < (offs + counts)[blk_e][:, None]
    src = order[jnp.clip(j, 0, N - 1)]
    row_tok = jnp.where(valid, src % T, 0).reshape(P)
    nused = (pends[-1:] // blk).astype(I32)
    return blk_e, nused, row_tok, pos.reshape(K, T)


def _projection_weights(w_in_l):
    sizes = (ATT_W, ATT_W, ATT_W, IDX_HEADS * IDX_DIM, IDX_DIM, IDX_HEADS, ATT_W, ATT_W, ATT_W)
    offs = np.concatenate([[0], np.cumsum(sizes)])
    qa, ka, va, qi, ki, wi, qb, kb, vb = (w_in_l[:, offs[n]:offs[n + 1]] for n in range(9))
    pad_wi = jnp.zeros((w_in_l.shape[0], LANES - IDX_HEADS), w_in_l.dtype)
    w_att = jnp.concatenate([qa, ka, qi, qb, kb, vb, ki, ki, wi, pad_wi], axis=1).astype(BF16)
    w_va_t = va.T.astype(BF16)
    w_gate = w_in_l[:, offs[9]:].astype(BF16)
    return w_att, w_va_t, w_gate


def kernel(x, w_in, w_branch_a, w_branch_b, w_out, rel_bias, ln1_g, ln1_b, w_router, b_router,
           w_gate_up, b_gate_up, w_down, b_down, ln2_g, ln2_b):
    B, S, D = x.shape
    depth = w_in.shape[0]
    alpha = (2 * depth) ** 0.25
    T = B * S
    h = x.reshape(T, D)
    for l in range(depth):
        w_att, w_va_t, w_gate = _projection_weights(w_in[l])
        proj = _matmul(h, w_att, BF16, min(T, 1024), PROJ_TN).reshape(B, S, ATT_COLS)
        vt = _matmul_t(h, w_va_t, DSA_BLK).reshape(B, S // DSA_BLK, ATT_W, DSA_BLK)
        gates = _matmul(h, w_gate, F32, min(T, 1024), 1024)
        ya = _dsa(proj, vt, rel_bias).reshape(T, ATT_W)
        yb = _stick_breaking(proj).reshape(T, ATT_W)
        h1, top_e, top_p = _merge(
            h, ya, yb, gates, w_branch_a[l].astype(BF16), w_branch_b[l].astype(BF16),
            w_out[l].astype(BF16), ln1_g[l].reshape(1, D), ln1_b[l].reshape(1, D),
            w_router[l].T, b_router[l].reshape(N_EXPERTS, 1), alpha)
        blk_e, nused, row_tok, pos = _route(top_e, MOE_BLK)
        ys = _moe_ffn(h1, blk_e, nused, row_tok, w_gate_up[l], b_gate_up[l], w_down[l], b_down[l])
        h = _combine(h1, ys, pos, top_p, ln2_g[l].reshape(1, D), ln2_b[l].reshape(1, D), alpha)
    return h.reshape(B, S, D)
```

```python
import functools
import math

import numpy as np
import jax
import jax.numpy as jnp
from jax import lax
from jax.experimental import pallas as pl
from jax.experimental.pallas import tpu as pltpu
from jax.experimental.pallas import tpu_sc as plsc

F32 = jnp.float32
BF16 = jnp.bfloat16
I32 = jnp.int32

A_HEADS = 8
HEAD_DIM = 64
ATT_W = A_HEADS * HEAD_DIM
IDX_HEADS = 8
IDX_DIM = 64
IDX_SCALE = (IDX_HEADS * IDX_DIM) ** -0.5
TOPK_MAX = 256
N_BUCKETS = 32
MAX_DISTANCE = 128
N_EXPERTS = 32
TOP_K = 4
SWIGLU_LIMIT = 7.0
SWIGLU_ALPHA = 1.702
LN_EPS = 1e-5
QK_SCALE = HEAD_DIM ** -0.5

LANES = 128
SUBLANES = 8
HALF = LANES // 2
N_PAIRS = A_HEADS // 2
VMEM_LIMIT = 56 * 1024 * 1024

DSA_BLK = 256
SB_T = 256
MERGE_TM = 512
MOE_BLK = 512
COMB_TM = 256
SC_GATHER_ROWS = 32
BISECT_CAP = 24
NEG = -1e30

COL_QA, COL_KA, COL_QI, COL_QB, COL_KB, COL_VB = (g * ATT_W for g in range(6))
COL_KK = 6 * ATT_W
COL_WI = COL_KK + LANES
ATT_COLS = COL_WI + LANES
PROJ_TN = ATT_COLS // 2

NT_DIMS = (((1,), (1,)), ((), ()))


def _params(sem, vmem=VMEM_LIMIT):
    return pltpu.CompilerParams(dimension_semantics=sem, vmem_limit_bytes=vmem)


def _mm_kernel(x_ref, w_ref, o_ref, xb_ref):
    @pl.when(pl.program_id(1) == 0)
    def _cast():
        xb_ref[...] = x_ref[...].astype(BF16)

    o_ref[...] = jnp.dot(xb_ref[...], w_ref[...], preferred_element_type=F32).astype(o_ref.dtype)


def _matmul(x, w, out_dtype, tm, tn):
    M, K = x.shape
    N = w.shape[1]
    return pl.pallas_call(
        _mm_kernel,
        grid=(M // tm, N // tn),
        in_specs=[pl.BlockSpec((tm, K), lambda i, j: (i, 0)),
                  pl.BlockSpec((K, tn), lambda i, j: (0, j))],
        out_specs=pl.BlockSpec((tm, tn), lambda i, j: (i, j)),
        out_shape=jax.ShapeDtypeStruct((M, N), out_dtype),
        scratch_shapes=[pltpu.VMEM((tm, K), BF16)],
        compiler_params=_params(("arbitrary", "arbitrary")),
    )(x, w)


def _mm_t_kernel(x_ref, w_ref, o_ref):
    o_ref[0] = lax.dot_general(w_ref[...], x_ref[...].astype(BF16), NT_DIMS,
                               preferred_element_type=F32).astype(o_ref.dtype)


def _matmul_t(x, w_t, tm):
    M, K = x.shape
    N = w_t.shape[0]
    return pl.pallas_call(
        _mm_t_kernel,
        grid=(M // tm,),
        in_specs=[pl.BlockSpec((tm, K), lambda i: (i, 0)),
                  pl.BlockSpec((N, K), lambda i: (0, 0))],
        out_specs=pl.BlockSpec((1, N, tm), lambda i: (i, 0, 0)),
        out_shape=jax.ShapeDtypeStruct((M // tm, N, tm), BF16),
        compiler_params=_params(("arbitrary",)),
    )(x, w_t)


def _t5_bucket_np(n):
    n = np.maximum(n, 0)
    max_exact = N_BUCKETS // 2
    nf = np.maximum(n, 1).astype(np.float32)
    large = max_exact + (np.log(nf / max_exact) / math.log(MAX_DISTANCE / max_exact)
                         * (N_BUCKETS - max_exact)).astype(np.int32)
    large = np.minimum(large, N_BUCKETS - 1)
    return np.where(n < max_exact, n, large).astype(np.int32)


def _dsa_n_off(blk):
    return 2 + -(-MAX_DISTANCE // blk)


def _dsa_bucket_tiles(blk):
    n_off = _dsa_n_off(blk)
    j = np.arange(blk)[None, :, None]
    i = np.arange(blk)[None, None, :]
    o = np.arange(n_off)[:, None, None]
    return _t5_bucket_np(i - j + blk * (n_off - 1 - o))


def _dsa_kernel(bucket_ref, relb_ref, q_ref, k_ref, vt_ref, qi_ref, kk_ref, wi_ref, o_ref,
                sc_ref, bias_ref, m_ref, l_ref, acc_ref, st_ref, mx_ref,
                *, blk, n_chunks, n_sel, n_off, idx_bits):
    b = pl.program_id(0)
    i = pl.program_id(1)
    q0 = i * blk
    nck = i + 1
    groups = blk // SUBLANES

    @pl.when(jnp.logical_and(b == 0, i == 0))
    def _build_bias():
        def head_body(h, _):
            for o in range(n_off):
                for rb in range(blk // LANES):
                    for cb in range(blk // LANES):
                        rs = slice(rb * LANES, (rb + 1) * LANES)
                        cs = slice(cb * LANES, (cb + 1) * LANES)
                        bk = bucket_ref[o, rs, cs]

                        def bucket_body(n, acc):
                            return jnp.where(bk == n, relb_ref[n, h], acc)

                        bias_ref[h, o, rs, cs] = lax.fori_loop(
                            0, N_BUCKETS, bucket_body, jnp.zeros((LANES, LANES), F32))
            return 0

        lax.fori_loop(0, A_HEADS, head_body, 0)

    lane = lax.broadcasted_iota(I32, (blk, LANES), 1)
    lo_half = lane < HALF
    krow = lax.broadcasted_iota(I32, (blk, blk), 0)
    qpos = q0 + lax.broadcasted_iota(I32, (1, blk), 1)

    def pair_split(ref, scale):
        out = []
        for p in range(N_PAIRS):
            v = ref[0, :, p * LANES:(p + 1) * LANES].astype(F32)
            if scale != 1.0:
                v = v * scale
            out.append(jnp.where(lo_half, v, 0.0).astype(BF16))
            out.append(jnp.where(lo_half, 0.0, v).astype(BF16))
        return out

    wi_t = wi_ref[0].astype(F32).T
    wrow = [wi_t[h:h + 1, :] * IDX_SCALE for h in range(IDX_HEADS)]
    qi_m = pair_split(qi_ref, 1.0)

    def score_chunk(c, _):
        c0 = pl.multiple_of(c * blk, blk)
        kk = kk_ref[0, pl.ds(c0, blk), :]
        acc = jnp.zeros((blk, blk), F32)
        for h in range(IDX_HEADS):
            s = lax.dot_general(kk, qi_m[h], NT_DIMS, preferred_element_type=F32)
            acc = acc + wrow[h] * jnp.maximum(s, 0.0)
        sc_ref[c] = jnp.where(c0 + krow <= qpos, acc, -jnp.inf)
        return 0

    lax.fori_loop(0, nck, score_chunk, 0)

    kt = jnp.minimum(qpos + 1, n_sel).astype(F32)

    def fold(fn, init):
        def body(c, acc):
            return fn(acc, sc_ref[c], c * blk + krow)
        return lax.fori_loop(0, nck, body, init)

    def part(x, op):
        return op(x.reshape(groups, SUBLANES, blk), axis=0)

    def fin(x, op):
        return op(x, axis=0, keepdims=True)

    zeros8 = jnp.zeros((SUBLANES, blk), F32)
    pinf8 = jnp.full((SUBLANES, blk), jnp.inf, F32)

    def count_ge(th):
        return fin(fold(lambda a, s, idx: a + part(jnp.where(s >= th, 1.0, 0.0), jnp.sum), zeros8),
                   jnp.sum)

    mn, mx = fold(lambda a, s, idx: (
        jnp.minimum(a[0], part(jnp.where(s == -jnp.inf, jnp.inf, s), jnp.min)),
        jnp.maximum(a[1], part(s, jnp.max))), (pinf8, -pinf8))
    rmin = fin(mn, jnp.min)
    rmax = fin(mx, jnp.max)

    def bis_cond(st):
        it, lo, hi, clo = st
        return jnp.logical_and(it < BISECT_CAP, jnp.max(jnp.abs(clo - kt)) > 0.0)

    def bis_body(st):
        it, lo, hi, clo = st
        mid = 0.5 * lo + 0.5 * hi
        c = count_ge(mid)
        active = clo != kt
        up = jnp.logical_and(active, c >= kt)
        dn = jnp.logical_and(active, c < kt)
        return (it + 1, jnp.where(up, mid, lo), jnp.where(dn, mid, hi), jnp.where(up, c, clo))

    _, lo, _, _ = lax.while_loop(
        bis_cond, bis_body, (jnp.int32(0), rmin, rmax + 1.0, (qpos + 1).astype(F32)))

    def stats(lo_):
        a_ = fin(fold(lambda a, s, idx: jnp.minimum(a, part(jnp.where(s >= lo_, s, jnp.inf), jnp.min)),
                      pinf8), jnp.min)
        cg, ct, nx = fold(
            lambda a, s, idx: (a[0] + part(jnp.where(s > a_, 1.0, 0.0), jnp.sum),
                               a[1] + part(jnp.where(s == a_, 1.0, 0.0), jnp.sum),
                               jnp.minimum(a[2], part(jnp.where(s > a_, s, jnp.inf), jnp.min))),
            (zeros8, zeros8, pinf8))
        return a_, fin(cg, jnp.sum), fin(ct, jnp.sum), fin(nx, jnp.min)

    def fin_cond(st):
        return st[0]

    def fin_body(st):
        _, lo_, _, _ = st
        a_, cgt_, nt_, nxt_ = stats(lo_)
        bad = cgt_ >= kt
        return (jnp.max(jnp.where(bad, 1.0, 0.0)) > 0.0, jnp.where(bad, nxt_, a_), cgt_, nt_)

    _, a, cgt, nties = lax.while_loop(fin_cond, fin_body, (jnp.bool_(True), lo, kt, kt))
    need = kt - cgt
    excess = jnp.max(jnp.where(nties > need, 1.0, 0.0)) > 0.0

    def tie_search():
        ans = jnp.zeros((1, blk), I32)
        for bit in reversed(range(idx_bits)):
            cand = ans + (1 << bit)
            cnt = fin(fold(lambda acc, s, idx: acc + part(jnp.where(
                jnp.logical_and(s == a, idx < cand), 1.0, 0.0), jnp.sum), zeros8), jnp.sum)
            ans = jnp.where(cnt < need, cand, ans)
        return ans

    jstar = lax.cond(excess, tie_search, lambda: jnp.full((1, blk), (1 << idx_bits) - 1, I32))

    def mask_chunk(c, _):
        s = sc_ref[c]
        idx = c * blk + krow
        sel = jnp.logical_or(s > a, jnp.logical_and(s == a, idx <= jstar))
        sc_ref[c] = jnp.where(sel, 0.0, NEG)
        return 0

    lax.fori_loop(0, nck, mask_chunk, 0)

    odd = lax.rem(nck, 2)
    npair = (nck + odd) // 2

    @pl.when(odd == 1)
    def _mask_extra_chunk():
        sc_ref[nck] = jnp.full((blk, blk), NEG, F32)

    m_ref[...] = jnp.full(m_ref.shape, NEG, F32)
    l_ref[...] = jnp.zeros(l_ref.shape, F32)
    acc_ref[...] = jnp.zeros(acc_ref.shape, F32)
    q_m = pair_split(q_ref, QK_SCALE)

    def stage_logits(c, slot):
        c = jnp.minimum(c, n_chunks - 1)
        c0 = pl.multiple_of(c * blk, blk)
        madd = sc_ref[c]
        o_idx = jnp.clip(c - i + (n_off - 1), 0, n_off - 1)
        for p in range(N_PAIRS):
            k2 = k_ref[0, pl.ds(c0, blk), p * LANES:(p + 1) * LANES]
            for hh in range(2):
                h = 2 * p + hh
                s = lax.dot_general(k2, q_m[h], NT_DIMS, preferred_element_type=F32)
                s = s + bias_ref[h, o_idx] + madd
                st_ref[slot, h] = s
                mx_ref[slot, h] = jnp.max(s, axis=0, keepdims=True)

    def stage_values(c, slot):
        for p in range(N_PAIRS):
            vt2 = vt_ref[0, c, p * LANES:(p + 1) * LANES, :]
            for hh in range(2):
                h = 2 * p + hh
                m_old = m_ref[h]
                m_new = jnp.maximum(m_old, mx_ref[slot, h])
                alpha = jnp.exp(m_old - m_new)
                pexp = jnp.exp(st_ref[slot, h] - m_new)
                l_ref[h] = alpha * l_ref[h] + jnp.sum(pexp, axis=0, keepdims=True)
                acc_ref[h] = alpha * acc_ref[h] + jnp.dot(vt2, pexp.astype(BF16),
                                                          preferred_element_type=F32)
                m_ref[h] = m_new

    stage_logits(0, 0)

    def att_pair(pp, _):
        c = 2 * pp
        stage_logits(c + 1, 1)
        stage_values(c, 0)
        stage_logits(c + 2, 0)
        stage_values(c + 1, 1)
        return 0

    lax.fori_loop(0, npair, att_pair, 0)

    lo_rows = lax.broadcasted_iota(I32, (LANES, blk), 0) < HALF
    for p in range(N_PAIRS):
        oa = acc_ref[2 * p] / l_ref[2 * p]
        ob = acc_ref[2 * p + 1] / l_ref[2 * p + 1]
        o_ref[0, :, p * LANES:(p + 1) * LANES] = jnp.where(lo_rows, oa, ob).T.astype(o_ref.dtype)


def _dsa(proj3, vt4, rel_bias):
    B, S, _ = proj3.shape
    blk = DSA_BLK
    n_off = _dsa_n_off(blk)
    n_sel = min(TOPK_MAX, S // 4)
    idx_bits = max(1, int(math.ceil(math.log2(S))))
    bucket = jnp.asarray(_dsa_bucket_tiles(blk))
    n_chunks = S // blk
    assert S % blk == 0 and n_chunks % 2 == 0, "DSA consumes key chunks in pairs"
    kern = functools.partial(_dsa_kernel, blk=blk, n_chunks=n_chunks, n_sel=n_sel, n_off=n_off,
                             idx_bits=idx_bits)
    return pl.pallas_call(
        kern,
        grid=(B, S // blk),
        in_specs=[
            pl.BlockSpec((n_off, blk, blk), lambda b, i: (0, 0, 0)),
            pl.BlockSpec(memory_space=pltpu.SMEM),
            pl.BlockSpec((1, blk, ATT_W), lambda b, i: (b, i, COL_QA // ATT_W)),
            pl.BlockSpec((1, S, ATT_W), lambda b, i: (b, 0, COL_KA // ATT_W)),
            pl.BlockSpec((1, S // blk, ATT_W, blk), lambda b, i: (b, 0, 0, 0)),
            pl.BlockSpec((1, blk, ATT_W), lambda b, i: (b, i, COL_QI // ATT_W)),
            pl.BlockSpec((1, S, LANES), lambda b, i: (b, 0, COL_KK // LANES)),
            pl.BlockSpec((1, blk, LANES), lambda b, i: (b, i, COL_WI // LANES)),
        ],
        out_specs=pl.BlockSpec((1, blk, ATT_W), lambda b, i: (b, i, 0)),
        out_shape=jax.ShapeDtypeStruct((B, S, ATT_W), BF16),
        scratch_shapes=[
            pltpu.VMEM((S // blk, blk, blk), F32),
            pltpu.VMEM((A_HEADS, n_off, blk, blk), F32),
            pltpu.VMEM((A_HEADS, 1, blk), F32),
            pltpu.VMEM((A_HEADS, 1, blk), F32),
            pltpu.VMEM((A_HEADS, LANES, blk), F32),
            pltpu.VMEM((2, A_HEADS, blk, blk), F32),
            pltpu.VMEM((2, A_HEADS, 1, blk), F32),
        ],
        compiler_params=_params(("arbitrary", "arbitrary")),
    )(bucket, rel_bias, proj3, proj3, vt4, proj3, proj3, proj3)


def _sb_kernel(q_ref, k_ref, v_ref, o_ref, hl_ref, ls_ref, rs_ref, *, t, n_chunks):
    i = pl.program_id(2)
    n = i + 1
    odd = lax.rem(n, 2)
    top = i + odd
    npair = (n + odd) // 2
    lane = lax.broadcasted_iota(I32, (t, LANES), 1)
    lo_half = lane < HALF
    q2 = q_ref[0].astype(F32) * QK_SCALE
    q_m = (jnp.where(lo_half, q2, 0.0).astype(BF16), jnp.where(lo_half, 0.0, q2).astype(BF16))
    r = lax.broadcasted_iota(I32, (t, t), 0)
    cidx = lax.broadcasted_iota(I32, (t, t), 1)
    later = (r > cidx).astype(BF16)
    diff = cidx - r

    def stage_terms(step, slot, masked):
        chunk = top - step
        c0 = pl.multiple_of(jnp.clip(chunk, 0, n_chunks - 1) * t, t)
        k2 = k_ref[0, pl.ds(c0, t), :]
        if masked:
            keep = diff < (i - chunk) * t
        for hh in range(2):
            z = lax.dot_general(q_m[hh], k2, NT_DIMS, preferred_element_type=F32)
            sp = jnp.maximum(z, 0.0) + jnp.log(1.0 + jnp.exp(-jnp.abs(z)))
            lm = -sp
            ls = z - sp
            if masked:
                lm = jnp.where(keep, lm, 0.0)
                ls = jnp.where(keep, ls, NEG)
            hi = lm.astype(BF16)
            hl_ref[slot, 2 * hh] = hi
            hl_ref[slot, 2 * hh + 1] = (lm - hi.astype(F32)).astype(BF16)
            ls_ref[slot, hh] = ls
            rs_ref[slot, hh] = jnp.sum(lm, axis=1, keepdims=True)

    def stage_apply(step, slot, carry):
        c0 = pl.multiple_of(jnp.minimum(top - step, n_chunks - 1) * t, t)
        v2 = v_ref[0, pl.ds(c0, t), :]
        after4 = jnp.dot(hl_ref[slot].reshape(4 * t, t), later, preferred_element_type=F32)
        out = []
        for hh in range(2):
            car, acc = carry[hh]
            after = after4[(2 * hh) * t:(2 * hh + 1) * t] + after4[(2 * hh + 1) * t:(2 * hh + 2) * t]
            w = jnp.exp(ls_ref[slot, hh] + after + car)
            acc = acc + jnp.dot(w.astype(BF16), v2, preferred_element_type=F32)
            out.append((car + rs_ref[slot, hh], acc))
        return tuple(out)

    z1 = jnp.zeros((t, 1), F32)
    za = jnp.zeros((t, LANES), F32)
    stage_terms(0, 0, True)
    stage_terms(1, 1, True)
    carry = stage_apply(0, 0, ((z1, za), (z1, za)))
    stage_terms(2, 0, False)
    carry = stage_apply(1, 1, carry)

    def pair_body(pp, carry):
        step = 2 * pp
        stage_terms(step + 1, 1, False)
        carry = stage_apply(step, 0, carry)
        stage_terms(step + 2, 0, False)
        return stage_apply(step + 1, 1, carry)

    (_, acc_a), (_, acc_b) = lax.fori_loop(1, npair, pair_body, carry)
    o_ref[0] = jnp.where(lo_half, acc_a, acc_b).astype(o_ref.dtype)


def _stick_breaking(proj3):
    B, S, _ = proj3.shape
    t = SB_T
    qb, kb, vb = COL_QB // LANES, COL_KB // LANES, COL_VB // LANES
    return pl.pallas_call(
        functools.partial(_sb_kernel, t=t, n_chunks=S // t),
        grid=(B, N_PAIRS, S // t),
        in_specs=[
            pl.BlockSpec((1, t, LANES), lambda b, p, i: (b, i, qb + p)),
            pl.BlockSpec((1, S, LANES), lambda b, p, i: (b, 0, kb + p)),
            pl.BlockSpec((1, S, LANES), lambda b, p, i: (b, 0, vb + p)),
        ],
        out_specs=pl.BlockSpec((1, t, LANES), lambda b, p, i: (b, i, p)),
        out_shape=jax.ShapeDtypeStruct((B, S, ATT_W), BF16),
        scratch_shapes=[
            pltpu.VMEM((2, 4, t, t), BF16),
            pltpu.VMEM((2, 2, t, t), F32),
            pltpu.VMEM((2, 2, t, 1), F32),
        ],
        compiler_params=_params(("arbitrary", "arbitrary", "arbitrary")),
    )(proj3, proj3, proj3)


def _layer_norm(r, g, b):
    mu = jnp.mean(r, axis=-1, keepdims=True)
    d = r - mu
    var = jnp.mean(d * d, axis=-1, keepdims=True)
    return d * lax.rsqrt(var + LN_EPS) * g + b


def _split_bf16(v):
    hi = v.astype(BF16)
    return hi, (v - hi.astype(F32)).astype(BF16)


def _merge_kernel(x_ref, ya_ref, yb_ref, gate_ref, wa_ref, wb_ref, wo_ref, g_ref, b_ref,
                  wr_ref, br_ref, h_ref, e_ref, p_ref, *, alpha, d):
    pa = jnp.dot(ya_ref[...], wa_ref[...], preferred_element_type=F32)
    pb = jnp.dot(yb_ref[...], wb_ref[...], preferred_element_type=F32)
    merged = jax.nn.sigmoid(gate_ref[:, :d]) * pa + jax.nn.sigmoid(gate_ref[:, d:]) * pb
    m = jnp.dot(merged.astype(BF16), wo_ref[...], preferred_element_type=F32)
    h = _layer_norm(alpha * x_ref[...] + m, g_ref[...], b_ref[...])
    h_ref[...] = h

    h_hi, h_lo = _split_bf16(h)
    w_hi, w_lo = _split_bf16(wr_ref[...])
    logit = (lax.dot_general(w_hi, h_hi, NT_DIMS, preferred_element_type=F32)
             + lax.dot_general(w_hi, h_lo, NT_DIMS, preferred_element_type=F32)
             + lax.dot_general(w_lo, h_hi, NT_DIMS, preferred_element_type=F32)) + br_ref[...]
    eid = lax.broadcasted_iota(I32, logit.shape, 0)
    vals, ids = [], []
    for _ in range(TOP_K):
        mx = jnp.max(logit, axis=0, keepdims=True)
        am = jnp.min(jnp.where(logit == mx, eid, N_EXPERTS), axis=0, keepdims=True)
        vals.append(mx)
        ids.append(am)
        logit = jnp.where(eid == am, -jnp.inf, logit)
    ex = [jnp.exp(v - vals[0]) for v in vals]
    den = ex[0] + ex[1] + ex[2] + ex[3]
    for k in range(TOP_K):
        e_ref[k:k + 1, :] = ids[k]
        p_ref[k:k + 1, :] = ex[k] / den


def _merge(x2, ya, yb, gates, wa, wb, wo, g, b, wr_t, br, alpha):
    T, D = x2.shape
    tm = MERGE_TM
    row = lambda i: (i, 0)
    fixed = lambda i: (0, 0)
    return pl.pallas_call(
        functools.partial(_merge_kernel, alpha=alpha, d=D),
        grid=(T // tm,),
        in_specs=[
            pl.BlockSpec((tm, D), row),
            pl.BlockSpec((tm, ATT_W), row),
            pl.BlockSpec((tm, ATT_W), row),
            pl.BlockSpec((tm, 2 * D), row),
            pl.BlockSpec((ATT_W, D), fixed),
            pl.BlockSpec((ATT_W, D), fixed),
            pl.BlockSpec((D, D), fixed),
            pl.BlockSpec((1, D), fixed),
            pl.BlockSpec((1, D), fixed),
            pl.BlockSpec((N_EXPERTS, D), fixed),
            pl.BlockSpec((N_EXPERTS, 1), fixed),
        ],
        out_specs=[
            pl.BlockSpec((tm, D), row),
            pl.BlockSpec((TOP_K, tm), lambda i: (0, i)),
            pl.BlockSpec((TOP_K, tm), lambda i: (0, i)),
        ],
        out_shape=[
            jax.ShapeDtypeStruct((T, D), F32),
            jax.ShapeDtypeStruct((TOP_K, T), I32),
            jax.ShapeDtypeStruct((TOP_K, T), F32),
        ],
        compiler_params=_params(("arbitrary",)),
    )(x2, ya, yb, gates, wa, wb, wo, g, b, wr_t, br)


def _sc_gather_rows(table, idx):
    n = idx.shape[0]
    d = table.shape[1]
    info = plsc.get_sparse_core_info()
    n_cores, n_sub = info.num_cores, info.num_subcores
    per_w = n // (n_cores * n_sub)
    c = SC_GATHER_ROWS
    n_g = per_w // c
    assert n == per_w * n_cores * n_sub and per_w == n_g * c and n_g % 2 == 0 and n_g >= 2
    mesh = plsc.VectorSubcoreMesh(core_axis_name="c", subcore_axis_name="s")

    @functools.partial(
        pl.kernel, mesh=mesh, out_type=jax.ShapeDtypeStruct((n, d), table.dtype),
        scratch_types=[pltpu.VMEM((per_w,), I32), pltpu.VMEM((2, c, d), table.dtype),
                       pltpu.SemaphoreType.DMA((2,)), pltpu.SemaphoreType.DMA((2,))])
    def gather_kernel(table_hbm, idx_hbm, out_hbm, idx_v, rows_v, gsem, wsem):
        base = (lax.axis_index("s") * n_cores + lax.axis_index("c")) * per_w
        pltpu.sync_copy(idx_hbm.at[pl.ds(base, per_w)], idx_v)

        def gather(g, b):
            return pltpu.make_async_copy(table_hbm.at[idx_v.at[pl.ds(g * c, c)]], rows_v.at[b],
                                         gsem.at[b])

        def write(g, b):
            return pltpu.make_async_copy(rows_v.at[b], out_hbm.at[pl.ds(base + g * c, c)], wsem.at[b])

        gather(0, 0).start()

        @pl.loop(0, n_g, step=2)
        def _ring(g0):
            for b in range(2):
                g = g0 + b

                @pl.when(g + 1 < n_g)
                def _next():
                    @pl.when(g >= 1)
                    def _buffer_free():
                        write(g - 1, 1 - b).wait()
                    gather(g + 1, 1 - b).start()

                gather(g, b).wait()
                write(g, b).start()

        write(n_g - 2, 0).wait()
        write(n_g - 1, 1).wait()

    return gather_kernel(table, idx)


def _moe_kernel(blk_e_ref, nused_ref, x_ref, wgu_ref, bgu_ref, wdn_ref, bdn_ref, o_ref,
                wgu_s, wdn_s, *, f):
    i = pl.program_id(0)
    nused = nused_ref[0]

    @pl.when(i < nused)
    def _compute():
        changed = jnp.logical_or(i == 0, blk_e_ref[i] != blk_e_ref[jnp.maximum(i - 1, 0)])

        @pl.when(changed)
        def _cast_weights():
            wgu_s[...] = wgu_ref[0].astype(BF16)
            wdn_s[...] = wdn_ref[0].astype(BF16)

        x = x_ref[...].astype(BF16)
        hgu = jnp.dot(x, wgu_s[...], preferred_element_type=F32) + bgu_ref[0]
        a = jnp.minimum(hgu[:, :f], SWIGLU_LIMIT)
        u = jnp.clip(hgu[:, f:], -SWIGLU_LIMIT, SWIGLU_LIMIT)
        glu = a * jax.nn.sigmoid(a * SWIGLU_ALPHA)
        o_ref[...] = jnp.dot(((u + 1.0) * glu).astype(BF16), wdn_s[...],
                             preferred_element_type=F32) + bdn_ref[0]

    @pl.when(i >= nused)
    def _unused_block():
        o_ref[...] = jnp.zeros(o_ref.shape, o_ref.dtype)


def _moe_ffn(xs, blk_e, nused, w_gu, b_gu, w_dn, b_dn):
    P, D = xs.shape
    E, _, F2 = w_gu.shape
    f = F2 // 2
    blk = MOE_BLK
    nb = P // blk
    grid_spec = pltpu.PrefetchScalarGridSpec(
        num_scalar_prefetch=2,
        grid=(nb,),
        in_specs=[
            pl.BlockSpec((blk, D), lambda i, be, nu: (i, 0)),
            pl.BlockSpec((1, D, F2), lambda i, be, nu: (be[i], 0, 0)),
            pl.BlockSpec((1, 1, F2), lambda i, be, nu: (be[i], 0, 0)),
            pl.BlockSpec((1, f, D), lambda i, be, nu: (be[i], 0, 0)),
            pl.BlockSpec((1, 1, D), lambda i, be, nu: (be[i], 0, 0)),
        ],
        out_specs=pl.BlockSpec((blk, D), lambda i, be, nu: (i, 0)),
        scratch_shapes=[
            pltpu.VMEM((D, F2), BF16),
            pltpu.VMEM((f, D), BF16),
        ],
    )
    return pl.pallas_call(
        functools.partial(_moe_kernel, f=f),
        grid_spec=grid_spec,
        out_shape=jax.ShapeDtypeStruct((P, D), F32),
        compiler_params=_params(("arbitrary",)),
    )(blk_e, nused, xs, w_gu, b_gu.reshape(E, 1, F2), w_dn, b_dn.reshape(E, 1, D))


def _comb_kernel(h_ref, p_ref, y_ref, g_ref, b_ref, o_ref, *, alpha):
    gate = p_ref[...]
    fsum = ((y_ref[0] * gate[:, 0:1] + y_ref[1] * gate[:, 1:2])
            + (y_ref[2] * gate[:, 2:3] + y_ref[3] * gate[:, 3:4]))
    o_ref[...] = _layer_norm(alpha * h_ref[...] + fsum, g_ref[...], b_ref[...])


def _combine(h, y4, top_p, g, b, alpha):
    T, D = h.shape
    tm = COMB_TM
    return pl.pallas_call(
        functools.partial(_comb_kernel, alpha=alpha),
        grid=(T // tm,),
        in_specs=[
            pl.BlockSpec((tm, D), lambda i: (i, 0)),
            pl.BlockSpec((tm, TOP_K), lambda i: (i, 0)),
            pl.BlockSpec((TOP_K, tm, D), lambda i: (0, i, 0)),
            pl.BlockSpec((1, D), lambda i: (0, 0)),
            pl.BlockSpec((1, D), lambda i: (0, 0)),
        ],
        out_specs=pl.BlockSpec((tm, D), lambda i: (i, 0)),
        out_shape=jax.ShapeDtypeStruct((T, D), F32),
        compiler_params=_params(("arbitrary",)),
    )(h, top_p.T, y4, g, b)


def _route(top_e, blk):
    K, T = top_e.shape
    N = K * T
    flat_e = top_e.reshape(N)
    experts = jnp.arange(N_EXPERTS, dtype=I32)
    order = jnp.argsort(flat_e, stable=True).astype(I32)
    inv = jnp.argsort(order).astype(I32)
    onehot = flat_e[:, None] == experts[None, :]
    counts = jnp.sum(onehot, axis=0, dtype=I32)
    padded = (counts + blk - 1) // blk * blk
    pends = jnp.cumsum(padded)
    offs = jnp.cumsum(counts) - counts
    shift = (pends - padded) - offs
    pos = inv + jnp.sum(jnp.where(onehot, shift[None, :], 0), axis=1, dtype=I32)
    P = N + N_EXPERTS * blk
    nb = P // blk
    blk_start = jnp.arange(nb, dtype=I32) * blk
    blk_e = jnp.minimum(jnp.sum(pends[None, :] <= blk_start[:, None], axis=1, dtype=I32), N_EXPERTS - 1)
    j = (blk_start - shift[blk_e])[:, None] + jnp.arange(blk, dtype=I32)[None, :]
    valid = j < (offs + counts)[blk_e][:, None]
    src = order[jnp.clip(j, 0, N - 1)]
    row_tok = jnp.where(valid, src % T, 0).reshape(P)
    nused = (pends[-1:] // blk).astype(I32)
    return blk_e, nused, row_tok, pos.reshape(K, T)


def _projection_weights(w_in_l):
    sizes = (ATT_W, ATT_W, ATT_W, IDX_HEADS * IDX_DIM, IDX_DIM, IDX_HEADS, ATT_W, ATT_W, ATT_W)
    offs = np.concatenate([[0], np.cumsum(sizes)])
    qa, ka, va, qi, ki, wi, qb, kb, vb = (w_in_l[:, offs[n]:offs[n + 1]] for n in range(9))
    pad_wi = jnp.zeros((w_in_l.shape[0], LANES - IDX_HEADS), w_in_l.dtype)
    w_att = jnp.concatenate([qa, ka, qi, qb, kb, vb, ki, ki, wi, pad_wi], axis=1).astype(BF16)
    w_va_t = va.T.astype(BF16)
    w_gate = w_in_l[:, offs[9]:].astype(BF16)
    return w_att, w_va_t, w_gate


def kernel(x, w_in, w_branch_a, w_branch_b, w_out, rel_bias, ln1_g, ln1_b, w_router, b_router,
           w_gate_up, b_gate_up, w_down, b_down, ln2_g, ln2_b):
    B, S, D = x.shape
    depth = w_in.shape[0]
    alpha = (2 * depth) ** 0.25
    T = B * S
    h = x.reshape(T, D)
    for l in range(depth):
        w_att, w_va_t, w_gate = _projection_weights(w_in[l])
        proj = _matmul(h, w_att, BF16, min(T, 1024), PROJ_TN).reshape(B, S, ATT_COLS)
        vt = _matmul_t(h, w_va_t, DSA_BLK).reshape(B, S // DSA_BLK, ATT_W, DSA_BLK)
        gates = _matmul(h, w_gate, F32, min(T, 1024), 1024)
        ya = _dsa(proj, vt, rel_bias).reshape(T, ATT_W)
        yb = _stick_breaking(proj).reshape(T, ATT_W)
        h1, top_e, top_p = _merge(
            h, ya, yb, gates, w_branch_a[l].astype(BF16), w_branch_b[l].astype(BF16),
            w_out[l].astype(BF16), ln1_g[l].reshape(1, D), ln1_b[l].reshape(1, D),
            w_router[l].T, b_router[l].reshape(N_EXPERTS, 1), alpha)
        blk_e, nused, row_tok, pos = _route(top_e, MOE_BLK)
        xs = _sc_gather_rows(h1, row_tok)
        ys = _moe_ffn(xs, blk_e, nused, w_gate_up[l], b_gate_up[l], w_down[l], b_down[l])
        y4 = _sc_gather_rows(ys, pos.reshape(TOP_K * T)).reshape(TOP_K, T, D)
        h = _combine(h1, y4, top_p, ln2_g[l].reshape(1, D), ln2_b[l].reshape(1, D), alpha)
    return h.reshape(B, S, D)
```

```python
import functools
import math

import numpy as np
import jax
import jax.numpy as jnp
from jax import lax
from jax.experimental import pallas as pl
from jax.experimental.pallas import tpu as pltpu
from jax.experimental.pallas import tpu_sc as plsc

F32 = jnp.float32
BF16 = jnp.bfloat16
I32 = jnp.int32

A_HEADS = 8
HEAD_DIM = 64
ATT_W = A_HEADS * HEAD_DIM
IDX_HEADS = 8
IDX_DIM = 64
IDX_SCALE = (IDX_HEADS * IDX_DIM) ** -0.5
TOPK_MAX = 256
N_BUCKETS = 32
MAX_DISTANCE = 128
N_EXPERTS = 32
TOP_K = 4
SWIGLU_LIMIT = 7.0
SWIGLU_ALPHA = 1.702
LN_EPS = 1e-5
QK_SCALE = HEAD_DIM ** -0.5

LANES = 128
SUBLANES = 8
HALF = LANES // 2
N_PAIRS = A_HEADS // 2
VMEM_LIMIT = 56 * 1024 * 1024

DSA_BLK = 256
SB_T = 256
MERGE_TM = 512
MOE_BLK = 512
COMB_TM = 256
SC_GATHER_ROWS = 32
BISECT_CAP = 24
NEG = -1e30

COL_QA, COL_KA, COL_QI, COL_QB, COL_KB, COL_VB = (g * ATT_W for g in range(6))
COL_KK = 6 * ATT_W
COL_WI = COL_KK + LANES
ATT_COLS = COL_WI + LANES
PROJ_TN = ATT_COLS // 2

NT_DIMS = (((1,), (1,)), ((), ()))


def _params(sem, vmem=VMEM_LIMIT):
    return pltpu.CompilerParams(dimension_semantics=sem, vmem_limit_bytes=vmem)


def _mm_kernel(x_ref, w_ref, o_ref, xb_ref):
    @pl.when(pl.program_id(1) == 0)
    def _cast():
        xb_ref[...] = x_ref[...].astype(BF16)

    o_ref[...] = jnp.dot(xb_ref[...], w_ref[...], preferred_element_type=F32).astype(o_ref.dtype)


def _matmul(x, w, out_dtype, tm, tn):
    M, K = x.shape
    N = w.shape[1]
    return pl.pallas_call(
        _mm_kernel,
        grid=(M // tm, N // tn),
        in_specs=[pl.BlockSpec((tm, K), lambda i, j: (i, 0)),
                  pl.BlockSpec((K, tn), lambda i, j: (0, j))],
        out_specs=pl.BlockSpec((tm, tn), lambda i, j: (i, j)),
        out_shape=jax.ShapeDtypeStruct((M, N), out_dtype),
        scratch_shapes=[pltpu.VMEM((tm, K), BF16)],
        compiler_params=_params(("arbitrary", "arbitrary")),
    )(x, w)


def _mm_t_kernel(x_ref, w_ref, o_ref):
    o_ref[0] = lax.dot_general(w_ref[...], x_ref[...].astype(BF16), NT_DIMS,
                               preferred_element_type=F32).astype(o_ref.dtype)


def _matmul_t(x, w_t, tm):
    M, K = x.shape
    N = w_t.shape[0]
    return pl.pallas_call(
        _mm_t_kernel,
        grid=(M // tm,),
        in_specs=[pl.BlockSpec((tm, K), lambda i: (i, 0)),
                  pl.BlockSpec((N, K), lambda i: (0, 0))],
        out_specs=pl.BlockSpec((1, N, tm), lambda i: (i, 0, 0)),
        out_shape=jax.ShapeDtypeStruct((M // tm, N, tm), BF16),
        compiler_params=_params(("arbitrary",)),
    )(x, w_t)


def _t5_bucket_np(n):
    n = np.maximum(n, 0)
    max_exact = N_BUCKETS // 2
    nf = np.maximum(n, 1).astype(np.float32)
    large = max_exact + (np.log(nf / max_exact) / math.log(MAX_DISTANCE / max_exact)
                         * (N_BUCKETS - max_exact)).astype(np.int32)
    large = np.minimum(large, N_BUCKETS - 1)
    return np.where(n < max_exact, n, large).astype(np.int32)


def _dsa_n_off(blk):
    return 2 + -(-MAX_DISTANCE // blk)


def _dsa_bucket_tiles(blk):
    n_off = _dsa_n_off(blk)
    j = np.arange(blk)[None, :, None]
    i = np.arange(blk)[None, None, :]
    o = np.arange(n_off)[:, None, None]
    return _t5_bucket_np(i - j + blk * (n_off - 1 - o))


def _dsa_kernel(bucket_ref, relb_ref, q_ref, k_ref, vt_ref, qi_ref, kk_ref, wi_ref, o_ref,
                sc_ref, bias_ref, m_ref, l_ref, acc_ref, st_ref, mx_ref,
                *, blk, n_chunks, n_sel, n_off, idx_bits):
    b = pl.program_id(0)
    i = pl.program_id(1)
    q0 = i * blk
    nck = i + 1
    groups = blk // SUBLANES

    @pl.when(jnp.logical_and(b == 0, i == 0))
    def _build_bias():
        def head_body(h, _):
            for o in range(n_off):
                for rb in range(blk // LANES):
                    for cb in range(blk // LANES):
                        rs = slice(rb * LANES, (rb + 1) * LANES)
                        cs = slice(cb * LANES, (cb + 1) * LANES)
                        bk = bucket_ref[o, rs, cs]

                        def bucket_body(n, acc):
                            return jnp.where(bk == n, relb_ref[n, h], acc)

                        bias_ref[h, o, rs, cs] = lax.fori_loop(
                            0, N_BUCKETS, bucket_body, jnp.zeros((LANES, LANES), F32))
            return 0

        lax.fori_loop(0, A_HEADS, head_body, 0)

    lane = lax.broadcasted_iota(I32, (blk, LANES), 1)
    lo_half = lane < HALF
    krow = lax.broadcasted_iota(I32, (blk, blk), 0)
    qpos = q0 + lax.broadcasted_iota(I32, (1, blk), 1)

    def pair_split(ref, scale):
        out = []
        for p in range(N_PAIRS):
            v = ref[0, :, p * LANES:(p + 1) * LANES].astype(F32)
            if scale != 1.0:
                v = v * scale
            out.append(jnp.where(lo_half, v, 0.0).astype(BF16))
            out.append(jnp.where(lo_half, 0.0, v).astype(BF16))
        return out

    wi_t = wi_ref[0].astype(F32).T
    wrow = [wi_t[h:h + 1, :] * IDX_SCALE for h in range(IDX_HEADS)]
    qi_m = pair_split(qi_ref, 1.0)

    def score_chunk(c, _):
        c0 = pl.multiple_of(c * blk, blk)
        kk = kk_ref[0, pl.ds(c0, blk), :]
        acc = jnp.zeros((blk, blk), F32)
        for h in range(IDX_HEADS):
            s = lax.dot_general(kk, qi_m[h], NT_DIMS, preferred_element_type=F32)
            acc = acc + wrow[h] * jnp.maximum(s, 0.0)
        sc_ref[c] = jnp.where(c0 + krow <= qpos, acc, -jnp.inf)
        return 0

    lax.fori_loop(0, nck, score_chunk, 0)

    kt = jnp.minimum(qpos + 1, n_sel).astype(F32)

    def fold(fn, init):
        def body(c, acc):
            return fn(acc, sc_ref[c], c * blk + krow)
        return lax.fori_loop(0, nck, body, init)

    def part(x, op):
        return op(x.reshape(groups, SUBLANES, blk), axis=0)

    def fin(x, op):
        return op(x, axis=0, keepdims=True)

    zeros8 = jnp.zeros((SUBLANES, blk), F32)
    pinf8 = jnp.full((SUBLANES, blk), jnp.inf, F32)

    def count_ge(th):
        return fin(fold(lambda a, s, idx: a + part(jnp.where(s >= th, 1.0, 0.0), jnp.sum), zeros8),
                   jnp.sum)

    mn, mx = fold(lambda a, s, idx: (
        jnp.minimum(a[0], part(jnp.where(s == -jnp.inf, jnp.inf, s), jnp.min)),
        jnp.maximum(a[1], part(s, jnp.max))), (pinf8, -pinf8))
    rmin = fin(mn, jnp.min)
    rmax = fin(mx, jnp.max)

    def bis_cond(st):
        it, lo, hi, clo = st
        return jnp.logical_and(it < BISECT_CAP, jnp.max(jnp.abs(clo - kt)) > 0.0)

    def bis_body(st):
        it, lo, hi, clo = st
        mid = 0.5 * lo + 0.5 * hi
        c = count_ge(mid)
        active = clo != kt
        up = jnp.logical_and(active, c >= kt)
        dn = jnp.logical_and(active, c < kt)
        return (it + 1, jnp.where(up, mid, lo), jnp.where(dn, mid, hi), jnp.where(up, c, clo))

    _, lo, _, _ = lax.while_loop(
        bis_cond, bis_body, (jnp.int32(0), rmin, rmax + 1.0, (qpos + 1).astype(F32)))

    def stats(lo_):
        a_ = fin(fold(lambda a, s, idx: jnp.minimum(a, part(jnp.where(s >= lo_, s, jnp.inf), jnp.min)),
                      pinf8), jnp.min)
        cg, ct, nx = fold(
            lambda a, s, idx: (a[0] + part(jnp.where(s > a_, 1.0, 0.0), jnp.sum),
                               a[1] + part(jnp.where(s == a_, 1.0, 0.0), jnp.sum),
                               jnp.minimum(a[2], part(jnp.where(s > a_, s, jnp.inf), jnp.min))),
            (zeros8, zeros8, pinf8))
        return a_, fin(cg, jnp.sum), fin(ct, jnp.sum), fin(nx, jnp.min)

    def fin_cond(st):
        return st[0]

    def fin_body(st):
        _, lo_, _, _ = st
        a_, cgt_, nt_, nxt_ = stats(lo_)
        bad = cgt_ >= kt
        return (jnp.max(jnp.where(bad, 1.0, 0.0)) > 0.0, jnp.where(bad, nxt_, a_), cgt_, nt_)

    _, a, cgt, nties = lax.while_loop(fin_cond, fin_body, (jnp.bool_(True), lo, kt, kt))
    need = kt - cgt
    excess = jnp.max(jnp.where(nties > need, 1.0, 0.0)) > 0.0

    def tie_search():
        ans = jnp.zeros((1, blk), I32)
        for bit in reversed(range(idx_bits)):
            cand = ans + (1 << bit)
            cnt = fin(fold(lambda acc, s, idx: acc + part(jnp.where(
                jnp.logical_and(s == a, idx < cand), 1.0, 0.0), jnp.sum), zeros8), jnp.sum)
            ans = jnp.where(cnt < need, cand, ans)
        return ans

    jstar = lax.cond(excess, tie_search, lambda: jnp.full((1, blk), (1 << idx_bits) - 1, I32))

    def mask_chunk(c, _):
        s = sc_ref[c]
        idx = c * blk + krow
        sel = jnp.logical_or(s > a, jnp.logical_and(s == a, idx <= jstar))
        sc_ref[c] = jnp.where(sel, 0.0, NEG)
        return 0

    lax.fori_loop(0, nck, mask_chunk, 0)

    odd = lax.rem(nck, 2)
    npair = (nck + odd) // 2

    @pl.when(odd == 1)
    def _mask_extra_chunk():
        sc_ref[nck] = jnp.full((blk, blk), NEG, F32)

    m_ref[...] = jnp.full(m_ref.shape, NEG, F32)
    l_ref[...] = jnp.zeros(l_ref.shape, F32)
    acc_ref[...] = jnp.zeros(acc_ref.shape, F32)
    q_m = pair_split(q_ref, QK_SCALE)

    def stage_logits(c, slot):
        c = jnp.minimum(c, n_chunks - 1)
        c0 = pl.multiple_of(c * blk, blk)
        madd = sc_ref[c]
        o_idx = jnp.clip(c - i + (n_off - 1), 0, n_off - 1)
        for p in range(N_PAIRS):
            k2 = k_ref[0, pl.ds(c0, blk), p * LANES:(p + 1) * LANES]
            for hh in range(2):
                h = 2 * p + hh
                s = lax.dot_general(k2, q_m[h], NT_DIMS, preferred_element_type=F32)
                s = s + bias_ref[h, o_idx] + madd
                st_ref[slot, h] = s
                mx_ref[slot, h] = jnp.max(s, axis=0, keepdims=True)

    def stage_values(c, slot):
        for p in range(N_PAIRS):
            vt2 = vt_ref[0, c, p * LANES:(p + 1) * LANES, :]
            for hh in range(2):
                h = 2 * p + hh
                m_old = m_ref[h]
                m_new = jnp.maximum(m_old, mx_ref[slot, h])
                alpha = jnp.exp(m_old - m_new)
                pexp = jnp.exp(st_ref[slot, h] - m_new)
                l_ref[h] = alpha * l_ref[h] + jnp.sum(pexp, axis=0, keepdims=True)
                acc_ref[h] = alpha * acc_ref[h] + jnp.dot(vt2, pexp.astype(BF16),
                                                          preferred_element_type=F32)
                m_ref[h] = m_new

    stage_logits(0, 0)

    def att_pair(pp, _):
        c = 2 * pp
        stage_logits(c + 1, 1)
        stage_values(c, 0)
        stage_logits(c + 2, 0)
        stage_values(c + 1, 1)
        return 0

    lax.fori_loop(0, npair, att_pair, 0)

    lo_rows = lax.broadcasted_iota(I32, (LANES, blk), 0) < HALF
    for p in range(N_PAIRS):
        oa = acc_ref[2 * p] / l_ref[2 * p]
        ob = acc_ref[2 * p + 1] / l_ref[2 * p + 1]
        o_ref[0, :, p * LANES:(p + 1) * LANES] = jnp.where(lo_rows, oa, ob).T.astype(o_ref.dtype)


def _dsa(proj3, vt4, rel_bias):
    B, S, _ = proj3.shape
    blk = DSA_BLK
    n_off = _dsa_n_off(blk)
    n_sel = min(TOPK_MAX, S // 4)
    idx_bits = max(1, int(math.ceil(math.log2(S))))
    bucket = jnp.asarray(_dsa_bucket_tiles(blk))
    n_chunks = S // blk
    assert S % blk == 0 and n_chunks % 2 == 0, "DSA consumes key chunks in pairs"
    kern = functools.partial(_dsa_kernel, blk=blk, n_chunks=n_chunks, n_sel=n_sel, n_off=n_off,
                             idx_bits=idx_bits)
    return pl.pallas_call(
        kern,
        grid=(B, S // blk),
        in_specs=[
            pl.BlockSpec((n_off, blk, blk), lambda b, i: (0, 0, 0)),
            pl.BlockSpec(memory_space=pltpu.SMEM),
            pl.BlockSpec((1, blk, ATT_W), lambda b, i: (b, i, COL_QA // ATT_W)),
            pl.BlockSpec((1, S, ATT_W), lambda b, i: (b, 0, COL_KA // ATT_W)),
            pl.BlockSpec((1, S // blk, ATT_W, blk), lambda b, i: (b, 0, 0, 0)),
            pl.BlockSpec((1, blk, ATT_W), lambda b, i: (b, i, COL_QI // ATT_W)),
            pl.BlockSpec((1, S, LANES), lambda b, i: (b, 0, COL_KK // LANES)),
            pl.BlockSpec((1, blk, LANES), lambda b, i: (b, i, COL_WI // LANES)),
        ],
        out_specs=pl.BlockSpec((1, blk, ATT_W), lambda b, i: (b, i, 0)),
        out_shape=jax.ShapeDtypeStruct((B, S, ATT_W), BF16),
        scratch_shapes=[
            pltpu.VMEM((S // blk, blk, blk), F32),
            pltpu.VMEM((A_HEADS, n_off, blk, blk), F32),
            pltpu.VMEM((A_HEADS, 1, blk), F32),
            pltpu.VMEM((A_HEADS, 1, blk), F32),
            pltpu.VMEM((A_HEADS, LANES, blk), F32),
            pltpu.VMEM((2, A_HEADS, blk, blk), F32),
            pltpu.VMEM((2, A_HEADS, 1, blk), F32),
        ],
        compiler_params=_params(("arbitrary", "arbitrary")),
    )(bucket, rel_bias, proj3, proj3, vt4, proj3, proj3, proj3)


def _sb_kernel(q_ref, k_ref, v_ref, o_ref, hl_ref, ls_ref, rs_ref, *, t, n_chunks):
    i = pl.program_id(2)
    n = i + 1
    odd = lax.rem(n, 2)
    top = i + odd
    npair = (n + odd) // 2
    lane = lax.broadcasted_iota(I32, (t, LANES), 1)
    lo_half = lane < HALF
    q2 = q_ref[0].astype(F32) * QK_SCALE
    q_m = (jnp.where(lo_half, q2, 0.0).astype(BF16), jnp.where(lo_half, 0.0, q2).astype(BF16))
    r = lax.broadcasted_iota(I32, (t, t), 0)
    cidx = lax.broadcasted_iota(I32, (t, t), 1)
    later = (r > cidx).astype(BF16)
    diff = cidx - r

    def stage_terms(step, slot, masked):
        chunk = top - step
        c0 = pl.multiple_of(jnp.clip(chunk, 0, n_chunks - 1) * t, t)
        k2 = k_ref[0, pl.ds(c0, t), :]
        if masked:
            keep = diff < (i - chunk) * t
        for hh in range(2):
            z = lax.dot_general(q_m[hh], k2, NT_DIMS, preferred_element_type=F32)
            sp = jnp.maximum(z, 0.0) + jnp.log(1.0 + jnp.exp(-jnp.abs(z)))
            lm = -sp
            ls = z - sp
            if masked:
                lm = jnp.where(keep, lm, 0.0)
                ls = jnp.where(keep, ls, NEG)
            hi = lm.astype(BF16)
            hl_ref[slot, 2 * hh] = hi
            hl_ref[slot, 2 * hh + 1] = (lm - hi.astype(F32)).astype(BF16)
            ls_ref[slot, hh] = ls
            rs_ref[slot, hh] = jnp.sum(lm, axis=1, keepdims=True)

    def stage_apply(step, slot, carry):
        c0 = pl.multiple_of(jnp.minimum(top - step, n_chunks - 1) * t, t)
        v2 = v_ref[0, pl.ds(c0, t), :]
        after4 = jnp.dot(hl_ref[slot].reshape(4 * t, t), later, preferred_element_type=F32)
        out = []
        for hh in range(2):
            car, acc = carry[hh]
            after = after4[(2 * hh) * t:(2 * hh + 1) * t] + after4[(2 * hh + 1) * t:(2 * hh + 2) * t]
            w = jnp.exp(ls_ref[slot, hh] + after + car)
            acc = acc + jnp.dot(w.astype(BF16), v2, preferred_element_type=F32)
            out.append((car + rs_ref[slot, hh], acc))
        return tuple(out)

    z1 = jnp.zeros((t, 1), F32)
    za = jnp.zeros((t, LANES), F32)
    stage_terms(0, 0, True)
    stage_terms(1, 1, True)
    carry = stage_apply(0, 0, ((z1, za), (z1, za)))
    stage_terms(2, 0, False)
    carry = stage_apply(1, 1, carry)

    def pair_body(pp, carry):
        step = 2 * pp
        stage_terms(step + 1, 1, False)
        carry = stage_apply(step, 0, carry)
        stage_terms(step + 2, 0, False)
        return stage_apply(step + 1, 1, carry)

    (_, acc_a), (_, acc_b) = lax.fori_loop(1, npair, pair_body, carry)
    o_ref[0] = jnp.where(lo_half, acc_a, acc_b).astype(o_ref.dtype)


def _stick_breaking(proj3):
    B, S, _ = proj3.shape
    t = SB_T
    qb, kb, vb = COL_QB // LANES, COL_KB // LANES, COL_VB // LANES
    return pl.pallas_call(
        functools.partial(_sb_kernel, t=t, n_chunks=S // t),
        grid=(B, N_PAIRS, S // t),
        in_specs=[
            pl.BlockSpec((1, t, LANES), lambda b, p, i: (b, i, qb + p)),
            pl.BlockSpec((1, S, LANES), lambda b, p, i: (b, 0, kb + p)),
            pl.BlockSpec((1, S, LANES), lambda b, p, i: (b, 0, vb + p)),
        ],
        out_specs=pl.BlockSpec((1, t, LANES), lambda b, p, i: (b, i, p)),
        out_shape=jax.ShapeDtypeStruct((B, S, ATT_W), BF16),
        scratch_shapes=[
            pltpu.VMEM((2, 4, t, t), BF16),
            pltpu.VMEM((2, 2, t, t), F32),
            pltpu.VMEM((2, 2, t, 1), F32),
        ],
        compiler_params=_params(("arbitrary", "arbitrary", "arbitrary")),
    )(proj3, proj3, proj3)


def _layer_norm(r, g, b):
    mu = jnp.mean(r, axis=-1, keepdims=True)
    d = r - mu
    var = jnp.mean(d * d, axis=-1, keepdims=True)
    return d * lax.rsqrt(var + LN_EPS) * g + b


def _split_bf16(v):
    hi = v.astype(BF16)
    return hi, (v - hi.astype(F32)).astype(BF16)


def _merge_kernel(x_ref, ya_ref, yb_ref, gate_ref, wa_ref, wb_ref, wo_ref, g_ref, b_ref,
                  wr_ref, br_ref, h_ref, e_ref, p_ref, *, alpha, d):
    pa = jnp.dot(ya_ref[...], wa_ref[...], preferred_element_type=F32)
    pb = jnp.dot(yb_ref[...], wb_ref[...], preferred_element_type=F32)
    merged = jax.nn.sigmoid(gate_ref[:, :d]) * pa + jax.nn.sigmoid(gate_ref[:, d:]) * pb
    m = jnp.dot(merged.astype(BF16), wo_ref[...], preferred_element_type=F32)
    h = _layer_norm(alpha * x_ref[...] + m, g_ref[...], b_ref[...])
    h_ref[...] = h

    h_hi, h_lo = _split_bf16(h)
    w_hi, w_lo = _split_bf16(wr_ref[...])
    logit = (lax.dot_general(w_hi, h_hi, NT_DIMS, preferred_element_type=F32)
             + lax.dot_general(w_hi, h_lo, NT_DIMS, preferred_element_type=F32)
             + lax.dot_general(w_lo, h_hi, NT_DIMS, preferred_element_type=F32)) + br_ref[...]
    eid = lax.broadcasted_iota(I32, logit.shape, 0)
    vals, ids = [], []
    for _ in range(TOP_K):
        mx = jnp.max(logit, axis=0, keepdims=True)
        am = jnp.min(jnp.where(logit == mx, eid, N_EXPERTS), axis=0, keepdims=True)
        vals.append(mx)
        ids.append(am)
        logit = jnp.where(eid == am, -jnp.inf, logit)
    ex = [jnp.exp(v - vals[0]) for v in vals]
    den = ex[0] + ex[1] + ex[2] + ex[3]
    for k in range(TOP_K):
        e_ref[k:k + 1, :] = ids[k]
        p_ref[k:k + 1, :] = ex[k] / den


def _merge(x2, ya, yb, gates, wa, wb, wo, g, b, wr_t, br, alpha):
    T, D = x2.shape
    tm = MERGE_TM
    row = lambda i: (i, 0)
    fixed = lambda i: (0, 0)
    return pl.pallas_call(
        functools.partial(_merge_kernel, alpha=alpha, d=D),
        grid=(T // tm,),
        in_specs=[
            pl.BlockSpec((tm, D), row),
            pl.BlockSpec((tm, ATT_W), row),
            pl.BlockSpec((tm, ATT_W), row),
            pl.BlockSpec((tm, 2 * D), row),
            pl.BlockSpec((ATT_W, D), fixed),
            pl.BlockSpec((ATT_W, D), fixed),
            pl.BlockSpec((D, D), fixed),
            pl.BlockSpec((1, D), fixed),
            pl.BlockSpec((1, D), fixed),
            pl.BlockSpec((N_EXPERTS, D), fixed),
            pl.BlockSpec((N_EXPERTS, 1), fixed),
        ],
        out_specs=[
            pl.BlockSpec((tm, D), row),
            pl.BlockSpec((TOP_K, tm), lambda i: (0, i)),
            pl.BlockSpec((TOP_K, tm), lambda i: (0, i)),
        ],
        out_shape=[
            jax.ShapeDtypeStruct((T, D), F32),
            jax.ShapeDtypeStruct((TOP_K, T), I32),
            jax.ShapeDtypeStruct((TOP_K, T), F32),
        ],
        compiler_params=_params(("arbitrary",)),
    )(x2, ya, yb, gates, wa, wb, wo, g, b, wr_t, br)


def _sc_gather_rows(table, idx):
    n = idx.shape[0]
    d = table.shape[1]
    info = plsc.get_sparse_core_info()
    n_cores, n_sub = info.num_cores, info.num_subcores
    per_w = n // (n_cores * n_sub)
    c = SC_GATHER_ROWS
    n_g = per_w // c
    assert n == per_w * n_cores * n_sub and per_w == n_g * c and n_g % 2 == 0 and n_g >= 2
    mesh = plsc.VectorSubcoreMesh(core_axis_name="c", subcore_axis_name="s")

    @functools.partial(
        pl.kernel, mesh=mesh, out_type=jax.ShapeDtypeStruct((n, d), table.dtype),
        scratch_types=[pltpu.VMEM((per_w,), I32), pltpu.VMEM((2, c, d), table.dtype),
                       pltpu.SemaphoreType.DMA((2,)), pltpu.SemaphoreType.DMA((2,))])
    def gather_kernel(table_hbm, idx_hbm, out_hbm, idx_v, rows_v, gsem, wsem):
        base = (lax.axis_index("s") * n_cores + lax.axis_index("c")) * per_w
        pltpu.sync_copy(idx_hbm.at[pl.ds(base, per_w)], idx_v)

        def gather(g, b):
            return pltpu.make_async_copy(table_hbm.at[idx_v.at[pl.ds(g * c, c)]], rows_v.at[b],
                                         gsem.at[b])

        def write(g, b):
            return pltpu.make_async_copy(rows_v.at[b], out_hbm.at[pl.ds(base + g * c, c)], wsem.at[b])

        gather(0, 0).start()

        @pl.loop(0, n_g, step=2)
        def _ring(g0):
            for b in range(2):
                g = g0 + b

                @pl.when(g + 1 < n_g)
                def _next():
                    @pl.when(g >= 1)
                    def _buffer_free():
                        write(g - 1, 1 - b).wait()
                    gather(g + 1, 1 - b).start()

                gather(g, b).wait()
                write(g, b).start()

        write(n_g - 2, 0).wait()
        write(n_g - 1, 1).wait()

    return gather_kernel(table, idx)


def _moe_kernel(blk_e_ref, nused_ref, x_ref, wgu_ref, bgu_ref, wdn_ref, bdn_ref, o_ref,
                wgu_s, wdn_s, *, f):
    i = pl.program_id(0)
    nused = nused_ref[0]

    @pl.when(i < nused)
    def _compute():
        changed = jnp.logical_or(i == 0, blk_e_ref[i] != blk_e_ref[jnp.maximum(i - 1, 0)])

        @pl.when(changed)
        def _cast_weights():
            wgu_s[...] = wgu_ref[0].astype(BF16)
            wdn_s[...] = wdn_ref[0].astype(BF16)

        x = x_ref[...].astype(BF16)
        hgu = jnp.dot(x, wgu_s[...], preferred_element_type=F32) + bgu_ref[0]
        a = jnp.minimum(hgu[:, :f], SWIGLU_LIMIT)
        u = jnp.clip(hgu[:, f:], -SWIGLU_LIMIT, SWIGLU_LIMIT)
        glu = a * jax.nn.sigmoid(a * SWIGLU_ALPHA)
        o_ref[...] = jnp.dot(((u + 1.0) * glu).astype(BF16), wdn_s[...],
                             preferred_element_type=F32) + bdn_ref[0]

    @pl.when(i >= nused)
    def _unused_block():
        o_ref[...] = jnp.zeros(o_ref.shape, o_ref.dtype)


def _moe_ffn(xs, blk_e, nused, w_gu, b_gu, w_dn, b_dn):
    P, D = xs.shape
    E, _, F2 = w_gu.shape
    f = F2 // 2
    blk = MOE_BLK
    nb = P // blk
    grid_spec = pltpu.PrefetchScalarGridSpec(
        num_scalar_prefetch=2,
        grid=(nb,),
        in_specs=[
            pl.BlockSpec((blk, D), lambda i, be, nu: (i, 0)),
            pl.BlockSpec((1, D, F2), lambda i, be, nu: (be[i], 0, 0)),
            pl.BlockSpec((1, 1, F2), lambda i, be, nu: (be[i], 0, 0)),
            pl.BlockSpec((1, f, D), lambda i, be, nu: (be[i], 0, 0)),
            pl.BlockSpec((1, 1, D), lambda i, be, nu: (be[i], 0, 0)),
        ],
        out_specs=pl.BlockSpec((blk, D), lambda i, be, nu: (i, 0)),
        scratch_shapes=[
            pltpu.VMEM((D, F2), BF16),
            pltpu.VMEM((f, D), BF16),
        ],
    )
    return pl.pallas_call(
        functools.partial(_moe_kernel, f=f),
        grid_spec=grid_spec,
        out_shape=jax.ShapeDtypeStruct((P, D), F32),
        compiler_params=_params(("arbitrary",)),
    )(blk_e, nused, xs, w_gu, b_gu.reshape(E, 1, F2), w_dn, b_dn.reshape(E, 1, D))


def _comb_kernel(h_ref, p_ref, y_ref, g_ref, b_ref, o_ref, *, alpha):
    gate = p_ref[...]
    fsum = ((y_ref[0] * gate[:, 0:1] + y_ref[1] * gate[:, 1:2])
            + (y_ref[2] * gate[:, 2:3] + y_ref[3] * gate[:, 3:4]))
    o_ref[...] = _layer_norm(alpha * h_ref[...] + fsum, g_ref[...], b_ref[...])


def _combine(h, y4, top_p, g, b, alpha):
    T, D = h.shape
    tm = COMB_TM
    return pl.pallas_call(
        functools.partial(_comb_kernel, alpha=alpha),
        grid=(T // tm,),
        in_specs=[
            pl.BlockSpec((tm, D), lambda i: (i, 0)),
            pl.BlockSpec((tm, TOP_K), lambda i: (i, 0)),
            pl.BlockSpec((TOP_K, tm, D), lambda i: (0, i, 0)),
            pl.BlockSpec((1, D), lambda i: (0, 0)),
            pl.BlockSpec((1, D), lambda i: (0, 0)),
        ],
        out_specs=pl.BlockSpec((tm, D), lambda i: (i, 0)),
        out_shape=jax.ShapeDtypeStruct((T, D), F32),
        compiler_params=_params(("arbitrary",)),
    )(h, top_p.T, y4, g, b)


def _route(top_e, blk):
    K, T = top_e.shape
    N = K * T
    flat_e = top_e.reshape(N)
    experts = jnp.arange(N_EXPERTS, dtype=I32)
    order = jnp.argsort(flat_e, stable=True).astype(I32)
    inv = jnp.argsort(order).astype(I32)
    onehot = flat_e[:, None] == experts[None, :]
    counts = jnp.sum(onehot, axis=0, dtype=I32)
    padded = (counts + blk - 1) // blk * blk
    pends = jnp.cumsum(padded)
    offs = jnp.cumsum(counts) - counts
    shift = (pends - padded) - offs
    pos = inv + jnp.sum(jnp.where(onehot, shift[None, :], 0), axis=1, dtype=I32)
    P = N + N_EXPERTS * blk
    nb = P // blk
    blk_start = jnp.arange(nb, dtype=I32) * blk
    blk_e = jnp.minimum(jnp.sum(pends[None, :] <= blk_start[:, None], axis=1, dtype=I32), N_EXPERTS - 1)
    j = (blk_start - shift[blk_e])[:, None] + jnp.arange(blk, dtype=I32)[None, :]
    valid = j < (offs + counts)[blk_e][:, None]
    src = order[jnp.clip(j, 0, N - 1)]
    row_tok = jnp.where(valid, src % T, j % T).reshape(P)
    nused = (pends[-1:] // blk).astype(I32)
    return blk_e, nused, row_tok, pos.reshape(K, T)


def _projection_weights(w_in_l):
    sizes = (ATT_W, ATT_W, ATT_W, IDX_HEADS * IDX_DIM, IDX_DIM, IDX_HEADS, ATT_W, ATT_W, ATT_W)
    offs = np.concatenate([[0], np.cumsum(sizes)])
    qa, ka, va, qi, ki, wi, qb, kb, vb = (w_in_l[:, offs[n]:offs[n + 1]] for n in range(9))
    pad_wi = jnp.zeros((w_in_l.shape[0], LANES - IDX_HEADS), w_in_l.dtype)
    w_att = jnp.concatenate([qa, ka, qi, qb, kb, vb, ki, ki, wi, pad_wi], axis=1).astype(BF16)
    w_va_t = va.T.astype(BF16)
    w_gate = w_in_l[:, offs[9]:].astype(BF16)
    return w_att, w_va_t, w_gate


def kernel(x, w_in, w_branch_a, w_branch_b, w_out, rel_bias, ln1_g, ln1_b, w_router, b_router,
           w_gate_up, b_gate_up, w_down, b_down, ln2_g, ln2_b):
    B, S, D = x.shape
    depth = w_in.shape[0]
    alpha = (2 * depth) ** 0.25
    T = B * S
    h = x.reshape(T, D)
    for l in range(depth):
        w_att, w_va_t, w_gate = _projection_weights(w_in[l])
        proj = _matmul(h, w_att, BF16, min(T, 1024), PROJ_TN).reshape(B, S, ATT_COLS)
        vt = _matmul_t(h, w_va_t, DSA_BLK).reshape(B, S // DSA_BLK, ATT_W, DSA_BLK)
        gates = _matmul(h, w_gate, F32, min(T, 1024), 1024)
        ya = _dsa(proj, vt, rel_bias).reshape(T, ATT_W)
        yb = _stick_breaking(proj).reshape(T, ATT_W)
        h1, top_e, top_p = _merge(
            h, ya, yb, gates, w_branch_a[l].astype(BF16), w_branch_b[l].astype(BF16),
            w_out[l].astype(BF16), ln1_g[l].reshape(1, D), ln1_b[l].reshape(1, D),
            w_router[l].T, b_router[l].reshape(N_EXPERTS, 1), alpha)
        blk_e, nused, row_tok, pos = _route(top_e, MOE_BLK)
        xs = _sc_gather_rows(h1, row_tok)
        ys = _moe_ffn(xs, blk_e, nused, w_gate_up[l], b_gate_up[l], w_down[l], b_down[l])
        y4 = _sc_gather_rows(ys, pos.reshape(TOP_K * T)).reshape(TOP_K, T, D)
        h = _combine(h1, y4, top_p, ln2_g[l].reshape(1, D), ln2_b[l].reshape(1, D), alpha)
    return h.reshape(B, S, D)
```

```python
import functools
import math

import numpy as np
import jax
import jax.numpy as jnp
from jax import lax
from jax.experimental import pallas as pl
from jax.experimental.pallas import tpu as pltpu
from jax.experimental.pallas import tpu_sc as plsc

F32 = jnp.float32
BF16 = jnp.bfloat16
I32 = jnp.int32

A_HEADS = 8
HEAD_DIM = 64
ATT_W = A_HEADS * HEAD_DIM
IDX_HEADS = 8
IDX_DIM = 64
IDX_SCALE = (IDX_HEADS * IDX_DIM) ** -0.5
TOPK_MAX = 256
N_BUCKETS = 32
MAX_DISTANCE = 128
N_EXPERTS = 32
TOP_K = 4
SWIGLU_LIMIT = 7.0
SWIGLU_ALPHA = 1.702
LN_EPS = 1e-5
QK_SCALE = HEAD_DIM ** -0.5

LANES = 128
SUBLANES = 8
HALF = LANES // 2
N_PAIRS = A_HEADS // 2
VMEM_LIMIT = 56 * 1024 * 1024

DSA_BLK = 256
SB_T = 256
MERGE_TM = 512
MOE_BLK = 512
COMB_TM = 256
SC_GATHER_ROWS = 32
BISECT_CAP = 24
NEG = -1e30

COL_QA, COL_KA, COL_QI, COL_QB, COL_KB, COL_VB = (g * ATT_W for g in range(6))
COL_KK = 6 * ATT_W
COL_WI = COL_KK + LANES
ATT_COLS = COL_WI + LANES
PROJ_TN = ATT_COLS // 2

NT_DIMS = (((1,), (1,)), ((), ()))


def _params(sem, vmem=VMEM_LIMIT):
    return pltpu.CompilerParams(dimension_semantics=sem, vmem_limit_bytes=vmem)


def _mm_kernel(x_ref, w_ref, o_ref, xb_ref):
    @pl.when(pl.program_id(1) == 0)
    def _cast():
        xb_ref[...] = x_ref[...].astype(BF16)

    o_ref[...] = jnp.dot(xb_ref[...], w_ref[...], preferred_element_type=F32).astype(o_ref.dtype)


def _matmul(x, w, out_dtype, tm, tn):
    M, K = x.shape
    N = w.shape[1]
    return pl.pallas_call(
        _mm_kernel,
        grid=(M // tm, N // tn),
        in_specs=[pl.BlockSpec((tm, K), lambda i, j: (i, 0)),
                  pl.BlockSpec((K, tn), lambda i, j: (0, j))],
        out_specs=pl.BlockSpec((tm, tn), lambda i, j: (i, j)),
        out_shape=jax.ShapeDtypeStruct((M, N), out_dtype),
        scratch_shapes=[pltpu.VMEM((tm, K), BF16)],
        compiler_params=_params(("arbitrary", "arbitrary")),
    )(x, w)


def _mm_t_kernel(x_ref, w_ref, o_ref):
    o_ref[0] = lax.dot_general(w_ref[...], x_ref[...].astype(BF16), NT_DIMS,
                               preferred_element_type=F32).astype(o_ref.dtype)


def _matmul_t(x, w_t, tm):
    M, K = x.shape
    N = w_t.shape[0]
    return pl.pallas_call(
        _mm_t_kernel,
        grid=(M // tm,),
        in_specs=[pl.BlockSpec((tm, K), lambda i: (i, 0)),
                  pl.BlockSpec((N, K), lambda i: (0, 0))],
        out_specs=pl.BlockSpec((1, N, tm), lambda i: (i, 0, 0)),
        out_shape=jax.ShapeDtypeStruct((M // tm, N, tm), BF16),
        compiler_params=_params(("arbitrary",)),
    )(x, w_t)


def _t5_bucket_np(n):
    n = np.maximum(n, 0)
    max_exact = N_BUCKETS // 2
    nf = np.maximum(n, 1).astype(np.float32)
    large = max_exact + (np.log(nf / max_exact) / math.log(MAX_DISTANCE / max_exact)
                         * (N_BUCKETS - max_exact)).astype(np.int32)
    large = np.minimum(large, N_BUCKETS - 1)
    return np.where(n < max_exact, n, large).astype(np.int32)


def _dsa_n_off(blk):
    return 2 + -(-MAX_DISTANCE // blk)


def _dsa_bucket_tiles(blk):
    n_off = _dsa_n_off(blk)
    j = np.arange(blk)[None, :, None]
    i = np.arange(blk)[None, None, :]
    o = np.arange(n_off)[:, None, None]
    return _t5_bucket_np(i - j + blk * (n_off - 1 - o))


def _dsa_kernel(bucket_ref, relb_ref, q_ref, k_ref, vt_ref, qi_ref, kk_ref, wi_ref, o_ref,
                sc_ref, bias_ref, m_ref, l_ref, acc_ref, st_ref, mx_ref,
                *, blk, n_chunks, n_sel, n_off):
    b = pl.program_id(0)
    i = pl.program_id(1)
    q0 = i * blk
    nck = i + 1
    groups = blk // SUBLANES

    @pl.when(jnp.logical_and(b == 0, i == 0))
    def _build_bias():
        def head_body(h, _):
            for o in range(n_off):
                for rb in range(blk // LANES):
                    for cb in range(blk // LANES):
                        rs = slice(rb * LANES, (rb + 1) * LANES)
                        cs = slice(cb * LANES, (cb + 1) * LANES)
                        bk = bucket_ref[o, rs, cs]

                        def bucket_body(n, acc):
                            return jnp.where(bk == n, relb_ref[n, h], acc)

                        bias_ref[h, o, rs, cs] = lax.fori_loop(
                            0, N_BUCKETS, bucket_body, jnp.zeros((LANES, LANES), F32))
            return 0

        lax.fori_loop(0, A_HEADS, head_body, 0)

    lane = lax.broadcasted_iota(I32, (blk, LANES), 1)
    lo_half = lane < HALF
    krow = lax.broadcasted_iota(I32, (blk, blk), 0)
    qpos = q0 + lax.broadcasted_iota(I32, (1, blk), 1)

    def pair_split(ref, scale):
        out = []
        for p in range(N_PAIRS):
            v = ref[0, :, p * LANES:(p + 1) * LANES].astype(F32)
            if scale != 1.0:
                v = v * scale
            out.append(jnp.where(lo_half, v, 0.0).astype(BF16))
            out.append(jnp.where(lo_half, 0.0, v).astype(BF16))
        return out

    wi_t = wi_ref[0].astype(F32).T
    wrow = [wi_t[h:h + 1, :] * IDX_SCALE for h in range(IDX_HEADS)]
    qi_m = pair_split(qi_ref, 1.0)

    def score_chunk(c, _):
        c0 = pl.multiple_of(c * blk, blk)
        kk = kk_ref[0, pl.ds(c0, blk), :]
        acc = jnp.zeros((blk, blk), F32)
        for h in range(IDX_HEADS):
            s = lax.dot_general(kk, qi_m[h], NT_DIMS, preferred_element_type=F32)
            acc = acc + wrow[h] * jnp.maximum(s, 0.0)
        sc_ref[c] = jnp.where(c0 + krow <= qpos, acc, -jnp.inf)
        return 0

    lax.fori_loop(0, nck, score_chunk, 0)

    kt = jnp.minimum(qpos + 1, n_sel).astype(F32)

    def fold(fn, init):
        def body(c, acc):
            return fn(acc, sc_ref[c])
        return lax.fori_loop(0, nck, body, init)

    def part(x, op):
        return op(x.reshape(groups, SUBLANES, blk), axis=0)

    def fin(x, op):
        return op(x, axis=0, keepdims=True)

    zeros8 = jnp.zeros((SUBLANES, blk), F32)
    pinf8 = jnp.full((SUBLANES, blk), jnp.inf, F32)

    def count_ge(th):
        return fin(fold(lambda a, s: a + part(jnp.where(s >= th, 1.0, 0.0), jnp.sum), zeros8),
                   jnp.sum)

    mn, mx = fold(lambda a, s: (
        jnp.minimum(a[0], part(jnp.where(s == -jnp.inf, jnp.inf, s), jnp.min)),
        jnp.maximum(a[1], part(s, jnp.max))), (pinf8, -pinf8))
    rmin = fin(mn, jnp.min)
    rmax = fin(mx, jnp.max)

    def bis_cond(st):
        it, lo, hi, clo = st
        return jnp.logical_and(it < BISECT_CAP, jnp.max(jnp.abs(clo - kt)) > 0.0)

    def bis_body(st):
        it, lo, hi, clo = st
        mid = 0.5 * lo + 0.5 * hi
        c = count_ge(mid)
        active = clo != kt
        up = jnp.logical_and(active, c >= kt)
        dn = jnp.logical_and(active, c < kt)
        return (it + 1, jnp.where(up, mid, lo), jnp.where(dn, mid, hi), jnp.where(up, c, clo))

    _, lo, _, _ = lax.while_loop(
        bis_cond, bis_body, (jnp.int32(0), rmin, rmax + 1.0, (qpos + 1).astype(F32)))

    def stats(lo_):
        a_ = fin(fold(lambda a, s: jnp.minimum(a, part(jnp.where(s >= lo_, s, jnp.inf), jnp.min)),
                      pinf8), jnp.min)
        cg, ct, nx = fold(
            lambda a, s: (a[0] + part(jnp.where(s > a_, 1.0, 0.0), jnp.sum),
                               a[1] + part(jnp.where(s == a_, 1.0, 0.0), jnp.sum),
                               jnp.minimum(a[2], part(jnp.where(s > a_, s, jnp.inf), jnp.min))),
            (zeros8, zeros8, pinf8))
        return a_, fin(cg, jnp.sum), fin(ct, jnp.sum), fin(nx, jnp.min)

    def fin_cond(st):
        return st[0]

    def fin_body(st):
        _, lo_, _, _ = st
        a_, cgt_, nt_, nxt_ = stats(lo_)
        bad = cgt_ >= kt
        return (jnp.max(jnp.where(bad, 1.0, 0.0)) > 0.0, jnp.where(bad, nxt_, a_), cgt_, nt_)

    _, a, cgt, nties = lax.while_loop(fin_cond, fin_body, (jnp.bool_(True), lo, kt, kt))
    need = kt - cgt
    excess = jnp.max(jnp.where(nties > need, 1.0, 0.0)) > 0.0

    def mask_plain():
        def body(c, _):
            sc_ref[c] = jnp.where(sc_ref[c] >= a, 0.0, NEG)
            return 0
        lax.fori_loop(0, nck, body, 0)

    def mask_ties():
        upto = (krow >= lax.broadcasted_iota(I32, (blk, blk), 1)).astype(BF16)

        def body(c, seen):
            s = sc_ref[c]
            tie = s == a
            rank = jnp.dot(upto, jnp.where(tie, 1.0, 0.0).astype(BF16),
                           preferred_element_type=F32) + seen
            sel = jnp.logical_or(s > a, jnp.logical_and(tie, rank <= need))
            sc_ref[c] = jnp.where(sel, 0.0, NEG)
            return rank[blk - 1:blk, :]

        lax.fori_loop(0, nck, body, jnp.zeros((1, blk), F32))

    lax.cond(excess, mask_ties, mask_plain)

    odd = lax.rem(nck, 2)
    npair = (nck + odd) // 2

    @pl.when(odd == 1)
    def _mask_extra_chunk():
        sc_ref[nck] = jnp.full((blk, blk), NEG, F32)

    m_ref[...] = jnp.full(m_ref.shape, NEG, F32)
    l_ref[...] = jnp.zeros(l_ref.shape, F32)
    acc_ref[...] = jnp.zeros(acc_ref.shape, F32)
    q_m = pair_split(q_ref, QK_SCALE)

    def stage_logits(c, slot):
        c = jnp.minimum(c, n_chunks - 1)
        c0 = pl.multiple_of(c * blk, blk)
        madd = sc_ref[c]
        o_idx = jnp.clip(c - i + (n_off - 1), 0, n_off - 1)
        for p in range(N_PAIRS):
            k2 = k_ref[0, pl.ds(c0, blk), p * LANES:(p + 1) * LANES]
            for hh in range(2):
                h = 2 * p + hh
                s = lax.dot_general(k2, q_m[h], NT_DIMS, preferred_element_type=F32)
                s = s + bias_ref[h, o_idx] + madd
                st_ref[slot, h] = s
                mx_ref[slot, h] = jnp.max(s, axis=0, keepdims=True)

    def stage_values(c, slot):
        for p in range(N_PAIRS):
            vt2 = vt_ref[0, c, p * LANES:(p + 1) * LANES, :]
            for hh in range(2):
                h = 2 * p + hh
                m_old = m_ref[h]
                m_new = jnp.maximum(m_old, mx_ref[slot, h])
                alpha = jnp.exp(m_old - m_new)
                pexp = jnp.exp(st_ref[slot, h] - m_new)
                l_ref[h] = alpha * l_ref[h] + jnp.sum(pexp, axis=0, keepdims=True)
                acc_ref[h] = alpha * acc_ref[h] + jnp.dot(vt2, pexp.astype(BF16),
                                                          preferred_element_type=F32)
                m_ref[h] = m_new

    stage_logits(0, 0)

    def att_pair(pp, _):
        c = 2 * pp
        stage_logits(c + 1, 1)
        stage_values(c, 0)
        stage_logits(c + 2, 0)
        stage_values(c + 1, 1)
        return 0

    lax.fori_loop(0, npair, att_pair, 0)

    lo_rows = lax.broadcasted_iota(I32, (LANES, blk), 0) < HALF
    for p in range(N_PAIRS):
        oa = acc_ref[2 * p] / l_ref[2 * p]
        ob = acc_ref[2 * p + 1] / l_ref[2 * p + 1]
        o_ref[0, :, p * LANES:(p + 1) * LANES] = jnp.where(lo_rows, oa, ob).T.astype(o_ref.dtype)


def _dsa(proj3, vt4, rel_bias):
    B, S, _ = proj3.shape
    blk = DSA_BLK
    n_off = _dsa_n_off(blk)
    n_sel = min(TOPK_MAX, S // 4)
    bucket = jnp.asarray(_dsa_bucket_tiles(blk))
    n_chunks = S // blk
    assert S % blk == 0 and n_chunks % 2 == 0, "DSA consumes key chunks in pairs"
    kern = functools.partial(_dsa_kernel, blk=blk, n_chunks=n_chunks, n_sel=n_sel, n_off=n_off)
    return pl.pallas_call(
        kern,
        grid=(B, S // blk),
        in_specs=[
            pl.BlockSpec((n_off, blk, blk), lambda b, i: (0, 0, 0)),
            pl.BlockSpec(memory_space=pltpu.SMEM),
            pl.BlockSpec((1, blk, ATT_W), lambda b, i: (b, i, COL_QA // ATT_W)),
            pl.BlockSpec((1, S, ATT_W), lambda b, i: (b, 0, COL_KA // ATT_W)),
            pl.BlockSpec((1, S // blk, ATT_W, blk), lambda b, i: (b, 0, 0, 0)),
            pl.BlockSpec((1, blk, ATT_W), lambda b, i: (b, i, COL_QI // ATT_W)),
            pl.BlockSpec((1, S, LANES), lambda b, i: (b, 0, COL_KK // LANES)),
            pl.BlockSpec((1, blk, LANES), lambda b, i: (b, i, COL_WI // LANES)),
        ],
        out_specs=pl.BlockSpec((1, blk, ATT_W), lambda b, i: (b, i, 0)),
        out_shape=jax.ShapeDtypeStruct((B, S, ATT_W), BF16),
        scratch_shapes=[
            pltpu.VMEM((S // blk, blk, blk), F32),
            pltpu.VMEM((A_HEADS, n_off, blk, blk), F32),
            pltpu.VMEM((A_HEADS, 1, blk), F32),
            pltpu.VMEM((A_HEADS, 1, blk), F32),
            pltpu.VMEM((A_HEADS, LANES, blk), F32),
            pltpu.VMEM((2, A_HEADS, blk, blk), F32),
            pltpu.VMEM((2, A_HEADS, 1, blk), F32),
        ],
        compiler_params=_params(("arbitrary", "arbitrary")),
    )(bucket, rel_bias, proj3, proj3, vt4, proj3, proj3, proj3)


def _sb_kernel(q_ref, k_ref, v_ref, o_ref, hl_ref, z_ref, *, t, n_chunks):
    i = pl.program_id(2)
    n = i + 1
    odd = lax.rem(n, 2)
    top = i + odd
    npair = (n + odd) // 2
    lane = lax.broadcasted_iota(I32, (t, LANES), 1)
    lo_half = lane < HALF
    q2 = q_ref[0].astype(F32) * QK_SCALE
    q_m = (jnp.where(lo_half, q2, 0.0).astype(BF16), jnp.where(lo_half, 0.0, q2).astype(BF16))
    r = lax.broadcasted_iota(I32, (t, t), 0)
    cidx = lax.broadcasted_iota(I32, (t, t), 1)
    neg_from = jnp.where(r >= cidx, -1.0, 0.0).astype(BF16)
    diff = cidx - r

    def stage_terms(step, slot, masked):
        chunk = top - step
        c0 = pl.multiple_of(jnp.clip(chunk, 0, n_chunks - 1) * t, t)
        k2 = k_ref[0, pl.ds(c0, t), :]
        if masked:
            keep = diff < (i - chunk) * t
        for hh in range(2):
            z = lax.dot_general(q_m[hh], k2, NT_DIMS, preferred_element_type=F32)
            sp = jnp.maximum(z, 0.0) + jnp.log(1.0 + jnp.exp(-jnp.abs(z)))
            if masked:
                sp = jnp.where(keep, sp, 0.0)
                z = jnp.where(keep, z, NEG)
            hi = sp.astype(BF16)
            hl_ref[slot, 2 * hh] = hi
            hl_ref[slot, 2 * hh + 1] = (sp - hi.astype(F32)).astype(BF16)
            z_ref[slot, hh] = z

    def stage_apply(step, slot, carry):
        c0 = pl.multiple_of(jnp.minimum(top - step, n_chunks - 1) * t, t)
        v2 = v_ref[0, pl.ds(c0, t), :]
        cum4 = jnp.dot(hl_ref[slot].reshape(4 * t, t), neg_from, preferred_element_type=F32)
        out = []
        for hh in range(2):
            car, acc = carry[hh]
            cum = cum4[(2 * hh) * t:(2 * hh + 1) * t] + cum4[(2 * hh + 1) * t:(2 * hh + 2) * t]
            w = jnp.exp(z_ref[slot, hh] + cum + car)
            acc = acc + jnp.dot(w.astype(BF16), v2, preferred_element_type=F32)
            out.append((car + cum[:, 0:1], acc))
        return tuple(out)

    z1 = jnp.zeros((t, 1), F32)
    za = jnp.zeros((t, LANES), F32)
    stage_terms(0, 0, True)
    stage_terms(1, 1, True)
    carry = stage_apply(0, 0, ((z1, za), (z1, za)))
    stage_terms(2, 0, False)
    carry = stage_apply(1, 1, carry)

    def pair_body(pp, carry):
        step = 2 * pp
        stage_terms(step + 1, 1, False)
        carry = stage_apply(step, 0, carry)
        stage_terms(step + 2, 0, False)
        return stage_apply(step + 1, 1, carry)

    (_, acc_a), (_, acc_b) = lax.fori_loop(1, npair, pair_body, carry)
    o_ref[0] = jnp.where(lo_half, acc_a, acc_b).astype(o_ref.dtype)


def _stick_breaking(proj3):
    B, S, _ = proj3.shape
    t = SB_T
    qb, kb, vb = COL_QB // LANES, COL_KB // LANES, COL_VB // LANES
    return pl.pallas_call(
        functools.partial(_sb_kernel, t=t, n_chunks=S // t),
        grid=(B, N_PAIRS, S // t),
        in_specs=[
            pl.BlockSpec((1, t, LANES), lambda b, p, i: (b, i, qb + p)),
            pl.BlockSpec((1, S, LANES), lambda b, p, i: (b, 0, kb + p)),
            pl.BlockSpec((1, S, LANES), lambda b, p, i: (b, 0, vb + p)),
        ],
        out_specs=pl.BlockSpec((1, t, LANES), lambda b, p, i: (b, i, p)),
        out_shape=jax.ShapeDtypeStruct((B, S, ATT_W), BF16),
        scratch_shapes=[
            pltpu.VMEM((2, 4, t, t), BF16),
            pltpu.VMEM((2, 2, t, t), F32),
        ],
        compiler_params=_params(("arbitrary", "arbitrary", "arbitrary")),
    )(proj3, proj3, proj3)


def _layer_norm(r, g, b):
    mu = jnp.mean(r, axis=-1, keepdims=True)
    d = r - mu
    var = jnp.mean(d * d, axis=-1, keepdims=True)
    return d * lax.rsqrt(var + LN_EPS) * g + b


def _split_bf16(v):
    hi = v.astype(BF16)
    return hi, (v - hi.astype(F32)).astype(BF16)


def _merge_kernel(x_ref, ya_ref, yb_ref, wg_ref, wa_ref, wb_ref, wo_ref, g_ref, b_ref,
                  wr_ref, br_ref, h_ref, e_ref, p_ref, *, alpha, d):
    pa = jnp.dot(ya_ref[...], wa_ref[...], preferred_element_type=F32)
    pb = jnp.dot(yb_ref[...], wb_ref[...], preferred_element_type=F32)
    gates = jnp.dot(x_ref[...].astype(BF16), wg_ref[...], preferred_element_type=F32)
    merged = jax.nn.sigmoid(gates[:, :d]) * pa + jax.nn.sigmoid(gates[:, d:]) * pb
    m = jnp.dot(merged.astype(BF16), wo_ref[...], preferred_element_type=F32)
    h = _layer_norm(alpha * x_ref[...] + m, g_ref[...], b_ref[...])
    h_ref[...] = h

    h_hi, h_lo = _split_bf16(h)
    w_hi, w_lo = _split_bf16(wr_ref[...])
    logit = (lax.dot_general(w_hi, h_hi, NT_DIMS, preferred_element_type=F32)
             + lax.dot_general(w_hi, h_lo, NT_DIMS, preferred_element_type=F32)
             + lax.dot_general(w_lo, h_hi, NT_DIMS, preferred_element_type=F32)) + br_ref[...]
    eid = lax.broadcasted_iota(I32, logit.shape, 0)
    vals, ids = [], []
    for _ in range(TOP_K):
        mx = jnp.max(logit, axis=0, keepdims=True)
        am = jnp.min(jnp.where(logit == mx, eid, N_EXPERTS), axis=0, keepdims=True)
        vals.append(mx)
        ids.append(am)
        logit = jnp.where(eid == am, -jnp.inf, logit)
    ex = [jnp.exp(v - vals[0]) for v in vals]
    den = ex[0] + ex[1] + ex[2] + ex[3]
    for k in range(TOP_K):
        e_ref[k:k + 1, :] = ids[k]
        p_ref[k:k + 1, :] = ex[k] / den


def _merge(x2, ya, yb, wg, wa, wb, wo, g, b, wr_t, br, alpha):
    T, D = x2.shape
    tm = MERGE_TM
    row = lambda i: (i, 0)
    fixed = lambda i: (0, 0)
    return pl.pallas_call(
        functools.partial(_merge_kernel, alpha=alpha, d=D),
        grid=(T // tm,),
        in_specs=[
            pl.BlockSpec((tm, D), row),
            pl.BlockSpec((tm, ATT_W), row),
            pl.BlockSpec((tm, ATT_W), row),
            pl.BlockSpec((D, 2 * D), fixed),
            pl.BlockSpec((ATT_W, D), fixed),
            pl.BlockSpec((ATT_W, D), fixed),
            pl.BlockSpec((D, D), fixed),
            pl.BlockSpec((1, D), fixed),
            pl.BlockSpec((1, D), fixed),
            pl.BlockSpec((N_EXPERTS, D), fixed),
            pl.BlockSpec((N_EXPERTS, 1), fixed),
        ],
        out_specs=[
            pl.BlockSpec((tm, D), row),
            pl.BlockSpec((TOP_K, tm), lambda i: (0, i)),
            pl.BlockSpec((TOP_K, tm), lambda i: (0, i)),
        ],
        out_shape=[
            jax.ShapeDtypeStruct((T, D), F32),
            jax.ShapeDtypeStruct((TOP_K, T), I32),
            jax.ShapeDtypeStruct((TOP_K, T), F32),
        ],
        compiler_params=_params(("arbitrary",)),
    )(x2, ya, yb, wg, wa, wb, wo, g, b, wr_t, br)


def _sc_gather_rows(table, idx):
    n = idx.shape[0]
    d = table.shape[1]
    info = plsc.get_sparse_core_info()
    n_cores, n_sub = info.num_cores, info.num_subcores
    per_w = n // (n_cores * n_sub)
    c = SC_GATHER_ROWS
    n_g = per_w // c
    assert n == per_w * n_cores * n_sub and per_w == n_g * c and n_g % 2 == 0 and n_g >= 2
    mesh = plsc.VectorSubcoreMesh(core_axis_name="c", subcore_axis_name="s")

    @functools.partial(
        pl.kernel, mesh=mesh, out_type=jax.ShapeDtypeStruct((n, d), table.dtype),
        scratch_types=[pltpu.VMEM((per_w,), I32), pltpu.VMEM((2, c, d), table.dtype),
                       pltpu.SemaphoreType.DMA((2,)), pltpu.SemaphoreType.DMA((2,))])
    def gather_kernel(table_hbm, idx_hbm, out_hbm, idx_v, rows_v, gsem, wsem):
        base = (lax.axis_index("s") * n_cores + lax.axis_index("c")) * per_w
        pltpu.sync_copy(idx_hbm.at[pl.ds(base, per_w)], idx_v)

        def gather(g, b):
            return pltpu.make_async_copy(table_hbm.at[idx_v.at[pl.ds(g * c, c)]], rows_v.at[b],
                                         gsem.at[b])

        def write(g, b):
            return pltpu.make_async_copy(rows_v.at[b], out_hbm.at[pl.ds(base + g * c, c)], wsem.at[b])

        gather(0, 0).start()

        @pl.loop(0, n_g, step=2)
        def _ring(g0):
            for b in range(2):
                g = g0 + b

                @pl.when(g + 1 < n_g)
                def _next():
                    @pl.when(g >= 1)
                    def _buffer_free():
                        write(g - 1, 1 - b).wait()
                    gather(g + 1, 1 - b).start()

                gather(g, b).wait()
                write(g, b).start()

        write(n_g - 2, 0).wait()
        write(n_g - 1, 1).wait()

    return gather_kernel(table, idx)


def _moe_kernel(blk_e_ref, nused_ref, x_ref, wgu_ref, bgu_ref, wdn_ref, bdn_ref, o_ref,
                wgu_s, wdn_s, *, f):
    i = pl.program_id(0)
    nused = nused_ref[0]

    @pl.when(i < nused)
    def _compute():
        changed = jnp.logical_or(i == 0, blk_e_ref[i] != blk_e_ref[jnp.maximum(i - 1, 0)])

        @pl.when(changed)
        def _cast_weights():
            wgu_s[...] = wgu_ref[0].astype(BF16)
            wdn_s[...] = wdn_ref[0].astype(BF16)

        x = x_ref[...].astype(BF16)
        hgu = jnp.dot(x, wgu_s[...], preferred_element_type=F32) + bgu_ref[0]
        a = jnp.minimum(hgu[:, :f], SWIGLU_LIMIT)
        u = jnp.clip(hgu[:, f:], -SWIGLU_LIMIT, SWIGLU_LIMIT)
        glu = a * jax.nn.sigmoid(a * SWIGLU_ALPHA)
        o_ref[...] = jnp.dot(((u + 1.0) * glu).astype(BF16), wdn_s[...],
                             preferred_element_type=F32) + bdn_ref[0]

    @pl.when(i >= nused)
    def _unused_block():
        o_ref[...] = jnp.zeros(o_ref.shape, o_ref.dtype)


def _moe_ffn(xs, blk_e, nused, w_gu, b_gu, w_dn, b_dn):
    P, D = xs.shape
    E, _, F2 = w_gu.shape
    f = F2 // 2
    blk = MOE_BLK
    nb = P // blk
    grid_spec = pltpu.PrefetchScalarGridSpec(
        num_scalar_prefetch=2,
        grid=(nb,),
        in_specs=[
            pl.BlockSpec((blk, D), lambda i, be, nu: (i, 0)),
            pl.BlockSpec((1, D, F2), lambda i, be, nu: (be[i], 0, 0)),
            pl.BlockSpec((1, 1, F2), lambda i, be, nu: (be[i], 0, 0)),
            pl.BlockSpec((1, f, D), lambda i, be, nu: (be[i], 0, 0)),
            pl.BlockSpec((1, 1, D), lambda i, be, nu: (be[i], 0, 0)),
        ],
        out_specs=pl.BlockSpec((blk, D), lambda i, be, nu: (i, 0)),
        scratch_shapes=[
            pltpu.VMEM((D, F2), BF16),
            pltpu.VMEM((f, D), BF16),
        ],
    )
    return pl.pallas_call(
        functools.partial(_moe_kernel, f=f),
        grid_spec=grid_spec,
        out_shape=jax.ShapeDtypeStruct((P, D), F32),
        compiler_params=_params(("arbitrary",)),
    )(blk_e, nused, xs, w_gu, b_gu.reshape(E, 1, F2), w_dn, b_dn.reshape(E, 1, D))


def _comb_kernel(h_ref, p_ref, y_ref, g_ref, b_ref, o_ref, *, alpha):
    gate = p_ref[...]
    fsum = ((y_ref[0] * gate[:, 0:1] + y_ref[1] * gate[:, 1:2])
            + (y_ref[2] * gate[:, 2:3] + y_ref[3] * gate[:, 3:4]))
    o_ref[...] = _layer_norm(alpha * h_ref[...] + fsum, g_ref[...], b_ref[...])


def _combine(h, y4, top_p, g, b, alpha):
    T, D = h.shape
    tm = COMB_TM
    return pl.pallas_call(
        functools.partial(_comb_kernel, alpha=alpha),
        grid=(T // tm,),
        in_specs=[
            pl.BlockSpec((tm, D), lambda i: (i, 0)),
            pl.BlockSpec((tm, TOP_K), lambda i: (i, 0)),
            pl.BlockSpec((TOP_K, tm, D), lambda i: (0, i, 0)),
            pl.BlockSpec((1, D), lambda i: (0, 0)),
            pl.BlockSpec((1, D), lambda i: (0, 0)),
        ],
        out_specs=pl.BlockSpec((tm, D), lambda i: (i, 0)),
        out_shape=jax.ShapeDtypeStruct((T, D), F32),
        compiler_params=_params(("arbitrary",)),
    )(h, top_p.T, y4, g, b)


def _route(top_e, blk):
    K, T = top_e.shape
    N = K * T
    flat_e = top_e.reshape(N)
    experts = jnp.arange(N_EXPERTS, dtype=I32)
    order = jnp.argsort(flat_e, stable=True).astype(I32)
    inv = jnp.argsort(order).astype(I32)
    onehot = flat_e[:, None] == experts[None, :]
    counts = jnp.sum(onehot, axis=0, dtype=I32)
    padded = (counts + blk - 1) // blk * blk
    pends = jnp.cumsum(padded)
    offs = jnp.cumsum(counts) - counts
    shift = (pends - padded) - offs
    pos = inv + jnp.sum(jnp.where(onehot, shift[None, :], 0), axis=1, dtype=I32)
    P = N + N_EXPERTS * blk
    nb = P // blk
    blk_start = jnp.arange(nb, dtype=I32) * blk
    blk_e = jnp.minimum(jnp.sum(pends[None, :] <= blk_start[:, None], axis=1, dtype=I32), N_EXPERTS - 1)
    j = (blk_start - shift[blk_e])[:, None] + jnp.arange(blk, dtype=I32)[None, :]
    valid = j < (offs + counts)[blk_e][:, None]
    src = order[jnp.clip(j, 0, N - 1)]
    row_tok = jnp.where(valid, src % T, j % T).reshape(P)
    nused = (pends[-1:] // blk).astype(I32)
    return blk_e, nused, row_tok, pos.reshape(K, T)


def _projection_weights(w_in_l):
    sizes = (ATT_W, ATT_W, ATT_W, IDX_HEADS * IDX_DIM, IDX_DIM, IDX_HEADS, ATT_W, ATT_W, ATT_W)
    offs = np.concatenate([[0], np.cumsum(sizes)])
    qa, ka, va, qi, ki, wi, qb, kb, vb = (w_in_l[:, offs[n]:offs[n + 1]] for n in range(9))
    pad_wi = jnp.zeros((w_in_l.shape[0], LANES - IDX_HEADS), w_in_l.dtype)
    w_att = jnp.concatenate([qa, ka, qi, qb, kb, vb, ki, ki, wi, pad_wi], axis=1).astype(BF16)
    w_va_t = va.T.astype(BF16)
    w_gate = w_in_l[:, offs[9]:].astype(BF16)
    return w_att, w_va_t, w_gate


def kernel(x, w_in, w_branch_a, w_branch_b, w_out, rel_bias, ln1_g, ln1_b, w_router, b_router,
           w_gate_up, b_gate_up, w_down, b_down, ln2_g, ln2_b):
    B, S, D = x.shape
    depth = w_in.shape[0]
    alpha = (2 * depth) ** 0.25
    T = B * S
    h = x.reshape(T, D)
    for l in range(depth):
        w_att, w_va_t, w_gate = _projection_weights(w_in[l])
        proj = _matmul(h, w_att, BF16, min(T, 1024), PROJ_TN).reshape(B, S, ATT_COLS)
        vt = _matmul_t(h, w_va_t, DSA_BLK).reshape(B, S // DSA_BLK, ATT_W, DSA_BLK)
        ya = _dsa(proj, vt, rel_bias).reshape(T, ATT_W)
        yb = _stick_breaking(proj).reshape(T, ATT_W)
        h1, top_e, top_p = _merge(
            h, ya, yb, w_gate, w_branch_a[l].astype(BF16), w_branch_b[l].astype(BF16),
            w_out[l].astype(BF16), ln1_g[l].reshape(1, D), ln1_b[l].reshape(1, D),
            w_router[l].T, b_router[l].reshape(N_EXPERTS, 1), alpha)
        blk_e, nused, row_tok, pos = _route(top_e, MOE_BLK)
        xs = _sc_gather_rows(h1, row_tok)
        ys = _moe_ffn(xs, blk_e, nused, w_gate_up[l], b_gate_up[l], w_down[l], b_down[l])
        y4 = _sc_gather_rows(ys, pos.reshape(TOP_K * T)).reshape(TOP_K, T, D)
        h = _combine(h1, y4, top_p, ln2_g[l].reshape(1, D), ln2_b[l].reshape(1, D), alpha)
    return h.reshape(B, S, D)
```

```python
import functools
import math

import numpy as np
import jax
import jax.numpy as jnp
from jax import lax
from jax.experimental import pallas as pl
from jax.experimental.pallas import tpu as pltpu
from jax.experimental.pallas import tpu_sc as plsc

F32 = jnp.float32
BF16 = jnp.bfloat16
I32 = jnp.int32

A_HEADS = 8
HEAD_DIM = 64
ATT_W = A_HEADS * HEAD_DIM
IDX_HEADS = 8
IDX_DIM = 64
IDX_SCALE = (IDX_HEADS * IDX_DIM) ** -0.5
TOPK_MAX = 256
N_BUCKETS = 32
MAX_DISTANCE = 128
N_EXPERTS = 32
TOP_K = 4
SWIGLU_LIMIT = 7.0
SWIGLU_ALPHA = 1.702
LN_EPS = 1e-5
QK_SCALE = HEAD_DIM ** -0.5

LANES = 128
SUBLANES = 8
HALF = LANES // 2
N_PAIRS = A_HEADS // 2
VMEM_LIMIT = 56 * 1024 * 1024

DSA_BLK = 256
SB_T = 256
MERGE_TM = 512
MOE_BLK = 512
COMB_TM = 256
SC_GATHER_ROWS = 32
BISECT_CAP = 24
NEG = -1e30

COL_QA, COL_KA, COL_QI, COL_QB, COL_KB, COL_VB = (g * ATT_W for g in range(6))
COL_KK = 6 * ATT_W
COL_WI = COL_KK + LANES
ATT_COLS = COL_WI + LANES
PROJ_TN = ATT_COLS // 2

NT_DIMS = (((1,), (1,)), ((), ()))


def _params(sem, vmem=VMEM_LIMIT):
    return pltpu.CompilerParams(dimension_semantics=sem, vmem_limit_bytes=vmem)


def _mm_kernel(x_ref, w_ref, o_ref, xb_ref):
    @pl.when(pl.program_id(1) == 0)
    def _cast():
        xb_ref[...] = x_ref[...].astype(BF16)

    o_ref[...] = jnp.dot(xb_ref[...], w_ref[...], preferred_element_type=F32).astype(o_ref.dtype)


def _matmul(x, w, out_dtype, tm, tn):
    M, K = x.shape
    N = w.shape[1]
    return pl.pallas_call(
        _mm_kernel,
        grid=(M // tm, N // tn),
        in_specs=[pl.BlockSpec((tm, K), lambda i, j: (i, 0)),
                  pl.BlockSpec((K, tn), lambda i, j: (0, j))],
        out_specs=pl.BlockSpec((tm, tn), lambda i, j: (i, j)),
        out_shape=jax.ShapeDtypeStruct((M, N), out_dtype),
        scratch_shapes=[pltpu.VMEM((tm, K), BF16)],
        compiler_params=_params(("arbitrary", "arbitrary")),
    )(x, w)


def _mm_t_kernel(x_ref, w_ref, o_ref):
    o_ref[0] = lax.dot_general(w_ref[...], x_ref[...].astype(BF16), NT_DIMS,
                               preferred_element_type=F32).astype(o_ref.dtype)


def _matmul_t(x, w_t, tm):
    M, K = x.shape
    N = w_t.shape[0]
    return pl.pallas_call(
        _mm_t_kernel,
        grid=(M // tm,),
        in_specs=[pl.BlockSpec((tm, K), lambda i: (i, 0)),
                  pl.BlockSpec((N, K), lambda i: (0, 0))],
        out_specs=pl.BlockSpec((1, N, tm), lambda i: (i, 0, 0)),
        out_shape=jax.ShapeDtypeStruct((M // tm, N, tm), BF16),
        compiler_params=_params(("arbitrary",)),
    )(x, w_t)


def _t5_bucket_np(n):
    n = np.maximum(n, 0)
    max_exact = N_BUCKETS // 2
    nf = np.maximum(n, 1).astype(np.float32)
    large = max_exact + (np.log(nf / max_exact) / math.log(MAX_DISTANCE / max_exact)
                         * (N_BUCKETS - max_exact)).astype(np.int32)
    large = np.minimum(large, N_BUCKETS - 1)
    return np.where(n < max_exact, n, large).astype(np.int32)


def _dsa_n_off(blk):
    return 2 + -(-MAX_DISTANCE // blk)


def _dsa_bucket_tiles(blk):
    n_off = _dsa_n_off(blk)
    j = np.arange(blk)[None, :, None]
    i = np.arange(blk)[None, None, :]
    o = np.arange(n_off)[:, None, None]
    return _t5_bucket_np(i - j + blk * (n_off - 1 - o))


def _dsa_kernel(bucket_ref, relb_ref, q_ref, k_ref, vt_ref, qi_ref, kk_ref, wi_ref, o_ref,
                sc_ref, bias_ref, m_ref, l_ref, acc_ref, st_ref, mx_ref,
                *, blk, n_chunks, n_sel, n_off):
    b = pl.program_id(0)
    i = pl.program_id(1)
    q0 = i * blk
    nck = i + 1
    groups = blk // SUBLANES

    @pl.when(jnp.logical_and(b == 0, i == 0))
    def _build_bias():
        def head_body(h, _):
            for o in range(n_off):
                for rb in range(blk // LANES):
                    for cb in range(blk // LANES):
                        rs = slice(rb * LANES, (rb + 1) * LANES)
                        cs = slice(cb * LANES, (cb + 1) * LANES)
                        bk = bucket_ref[o, rs, cs]

                        def bucket_body(n, acc):
                            return jnp.where(bk == n, relb_ref[n, h], acc)

                        bias_ref[h, o, rs, cs] = lax.fori_loop(
                            0, N_BUCKETS, bucket_body, jnp.zeros((LANES, LANES), F32))
            return 0

        lax.fori_loop(0, A_HEADS, head_body, 0)

    lane = lax.broadcasted_iota(I32, (blk, LANES), 1)
    lo_half = lane < HALF
    krow = lax.broadcasted_iota(I32, (blk, blk), 0)
    qpos = q0 + lax.broadcasted_iota(I32, (1, blk), 1)

    def pair_split(ref, scale):
        out = []
        for p in range(N_PAIRS):
            v = ref[0, :, p * LANES:(p + 1) * LANES].astype(F32)
            if scale != 1.0:
                v = v * scale
            out.append(jnp.where(lo_half, v, 0.0).astype(BF16))
            out.append(jnp.where(lo_half, 0.0, v).astype(BF16))
        return out

    wi_t = wi_ref[0].astype(F32).T
    wrow = [wi_t[h:h + 1, :] * IDX_SCALE for h in range(IDX_HEADS)]
    qi_m = pair_split(qi_ref, 1.0)

    def score_chunk(c, _):
        c0 = pl.multiple_of(c * blk, blk)
        kk = kk_ref[0, pl.ds(c0, blk), :]
        acc = jnp.zeros((blk, blk), F32)
        for h in range(IDX_HEADS):
            s = lax.dot_general(kk, qi_m[h], NT_DIMS, preferred_element_type=F32)
            acc = acc + wrow[h] * jnp.maximum(s, 0.0)
        sc_ref[c] = jnp.where(c0 + krow <= qpos, acc, -jnp.inf)
        return 0

    lax.fori_loop(0, nck, score_chunk, 0)

    kt = jnp.minimum(qpos + 1, n_sel).astype(F32)

    def fold(fn, init):
        def body(c, acc):
            return fn(acc, sc_ref[c])
        return lax.fori_loop(0, nck, body, init)

    def part(x, op):
        return op(x.reshape(groups, SUBLANES, blk), axis=0)

    def fin(x, op):
        return op(x, axis=0, keepdims=True)

    zeros8 = jnp.zeros((SUBLANES, blk), F32)
    pinf8 = jnp.full((SUBLANES, blk), jnp.inf, F32)

    def count_ge(th):
        return fin(fold(lambda a, s: a + part(jnp.where(s >= th, 1.0, 0.0), jnp.sum), zeros8),
                   jnp.sum)

    mn, mx = fold(lambda a, s: (
        jnp.minimum(a[0], part(jnp.where(s == -jnp.inf, jnp.inf, s), jnp.min)),
        jnp.maximum(a[1], part(s, jnp.max))), (pinf8, -pinf8))
    rmin = fin(mn, jnp.min)
    rmax = fin(mx, jnp.max)

    def bis_cond(st):
        it, lo, hi, clo = st
        return jnp.logical_and(it < BISECT_CAP, jnp.max(jnp.abs(clo - kt)) > 0.0)

    def bis_body(st):
        it, lo, hi, clo = st
        mid = 0.5 * lo + 0.5 * hi
        c = count_ge(mid)
        active = clo != kt
        up = jnp.logical_and(active, c >= kt)
        dn = jnp.logical_and(active, c < kt)
        return (it + 1, jnp.where(up, mid, lo), jnp.where(dn, mid, hi), jnp.where(up, c, clo))

    _, lo, _, _ = lax.while_loop(
        bis_cond, bis_body, (jnp.int32(0), rmin, rmax + 1.0, (qpos + 1).astype(F32)))

    def stats(lo_):
        a_ = fin(fold(lambda a, s: jnp.minimum(a, part(jnp.where(s >= lo_, s, jnp.inf), jnp.min)),
                      pinf8), jnp.min)
        cg, ct, nx = fold(
            lambda a, s: (a[0] + part(jnp.where(s > a_, 1.0, 0.0), jnp.sum),
                               a[1] + part(jnp.where(s == a_, 1.0, 0.0), jnp.sum),
                               jnp.minimum(a[2], part(jnp.where(s > a_, s, jnp.inf), jnp.min))),
            (zeros8, zeros8, pinf8))
        return a_, fin(cg, jnp.sum), fin(ct, jnp.sum), fin(nx, jnp.min)

    def fin_cond(st):
        return st[0]

    def fin_body(st):
        _, lo_, _, _ = st
        a_, cgt_, nt_, nxt_ = stats(lo_)
        bad = cgt_ >= kt
        return (jnp.max(jnp.where(bad, 1.0, 0.0)) > 0.0, jnp.where(bad, nxt_, a_), cgt_, nt_)

    _, a, cgt, nties = lax.while_loop(fin_cond, fin_body, (jnp.bool_(True), lo, kt, kt))
    need = kt - cgt
    excess = jnp.max(jnp.where(nties > need, 1.0, 0.0)) > 0.0

    def mask_plain():
        def body(c, _):
            sc_ref[c] = jnp.where(sc_ref[c] >= a, 0.0, NEG)
            return 0
        lax.fori_loop(0, nck, body, 0)

    def mask_ties():
        upto = (krow >= lax.broadcasted_iota(I32, (blk, blk), 1)).astype(BF16)

        def body(c, seen):
            s = sc_ref[c]
            tie = s == a
            rank = jnp.dot(upto, jnp.where(tie, 1.0, 0.0).astype(BF16),
                           preferred_element_type=F32) + seen
            sel = jnp.logical_or(s > a, jnp.logical_and(tie, rank <= need))
            sc_ref[c] = jnp.where(sel, 0.0, NEG)
            return rank[blk - 1:blk, :]

        lax.fori_loop(0, nck, body, jnp.zeros((1, blk), F32))

    lax.cond(excess, mask_ties, mask_plain)

    odd = lax.rem(nck, 2)
    npair = (nck + odd) // 2

    @pl.when(odd == 1)
    def _mask_extra_chunk():
        sc_ref[nck] = jnp.full((blk, blk), NEG, F32)

    m_ref[...] = jnp.full(m_ref.shape, NEG, F32)
    l_ref[...] = jnp.zeros(l_ref.shape, F32)
    acc_ref[...] = jnp.zeros(acc_ref.shape, F32)
    q_m = pair_split(q_ref, QK_SCALE)

    def stage_logits(c, slot):
        c = jnp.minimum(c, n_chunks - 1)
        c0 = pl.multiple_of(c * blk, blk)
        madd = sc_ref[c]
        o_idx = jnp.clip(c - i + (n_off - 1), 0, n_off - 1)
        for p in range(N_PAIRS):
            k2 = k_ref[0, pl.ds(c0, blk), p * LANES:(p + 1) * LANES]
            for hh in range(2):
                h = 2 * p + hh
                s = lax.dot_general(k2, q_m[h], NT_DIMS, preferred_element_type=F32)
                s = s + bias_ref[h, o_idx] + madd
                st_ref[slot, h] = s
                mx_ref[slot, h] = jnp.max(s, axis=0, keepdims=True)

    def stage_values(c, slot):
        for p in range(N_PAIRS):
            vt2 = vt_ref[0, c, p * LANES:(p + 1) * LANES, :]
            for hh in range(2):
                h = 2 * p + hh
                m_old = m_ref[h]
                m_new = jnp.maximum(m_old, mx_ref[slot, h])
                alpha = jnp.exp(m_old - m_new)
                pexp = jnp.exp(st_ref[slot, h] - m_new)
                l_ref[h] = alpha * l_ref[h] + jnp.sum(pexp, axis=0, keepdims=True)
                acc_ref[h] = alpha * acc_ref[h] + jnp.dot(vt2, pexp.astype(BF16),
                                                          preferred_element_type=F32)
                m_ref[h] = m_new

    stage_logits(0, 0)
    stage_logits(1, 1)

    def att_pair(pp, _):
        c = 2 * pp
        stage_values(c, 0)
        stage_logits(c + 2, 0)
        stage_values(c + 1, 1)
        stage_logits(c + 3, 1)
        return 0

    lax.fori_loop(0, npair, att_pair, 0)

    lo_rows = lax.broadcasted_iota(I32, (LANES, blk), 0) < HALF
    for p in range(N_PAIRS):
        oa = acc_ref[2 * p] / l_ref[2 * p]
        ob = acc_ref[2 * p + 1] / l_ref[2 * p + 1]
        o_ref[0, :, p * LANES:(p + 1) * LANES] = jnp.where(lo_rows, oa, ob).T.astype(o_ref.dtype)


def _dsa(proj3, vt4, rel_bias):
    B, S, _ = proj3.shape
    blk = DSA_BLK
    n_off = _dsa_n_off(blk)
    n_sel = min(TOPK_MAX, S // 4)
    bucket = jnp.asarray(_dsa_bucket_tiles(blk))
    n_chunks = S // blk
    assert S % blk == 0 and n_chunks % 2 == 0, "DSA consumes key chunks in pairs"
    kern = functools.partial(_dsa_kernel, blk=blk, n_chunks=n_chunks, n_sel=n_sel, n_off=n_off)
    return pl.pallas_call(
        kern,
        grid=(B, S // blk),
        in_specs=[
            pl.BlockSpec((n_off, blk, blk), lambda b, i: (0, 0, 0)),
            pl.BlockSpec(memory_space=pltpu.SMEM),
            pl.BlockSpec((1, blk, ATT_W), lambda b, i: (b, i, COL_QA // ATT_W)),
            pl.BlockSpec((1, S, ATT_W), lambda b, i: (b, 0, COL_KA // ATT_W)),
            pl.BlockSpec((1, S // blk, ATT_W, blk), lambda b, i: (b, 0, 0, 0)),
            pl.BlockSpec((1, blk, ATT_W), lambda b, i: (b, i, COL_QI // ATT_W)),
            pl.BlockSpec((1, S, LANES), lambda b, i: (b, 0, COL_KK // LANES)),
            pl.BlockSpec((1, blk, LANES), lambda b, i: (b, i, COL_WI // LANES)),
        ],
        out_specs=pl.BlockSpec((1, blk, ATT_W), lambda b, i: (b, i, 0)),
        out_shape=jax.ShapeDtypeStruct((B, S, ATT_W), BF16),
        scratch_shapes=[
            pltpu.VMEM((S // blk, blk, blk), F32),
            pltpu.VMEM((A_HEADS, n_off, blk, blk), F32),
            pltpu.VMEM((A_HEADS, 1, blk), F32),
            pltpu.VMEM((A_HEADS, 1, blk), F32),
            pltpu.VMEM((A_HEADS, LANES, blk), F32),
            pltpu.VMEM((2, A_HEADS, blk, blk), F32),
            pltpu.VMEM((2, A_HEADS, 1, blk), F32),
        ],
        compiler_params=_params(("arbitrary", "arbitrary")),
    )(bucket, rel_bias, proj3, proj3, vt4, proj3, proj3, proj3)


def _sb_kernel(q_ref, k_ref, v_ref, o_ref, hl_ref, z_ref, *, t, n_chunks):
    i = pl.program_id(2)
    n = i + 1
    odd = lax.rem(n, 2)
    top = i + odd
    npair = (n + odd) // 2
    lane = lax.broadcasted_iota(I32, (t, LANES), 1)
    lo_half = lane < HALF
    q2 = q_ref[0].astype(F32) * QK_SCALE
    q_m = (jnp.where(lo_half, q2, 0.0).astype(BF16), jnp.where(lo_half, 0.0, q2).astype(BF16))
    r = lax.broadcasted_iota(I32, (t, t), 0)
    cidx = lax.broadcasted_iota(I32, (t, t), 1)
    neg_from = jnp.where(r >= cidx, -1.0, 0.0).astype(BF16)
    diff = cidx - r

    def stage_terms(step, slot, masked):
        chunk = top - step
        c0 = pl.multiple_of(jnp.clip(chunk, 0, n_chunks - 1) * t, t)
        k2 = k_ref[0, pl.ds(c0, t), :]
        if masked:
            keep = diff < (i - chunk) * t
        for hh in range(2):
            z = lax.dot_general(q_m[hh], k2, NT_DIMS, preferred_element_type=F32)
            sp = jnp.maximum(z, 0.0) + jnp.log(1.0 + jnp.exp(-jnp.abs(z)))
            if masked:
                sp = jnp.where(keep, sp, 0.0)
                z = jnp.where(keep, z, NEG)
            hi = sp.astype(BF16)
            hl_ref[slot, 2 * hh] = hi
            hl_ref[slot, 2 * hh + 1] = (sp - hi.astype(F32)).astype(BF16)
            z_ref[slot, hh] = z

    def stage_apply(step, slot, carry):
        c0 = pl.multiple_of(jnp.minimum(top - step, n_chunks - 1) * t, t)
        v2 = v_ref[0, pl.ds(c0, t), :]
        cum4 = jnp.dot(hl_ref[slot].reshape(4 * t, t), neg_from, preferred_element_type=F32)
        out = []
        for hh in range(2):
            car, acc = carry[hh]
            cum = cum4[(2 * hh) * t:(2 * hh + 1) * t] + cum4[(2 * hh + 1) * t:(2 * hh + 2) * t]
            w = jnp.exp(z_ref[slot, hh] + cum + car)
            acc = acc + jnp.dot(w.astype(BF16), v2, preferred_element_type=F32)
            out.append((car + cum[:, 0:1], acc))
        return tuple(out)

    z1 = jnp.zeros((t, 1), F32)
    za = jnp.zeros((t, LANES), F32)
    stage_terms(0, 0, True)
    stage_terms(1, 1, True)

    def pair_body(pp, carry):
        step = 2 * pp
        carry = stage_apply(step, 0, carry)
        stage_terms(step + 2, 0, False)
        carry = stage_apply(step + 1, 1, carry)
        stage_terms(step + 3, 1, False)
        return carry

    (_, acc_a), (_, acc_b) = lax.fori_loop(0, npair, pair_body, ((z1, za), (z1, za)))
    o_ref[0] = jnp.where(lo_half, acc_a, acc_b).astype(o_ref.dtype)


def _stick_breaking(proj3):
    B, S, _ = proj3.shape
    t = SB_T
    qb, kb, vb = COL_QB // LANES, COL_KB // LANES, COL_VB // LANES
    return pl.pallas_call(
        functools.partial(_sb_kernel, t=t, n_chunks=S // t),
        grid=(B, N_PAIRS, S // t),
        in_specs=[
            pl.BlockSpec((1, t, LANES), lambda b, p, i: (b, i, qb + p)),
            pl.BlockSpec((1, S, LANES), lambda b, p, i: (b, 0, kb + p)),
            pl.BlockSpec((1, S, LANES), lambda b, p, i: (b, 0, vb + p)),
        ],
        out_specs=pl.BlockSpec((1, t, LANES), lambda b, p, i: (b, i, p)),
        out_shape=jax.ShapeDtypeStruct((B, S, ATT_W), BF16),
        scratch_shapes=[
            pltpu.VMEM((2, 4, t, t), BF16),
            pltpu.VMEM((2, 2, t, t), F32),
        ],
        compiler_params=_params(("arbitrary", "arbitrary", "arbitrary")),
    )(proj3, proj3, proj3)


def _layer_norm(r, g, b):
    mu = jnp.mean(r, axis=-1, keepdims=True)
    d = r - mu
    var = jnp.mean(d * d, axis=-1, keepdims=True)
    return d * lax.rsqrt(var + LN_EPS) * g + b


def _split_bf16(v):
    hi = v.astype(BF16)
    return hi, (v - hi.astype(F32)).astype(BF16)


def _merge_kernel(x_ref, ya_ref, yb_ref, wg_ref, wa_ref, wb_ref, wo_ref, g_ref, b_ref,
                  wr_ref, br_ref, h_ref, e_ref, p_ref, *, alpha, d):
    pa = jnp.dot(ya_ref[...], wa_ref[...], preferred_element_type=F32)
    pb = jnp.dot(yb_ref[...], wb_ref[...], preferred_element_type=F32)
    gates = jnp.dot(x_ref[...].astype(BF16), wg_ref[...], preferred_element_type=F32)
    merged = jax.nn.sigmoid(gates[:, :d]) * pa + jax.nn.sigmoid(gates[:, d:]) * pb
    m = jnp.dot(merged.astype(BF16), wo_ref[...], preferred_element_type=F32)
    h = _layer_norm(alpha * x_ref[...] + m, g_ref[...], b_ref[...])
    h_ref[...] = h

    h_hi, h_lo = _split_bf16(h)
    w_hi, w_lo = _split_bf16(wr_ref[...])
    logit = (lax.dot_general(w_hi, h_hi, NT_DIMS, preferred_element_type=F32)
             + lax.dot_general(w_hi, h_lo, NT_DIMS, preferred_element_type=F32)
             + lax.dot_general(w_lo, h_hi, NT_DIMS, preferred_element_type=F32)) + br_ref[...]
    eid = lax.broadcasted_iota(I32, logit.shape, 0)
    vals, ids = [], []
    for _ in range(TOP_K):
        mx = jnp.max(logit, axis=0, keepdims=True)
        am = jnp.min(jnp.where(logit == mx, eid, N_EXPERTS), axis=0, keepdims=True)
        vals.append(mx)
        ids.append(am)
        logit = jnp.where(eid == am, -jnp.inf, logit)
    ex = [jnp.exp(v - vals[0]) for v in vals]
    den = ex[0] + ex[1] + ex[2] + ex[3]
    for k in range(TOP_K):
        e_ref[k:k + 1, :] = ids[k]
        p_ref[k:k + 1, :] = ex[k] / den


def _merge(x2, ya, yb, wg, wa, wb, wo, g, b, wr_t, br, alpha):
    T, D = x2.shape
    tm = MERGE_TM
    row = lambda i: (i, 0)
    fixed = lambda i: (0, 0)
    return pl.pallas_call(
        functools.partial(_merge_kernel, alpha=alpha, d=D),
        grid=(T // tm,),
        in_specs=[
            pl.BlockSpec((tm, D), row),
            pl.BlockSpec((tm, ATT_W), row),
            pl.BlockSpec((tm, ATT_W), row),
            pl.BlockSpec((D, 2 * D), fixed),
            pl.BlockSpec((ATT_W, D), fixed),
            pl.BlockSpec((ATT_W, D), fixed),
            pl.BlockSpec((D, D), fixed),
            pl.BlockSpec((1, D), fixed),
            pl.BlockSpec((1, D), fixed),
            pl.BlockSpec((N_EXPERTS, D), fixed),
            pl.BlockSpec((N_EXPERTS, 1), fixed),
        ],
        out_specs=[
            pl.BlockSpec((tm, D), row),
            pl.BlockSpec((TOP_K, tm), lambda i: (0, i)),
            pl.BlockSpec((TOP_K, tm), lambda i: (0, i)),
        ],
        out_shape=[
            jax.ShapeDtypeStruct((T, D), F32),
            jax.ShapeDtypeStruct((TOP_K, T), I32),
            jax.ShapeDtypeStruct((TOP_K, T), F32),
        ],
        compiler_params=_params(("arbitrary",)),
    )(x2, ya, yb, wg, wa, wb, wo, g, b, wr_t, br)


def _sc_gather_rows(table, idx):
    n = idx.shape[0]
    d = table.shape[1]
    info = plsc.get_sparse_core_info()
    n_cores, n_sub = info.num_cores, info.num_subcores
    per_w = n // (n_cores * n_sub)
    c = SC_GATHER_ROWS
    n_g = per_w // c
    assert n == per_w * n_cores * n_sub and per_w == n_g * c and n_g % 2 == 0 and n_g >= 2
    mesh = plsc.VectorSubcoreMesh(core_axis_name="c", subcore_axis_name="s")

    @functools.partial(
        pl.kernel, mesh=mesh, out_type=jax.ShapeDtypeStruct((n, d), table.dtype),
        scratch_types=[pltpu.VMEM((per_w,), I32), pltpu.VMEM((2, c, d), table.dtype),
                       pltpu.SemaphoreType.DMA((2,)), pltpu.SemaphoreType.DMA((2,))])
    def gather_kernel(table_hbm, idx_hbm, out_hbm, idx_v, rows_v, gsem, wsem):
        base = (lax.axis_index("s") * n_cores + lax.axis_index("c")) * per_w
        pltpu.sync_copy(idx_hbm.at[pl.ds(base, per_w)], idx_v)

        def gather(g, b):
            return pltpu.make_async_copy(table_hbm.at[idx_v.at[pl.ds(g * c, c)]], rows_v.at[b],
                                         gsem.at[b])

        def write(g, b):
            return pltpu.make_async_copy(rows_v.at[b], out_hbm.at[pl.ds(base + g * c, c)], wsem.at[b])

        gather(0, 0).start()

        @pl.loop(0, n_g, step=2)
        def _ring(g0):
            for b in range(2):
                g = g0 + b

                @pl.when(g + 1 < n_g)
                def _next():
                    @pl.when(g >= 1)
                    def _buffer_free():
                        write(g - 1, 1 - b).wait()
                    gather(g + 1, 1 - b).start()

                gather(g, b).wait()
                write(g, b).start()

        write(n_g - 2, 0).wait()
        write(n_g - 1, 1).wait()

    return gather_kernel(table, idx)


def _moe_kernel(blk_e_ref, nused_ref, x_ref, wgu_ref, bgu_ref, wdn_ref, bdn_ref, o_ref,
                wgu_s, wdn_s, *, f):
    i = pl.program_id(0)
    nused = nused_ref[0]

    @pl.when(i < nused)
    def _compute():
        changed = jnp.logical_or(i == 0, blk_e_ref[i] != blk_e_ref[jnp.maximum(i - 1, 0)])

        @pl.when(changed)
        def _cast_weights():
            wgu_s[...] = wgu_ref[0].astype(BF16)
            wdn_s[...] = wdn_ref[0].astype(BF16)

        x = x_ref[...].astype(BF16)
        hgu = jnp.dot(x, wgu_s[...], preferred_element_type=F32) + bgu_ref[0]
        a = jnp.minimum(hgu[:, :f], SWIGLU_LIMIT)
        u = jnp.clip(hgu[:, f:], -SWIGLU_LIMIT, SWIGLU_LIMIT)
        glu = a * jax.nn.sigmoid(a * SWIGLU_ALPHA)
        o_ref[...] = jnp.dot(((u + 1.0) * glu).astype(BF16), wdn_s[...],
                             preferred_element_type=F32) + bdn_ref[0]

    @pl.when(i >= nused)
    def _unused_block():
        o_ref[...] = jnp.zeros(o_ref.shape, o_ref.dtype)


def _moe_ffn(xs, blk_e, nused, w_gu, b_gu, w_dn, b_dn):
    P, D = xs.shape
    E, _, F2 = w_gu.shape
    f = F2 // 2
    blk = MOE_BLK
    nb = P // blk
    grid_spec = pltpu.PrefetchScalarGridSpec(
        num_scalar_prefetch=2,
        grid=(nb,),
        in_specs=[
            pl.BlockSpec((blk, D), lambda i, be, nu: (i, 0)),
            pl.BlockSpec((1, D, F2), lambda i, be, nu: (be[i], 0, 0)),
            pl.BlockSpec((1, 1, F2), lambda i, be, nu: (be[i], 0, 0)),
            pl.BlockSpec((1, f, D), lambda i, be, nu: (be[i], 0, 0)),
            pl.BlockSpec((1, 1, D), lambda i, be, nu: (be[i], 0, 0)),
        ],
        out_specs=pl.BlockSpec((blk, D), lambda i, be, nu: (i, 0)),
        scratch_shapes=[
            pltpu.VMEM((D, F2), BF16),
            pltpu.VMEM((f, D), BF16),
        ],
    )
    return pl.pallas_call(
        functools.partial(_moe_kernel, f=f),
        grid_spec=grid_spec,
        out_shape=jax.ShapeDtypeStruct((P, D), F32),
        compiler_params=_params(("arbitrary",)),
    )(blk_e, nused, xs, w_gu, b_gu.reshape(E, 1, F2), w_dn, b_dn.reshape(E, 1, D))


def _comb_kernel(h_ref, p_ref, y_ref, g_ref, b_ref, o_ref, *, alpha):
    gate = p_ref[...]
    fsum = ((y_ref[0] * gate[:, 0:1] + y_ref[1] * gate[:, 1:2])
            + (y_ref[2] * gate[:, 2:3] + y_ref[3] * gate[:, 3:4]))
    o_ref[...] = _layer_norm(alpha * h_ref[...] + fsum, g_ref[...], b_ref[...])


def _combine(h, y4, top_p, g, b, alpha):
    T, D = h.shape
    tm = COMB_TM
    return pl.pallas_call(
        functools.partial(_comb_kernel, alpha=alpha),
        grid=(T // tm,),
        in_specs=[
            pl.BlockSpec((tm, D), lambda i: (i, 0)),
            pl.BlockSpec((tm, TOP_K), lambda i: (i, 0)),
            pl.BlockSpec((TOP_K, tm, D), lambda i: (0, i, 0)),
            pl.BlockSpec((1, D), lambda i: (0, 0)),
            pl.BlockSpec((1, D), lambda i: (0, 0)),
        ],
        out_specs=pl.BlockSpec((tm, D), lambda i: (i, 0)),
        out_shape=jax.ShapeDtypeStruct((T, D), F32),
        compiler_params=_params(("arbitrary",)),
    )(h, top_p.T, y4, g, b)


def _route(top_e, blk):
    K, T = top_e.shape
    N = K * T
    flat_e = top_e.reshape(N)
    experts = jnp.arange(N_EXPERTS, dtype=I32)
    order = jnp.argsort(flat_e, stable=True).astype(I32)
    inv = jnp.argsort(order).astype(I32)
    onehot = flat_e[:, None] == experts[None, :]
    counts = jnp.sum(onehot, axis=0, dtype=I32)
    padded = (counts + blk - 1) // blk * blk
    pends = jnp.cumsum(padded)
    offs = jnp.cumsum(counts) - counts
    shift = (pends - padded) - offs
    pos = inv + jnp.sum(jnp.where(onehot, shift[None, :], 0), axis=1, dtype=I32)
    P = N + N_EXPERTS * blk
    nb = P // blk
    blk_start = jnp.arange(nb, dtype=I32) * blk
    blk_e = jnp.minimum(jnp.sum(pends[None, :] <= blk_start[:, None], axis=1, dtype=I32), N_EXPERTS - 1)
    j = (blk_start - shift[blk_e])[:, None] + jnp.arange(blk, dtype=I32)[None, :]
    valid = j < (offs + counts)[blk_e][:, None]
    src = order[jnp.clip(j, 0, N - 1)]
    row_tok = jnp.where(valid, src % T, j % T).reshape(P)
    nused = (pends[-1:] // blk).astype(I32)
    return blk_e, nused, row_tok, pos.reshape(K, T)


def _projection_weights(w_in_l):
    sizes = (ATT_W, ATT_W, ATT_W, IDX_HEADS * IDX_DIM, IDX_DIM, IDX_HEADS, ATT_W, ATT_W, ATT_W)
    offs = np.concatenate([[0], np.cumsum(sizes)])
    qa, ka, va, qi, ki, wi, qb, kb, vb = (w_in_l[:, offs[n]:offs[n + 1]] for n in range(9))
    pad_wi = jnp.zeros((w_in_l.shape[0], LANES - IDX_HEADS), w_in_l.dtype)
    w_att = jnp.concatenate([qa, ka, qi, qb, kb, vb, ki, ki, wi, pad_wi], axis=1).astype(BF16)
    w_va_t = va.T.astype(BF16)
    w_gate = w_in_l[:, offs[9]:].astype(BF16)
    return w_att, w_va_t, w_gate


def kernel(x, w_in, w_branch_a, w_branch_b, w_out, rel_bias, ln1_g, ln1_b, w_router, b_router,
           w_gate_up, b_gate_up, w_down, b_down, ln2_g, ln2_b):
    B, S, D = x.shape
    depth = w_in.shape[0]
    alpha = (2 * depth) ** 0.25
    T = B * S
    h = x.reshape(T, D)
    for l in range(depth):
        w_att, w_va_t, w_gate = _projection_weights(w_in[l])
        proj = _matmul(h, w_att, BF16, min(T, 1024), PROJ_TN).reshape(B, S, ATT_COLS)
        vt = _matmul_t(h, w_va_t, DSA_BLK).reshape(B, S // DSA_BLK, ATT_W, DSA_BLK)
        ya = _dsa(proj, vt, rel_bias).reshape(T, ATT_W)
        yb = _stick_breaking(proj).reshape(T, ATT_W)
        h1, top_e, top_p = _merge(
            h, ya, yb, w_gate, w_branch_a[l].astype(BF16), w_branch_b[l].astype(BF16),
            w_out[l].astype(BF16), ln1_g[l].reshape(1, D), ln1_b[l].reshape(1, D),
            w_router[l].T, b_router[l].reshape(N_EXPERTS, 1), alpha)
        blk_e, nused, row_tok, pos = _route(top_e, MOE_BLK)
        xs = _sc_gather_rows(h1, row_tok)
        ys = _moe_ffn(xs, blk_e, nused, w_gate_up[l], b_gate_up[l], w_down[l], b_down[l])
        y4 = _sc_gather_rows(ys, pos.reshape(TOP_K * T)).reshape(TOP_K, T, D)
        h = _combine(h1, y4, top_p, ln2_g[l].reshape(1, D), ln2_b[l].reshape(1, D), alpha)
    return h.reshape(B, S, D)
```

```python
import functools
import math

import numpy as np
import jax
import jax.numpy as jnp
from jax import lax
from jax.experimental import pallas as pl
from jax.experimental.pallas import tpu as pltpu
from jax.experimental.pallas import tpu_sc as plsc

F32 = jnp.float32
BF16 = jnp.bfloat16
I32 = jnp.int32

A_HEADS = 8
HEAD_DIM = 64
ATT_W = A_HEADS * HEAD_DIM
IDX_HEADS = 8
IDX_DIM = 64
IDX_SCALE = (IDX_HEADS * IDX_DIM) ** -0.5
TOPK_MAX = 256
N_BUCKETS = 32
MAX_DISTANCE = 128
N_EXPERTS = 32
TOP_K = 4
SWIGLU_LIMIT = 7.0
SWIGLU_ALPHA = 1.702
LN_EPS = 1e-5
QK_SCALE = HEAD_DIM ** -0.5

LANES = 128
SUBLANES = 8
HALF = LANES // 2
N_PAIRS = A_HEADS // 2
VMEM_LIMIT = 56 * 1024 * 1024

DSA_BLK = 256
SB_T = 256
MERGE_TM = 512
MOE_BLK = 512
COMB_TM = 256
MOE_GROUPS = 2
SC_GATHER_ROWS = 32
BISECT_CAP = 24
NEG = -1e30

COL_QA, COL_KA, COL_QI, COL_QB, COL_KB, COL_VB = (g * ATT_W for g in range(6))
COL_KK = 6 * ATT_W
COL_WI = COL_KK + LANES
ATT_COLS = COL_WI + LANES
PROJ_TN = ATT_COLS // 2

NT_DIMS = (((1,), (1,)), ((), ()))


def _params(sem, vmem=VMEM_LIMIT):
    return pltpu.CompilerParams(dimension_semantics=sem, vmem_limit_bytes=vmem)


def _mm_kernel(x_ref, w_ref, o_ref, xb_ref):
    @pl.when(pl.program_id(1) == 0)
    def _cast():
        xb_ref[...] = x_ref[...].astype(BF16)

    o_ref[...] = jnp.dot(xb_ref[...], w_ref[...], preferred_element_type=F32).astype(o_ref.dtype)


def _matmul(x, w, out_dtype, tm, tn):
    M, K = x.shape
    N = w.shape[1]
    return pl.pallas_call(
        _mm_kernel,
        grid=(M // tm, N // tn),
        in_specs=[pl.BlockSpec((tm, K), lambda i, j: (i, 0)),
                  pl.BlockSpec((K, tn), lambda i, j: (0, j))],
        out_specs=pl.BlockSpec((tm, tn), lambda i, j: (i, j)),
        out_shape=jax.ShapeDtypeStruct((M, N), out_dtype),
        scratch_shapes=[pltpu.VMEM((tm, K), BF16)],
        compiler_params=_params(("arbitrary", "arbitrary")),
    )(x, w)


def _mm_t_kernel(x_ref, w_ref, o_ref):
    o_ref[0] = lax.dot_general(w_ref[...], x_ref[...].astype(BF16), NT_DIMS,
                               preferred_element_type=F32).astype(o_ref.dtype)


def _matmul_t(x, w_t, tm):
    M, K = x.shape
    N = w_t.shape[0]
    return pl.pallas_call(
        _mm_t_kernel,
        grid=(M // tm,),
        in_specs=[pl.BlockSpec((tm, K), lambda i: (i, 0)),
                  pl.BlockSpec((N, K), lambda i: (0, 0))],
        out_specs=pl.BlockSpec((1, N, tm), lambda i: (i, 0, 0)),
        out_shape=jax.ShapeDtypeStruct((M // tm, N, tm), BF16),
        compiler_params=_params(("arbitrary",)),
    )(x, w_t)


def _t5_bucket_np(n):
    n = np.maximum(n, 0)
    max_exact = N_BUCKETS // 2
    nf = np.maximum(n, 1).astype(np.float32)
    large = max_exact + (np.log(nf / max_exact) / math.log(MAX_DISTANCE / max_exact)
                         * (N_BUCKETS - max_exact)).astype(np.int32)
    large = np.minimum(large, N_BUCKETS - 1)
    return np.where(n < max_exact, n, large).astype(np.int32)


def _dsa_n_off(blk):
    return 2 + -(-MAX_DISTANCE // blk)


def _dsa_bucket_tiles(blk):
    n_off = _dsa_n_off(blk)
    j = np.arange(blk)[None, :, None]
    i = np.arange(blk)[None, None, :]
    o = np.arange(n_off)[:, None, None]
    return _t5_bucket_np(i - j + blk * (n_off - 1 - o))


def _dsa_kernel(bucket_ref, relb_ref, q_ref, k_ref, vt_ref, qi_ref, kk_ref, wi_ref, o_ref,
                sc_ref, bias_ref, m_ref, l_ref, acc_ref, st_ref, mx_ref,
                *, blk, n_chunks, n_sel, n_off):
    b = pl.program_id(0)
    i = pl.program_id(1)
    q0 = i * blk
    nck = i + 1
    groups = blk // SUBLANES

    @pl.when(jnp.logical_and(b == 0, i == 0))
    def _build_bias():
        def head_body(h, _):
            for o in range(n_off):
                for rb in range(blk // LANES):
                    for cb in range(blk // LANES):
                        rs = slice(rb * LANES, (rb + 1) * LANES)
                        cs = slice(cb * LANES, (cb + 1) * LANES)
                        bk = bucket_ref[o, rs, cs]

                        def bucket_body(n, acc):
                            return jnp.where(bk == n, relb_ref[n, h], acc)

                        bias_ref[h, o, rs, cs] = lax.fori_loop(
                            0, N_BUCKETS, bucket_body, jnp.zeros((LANES, LANES), F32))
            return 0

        lax.fori_loop(0, A_HEADS, head_body, 0)

    lane = lax.broadcasted_iota(I32, (blk, LANES), 1)
    lo_half = lane < HALF
    krow = lax.broadcasted_iota(I32, (blk, blk), 0)
    qpos = q0 + lax.broadcasted_iota(I32, (1, blk), 1)

    def pair_split(ref, scale):
        out = []
        for p in range(N_PAIRS):
            v = ref[0, :, p * LANES:(p + 1) * LANES].astype(F32)
            if scale != 1.0:
                v = v * scale
            out.append(jnp.where(lo_half, v, 0.0).astype(BF16))
            out.append(jnp.where(lo_half, 0.0, v).astype(BF16))
        return out

    wi_t = wi_ref[0].astype(F32).T
    wrow = [wi_t[h:h + 1, :] * IDX_SCALE for h in range(IDX_HEADS)]
    qi_m = pair_split(qi_ref, 1.0)

    def score_chunk(c, _):
        c0 = pl.multiple_of(c * blk, blk)
        kk = kk_ref[0, pl.ds(c0, blk), :]
        acc = jnp.zeros((blk, blk), F32)
        for h in range(IDX_HEADS):
            s = lax.dot_general(kk, qi_m[h], NT_DIMS, preferred_element_type=F32)
            acc = acc + wrow[h] * jnp.maximum(s, 0.0)
        sc_ref[c] = jnp.where(c0 + krow <= qpos, acc, -jnp.inf)
        return 0

    lax.fori_loop(0, nck, score_chunk, 0)

    kt = jnp.minimum(qpos + 1, n_sel).astype(F32)

    def fold(fn, init):
        def body(c, acc):
            return fn(acc, sc_ref[c])
        return lax.fori_loop(0, nck, body, init)

    def part(x, op):
        return op(x.reshape(groups, SUBLANES, blk), axis=0)

    def fin(x, op):
        return op(x, axis=0, keepdims=True)

    zeros8 = jnp.zeros((SUBLANES, blk), F32)
    pinf8 = jnp.full((SUBLANES, blk), jnp.inf, F32)

    def count_ge(th):
        return fin(fold(lambda a, s: a + part(jnp.where(s >= th, 1.0, 0.0), jnp.sum), zeros8),
                   jnp.sum)

    mn, mx = fold(lambda a, s: (
        jnp.minimum(a[0], part(jnp.where(s == -jnp.inf, jnp.inf, s), jnp.min)),
        jnp.maximum(a[1], part(s, jnp.max))), (pinf8, -pinf8))
    rmin = fin(mn, jnp.min)
    rmax = fin(mx, jnp.max)

    def bis_cond(st):
        it, lo, hi, clo = st
        return jnp.logical_and(it < BISECT_CAP, jnp.max(jnp.abs(clo - kt)) > 0.0)

    def halve(lo, hi, clo):
        mid = 0.5 * lo + 0.5 * hi
        c = count_ge(mid)
        active = clo != kt
        up = jnp.logical_and(active, c >= kt)
        dn = jnp.logical_and(active, c < kt)
        return jnp.where(up, mid, lo), jnp.where(dn, mid, hi), jnp.where(up, c, clo)

    def bis_body(st):
        it, lo, hi, clo = st
        return (it + 2,) + halve(*halve(lo, hi, clo))

    _, lo, _, _ = lax.while_loop(
        bis_cond, bis_body, (jnp.int32(0), rmin, rmax + 1.0, (qpos + 1).astype(F32)))

    def stats(lo_):
        a_ = fin(fold(lambda a, s: jnp.minimum(a, part(jnp.where(s >= lo_, s, jnp.inf), jnp.min)),
                      pinf8), jnp.min)
        cg, ct, nx = fold(
            lambda a, s: (a[0] + part(jnp.where(s > a_, 1.0, 0.0), jnp.sum),
                               a[1] + part(jnp.where(s == a_, 1.0, 0.0), jnp.sum),
                               jnp.minimum(a[2], part(jnp.where(s > a_, s, jnp.inf), jnp.min))),
            (zeros8, zeros8, pinf8))
        return a_, fin(cg, jnp.sum), fin(ct, jnp.sum), fin(nx, jnp.min)

    def fin_cond(st):
        return st[0]

    def fin_body(st):
        _, lo_, _, _ = st
        a_, cgt_, nt_, nxt_ = stats(lo_)
        bad = cgt_ >= kt
        return (jnp.max(jnp.where(bad, 1.0, 0.0)) > 0.0, jnp.where(bad, nxt_, a_), cgt_, nt_)

    _, a, cgt, nties = lax.while_loop(fin_cond, fin_body, (jnp.bool_(True), lo, kt, kt))
    need = kt - cgt
    excess = jnp.max(jnp.where(nties > need, 1.0, 0.0)) > 0.0

    def mask_plain():
        def body(c, _):
            sc_ref[c] = jnp.where(sc_ref[c] >= a, 0.0, NEG)
            return 0
        lax.fori_loop(0, nck, body, 0)

    def mask_ties():
        upto = (krow >= lax.broadcasted_iota(I32, (blk, blk), 1)).astype(BF16)

        def body(c, seen):
            s = sc_ref[c]
            tie = s == a
            rank = jnp.dot(upto, jnp.where(tie, 1.0, 0.0).astype(BF16),
                           preferred_element_type=F32) + seen
            sel = jnp.logical_or(s > a, jnp.logical_and(tie, rank <= need))
            sc_ref[c] = jnp.where(sel, 0.0, NEG)
            return rank[blk - 1:blk, :]

        lax.fori_loop(0, nck, body, jnp.zeros((1, blk), F32))

    lax.cond(excess, mask_ties, mask_plain)

    odd = lax.rem(nck, 2)
    npair = (nck + odd) // 2

    @pl.when(odd == 1)
    def _mask_extra_chunk():
        sc_ref[nck] = jnp.full((blk, blk), NEG, F32)

    m_ref[...] = jnp.full(m_ref.shape, NEG, F32)
    l_ref[...] = jnp.zeros(l_ref.shape, F32)
    acc_ref[...] = jnp.zeros(acc_ref.shape, F32)
    q_m = pair_split(q_ref, QK_SCALE)

    def stage_logits(c, slot):
        c = jnp.minimum(c, n_chunks - 1)
        c0 = pl.multiple_of(c * blk, blk)
        madd = sc_ref[c]
        o_idx = jnp.clip(c - i + (n_off - 1), 0, n_off - 1)
        for p in range(N_PAIRS):
            k2 = k_ref[0, pl.ds(c0, blk), p * LANES:(p + 1) * LANES]
            for hh in range(2):
                h = 2 * p + hh
                s = lax.dot_general(k2, q_m[h], NT_DIMS, preferred_element_type=F32)
                s = s + bias_ref[h, o_idx] + madd
                st_ref[slot, h] = s
                mx_ref[slot, h] = jnp.max(s, axis=0, keepdims=True)

    def stage_values(c, slot):
        for p in range(N_PAIRS):
            vt2 = vt_ref[0, c, p * LANES:(p + 1) * LANES, :]
            for hh in range(2):
                h = 2 * p + hh
                m_old = m_ref[h]
                m_new = jnp.maximum(m_old, mx_ref[slot, h])
                alpha = jnp.exp(m_old - m_new)
                pexp = jnp.exp(st_ref[slot, h] - m_new)
                l_ref[h] = alpha * l_ref[h] + jnp.sum(pexp, axis=0, keepdims=True)
                acc_ref[h] = alpha * acc_ref[h] + jnp.dot(vt2, pexp.astype(BF16),
                                                          preferred_element_type=F32)
                m_ref[h] = m_new

    stage_logits(0, 0)

    def att_pair(pp, _):
        c = 2 * pp
        stage_logits(c + 1, 1)
        stage_values(c, 0)
        stage_logits(c + 2, 0)
        stage_values(c + 1, 1)
        return 0

    lax.fori_loop(0, npair, att_pair, 0)

    lo_rows = lax.broadcasted_iota(I32, (LANES, blk), 0) < HALF
    for p in range(N_PAIRS):
        oa = acc_ref[2 * p] / l_ref[2 * p]
        ob = acc_ref[2 * p + 1] / l_ref[2 * p + 1]
        o_ref[0, :, p * LANES:(p + 1) * LANES] = jnp.where(lo_rows, oa, ob).T.astype(o_ref.dtype)


def _dsa(proj3, vt4, rel_bias):
    B, S, _ = proj3.shape
    blk = DSA_BLK
    n_off = _dsa_n_off(blk)
    n_sel = min(TOPK_MAX, S // 4)
    bucket = jnp.asarray(_dsa_bucket_tiles(blk))
    n_chunks = S // blk
    assert S % blk == 0 and n_chunks % 2 == 0, "DSA consumes key chunks in pairs"
    kern = functools.partial(_dsa_kernel, blk=blk, n_chunks=n_chunks, n_sel=n_sel, n_off=n_off)
    return pl.pallas_call(
        kern,
        grid=(B, S // blk),
        in_specs=[
            pl.BlockSpec((n_off, blk, blk), lambda b, i: (0, 0, 0)),
            pl.BlockSpec(memory_space=pltpu.SMEM),
            pl.BlockSpec((1, blk, ATT_W), lambda b, i: (b, i, COL_QA // ATT_W)),
            pl.BlockSpec((1, S, ATT_W), lambda b, i: (b, 0, COL_KA // ATT_W)),
            pl.BlockSpec((1, S // blk, ATT_W, blk), lambda b, i: (b, 0, 0, 0)),
            pl.BlockSpec((1, blk, ATT_W), lambda b, i: (b, i, COL_QI // ATT_W)),
            pl.BlockSpec((1, S, LANES), lambda b, i: (b, 0, COL_KK // LANES)),
            pl.BlockSpec((1, blk, LANES), lambda b, i: (b, i, COL_WI // LANES)),
        ],
        out_specs=pl.BlockSpec((1, blk, ATT_W), lambda b, i: (b, i, 0)),
        out_shape=jax.ShapeDtypeStruct((B, S, ATT_W), BF16),
        scratch_shapes=[
            pltpu.VMEM((S // blk, blk, blk), F32),
            pltpu.VMEM((A_HEADS, n_off, blk, blk), F32),
            pltpu.VMEM((A_HEADS, 1, blk), F32),
            pltpu.VMEM((A_HEADS, 1, blk), F32),
            pltpu.VMEM((A_HEADS, LANES, blk), F32),
            pltpu.VMEM((2, A_HEADS, blk, blk), F32),
            pltpu.VMEM((2, A_HEADS, 1, blk), F32),
        ],
        compiler_params=_params(("arbitrary", "arbitrary")),
    )(bucket, rel_bias, proj3, proj3, vt4, proj3, proj3, proj3)


def _sb_kernel(q_ref, k_ref, v_ref, o_ref, hl_ref, z_ref, *, t, n_chunks):
    i = pl.program_id(2)
    n = i + 1
    odd = lax.rem(n, 2)
    top = i + odd
    npair = (n + odd) // 2
    lane = lax.broadcasted_iota(I32, (t, LANES), 1)
    lo_half = lane < HALF
    q2 = q_ref[0].astype(F32) * QK_SCALE
    q_m = (jnp.where(lo_half, q2, 0.0).astype(BF16), jnp.where(lo_half, 0.0, q2).astype(BF16))
    r = lax.broadcasted_iota(I32, (t, t), 0)
    cidx = lax.broadcasted_iota(I32, (t, t), 1)
    neg_from = jnp.where(r >= cidx, -1.0, 0.0).astype(BF16)
    diff = cidx - r

    def stage_terms(step, slot, masked):
        chunk = top - step
        c0 = pl.multiple_of(jnp.clip(chunk, 0, n_chunks - 1) * t, t)
        k2 = k_ref[0, pl.ds(c0, t), :]
        if masked:
            keep = diff < (i - chunk) * t
        for hh in range(2):
            z = lax.dot_general(q_m[hh], k2, NT_DIMS, preferred_element_type=F32)
            sp = jnp.maximum(z, 0.0) + jnp.log(1.0 + jnp.exp(-jnp.abs(z)))
            if masked:
                sp = jnp.where(keep, sp, 0.0)
                z = jnp.where(keep, z, NEG)
            hi = sp.astype(BF16)
            hl_ref[slot, 2 * hh] = hi
            hl_ref[slot, 2 * hh + 1] = (sp - hi.astype(F32)).astype(BF16)
            z_ref[slot, hh] = z

    def stage_apply(step, slot, carry):
        c0 = pl.multiple_of(jnp.minimum(top - step, n_chunks - 1) * t, t)
        v2 = v_ref[0, pl.ds(c0, t), :]
        cum4 = jnp.dot(hl_ref[slot].reshape(4 * t, t), neg_from, preferred_element_type=F32)
        out = []
        for hh in range(2):
            car, acc = carry[hh]
            cum = cum4[(2 * hh) * t:(2 * hh + 1) * t] + cum4[(2 * hh + 1) * t:(2 * hh + 2) * t]
            w = jnp.exp(z_ref[slot, hh] + cum + car)
            acc = acc + jnp.dot(w.astype(BF16), v2, preferred_element_type=F32)
            out.append((car + cum[:, 0:1], acc))
        return tuple(out)

    z1 = jnp.zeros((t, 1), F32)
    za = jnp.zeros((t, LANES), F32)
    stage_terms(0, 0, True)
    stage_terms(1, 1, True)
    carry = stage_apply(0, 0, ((z1, za), (z1, za)))
    stage_terms(2, 0, False)
    carry = stage_apply(1, 1, carry)

    def pair_body(pp, carry):
        step = 2 * pp
        stage_terms(step + 1, 1, False)
        carry = stage_apply(step, 0, carry)
        stage_terms(step + 2, 0, False)
        return stage_apply(step + 1, 1, carry)

    (_, acc_a), (_, acc_b) = lax.fori_loop(1, npair, pair_body, carry)
    o_ref[0] = jnp.where(lo_half, acc_a, acc_b).astype(o_ref.dtype)


def _stick_breaking(proj3):
    B, S, _ = proj3.shape
    t = SB_T
    qb, kb, vb = COL_QB // LANES, COL_KB // LANES, COL_VB // LANES
    return pl.pallas_call(
        functools.partial(_sb_kernel, t=t, n_chunks=S // t),
        grid=(B, N_PAIRS, S // t),
        in_specs=[
            pl.BlockSpec((1, t, LANES), lambda b, p, i: (b, i, qb + p)),
            pl.BlockSpec((1, S, LANES), lambda b, p, i: (b, 0, kb + p)),
            pl.BlockSpec((1, S, LANES), lambda b, p, i: (b, 0, vb + p)),
        ],
        out_specs=pl.BlockSpec((1, t, LANES), lambda b, p, i: (b, i, p)),
        out_shape=jax.ShapeDtypeStruct((B, S, ATT_W), BF16),
        scratch_shapes=[
            pltpu.VMEM((2, 4, t, t), BF16),
            pltpu.VMEM((2, 2, t, t), F32),
        ],
        compiler_params=_params(("arbitrary", "arbitrary", "arbitrary")),
    )(proj3, proj3, proj3)


def _layer_norm(r, g, b):
    mu = jnp.mean(r, axis=-1, keepdims=True)
    d = r - mu
    var = jnp.mean(d * d, axis=-1, keepdims=True)
    return d * lax.rsqrt(var + LN_EPS) * g + b


def _split_bf16(v):
    hi = v.astype(BF16)
    return hi, (v - hi.astype(F32)).astype(BF16)


def _merge_kernel(x_ref, ya_ref, yb_ref, wg_ref, wa_ref, wb_ref, wo_ref, g_ref, b_ref,
                  wr_ref, br_ref, h_ref, e_ref, p_ref, *, alpha, d):
    pa = jnp.dot(ya_ref[...], wa_ref[...], preferred_element_type=F32)
    pb = jnp.dot(yb_ref[...], wb_ref[...], preferred_element_type=F32)
    gates = jnp.dot(x_ref[...].astype(BF16), wg_ref[...], preferred_element_type=F32)
    merged = jax.nn.sigmoid(gates[:, :d]) * pa + jax.nn.sigmoid(gates[:, d:]) * pb
    m = jnp.dot(merged.astype(BF16), wo_ref[...], preferred_element_type=F32)
    h = _layer_norm(alpha * x_ref[...] + m, g_ref[...], b_ref[...])
    h_ref[...] = h

    h_hi, h_lo = _split_bf16(h)
    w_hi, w_lo = _split_bf16(wr_ref[...])
    logit = (lax.dot_general(w_hi, h_hi, NT_DIMS, preferred_element_type=F32)
             + lax.dot_general(w_hi, h_lo, NT_DIMS, preferred_element_type=F32)
             + lax.dot_general(w_lo, h_hi, NT_DIMS, preferred_element_type=F32)) + br_ref[...]
    eid = lax.broadcasted_iota(I32, logit.shape, 0)
    vals, ids = [], []
    for _ in range(TOP_K):
        mx = jnp.max(logit, axis=0, keepdims=True)
        am = jnp.min(jnp.where(logit == mx, eid, N_EXPERTS), axis=0, keepdims=True)
        vals.append(mx)
        ids.append(am)
        logit = jnp.where(eid == am, -jnp.inf, logit)
    ex = [jnp.exp(v - vals[0]) for v in vals]
    den = ex[0] + ex[1] + ex[2] + ex[3]
    for k in range(TOP_K):
        e_ref[k:k + 1, :] = ids[k]
        p_ref[k:k + 1, :] = ex[k] / den


def _merge(x2, ya, yb, wg, wa, wb, wo, g, b, wr_t, br, alpha):
    T, D = x2.shape
    tm = MERGE_TM
    row = lambda i: (i, 0)
    fixed = lambda i: (0, 0)
    return pl.pallas_call(
        functools.partial(_merge_kernel, alpha=alpha, d=D),
        grid=(T // tm,),
        in_specs=[
            pl.BlockSpec((tm, D), row),
            pl.BlockSpec((tm, ATT_W), row),
            pl.BlockSpec((tm, ATT_W), row),
            pl.BlockSpec((D, 2 * D), fixed),
            pl.BlockSpec((ATT_W, D), fixed),
            pl.BlockSpec((ATT_W, D), fixed),
            pl.BlockSpec((D, D), fixed),
            pl.BlockSpec((1, D), fixed),
            pl.BlockSpec((1, D), fixed),
            pl.BlockSpec((N_EXPERTS, D), fixed),
            pl.BlockSpec((N_EXPERTS, 1), fixed),
        ],
        out_specs=[
            pl.BlockSpec((tm, D), row),
            pl.BlockSpec((TOP_K, tm), lambda i: (0, i)),
            pl.BlockSpec((TOP_K, tm), lambda i: (0, i)),
        ],
        out_shape=[
            jax.ShapeDtypeStruct((T, D), F32),
            jax.ShapeDtypeStruct((TOP_K, T), I32),
            jax.ShapeDtypeStruct((TOP_K, T), F32),
        ],
        compiler_params=_params(("arbitrary",)),
    )(x2, ya, yb, wg, wa, wb, wo, g, b, wr_t, br)


def _sc_gather_rows(table, idx):
    n = idx.shape[0]
    d = table.shape[1]
    info = plsc.get_sparse_core_info()
    n_cores, n_sub = info.num_cores, info.num_subcores
    per_w = n // (n_cores * n_sub)
    c = SC_GATHER_ROWS
    n_g = per_w // c
    assert n == per_w * n_cores * n_sub and per_w == n_g * c and n_g % 2 == 0 and n_g >= 2
    mesh = plsc.VectorSubcoreMesh(core_axis_name="c", subcore_axis_name="s")

    @functools.partial(
        pl.kernel, mesh=mesh, out_type=jax.ShapeDtypeStruct((n, d), table.dtype),
        scratch_types=[pltpu.VMEM((per_w,), I32), pltpu.VMEM((2, c, d), table.dtype),
                       pltpu.SemaphoreType.DMA((2,)), pltpu.SemaphoreType.DMA((2,))])
    def gather_kernel(table_hbm, idx_hbm, out_hbm, idx_v, rows_v, gsem, wsem):
        base = (lax.axis_index("s") * n_cores + lax.axis_index("c")) * per_w
        pltpu.sync_copy(idx_hbm.at[pl.ds(base, per_w)], idx_v)

        def gather(g, b):
            return pltpu.make_async_copy(table_hbm.at[idx_v.at[pl.ds(g * c, c)]], rows_v.at[b],
                                         gsem.at[b])

        def write(g, b):
            return pltpu.make_async_copy(rows_v.at[b], out_hbm.at[pl.ds(base + g * c, c)], wsem.at[b])

        gather(0, 0).start()

        @pl.loop(0, n_g, step=2)
        def _ring(g0):
            for b in range(2):
                g = g0 + b

                @pl.when(g + 1 < n_g)
                def _next():
                    @pl.when(g >= 1)
                    def _buffer_free():
                        write(g - 1, 1 - b).wait()
                    gather(g + 1, 1 - b).start()

                gather(g, b).wait()
                write(g, b).start()

        write(n_g - 2, 0).wait()
        write(n_g - 1, 1).wait()

    return gather_kernel(table, idx)


def _moe_kernel(blk_e_ref, nused_ref, x_ref, wgu_ref, bgu_ref, wdn_ref, bdn_ref, o_ref,
                wgu_s, wdn_s, *, f):
    i = pl.program_id(0)
    nused = nused_ref[0]

    @pl.when(i < nused)
    def _compute():
        changed = jnp.logical_or(i == 0, blk_e_ref[i] != blk_e_ref[jnp.maximum(i - 1, 0)])

        @pl.when(changed)
        def _cast_weights():
            wgu_s[...] = wgu_ref[0].astype(BF16)
            wdn_s[...] = wdn_ref[0].astype(BF16)

        x = x_ref[...].astype(BF16)
        hgu = jnp.dot(x, wgu_s[...], preferred_element_type=F32) + bgu_ref[0]
        a = jnp.minimum(hgu[:, :f], SWIGLU_LIMIT)
        u = jnp.clip(hgu[:, f:], -SWIGLU_LIMIT, SWIGLU_LIMIT)
        glu = a * jax.nn.sigmoid(a * SWIGLU_ALPHA)
        o_ref[...] = jnp.dot(((u + 1.0) * glu).astype(BF16), wdn_s[...],
                             preferred_element_type=F32) + bdn_ref[0]

    @pl.when(i >= nused)
    def _unused_block():
        o_ref[...] = jnp.zeros(o_ref.shape, o_ref.dtype)


def _moe_ffn(xs, blk_e, nused, w_gu, b_gu, w_dn, b_dn):
    P, D = xs.shape
    E, _, F2 = w_gu.shape
    f = F2 // 2
    blk = MOE_BLK
    nb = P // blk
    used_block = lambda i, be, nu: (jnp.minimum(i, nu[0] - 1), 0)
    grid_spec = pltpu.PrefetchScalarGridSpec(
        num_scalar_prefetch=2,
        grid=(nb,),
        in_specs=[
            pl.BlockSpec((blk, D), used_block),
            pl.BlockSpec((1, D, F2), lambda i, be, nu: (be[i], 0, 0)),
            pl.BlockSpec((1, 1, F2), lambda i, be, nu: (be[i], 0, 0)),
            pl.BlockSpec((1, f, D), lambda i, be, nu: (be[i], 0, 0)),
            pl.BlockSpec((1, 1, D), lambda i, be, nu: (be[i], 0, 0)),
        ],
        out_specs=pl.BlockSpec((blk, D), lambda i, be, nu: (i, 0)),
        scratch_shapes=[
            pltpu.VMEM((D, F2), BF16),
            pltpu.VMEM((f, D), BF16),
        ],
    )
    return pl.pallas_call(
        functools.partial(_moe_kernel, f=f),
        grid_spec=grid_spec,
        out_shape=jax.ShapeDtypeStruct((P, D), F32),
        compiler_params=_params(("arbitrary",)),
    )(blk_e, nused, xs, w_gu, b_gu.reshape(E, 1, F2), w_dn, b_dn.reshape(E, 1, D))


def _comb_kernel(h_ref, p_ref, y_ref, g_ref, b_ref, *rest, alpha):
    o_ref = rest[-1]
    gate = p_ref[...]
    fsum = ((y_ref[0] * gate[:, 0:1] + y_ref[1] * gate[:, 1:2])
            + (y_ref[2] * gate[:, 2:3] + y_ref[3] * gate[:, 3:4]))
    o_ref[...] = _layer_norm(alpha * h_ref[...] + fsum, g_ref[...], b_ref[...])


def _combine(h, y4, top_p, g, b, alpha, t0, prev):
    T, D = h.shape
    tm = COMB_TM
    off = t0 // tm
    in_specs = [
        pl.BlockSpec((tm, D), lambda i: (i + off, 0)),
        pl.BlockSpec((tm, TOP_K), lambda i: (i, 0)),
        pl.BlockSpec((TOP_K, tm, D), lambda i: (0, i, 0)),
        pl.BlockSpec((1, D), lambda i: (0, 0)),
        pl.BlockSpec((1, D), lambda i: (0, 0)),
    ]
    args = [h, top_p.T, y4, g, b]
    aliases = {}
    if prev is not None:
        in_specs.append(pl.BlockSpec(memory_space=pl.ANY))
        args.append(prev)
        aliases = {len(args) - 1: 0}
    return pl.pallas_call(
        functools.partial(_comb_kernel, alpha=alpha),
        grid=(y4.shape[1] // tm,),
        in_specs=in_specs,
        out_specs=pl.BlockSpec((tm, D), lambda i: (i + off, 0)),
        out_shape=jax.ShapeDtypeStruct((T, D), F32),
        input_output_aliases=aliases,
        compiler_params=_params(("arbitrary",)),
    )(*args)


def _route(top_e, blk):
    K, T = top_e.shape
    N = K * T
    flat_e = top_e.reshape(N)
    experts = jnp.arange(N_EXPERTS, dtype=I32)
    order = jnp.argsort(flat_e, stable=True).astype(I32)
    inv = jnp.argsort(order).astype(I32)
    onehot = flat_e[:, None] == experts[None, :]
    counts = jnp.sum(onehot, axis=0, dtype=I32)
    padded = (counts + blk - 1) // blk * blk
    pends = jnp.cumsum(padded)
    offs = jnp.cumsum(counts) - counts
    shift = (pends - padded) - offs
    pos = inv + jnp.sum(jnp.where(onehot, shift[None, :], 0), axis=1, dtype=I32)
    P = N + N_EXPERTS * blk
    nb = P // blk
    blk_start = jnp.arange(nb, dtype=I32) * blk
    blk_e = jnp.minimum(jnp.sum(pends[None, :] <= blk_start[:, None], axis=1, dtype=I32), N_EXPERTS - 1)
    j = (blk_start - shift[blk_e])[:, None] + jnp.arange(blk, dtype=I32)[None, :]
    valid = j < (offs + counts)[blk_e][:, None]
    src = order[jnp.clip(j, 0, N - 1)]
    row_tok = jnp.where(valid, src % T, j % T).reshape(P)
    nused = (pends[-1:] // blk).astype(I32)
    return blk_e, nused, row_tok, pos.reshape(K, T)


def _projection_weights(w_in_l):
    sizes = (ATT_W, ATT_W, ATT_W, IDX_HEADS * IDX_DIM, IDX_DIM, IDX_HEADS, ATT_W, ATT_W, ATT_W)
    offs = np.concatenate([[0], np.cumsum(sizes)])
    qa, ka, va, qi, ki, wi, qb, kb, vb = (w_in_l[:, offs[n]:offs[n + 1]] for n in range(9))
    pad_wi = jnp.zeros((w_in_l.shape[0], LANES - IDX_HEADS), w_in_l.dtype)
    w_att = jnp.concatenate([qa, ka, qi, qb, kb, vb, ki, ki, wi, pad_wi], axis=1).astype(BF16)
    w_va_t = va.T.astype(BF16)
    w_gate = w_in_l[:, offs[9]:].astype(BF16)
    return w_att, w_va_t, w_gate


def kernel(x, w_in, w_branch_a, w_branch_b, w_out, rel_bias, ln1_g, ln1_b, w_router, b_router,
           w_gate_up, b_gate_up, w_down, b_down, ln2_g, ln2_b):
    B, S, D = x.shape
    depth = w_in.shape[0]
    alpha = (2 * depth) ** 0.25
    T = B * S
    h = x.reshape(T, D)
    for l in range(depth):
        w_att, w_va_t, w_gate = _projection_weights(w_in[l])
        proj = _matmul(h, w_att, BF16, min(T, 1024), PROJ_TN).reshape(B, S, ATT_COLS)
        vt = _matmul_t(h, w_va_t, DSA_BLK).reshape(B, S // DSA_BLK, ATT_W, DSA_BLK)
        ya = _dsa(proj, vt, rel_bias).reshape(T, ATT_W)
        yb = _stick_breaking(proj).reshape(T, ATT_W)
        h1, top_e, top_p = _merge(
            h, ya, yb, w_gate, w_branch_a[l].astype(BF16), w_branch_b[l].astype(BF16),
            w_out[l].astype(BF16), ln1_g[l].reshape(1, D), ln1_b[l].reshape(1, D),
            w_router[l].T, b_router[l].reshape(N_EXPERTS, 1), alpha)
        tg = T // MOE_GROUPS
        h = None
        for grp in range(MOE_GROUPS):
            t0 = grp * tg
            e_g, p_g = top_e[:, t0:t0 + tg], top_p[:, t0:t0 + tg]
            blk_e, nused, row_tok, pos = _route(e_g, MOE_BLK)
            xs = _sc_gather_rows(h1, row_tok + t0)
            ys = _moe_ffn(xs, blk_e, nused, w_gate_up[l], b_gate_up[l], w_down[l], b_down[l])
            y4 = _sc_gather_rows(ys, pos.reshape(TOP_K * tg)).reshape(TOP_K, tg, D)
            h = _combine(h1, y4, p_g, ln2_g[l].reshape(1, D), ln2_b[l].reshape(1, D), alpha, t0, h)
    return h.reshape(B, S, D)
```

```python
import functools
import math

import numpy as np
import jax
import jax.numpy as jnp
from jax import lax
from jax.experimental import pallas as pl
from jax.experimental.pallas import tpu as pltpu
from jax.experimental.pallas import tpu_sc as plsc

F32 = jnp.float32
BF16 = jnp.bfloat16
I32 = jnp.int32

A_HEADS = 8
HEAD_DIM = 64
ATT_W = A_HEADS * HEAD_DIM
IDX_HEADS = 8
IDX_DIM = 64
IDX_SCALE = (IDX_HEADS * IDX_DIM) ** -0.5
TOPK_MAX = 256
N_BUCKETS = 32
MAX_DISTANCE = 128
N_EXPERTS = 32
TOP_K = 4
SWIGLU_LIMIT = 7.0
SWIGLU_ALPHA = 1.702
LN_EPS = 1e-5
QK_SCALE = HEAD_DIM ** -0.5

LANES = 128
SUBLANES = 8
HALF = LANES // 2
N_PAIRS = A_HEADS // 2
VMEM_LIMIT = 56 * 1024 * 1024

DSA_BLK = 256
SB_T = 256
MERGE_TM = 512
MOE_BLK = 512
COMB_TM = 256
SC_GATHER_BYTES = 128 * 1024
BISECT_CAP = 24
NEG = -1e30

COL_QA, COL_KA, COL_QI, COL_QB, COL_KB, COL_VB = (g * ATT_W for g in range(6))
COL_KK = 6 * ATT_W
COL_WI = COL_KK + LANES
ATT_COLS = COL_WI + LANES
PROJ_TN = ATT_COLS // 2

NT_DIMS = (((1,), (1,)), ((), ()))


def _params(sem, vmem=VMEM_LIMIT):
    return pltpu.CompilerParams(dimension_semantics=sem, vmem_limit_bytes=vmem)


def _proj_kernel(x_ref, w_ref, wt_ref, o_ref, ot_ref, xb_ref, *, tt):
    @pl.when(pl.program_id(1) == 0)
    def _row_tile_start():
        xb_ref[...] = x_ref[...].astype(BF16)
        for r in range(ot_ref.shape[0]):
            ot_ref[r] = lax.dot_general(wt_ref[...], xb_ref[r * tt:(r + 1) * tt, :], NT_DIMS,
                                        preferred_element_type=F32).astype(ot_ref.dtype)

    o_ref[...] = jnp.dot(xb_ref[...], w_ref[...], preferred_element_type=F32).astype(o_ref.dtype)


def _projection(x, w, w_t, tm, tn, tt):
    M, K = x.shape
    N = w.shape[1]
    Nt = w_t.shape[0]
    return pl.pallas_call(
        functools.partial(_proj_kernel, tt=tt),
        grid=(M // tm, N // tn),
        in_specs=[pl.BlockSpec((tm, K), lambda i, j: (i, 0)),
                  pl.BlockSpec((K, tn), lambda i, j: (0, j)),
                  pl.BlockSpec((Nt, K), lambda i, j: (0, 0))],
        out_specs=[pl.BlockSpec((tm, tn), lambda i, j: (i, j)),
                   pl.BlockSpec((tm // tt, Nt, tt), lambda i, j: (i, 0, 0))],
        out_shape=[jax.ShapeDtypeStruct((M, N), BF16),
                   jax.ShapeDtypeStruct((M // tt, Nt, tt), BF16)],
        scratch_shapes=[pltpu.VMEM((tm, K), BF16)],
        compiler_params=_params(("arbitrary", "arbitrary")),
    )(x, w, w_t)


def _t5_bucket_np(n):
    n = np.maximum(n, 0)
    max_exact = N_BUCKETS // 2
    nf = np.maximum(n, 1).astype(np.float32)
    large = max_exact + (np.log(nf / max_exact) / math.log(MAX_DISTANCE / max_exact)
                         * (N_BUCKETS - max_exact)).astype(np.int32)
    large = np.minimum(large, N_BUCKETS - 1)
    return np.where(n < max_exact, n, large).astype(np.int32)


def _dsa_n_off(blk):
    return 2 + -(-MAX_DISTANCE // blk)


def _dsa_bucket_tiles(blk):
    n_off = _dsa_n_off(blk)
    j = np.arange(blk)[None, :, None]
    i = np.arange(blk)[None, None, :]
    o = np.arange(n_off)[:, None, None]
    return _t5_bucket_np(i - j + blk * (n_off - 1 - o))


def _dsa_kernel(bucket_ref, relb_ref, q_ref, k_ref, vt_ref, qi_ref, kk_ref, wi_ref, o_ref,
                sc_ref, bias_ref, m_ref, l_ref, acc_ref, st_ref, mx_ref,
                *, blk, n_chunks, n_sel, n_off):
    b = pl.program_id(0)
    i = pl.program_id(1)
    q0 = i * blk
    nck = i + 1
    groups = blk // SUBLANES

    @pl.when(jnp.logical_and(b == 0, i == 0))
    def _build_bias():
        def head_body(h, _):
            for o in range(n_off):
                for rb in range(blk // LANES):
                    for cb in range(blk // LANES):
                        rs = slice(rb * LANES, (rb + 1) * LANES)
                        cs = slice(cb * LANES, (cb + 1) * LANES)
                        bk = bucket_ref[o, rs, cs]

                        def bucket_body(n, acc):
                            return jnp.where(bk == n, relb_ref[n, h], acc)

                        bias_ref[h, o, rs, cs] = lax.fori_loop(
                            0, N_BUCKETS, bucket_body, jnp.zeros((LANES, LANES), F32))
            return 0

        lax.fori_loop(0, A_HEADS, head_body, 0)

    lane = lax.broadcasted_iota(I32, (blk, LANES), 1)
    lo_half = lane < HALF
    krow = lax.broadcasted_iota(I32, (blk, blk), 0)
    qpos = q0 + lax.broadcasted_iota(I32, (1, blk), 1)

    def pair_split(ref, scale):
        out = []
        for p in range(N_PAIRS):
            v = ref[0, :, p * LANES:(p + 1) * LANES].astype(F32)
            if scale != 1.0:
                v = v * scale
            out.append(jnp.where(lo_half, v, 0.0).astype(BF16))
            out.append(jnp.where(lo_half, 0.0, v).astype(BF16))
        return out

    wi_t = wi_ref[0].astype(F32).T
    wrow = [wi_t[h:h + 1, :] * IDX_SCALE for h in range(IDX_HEADS)]
    qi_m = pair_split(qi_ref, 1.0)

    def score_chunk(c, _):
        c0 = pl.multiple_of(c * blk, blk)
        kk = kk_ref[0, pl.ds(c0, blk), :]
        acc = jnp.zeros((blk, blk), F32)
        for h in range(IDX_HEADS):
            s = lax.dot_general(kk, qi_m[h], NT_DIMS, preferred_element_type=F32)
            acc = acc + wrow[h] * jnp.maximum(s, 0.0)
        sc_ref[c] = jnp.where(c0 + krow <= qpos, acc, -jnp.inf)
        return 0

    lax.fori_loop(0, nck, score_chunk, 0)

    kt = jnp.minimum(qpos + 1, n_sel).astype(F32)

    def fold(fn, init):
        def body(c, acc):
            return fn(acc, sc_ref[c])
        return lax.fori_loop(0, nck, body, init)

    def part(x, op):
        return op(x.reshape(groups, SUBLANES, blk), axis=0)

    def fin(x, op):
        return op(x, axis=0, keepdims=True)

    zeros8 = jnp.zeros((SUBLANES, blk), F32)
    pinf8 = jnp.full((SUBLANES, blk), jnp.inf, F32)

    def count_ge(th):
        return fin(fold(lambda a, s: a + part(jnp.where(s >= th, 1.0, 0.0), jnp.sum), zeros8),
                   jnp.sum)

    mn, mx = fold(lambda a, s: (
        jnp.minimum(a[0], part(jnp.where(s == -jnp.inf, jnp.inf, s), jnp.min)),
        jnp.maximum(a[1], part(s, jnp.max))), (pinf8, -pinf8))
    rmin = fin(mn, jnp.min)
    rmax = fin(mx, jnp.max)

    def bis_cond(st):
        it, lo, hi, clo = st
        return jnp.logical_and(it < BISECT_CAP, jnp.max(jnp.abs(clo - kt)) > 0.0)

    def halve(lo, hi, clo):
        mid = 0.5 * lo + 0.5 * hi
        c = count_ge(mid)
        active = clo != kt
        up = jnp.logical_and(active, c >= kt)
        dn = jnp.logical_and(active, c < kt)
        return jnp.where(up, mid, lo), jnp.where(dn, mid, hi), jnp.where(up, c, clo)

    def bis_body(st):
        it, lo, hi, clo = st
        return (it + 2,) + halve(*halve(lo, hi, clo))

    _, lo, _, _ = lax.while_loop(
        bis_cond, bis_body, (jnp.int32(0), rmin, rmax + 1.0, (qpos + 1).astype(F32)))

    def stats(lo_):
        a_ = fin(fold(lambda a, s: jnp.minimum(a, part(jnp.where(s >= lo_, s, jnp.inf), jnp.min)),
                      pinf8), jnp.min)
        cg, ct, nx = fold(
            lambda a, s: (a[0] + part(jnp.where(s > a_, 1.0, 0.0), jnp.sum),
                               a[1] + part(jnp.where(s == a_, 1.0, 0.0), jnp.sum),
                               jnp.minimum(a[2], part(jnp.where(s > a_, s, jnp.inf), jnp.min))),
            (zeros8, zeros8, pinf8))
        return a_, fin(cg, jnp.sum), fin(ct, jnp.sum), fin(nx, jnp.min)

    def fin_cond(st):
        return st[0]

    def fin_body(st):
        _, lo_, _, _ = st
        a_, cgt_, nt_, nxt_ = stats(lo_)
        bad = cgt_ >= kt
        return (jnp.max(jnp.where(bad, 1.0, 0.0)) > 0.0, jnp.where(bad, nxt_, a_), cgt_, nt_)

    _, a, cgt, nties = lax.while_loop(fin_cond, fin_body, (jnp.bool_(True), lo, kt, kt))
    need = kt - cgt
    excess = jnp.max(jnp.where(nties > need, 1.0, 0.0)) > 0.0

    def mask_plain():
        def body(c, _):
            sc_ref[c] = jnp.where(sc_ref[c] >= a, 0.0, NEG)
            return 0
        lax.fori_loop(0, nck, body, 0)

    def mask_ties():
        upto = (krow >= lax.broadcasted_iota(I32, (blk, blk), 1)).astype(BF16)

        def body(c, seen):
            s = sc_ref[c]
            tie = s == a
            rank = jnp.dot(upto, jnp.where(tie, 1.0, 0.0).astype(BF16),
                           preferred_element_type=F32) + seen
            sel = jnp.logical_or(s > a, jnp.logical_and(tie, rank <= need))
            sc_ref[c] = jnp.where(sel, 0.0, NEG)
            return rank[blk - 1:blk, :]

        lax.fori_loop(0, nck, body, jnp.zeros((1, blk), F32))

    lax.cond(excess, mask_ties, mask_plain)

    odd = lax.rem(nck, 2)
    npair = (nck + odd) // 2

    @pl.when(odd == 1)
    def _mask_extra_chunk():
        sc_ref[nck] = jnp.full((blk, blk), NEG, F32)

    m_ref[...] = jnp.full(m_ref.shape, NEG, F32)
    l_ref[...] = jnp.zeros(l_ref.shape, F32)
    acc_ref[...] = jnp.zeros(acc_ref.shape, F32)
    q_m = pair_split(q_ref, QK_SCALE)

    def stage_logits(c, slot):
        c = jnp.minimum(c, n_chunks - 1)
        c0 = pl.multiple_of(c * blk, blk)
        madd = sc_ref[c]
        o_idx = jnp.clip(c - i + (n_off - 1), 0, n_off - 1)
        for p in range(N_PAIRS):
            k2 = k_ref[0, pl.ds(c0, blk), p * LANES:(p + 1) * LANES]
            for hh in range(2):
                h = 2 * p + hh
                s = lax.dot_general(k2, q_m[h], NT_DIMS, preferred_element_type=F32)
                s = s + bias_ref[h, o_idx] + madd
                st_ref[slot, h] = s
                mx_ref[slot, h] = jnp.max(s, axis=0, keepdims=True)

    def stage_values(c, slot):
        for p in range(N_PAIRS):
            vt2 = vt_ref[0, c, p * LANES:(p + 1) * LANES, :]
            for hh in range(2):
                h = 2 * p + hh
                m_old = m_ref[h]
                m_new = jnp.maximum(m_old, mx_ref[slot, h])
                alpha = jnp.exp(m_old - m_new)
                pexp = jnp.exp(st_ref[slot, h] - m_new)
                l_ref[h] = alpha * l_ref[h] + jnp.sum(pexp, axis=0, keepdims=True)
                acc_ref[h] = alpha * acc_ref[h] + jnp.dot(vt2, pexp.astype(BF16),
                                                          preferred_element_type=F32)
                m_ref[h] = m_new

    stage_logits(0, 0)

    def att_pair(pp, _):
        c = 2 * pp
        stage_logits(c + 1, 1)
        stage_values(c, 0)
        stage_logits(c + 2, 0)
        stage_values(c + 1, 1)
        return 0

    lax.fori_loop(0, npair, att_pair, 0)

    lo_rows = lax.broadcasted_iota(I32, (LANES, blk), 0) < HALF
    for p in range(N_PAIRS):
        oa = acc_ref[2 * p] / l_ref[2 * p]
        ob = acc_ref[2 * p + 1] / l_ref[2 * p + 1]
        o_ref[0, :, p * LANES:(p + 1) * LANES] = jnp.where(lo_rows, oa, ob).T.astype(o_ref.dtype)


def _dsa(proj3, vt4, rel_bias):
    B, S, _ = proj3.shape
    blk = DSA_BLK
    n_off = _dsa_n_off(blk)
    n_sel = min(TOPK_MAX, S // 4)
    bucket = jnp.asarray(_dsa_bucket_tiles(blk))
    n_chunks = S // blk
    assert S % blk == 0 and n_chunks % 2 == 0, "DSA consumes key chunks in pairs"
    kern = functools.partial(_dsa_kernel, blk=blk, n_chunks=n_chunks, n_sel=n_sel, n_off=n_off)
    return pl.pallas_call(
        kern,
        grid=(B, S // blk),
        in_specs=[
            pl.BlockSpec((n_off, blk, blk), lambda b, i: (0, 0, 0)),
            pl.BlockSpec(memory_space=pltpu.SMEM),
            pl.BlockSpec((1, blk, ATT_W), lambda b, i: (b, i, COL_QA // ATT_W)),
            pl.BlockSpec((1, S, ATT_W), lambda b, i: (b, 0, COL_KA // ATT_W)),
            pl.BlockSpec((1, S // blk, ATT_W, blk), lambda b, i: (b, 0, 0, 0)),
            pl.BlockSpec((1, blk, ATT_W), lambda b, i: (b, i, COL_QI // ATT_W)),
            pl.BlockSpec((1, S, LANES), lambda b, i: (b, 0, COL_KK // LANES)),
            pl.BlockSpec((1, blk, LANES), lambda b, i: (b, i, COL_WI // LANES)),
        ],
        out_specs=pl.BlockSpec((1, blk, ATT_W), lambda b, i: (b, i, 0)),
        out_shape=jax.ShapeDtypeStruct((B, S, ATT_W), BF16),
        scratch_shapes=[
            pltpu.VMEM((S // blk, blk, blk), F32),
            pltpu.VMEM((A_HEADS, n_off, blk, blk), F32),
            pltpu.VMEM((A_HEADS, 1, blk), F32),
            pltpu.VMEM((A_HEADS, 1, blk), F32),
            pltpu.VMEM((A_HEADS, LANES, blk), F32),
            pltpu.VMEM((2, A_HEADS, blk, blk), F32),
            pltpu.VMEM((2, A_HEADS, 1, blk), F32),
        ],
        compiler_params=_params(("arbitrary", "arbitrary")),
    )(bucket, rel_bias, proj3, proj3, vt4, proj3, proj3, proj3)


def _sb_kernel(q_ref, k_ref, v_ref, o_ref, hl_ref, z_ref, *, t, n_chunks):
    i = pl.program_id(2)
    n = i + 1
    odd = lax.rem(n, 2)
    top = i + odd
    npair = (n + odd) // 2
    lane = lax.broadcasted_iota(I32, (t, LANES), 1)
    lo_half = lane < HALF
    q2 = q_ref[0].astype(F32) * QK_SCALE
    q_m = (jnp.where(lo_half, q2, 0.0).astype(BF16), jnp.where(lo_half, 0.0, q2).astype(BF16))
    r = lax.broadcasted_iota(I32, (t, t), 0)
    cidx = lax.broadcasted_iota(I32, (t, t), 1)
    neg_from = jnp.where(r >= cidx, -1.0, 0.0).astype(BF16)
    diff = cidx - r

    def stage_terms(step, slot, masked):
        chunk = top - step
        c0 = pl.multiple_of(jnp.clip(chunk, 0, n_chunks - 1) * t, t)
        k2 = k_ref[0, pl.ds(c0, t), :]
        if masked:
            keep = diff < (i - chunk) * t
        for hh in range(2):
            z = lax.dot_general(q_m[hh], k2, NT_DIMS, preferred_element_type=F32)
            sp = jnp.maximum(z, 0.0) + jnp.log(1.0 + jnp.exp(-jnp.abs(z)))
            if masked:
                sp = jnp.where(keep, sp, 0.0)
                z = jnp.where(keep, z, NEG)
            hi = sp.astype(BF16)
            hl_ref[slot, 2 * hh] = hi
            hl_ref[slot, 2 * hh + 1] = (sp - hi.astype(F32)).astype(BF16)
            z_ref[slot, hh] = z

    def stage_apply(step, slot, carry):
        c0 = pl.multiple_of(jnp.minimum(top - step, n_chunks - 1) * t, t)
        v2 = v_ref[0, pl.ds(c0, t), :]
        cum4 = jnp.dot(hl_ref[slot].reshape(4 * t, t), neg_from, preferred_element_type=F32)
        out = []
        for hh in range(2):
            car, acc = carry[hh]
            cum = cum4[(2 * hh) * t:(2 * hh + 1) * t] + cum4[(2 * hh + 1) * t:(2 * hh + 2) * t]
            w = jnp.exp(z_ref[slot, hh] + cum + car)
            acc = acc + jnp.dot(w.astype(BF16), v2, preferred_element_type=F32)
            out.append((car + cum[:, 0:1], acc))
        return tuple(out)

    z1 = jnp.zeros((t, 1), F32)
    za = jnp.zeros((t, LANES), F32)
    stage_terms(0, 0, True)
    stage_terms(1, 1, True)
    carry = stage_apply(0, 0, ((z1, za), (z1, za)))
    stage_terms(2, 0, False)
    carry = stage_apply(1, 1, carry)

    def pair_body(pp, carry):
        step = 2 * pp
        stage_terms(step + 1, 1, False)
        carry = stage_apply(step, 0, carry)
        stage_terms(step + 2, 0, False)
        return stage_apply(step + 1, 1, carry)

    (_, acc_a), (_, acc_b) = lax.fori_loop(1, npair, pair_body, carry)
    o_ref[0] = jnp.where(lo_half, acc_a, acc_b).astype(o_ref.dtype)


def _stick_breaking(proj3):
    B, S, _ = proj3.shape
    t = SB_T
    qb, kb, vb = COL_QB // LANES, COL_KB // LANES, COL_VB // LANES
    return pl.pallas_call(
        functools.partial(_sb_kernel, t=t, n_chunks=S // t),
        grid=(B, N_PAIRS, S // t),
        in_specs=[
            pl.BlockSpec((1, t, LANES), lambda b, p, i: (b, i, qb + p)),
            pl.BlockSpec((1, S, LANES), lambda b, p, i: (b, 0, kb + p)),
            pl.BlockSpec((1, S, LANES), lambda b, p, i: (b, 0, vb + p)),
        ],
        out_specs=pl.BlockSpec((1, t, LANES), lambda b, p, i: (b, i, p)),
        out_shape=jax.ShapeDtypeStruct((B, S, ATT_W), BF16),
        scratch_shapes=[
            pltpu.VMEM((2, 4, t, t), BF16),
            pltpu.VMEM((2, 2, t, t), F32),
        ],
        compiler_params=_params(("arbitrary", "arbitrary", "arbitrary")),
    )(proj3, proj3, proj3)


def _layer_norm(r, g, b):
    mu = jnp.mean(r, axis=-1, keepdims=True)
    d = r - mu
    var = jnp.mean(d * d, axis=-1, keepdims=True)
    return d * lax.rsqrt(var + LN_EPS) * g + b


def _split_bf16(v):
    hi = v.astype(BF16)
    return hi, (v - hi.astype(F32)).astype(BF16)


def _merge_kernel(x_ref, ya_ref, yb_ref, wg_ref, wa_ref, wb_ref, wo_ref, g_ref, b_ref,
                  wr_ref, br_ref, h_ref, hp_ref, e_ref, p_ref, *, alpha, d):
    pa = jnp.dot(ya_ref[...], wa_ref[...], preferred_element_type=F32)
    pb = jnp.dot(yb_ref[...], wb_ref[...], preferred_element_type=F32)
    gates = jnp.dot(x_ref[...].astype(BF16), wg_ref[...], preferred_element_type=F32)
    merged = jax.nn.sigmoid(gates[:, :d]) * pa + jax.nn.sigmoid(gates[:, d:]) * pb
    m = jnp.dot(merged.astype(BF16), wo_ref[...], preferred_element_type=F32)
    h = _layer_norm(alpha * x_ref[...] + m, g_ref[...], b_ref[...])
    h_ref[...] = h
    bits = pltpu.bitcast(h.astype(BF16).astype(F32), jnp.uint32)
    hp_ref[...] = bits[:, :d // 2] | (bits[:, d // 2:] >> 16)

    h_hi, h_lo = _split_bf16(h)
    w_hi, w_lo = _split_bf16(wr_ref[...])
    logit = (lax.dot_general(w_hi, h_hi, NT_DIMS, preferred_element_type=F32)
             + lax.dot_general(w_hi, h_lo, NT_DIMS, preferred_element_type=F32)
             + lax.dot_general(w_lo, h_hi, NT_DIMS, preferred_element_type=F32)) + br_ref[...]
    eid = lax.broadcasted_iota(I32, logit.shape, 0)
    vals, ids = [], []
    for _ in range(TOP_K):
        mx = jnp.max(logit, axis=0, keepdims=True)
        am = jnp.min(jnp.where(logit == mx, eid, N_EXPERTS), axis=0, keepdims=True)
        vals.append(mx)
        ids.append(am)
        logit = jnp.where(eid == am, -jnp.inf, logit)
    ex = [jnp.exp(v - vals[0]) for v in vals]
    den = ex[0] + ex[1] + ex[2] + ex[3]
    for k in range(TOP_K):
        e_ref[k:k + 1, :] = ids[k]
        p_ref[k:k + 1, :] = ex[k] / den


def _merge(x2, ya, yb, wg, wa, wb, wo, g, b, wr_t, br, alpha):
    T, D = x2.shape
    tm = MERGE_TM
    row = lambda i: (i, 0)
    fixed = lambda i: (0, 0)
    return pl.pallas_call(
        functools.partial(_merge_kernel, alpha=alpha, d=D),
        grid=(T // tm,),
        in_specs=[
            pl.BlockSpec((tm, D), row),
            pl.BlockSpec((tm, ATT_W), row),
            pl.BlockSpec((tm, ATT_W), row),
            pl.BlockSpec((D, 2 * D), fixed),
            pl.BlockSpec((ATT_W, D), fixed),
            pl.BlockSpec((ATT_W, D), fixed),
            pl.BlockSpec((D, D), fixed),
            pl.BlockSpec((1, D), fixed),
            pl.BlockSpec((1, D), fixed),
            pl.BlockSpec((N_EXPERTS, D), fixed),
            pl.BlockSpec((N_EXPERTS, 1), fixed),
        ],
        out_specs=[
            pl.BlockSpec((tm, D), row),
            pl.BlockSpec((tm, D // 2), row),
            pl.BlockSpec((TOP_K, tm), lambda i: (0, i)),
            pl.BlockSpec((TOP_K, tm), lambda i: (0, i)),
        ],
        out_shape=[
            jax.ShapeDtypeStruct((T, D), F32),
            jax.ShapeDtypeStruct((T, D // 2), jnp.uint32),
            jax.ShapeDtypeStruct((TOP_K, T), I32),
            jax.ShapeDtypeStruct((TOP_K, T), F32),
        ],
        compiler_params=_params(("arbitrary",)),
    )(x2, ya, yb, wg, wa, wb, wo, g, b, wr_t, br)


def _sc_gather_rows(table, idx):
    n = idx.shape[0]
    d = table.shape[1]
    info = plsc.get_sparse_core_info()
    n_cores, n_sub = info.num_cores, info.num_subcores
    per_w = n // (n_cores * n_sub)
    c = SC_GATHER_BYTES // (d * table.dtype.itemsize)
    n_g = per_w // c
    assert n == per_w * n_cores * n_sub and per_w == n_g * c and n_g % 2 == 0 and n_g >= 2
    mesh = plsc.VectorSubcoreMesh(core_axis_name="c", subcore_axis_name="s")

    @functools.partial(
        pl.kernel, mesh=mesh, out_type=jax.ShapeDtypeStruct((n, d), table.dtype),
        scratch_types=[pltpu.VMEM((per_w,), I32), pltpu.VMEM((2, c, d), table.dtype),
                       pltpu.SemaphoreType.DMA((2,)), pltpu.SemaphoreType.DMA((2,))])
    def gather_kernel(table_hbm, idx_hbm, out_hbm, idx_v, rows_v, gsem, wsem):
        base = (lax.axis_index("s") * n_cores + lax.axis_index("c")) * per_w
        pltpu.sync_copy(idx_hbm.at[pl.ds(base, per_w)], idx_v)

        def gather(g, b):
            return pltpu.make_async_copy(table_hbm.at[idx_v.at[pl.ds(g * c, c)]], rows_v.at[b],
                                         gsem.at[b])

        def write(g, b):
            return pltpu.make_async_copy(rows_v.at[b], out_hbm.at[pl.ds(base + g * c, c)], wsem.at[b])

        gather(0, 0).start()

        @pl.loop(0, n_g, step=2)
        def _ring(g0):
            for b in range(2):
                g = g0 + b

                @pl.when(g + 1 < n_g)
                def _next():
                    @pl.when(g >= 1)
                    def _buffer_free():
                        write(g - 1, 1 - b).wait()
                    gather(g + 1, 1 - b).start()

                gather(g, b).wait()
                write(g, b).start()

        write(n_g - 2, 0).wait()
        write(n_g - 1, 1).wait()

    return gather_kernel(table, idx)


def _moe_kernel(blk_e_ref, nused_ref, x_ref, wgu_ref, bgu_ref, wdn_ref, bdn_ref, o_ref,
                wgu_s, wdn_s, *, f):
    i = pl.program_id(0)
    nused = nused_ref[0]

    @pl.when(i < nused)
    def _compute():
        changed = jnp.logical_or(i == 0, blk_e_ref[i] != blk_e_ref[jnp.maximum(i - 1, 0)])

        @pl.when(changed)
        def _cast_weights():
            wgu_s[...] = wgu_ref[0].astype(BF16)
            wdn_s[...] = wdn_ref[0].astype(BF16)

        xw = x_ref[...]
        x = jnp.concatenate([pltpu.bitcast(xw & jnp.uint32(0xFFFF0000), F32),
                             pltpu.bitcast(xw << 16, F32)], axis=1).astype(BF16)
        hgu = jnp.dot(x, wgu_s[...], preferred_element_type=F32) + bgu_ref[0]
        a = jnp.minimum(hgu[:, :f], SWIGLU_LIMIT)
        u = jnp.clip(hgu[:, f:], -SWIGLU_LIMIT, SWIGLU_LIMIT)
        glu = a * jax.nn.sigmoid(a * SWIGLU_ALPHA)
        o_ref[...] = jnp.dot(((u + 1.0) * glu).astype(BF16), wdn_s[...],
                             preferred_element_type=F32) + bdn_ref[0]

    @pl.when(i >= nused)
    def _unused_block():
        o_ref[...] = jnp.zeros(o_ref.shape, o_ref.dtype)


def _moe_ffn(xs, blk_e, nused, w_gu, b_gu, w_dn, b_dn):
    P = xs.shape[0]
    E, D, F2 = w_gu.shape
    f = F2 // 2
    blk = MOE_BLK
    nb = P // blk
    used_block = lambda i, be, nu: (jnp.minimum(i, nu[0] - 1), 0)
    grid_spec = pltpu.PrefetchScalarGridSpec(
        num_scalar_prefetch=2,
        grid=(nb,),
        in_specs=[
            pl.BlockSpec((blk, D // 2), used_block),
            pl.BlockSpec((1, D, F2), lambda i, be, nu: (be[i], 0, 0)),
            pl.BlockSpec((1, 1, F2), lambda i, be, nu: (be[i], 0, 0)),
            pl.BlockSpec((1, f, D), lambda i, be, nu: (be[i], 0, 0)),
            pl.BlockSpec((1, 1, D), lambda i, be, nu: (be[i], 0, 0)),
        ],
        out_specs=pl.BlockSpec((blk, D), lambda i, be, nu: (i, 0)),
        scratch_shapes=[
            pltpu.VMEM((D, F2), BF16),
            pltpu.VMEM((f, D), BF16),
        ],
    )
    return pl.pallas_call(
        functools.partial(_moe_kernel, f=f),
        grid_spec=grid_spec,
        out_shape=jax.ShapeDtypeStruct((P, D), F32),
        compiler_params=_params(("arbitrary",)),
    )(blk_e, nused, xs, w_gu, b_gu.reshape(E, 1, F2), w_dn, b_dn.reshape(E, 1, D))


def _comb_kernel(h_ref, p_ref, y_ref, g_ref, b_ref, o_ref, *, alpha):
    gate = p_ref[...]
    fsum = ((y_ref[0] * gate[:, 0:1] + y_ref[1] * gate[:, 1:2])
            + (y_ref[2] * gate[:, 2:3] + y_ref[3] * gate[:, 3:4]))
    o_ref[...] = _layer_norm(alpha * h_ref[...] + fsum, g_ref[...], b_ref[...])


def _combine(h, y4, top_p, g, b, alpha):
    T, D = h.shape
    tm = COMB_TM
    return pl.pallas_call(
        functools.partial(_comb_kernel, alpha=alpha),
        grid=(T // tm,),
        in_specs=[
            pl.BlockSpec((tm, D), lambda i: (i, 0)),
            pl.BlockSpec((tm, TOP_K), lambda i: (i, 0)),
            pl.BlockSpec((TOP_K, tm, D), lambda i: (0, i, 0)),
            pl.BlockSpec((1, D), lambda i: (0, 0)),
            pl.BlockSpec((1, D), lambda i: (0, 0)),
        ],
        out_specs=pl.BlockSpec((tm, D), lambda i: (i, 0)),
        out_shape=jax.ShapeDtypeStruct((T, D), F32),
        compiler_params=_params(("arbitrary",)),
    )(h, top_p.T, y4, g, b)


def _route(top_e, blk):
    K, T = top_e.shape
    N = K * T
    flat_e = top_e.reshape(N)
    experts = jnp.arange(N_EXPERTS, dtype=I32)
    order = jnp.argsort(flat_e, stable=True).astype(I32)
    inv = jnp.argsort(order).astype(I32)
    onehot = flat_e[:, None] == experts[None, :]
    counts = jnp.sum(onehot, axis=0, dtype=I32)
    padded = (counts + blk - 1) // blk * blk
    pends = jnp.cumsum(padded)
    offs = jnp.cumsum(counts) - counts
    shift = (pends - padded) - offs
    pos = inv + jnp.sum(jnp.where(onehot, shift[None, :], 0), axis=1, dtype=I32)
    P = N + N_EXPERTS * blk
    nb = P // blk
    blk_start = jnp.arange(nb, dtype=I32) * blk
    blk_e = jnp.minimum(jnp.sum(pends[None, :] <= blk_start[:, None], axis=1, dtype=I32), N_EXPERTS - 1)
    j = (blk_start - shift[blk_e])[:, None] + jnp.arange(blk, dtype=I32)[None, :]
    valid = j < (offs + counts)[blk_e][:, None]
    src = order[jnp.clip(j, 0, N - 1)]
    row_tok = jnp.where(valid, src % T, j % T).reshape(P)
    nused = (pends[-1:] // blk).astype(I32)
    return blk_e, nused, row_tok, pos.reshape(K, T)


def _projection_weights(w_in_l):
    sizes = (ATT_W, ATT_W, ATT_W, IDX_HEADS * IDX_DIM, IDX_DIM, IDX_HEADS, ATT_W, ATT_W, ATT_W)
    offs = np.concatenate([[0], np.cumsum(sizes)])
    qa, ka, va, qi, ki, wi, qb, kb, vb = (w_in_l[:, offs[n]:offs[n + 1]] for n in range(9))
    pad_wi = jnp.zeros((w_in_l.shape[0], LANES - IDX_HEADS), w_in_l.dtype)
    w_att = jnp.concatenate([qa, ka, qi, qb, kb, vb, ki, ki, wi, pad_wi], axis=1).astype(BF16)
    w_va_t = va.T.astype(BF16)
    w_gate = w_in_l[:, offs[9]:].astype(BF16)
    return w_att, w_va_t, w_gate


def kernel(x, w_in, w_branch_a, w_branch_b, w_out, rel_bias, ln1_g, ln1_b, w_router, b_router,
           w_gate_up, b_gate_up, w_down, b_down, ln2_g, ln2_b):
    B, S, D = x.shape
    depth = w_in.shape[0]
    alpha = (2 * depth) ** 0.25
    T = B * S
    h = x.reshape(T, D)
    for l in range(depth):
        w_att, w_va_t, w_gate = _projection_weights(w_in[l])
        proj, vt = _projection(h, w_att, w_va_t, min(T, 1024), PROJ_TN, DSA_BLK)
        proj = proj.reshape(B, S, ATT_COLS)
        vt = vt.reshape(B, S // DSA_BLK, ATT_W, DSA_BLK)
        ya = _dsa(proj, vt, rel_bias).reshape(T, ATT_W)
        yb = _stick_breaking(proj).reshape(T, ATT_W)
        h1, h1_packed, top_e, top_p = _merge(
            h, ya, yb, w_gate, w_branch_a[l].astype(BF16), w_branch_b[l].astype(BF16),
            w_out[l].astype(BF16), ln1_g[l].reshape(1, D), ln1_b[l].reshape(1, D),
            w_router[l].T, b_router[l].reshape(N_EXPERTS, 1), alpha)
        blk_e, nused, row_tok, pos = _route(top_e, MOE_BLK)
        xs = _sc_gather_rows(h1_packed, row_tok)
        ys = _moe_ffn(xs, blk_e, nused, w_gate_up[l], b_gate_up[l], w_down[l], b_down[l])
        y4 = _sc_gather_rows(ys, pos.reshape(TOP_K * T)).reshape(TOP_K, T, D)
        h = _combine(h1, y4, top_p, ln2_g[l].reshape(1, D), ln2_b[l].reshape(1, D), alpha)
    return h.reshape(B, S, D)
```

```python
import functools
import math

import numpy as np
import jax
import jax.numpy as jnp
from jax import lax
from jax.experimental import pallas as pl
from jax.experimental.pallas import tpu as pltpu
from jax.experimental.pallas import tpu_sc as plsc

F32 = jnp.float32
BF16 = jnp.bfloat16
I32 = jnp.int32

A_HEADS = 8
HEAD_DIM = 64
ATT_W = A_HEADS * HEAD_DIM
IDX_HEADS = 8
IDX_DIM = 64
IDX_SCALE = (IDX_HEADS * IDX_DIM) ** -0.5
TOPK_MAX = 256
N_BUCKETS = 32
MAX_DISTANCE = 128
N_EXPERTS = 32
TOP_K = 4
SWIGLU_LIMIT = 7.0
SWIGLU_ALPHA = 1.702
LN_EPS = 1e-5
QK_SCALE = HEAD_DIM ** -0.5

LANES = 128
SUBLANES = 8
HALF = LANES // 2
N_PAIRS = A_HEADS // 2
VMEM_LIMIT = 56 * 1024 * 1024

DSA_BLK = 256
SB_T = 256
MERGE_TM = 512
MOE_BLK = 512
COMB_TM = 256
SC_GATHER_BYTES = 128 * 1024
BISECT_CAP = 24
NEG = -1e30

COL_QA, COL_KA, COL_QI, COL_QB, COL_KB, COL_VB = (g * ATT_W for g in range(6))
COL_KK = 6 * ATT_W
COL_WI = COL_KK + LANES
ATT_COLS = COL_WI + LANES
PROJ_TN = ATT_COLS // 2

NT_DIMS = (((1,), (1,)), ((), ()))


def _params(sem, vmem=VMEM_LIMIT):
    return pltpu.CompilerParams(dimension_semantics=sem, vmem_limit_bytes=vmem)


def _proj_kernel(x_ref, w_ref, wt_ref, o_ref, ot_ref, xb_ref, *, tt):
    @pl.when(pl.program_id(1) == 0)
    def _row_tile_start():
        xb_ref[...] = x_ref[...].astype(BF16)
        for r in range(ot_ref.shape[0]):
            ot_ref[r] = lax.dot_general(wt_ref[...], xb_ref[r * tt:(r + 1) * tt, :], NT_DIMS,
                                        preferred_element_type=F32).astype(ot_ref.dtype)

    o_ref[...] = jnp.dot(xb_ref[...], w_ref[...], preferred_element_type=F32).astype(o_ref.dtype)


def _projection(x, w, w_t, tm, tn, tt):
    M, K = x.shape
    N = w.shape[1]
    Nt = w_t.shape[0]
    return pl.pallas_call(
        functools.partial(_proj_kernel, tt=tt),
        grid=(M // tm, N // tn),
        in_specs=[pl.BlockSpec((tm, K), lambda i, j: (i, 0)),
                  pl.BlockSpec((K, tn), lambda i, j: (0, j)),
                  pl.BlockSpec((Nt, K), lambda i, j: (0, 0))],
        out_specs=[pl.BlockSpec((tm, tn), lambda i, j: (i, j)),
                   pl.BlockSpec((tm // tt, Nt, tt), lambda i, j: (i, 0, 0))],
        out_shape=[jax.ShapeDtypeStruct((M, N), BF16),
                   jax.ShapeDtypeStruct((M // tt, Nt, tt), BF16)],
        scratch_shapes=[pltpu.VMEM((tm, K), BF16)],
        compiler_params=_params(("arbitrary", "arbitrary")),
    )(x, w, w_t)


def _t5_bucket_np(n):
    n = np.maximum(n, 0)
    max_exact = N_BUCKETS // 2
    nf = np.maximum(n, 1).astype(np.float32)
    large = max_exact + (np.log(nf / max_exact) / math.log(MAX_DISTANCE / max_exact)
                         * (N_BUCKETS - max_exact)).astype(np.int32)
    large = np.minimum(large, N_BUCKETS - 1)
    return np.where(n < max_exact, n, large).astype(np.int32)


def _dsa_n_off(blk):
    return 2 + -(-MAX_DISTANCE // blk)


def _dsa_bucket_tiles(blk):
    n_off = _dsa_n_off(blk)
    j = np.arange(blk)[None, :, None]
    i = np.arange(blk)[None, None, :]
    o = np.arange(n_off)[:, None, None]
    return _t5_bucket_np(i - j + blk * (n_off - 1 - o))


def _dsa_kernel(bucket_ref, relb_ref, q_ref, k_ref, vt_ref, qi_ref, kk_ref, wi_ref, o_ref,
                sc_ref, bias_ref, m_ref, l_ref, acc_ref, st_ref, mx_ref,
                *, blk, n_chunks, n_sel, n_off):
    b = pl.program_id(0)
    i = pl.program_id(1)
    q0 = i * blk
    nck = i + 1
    groups = blk // SUBLANES

    @pl.when(jnp.logical_and(b == 0, i == 0))
    def _build_bias():
        def head_body(h, _):
            for o in range(n_off):
                for rb in range(blk // LANES):
                    for cb in range(blk // LANES):
                        rs = slice(rb * LANES, (rb + 1) * LANES)
                        cs = slice(cb * LANES, (cb + 1) * LANES)
                        bk = bucket_ref[o, rs, cs]

                        def bucket_body(n, acc):
                            return jnp.where(bk == n, relb_ref[n, h], acc)

                        bias_ref[h, o, rs, cs] = lax.fori_loop(
                            0, N_BUCKETS, bucket_body, jnp.zeros((LANES, LANES), F32))
            return 0

        lax.fori_loop(0, A_HEADS, head_body, 0)

    lane = lax.broadcasted_iota(I32, (blk, LANES), 1)
    lo_half = lane < HALF
    krow = lax.broadcasted_iota(I32, (blk, blk), 0)
    qpos = q0 + lax.broadcasted_iota(I32, (1, blk), 1)

    def pair_split(ref, scale):
        out = []
        for p in range(N_PAIRS):
            v = ref[0, :, p * LANES:(p + 1) * LANES].astype(F32)
            if scale != 1.0:
                v = v * scale
            out.append(jnp.where(lo_half, v, 0.0).astype(BF16))
            out.append(jnp.where(lo_half, 0.0, v).astype(BF16))
        return out

    wi_t = wi_ref[0].astype(F32).T
    wrow = [wi_t[h:h + 1, :] * IDX_SCALE for h in range(IDX_HEADS)]
    qi_m = pair_split(qi_ref, 1.0)

    def score_chunk(c, _):
        c0 = pl.multiple_of(c * blk, blk)
        kk = kk_ref[0, pl.ds(c0, blk), :]
        acc = jnp.zeros((blk, blk), F32)
        for h in range(IDX_HEADS):
            s = lax.dot_general(kk, qi_m[h], NT_DIMS, preferred_element_type=F32)
            acc = acc + wrow[h] * jnp.maximum(s, 0.0)
        sc_ref[c] = jnp.where(c0 + krow <= qpos, acc, -jnp.inf)
        return 0

    lax.fori_loop(0, nck, score_chunk, 0)

    kt = jnp.minimum(qpos + 1, n_sel).astype(F32)

    def fold(fn, init):
        def body(c, acc):
            return fn(acc, sc_ref[c])
        return lax.fori_loop(0, nck, body, init)

    def part(x, op):
        return op(x.reshape(groups, SUBLANES, blk), axis=0)

    def fin(x, op):
        return op(x, axis=0, keepdims=True)

    zeros8 = jnp.zeros((SUBLANES, blk), F32)
    pinf8 = jnp.full((SUBLANES, blk), jnp.inf, F32)

    def count_ge(th):
        return fin(fold(lambda a, s: a + part(jnp.where(s >= th, 1.0, 0.0), jnp.sum), zeros8),
                   jnp.sum)

    mn, mx = fold(lambda a, s: (
        jnp.minimum(a[0], part(jnp.where(s == -jnp.inf, jnp.inf, s), jnp.min)),
        jnp.maximum(a[1], part(s, jnp.max))), (pinf8, -pinf8))
    rmin = fin(mn, jnp.min)
    rmax = fin(mx, jnp.max)

    def bis_cond(st):
        it, lo, hi, clo = st
        return jnp.logical_and(it < BISECT_CAP, jnp.max(jnp.abs(clo - kt)) > 0.0)

    def halve(lo, hi, clo):
        mid = 0.5 * lo + 0.5 * hi
        c = count_ge(mid)
        active = clo != kt
        up = jnp.logical_and(active, c >= kt)
        dn = jnp.logical_and(active, c < kt)
        return jnp.where(up, mid, lo), jnp.where(dn, mid, hi), jnp.where(up, c, clo)

    def bis_body(st):
        it, lo, hi, clo = st
        return (it + 2,) + halve(*halve(lo, hi, clo))

    _, lo, _, _ = lax.while_loop(
        bis_cond, bis_body, (jnp.int32(0), rmin, rmax + 1.0, (qpos + 1).astype(F32)))

    def stats(lo_):
        a_ = fin(fold(lambda a, s: jnp.minimum(a, part(jnp.where(s >= lo_, s, jnp.inf), jnp.min)),
                      pinf8), jnp.min)
        cg, ct, nx = fold(
            lambda a, s: (a[0] + part(jnp.where(s > a_, 1.0, 0.0), jnp.sum),
                               a[1] + part(jnp.where(s == a_, 1.0, 0.0), jnp.sum),
                               jnp.minimum(a[2], part(jnp.where(s > a_, s, jnp.inf), jnp.min))),
            (zeros8, zeros8, pinf8))
        return a_, fin(cg, jnp.sum), fin(ct, jnp.sum), fin(nx, jnp.min)

    def fin_cond(st):
        return st[0]

    def fin_body(st):
        _, lo_, _, _ = st
        a_, cgt_, nt_, nxt_ = stats(lo_)
        bad = cgt_ >= kt
        return (jnp.max(jnp.where(bad, 1.0, 0.0)) > 0.0, jnp.where(bad, nxt_, a_), cgt_, nt_)

    _, a, cgt, nties = lax.while_loop(fin_cond, fin_body, (jnp.bool_(True), lo, kt, kt))
    need = kt - cgt
    excess = jnp.max(jnp.where(nties > need, 1.0, 0.0)) > 0.0

    def mask_plain():
        def body(c, _):
            sc_ref[c] = jnp.where(sc_ref[c] >= a, 0.0, NEG)
            return 0
        lax.fori_loop(0, nck, body, 0)

    def mask_ties():
        upto = (krow >= lax.broadcasted_iota(I32, (blk, blk), 1)).astype(BF16)

        def body(c, seen):
            s = sc_ref[c]
            tie = s == a
            rank = jnp.dot(upto, jnp.where(tie, 1.0, 0.0).astype(BF16),
                           preferred_element_type=F32) + seen
            sel = jnp.logical_or(s > a, jnp.logical_and(tie, rank <= need))
            sc_ref[c] = jnp.where(sel, 0.0, NEG)
            return rank[blk - 1:blk, :]

        lax.fori_loop(0, nck, body, jnp.zeros((1, blk), F32))

    lax.cond(excess, mask_ties, mask_plain)

    m_ref[...] = jnp.full(m_ref.shape, NEG, F32)
    l_ref[...] = jnp.zeros(l_ref.shape, F32)
    acc_ref[...] = jnp.zeros(acc_ref.shape, F32)
    q_m = pair_split(q_ref, QK_SCALE)

    def stage_logits(c, slot):
        c = jnp.minimum(c, n_chunks - 1)
        c0 = pl.multiple_of(c * blk, blk)
        madd = sc_ref[c]
        o_idx = jnp.clip(c - i + (n_off - 1), 0, n_off - 1)
        for p in range(N_PAIRS):
            k2 = k_ref[0, pl.ds(c0, blk), p * LANES:(p + 1) * LANES]
            for hh in range(2):
                h = 2 * p + hh
                s = lax.dot_general(k2, q_m[h], NT_DIMS, preferred_element_type=F32)
                s = s + bias_ref[h, o_idx] + madd
                st_ref[slot, h] = s
                mx_ref[slot, h] = jnp.max(s, axis=0, keepdims=True)

    def stage_values(c, slot):
        for p in range(N_PAIRS):
            vt2 = vt_ref[0, c, p * LANES:(p + 1) * LANES, :]
            for hh in range(2):
                h = 2 * p + hh
                m_old = m_ref[h]
                m_new = jnp.maximum(m_old, mx_ref[slot, h])
                alpha = jnp.exp(m_old - m_new)
                pexp = jnp.exp(st_ref[slot, h] - m_new)
                l_ref[h] = alpha * l_ref[h] + jnp.sum(pexp, axis=0, keepdims=True)
                acc_ref[h] = alpha * acc_ref[h] + jnp.dot(vt2, pexp.astype(BF16),
                                                          preferred_element_type=F32)
                m_ref[h] = m_new

    stage_logits(0, 0)

    def att_pair(pp, _):
        c = 2 * pp
        stage_logits(c + 1, 1)
        stage_values(c, 0)
        stage_logits(c + 2, 0)
        stage_values(c + 1, 1)
        return 0

    lax.fori_loop(0, nck // 2, att_pair, 0)

    @pl.when(lax.rem(nck, 2) == 1)
    def _last_chunk():
        stage_values(nck - 1, 0)

    lo_rows = lax.broadcasted_iota(I32, (LANES, blk), 0) < HALF
    for p in range(N_PAIRS):
        oa = acc_ref[2 * p] / l_ref[2 * p]
        ob = acc_ref[2 * p + 1] / l_ref[2 * p + 1]
        o_ref[0, :, p * LANES:(p + 1) * LANES] = jnp.where(lo_rows, oa, ob).T.astype(o_ref.dtype)


def _dsa(proj3, vt4, rel_bias):
    B, S, _ = proj3.shape
    blk = DSA_BLK
    n_off = _dsa_n_off(blk)
    n_sel = min(TOPK_MAX, S // 4)
    bucket = jnp.asarray(_dsa_bucket_tiles(blk))
    n_chunks = S // blk
    assert S % blk == 0
    kern = functools.partial(_dsa_kernel, blk=blk, n_chunks=n_chunks, n_sel=n_sel, n_off=n_off)
    return pl.pallas_call(
        kern,
        grid=(B, S // blk),
        in_specs=[
            pl.BlockSpec((n_off, blk, blk), lambda b, i: (0, 0, 0)),
            pl.BlockSpec(memory_space=pltpu.SMEM),
            pl.BlockSpec((1, blk, ATT_W), lambda b, i: (b, i, COL_QA // ATT_W)),
            pl.BlockSpec((1, S, ATT_W), lambda b, i: (b, 0, COL_KA // ATT_W)),
            pl.BlockSpec((1, S // blk, ATT_W, blk), lambda b, i: (b, 0, 0, 0)),
            pl.BlockSpec((1, blk, ATT_W), lambda b, i: (b, i, COL_QI // ATT_W)),
            pl.BlockSpec((1, S, LANES), lambda b, i: (b, 0, COL_KK // LANES)),
            pl.BlockSpec((1, blk, LANES), lambda b, i: (b, i, COL_WI // LANES)),
        ],
        out_specs=pl.BlockSpec((1, blk, ATT_W), lambda b, i: (b, i, 0)),
        out_shape=jax.ShapeDtypeStruct((B, S, ATT_W), BF16),
        scratch_shapes=[
            pltpu.VMEM((S // blk, blk, blk), F32),
            pltpu.VMEM((A_HEADS, n_off, blk, blk), F32),
            pltpu.VMEM((A_HEADS, 1, blk), F32),
            pltpu.VMEM((A_HEADS, 1, blk), F32),
            pltpu.VMEM((A_HEADS, LANES, blk), F32),
            pltpu.VMEM((2, A_HEADS, blk, blk), F32),
            pltpu.VMEM((2, A_HEADS, 1, blk), F32),
        ],
        compiler_params=_params(("arbitrary", "arbitrary")),
    )(bucket, rel_bias, proj3, proj3, vt4, proj3, proj3, proj3)


def _sb_kernel(q_ref, k_ref, v_ref, o_ref, hl_ref, z_ref, *, t):
    i = pl.program_id(2)
    n = i + 1
    lane = lax.broadcasted_iota(I32, (t, LANES), 1)
    lo_half = lane < HALF
    q2 = q_ref[0].astype(F32) * QK_SCALE
    q_m = (jnp.where(lo_half, q2, 0.0).astype(BF16), jnp.where(lo_half, 0.0, q2).astype(BF16))
    r = lax.broadcasted_iota(I32, (t, t), 0)
    cidx = lax.broadcasted_iota(I32, (t, t), 1)
    neg_from = jnp.where(r >= cidx, -1.0, 0.0).astype(BF16)
    diff = cidx - r

    def stage_terms(step, slot, diagonal):
        c0 = pl.multiple_of(jnp.maximum(i - step, 0) * t, t)
        k2 = k_ref[0, pl.ds(c0, t), :]
        if diagonal:
            keep = diff < 0
        for hh in range(2):
            z = lax.dot_general(q_m[hh], k2, NT_DIMS, preferred_element_type=F32)
            sp = jnp.maximum(z, 0.0) + jnp.log(1.0 + jnp.exp(-jnp.abs(z)))
            if diagonal:
                sp = jnp.where(keep, sp, 0.0)
                z = jnp.where(keep, z, NEG)
            hi = sp.astype(BF16)
            hl_ref[slot, 2 * hh] = hi
            hl_ref[slot, 2 * hh + 1] = (sp - hi.astype(F32)).astype(BF16)
            z_ref[slot, hh] = z

    def stage_apply(step, slot, carry):
        c0 = pl.multiple_of((i - step) * t, t)
        v2 = v_ref[0, pl.ds(c0, t), :]
        cum4 = jnp.dot(hl_ref[slot].reshape(4 * t, t), neg_from, preferred_element_type=F32)
        out = []
        for hh in range(2):
            car, acc = carry[hh]
            cum = cum4[(2 * hh) * t:(2 * hh + 1) * t] + cum4[(2 * hh + 1) * t:(2 * hh + 2) * t]
            w = jnp.exp(z_ref[slot, hh] + cum + car)
            acc = acc + jnp.dot(w.astype(BF16), v2, preferred_element_type=F32)
            out.append((car + cum[:, 0:1], acc))
        return tuple(out)

    z1 = jnp.zeros((t, 1), F32)
    za = jnp.zeros((t, LANES), F32)
    stage_terms(0, 0, True)

    def pair_body(pp, carry):
        step = 2 * pp
        stage_terms(step + 1, 1, False)
        carry = stage_apply(step, 0, carry)
        stage_terms(step + 2, 0, False)
        return stage_apply(step + 1, 1, carry)

    carry = lax.fori_loop(0, n // 2, pair_body, ((z1, za), (z1, za)))
    (_, acc_a), (_, acc_b) = lax.cond(lax.rem(n, 2) == 1,
                                      lambda c: stage_apply(n - 1, 0, c), lambda c: c, carry)
    o_ref[0] = jnp.where(lo_half, acc_a, acc_b).astype(o_ref.dtype)


def _stick_breaking(proj3):
    B, S, _ = proj3.shape
    t = SB_T
    qb, kb, vb = COL_QB // LANES, COL_KB // LANES, COL_VB // LANES
    return pl.pallas_call(
        functools.partial(_sb_kernel, t=t),
        grid=(B, N_PAIRS, S // t),
        in_specs=[
            pl.BlockSpec((1, t, LANES), lambda b, p, i: (b, i, qb + p)),
            pl.BlockSpec((1, S, LANES), lambda b, p, i: (b, 0, kb + p)),
            pl.BlockSpec((1, S, LANES), lambda b, p, i: (b, 0, vb + p)),
        ],
        out_specs=pl.BlockSpec((1, t, LANES), lambda b, p, i: (b, i, p)),
        out_shape=jax.ShapeDtypeStruct((B, S, ATT_W), BF16),
        scratch_shapes=[
            pltpu.VMEM((2, 4, t, t), BF16),
            pltpu.VMEM((2, 2, t, t), F32),
        ],
        compiler_params=_params(("arbitrary", "arbitrary", "arbitrary")),
    )(proj3, proj3, proj3)


def _layer_norm(r, g, b):
    mu = jnp.mean(r, axis=-1, keepdims=True)
    d = r - mu
    var = jnp.mean(d * d, axis=-1, keepdims=True)
    return d * lax.rsqrt(var + LN_EPS) * g + b


def _split_bf16(v):
    hi = v.astype(BF16)
    return hi, (v - hi.astype(F32)).astype(BF16)


def _merge_kernel(x_ref, ya_ref, yb_ref, wg_ref, wa_ref, wb_ref, wo_ref, g_ref, b_ref,
                  wr_ref, br_ref, h_ref, hp_ref, e_ref, p_ref, *, alpha, d):
    pa = jnp.dot(ya_ref[...], wa_ref[...], preferred_element_type=F32)
    pb = jnp.dot(yb_ref[...], wb_ref[...], preferred_element_type=F32)
    gates = jnp.dot(x_ref[...].astype(BF16), wg_ref[...], preferred_element_type=F32)
    merged = jax.nn.sigmoid(gates[:, :d]) * pa + jax.nn.sigmoid(gates[:, d:]) * pb
    m = jnp.dot(merged.astype(BF16), wo_ref[...], preferred_element_type=F32)
    h = _layer_norm(alpha * x_ref[...] + m, g_ref[...], b_ref[...])
    h_ref[...] = h
    bits = pltpu.bitcast(h.astype(BF16).astype(F32), jnp.uint32)
    hp_ref[...] = bits[:, :d // 2] | (bits[:, d // 2:] >> 16)

    h_hi, h_lo = _split_bf16(h)
    w_hi, w_lo = _split_bf16(wr_ref[...])
    logit = (lax.dot_general(w_hi, h_hi, NT_DIMS, preferred_element_type=F32)
             + lax.dot_general(w_hi, h_lo, NT_DIMS, preferred_element_type=F32)
             + lax.dot_general(w_lo, h_hi, NT_DIMS, preferred_element_type=F32)) + br_ref[...]
    eid = lax.broadcasted_iota(I32, logit.shape, 0)
    vals, ids = [], []
    for _ in range(TOP_K):
        mx = jnp.max(logit, axis=0, keepdims=True)
        am = jnp.min(jnp.where(logit == mx, eid, N_EXPERTS), axis=0, keepdims=True)
        vals.append(mx)
        ids.append(am)
        logit = jnp.where(eid == am, -jnp.inf, logit)
    ex = [jnp.exp(v - vals[0]) for v in vals]
    den = ex[0] + ex[1] + ex[2] + ex[3]
    for k in range(TOP_K):
        e_ref[k:k + 1, :] = ids[k]
        p_ref[k:k + 1, :] = ex[k] / den


def _merge(x2, ya, yb, wg, wa, wb, wo, g, b, wr_t, br, alpha):
    T, D = x2.shape
    tm = MERGE_TM
    row = lambda i: (i, 0)
    fixed = lambda i: (0, 0)
    return pl.pallas_call(
        functools.partial(_merge_kernel, alpha=alpha, d=D),
        grid=(T // tm,),
        in_specs=[
            pl.BlockSpec((tm, D), row),
            pl.BlockSpec((tm, ATT_W), row),
            pl.BlockSpec((tm, ATT_W), row),
            pl.BlockSpec((D, 2 * D), fixed),
            pl.BlockSpec((ATT_W, D), fixed),
            pl.BlockSpec((ATT_W, D), fixed),
            pl.BlockSpec((D, D), fixed),
            pl.BlockSpec((1, D), fixed),
            pl.BlockSpec((1, D), fixed),
            pl.BlockSpec((N_EXPERTS, D), fixed),
            pl.BlockSpec((N_EXPERTS, 1), fixed),
        ],
        out_specs=[
            pl.BlockSpec((tm, D), row),
            pl.BlockSpec((tm, D // 2), row),
            pl.BlockSpec((TOP_K, tm), lambda i: (0, i)),
            pl.BlockSpec((TOP_K, tm), lambda i: (0, i)),
        ],
        out_shape=[
            jax.ShapeDtypeStruct((T, D), F32),
            jax.ShapeDtypeStruct((T, D // 2), jnp.uint32),
            jax.ShapeDtypeStruct((TOP_K, T), I32),
            jax.ShapeDtypeStruct((TOP_K, T), F32),
        ],
        compiler_params=_params(("arbitrary",)),
    )(x2, ya, yb, wg, wa, wb, wo, g, b, wr_t, br)


def _sc_gather_rows(table, idx):
    n = idx.shape[0]
    d = table.shape[1]
    info = plsc.get_sparse_core_info()
    n_cores, n_sub = info.num_cores, info.num_subcores
    per_w = n // (n_cores * n_sub)
    c = SC_GATHER_BYTES // (d * table.dtype.itemsize)
    n_g = per_w // c
    assert n == per_w * n_cores * n_sub and per_w == n_g * c and n_g % 2 == 0 and n_g >= 2
    mesh = plsc.VectorSubcoreMesh(core_axis_name="c", subcore_axis_name="s")

    @functools.partial(
        pl.kernel, mesh=mesh, out_type=jax.ShapeDtypeStruct((n, d), table.dtype),
        scratch_types=[pltpu.VMEM((per_w,), I32), pltpu.VMEM((2, c, d), table.dtype),
                       pltpu.SemaphoreType.DMA((2,)), pltpu.SemaphoreType.DMA((2,))])
    def gather_kernel(table_hbm, idx_hbm, out_hbm, idx_v, rows_v, gsem, wsem):
        base = (lax.axis_index("s") * n_cores + lax.axis_index("c")) * per_w
        pltpu.sync_copy(idx_hbm.at[pl.ds(base, per_w)], idx_v)

        def gather(g, b):
            return pltpu.make_async_copy(table_hbm.at[idx_v.at[pl.ds(g * c, c)]], rows_v.at[b],
                                         gsem.at[b])

        def write(g, b):
            return pltpu.make_async_copy(rows_v.at[b], out_hbm.at[pl.ds(base + g * c, c)], wsem.at[b])

        gather(0, 0).start()

        @pl.loop(0, n_g, step=2)
        def _ring(g0):
            for b in range(2):
                g = g0 + b

                @pl.when(g + 1 < n_g)
                def _next():
                    @pl.when(g >= 1)
                    def _buffer_free():
                        write(g - 1, 1 - b).wait()
                    gather(g + 1, 1 - b).start()

                gather(g, b).wait()
                write(g, b).start()

        write(n_g - 2, 0).wait()
        write(n_g - 1, 1).wait()

    return gather_kernel(table, idx)


def _moe_kernel(blk_e_ref, nused_ref, x_ref, wgu_ref, bgu_ref, wdn_ref, bdn_ref, o_ref,
                wgu_s, wdn_s, *, f):
    i = pl.program_id(0)
    nused = nused_ref[0]

    @pl.when(i < nused)
    def _compute():
        changed = jnp.logical_or(i == 0, blk_e_ref[i] != blk_e_ref[jnp.maximum(i - 1, 0)])

        @pl.when(changed)
        def _cast_weights():
            wgu_s[...] = wgu_ref[0].astype(BF16)
            wdn_s[...] = wdn_ref[0].astype(BF16)

        xw = x_ref[...]
        x = jnp.concatenate([pltpu.bitcast(xw & jnp.uint32(0xFFFF0000), F32),
                             pltpu.bitcast(xw << 16, F32)], axis=1).astype(BF16)
        hgu = jnp.dot(x, wgu_s[...], preferred_element_type=F32) + bgu_ref[0]
        a = jnp.minimum(hgu[:, :f], SWIGLU_LIMIT)
        u = jnp.clip(hgu[:, f:], -SWIGLU_LIMIT, SWIGLU_LIMIT)
        glu = a * jax.nn.sigmoid(a * SWIGLU_ALPHA)
        o_ref[...] = jnp.dot(((u + 1.0) * glu).astype(BF16), wdn_s[...],
                             preferred_element_type=F32) + bdn_ref[0]

    @pl.when(i >= nused)
    def _unused_block():
        o_ref[...] = jnp.zeros(o_ref.shape, o_ref.dtype)


def _moe_ffn(xs, blk_e, nused, w_gu, b_gu, w_dn, b_dn):
    P = xs.shape[0]
    E, D, F2 = w_gu.shape
    f = F2 // 2
    blk = MOE_BLK
    nb = P // blk
    used_block = lambda i, be, nu: (jnp.minimum(i, nu[0] - 1), 0)
    grid_spec = pltpu.PrefetchScalarGridSpec(
        num_scalar_prefetch=2,
        grid=(nb,),
        in_specs=[
            pl.BlockSpec((blk, D // 2), used_block),
            pl.BlockSpec((1, D, F2), lambda i, be, nu: (be[i], 0, 0)),
            pl.BlockSpec((1, 1, F2), lambda i, be, nu: (be[i], 0, 0)),
            pl.BlockSpec((1, f, D), lambda i, be, nu: (be[i], 0, 0)),
            pl.BlockSpec((1, 1, D), lambda i, be, nu: (be[i], 0, 0)),
        ],
        out_specs=pl.BlockSpec((blk, D), lambda i, be, nu: (i, 0)),
        scratch_shapes=[
            pltpu.VMEM((D, F2), BF16),
            pltpu.VMEM((f, D), BF16),
        ],
    )
    return pl.pallas_call(
        functools.partial(_moe_kernel, f=f),
        grid_spec=grid_spec,
        out_shape=jax.ShapeDtypeStruct((P, D), F32),
        compiler_params=_params(("arbitrary",)),
    )(blk_e, nused, xs, w_gu, b_gu.reshape(E, 1, F2), w_dn, b_dn.reshape(E, 1, D))


def _comb_kernel(h_ref, p_ref, y_ref, g_ref, b_ref, o_ref, *, alpha):
    gate = p_ref[...]
    fsum = ((y_ref[0] * gate[:, 0:1] + y_ref[1] * gate[:, 1:2])
            + (y_ref[2] * gate[:, 2:3] + y_ref[3] * gate[:, 3:4]))
    o_ref[...] = _layer_norm(alpha * h_ref[...] + fsum, g_ref[...], b_ref[...])


def _combine(h, y4, top_p, g, b, alpha):
    T, D = h.shape
    tm = COMB_TM
    return pl.pallas_call(
        functools.partial(_comb_kernel, alpha=alpha),
        grid=(T // tm,),
        in_specs=[
            pl.BlockSpec((tm, D), lambda i: (i, 0)),
            pl.BlockSpec((tm, TOP_K), lambda i: (i, 0)),
            pl.BlockSpec((TOP_K, tm, D), lambda i: (0, i, 0)),
            pl.BlockSpec((1, D), lambda i: (0, 0)),
            pl.BlockSpec((1, D), lambda i: (0, 0)),
        ],
        out_specs=pl.BlockSpec((tm, D), lambda i: (i, 0)),
        out_shape=jax.ShapeDtypeStruct((T, D), F32),
        compiler_params=_params(("arbitrary",)),
    )(h, top_p.T, y4, g, b)


def _route(top_e, blk):
    K, T = top_e.shape
    N = K * T
    flat_e = top_e.reshape(N)
    experts = jnp.arange(N_EXPERTS, dtype=I32)
    order = jnp.argsort(flat_e, stable=True).astype(I32)
    inv = jnp.argsort(order).astype(I32)
    onehot = flat_e[:, None] == experts[None, :]
    counts = jnp.sum(onehot, axis=0, dtype=I32)
    padded = (counts + blk - 1) // blk * blk
    pends = jnp.cumsum(padded)
    offs = jnp.cumsum(counts) - counts
    shift = (pends - padded) - offs
    pos = inv + jnp.sum(jnp.where(onehot, shift[None, :], 0), axis=1, dtype=I32)
    P = N + N_EXPERTS * blk
    nb = P // blk
    blk_start = jnp.arange(nb, dtype=I32) * blk
    blk_e = jnp.minimum(jnp.sum(pends[None, :] <= blk_start[:, None], axis=1, dtype=I32), N_EXPERTS - 1)
    j = (blk_start - shift[blk_e])[:, None] + jnp.arange(blk, dtype=I32)[None, :]
    valid = j < (offs + counts)[blk_e][:, None]
    src = order[jnp.clip(j, 0, N - 1)]
    row_tok = jnp.where(valid, src % T, j % T).reshape(P)
    nused = (pends[-1:] // blk).astype(I32)
    return blk_e, nused, row_tok, pos.reshape(K, T)


def _projection_weights(w_in_l):
    sizes = (ATT_W, ATT_W, ATT_W, IDX_HEADS * IDX_DIM, IDX_DIM, IDX_HEADS, ATT_W, ATT_W, ATT_W)
    offs = np.concatenate([[0], np.cumsum(sizes)])
    qa, ka, va, qi, ki, wi, qb, kb, vb = (w_in_l[:, offs[n]:offs[n + 1]] for n in range(9))
    pad_wi = jnp.zeros((w_in_l.shape[0], LANES - IDX_HEADS), w_in_l.dtype)
    w_att = jnp.concatenate([qa, ka, qi, qb, kb, vb, ki, ki, wi, pad_wi], axis=1).astype(BF16)
    w_va_t = va.T.astype(BF16)
    w_gate = w_in_l[:, offs[9]:].astype(BF16)
    return w_att, w_va_t, w_gate


def kernel(x, w_in, w_branch_a, w_branch_b, w_out, rel_bias, ln1_g, ln1_b, w_router, b_router,
           w_gate_up, b_gate_up, w_down, b_down, ln2_g, ln2_b):
    B, S, D = x.shape
    depth = w_in.shape[0]
    alpha = (2 * depth) ** 0.25
    T = B * S
    h = x.reshape(T, D)
    for l in range(depth):
        w_att, w_va_t, w_gate = _projection_weights(w_in[l])
        proj, vt = _projection(h, w_att, w_va_t, min(T, 1024), PROJ_TN, DSA_BLK)
        proj = proj.reshape(B, S, ATT_COLS)
        vt = vt.reshape(B, S // DSA_BLK, ATT_W, DSA_BLK)
        ya = _dsa(proj, vt, rel_bias).reshape(T, ATT_W)
        yb = _stick_breaking(proj).reshape(T, ATT_W)
        h1, h1_packed, top_e, top_p = _merge(
            h, ya, yb, w_gate, w_branch_a[l].astype(BF16), w_branch_b[l].astype(BF16),
            w_out[l].astype(BF16), ln1_g[l].reshape(1, D), ln1_b[l].reshape(1, D),
            w_router[l].T, b_router[l].reshape(N_EXPERTS, 1), alpha)
        blk_e, nused, row_tok, pos = _route(top_e, MOE_BLK)
        xs = _sc_gather_rows(h1_packed, row_tok)
        ys = _moe_ffn(xs, blk_e, nused, w_gate_up[l], b_gate_up[l], w_down[l], b_down[l])
        y4 = _sc_gather_rows(ys, pos.reshape(TOP_K * T)).reshape(TOP_K, T, D)
        h = _combine(h1, y4, top_p, ln2_g[l].reshape(1, D), ln2_b[l].reshape(1, D), alpha)
    return h.reshape(B, S, D)
```

```python
import functools
import math

import numpy as np
import jax
import jax.numpy as jnp
from jax import lax
from jax.experimental import pallas as pl
from jax.experimental.pallas import tpu as pltpu
from jax.experimental.pallas import tpu_sc as plsc

F32 = jnp.float32
BF16 = jnp.bfloat16
I32 = jnp.int32

A_HEADS = 8
HEAD_DIM = 64
ATT_W = A_HEADS * HEAD_DIM
IDX_HEADS = 8
IDX_DIM = 64
IDX_SCALE = (IDX_HEADS * IDX_DIM) ** -0.5
TOPK_MAX = 256
N_BUCKETS = 32
MAX_DISTANCE = 128
N_EXPERTS = 32
TOP_K = 4
SWIGLU_LIMIT = 7.0
SWIGLU_ALPHA = 1.702
LN_EPS = 1e-5
QK_SCALE = HEAD_DIM ** -0.5

LANES = 128
SUBLANES = 8
HALF = LANES // 2
N_PAIRS = A_HEADS // 2
VMEM_LIMIT = 56 * 1024 * 1024

DSA_BLK = 256
SB_T = 256
MERGE_TM = 512
MOE_BLK = 512
COMB_TM = 256
SC_GATHER_BYTES = 128 * 1024
REDUCE_CHAINS = 4
BISECT_CAP = 24
NEG = -1e30

COL_QA, COL_KA, COL_QI, COL_QB, COL_KB, COL_VB = (g * ATT_W for g in range(6))
COL_KK = 6 * ATT_W
COL_WI = COL_KK + LANES
ATT_COLS = COL_WI + LANES
PROJ_TN = ATT_COLS // 2

NT_DIMS = (((1,), (1,)), ((), ()))


def _params(sem, vmem=VMEM_LIMIT):
    return pltpu.CompilerParams(dimension_semantics=sem, vmem_limit_bytes=vmem)


def _proj_kernel(x_ref, w_ref, wt_ref, o_ref, ot_ref, xb_ref, *, tt):
    @pl.when(pl.program_id(1) == 0)
    def _row_tile_start():
        xb_ref[...] = x_ref[...].astype(BF16)
        for r in range(ot_ref.shape[0]):
            ot_ref[r] = lax.dot_general(wt_ref[...], xb_ref[r * tt:(r + 1) * tt, :], NT_DIMS,
                                        preferred_element_type=F32).astype(ot_ref.dtype)

    o_ref[...] = jnp.dot(xb_ref[...], w_ref[...], preferred_element_type=F32).astype(o_ref.dtype)


def _projection(x, w, w_t, tm, tn, tt):
    M, K = x.shape
    N = w.shape[1]
    Nt = w_t.shape[0]
    return pl.pallas_call(
        functools.partial(_proj_kernel, tt=tt),
        grid=(M // tm, N // tn),
        in_specs=[pl.BlockSpec((tm, K), lambda i, j: (i, 0)),
                  pl.BlockSpec((K, tn), lambda i, j: (0, j)),
                  pl.BlockSpec((Nt, K), lambda i, j: (0, 0))],
        out_specs=[pl.BlockSpec((tm, tn), lambda i, j: (i, j)),
                   pl.BlockSpec((tm // tt, Nt, tt), lambda i, j: (i, 0, 0))],
        out_shape=[jax.ShapeDtypeStruct((M, N), BF16),
                   jax.ShapeDtypeStruct((M // tt, Nt, tt), BF16)],
        scratch_shapes=[pltpu.VMEM((tm, K), BF16)],
        compiler_params=_params(("arbitrary", "arbitrary")),
    )(x, w, w_t)


def _t5_bucket_np(n):
    n = np.maximum(n, 0)
    max_exact = N_BUCKETS // 2
    nf = np.maximum(n, 1).astype(np.float32)
    large = max_exact + (np.log(nf / max_exact) / math.log(MAX_DISTANCE / max_exact)
                         * (N_BUCKETS - max_exact)).astype(np.int32)
    large = np.minimum(large, N_BUCKETS - 1)
    return np.where(n < max_exact, n, large).astype(np.int32)


def _dsa_n_off(blk):
    return 2 + -(-MAX_DISTANCE // blk)


def _dsa_bucket_tiles(blk):
    n_off = _dsa_n_off(blk)
    j = np.arange(blk)[None, :, None]
    i = np.arange(blk)[None, None, :]
    o = np.arange(n_off)[:, None, None]
    return _t5_bucket_np(i - j + blk * (n_off - 1 - o))


def _dsa_kernel(bucket_ref, relb_ref, q_ref, k_ref, vt_ref, qi_ref, kk_ref, wi_ref, o_ref,
                sc_ref, bias_ref, m_ref, l_ref, acc_ref, st_ref, mx_ref,
                *, blk, n_chunks, n_sel, n_off):
    b = pl.program_id(0)
    i = pl.program_id(1)
    q0 = i * blk
    nck = i + 1
    groups = blk // SUBLANES

    @pl.when(jnp.logical_and(b == 0, i == 0))
    def _build_bias():
        def head_body(h, _):
            for o in range(n_off):
                for rb in range(blk // LANES):
                    for cb in range(blk // LANES):
                        rs = slice(rb * LANES, (rb + 1) * LANES)
                        cs = slice(cb * LANES, (cb + 1) * LANES)
                        bk = bucket_ref[o, rs, cs]

                        def bucket_body(n, acc):
                            return jnp.where(bk == n, relb_ref[n, h], acc)

                        bias_ref[h, o, rs, cs] = lax.fori_loop(
                            0, N_BUCKETS, bucket_body, jnp.zeros((LANES, LANES), F32))
            return 0

        lax.fori_loop(0, A_HEADS, head_body, 0)

    lane = lax.broadcasted_iota(I32, (blk, LANES), 1)
    lo_half = lane < HALF
    krow = lax.broadcasted_iota(I32, (blk, blk), 0)
    qpos = q0 + lax.broadcasted_iota(I32, (1, blk), 1)

    def pair_split(ref, scale):
        out = []
        for p in range(N_PAIRS):
            v = ref[0, :, p * LANES:(p + 1) * LANES].astype(F32)
            if scale != 1.0:
                v = v * scale
            out.append(jnp.where(lo_half, v, 0.0).astype(BF16))
            out.append(jnp.where(lo_half, 0.0, v).astype(BF16))
        return out

    wi_t = wi_ref[0].astype(F32).T
    wrow = [wi_t[h:h + 1, :] * IDX_SCALE for h in range(IDX_HEADS)]
    qi_m = pair_split(qi_ref, 1.0)

    def score_chunk(c, _):
        c0 = pl.multiple_of(c * blk, blk)
        kk = kk_ref[0, pl.ds(c0, blk), :]
        acc = jnp.zeros((blk, blk), F32)
        for h in range(IDX_HEADS):
            s = lax.dot_general(kk, qi_m[h], NT_DIMS, preferred_element_type=F32)
            acc = acc + wrow[h] * jnp.maximum(s, 0.0)
        sc_ref[c] = jnp.where(c0 + krow <= qpos, acc, -jnp.inf)
        return 0

    lax.fori_loop(0, nck, score_chunk, 0)

    kt = jnp.minimum(qpos + 1, n_sel).astype(F32)

    def fold(fn, init):
        def body(c, acc):
            return fn(acc, sc_ref[c])
        return lax.fori_loop(0, nck, body, init)

    def part(x, op):
        y = op(x.reshape(REDUCE_CHAINS, groups // REDUCE_CHAINS, SUBLANES, blk), axis=1)
        return op(y, axis=0)

    def fin(x, op):
        return op(x, axis=0, keepdims=True)

    zeros8 = jnp.zeros((SUBLANES, blk), F32)
    pinf8 = jnp.full((SUBLANES, blk), jnp.inf, F32)

    def count_ge(th):
        return fin(fold(lambda a, s: a + part(jnp.where(s >= th, 1.0, 0.0), jnp.sum), zeros8),
                   jnp.sum)

    mn, mx = fold(lambda a, s: (
        jnp.minimum(a[0], part(jnp.where(s == -jnp.inf, jnp.inf, s), jnp.min)),
        jnp.maximum(a[1], part(s, jnp.max))), (pinf8, -pinf8))
    rmin = fin(mn, jnp.min)
    rmax = fin(mx, jnp.max)

    def bis_cond(st):
        it, lo, hi, clo = st
        return jnp.logical_and(it < BISECT_CAP, jnp.max(jnp.abs(clo - kt)) > 0.0)

    def halve(lo, hi, clo):
        mid = 0.5 * lo + 0.5 * hi
        c = count_ge(mid)
        active = clo != kt
        up = jnp.logical_and(active, c >= kt)
        dn = jnp.logical_and(active, c < kt)
        return jnp.where(up, mid, lo), jnp.where(dn, mid, hi), jnp.where(up, c, clo)

    def bis_body(st):
        it, lo, hi, clo = st
        return (it + 2,) + halve(*halve(lo, hi, clo))

    _, lo, _, _ = lax.while_loop(
        bis_cond, bis_body, (jnp.int32(0), rmin, rmax + 1.0, (qpos + 1).astype(F32)))

    def stats(lo_):
        a_ = fin(fold(lambda a, s: jnp.minimum(a, part(jnp.where(s >= lo_, s, jnp.inf), jnp.min)),
                      pinf8), jnp.min)
        cg, ct, nx = fold(
            lambda a, s: (a[0] + part(jnp.where(s > a_, 1.0, 0.0), jnp.sum),
                               a[1] + part(jnp.where(s == a_, 1.0, 0.0), jnp.sum),
                               jnp.minimum(a[2], part(jnp.where(s > a_, s, jnp.inf), jnp.min))),
            (zeros8, zeros8, pinf8))
        return a_, fin(cg, jnp.sum), fin(ct, jnp.sum), fin(nx, jnp.min)

    def fin_cond(st):
        return st[0]

    def fin_body(st):
        _, lo_, _, _ = st
        a_, cgt_, nt_, nxt_ = stats(lo_)
        bad = cgt_ >= kt
        return (jnp.max(jnp.where(bad, 1.0, 0.0)) > 0.0, jnp.where(bad, nxt_, a_), cgt_, nt_)

    _, a, cgt, nties = lax.while_loop(fin_cond, fin_body, (jnp.bool_(True), lo, kt, kt))
    need = kt - cgt
    excess = jnp.max(jnp.where(nties > need, 1.0, 0.0)) > 0.0

    def mask_plain():
        def body(c, _):
            sc_ref[c] = jnp.where(sc_ref[c] >= a, 0.0, NEG)
            return 0
        lax.fori_loop(0, nck, body, 0)

    def mask_ties():
        upto = (krow >= lax.broadcasted_iota(I32, (blk, blk), 1)).astype(BF16)

        def body(c, seen):
            s = sc_ref[c]
            tie = s == a
            rank = jnp.dot(upto, jnp.where(tie, 1.0, 0.0).astype(BF16),
                           preferred_element_type=F32) + seen
            sel = jnp.logical_or(s > a, jnp.logical_and(tie, rank <= need))
            sc_ref[c] = jnp.where(sel, 0.0, NEG)
            return rank[blk - 1:blk, :]

        lax.fori_loop(0, nck, body, jnp.zeros((1, blk), F32))

    lax.cond(excess, mask_ties, mask_plain)

    m_ref[...] = jnp.full(m_ref.shape, NEG, F32)
    l_ref[...] = jnp.zeros(l_ref.shape, F32)
    acc_ref[...] = jnp.zeros(acc_ref.shape, F32)
    q_m = pair_split(q_ref, QK_SCALE)

    def stage_logits(c, slot):
        c = jnp.minimum(c, n_chunks - 1)
        c0 = pl.multiple_of(c * blk, blk)
        madd = sc_ref[c]
        o_idx = jnp.clip(c - i + (n_off - 1), 0, n_off - 1)
        for p in range(N_PAIRS):
            k2 = k_ref[0, pl.ds(c0, blk), p * LANES:(p + 1) * LANES]
            for hh in range(2):
                h = 2 * p + hh
                s = lax.dot_general(k2, q_m[h], NT_DIMS, preferred_element_type=F32)
                s = s + bias_ref[h, o_idx] + madd
                st_ref[slot, h] = s
                mx_ref[slot, h] = jnp.max(s, axis=0, keepdims=True)

    def stage_values(c, slot):
        for p in range(N_PAIRS):
            vt2 = vt_ref[0, c, p * LANES:(p + 1) * LANES, :]
            for hh in range(2):
                h = 2 * p + hh
                m_old = m_ref[h]
                m_new = jnp.maximum(m_old, mx_ref[slot, h])
                alpha = jnp.exp(m_old - m_new)
                pexp = jnp.exp(st_ref[slot, h] - m_new)
                l_ref[h] = alpha * l_ref[h] + jnp.sum(pexp, axis=0, keepdims=True)
                acc_ref[h] = alpha * acc_ref[h] + jnp.dot(vt2, pexp.astype(BF16),
                                                          preferred_element_type=F32)
                m_ref[h] = m_new

    stage_logits(0, 0)

    def att_pair(pp, _):
        c = 2 * pp
        stage_logits(c + 1, 1)
        stage_values(c, 0)
        stage_logits(c + 2, 0)
        stage_values(c + 1, 1)
        return 0

    lax.fori_loop(0, nck // 2, att_pair, 0)

    @pl.when(lax.rem(nck, 2) == 1)
    def _last_chunk():
        stage_values(nck - 1, 0)

    lo_rows = lax.broadcasted_iota(I32, (LANES, blk), 0) < HALF
    for p in range(N_PAIRS):
        oa = acc_ref[2 * p] / l_ref[2 * p]
        ob = acc_ref[2 * p + 1] / l_ref[2 * p + 1]
        o_ref[0, :, p * LANES:(p + 1) * LANES] = jnp.where(lo_rows, oa, ob).T.astype(o_ref.dtype)


def _dsa(proj3, vt4, rel_bias):
    B, S, _ = proj3.shape
    blk = DSA_BLK
    n_off = _dsa_n_off(blk)
    n_sel = min(TOPK_MAX, S // 4)
    bucket = jnp.asarray(_dsa_bucket_tiles(blk))
    n_chunks = S // blk
    assert S % blk == 0
    kern = functools.partial(_dsa_kernel, blk=blk, n_chunks=n_chunks, n_sel=n_sel, n_off=n_off)
    return pl.pallas_call(
        kern,
        grid=(B, S // blk),
        in_specs=[
            pl.BlockSpec((n_off, blk, blk), lambda b, i: (0, 0, 0)),
            pl.BlockSpec(memory_space=pltpu.SMEM),
            pl.BlockSpec((1, blk, ATT_W), lambda b, i: (b, i, COL_QA // ATT_W)),
            pl.BlockSpec((1, S, ATT_W), lambda b, i: (b, 0, COL_KA // ATT_W)),
            pl.BlockSpec((1, S // blk, ATT_W, blk), lambda b, i: (b, 0, 0, 0)),
            pl.BlockSpec((1, blk, ATT_W), lambda b, i: (b, i, COL_QI // ATT_W)),
            pl.BlockSpec((1, S, LANES), lambda b, i: (b, 0, COL_KK // LANES)),
            pl.BlockSpec((1, blk, LANES), lambda b, i: (b, i, COL_WI // LANES)),
        ],
        out_specs=pl.BlockSpec((1, blk, ATT_W), lambda b, i: (b, i, 0)),
        out_shape=jax.ShapeDtypeStruct((B, S, ATT_W), BF16),
        scratch_shapes=[
            pltpu.VMEM((S // blk, blk, blk), F32),
            pltpu.VMEM((A_HEADS, n_off, blk, blk), F32),
            pltpu.VMEM((A_HEADS, 1, blk), F32),
            pltpu.VMEM((A_HEADS, 1, blk), F32),
            pltpu.VMEM((A_HEADS, LANES, blk), F32),
            pltpu.VMEM((2, A_HEADS, blk, blk), F32),
            pltpu.VMEM((2, A_HEADS, 1, blk), F32),
        ],
        compiler_params=_params(("arbitrary", "arbitrary")),
    )(bucket, rel_bias, proj3, proj3, vt4, proj3, proj3, proj3)


def _sb_kernel(q_ref, k_ref, v_ref, o_ref, hl_ref, z_ref, *, t):
    i = pl.program_id(2)
    n = i + 1
    lane = lax.broadcasted_iota(I32, (t, LANES), 1)
    lo_half = lane < HALF
    q2 = q_ref[0].astype(F32) * QK_SCALE
    q_m = (jnp.where(lo_half, q2, 0.0).astype(BF16), jnp.where(lo_half, 0.0, q2).astype(BF16))
    r = lax.broadcasted_iota(I32, (t, t), 0)
    cidx = lax.broadcasted_iota(I32, (t, t), 1)
    neg_from = jnp.where(r >= cidx, -1.0, 0.0).astype(BF16)
    diff = cidx - r

    def stage_terms(step, slot, diagonal):
        c0 = pl.multiple_of(jnp.maximum(i - step, 0) * t, t)
        k2 = k_ref[0, pl.ds(c0, t), :]
        if diagonal:
            keep = diff < 0
        for hh in range(2):
            z = lax.dot_general(q_m[hh], k2, NT_DIMS, preferred_element_type=F32)
            sp = jnp.maximum(z, 0.0) + jnp.log(1.0 + jnp.exp(-jnp.abs(z)))
            if diagonal:
                sp = jnp.where(keep, sp, 0.0)
                z = jnp.where(keep, z, NEG)
            hi = sp.astype(BF16)
            hl_ref[slot, 2 * hh] = hi
            hl_ref[slot, 2 * hh + 1] = (sp - hi.astype(F32)).astype(BF16)
            z_ref[slot, hh] = z

    def stage_apply(step, slot, carry):
        c0 = pl.multiple_of((i - step) * t, t)
        v2 = v_ref[0, pl.ds(c0, t), :]
        cum4 = jnp.dot(hl_ref[slot].reshape(4 * t, t), neg_from, preferred_element_type=F32)
        out = []
        for hh in range(2):
            car, acc = carry[hh]
            cum = cum4[(2 * hh) * t:(2 * hh + 1) * t] + cum4[(2 * hh + 1) * t:(2 * hh + 2) * t]
            w = jnp.exp(z_ref[slot, hh] + cum + car)
            acc = acc + jnp.dot(w.astype(BF16), v2, preferred_element_type=F32)
            out.append((car + cum[:, 0:1], acc))
        return tuple(out)

    z1 = jnp.zeros((t, 1), F32)
    za = jnp.zeros((t, LANES), F32)
    stage_terms(0, 0, True)

    def pair_body(pp, carry):
        step = 2 * pp
        stage_terms(step + 1, 1, False)
        carry = stage_apply(step, 0, carry)
        stage_terms(step + 2, 0, False)
        return stage_apply(step + 1, 1, carry)

    carry = lax.fori_loop(0, n // 2, pair_body, ((z1, za), (z1, za)))
    (_, acc_a), (_, acc_b) = lax.cond(lax.rem(n, 2) == 1,
                                      lambda c: stage_apply(n - 1, 0, c), lambda c: c, carry)
    o_ref[0] = jnp.where(lo_half, acc_a, acc_b).astype(o_ref.dtype)


def _stick_breaking(proj3):
    B, S, _ = proj3.shape
    t = SB_T
    qb, kb, vb = COL_QB // LANES, COL_KB // LANES, COL_VB // LANES
    return pl.pallas_call(
        functools.partial(_sb_kernel, t=t),
        grid=(B, N_PAIRS, S // t),
        in_specs=[
            pl.BlockSpec((1, t, LANES), lambda b, p, i: (b, i, qb + p)),
            pl.BlockSpec((1, S, LANES), lambda b, p, i: (b, 0, kb + p)),
            pl.BlockSpec((1, S, LANES), lambda b, p, i: (b, 0, vb + p)),
        ],
        out_specs=pl.BlockSpec((1, t, LANES), lambda b, p, i: (b, i, p)),
        out_shape=jax.ShapeDtypeStruct((B, S, ATT_W), BF16),
        scratch_shapes=[
            pltpu.VMEM((2, 4, t, t), BF16),
            pltpu.VMEM((2, 2, t, t), F32),
        ],
        compiler_params=_params(("arbitrary", "arbitrary", "arbitrary")),
    )(proj3, proj3, proj3)


def _layer_norm(r, g, b):
    mu = jnp.mean(r, axis=-1, keepdims=True)
    d = r - mu
    var = jnp.mean(d * d, axis=-1, keepdims=True)
    return d * lax.rsqrt(var + LN_EPS) * g + b


def _split_bf16(v):
    hi = v.astype(BF16)
    return hi, (v - hi.astype(F32)).astype(BF16)


def _merge_kernel(x_ref, ya_ref, yb_ref, wg_ref, wa_ref, wb_ref, wo_ref, g_ref, b_ref,
                  wr_ref, br_ref, h_ref, hp_ref, e_ref, p_ref, *, alpha, d):
    pa = jnp.dot(ya_ref[...], wa_ref[...], preferred_element_type=F32)
    pb = jnp.dot(yb_ref[...], wb_ref[...], preferred_element_type=F32)
    gates = jnp.dot(x_ref[...].astype(BF16), wg_ref[...], preferred_element_type=F32)
    merged = jax.nn.sigmoid(gates[:, :d]) * pa + jax.nn.sigmoid(gates[:, d:]) * pb
    m = jnp.dot(merged.astype(BF16), wo_ref[...], preferred_element_type=F32)
    h = _layer_norm(alpha * x_ref[...] + m, g_ref[...], b_ref[...])
    h_ref[...] = h
    bits = pltpu.bitcast(h.astype(BF16).astype(F32), jnp.uint32)
    hp_ref[...] = bits[:, :d // 2] | (bits[:, d // 2:] >> 16)

    h_hi, h_lo = _split_bf16(h)
    w_hi, w_lo = _split_bf16(wr_ref[...])
    logit = (lax.dot_general(w_hi, h_hi, NT_DIMS, preferred_element_type=F32)
             + lax.dot_general(w_hi, h_lo, NT_DIMS, preferred_element_type=F32)
             + lax.dot_general(w_lo, h_hi, NT_DIMS, preferred_element_type=F32)) + br_ref[...]
    eid = lax.broadcasted_iota(I32, logit.shape, 0)
    vals, ids = [], []
    for _ in range(TOP_K):
        mx = jnp.max(logit, axis=0, keepdims=True)
        am = jnp.min(jnp.where(logit == mx, eid, N_EXPERTS), axis=0, keepdims=True)
        vals.append(mx)
        ids.append(am)
        logit = jnp.where(eid == am, -jnp.inf, logit)
    ex = [jnp.exp(v - vals[0]) for v in vals]
    den = ex[0] + ex[1] + ex[2] + ex[3]
    for k in range(TOP_K):
        e_ref[k:k + 1, :] = ids[k]
        p_ref[k:k + 1, :] = ex[k] / den


def _merge(x2, ya, yb, wg, wa, wb, wo, g, b, wr_t, br, alpha):
    T, D = x2.shape
    tm = MERGE_TM
    row = lambda i: (i, 0)
    fixed = lambda i: (0, 0)
    return pl.pallas_call(
        functools.partial(_merge_kernel, alpha=alpha, d=D),
        grid=(T // tm,),
        in_specs=[
            pl.BlockSpec((tm, D), row),
            pl.BlockSpec((tm, ATT_W), row),
            pl.BlockSpec((tm, ATT_W), row),
            pl.BlockSpec((D, 2 * D), fixed),
            pl.BlockSpec((ATT_W, D), fixed),
            pl.BlockSpec((ATT_W, D), fixed),
            pl.BlockSpec((D, D), fixed),
            pl.BlockSpec((1, D), fixed),
            pl.BlockSpec((1, D), fixed),
            pl.BlockSpec((N_EXPERTS, D), fixed),
            pl.BlockSpec((N_EXPERTS, 1), fixed),
        ],
        out_specs=[
            pl.BlockSpec((tm, D), row),
            pl.BlockSpec((tm, D // 2), row),
            pl.BlockSpec((TOP_K, tm), lambda i: (0, i)),
            pl.BlockSpec((TOP_K, tm), lambda i: (0, i)),
        ],
        out_shape=[
            jax.ShapeDtypeStruct((T, D), F32),
            jax.ShapeDtypeStruct((T, D // 2), jnp.uint32),
            jax.ShapeDtypeStruct((TOP_K, T), I32),
            jax.ShapeDtypeStruct((TOP_K, T), F32),
        ],
        compiler_params=_params(("arbitrary",)),
    )(x2, ya, yb, wg, wa, wb, wo, g, b, wr_t, br)


def _sc_gather_rows(table, idx):
    n = idx.shape[0]
    d = table.shape[1]
    info = plsc.get_sparse_core_info()
    n_cores, n_sub = info.num_cores, info.num_subcores
    per_w = n // (n_cores * n_sub)
    c = SC_GATHER_BYTES // (d * table.dtype.itemsize)
    n_g = per_w // c
    assert n == per_w * n_cores * n_sub and per_w == n_g * c and n_g % 2 == 0 and n_g >= 2
    mesh = plsc.VectorSubcoreMesh(core_axis_name="c", subcore_axis_name="s")

    @functools.partial(
        pl.kernel, mesh=mesh, out_type=jax.ShapeDtypeStruct((n, d), table.dtype),
        scratch_types=[pltpu.VMEM((per_w,), I32), pltpu.VMEM((2, c, d), table.dtype),
                       pltpu.SemaphoreType.DMA((2,)), pltpu.SemaphoreType.DMA((2,))])
    def gather_kernel(table_hbm, idx_hbm, out_hbm, idx_v, rows_v, gsem, wsem):
        base = (lax.axis_index("s") * n_cores + lax.axis_index("c")) * per_w
        pltpu.sync_copy(idx_hbm.at[pl.ds(base, per_w)], idx_v)

        def gather(g, b):
            return pltpu.make_async_copy(table_hbm.at[idx_v.at[pl.ds(g * c, c)]], rows_v.at[b],
                                         gsem.at[b])

        def write(g, b):
            return pltpu.make_async_copy(rows_v.at[b], out_hbm.at[pl.ds(base + g * c, c)], wsem.at[b])

        gather(0, 0).start()

        @pl.loop(0, n_g, step=2)
        def _ring(g0):
            for b in range(2):
                g = g0 + b

                @pl.when(g + 1 < n_g)
                def _next():
                    @pl.when(g >= 1)
                    def _buffer_free():
                        write(g - 1, 1 - b).wait()
                    gather(g + 1, 1 - b).start()

                gather(g, b).wait()
                write(g, b).start()

        write(n_g - 2, 0).wait()
        write(n_g - 1, 1).wait()

    return gather_kernel(table, idx)


def _moe_kernel(blk_e_ref, nused_ref, x_ref, wgu_ref, bgu_ref, wdn_ref, bdn_ref, o_ref,
                wgu_s, wdn_s, *, f):
    i = pl.program_id(0)
    nused = nused_ref[0]

    @pl.when(i < nused)
    def _compute():
        changed = jnp.logical_or(i == 0, blk_e_ref[i] != blk_e_ref[jnp.maximum(i - 1, 0)])

        @pl.when(changed)
        def _cast_weights():
            wgu_s[...] = wgu_ref[0].astype(BF16)
            wdn_s[...] = wdn_ref[0].astype(BF16)

        xw = x_ref[...]
        x = jnp.concatenate([pltpu.bitcast(xw & jnp.uint32(0xFFFF0000), F32),
                             pltpu.bitcast(xw << 16, F32)], axis=1).astype(BF16)
        hgu = jnp.dot(x, wgu_s[...], preferred_element_type=F32) + bgu_ref[0]
        a = jnp.minimum(hgu[:, :f], SWIGLU_LIMIT)
        u = jnp.clip(hgu[:, f:], -SWIGLU_LIMIT, SWIGLU_LIMIT)
        glu = a * jax.nn.sigmoid(a * SWIGLU_ALPHA)
        o_ref[...] = jnp.dot(((u + 1.0) * glu).astype(BF16), wdn_s[...],
                             preferred_element_type=F32) + bdn_ref[0]

    @pl.when(i >= nused)
    def _unused_block():
        o_ref[...] = jnp.zeros(o_ref.shape, o_ref.dtype)


def _moe_ffn(xs, blk_e, nused, w_gu, b_gu, w_dn, b_dn):
    P = xs.shape[0]
    E, D, F2 = w_gu.shape
    f = F2 // 2
    blk = MOE_BLK
    nb = P // blk
    used_block = lambda i, be, nu: (jnp.minimum(i, nu[0] - 1), 0)
    grid_spec = pltpu.PrefetchScalarGridSpec(
        num_scalar_prefetch=2,
        grid=(nb,),
        in_specs=[
            pl.BlockSpec((blk, D // 2), used_block),
            pl.BlockSpec((1, D, F2), lambda i, be, nu: (be[i], 0, 0)),
            pl.BlockSpec((1, 1, F2), lambda i, be, nu: (be[i], 0, 0)),
            pl.BlockSpec((1, f, D), lambda i, be, nu: (be[i], 0, 0)),
            pl.BlockSpec((1, 1, D), lambda i, be, nu: (be[i], 0, 0)),
        ],
        out_specs=pl.BlockSpec((blk, D), lambda i, be, nu: (i, 0)),
        scratch_shapes=[
            pltpu.VMEM((D, F2), BF16),
            pltpu.VMEM((f, D), BF16),
        ],
    )
    return pl.pallas_call(
        functools.partial(_moe_kernel, f=f),
        grid_spec=grid_spec,
        out_shape=jax.ShapeDtypeStruct((P, D), F32),
        compiler_params=_params(("arbitrary",)),
    )(blk_e, nused, xs, w_gu, b_gu.reshape(E, 1, F2), w_dn, b_dn.reshape(E, 1, D))


def _comb_kernel(h_ref, p_ref, y_ref, g_ref, b_ref, o_ref, *, alpha):
    gate = p_ref[...]
    fsum = ((y_ref[0] * gate[:, 0:1] + y_ref[1] * gate[:, 1:2])
            + (y_ref[2] * gate[:, 2:3] + y_ref[3] * gate[:, 3:4]))
    o_ref[...] = _layer_norm(alpha * h_ref[...] + fsum, g_ref[...], b_ref[...])


def _combine(h, y4, top_p, g, b, alpha):
    T, D = h.shape
    tm = COMB_TM
    return pl.pallas_call(
        functools.partial(_comb_kernel, alpha=alpha),
        grid=(T // tm,),
        in_specs=[
            pl.BlockSpec((tm, D), lambda i: (i, 0)),
            pl.BlockSpec((tm, TOP_K), lambda i: (i, 0)),
            pl.BlockSpec((TOP_K, tm, D), lambda i: (0, i, 0)),
            pl.BlockSpec((1, D), lambda i: (0, 0)),
            pl.BlockSpec((1, D), lambda i: (0, 0)),
        ],
        out_specs=pl.BlockSpec((tm, D), lambda i: (i, 0)),
        out_shape=jax.ShapeDtypeStruct((T, D), F32),
        compiler_params=_params(("arbitrary",)),
    )(h, top_p.T, y4, g, b)


def _route(top_e, blk):
    K, T = top_e.shape
    N = K * T
    flat_e = top_e.reshape(N)
    experts = jnp.arange(N_EXPERTS, dtype=I32)
    order = jnp.argsort(flat_e, stable=True).astype(I32)
    inv = jnp.argsort(order).astype(I32)
    onehot = flat_e[:, None] == experts[None, :]
    counts = jnp.sum(onehot, axis=0, dtype=I32)
    padded = (counts + blk - 1) // blk * blk
    pends = jnp.cumsum(padded)
    offs = jnp.cumsum(counts) - counts
    shift = (pends - padded) - offs
    pos = inv + jnp.sum(jnp.where(onehot, shift[None, :], 0), axis=1, dtype=I32)
    P = N + N_EXPERTS * blk
    nb = P // blk
    blk_start = jnp.arange(nb, dtype=I32) * blk
    blk_e = jnp.minimum(jnp.sum(pends[None, :] <= blk_start[:, None], axis=1, dtype=I32), N_EXPERTS - 1)
    j = (blk_start - shift[blk_e])[:, None] + jnp.arange(blk, dtype=I32)[None, :]
    valid = j < (offs + counts)[blk_e][:, None]
    src = order[jnp.clip(j, 0, N - 1)]
    row_tok = jnp.where(valid, src % T, j % T).reshape(P)
    nused = (pends[-1:] // blk).astype(I32)
    return blk_e, nused, row_tok, pos.reshape(K, T)


def _projection_weights(w_in_l):
    sizes = (ATT_W, ATT_W, ATT_W, IDX_HEADS * IDX_DIM, IDX_DIM, IDX_HEADS, ATT_W, ATT_W, ATT_W)
    offs = np.concatenate([[0], np.cumsum(sizes)])
    qa, ka, va, qi, ki, wi, qb, kb, vb = (w_in_l[:, offs[n]:offs[n + 1]] for n in range(9))
    pad_wi = jnp.zeros((w_in_l.shape[0], LANES - IDX_HEADS), w_in_l.dtype)
    w_att = jnp.concatenate([qa, ka, qi, qb, kb, vb, ki, ki, wi, pad_wi], axis=1).astype(BF16)
    w_va_t = va.T.astype(BF16)
    w_gate = w_in_l[:, offs[9]:].astype(BF16)
    return w_att, w_va_t, w_gate


def kernel(x, w_in, w_branch_a, w_branch_b, w_out, rel_bias, ln1_g, ln1_b, w_router, b_router,
           w_gate_up, b_gate_up, w_down, b_down, ln2_g, ln2_b):
    B, S, D = x.shape
    depth = w_in.shape[0]
    alpha = (2 * depth) ** 0.25
    T = B * S
    h = x.reshape(T, D)
    for l in range(depth):
        w_att, w_va_t, w_gate = _projection_weights(w_in[l])
        proj, vt = _projection(h, w_att, w_va_t, min(T, 1024), PROJ_TN, DSA_BLK)
        proj = proj.reshape(B, S, ATT_COLS)
        vt = vt.reshape(B, S // DSA_BLK, ATT_W, DSA_BLK)
        ya = _dsa(proj, vt, rel_bias).reshape(T, ATT_W)
        yb = _stick_breaking(proj).reshape(T, ATT_W)
        h1, h1_packed, top_e, top_p = _merge(
            h, ya, yb, w_gate, w_branch_a[l].astype(BF16), w_branch_b[l].astype(BF16),
            w_out[l].astype(BF16), ln1_g[l].reshape(1, D), ln1_b[l].reshape(1, D),
            w_router[l].T, b_router[l].reshape(N_EXPERTS, 1), alpha)
        blk_e, nused, row_tok, pos = _route(top_e, MOE_BLK)
        xs = _sc_gather_rows(h1_packed, row_tok)
        ys = _moe_ffn(xs, blk_e, nused, w_gate_up[l], b_gate_up[l], w_down[l], b_down[l])
        y4 = _sc_gather_rows(ys, pos.reshape(TOP_K * T)).reshape(TOP_K, T, D)
        h = _combine(h1, y4, top_p, ln2_g[l].reshape(1, D), ln2_b[l].reshape(1, D), alpha)
    return h.reshape(B, S, D)
```

```python
import functools
import math

import numpy as np
import jax
import jax.numpy as jnp
from jax import lax
from jax.experimental import pallas as pl
from jax.experimental.pallas import tpu as pltpu
from jax.experimental.pallas import tpu_sc as plsc

F32 = jnp.float32
BF16 = jnp.bfloat16
I32 = jnp.int32

A_HEADS = 8
HEAD_DIM = 64
ATT_W = A_HEADS * HEAD_DIM
IDX_HEADS = 8
IDX_DIM = 64
IDX_SCALE = (IDX_HEADS * IDX_DIM) ** -0.5
TOPK_MAX = 256
N_BUCKETS = 32
MAX_DISTANCE = 128
N_EXPERTS = 32
TOP_K = 4
SWIGLU_LIMIT = 7.0
SWIGLU_ALPHA = 1.702
LN_EPS = 1e-5
QK_SCALE = HEAD_DIM ** -0.5

LANES = 128
SUBLANES = 8
HALF = LANES // 2
N_PAIRS = A_HEADS // 2
VMEM_LIMIT = 56 * 1024 * 1024

DSA_BLK = 256
SB_T = 256
MERGE_TM = 512
MOE_BLK = 256
COMB_TM = 512
SC_GATHER_BYTES = 128 * 1024
REDUCE_CHAINS = 4
BISECT_CAP = 24
NEG = -1e30

COL_QA, COL_KA, COL_QI, COL_QB, COL_KB, COL_VB = (g * ATT_W for g in range(6))
COL_KK = 6 * ATT_W
COL_WI = COL_KK + LANES
ATT_COLS = COL_WI + LANES
PROJ_TN = ATT_COLS // 2

NT_DIMS = (((1,), (1,)), ((), ()))


def _params(sem, vmem=VMEM_LIMIT):
    return pltpu.CompilerParams(dimension_semantics=sem, vmem_limit_bytes=vmem)


def _proj_kernel(x_ref, w_ref, wt_ref, o_ref, ot_ref, xb_ref, *, tt):
    @pl.when(pl.program_id(1) == 0)
    def _row_tile_start():
        xb_ref[...] = x_ref[...].astype(BF16)
        for r in range(ot_ref.shape[0]):
            ot_ref[r] = lax.dot_general(wt_ref[...], xb_ref[r * tt:(r + 1) * tt, :], NT_DIMS,
                                        preferred_element_type=F32).astype(ot_ref.dtype)

    o_ref[...] = jnp.dot(xb_ref[...], w_ref[...], preferred_element_type=F32).astype(o_ref.dtype)


def _projection(x, w, w_t, tm, tn, tt):
    M, K = x.shape
    N = w.shape[1]
    Nt = w_t.shape[0]
    return pl.pallas_call(
        functools.partial(_proj_kernel, tt=tt),
        grid=(M // tm, N // tn),
        in_specs=[pl.BlockSpec((tm, K), lambda i, j: (i, 0)),
                  pl.BlockSpec((K, tn), lambda i, j: (0, j)),
                  pl.BlockSpec((Nt, K), lambda i, j: (0, 0))],
        out_specs=[pl.BlockSpec((tm, tn), lambda i, j: (i, j)),
                   pl.BlockSpec((tm // tt, Nt, tt), lambda i, j: (i, 0, 0))],
        out_shape=[jax.ShapeDtypeStruct((M, N), BF16),
                   jax.ShapeDtypeStruct((M // tt, Nt, tt), BF16)],
        scratch_shapes=[pltpu.VMEM((tm, K), BF16)],
        compiler_params=_params(("arbitrary", "arbitrary")),
    )(x, w, w_t)


def _t5_bucket_np(n):
    n = np.maximum(n, 0)
    max_exact = N_BUCKETS // 2
    nf = np.maximum(n, 1).astype(np.float32)
    large = max_exact + (np.log(nf / max_exact) / math.log(MAX_DISTANCE / max_exact)
                         * (N_BUCKETS - max_exact)).astype(np.int32)
    large = np.minimum(large, N_BUCKETS - 1)
    return np.where(n < max_exact, n, large).astype(np.int32)


def _dsa_n_off(blk):
    return 2 + -(-MAX_DISTANCE // blk)


def _dsa_bucket_tiles(blk):
    n_off = _dsa_n_off(blk)
    j = np.arange(blk)[None, :, None]
    i = np.arange(blk)[None, None, :]
    o = np.arange(n_off)[:, None, None]
    return _t5_bucket_np(i - j + blk * (n_off - 1 - o))


def _dsa_kernel(bucket_ref, relb_ref, q_ref, k_ref, vt_ref, qi_ref, kk_ref, wi_ref, o_ref,
                sc_ref, bias_ref, m_ref, l_ref, acc_ref, st_ref, mx_ref,
                *, blk, n_chunks, n_sel, n_off):
    b = pl.program_id(0)
    i = pl.program_id(1)
    q0 = i * blk
    nck = i + 1
    groups = blk // SUBLANES

    @pl.when(jnp.logical_and(b == 0, i == 0))
    def _build_bias():
        def head_body(h, _):
            for o in range(n_off):
                for rb in range(blk // LANES):
                    for cb in range(blk // LANES):
                        rs = slice(rb * LANES, (rb + 1) * LANES)
                        cs = slice(cb * LANES, (cb + 1) * LANES)
                        bk = bucket_ref[o, rs, cs]

                        def bucket_body(n, acc):
                            return jnp.where(bk == n, relb_ref[n, h], acc)

                        bias_ref[h, o, rs, cs] = lax.fori_loop(
                            0, N_BUCKETS, bucket_body, jnp.zeros((LANES, LANES), F32))
            return 0

        lax.fori_loop(0, A_HEADS, head_body, 0)

    lane = lax.broadcasted_iota(I32, (blk, LANES), 1)
    lo_half = lane < HALF
    krow = lax.broadcasted_iota(I32, (blk, blk), 0)
    qpos = q0 + lax.broadcasted_iota(I32, (1, blk), 1)

    def pair_split(ref, scale):
        out = []
        for p in range(N_PAIRS):
            v = ref[0, :, p * LANES:(p + 1) * LANES].astype(F32)
            if scale != 1.0:
                v = v * scale
            out.append(jnp.where(lo_half, v, 0.0).astype(BF16))
            out.append(jnp.where(lo_half, 0.0, v).astype(BF16))
        return out

    wi_t = wi_ref[0].astype(F32).T
    wrow = [wi_t[h:h + 1, :] * IDX_SCALE for h in range(IDX_HEADS)]
    qi_m = pair_split(qi_ref, 1.0)

    def score_chunk(c, _):
        c0 = pl.multiple_of(c * blk, blk)
        kk = kk_ref[0, pl.ds(c0, blk), :]
        acc = jnp.zeros((blk, blk), F32)
        for h in range(IDX_HEADS):
            s = lax.dot_general(kk, qi_m[h], NT_DIMS, preferred_element_type=F32)
            acc = acc + wrow[h] * jnp.maximum(s, 0.0)
        sc_ref[c] = jnp.where(c0 + krow <= qpos, acc, -jnp.inf)
        return 0

    lax.fori_loop(0, nck, score_chunk, 0)

    kt = jnp.minimum(qpos + 1, n_sel).astype(F32)

    def fold(fn, init):
        def body(c, acc):
            return fn(acc, sc_ref[c])
        return lax.fori_loop(0, nck, body, init)

    def part(x, op):
        y = op(x.reshape(REDUCE_CHAINS, groups // REDUCE_CHAINS, SUBLANES, blk), axis=1)
        return op(y, axis=0)

    def fin(x, op):
        return op(x, axis=0, keepdims=True)

    zeros8 = jnp.zeros((SUBLANES, blk), F32)
    pinf8 = jnp.full((SUBLANES, blk), jnp.inf, F32)

    def count_ge(th):
        return fin(fold(lambda a, s: a + part(jnp.where(s >= th, 1.0, 0.0), jnp.sum), zeros8),
                   jnp.sum)

    mn, mx = fold(lambda a, s: (
        jnp.minimum(a[0], part(jnp.where(s == -jnp.inf, jnp.inf, s), jnp.min)),
        jnp.maximum(a[1], part(s, jnp.max))), (pinf8, -pinf8))
    rmin = fin(mn, jnp.min)
    rmax = fin(mx, jnp.max)

    def bis_cond(st):
        it, lo, hi, clo = st
        return jnp.logical_and(it < BISECT_CAP, jnp.max(jnp.abs(clo - kt)) > 0.0)

    def halve(lo, hi, clo):
        mid = 0.5 * lo + 0.5 * hi
        c = count_ge(mid)
        active = clo != kt
        up = jnp.logical_and(active, c >= kt)
        dn = jnp.logical_and(active, c < kt)
        return jnp.where(up, mid, lo), jnp.where(dn, mid, hi), jnp.where(up, c, clo)

    def bis_body(st):
        it, lo, hi, clo = st
        return (it + 2,) + halve(*halve(lo, hi, clo))

    _, lo, _, _ = lax.while_loop(
        bis_cond, bis_body, (jnp.int32(0), rmin, rmax + 1.0, (qpos + 1).astype(F32)))

    def stats(lo_):
        a_ = fin(fold(lambda a, s: jnp.minimum(a, part(jnp.where(s >= lo_, s, jnp.inf), jnp.min)),
                      pinf8), jnp.min)
        cg, ct, nx = fold(
            lambda a, s: (a[0] + part(jnp.where(s > a_, 1.0, 0.0), jnp.sum),
                               a[1] + part(jnp.where(s == a_, 1.0, 0.0), jnp.sum),
                               jnp.minimum(a[2], part(jnp.where(s > a_, s, jnp.inf), jnp.min))),
            (zeros8, zeros8, pinf8))
        return a_, fin(cg, jnp.sum), fin(ct, jnp.sum), fin(nx, jnp.min)

    def fin_cond(st):
        return st[0]

    def fin_body(st):
        _, lo_, _, _ = st
        a_, cgt_, nt_, nxt_ = stats(lo_)
        bad = cgt_ >= kt
        return (jnp.max(jnp.where(bad, 1.0, 0.0)) > 0.0, jnp.where(bad, nxt_, a_), cgt_, nt_)

    _, a, cgt, nties = lax.while_loop(fin_cond, fin_body, (jnp.bool_(True), lo, kt, kt))
    need = kt - cgt
    excess = jnp.max(jnp.where(nties > need, 1.0, 0.0)) > 0.0

    def mask_plain():
        def body(c, _):
            sc_ref[c] = jnp.where(sc_ref[c] >= a, 0.0, NEG)
            return 0
        lax.fori_loop(0, nck, body, 0)

    def mask_ties():
        upto = (krow >= lax.broadcasted_iota(I32, (blk, blk), 1)).astype(BF16)

        def body(c, seen):
            s = sc_ref[c]
            tie = s == a
            rank = jnp.dot(upto, jnp.where(tie, 1.0, 0.0).astype(BF16),
                           preferred_element_type=F32) + seen
            sel = jnp.logical_or(s > a, jnp.logical_and(tie, rank <= need))
            sc_ref[c] = jnp.where(sel, 0.0, NEG)
            return rank[blk - 1:blk, :]

        lax.fori_loop(0, nck, body, jnp.zeros((1, blk), F32))

    lax.cond(excess, mask_ties, mask_plain)

    m_ref[...] = jnp.full(m_ref.shape, NEG, F32)
    l_ref[...] = jnp.zeros(l_ref.shape, F32)
    acc_ref[...] = jnp.zeros(acc_ref.shape, F32)
    q_m = pair_split(q_ref, QK_SCALE)

    def stage_logits(c, slot):
        c = jnp.minimum(c, n_chunks - 1)
        c0 = pl.multiple_of(c * blk, blk)
        madd = sc_ref[c]
        o_idx = jnp.clip(c - i + (n_off - 1), 0, n_off - 1)
        for p in range(N_PAIRS):
            k2 = k_ref[0, pl.ds(c0, blk), p * LANES:(p + 1) * LANES]
            for hh in range(2):
                h = 2 * p + hh
                s = lax.dot_general(k2, q_m[h], NT_DIMS, preferred_element_type=F32)
                s = s + bias_ref[h, o_idx] + madd
                st_ref[slot, h] = s
                mx_ref[slot, h] = fin(part(s, jnp.max), jnp.max)

    def stage_values(c, slot):
        for p in range(N_PAIRS):
            vt2 = vt_ref[0, c, p * LANES:(p + 1) * LANES, :]
            for hh in range(2):
                h = 2 * p + hh
                m_old = m_ref[h]
                m_new = jnp.maximum(m_old, mx_ref[slot, h])
                alpha = jnp.exp(m_old - m_new)
                pexp = jnp.exp(st_ref[slot, h] - m_new)
                l_ref[h] = alpha * l_ref[h] + fin(part(pexp, jnp.sum), jnp.sum)
                acc_ref[h] = alpha * acc_ref[h] + jnp.dot(vt2, pexp.astype(BF16),
                                                          preferred_element_type=F32)
                m_ref[h] = m_new

    stage_logits(0, 0)

    def att_pair(pp, _):
        c = 2 * pp
        stage_logits(c + 1, 1)
        stage_values(c, 0)
        stage_logits(c + 2, 0)
        stage_values(c + 1, 1)
        return 0

    lax.fori_loop(0, nck // 2, att_pair, 0)

    @pl.when(lax.rem(nck, 2) == 1)
    def _last_chunk():
        stage_values(nck - 1, 0)

    lo_rows = lax.broadcasted_iota(I32, (LANES, blk), 0) < HALF
    for p in range(N_PAIRS):
        oa = acc_ref[2 * p] / l_ref[2 * p]
        ob = acc_ref[2 * p + 1] / l_ref[2 * p + 1]
        o_ref[0, :, p * LANES:(p + 1) * LANES] = jnp.where(lo_rows, oa, ob).T.astype(o_ref.dtype)


def _dsa(proj3, vt4, rel_bias):
    B, S, _ = proj3.shape
    blk = DSA_BLK
    n_off = _dsa_n_off(blk)
    n_sel = min(TOPK_MAX, S // 4)
    bucket = jnp.asarray(_dsa_bucket_tiles(blk))
    n_chunks = S // blk
    assert S % blk == 0
    kern = functools.partial(_dsa_kernel, blk=blk, n_chunks=n_chunks, n_sel=n_sel, n_off=n_off)
    return pl.pallas_call(
        kern,
        grid=(B, S // blk),
        in_specs=[
            pl.BlockSpec((n_off, blk, blk), lambda b, i: (0, 0, 0)),
            pl.BlockSpec(memory_space=pltpu.SMEM),
            pl.BlockSpec((1, blk, ATT_W), lambda b, i: (b, i, COL_QA // ATT_W)),
            pl.BlockSpec((1, S, ATT_W), lambda b, i: (b, 0, COL_KA // ATT_W)),
            pl.BlockSpec((1, S // blk, ATT_W, blk), lambda b, i: (b, 0, 0, 0)),
            pl.BlockSpec((1, blk, ATT_W), lambda b, i: (b, i, COL_QI // ATT_W)),
            pl.BlockSpec((1, S, LANES), lambda b, i: (b, 0, COL_KK // LANES)),
            pl.BlockSpec((1, blk, LANES), lambda b, i: (b, i, COL_WI // LANES)),
        ],
        out_specs=pl.BlockSpec((1, blk, ATT_W), lambda b, i: (b, i, 0)),
        out_shape=jax.ShapeDtypeStruct((B, S, ATT_W), BF16),
        scratch_shapes=[
            pltpu.VMEM((S // blk, blk, blk), F32),
            pltpu.VMEM((A_HEADS, n_off, blk, blk), F32),
            pltpu.VMEM((A_HEADS, 1, blk), F32),
            pltpu.VMEM((A_HEADS, 1, blk), F32),
            pltpu.VMEM((A_HEADS, LANES, blk), F32),
            pltpu.VMEM((2, A_HEADS, blk, blk), F32),
            pltpu.VMEM((2, A_HEADS, 1, blk), F32),
        ],
        compiler_params=_params(("arbitrary", "arbitrary")),
    )(bucket, rel_bias, proj3, proj3, vt4, proj3, proj3, proj3)


def _sb_kernel(q_ref, k_ref, v_ref, o_ref, hl_ref, z_ref, *, t):
    i = pl.program_id(2)
    n = i + 1
    lane = lax.broadcasted_iota(I32, (t, LANES), 1)
    lo_half = lane < HALF
    q2 = q_ref[0].astype(F32) * QK_SCALE
    q_m = (jnp.where(lo_half, q2, 0.0).astype(BF16), jnp.where(lo_half, 0.0, q2).astype(BF16))
    r = lax.broadcasted_iota(I32, (t, t), 0)
    cidx = lax.broadcasted_iota(I32, (t, t), 1)
    neg_from = jnp.where(r >= cidx, -1.0, 0.0).astype(BF16)
    diff = cidx - r

    def stage_terms(step, slot, diagonal):
        c0 = pl.multiple_of(jnp.maximum(i - step, 0) * t, t)
        k2 = k_ref[0, pl.ds(c0, t), :]
        if diagonal:
            keep = diff < 0
        for hh in range(2):
            z = lax.dot_general(q_m[hh], k2, NT_DIMS, preferred_element_type=F32)
            sp = jnp.maximum(z, 0.0) + jnp.log(1.0 + jnp.exp(-jnp.abs(z)))
            if diagonal:
                sp = jnp.where(keep, sp, 0.0)
                z = jnp.where(keep, z, NEG)
            hi = sp.astype(BF16)
            hl_ref[slot, 2 * hh] = hi
            hl_ref[slot, 2 * hh + 1] = (sp - hi.astype(F32)).astype(BF16)
            z_ref[slot, hh] = z

    def stage_apply(step, slot, carry):
        c0 = pl.multiple_of((i - step) * t, t)
        v2 = v_ref[0, pl.ds(c0, t), :]
        cum4 = jnp.dot(hl_ref[slot].reshape(4 * t, t), neg_from, preferred_element_type=F32)
        out = []
        for hh in range(2):
            car, acc = carry[hh]
            cum = cum4[(2 * hh) * t:(2 * hh + 1) * t] + cum4[(2 * hh + 1) * t:(2 * hh + 2) * t]
            w = jnp.exp(z_ref[slot, hh] + cum + car)
            acc = acc + jnp.dot(w.astype(BF16), v2, preferred_element_type=F32)
            out.append((car + cum[:, 0:1], acc))
        return tuple(out)

    z1 = jnp.zeros((t, 1), F32)
    za = jnp.zeros((t, LANES), F32)
    stage_terms(0, 0, True)

    def pair_body(pp, carry):
        step = 2 * pp
        stage_terms(step + 1, 1, False)
        carry = stage_apply(step, 0, carry)
        stage_terms(step + 2, 0, False)
        return stage_apply(step + 1, 1, carry)

    carry = lax.fori_loop(0, n // 2, pair_body, ((z1, za), (z1, za)))
    (_, acc_a), (_, acc_b) = lax.cond(lax.rem(n, 2) == 1,
                                      lambda c: stage_apply(n - 1, 0, c), lambda c: c, carry)
    o_ref[0] = jnp.where(lo_half, acc_a, acc_b).astype(o_ref.dtype)


def _stick_breaking(proj3):
    B, S, _ = proj3.shape
    t = SB_T
    qb, kb, vb = COL_QB // LANES, COL_KB // LANES, COL_VB // LANES
    return pl.pallas_call(
        functools.partial(_sb_kernel, t=t),
        grid=(B, N_PAIRS, S // t),
        in_specs=[
            pl.BlockSpec((1, t, LANES), lambda b, p, i: (b, i, qb + p)),
            pl.BlockSpec((1, S, LANES), lambda b, p, i: (b, 0, kb + p)),
            pl.BlockSpec((1, S, LANES), lambda b, p, i: (b, 0, vb + p)),
        ],
        out_specs=pl.BlockSpec((1, t, LANES), lambda b, p, i: (b, i, p)),
        out_shape=jax.ShapeDtypeStruct((B, S, ATT_W), BF16),
        scratch_shapes=[
            pltpu.VMEM((2, 4, t, t), BF16),
            pltpu.VMEM((2, 2, t, t), F32),
        ],
        compiler_params=_params(("arbitrary", "arbitrary", "arbitrary")),
    )(proj3, proj3, proj3)


def _layer_norm(r, g, b):
    mu = jnp.mean(r, axis=-1, keepdims=True)
    d = r - mu
    var = jnp.mean(d * d, axis=-1, keepdims=True)
    return d * lax.rsqrt(var + LN_EPS) * g + b


def _split_bf16(v):
    hi = v.astype(BF16)
    return hi, (v - hi.astype(F32)).astype(BF16)


def _merge_kernel(x_ref, ya_ref, yb_ref, wg_ref, wa_ref, wb_ref, wo_ref, g_ref, b_ref,
                  wr_ref, br_ref, h_ref, hp_ref, e_ref, p_ref, *, alpha, d):
    pa = jnp.dot(ya_ref[...], wa_ref[...], preferred_element_type=F32)
    pb = jnp.dot(yb_ref[...], wb_ref[...], preferred_element_type=F32)
    gates = jnp.dot(x_ref[...].astype(BF16), wg_ref[...], preferred_element_type=F32)
    merged = jax.nn.sigmoid(gates[:, :d]) * pa + jax.nn.sigmoid(gates[:, d:]) * pb
    m = jnp.dot(merged.astype(BF16), wo_ref[...], preferred_element_type=F32)
    h = _layer_norm(alpha * x_ref[...] + m, g_ref[...], b_ref[...])
    h_ref[...] = h
    bits = pltpu.bitcast(h.astype(BF16).astype(F32), jnp.uint32)
    hp_ref[...] = bits[:, :d // 2] | (bits[:, d // 2:] >> 16)

    h_hi, h_lo = _split_bf16(h)
    w_hi, w_lo = _split_bf16(wr_ref[...])
    logit = (lax.dot_general(w_hi, h_hi, NT_DIMS, preferred_element_type=F32)
             + lax.dot_general(w_hi, h_lo, NT_DIMS, preferred_element_type=F32)
             + lax.dot_general(w_lo, h_hi, NT_DIMS, preferred_element_type=F32)) + br_ref[...]
    eid = lax.broadcasted_iota(I32, logit.shape, 0)
    vals, ids = [], []
    for _ in range(TOP_K):
        mx = jnp.max(logit, axis=0, keepdims=True)
        am = jnp.min(jnp.where(logit == mx, eid, N_EXPERTS), axis=0, keepdims=True)
        vals.append(mx)
        ids.append(am)
        logit = jnp.where(eid == am, -jnp.inf, logit)
    ex = [jnp.exp(v - vals[0]) for v in vals]
    den = ex[0] + ex[1] + ex[2] + ex[3]
    for k in range(TOP_K):
        e_ref[k:k + 1, :] = ids[k]
        p_ref[k:k + 1, :] = ex[k] / den


def _merge(x2, ya, yb, wg, wa, wb, wo, g, b, wr_t, br, alpha):
    T, D = x2.shape
    tm = MERGE_TM
    row = lambda i: (i, 0)
    fixed = lambda i: (0, 0)
    return pl.pallas_call(
        functools.partial(_merge_kernel, alpha=alpha, d=D),
        grid=(T // tm,),
        in_specs=[
            pl.BlockSpec((tm, D), row),
            pl.BlockSpec((tm, ATT_W), row),
            pl.BlockSpec((tm, ATT_W), row),
            pl.BlockSpec((D, 2 * D), fixed),
            pl.BlockSpec((ATT_W, D), fixed),
            pl.BlockSpec((ATT_W, D), fixed),
            pl.BlockSpec((D, D), fixed),
            pl.BlockSpec((1, D), fixed),
            pl.BlockSpec((1, D), fixed),
            pl.BlockSpec((N_EXPERTS, D), fixed),
            pl.BlockSpec((N_EXPERTS, 1), fixed),
        ],
        out_specs=[
            pl.BlockSpec((tm, D), row),
            pl.BlockSpec((tm, D // 2), row),
            pl.BlockSpec((TOP_K, tm), lambda i: (0, i)),
            pl.BlockSpec((TOP_K, tm), lambda i: (0, i)),
        ],
        out_shape=[
            jax.ShapeDtypeStruct((T, D), F32),
            jax.ShapeDtypeStruct((T, D // 2), jnp.uint32),
            jax.ShapeDtypeStruct((TOP_K, T), I32),
            jax.ShapeDtypeStruct((TOP_K, T), F32),
        ],
        compiler_params=_params(("arbitrary",)),
    )(x2, ya, yb, wg, wa, wb, wo, g, b, wr_t, br)


def _sc_gather_rows(table, idx):
    n = idx.shape[0]
    d = table.shape[1]
    info = plsc.get_sparse_core_info()
    n_cores, n_sub = info.num_cores, info.num_subcores
    per_w = n // (n_cores * n_sub)
    c = SC_GATHER_BYTES // (d * table.dtype.itemsize)
    n_g = per_w // c
    assert n == per_w * n_cores * n_sub and per_w == n_g * c and n_g % 2 == 0 and n_g >= 2
    mesh = plsc.VectorSubcoreMesh(core_axis_name="c", subcore_axis_name="s")

    @functools.partial(
        pl.kernel, mesh=mesh, out_type=jax.ShapeDtypeStruct((n, d), table.dtype),
        scratch_types=[pltpu.VMEM((per_w,), I32), pltpu.VMEM((2, c, d), table.dtype),
                       pltpu.SemaphoreType.DMA((2,)), pltpu.SemaphoreType.DMA((2,))])
    def gather_kernel(table_hbm, idx_hbm, out_hbm, idx_v, rows_v, gsem, wsem):
        base = (lax.axis_index("s") * n_cores + lax.axis_index("c")) * per_w
        pltpu.sync_copy(idx_hbm.at[pl.ds(base, per_w)], idx_v)

        def gather(g, b):
            return pltpu.make_async_copy(table_hbm.at[idx_v.at[pl.ds(g * c, c)]], rows_v.at[b],
                                         gsem.at[b])

        def write(g, b):
            return pltpu.make_async_copy(rows_v.at[b], out_hbm.at[pl.ds(base + g * c, c)], wsem.at[b])

        gather(0, 0).start()

        @pl.loop(0, n_g, step=2)
        def _ring(g0):
            for b in range(2):
                g = g0 + b

                @pl.when(g + 1 < n_g)
                def _next():
                    @pl.when(g >= 1)
                    def _buffer_free():
                        write(g - 1, 1 - b).wait()
                    gather(g + 1, 1 - b).start()

                gather(g, b).wait()
                write(g, b).start()

        write(n_g - 2, 0).wait()
        write(n_g - 1, 1).wait()

    return gather_kernel(table, idx)


def _moe_kernel(blk_e_ref, nused_ref, x_ref, wgu_ref, bgu_ref, wdn_ref, bdn_ref, o_ref,
                wgu_s, wdn_s, *, f):
    i = pl.program_id(0)
    nused = nused_ref[0]

    @pl.when(i < nused)
    def _compute():
        changed = jnp.logical_or(i == 0, blk_e_ref[i] != blk_e_ref[jnp.maximum(i - 1, 0)])

        @pl.when(changed)
        def _cast_weights():
            wgu_s[...] = wgu_ref[0].astype(BF16)
            wdn_s[...] = wdn_ref[0].astype(BF16)

        xw = x_ref[...]
        x = jnp.concatenate([pltpu.bitcast(xw & jnp.uint32(0xFFFF0000), F32),
                             pltpu.bitcast(xw << 16, F32)], axis=1).astype(BF16)
        hgu = jnp.dot(x, wgu_s[...], preferred_element_type=F32) + bgu_ref[0]
        a = jnp.minimum(hgu[:, :f], SWIGLU_LIMIT)
        u = jnp.clip(hgu[:, f:], -SWIGLU_LIMIT, SWIGLU_LIMIT)
        glu = a * jax.nn.sigmoid(a * SWIGLU_ALPHA)
        o_ref[...] = jnp.dot(((u + 1.0) * glu).astype(BF16), wdn_s[...],
                             preferred_element_type=F32) + bdn_ref[0]

    @pl.when(i >= nused)
    def _unused_block():
        o_ref[...] = jnp.zeros(o_ref.shape, o_ref.dtype)


def _moe_ffn(xs, blk_e, nused, w_gu, b_gu, w_dn, b_dn):
    P = xs.shape[0]
    E, D, F2 = w_gu.shape
    f = F2 // 2
    blk = MOE_BLK
    nb = P // blk
    used_block = lambda i, be, nu: (jnp.minimum(i, nu[0] - 1), 0)
    grid_spec = pltpu.PrefetchScalarGridSpec(
        num_scalar_prefetch=2,
        grid=(nb,),
        in_specs=[
            pl.BlockSpec((blk, D // 2), used_block),
            pl.BlockSpec((1, D, F2), lambda i, be, nu: (be[i], 0, 0)),
            pl.BlockSpec((1, 1, F2), lambda i, be, nu: (be[i], 0, 0)),
            pl.BlockSpec((1, f, D), lambda i, be, nu: (be[i], 0, 0)),
            pl.BlockSpec((1, 1, D), lambda i, be, nu: (be[i], 0, 0)),
        ],
        out_specs=pl.BlockSpec((blk, D), lambda i, be, nu: (i, 0)),
        scratch_shapes=[
            pltpu.VMEM((D, F2), BF16),
            pltpu.VMEM((f, D), BF16),
        ],
    )
    return pl.pallas_call(
        functools.partial(_moe_kernel, f=f),
        grid_spec=grid_spec,
        out_shape=jax.ShapeDtypeStruct((P, D), F32),
        compiler_params=_params(("arbitrary",)),
    )(blk_e, nused, xs, w_gu, b_gu.reshape(E, 1, F2), w_dn, b_dn.reshape(E, 1, D))


def _comb_kernel(h_ref, p_ref, y_ref, g_ref, b_ref, o_ref, *, alpha):
    gate = p_ref[...]
    fsum = ((y_ref[0] * gate[:, 0:1] + y_ref[1] * gate[:, 1:2])
            + (y_ref[2] * gate[:, 2:3] + y_ref[3] * gate[:, 3:4]))
    o_ref[...] = _layer_norm(alpha * h_ref[...] + fsum, g_ref[...], b_ref[...])


def _combine(h, y4, top_p, g, b, alpha):
    T, D = h.shape
    tm = COMB_TM
    return pl.pallas_call(
        functools.partial(_comb_kernel, alpha=alpha),
        grid=(T // tm,),
        in_specs=[
            pl.BlockSpec((tm, D), lambda i: (i, 0)),
            pl.BlockSpec((tm, TOP_K), lambda i: (i, 0)),
            pl.BlockSpec((TOP_K, tm, D), lambda i: (0, i, 0)),
            pl.BlockSpec((1, D), lambda i: (0, 0)),
            pl.BlockSpec((1, D), lambda i: (0, 0)),
        ],
        out_specs=pl.BlockSpec((tm, D), lambda i: (i, 0)),
        out_shape=jax.ShapeDtypeStruct((T, D), F32),
        compiler_params=_params(("arbitrary",)),
    )(h, top_p.T, y4, g, b)


def _route(top_e, blk):
    K, T = top_e.shape
    N = K * T
    flat_e = top_e.reshape(N)
    experts = jnp.arange(N_EXPERTS, dtype=I32)
    order = jnp.argsort(flat_e, stable=True).astype(I32)
    inv = jnp.argsort(order).astype(I32)
    onehot = flat_e[:, None] == experts[None, :]
    counts = jnp.sum(onehot, axis=0, dtype=I32)
    padded = (counts + blk - 1) // blk * blk
    pends = jnp.cumsum(padded)
    offs = jnp.cumsum(counts) - counts
    shift = (pends - padded) - offs
    pos = inv + jnp.sum(jnp.where(onehot, shift[None, :], 0), axis=1, dtype=I32)
    P = N + N_EXPERTS * blk
    nb = P // blk
    blk_start = jnp.arange(nb, dtype=I32) * blk
    blk_e = jnp.minimum(jnp.sum(pends[None, :] <= blk_start[:, None], axis=1, dtype=I32), N_EXPERTS - 1)
    j = (blk_start - shift[blk_e])[:, None] + jnp.arange(blk, dtype=I32)[None, :]
    valid = j < (offs + counts)[blk_e][:, None]
    src = order[jnp.clip(j, 0, N - 1)]
    row_tok = jnp.where(valid, src % T, j % T).reshape(P)
    nused = (pends[-1:] // blk).astype(I32)
    return blk_e, nused, row_tok, pos.reshape(K, T)


def _projection_weights(w_in_l):
    sizes = (ATT_W, ATT_W, ATT_W, IDX_HEADS * IDX_DIM, IDX_DIM, IDX_HEADS, ATT_W, ATT_W, ATT_W)
    offs = np.concatenate([[0], np.cumsum(sizes)])
    qa, ka, va, qi, ki, wi, qb, kb, vb = (w_in_l[:, offs[n]:offs[n + 1]] for n in range(9))
    pad_wi = jnp.zeros((w_in_l.shape[0], LANES - IDX_HEADS), w_in_l.dtype)
    w_att = jnp.concatenate([qa, ka, qi, qb, kb, vb, ki, ki, wi, pad_wi], axis=1).astype(BF16)
    w_va_t = va.T.astype(BF16)
    w_gate = w_in_l[:, offs[9]:].astype(BF16)
    return w_att, w_va_t, w_gate


def kernel(x, w_in, w_branch_a, w_branch_b, w_out, rel_bias, ln1_g, ln1_b, w_router, b_router,
           w_gate_up, b_gate_up, w_down, b_down, ln2_g, ln2_b):
    B, S, D = x.shape
    depth = w_in.shape[0]
    alpha = (2 * depth) ** 0.25
    T = B * S
    h = x.reshape(T, D)
    for l in range(depth):
        w_att, w_va_t, w_gate = _projection_weights(w_in[l])
        proj, vt = _projection(h, w_att, w_va_t, min(T, 1024), PROJ_TN, DSA_BLK)
        proj = proj.reshape(B, S, ATT_COLS)
        vt = vt.reshape(B, S // DSA_BLK, ATT_W, DSA_BLK)
        ya = _dsa(proj, vt, rel_bias).reshape(T, ATT_W)
        yb = _stick_breaking(proj).reshape(T, ATT_W)
        h1, h1_packed, top_e, top_p = _merge(
            h, ya, yb, w_gate, w_branch_a[l].astype(BF16), w_branch_b[l].astype(BF16),
            w_out[l].astype(BF16), ln1_g[l].reshape(1, D), ln1_b[l].reshape(1, D),
            w_router[l].T, b_router[l].reshape(N_EXPERTS, 1), alpha)
        blk_e, nused, row_tok, pos = _route(top_e, MOE_BLK)
        xs = _sc_gather_rows(h1_packed, row_tok)
        ys = _moe_ffn(xs, blk_e, nused, w_gate_up[l], b_gate_up[l], w_down[l], b_down[l])
        y4 = _sc_gather_rows(ys, pos.reshape(TOP_K * T)).reshape(TOP_K, T, D)
        h = _combine(h1, y4, top_p, ln2_g[l].reshape(1, D), ln2_b[l].reshape(1, D), alpha)
    return h.reshape(B, S, D)
```

```python
import functools
import math

import numpy as np
import jax
import jax.numpy as jnp
from jax import lax
from jax.experimental import pallas as pl
from jax.experimental.pallas import tpu as pltpu
from jax.experimental.pallas import tpu_sc as plsc

F32 = jnp.float32
BF16 = jnp.bfloat16
I32 = jnp.int32

A_HEADS = 8
HEAD_DIM = 64
ATT_W = A_HEADS * HEAD_DIM
IDX_HEADS = 8
IDX_DIM = 64
IDX_SCALE = (IDX_HEADS * IDX_DIM) ** -0.5
TOPK_MAX = 256
N_BUCKETS = 32
MAX_DISTANCE = 128
N_EXPERTS = 32
TOP_K = 4
SWIGLU_LIMIT = 7.0
SWIGLU_ALPHA = 1.702
LN_EPS = 1e-5
QK_SCALE = HEAD_DIM ** -0.5

LANES = 128
SUBLANES = 8
HALF = LANES // 2
N_PAIRS = A_HEADS // 2
VMEM_LIMIT = 56 * 1024 * 1024

DSA_BLK = 256
SB_T = 256
SB_PAIRS = 2
MERGE_TM = 512
MOE_BLK = 512
COMB_TM = 512
SC_GATHER_BYTES = 128 * 1024
REDUCE_CHAINS = 4
BISECT_CAP = 24
NEG = -1e30

COL_QA, COL_KA, COL_QI, COL_QB, COL_KB, COL_VB = (g * ATT_W for g in range(6))
COL_KK = 6 * ATT_W
COL_WI = COL_KK + LANES
ATT_COLS = COL_WI + LANES
PROJ_TN = ATT_COLS // 2

NT_DIMS = (((1,), (1,)), ((), ()))


def _params(sem, vmem=VMEM_LIMIT):
    return pltpu.CompilerParams(dimension_semantics=sem, vmem_limit_bytes=vmem)


def _proj_kernel(x_ref, w_ref, wt_ref, o_ref, ot_ref, xb_ref, *, tt):
    @pl.when(pl.program_id(1) == 0)
    def _row_tile_start():
        xb_ref[...] = x_ref[...].astype(BF16)
        for r in range(ot_ref.shape[0]):
            ot_ref[r] = lax.dot_general(wt_ref[...], xb_ref[r * tt:(r + 1) * tt, :], NT_DIMS,
                                        preferred_element_type=F32).astype(ot_ref.dtype)

    o_ref[...] = jnp.dot(xb_ref[...], w_ref[...], preferred_element_type=F32).astype(o_ref.dtype)


def _projection(x, w, w_t, tm, tn, tt):
    M, K = x.shape
    N = w.shape[1]
    Nt = w_t.shape[0]
    return pl.pallas_call(
        functools.partial(_proj_kernel, tt=tt),
        grid=(M // tm, N // tn),
        in_specs=[pl.BlockSpec((tm, K), lambda i, j: (i, 0)),
                  pl.BlockSpec((K, tn), lambda i, j: (0, j)),
                  pl.BlockSpec((Nt, K), lambda i, j: (0, 0))],
        out_specs=[pl.BlockSpec((tm, tn), lambda i, j: (i, j)),
                   pl.BlockSpec((tm // tt, Nt, tt), lambda i, j: (i, 0, 0))],
        out_shape=[jax.ShapeDtypeStruct((M, N), BF16),
                   jax.ShapeDtypeStruct((M // tt, Nt, tt), BF16)],
        scratch_shapes=[pltpu.VMEM((tm, K), BF16)],
        compiler_params=_params(("arbitrary", "arbitrary")),
    )(x, w, w_t)


def _t5_bucket_np(n):
    n = np.maximum(n, 0)
    max_exact = N_BUCKETS // 2
    nf = np.maximum(n, 1).astype(np.float32)
    large = max_exact + (np.log(nf / max_exact) / math.log(MAX_DISTANCE / max_exact)
                         * (N_BUCKETS - max_exact)).astype(np.int32)
    large = np.minimum(large, N_BUCKETS - 1)
    return np.where(n < max_exact, n, large).astype(np.int32)


def _dsa_n_off(blk):
    return 2 + -(-MAX_DISTANCE // blk)


def _dsa_bucket_tiles(blk):
    n_off = _dsa_n_off(blk)
    j = np.arange(blk)[None, :, None]
    i = np.arange(blk)[None, None, :]
    o = np.arange(n_off)[:, None, None]
    return _t5_bucket_np(i - j + blk * (n_off - 1 - o))


def _dsa_kernel(bucket_ref, relb_ref, q_ref, k_ref, vt_ref, qi_ref, kk_ref, wi_ref, o_ref,
                sc_ref, bias_ref, m_ref, l_ref, acc_ref, st_ref, mx_ref,
                *, blk, n_chunks, n_sel, n_off):
    b = pl.program_id(0)
    i = pl.program_id(1)
    q0 = i * blk
    nck = i + 1
    groups = blk // SUBLANES

    @pl.when(jnp.logical_and(b == 0, i == 0))
    def _build_bias():
        def head_body(h, _):
            for o in range(n_off):
                for rb in range(blk // LANES):
                    for cb in range(blk // LANES):
                        rs = slice(rb * LANES, (rb + 1) * LANES)
                        cs = slice(cb * LANES, (cb + 1) * LANES)
                        bk = bucket_ref[o, rs, cs]

                        def bucket_body(n, acc):
                            return jnp.where(bk == n, relb_ref[n, h], acc)

                        bias_ref[h, o, rs, cs] = lax.fori_loop(
                            0, N_BUCKETS, bucket_body, jnp.zeros((LANES, LANES), F32))
            return 0

        lax.fori_loop(0, A_HEADS, head_body, 0)

    lane = lax.broadcasted_iota(I32, (blk, LANES), 1)
    lo_half = lane < HALF
    krow = lax.broadcasted_iota(I32, (blk, blk), 0)
    qpos = q0 + lax.broadcasted_iota(I32, (1, blk), 1)

    def pair_split(ref, scale):
        out = []
        for p in range(N_PAIRS):
            v = ref[0, :, p * LANES:(p + 1) * LANES].astype(F32)
            if scale != 1.0:
                v = v * scale
            out.append(jnp.where(lo_half, v, 0.0).astype(BF16))
            out.append(jnp.where(lo_half, 0.0, v).astype(BF16))
        return out

    wi_t = wi_ref[0].astype(F32).T
    wrow = [wi_t[h:h + 1, :] * IDX_SCALE for h in range(IDX_HEADS)]
    qi_m = pair_split(qi_ref, 1.0)

    def score_chunk(c, _):
        c0 = pl.multiple_of(c * blk, blk)
        kk = kk_ref[0, pl.ds(c0, blk), :]
        acc = jnp.zeros((blk, blk), F32)
        for h in range(IDX_HEADS):
            s = lax.dot_general(kk, qi_m[h], NT_DIMS, preferred_element_type=F32)
            acc = acc + wrow[h] * jnp.maximum(s, 0.0)
        sc_ref[c] = jnp.where(c0 + krow <= qpos, acc, -jnp.inf)
        return 0

    lax.fori_loop(0, nck, score_chunk, 0)

    kt = jnp.minimum(qpos + 1, n_sel).astype(F32)

    def fold(fn, init):
        def body(c, acc):
            return fn(acc, sc_ref[c])
        return lax.fori_loop(0, nck, body, init)

    def part(x, op):
        y = op(x.reshape(REDUCE_CHAINS, groups // REDUCE_CHAINS, SUBLANES, blk), axis=1)
        return op(y, axis=0)

    def fin(x, op):
        return op(x, axis=0, keepdims=True)

    zeros8 = jnp.zeros((SUBLANES, blk), F32)
    pinf8 = jnp.full((SUBLANES, blk), jnp.inf, F32)

    def count_ge(th):
        return fin(fold(lambda a, s: a + part(jnp.where(s >= th, 1.0, 0.0), jnp.sum), zeros8),
                   jnp.sum)

    mn, mx = fold(lambda a, s: (
        jnp.minimum(a[0], part(jnp.where(s == -jnp.inf, jnp.inf, s), jnp.min)),
        jnp.maximum(a[1], part(s, jnp.max))), (pinf8, -pinf8))
    rmin = fin(mn, jnp.min)
    rmax = fin(mx, jnp.max)

    def bis_cond(st):
        it, lo, hi, clo = st
        return jnp.logical_and(it < BISECT_CAP, jnp.max(jnp.abs(clo - kt)) > 0.0)

    def halve(lo, hi, clo):
        mid = 0.5 * lo + 0.5 * hi
        c = count_ge(mid)
        active = clo != kt
        up = jnp.logical_and(active, c >= kt)
        dn = jnp.logical_and(active, c < kt)
        return jnp.where(up, mid, lo), jnp.where(dn, mid, hi), jnp.where(up, c, clo)

    def bis_body(st):
        it, lo, hi, clo = st
        return (it + 2,) + halve(*halve(lo, hi, clo))

    _, lo, _, _ = lax.while_loop(
        bis_cond, bis_body, (jnp.int32(0), rmin, rmax + 1.0, (qpos + 1).astype(F32)))

    def stats(lo_):
        a_ = fin(fold(lambda a, s: jnp.minimum(a, part(jnp.where(s >= lo_, s, jnp.inf), jnp.min)),
                      pinf8), jnp.min)
        cg, ct, nx = fold(
            lambda a, s: (a[0] + part(jnp.where(s > a_, 1.0, 0.0), jnp.sum),
                               a[1] + part(jnp.where(s == a_, 1.0, 0.0), jnp.sum),
                               jnp.minimum(a[2], part(jnp.where(s > a_, s, jnp.inf), jnp.min))),
            (zeros8, zeros8, pinf8))
        return a_, fin(cg, jnp.sum), fin(ct, jnp.sum), fin(nx, jnp.min)

    def fin_cond(st):
        return st[0]

    def fin_body(st):
        _, lo_, _, _ = st
        a_, cgt_, nt_, nxt_ = stats(lo_)
        bad = cgt_ >= kt
        return (jnp.max(jnp.where(bad, 1.0, 0.0)) > 0.0, jnp.where(bad, nxt_, a_), cgt_, nt_)

    _, a, cgt, nties = lax.while_loop(fin_cond, fin_body, (jnp.bool_(True), lo, kt, kt))
    need = kt - cgt
    excess = jnp.max(jnp.where(nties > need, 1.0, 0.0)) > 0.0

    def mask_plain():
        def body(c, _):
            sc_ref[c] = jnp.where(sc_ref[c] >= a, 0.0, NEG)
            return 0
        lax.fori_loop(0, nck, body, 0)

    def mask_ties():
        upto = (krow >= lax.broadcasted_iota(I32, (blk, blk), 1)).astype(BF16)

        def body(c, seen):
            s = sc_ref[c]
            tie = s == a
            rank = jnp.dot(upto, jnp.where(tie, 1.0, 0.0).astype(BF16),
                           preferred_element_type=F32) + seen
            sel = jnp.logical_or(s > a, jnp.logical_and(tie, rank <= need))
            sc_ref[c] = jnp.where(sel, 0.0, NEG)
            return rank[blk - 1:blk, :]

        lax.fori_loop(0, nck, body, jnp.zeros((1, blk), F32))

    lax.cond(excess, mask_ties, mask_plain)

    m_ref[...] = jnp.full(m_ref.shape, NEG, F32)
    l_ref[...] = jnp.zeros(l_ref.shape, F32)
    acc_ref[...] = jnp.zeros(acc_ref.shape, F32)
    q_m = pair_split(q_ref, QK_SCALE)

    def stage_logits(c, slot):
        c = jnp.minimum(c, n_chunks - 1)
        c0 = pl.multiple_of(c * blk, blk)
        madd = sc_ref[c]
        o_idx = jnp.clip(c - i + (n_off - 1), 0, n_off - 1)
        for p in range(N_PAIRS):
            k2 = k_ref[0, pl.ds(c0, blk), p * LANES:(p + 1) * LANES]
            for hh in range(2):
                h = 2 * p + hh
                s = lax.dot_general(k2, q_m[h], NT_DIMS, preferred_element_type=F32)
                s = s + bias_ref[h, o_idx] + madd
                st_ref[slot, h] = s
                mx_ref[slot, h] = fin(part(s, jnp.max), jnp.max)

    def stage_values(c, slot):
        for p in range(N_PAIRS):
            vt2 = vt_ref[0, c, p * LANES:(p + 1) * LANES, :]
            for hh in range(2):
                h = 2 * p + hh
                m_old = m_ref[h]
                m_new = jnp.maximum(m_old, mx_ref[slot, h])
                alpha = jnp.exp(m_old - m_new)
                pexp = jnp.exp(st_ref[slot, h] - m_new)
                l_ref[h] = alpha * l_ref[h] + fin(part(pexp, jnp.sum), jnp.sum)
                acc_ref[h] = alpha * acc_ref[h] + jnp.dot(vt2, pexp.astype(BF16),
                                                          preferred_element_type=F32)
                m_ref[h] = m_new

    stage_logits(0, 0)

    def att_pair(pp, _):
        c = 2 * pp
        stage_logits(c + 1, 1)
        stage_values(c, 0)
        stage_logits(c + 2, 0)
        stage_values(c + 1, 1)
        return 0

    lax.fori_loop(0, nck // 2, att_pair, 0)

    @pl.when(lax.rem(nck, 2) == 1)
    def _last_chunk():
        stage_values(nck - 1, 0)

    lo_rows = lax.broadcasted_iota(I32, (LANES, blk), 0) < HALF
    for p in range(N_PAIRS):
        oa = acc_ref[2 * p] / l_ref[2 * p]
        ob = acc_ref[2 * p + 1] / l_ref[2 * p + 1]
        o_ref[0, :, p * LANES:(p + 1) * LANES] = jnp.where(lo_rows, oa, ob).T.astype(o_ref.dtype)


def _dsa(proj3, vt4, rel_bias):
    B, S, _ = proj3.shape
    blk = DSA_BLK
    n_off = _dsa_n_off(blk)
    n_sel = min(TOPK_MAX, S // 4)
    bucket = jnp.asarray(_dsa_bucket_tiles(blk))
    n_chunks = S // blk
    assert S % blk == 0
    kern = functools.partial(_dsa_kernel, blk=blk, n_chunks=n_chunks, n_sel=n_sel, n_off=n_off)
    return pl.pallas_call(
        kern,
        grid=(B, S // blk),
        in_specs=[
            pl.BlockSpec((n_off, blk, blk), lambda b, i: (0, 0, 0)),
            pl.BlockSpec(memory_space=pltpu.SMEM),
            pl.BlockSpec((1, blk, ATT_W), lambda b, i: (b, i, COL_QA // ATT_W)),
            pl.BlockSpec((1, S, ATT_W), lambda b, i: (b, 0, COL_KA // ATT_W)),
            pl.BlockSpec((1, S // blk, ATT_W, blk), lambda b, i: (b, 0, 0, 0)),
            pl.BlockSpec((1, blk, ATT_W), lambda b, i: (b, i, COL_QI // ATT_W)),
            pl.BlockSpec((1, S, LANES), lambda b, i: (b, 0, COL_KK // LANES)),
            pl.BlockSpec((1, blk, LANES), lambda b, i: (b, i, COL_WI // LANES)),
        ],
        out_specs=pl.BlockSpec((1, blk, ATT_W), lambda b, i: (b, i, 0)),
        out_shape=jax.ShapeDtypeStruct((B, S, ATT_W), BF16),
        scratch_shapes=[
            pltpu.VMEM((S // blk, blk, blk), F32),
            pltpu.VMEM((A_HEADS, n_off, blk, blk), F32),
            pltpu.VMEM((A_HEADS, 1, blk), F32),
            pltpu.VMEM((A_HEADS, 1, blk), F32),
            pltpu.VMEM((A_HEADS, LANES, blk), F32),
            pltpu.VMEM((2, A_HEADS, blk, blk), F32),
            pltpu.VMEM((2, A_HEADS, 1, blk), F32),
        ],
        compiler_params=_params(("arbitrary", "arbitrary")),
    )(bucket, rel_bias, proj3, proj3, vt4, proj3, proj3, proj3)


def _sb_kernel(q_ref, k_ref, v_ref, o_ref, hl_ref, z_ref, *, t):
    i = pl.program_id(2)
    n = i + 1
    lane = lax.broadcasted_iota(I32, (t, LANES), 1)
    lo_half = lane < HALF
    q_m = []
    for pr in range(SB_PAIRS):
        q2 = q_ref[0, :, pr * LANES:(pr + 1) * LANES].astype(F32) * QK_SCALE
        q_m += [jnp.where(lo_half, q2, 0.0).astype(BF16), jnp.where(lo_half, 0.0, q2).astype(BF16)]
    heads = 2 * SB_PAIRS
    r = lax.broadcasted_iota(I32, (t, t), 0)
    cidx = lax.broadcasted_iota(I32, (t, t), 1)
    neg_from = jnp.where(r >= cidx, -1.0, 0.0).astype(BF16)
    diff = cidx - r

    def stage_terms(step, slot, diagonal):
        c0 = pl.multiple_of(jnp.maximum(i - step, 0) * t, t)
        if diagonal:
            keep = diff < 0
        for hh in range(heads):
            pr = hh // 2
            k2 = k_ref[0, pl.ds(c0, t), pr * LANES:(pr + 1) * LANES]
            z = lax.dot_general(q_m[hh], k2, NT_DIMS, preferred_element_type=F32)
            sp = jnp.maximum(z, 0.0) + jnp.log(1.0 + jnp.exp(-jnp.abs(z)))
            if diagonal:
                sp = jnp.where(keep, sp, 0.0)
                z = jnp.where(keep, z, NEG)
            hi = sp.astype(BF16)
            hl_ref[slot, 2 * hh] = hi
            hl_ref[slot, 2 * hh + 1] = (sp - hi.astype(F32)).astype(BF16)
            z_ref[slot, hh] = z

    def stage_apply(step, slot, carry):
        c0 = pl.multiple_of((i - step) * t, t)
        cum4 = jnp.dot(hl_ref[slot].reshape(2 * heads * t, t), neg_from, preferred_element_type=F32)
        out = []
        for hh in range(heads):
            pr = hh // 2
            v2 = v_ref[0, pl.ds(c0, t), pr * LANES:(pr + 1) * LANES]
            car, acc = carry[hh]
            cum = cum4[(2 * hh) * t:(2 * hh + 1) * t] + cum4[(2 * hh + 1) * t:(2 * hh + 2) * t]
            w = jnp.exp(z_ref[slot, hh] + cum + car)
            acc = acc + jnp.dot(w.astype(BF16), v2, preferred_element_type=F32)
            out.append((car + cum[:, 0:1], acc))
        return tuple(out)

    z1 = jnp.zeros((t, 1), F32)
    za = jnp.zeros((t, LANES), F32)
    stage_terms(0, 0, True)

    def pair_body(pp, carry):
        step = 2 * pp
        stage_terms(step + 1, 1, False)
        carry = stage_apply(step, 0, carry)
        stage_terms(step + 2, 0, False)
        return stage_apply(step + 1, 1, carry)

    carry = lax.fori_loop(0, n // 2, pair_body, ((z1, za),) * heads)
    carry = lax.cond(lax.rem(n, 2) == 1, lambda c: stage_apply(n - 1, 0, c), lambda c: c, carry)
    for pr in range(SB_PAIRS):
        o_ref[0, :, pr * LANES:(pr + 1) * LANES] = jnp.where(
            lo_half, carry[2 * pr][1], carry[2 * pr + 1][1]).astype(o_ref.dtype)


def _stick_breaking(proj3):
    B, S, _ = proj3.shape
    t = SB_T
    w = SB_PAIRS * LANES
    qb, kb, vb = COL_QB // w, COL_KB // w, COL_VB // w
    return pl.pallas_call(
        functools.partial(_sb_kernel, t=t),
        grid=(B, N_PAIRS // SB_PAIRS, S // t),
        in_specs=[
            pl.BlockSpec((1, t, w), lambda b, p, i: (b, i, qb + p)),
            pl.BlockSpec((1, S, w), lambda b, p, i: (b, 0, kb + p)),
            pl.BlockSpec((1, S, w), lambda b, p, i: (b, 0, vb + p)),
        ],
        out_specs=pl.BlockSpec((1, t, w), lambda b, p, i: (b, i, p)),
        out_shape=jax.ShapeDtypeStruct((B, S, ATT_W), BF16),
        scratch_shapes=[
            pltpu.VMEM((2, 4 * SB_PAIRS, t, t), BF16),
            pltpu.VMEM((2, 2 * SB_PAIRS, t, t), F32),
        ],
        compiler_params=_params(("arbitrary", "arbitrary", "arbitrary")),
    )(proj3, proj3, proj3)


def _layer_norm(r, g, b):
    mu = jnp.mean(r, axis=-1, keepdims=True)
    d = r - mu
    var = jnp.mean(d * d, axis=-1, keepdims=True)
    return d * lax.rsqrt(var + LN_EPS) * g + b


def _split_bf16(v):
    hi = v.astype(BF16)
    return hi, (v - hi.astype(F32)).astype(BF16)


def _merge_kernel(x_ref, ya_ref, yb_ref, wg_ref, wa_ref, wb_ref, wo_ref, g_ref, b_ref,
                  wr_ref, br_ref, h_ref, hp_ref, e_ref, p_ref, *, alpha, d):
    pa = jnp.dot(ya_ref[...], wa_ref[...], preferred_element_type=F32)
    pb = jnp.dot(yb_ref[...], wb_ref[...], preferred_element_type=F32)
    gates = jnp.dot(x_ref[...].astype(BF16), wg_ref[...], preferred_element_type=F32)
    merged = jax.nn.sigmoid(gates[:, :d]) * pa + jax.nn.sigmoid(gates[:, d:]) * pb
    m = jnp.dot(merged.astype(BF16), wo_ref[...], preferred_element_type=F32)
    h = _layer_norm(alpha * x_ref[...] + m, g_ref[...], b_ref[...])
    h_ref[...] = h
    bits = pltpu.bitcast(h.astype(BF16).astype(F32), jnp.uint32)
    hp_ref[...] = bits[:, :d // 2] | (bits[:, d // 2:] >> 16)

    h_hi, h_lo = _split_bf16(h)
    w_hi, w_lo = _split_bf16(wr_ref[...])
    logit = (lax.dot_general(w_hi, h_hi, NT_DIMS, preferred_element_type=F32)
             + lax.dot_general(w_hi, h_lo, NT_DIMS, preferred_element_type=F32)
             + lax.dot_general(w_lo, h_hi, NT_DIMS, preferred_element_type=F32)) + br_ref[...]
    eid = lax.broadcasted_iota(I32, logit.shape, 0)
    vals, ids = [], []
    for _ in range(TOP_K):
        mx = jnp.max(logit, axis=0, keepdims=True)
        am = jnp.min(jnp.where(logit == mx, eid, N_EXPERTS), axis=0, keepdims=True)
        vals.append(mx)
        ids.append(am)
        logit = jnp.where(eid == am, -jnp.inf, logit)
    ex = [jnp.exp(v - vals[0]) for v in vals]
    den = ex[0] + ex[1] + ex[2] + ex[3]
    for k in range(TOP_K):
        e_ref[k:k + 1, :] = ids[k]
        p_ref[k:k + 1, :] = ex[k] / den


def _merge(x2, ya, yb, wg, wa, wb, wo, g, b, wr_t, br, alpha):
    T, D = x2.shape
    tm = MERGE_TM
    row = lambda i: (i, 0)
    fixed = lambda i: (0, 0)
    return pl.pallas_call(
        functools.partial(_merge_kernel, alpha=alpha, d=D),
        grid=(T // tm,),
        in_specs=[
            pl.BlockSpec((tm, D), row),
            pl.BlockSpec((tm, ATT_W), row),
            pl.BlockSpec((tm, ATT_W), row),
            pl.BlockSpec((D, 2 * D), fixed),
            pl.BlockSpec((ATT_W, D), fixed),
            pl.BlockSpec((ATT_W, D), fixed),
            pl.BlockSpec((D, D), fixed),
            pl.BlockSpec((1, D), fixed),
            pl.BlockSpec((1, D), fixed),
            pl.BlockSpec((N_EXPERTS, D), fixed),
            pl.BlockSpec((N_EXPERTS, 1), fixed),
        ],
        out_specs=[
            pl.BlockSpec((tm, D), row),
            pl.BlockSpec((tm, D // 2), row),
            pl.BlockSpec((TOP_K, tm), lambda i: (0, i)),
            pl.BlockSpec((TOP_K, tm), lambda i: (0, i)),
        ],
        out_shape=[
            jax.ShapeDtypeStruct((T, D), F32),
            jax.ShapeDtypeStruct((T, D // 2), jnp.uint32),
            jax.ShapeDtypeStruct((TOP_K, T), I32),
            jax.ShapeDtypeStruct((TOP_K, T), F32),
        ],
        compiler_params=_params(("arbitrary",)),
    )(x2, ya, yb, wg, wa, wb, wo, g, b, wr_t, br)


def _sc_gather_rows(table, idx):
    n = idx.shape[0]
    d = table.shape[1]
    info = plsc.get_sparse_core_info()
    n_cores, n_sub = info.num_cores, info.num_subcores
    per_w = n // (n_cores * n_sub)
    c = SC_GATHER_BYTES // (d * table.dtype.itemsize)
    n_g = per_w // c
    assert n == per_w * n_cores * n_sub and per_w == n_g * c and n_g % 2 == 0 and n_g >= 2
    mesh = plsc.VectorSubcoreMesh(core_axis_name="c", subcore_axis_name="s")

    @functools.partial(
        pl.kernel, mesh=mesh, out_type=jax.ShapeDtypeStruct((n, d), table.dtype),
        scratch_types=[pltpu.VMEM((per_w,), I32), pltpu.VMEM((2, c, d), table.dtype),
                       pltpu.SemaphoreType.DMA((2,)), pltpu.SemaphoreType.DMA((2,))])
    def gather_kernel(table_hbm, idx_hbm, out_hbm, idx_v, rows_v, gsem, wsem):
        base = (lax.axis_index("s") * n_cores + lax.axis_index("c")) * per_w
        pltpu.sync_copy(idx_hbm.at[pl.ds(base, per_w)], idx_v)

        def gather(g, b):
            return pltpu.make_async_copy(table_hbm.at[idx_v.at[pl.ds(g * c, c)]], rows_v.at[b],
                                         gsem.at[b])

        def write(g, b):
            return pltpu.make_async_copy(rows_v.at[b], out_hbm.at[pl.ds(base + g * c, c)], wsem.at[b])

        gather(0, 0).start()

        @pl.loop(0, n_g, step=2)
        def _ring(g0):
            for b in range(2):
                g = g0 + b

                @pl.when(g + 1 < n_g)
                def _next():
                    @pl.when(g >= 1)
                    def _buffer_free():
                        write(g - 1, 1 - b).wait()
                    gather(g + 1, 1 - b).start()

                gather(g, b).wait()
                write(g, b).start()

        write(n_g - 2, 0).wait()
        write(n_g - 1, 1).wait()

    return gather_kernel(table, idx)


def _moe_kernel(blk_e_ref, nused_ref, x_ref, wgu_ref, bgu_ref, wdn_ref, bdn_ref, o_ref,
                wgu_s, wdn_s, *, f):
    i = pl.program_id(0)
    nused = nused_ref[0]

    @pl.when(i < nused)
    def _compute():
        changed = jnp.logical_or(i == 0, blk_e_ref[i] != blk_e_ref[jnp.maximum(i - 1, 0)])

        @pl.when(changed)
        def _cast_weights():
            wgu_s[...] = wgu_ref[0].astype(BF16)
            wdn_s[...] = wdn_ref[0].astype(BF16)

        xw = x_ref[...]
        x = jnp.concatenate([pltpu.bitcast(xw & jnp.uint32(0xFFFF0000), F32),
                             pltpu.bitcast(xw << 16, F32)], axis=1).astype(BF16)
        hgu = jnp.dot(x, wgu_s[...], preferred_element_type=F32) + bgu_ref[0]
        a = jnp.minimum(hgu[:, :f], SWIGLU_LIMIT)
        u = jnp.clip(hgu[:, f:], -SWIGLU_LIMIT, SWIGLU_LIMIT)
        glu = a * jax.nn.sigmoid(a * SWIGLU_ALPHA)
        o_ref[...] = jnp.dot(((u + 1.0) * glu).astype(BF16), wdn_s[...],
                             preferred_element_type=F32) + bdn_ref[0]

    @pl.when(i >= nused)
    def _unused_block():
        o_ref[...] = jnp.zeros(o_ref.shape, o_ref.dtype)


def _moe_ffn(xs, blk_e, nused, w_gu, b_gu, w_dn, b_dn):
    P = xs.shape[0]
    E, D, F2 = w_gu.shape
    f = F2 // 2
    blk = MOE_BLK
    nb = P // blk
    used_block = lambda i, be, nu: (jnp.minimum(i, nu[0] - 1), 0)
    grid_spec = pltpu.PrefetchScalarGridSpec(
        num_scalar_prefetch=2,
        grid=(nb,),
        in_specs=[
            pl.BlockSpec((blk, D // 2), used_block),
            pl.BlockSpec((1, D, F2), lambda i, be, nu: (be[i], 0, 0)),
            pl.BlockSpec((1, 1, F2), lambda i, be, nu: (be[i], 0, 0)),
            pl.BlockSpec((1, f, D), lambda i, be, nu: (be[i], 0, 0)),
            pl.BlockSpec((1, 1, D), lambda i, be, nu: (be[i], 0, 0)),
        ],
        out_specs=pl.BlockSpec((blk, D), lambda i, be, nu: (i, 0)),
        scratch_shapes=[
            pltpu.VMEM((D, F2), BF16),
            pltpu.VMEM((f, D), BF16),
        ],
    )
    return pl.pallas_call(
        functools.partial(_moe_kernel, f=f),
        grid_spec=grid_spec,
        out_shape=jax.ShapeDtypeStruct((P, D), F32),
        compiler_params=_params(("arbitrary",)),
    )(blk_e, nused, xs, w_gu, b_gu.reshape(E, 1, F2), w_dn, b_dn.reshape(E, 1, D))


def _comb_kernel(h_ref, p_ref, y_ref, g_ref, b_ref, o_ref, *, alpha):
    gate = p_ref[...]
    fsum = ((y_ref[0] * gate[:, 0:1] + y_ref[1] * gate[:, 1:2])
            + (y_ref[2] * gate[:, 2:3] + y_ref[3] * gate[:, 3:4]))
    o_ref[...] = _layer_norm(alpha * h_ref[...] + fsum, g_ref[...], b_ref[...])


def _combine(h, y4, top_p, g, b, alpha):
    T, D = h.shape
    tm = COMB_TM
    return pl.pallas_call(
        functools.partial(_comb_kernel, alpha=alpha),
        grid=(T // tm,),
        in_specs=[
            pl.BlockSpec((tm, D), lambda i: (i, 0)),
            pl.BlockSpec((tm, TOP_K), lambda i: (i, 0)),
            pl.BlockSpec((TOP_K, tm, D), lambda i: (0, i, 0)),
            pl.BlockSpec((1, D), lambda i: (0, 0)),
            pl.BlockSpec((1, D), lambda i: (0, 0)),
        ],
        out_specs=pl.BlockSpec((tm, D), lambda i: (i, 0)),
        out_shape=jax.ShapeDtypeStruct((T, D), F32),
        compiler_params=_params(("arbitrary",)),
    )(h, top_p.T, y4, g, b)


def _route(top_e, blk):
    K, T = top_e.shape
    N = K * T
    flat_e = top_e.reshape(N)
    experts = jnp.arange(N_EXPERTS, dtype=I32)
    order = jnp.argsort(flat_e, stable=True).astype(I32)
    inv = jnp.argsort(order).astype(I32)
    onehot = flat_e[:, None] == experts[None, :]
    counts = jnp.sum(onehot, axis=0, dtype=I32)
    padded = (counts + blk - 1) // blk * blk
    pends = jnp.cumsum(padded)
    offs = jnp.cumsum(counts) - counts
    shift = (pends - padded) - offs
    pos = inv + jnp.sum(jnp.where(onehot, shift[None, :], 0), axis=1, dtype=I32)
    P = N + N_EXPERTS * blk
    nb = P // blk
    blk_start = jnp.arange(nb, dtype=I32) * blk
    blk_e = jnp.minimum(jnp.sum(pends[None, :] <= blk_start[:, None], axis=1, dtype=I32), N_EXPERTS - 1)
    j = (blk_start - shift[blk_e])[:, None] + jnp.arange(blk, dtype=I32)[None, :]
    valid = j < (offs + counts)[blk_e][:, None]
    src = order[jnp.clip(j, 0, N - 1)]
    row_tok = jnp.where(valid, src % T, j % T).reshape(P)
    nused = (pends[-1:] // blk).astype(I32)
    return blk_e, nused, row_tok, pos.reshape(K, T)


def _projection_weights(w_in_l):
    sizes = (ATT_W, ATT_W, ATT_W, IDX_HEADS * IDX_DIM, IDX_DIM, IDX_HEADS, ATT_W, ATT_W, ATT_W)
    offs = np.concatenate([[0], np.cumsum(sizes)])
    qa, ka, va, qi, ki, wi, qb, kb, vb = (w_in_l[:, offs[n]:offs[n + 1]] for n in range(9))
    pad_wi = jnp.zeros((w_in_l.shape[0], LANES - IDX_HEADS), w_in_l.dtype)
    w_att = jnp.concatenate([qa, ka, qi, qb, kb, vb, ki, ki, wi, pad_wi], axis=1).astype(BF16)
    w_va_t = va.T.astype(BF16)
    w_gate = w_in_l[:, offs[9]:].astype(BF16)
    return w_att, w_va_t, w_gate


def kernel(x, w_in, w_branch_a, w_branch_b, w_out, rel_bias, ln1_g, ln1_b, w_router, b_router,
           w_gate_up, b_gate_up, w_down, b_down, ln2_g, ln2_b):
    B, S, D = x.shape
    depth = w_in.shape[0]
    alpha = (2 * depth) ** 0.25
    T = B * S
    h = x.reshape(T, D)
    for l in range(depth):
        w_att, w_va_t, w_gate = _projection_weights(w_in[l])
        proj, vt = _projection(h, w_att, w_va_t, min(T, 1024), PROJ_TN, DSA_BLK)
        proj = proj.reshape(B, S, ATT_COLS)
        vt = vt.reshape(B, S // DSA_BLK, ATT_W, DSA_BLK)
        ya = _dsa(proj, vt, rel_bias).reshape(T, ATT_W)
        yb = _stick_breaking(proj).reshape(T, ATT_W)
        h1, h1_packed, top_e, top_p = _merge(
            h, ya, yb, w_gate, w_branch_a[l].astype(BF16), w_branch_b[l].astype(BF16),
            w_out[l].astype(BF16), ln1_g[l].reshape(1, D), ln1_b[l].reshape(1, D),
            w_router[l].T, b_router[l].reshape(N_EXPERTS, 1), alpha)
        blk_e, nused, row_tok, pos = _route(top_e, MOE_BLK)
        xs = _sc_gather_rows(h1_packed, row_tok)
        ys = _moe_ffn(xs, blk_e, nused, w_gate_up[l], b_gate_up[l], w_down[l], b_down[l])
        y4 = _sc_gather_rows(ys, pos.reshape(TOP_K * T)).reshape(TOP_K, T, D)
        h = _combine(h1, y4, top_p, ln2_g[l].reshape(1, D), ln2_b[l].reshape(1, D), alpha)
    return h.reshape(B, S, D)
```

```python
import functools
import math

import numpy as np
import jax
import jax.numpy as jnp
from jax import lax
from jax.experimental import pallas as pl
from jax.experimental.pallas import tpu as pltpu
from jax.experimental.pallas import tpu_sc as plsc

F32 = jnp.float32
BF16 = jnp.bfloat16
I32 = jnp.int32

A_HEADS = 8
HEAD_DIM = 64
ATT_W = A_HEADS * HEAD_DIM
IDX_HEADS = 8
IDX_DIM = 64
IDX_SCALE = (IDX_HEADS * IDX_DIM) ** -0.5
TOPK_MAX = 256
N_BUCKETS = 32
MAX_DISTANCE = 128
N_EXPERTS = 32
TOP_K = 4
SWIGLU_LIMIT = 7.0
SWIGLU_ALPHA = 1.702
LN_EPS = 1e-5
QK_SCALE = HEAD_DIM ** -0.5

LANES = 128
SUBLANES = 8
HALF = LANES // 2
N_PAIRS = A_HEADS // 2
VMEM_LIMIT = 56 * 1024 * 1024

DSA_BLK = 256
SB_T = 256
SB_PAIRS = 4
MERGE_TM = 512
MOE_BLK = 512
COMB_TM = 512
SC_GATHER_BYTES = 128 * 1024
REDUCE_CHAINS = 4
BISECT_CAP = 24
NEG = -1e30

COL_QA, COL_KA, COL_QI, COL_QB, COL_KB, COL_VB = (g * ATT_W for g in range(6))
COL_KK = 6 * ATT_W
COL_WI = COL_KK + LANES
ATT_COLS = COL_WI + LANES
PROJ_TN = ATT_COLS // 2

NT_DIMS = (((1,), (1,)), ((), ()))


def _params(sem, vmem=VMEM_LIMIT):
    return pltpu.CompilerParams(dimension_semantics=sem, vmem_limit_bytes=vmem)


def _proj_kernel(x_ref, w_ref, wt_ref, o_ref, ot_ref, xb_ref, *, tt):
    @pl.when(pl.program_id(1) == 0)
    def _row_tile_start():
        xb_ref[...] = x_ref[...].astype(BF16)
        for r in range(ot_ref.shape[0]):
            ot_ref[r] = lax.dot_general(wt_ref[...], xb_ref[r * tt:(r + 1) * tt, :], NT_DIMS,
                                        preferred_element_type=F32).astype(ot_ref.dtype)

    o_ref[...] = jnp.dot(xb_ref[...], w_ref[...], preferred_element_type=F32).astype(o_ref.dtype)


def _projection(x, w, w_t, tm, tn, tt):
    M, K = x.shape
    N = w.shape[1]
    Nt = w_t.shape[0]
    return pl.pallas_call(
        functools.partial(_proj_kernel, tt=tt),
        grid=(M // tm, N // tn),
        in_specs=[pl.BlockSpec((tm, K), lambda i, j: (i, 0)),
                  pl.BlockSpec((K, tn), lambda i, j: (0, j)),
                  pl.BlockSpec((Nt, K), lambda i, j: (0, 0))],
        out_specs=[pl.BlockSpec((tm, tn), lambda i, j: (i, j)),
                   pl.BlockSpec((tm // tt, Nt, tt), lambda i, j: (i, 0, 0))],
        out_shape=[jax.ShapeDtypeStruct((M, N), BF16),
                   jax.ShapeDtypeStruct((M // tt, Nt, tt), BF16)],
        scratch_shapes=[pltpu.VMEM((tm, K), BF16)],
        compiler_params=_params(("arbitrary", "arbitrary")),
    )(x, w, w_t)


def _t5_bucket_np(n):
    n = np.maximum(n, 0)
    max_exact = N_BUCKETS // 2
    nf = np.maximum(n, 1).astype(np.float32)
    large = max_exact + (np.log(nf / max_exact) / math.log(MAX_DISTANCE / max_exact)
                         * (N_BUCKETS - max_exact)).astype(np.int32)
    large = np.minimum(large, N_BUCKETS - 1)
    return np.where(n < max_exact, n, large).astype(np.int32)


def _dsa_n_off(blk):
    return 2 + -(-MAX_DISTANCE // blk)


def _dsa_bucket_tiles(blk):
    n_off = _dsa_n_off(blk)
    j = np.arange(blk)[None, :, None]
    i = np.arange(blk)[None, None, :]
    o = np.arange(n_off)[:, None, None]
    return _t5_bucket_np(i - j + blk * (n_off - 1 - o))


def _dsa_kernel(bucket_ref, relb_ref, q_ref, k_ref, vt_ref, qi_ref, kk_ref, wi_ref, o_ref,
                sc_ref, bias_ref, m_ref, l_ref, acc_ref, st_ref, mx_ref,
                *, blk, n_chunks, n_sel, n_off):
    b = pl.program_id(0)
    i = pl.program_id(1)
    q0 = i * blk
    nck = i + 1
    groups = blk // SUBLANES

    @pl.when(jnp.logical_and(b == 0, i == 0))
    def _build_bias():
        def head_body(h, _):
            for o in range(n_off):
                for rb in range(blk // LANES):
                    for cb in range(blk // LANES):
                        rs = slice(rb * LANES, (rb + 1) * LANES)
                        cs = slice(cb * LANES, (cb + 1) * LANES)
                        bk = bucket_ref[o, rs, cs]

                        def bucket_body(n, acc):
                            return jnp.where(bk == n, relb_ref[n, h], acc)

                        bias_ref[h, o, rs, cs] = lax.fori_loop(
                            0, N_BUCKETS, bucket_body, jnp.zeros((LANES, LANES), F32))
            return 0

        lax.fori_loop(0, A_HEADS, head_body, 0)

    lane = lax.broadcasted_iota(I32, (blk, LANES), 1)
    lo_half = lane < HALF
    krow = lax.broadcasted_iota(I32, (blk, blk), 0)
    qpos = q0 + lax.broadcasted_iota(I32, (1, blk), 1)

    def pair_split(ref, scale):
        out = []
        for p in range(N_PAIRS):
            v = ref[0, :, p * LANES:(p + 1) * LANES].astype(F32)
            if scale != 1.0:
                v = v * scale
            out.append(jnp.where(lo_half, v, 0.0).astype(BF16))
            out.append(jnp.where(lo_half, 0.0, v).astype(BF16))
        return out

    wi_t = wi_ref[0].astype(F32).T
    wrow = [wi_t[h:h + 1, :] * IDX_SCALE for h in range(IDX_HEADS)]
    qi_m = pair_split(qi_ref, 1.0)

    def score_chunk(c, _):
        c0 = pl.multiple_of(c * blk, blk)
        kk = kk_ref[0, pl.ds(c0, blk), :]
        acc = jnp.zeros((blk, blk), F32)
        for h in range(IDX_HEADS):
            s = lax.dot_general(kk, qi_m[h], NT_DIMS, preferred_element_type=F32)
            acc = acc + wrow[h] * jnp.maximum(s, 0.0)
        sc_ref[c] = jnp.where(c0 + krow <= qpos, acc, -jnp.inf)
        return 0

    lax.fori_loop(0, nck, score_chunk, 0)

    kt = jnp.minimum(qpos + 1, n_sel).astype(F32)

    def fold(fn, init):
        def body(c, acc):
            return fn(acc, sc_ref[c])
        return lax.fori_loop(0, nck, body, init)

    def part(x, op):
        y = op(x.reshape(REDUCE_CHAINS, groups // REDUCE_CHAINS, SUBLANES, blk), axis=1)
        return op(y, axis=0)

    def fin(x, op):
        return op(x, axis=0, keepdims=True)

    zeros8 = jnp.zeros((SUBLANES, blk), F32)
    pinf8 = jnp.full((SUBLANES, blk), jnp.inf, F32)

    def count_ge(th):
        return fin(fold(lambda a, s: a + part(jnp.where(s >= th, 1.0, 0.0), jnp.sum), zeros8),
                   jnp.sum)

    mn, mx = fold(lambda a, s: (
        jnp.minimum(a[0], part(jnp.where(s == -jnp.inf, jnp.inf, s), jnp.min)),
        jnp.maximum(a[1], part(s, jnp.max))), (pinf8, -pinf8))
    rmin = fin(mn, jnp.min)
    rmax = fin(mx, jnp.max)

    def bis_cond(st):
        it, lo, hi, clo = st
        return jnp.logical_and(it < BISECT_CAP, jnp.max(jnp.abs(clo - kt)) > 0.0)

    def halve(lo, hi, clo):
        mid = 0.5 * lo + 0.5 * hi
        c = count_ge(mid)
        active = clo != kt
        up = jnp.logical_and(active, c >= kt)
        dn = jnp.logical_and(active, c < kt)
        return jnp.where(up, mid, lo), jnp.where(dn, mid, hi), jnp.where(up, c, clo)

    def bis_body(st):
        it, lo, hi, clo = st
        return (it + 2,) + halve(*halve(lo, hi, clo))

    _, lo, _, _ = lax.while_loop(
        bis_cond, bis_body, (jnp.int32(0), rmin, rmax + 1.0, (qpos + 1).astype(F32)))

    def stats(lo_):
        a_ = fin(fold(lambda a, s: jnp.minimum(a, part(jnp.where(s >= lo_, s, jnp.inf), jnp.min)),
                      pinf8), jnp.min)
        cg, ct, nx = fold(
            lambda a, s: (a[0] + part(jnp.where(s > a_, 1.0, 0.0), jnp.sum),
                               a[1] + part(jnp.where(s == a_, 1.0, 0.0), jnp.sum),
                               jnp.minimum(a[2], part(jnp.where(s > a_, s, jnp.inf), jnp.min))),
            (zeros8, zeros8, pinf8))
        return a_, fin(cg, jnp.sum), fin(ct, jnp.sum), fin(nx, jnp.min)

    def fin_cond(st):
        return st[0]

    def fin_body(st):
        _, lo_, _, _ = st
        a_, cgt_, nt_, nxt_ = stats(lo_)
        bad = cgt_ >= kt
        return (jnp.max(jnp.where(bad, 1.0, 0.0)) > 0.0, jnp.where(bad, nxt_, a_), cgt_, nt_)

    _, a, cgt, nties = lax.while_loop(fin_cond, fin_body, (jnp.bool_(True), lo, kt, kt))
    need = kt - cgt
    excess = jnp.max(jnp.where(nties > need, 1.0, 0.0)) > 0.0

    def mask_plain():
        def body(c, _):
            sc_ref[c] = jnp.where(sc_ref[c] >= a, 0.0, NEG)
            return 0
        lax.fori_loop(0, nck, body, 0)

    def mask_ties():
        upto = (krow >= lax.broadcasted_iota(I32, (blk, blk), 1)).astype(BF16)

        def body(c, seen):
            s = sc_ref[c]
            tie = s == a
            rank = jnp.dot(upto, jnp.where(tie, 1.0, 0.0).astype(BF16),
                           preferred_element_type=F32) + seen
            sel = jnp.logical_or(s > a, jnp.logical_and(tie, rank <= need))
            sc_ref[c] = jnp.where(sel, 0.0, NEG)
            return rank[blk - 1:blk, :]

        lax.fori_loop(0, nck, body, jnp.zeros((1, blk), F32))

    lax.cond(excess, mask_ties, mask_plain)

    m_ref[...] = jnp.full(m_ref.shape, NEG, F32)
    l_ref[...] = jnp.zeros(l_ref.shape, F32)
    acc_ref[...] = jnp.zeros(acc_ref.shape, F32)
    q_m = pair_split(q_ref, QK_SCALE)

    def stage_logits(c, slot):
        c = jnp.minimum(c, n_chunks - 1)
        c0 = pl.multiple_of(c * blk, blk)
        madd = sc_ref[c]
        o_idx = jnp.clip(c - i + (n_off - 1), 0, n_off - 1)
        for p in range(N_PAIRS):
            k2 = k_ref[0, pl.ds(c0, blk), p * LANES:(p + 1) * LANES]
            for hh in range(2):
                h = 2 * p + hh
                s = lax.dot_general(k2, q_m[h], NT_DIMS, preferred_element_type=F32)
                s = s + bias_ref[h, o_idx] + madd
                st_ref[slot, h] = s
                mx_ref[slot, h] = fin(part(s, jnp.max), jnp.max)

    def stage_values(c, slot):
        for p in range(N_PAIRS):
            vt2 = vt_ref[0, c, p * LANES:(p + 1) * LANES, :]
            for hh in range(2):
                h = 2 * p + hh
                m_old = m_ref[h]
                m_new = jnp.maximum(m_old, mx_ref[slot, h])
                alpha = jnp.exp(m_old - m_new)
                pexp = jnp.exp(st_ref[slot, h] - m_new)
                l_ref[h] = alpha * l_ref[h] + fin(part(pexp, jnp.sum), jnp.sum)
                acc_ref[h] = alpha * acc_ref[h] + jnp.dot(vt2, pexp.astype(BF16),
                                                          preferred_element_type=F32)
                m_ref[h] = m_new

    stage_logits(0, 0)

    def att_pair(pp, _):
        c = 2 * pp
        stage_logits(c + 1, 1)
        stage_values(c, 0)
        stage_logits(c + 2, 0)
        stage_values(c + 1, 1)
        return 0

    lax.fori_loop(0, nck // 2, att_pair, 0)

    @pl.when(lax.rem(nck, 2) == 1)
    def _last_chunk():
        stage_values(nck - 1, 0)

    lo_rows = lax.broadcasted_iota(I32, (LANES, blk), 0) < HALF
    for p in range(N_PAIRS):
        oa = acc_ref[2 * p] / l_ref[2 * p]
        ob = acc_ref[2 * p + 1] / l_ref[2 * p + 1]
        o_ref[0, :, p * LANES:(p + 1) * LANES] = jnp.where(lo_rows, oa, ob).T.astype(o_ref.dtype)


def _dsa(proj3, vt4, rel_bias):
    B, S, _ = proj3.shape
    blk = DSA_BLK
    n_off = _dsa_n_off(blk)
    n_sel = min(TOPK_MAX, S // 4)
    bucket = jnp.asarray(_dsa_bucket_tiles(blk))
    n_chunks = S // blk
    assert S % blk == 0
    kern = functools.partial(_dsa_kernel, blk=blk, n_chunks=n_chunks, n_sel=n_sel, n_off=n_off)
    return pl.pallas_call(
        kern,
        grid=(B, S // blk),
        in_specs=[
            pl.BlockSpec((n_off, blk, blk), lambda b, i: (0, 0, 0)),
            pl.BlockSpec(memory_space=pltpu.SMEM),
            pl.BlockSpec((1, blk, ATT_W), lambda b, i: (b, i, COL_QA // ATT_W)),
            pl.BlockSpec((1, S, ATT_W), lambda b, i: (b, 0, COL_KA // ATT_W)),
            pl.BlockSpec((1, S // blk, ATT_W, blk), lambda b, i: (b, 0, 0, 0)),
            pl.BlockSpec((1, blk, ATT_W), lambda b, i: (b, i, COL_QI // ATT_W)),
            pl.BlockSpec((1, S, LANES), lambda b, i: (b, 0, COL_KK // LANES)),
            pl.BlockSpec((1, blk, LANES), lambda b, i: (b, i, COL_WI // LANES)),
        ],
        out_specs=pl.BlockSpec((1, blk, ATT_W), lambda b, i: (b, i, 0)),
        out_shape=jax.ShapeDtypeStruct((B, S, ATT_W), BF16),
        scratch_shapes=[
            pltpu.VMEM((S // blk, blk, blk), F32),
            pltpu.VMEM((A_HEADS, n_off, blk, blk), F32),
            pltpu.VMEM((A_HEADS, 1, blk), F32),
            pltpu.VMEM((A_HEADS, 1, blk), F32),
            pltpu.VMEM((A_HEADS, LANES, blk), F32),
            pltpu.VMEM((2, A_HEADS, blk, blk), F32),
            pltpu.VMEM((2, A_HEADS, 1, blk), F32),
        ],
        compiler_params=_params(("arbitrary", "arbitrary")),
    )(bucket, rel_bias, proj3, proj3, vt4, proj3, proj3, proj3)


def _sb_kernel(q_ref, k_ref, v_ref, o_ref, hl_ref, z_ref, *, t):
    i = pl.program_id(2)
    n = i + 1
    lane = lax.broadcasted_iota(I32, (t, LANES), 1)
    lo_half = lane < HALF
    q_m = []
    for pr in range(SB_PAIRS):
        q2 = q_ref[0, :, pr * LANES:(pr + 1) * LANES].astype(F32) * QK_SCALE
        q_m += [jnp.where(lo_half, q2, 0.0).astype(BF16), jnp.where(lo_half, 0.0, q2).astype(BF16)]
    heads = 2 * SB_PAIRS
    r = lax.broadcasted_iota(I32, (t, t), 0)
    cidx = lax.broadcasted_iota(I32, (t, t), 1)
    neg_from = jnp.where(r >= cidx, -1.0, 0.0).astype(BF16)
    diff = cidx - r

    def stage_terms(step, slot, diagonal):
        c0 = pl.multiple_of(jnp.maximum(i - step, 0) * t, t)
        if diagonal:
            keep = diff < 0
        for hh in range(heads):
            pr = hh // 2
            k2 = k_ref[0, pl.ds(c0, t), pr * LANES:(pr + 1) * LANES]
            z = lax.dot_general(q_m[hh], k2, NT_DIMS, preferred_element_type=F32)
            sp = jnp.maximum(z, 0.0) + jnp.log(1.0 + jnp.exp(-jnp.abs(z)))
            if diagonal:
                sp = jnp.where(keep, sp, 0.0)
                z = jnp.where(keep, z, NEG)
            hi = sp.astype(BF16)
            hl_ref[slot, 2 * hh] = hi
            hl_ref[slot, 2 * hh + 1] = (sp - hi.astype(F32)).astype(BF16)
            z_ref[slot, hh] = z

    def stage_apply(step, slot, carry):
        c0 = pl.multiple_of((i - step) * t, t)
        cum4 = jnp.dot(hl_ref[slot].reshape(2 * heads * t, t), neg_from, preferred_element_type=F32)
        out = []
        for hh in range(heads):
            pr = hh // 2
            v2 = v_ref[0, pl.ds(c0, t), pr * LANES:(pr + 1) * LANES]
            car, acc = carry[hh]
            cum = cum4[(2 * hh) * t:(2 * hh + 1) * t] + cum4[(2 * hh + 1) * t:(2 * hh + 2) * t]
            w = jnp.exp(z_ref[slot, hh] + cum + car)
            acc = acc + jnp.dot(w.astype(BF16), v2, preferred_element_type=F32)
            out.append((car + cum[:, 0:1], acc))
        return tuple(out)

    z1 = jnp.zeros((t, 1), F32)
    za = jnp.zeros((t, LANES), F32)
    stage_terms(0, 0, True)

    def pair_body(pp, carry):
        step = 2 * pp
        stage_terms(step + 1, 1, False)
        carry = stage_apply(step, 0, carry)
        stage_terms(step + 2, 0, False)
        return stage_apply(step + 1, 1, carry)

    carry = lax.fori_loop(0, n // 2, pair_body, ((z1, za),) * heads)
    carry = lax.cond(lax.rem(n, 2) == 1, lambda c: stage_apply(n - 1, 0, c), lambda c: c, carry)
    for pr in range(SB_PAIRS):
        o_ref[0, :, pr * LANES:(pr + 1) * LANES] = jnp.where(
            lo_half, carry[2 * pr][1], carry[2 * pr + 1][1]).astype(o_ref.dtype)


def _stick_breaking(proj3):
    B, S, _ = proj3.shape
    t = SB_T
    w = SB_PAIRS * LANES
    qb, kb, vb = COL_QB // w, COL_KB // w, COL_VB // w
    return pl.pallas_call(
        functools.partial(_sb_kernel, t=t),
        grid=(B, N_PAIRS // SB_PAIRS, S // t),
        in_specs=[
            pl.BlockSpec((1, t, w), lambda b, p, i: (b, i, qb + p)),
            pl.BlockSpec((1, S, w), lambda b, p, i: (b, 0, kb + p)),
            pl.BlockSpec((1, S, w), lambda b, p, i: (b, 0, vb + p)),
        ],
        out_specs=pl.BlockSpec((1, t, w), lambda b, p, i: (b, i, p)),
        out_shape=jax.ShapeDtypeStruct((B, S, ATT_W), BF16),
        scratch_shapes=[
            pltpu.VMEM((2, 4 * SB_PAIRS, t, t), BF16),
            pltpu.VMEM((2, 2 * SB_PAIRS, t, t), F32),
        ],
        compiler_params=_params(("arbitrary", "arbitrary", "arbitrary")),
    )(proj3, proj3, proj3)


def _layer_norm(r, g, b):
    mu = jnp.mean(r, axis=-1, keepdims=True)
    d = r - mu
    var = jnp.mean(d * d, axis=-1, keepdims=True)
    return d * lax.rsqrt(var + LN_EPS) * g + b


def _split_bf16(v):
    hi = v.astype(BF16)
    return hi, (v - hi.astype(F32)).astype(BF16)


def _merge_kernel(x_ref, ya_ref, yb_ref, wg_ref, wa_ref, wb_ref, wo_ref, g_ref, b_ref,
                  wr_ref, br_ref, h_ref, hp_ref, e_ref, p_ref, *, alpha, d):
    pa = jnp.dot(ya_ref[...], wa_ref[...], preferred_element_type=F32)
    pb = jnp.dot(yb_ref[...], wb_ref[...], preferred_element_type=F32)
    gates = jnp.dot(x_ref[...].astype(BF16), wg_ref[...], preferred_element_type=F32)
    merged = jax.nn.sigmoid(gates[:, :d]) * pa + jax.nn.sigmoid(gates[:, d:]) * pb
    m = jnp.dot(merged.astype(BF16), wo_ref[...], preferred_element_type=F32)
    h = _layer_norm(alpha * x_ref[...] + m, g_ref[...], b_ref[...])
    h_ref[...] = h
    bits = pltpu.bitcast(h.astype(BF16).astype(F32), jnp.uint32)
    hp_ref[...] = bits[:, :d // 2] | (bits[:, d // 2:] >> 16)

    h_hi, h_lo = _split_bf16(h)
    w_hi, w_lo = _split_bf16(wr_ref[...])
    logit = (lax.dot_general(w_hi, h_hi, NT_DIMS, preferred_element_type=F32)
             + lax.dot_general(w_hi, h_lo, NT_DIMS, preferred_element_type=F32)
             + lax.dot_general(w_lo, h_hi, NT_DIMS, preferred_element_type=F32)) + br_ref[...]
    eid = lax.broadcasted_iota(I32, logit.shape, 0)
    vals, ids = [], []
    for _ in range(TOP_K):
        mx = jnp.max(logit, axis=0, keepdims=True)
        am = jnp.min(jnp.where(logit == mx, eid, N_EXPERTS), axis=0, keepdims=True)
        vals.append(mx)
        ids.append(am)
        logit = jnp.where(eid == am, -jnp.inf, logit)
    ex = [jnp.exp(v - vals[0]) for v in vals]
    den = ex[0] + ex[1] + ex[2] + ex[3]
    for k in range(TOP_K):
        e_ref[k:k + 1, :] = ids[k]
        p_ref[k:k + 1, :] = ex[k] / den


def _merge(x2, ya, yb, wg, wa, wb, wo, g, b, wr_t, br, alpha):
    T, D = x2.shape
    tm = MERGE_TM
    row = lambda i: (i, 0)
    fixed = lambda i: (0, 0)
    return pl.pallas_call(
        functools.partial(_merge_kernel, alpha=alpha, d=D),
        grid=(T // tm,),
        in_specs=[
            pl.BlockSpec((tm, D), row),
            pl.BlockSpec((tm, ATT_W), row),
            pl.BlockSpec((tm, ATT_W), row),
            pl.BlockSpec((D, 2 * D), fixed),
            pl.BlockSpec((ATT_W, D), fixed),
            pl.BlockSpec((ATT_W, D), fixed),
            pl.BlockSpec((D, D), fixed),
            pl.BlockSpec((1, D), fixed),
            pl.BlockSpec((1, D), fixed),
            pl.BlockSpec((N_EXPERTS, D), fixed),
            pl.BlockSpec((N_EXPERTS, 1), fixed),
        ],
        out_specs=[
            pl.BlockSpec((tm, D), row),
            pl.BlockSpec((tm, D // 2), row),
            pl.BlockSpec((TOP_K, tm), lambda i: (0, i)),
            pl.BlockSpec((TOP_K, tm), lambda i: (0, i)),
        ],
        out_shape=[
            jax.ShapeDtypeStruct((T, D), F32),
            jax.ShapeDtypeStruct((T, D // 2), jnp.uint32),
            jax.ShapeDtypeStruct((TOP_K, T), I32),
            jax.ShapeDtypeStruct((TOP_K, T), F32),
        ],
        compiler_params=_params(("arbitrary",)),
    )(x2, ya, yb, wg, wa, wb, wo, g, b, wr_t, br)


def _sc_gather_rows(table, idx):
    n = idx.shape[0]
    d = table.shape[1]
    info = plsc.get_sparse_core_info()
    n_cores, n_sub = info.num_cores, info.num_subcores
    per_w = n // (n_cores * n_sub)
    c = SC_GATHER_BYTES // (d * table.dtype.itemsize)
    n_g = per_w // c
    assert n == per_w * n_cores * n_sub and per_w == n_g * c and n_g % 2 == 0 and n_g >= 2
    mesh = plsc.VectorSubcoreMesh(core_axis_name="c", subcore_axis_name="s")

    @functools.partial(
        pl.kernel, mesh=mesh, out_type=jax.ShapeDtypeStruct((n, d), table.dtype),
        scratch_types=[pltpu.VMEM((per_w,), I32), pltpu.VMEM((2, c, d), table.dtype),
                       pltpu.SemaphoreType.DMA((2,)), pltpu.SemaphoreType.DMA((2,))])
    def gather_kernel(table_hbm, idx_hbm, out_hbm, idx_v, rows_v, gsem, wsem):
        base = (lax.axis_index("s") * n_cores + lax.axis_index("c")) * per_w
        pltpu.sync_copy(idx_hbm.at[pl.ds(base, per_w)], idx_v)

        def gather(g, b):
            return pltpu.make_async_copy(table_hbm.at[idx_v.at[pl.ds(g * c, c)]], rows_v.at[b],
                                         gsem.at[b])

        def write(g, b):
            return pltpu.make_async_copy(rows_v.at[b], out_hbm.at[pl.ds(base + g * c, c)], wsem.at[b])

        gather(0, 0).start()

        @pl.loop(0, n_g, step=2)
        def _ring(g0):
            for b in range(2):
                g = g0 + b

                @pl.when(g + 1 < n_g)
                def _next():
                    @pl.when(g >= 1)
                    def _buffer_free():
                        write(g - 1, 1 - b).wait()
                    gather(g + 1, 1 - b).start()

                gather(g, b).wait()
                write(g, b).start()

        write(n_g - 2, 0).wait()
        write(n_g - 1, 1).wait()

    return gather_kernel(table, idx)


def _moe_kernel(blk_e_ref, nused_ref, x_ref, wgu_ref, bgu_ref, wdn_ref, bdn_ref, o_ref,
                wgu_s, wdn_s, *, f):
    i = pl.program_id(0)
    nused = nused_ref[0]

    @pl.when(i < nused)
    def _compute():
        changed = jnp.logical_or(i == 0, blk_e_ref[i] != blk_e_ref[jnp.maximum(i - 1, 0)])

        @pl.when(changed)
        def _cast_weights():
            wgu_s[...] = wgu_ref[0].astype(BF16)
            wdn_s[...] = wdn_ref[0].astype(BF16)

        xw = x_ref[...]
        x = jnp.concatenate([pltpu.bitcast(xw & jnp.uint32(0xFFFF0000), F32),
                             pltpu.bitcast(xw << 16, F32)], axis=1).astype(BF16)
        hgu = jnp.dot(x, wgu_s[...], preferred_element_type=F32) + bgu_ref[0]
        a = jnp.minimum(hgu[:, :f], SWIGLU_LIMIT)
        u = jnp.clip(hgu[:, f:], -SWIGLU_LIMIT, SWIGLU_LIMIT)
        glu = a * jax.nn.sigmoid(a * SWIGLU_ALPHA)
        o_ref[...] = jnp.dot(((u + 1.0) * glu).astype(BF16), wdn_s[...],
                             preferred_element_type=F32) + bdn_ref[0]

    @pl.when(i >= nused)
    def _unused_block():
        o_ref[...] = jnp.zeros(o_ref.shape, o_ref.dtype)


def _moe_ffn(xs, blk_e, nused, w_gu, b_gu, w_dn, b_dn):
    P = xs.shape[0]
    E, D, F2 = w_gu.shape
    f = F2 // 2
    blk = MOE_BLK
    nb = P // blk
    used_block = lambda i, be, nu: (jnp.minimum(i, nu[0] - 1), 0)
    grid_spec = pltpu.PrefetchScalarGridSpec(
        num_scalar_prefetch=2,
        grid=(nb,),
        in_specs=[
            pl.BlockSpec((blk, D // 2), used_block),
            pl.BlockSpec((1, D, F2), lambda i, be, nu: (be[i], 0, 0)),
            pl.BlockSpec((1, 1, F2), lambda i, be, nu: (be[i], 0, 0)),
            pl.BlockSpec((1, f, D), lambda i, be, nu: (be[i], 0, 0)),
            pl.BlockSpec((1, 1, D), lambda i, be, nu: (be[i], 0, 0)),
        ],
        out_specs=pl.BlockSpec((blk, D), lambda i, be, nu: (i, 0)),
        scratch_shapes=[
            pltpu.VMEM((D, F2), BF16),
            pltpu.VMEM((f, D), BF16),
        ],
    )
    return pl.pallas_call(
        functools.partial(_moe_kernel, f=f),
        grid_spec=grid_spec,
        out_shape=jax.ShapeDtypeStruct((P, D), F32),
        compiler_params=_params(("arbitrary",)),
    )(blk_e, nused, xs, w_gu, b_gu.reshape(E, 1, F2), w_dn, b_dn.reshape(E, 1, D))


def _comb_kernel(h_ref, p_ref, y_ref, g_ref, b_ref, o_ref, *, alpha):
    gate = p_ref[...]
    fsum = ((y_ref[0] * gate[:, 0:1] + y_ref[1] * gate[:, 1:2])
            + (y_ref[2] * gate[:, 2:3] + y_ref[3] * gate[:, 3:4]))
    o_ref[...] = _layer_norm(alpha * h_ref[...] + fsum, g_ref[...], b_ref[...])


def _combine(h, y4, top_p, g, b, alpha):
    T, D = h.shape
    tm = COMB_TM
    return pl.pallas_call(
        functools.partial(_comb_kernel, alpha=alpha),
        grid=(T // tm,),
        in_specs=[
            pl.BlockSpec((tm, D), lambda i: (i, 0)),
            pl.BlockSpec((tm, TOP_K), lambda i: (i, 0)),
            pl.BlockSpec((TOP_K, tm, D), lambda i: (0, i, 0)),
            pl.BlockSpec((1, D), lambda i: (0, 0)),
            pl.BlockSpec((1, D), lambda i: (0, 0)),
        ],
        out_specs=pl.BlockSpec((tm, D), lambda i: (i, 0)),
        out_shape=jax.ShapeDtypeStruct((T, D), F32),
        compiler_params=_params(("arbitrary",)),
    )(h, top_p.T, y4, g, b)


def _route(top_e, blk):
    K, T = top_e.shape
    N = K * T
    flat_e = top_e.reshape(N)
    experts = jnp.arange(N_EXPERTS, dtype=I32)
    order = jnp.argsort(flat_e, stable=True).astype(I32)
    inv = jnp.argsort(order).astype(I32)
    onehot = flat_e[:, None] == experts[None, :]
    counts = jnp.sum(onehot, axis=0, dtype=I32)
    padded = (counts + blk - 1) // blk * blk
    pends = jnp.cumsum(padded)
    offs = jnp.cumsum(counts) - counts
    shift = (pends - padded) - offs
    pos = inv + jnp.sum(jnp.where(onehot, shift[None, :], 0), axis=1, dtype=I32)
    P = N + N_EXPERTS * blk
    nb = P // blk
    blk_start = jnp.arange(nb, dtype=I32) * blk
    blk_e = jnp.minimum(jnp.sum(pends[None, :] <= blk_start[:, None], axis=1, dtype=I32), N_EXPERTS - 1)
    j = (blk_start - shift[blk_e])[:, None] + jnp.arange(blk, dtype=I32)[None, :]
    valid = j < (offs + counts)[blk_e][:, None]
    src = order[jnp.clip(j, 0, N - 1)]
    row_tok = jnp.where(valid, src % T, j % T).reshape(P)
    nused = (pends[-1:] // blk).astype(I32)
    return blk_e, nused, row_tok, pos.reshape(K, T)


def _projection_weights(w_in_l):
    sizes = (ATT_W, ATT_W, ATT_W, IDX_HEADS * IDX_DIM, IDX_DIM, IDX_HEADS, ATT_W, ATT_W, ATT_W)
    offs = np.concatenate([[0], np.cumsum(sizes)])
    qa, ka, va, qi, ki, wi, qb, kb, vb = (w_in_l[:, offs[n]:offs[n + 1]] for n in range(9))
    pad_wi = jnp.zeros((w_in_l.shape[0], LANES - IDX_HEADS), w_in_l.dtype)
    w_att = jnp.concatenate([qa, ka, qi, qb, kb, vb, ki, ki, wi, pad_wi], axis=1).astype(BF16)
    w_va_t = va.T.astype(BF16)
    w_gate = w_in_l[:, offs[9]:].astype(BF16)
    return w_att, w_va_t, w_gate


def kernel(x, w_in, w_branch_a, w_branch_b, w_out, rel_bias, ln1_g, ln1_b, w_router, b_router,
           w_gate_up, b_gate_up, w_down, b_down, ln2_g, ln2_b):
    B, S, D = x.shape
    depth = w_in.shape[0]
    alpha = (2 * depth) ** 0.25
    T = B * S
    h = x.reshape(T, D)
    for l in range(depth):
        w_att, w_va_t, w_gate = _projection_weights(w_in[l])
        proj, vt = _projection(h, w_att, w_va_t, min(T, 1024), PROJ_TN, DSA_BLK)
        proj = proj.reshape(B, S, ATT_COLS)
        vt = vt.reshape(B, S // DSA_BLK, ATT_W, DSA_BLK)
        ya = _dsa(proj, vt, rel_bias).reshape(T, ATT_W)
        yb = _stick_breaking(proj).reshape(T, ATT_W)
        h1, h1_packed, top_e, top_p = _merge(
            h, ya, yb, w_gate, w_branch_a[l].astype(BF16), w_branch_b[l].astype(BF16),
            w_out[l].astype(BF16), ln1_g[l].reshape(1, D), ln1_b[l].reshape(1, D),
            w_router[l].T, b_router[l].reshape(N_EXPERTS, 1), alpha)
        blk_e, nused, row_tok, pos = _route(top_e, MOE_BLK)
        xs = _sc_gather_rows(h1_packed, row_tok)
        ys = _moe_ffn(xs, blk_e, nused, w_gate_up[l], b_gate_up[l], w_down[l], b_down[l])
        y4 = _sc_gather_rows(ys, pos.reshape(TOP_K * T)).reshape(TOP_K, T, D)
        h = _combine(h1, y4, top_p, ln2_g[l].reshape(1, D), ln2_b[l].reshape(1, D), alpha)
    return h.reshape(B, S, D)
```

```python
import functools
import math

import numpy as np
import jax
import jax.numpy as jnp
from jax import lax
from jax.experimental import pallas as pl
from jax.experimental.pallas import tpu as pltpu
from jax.experimental.pallas import tpu_sc as plsc

F32 = jnp.float32
BF16 = jnp.bfloat16
I32 = jnp.int32

A_HEADS = 8
HEAD_DIM = 64
ATT_W = A_HEADS * HEAD_DIM
IDX_HEADS = 8
IDX_DIM = 64
IDX_SCALE = (IDX_HEADS * IDX_DIM) ** -0.5
TOPK_MAX = 256
N_BUCKETS = 32
MAX_DISTANCE = 128
N_EXPERTS = 32
TOP_K = 4
SWIGLU_LIMIT = 7.0
SWIGLU_ALPHA = 1.702
LN_EPS = 1e-5
QK_SCALE = HEAD_DIM ** -0.5

LANES = 128
SUBLANES = 8
HALF = LANES // 2
N_PAIRS = A_HEADS // 2
VMEM_LIMIT = 56 * 1024 * 1024

DSA_BLK = 256
SB_T = 256
SB_PAIRS = 4
PROJ_TM = 1024
MERGE_TM = 512
MOE_BLK = 512
COMB_TM = 512
SC_GATHER_BYTES = 128 * 1024
REDUCE_CHAINS = 4
BISECT_CAP = 24
NEG = -1e30

COL_QA, COL_KA, COL_QI, COL_QB, COL_KB, COL_VB = (g * ATT_W for g in range(6))
COL_KK = 6 * ATT_W
COL_WI = COL_KK + LANES
ATT_COLS = COL_WI + LANES
PROJ_TN = ATT_COLS // 2

NT_DIMS = (((1,), (1,)), ((), ()))


def _params(sem, vmem=VMEM_LIMIT):
    return pltpu.CompilerParams(dimension_semantics=sem, vmem_limit_bytes=vmem)


def _proj_kernel(x_ref, w_ref, wt_ref, o_ref, ot_ref, xb_ref, *, tt):
    @pl.when(pl.program_id(1) == 0)
    def _row_tile_start():
        xb_ref[...] = x_ref[...].astype(BF16)
        for r in range(ot_ref.shape[0]):
            ot_ref[r] = lax.dot_general(wt_ref[...], xb_ref[r * tt:(r + 1) * tt, :], NT_DIMS,
                                        preferred_element_type=F32).astype(ot_ref.dtype)

    o_ref[...] = jnp.dot(xb_ref[...], w_ref[...], preferred_element_type=F32).astype(o_ref.dtype)


def _projection(x, w, w_t, tm, tn, tt):
    M, K = x.shape
    N = w.shape[1]
    Nt = w_t.shape[0]
    return pl.pallas_call(
        functools.partial(_proj_kernel, tt=tt),
        grid=(M // tm, N // tn),
        in_specs=[pl.BlockSpec((tm, K), lambda i, j: (i, 0)),
                  pl.BlockSpec((K, tn), lambda i, j: (0, j)),
                  pl.BlockSpec((Nt, K), lambda i, j: (0, 0))],
        out_specs=[pl.BlockSpec((tm, tn), lambda i, j: (i, j)),
                   pl.BlockSpec((tm // tt, Nt, tt), lambda i, j: (i, 0, 0))],
        out_shape=[jax.ShapeDtypeStruct((M, N), BF16),
                   jax.ShapeDtypeStruct((M // tt, Nt, tt), BF16)],
        scratch_shapes=[pltpu.VMEM((tm, K), BF16)],
        compiler_params=_params(("arbitrary", "arbitrary")),
    )(x, w, w_t)


def _t5_bucket_np(n):
    n = np.maximum(n, 0)
    max_exact = N_BUCKETS // 2
    nf = np.maximum(n, 1).astype(np.float32)
    large = max_exact + (np.log(nf / max_exact) / math.log(MAX_DISTANCE / max_exact)
                         * (N_BUCKETS - max_exact)).astype(np.int32)
    large = np.minimum(large, N_BUCKETS - 1)
    return np.where(n < max_exact, n, large).astype(np.int32)


def _dsa_n_off(blk):
    return 2 + -(-MAX_DISTANCE // blk)


def _dsa_bucket_tiles(blk):
    n_off = _dsa_n_off(blk)
    j = np.arange(blk)[None, :, None]
    i = np.arange(blk)[None, None, :]
    o = np.arange(n_off)[:, None, None]
    return _t5_bucket_np(i - j + blk * (n_off - 1 - o))


def _dsa_kernel(bucket_ref, relb_ref, q_ref, k_ref, vt_ref, qi_ref, kk_ref, wi_ref, o_ref,
                sc_ref, bias_ref, m_ref, l_ref, acc_ref, st_ref, mx_ref,
                *, blk, n_chunks, n_sel, n_off):
    b = pl.program_id(0)
    i = pl.program_id(1)
    q0 = i * blk
    nck = i + 1
    groups = blk // SUBLANES

    @pl.when(jnp.logical_and(b == 0, i == 0))
    def _build_bias():
        def head_body(h, _):
            for o in range(n_off):
                for rb in range(blk // LANES):
                    for cb in range(blk // LANES):
                        rs = slice(rb * LANES, (rb + 1) * LANES)
                        cs = slice(cb * LANES, (cb + 1) * LANES)
                        bk = bucket_ref[o, rs, cs]

                        def bucket_body(n, acc):
                            return jnp.where(bk == n, relb_ref[n, h], acc)

                        bias_ref[h, o, rs, cs] = lax.fori_loop(
                            0, N_BUCKETS, bucket_body, jnp.zeros((LANES, LANES), F32))
            return 0

        lax.fori_loop(0, A_HEADS, head_body, 0)

    lane = lax.broadcasted_iota(I32, (blk, LANES), 1)
    lo_half = lane < HALF
    krow = lax.broadcasted_iota(I32, (blk, blk), 0)
    qpos = q0 + lax.broadcasted_iota(I32, (1, blk), 1)

    def pair_split(ref, scale):
        out = []
        for p in range(N_PAIRS):
            v = ref[0, :, p * LANES:(p + 1) * LANES].astype(F32)
            if scale != 1.0:
                v = v * scale
            out.append(jnp.where(lo_half, v, 0.0).astype(BF16))
            out.append(jnp.where(lo_half, 0.0, v).astype(BF16))
        return out

    wi_t = wi_ref[0].astype(F32).T
    wrow = [wi_t[h:h + 1, :] * IDX_SCALE for h in range(IDX_HEADS)]
    qi_m = pair_split(qi_ref, 1.0)

    def score_chunk(c, _):
        c0 = pl.multiple_of(c * blk, blk)
        kk = kk_ref[0, pl.ds(c0, blk), :]
        acc = jnp.zeros((blk, blk), F32)
        for h in range(IDX_HEADS):
            s = lax.dot_general(kk, qi_m[h], NT_DIMS, preferred_element_type=F32)
            acc = acc + wrow[h] * jnp.maximum(s, 0.0)
        sc_ref[c] = jnp.where(c0 + krow <= qpos, acc, -jnp.inf)
        return 0

    lax.fori_loop(0, nck, score_chunk, 0)

    kt = jnp.minimum(qpos + 1, n_sel).astype(F32)

    def fold(fn, init):
        def body(c, acc):
            return fn(acc, sc_ref[c])
        return lax.fori_loop(0, nck, body, init)

    def part(x, op):
        y = op(x.reshape(REDUCE_CHAINS, groups // REDUCE_CHAINS, SUBLANES, blk), axis=1)
        return op(y, axis=0)

    def fin(x, op):
        return op(x, axis=0, keepdims=True)

    zeros8 = jnp.zeros((SUBLANES, blk), F32)
    pinf8 = jnp.full((SUBLANES, blk), jnp.inf, F32)

    def count_ge(th):
        return fin(fold(lambda a, s: a + part(jnp.where(s >= th, 1.0, 0.0), jnp.sum), zeros8),
                   jnp.sum)

    mn, mx = fold(lambda a, s: (
        jnp.minimum(a[0], part(jnp.where(s == -jnp.inf, jnp.inf, s), jnp.min)),
        jnp.maximum(a[1], part(s, jnp.max))), (pinf8, -pinf8))
    rmin = fin(mn, jnp.min)
    rmax = fin(mx, jnp.max)

    def bis_cond(st):
        it, lo, hi, clo = st
        return jnp.logical_and(it < BISECT_CAP, jnp.max(jnp.abs(clo - kt)) > 0.0)

    def halve(lo, hi, clo):
        mid = 0.5 * lo + 0.5 * hi
        c = count_ge(mid)
        active = clo != kt
        up = jnp.logical_and(active, c >= kt)
        dn = jnp.logical_and(active, c < kt)
        return jnp.where(up, mid, lo), jnp.where(dn, mid, hi), jnp.where(up, c, clo)

    def bis_body(st):
        it, lo, hi, clo = st
        return (it + 2,) + halve(*halve(lo, hi, clo))

    _, lo, _, _ = lax.while_loop(
        bis_cond, bis_body, (jnp.int32(0), rmin, rmax + 1.0, (qpos + 1).astype(F32)))

    def stats(lo_):
        a_ = fin(fold(lambda a, s: jnp.minimum(a, part(jnp.where(s >= lo_, s, jnp.inf), jnp.min)),
                      pinf8), jnp.min)
        cg, ct, nx = fold(
            lambda a, s: (a[0] + part(jnp.where(s > a_, 1.0, 0.0), jnp.sum),
                               a[1] + part(jnp.where(s == a_, 1.0, 0.0), jnp.sum),
                               jnp.minimum(a[2], part(jnp.where(s > a_, s, jnp.inf), jnp.min))),
            (zeros8, zeros8, pinf8))
        return a_, fin(cg, jnp.sum), fin(ct, jnp.sum), fin(nx, jnp.min)

    def fin_cond(st):
        return st[0]

    def fin_body(st):
        _, lo_, _, _ = st
        a_, cgt_, nt_, nxt_ = stats(lo_)
        bad = cgt_ >= kt
        return (jnp.max(jnp.where(bad, 1.0, 0.0)) > 0.0, jnp.where(bad, nxt_, a_), cgt_, nt_)

    _, a, cgt, nties = lax.while_loop(fin_cond, fin_body, (jnp.bool_(True), lo, kt, kt))
    need = kt - cgt
    excess = jnp.max(jnp.where(nties > need, 1.0, 0.0)) > 0.0

    def mask_plain():
        def body(c, _):
            sc_ref[c] = jnp.where(sc_ref[c] >= a, 0.0, NEG)
            return 0
        lax.fori_loop(0, nck, body, 0)

    def mask_ties():
        upto = (krow >= lax.broadcasted_iota(I32, (blk, blk), 1)).astype(BF16)

        def body(c, seen):
            s = sc_ref[c]
            tie = s == a
            rank = jnp.dot(upto, jnp.where(tie, 1.0, 0.0).astype(BF16),
                           preferred_element_type=F32) + seen
            sel = jnp.logical_or(s > a, jnp.logical_and(tie, rank <= need))
            sc_ref[c] = jnp.where(sel, 0.0, NEG)
            return rank[blk - 1:blk, :]

        lax.fori_loop(0, nck, body, jnp.zeros((1, blk), F32))

    lax.cond(excess, mask_ties, mask_plain)

    m_ref[...] = jnp.full(m_ref.shape, NEG, F32)
    l_ref[...] = jnp.zeros(l_ref.shape, F32)
    acc_ref[...] = jnp.zeros(acc_ref.shape, F32)
    q_m = pair_split(q_ref, QK_SCALE)

    def stage_logits(c, slot):
        c = jnp.minimum(c, n_chunks - 1)
        c0 = pl.multiple_of(c * blk, blk)
        madd = sc_ref[c]
        o_idx = jnp.clip(c - i + (n_off - 1), 0, n_off - 1)
        for p in range(N_PAIRS):
            k2 = k_ref[0, pl.ds(c0, blk), p * LANES:(p + 1) * LANES]
            for hh in range(2):
                h = 2 * p + hh
                s = lax.dot_general(k2, q_m[h], NT_DIMS, preferred_element_type=F32)
                s = s + bias_ref[h, o_idx] + madd
                st_ref[slot, h] = s
                mx_ref[slot, h] = fin(part(s, jnp.max), jnp.max)

    def stage_values(c, slot):
        for p in range(N_PAIRS):
            vt2 = vt_ref[0, c, p * LANES:(p + 1) * LANES, :]
            for hh in range(2):
                h = 2 * p + hh
                m_old = m_ref[h]
                m_new = jnp.maximum(m_old, mx_ref[slot, h])
                alpha = jnp.exp(m_old - m_new)
                pexp = jnp.exp(st_ref[slot, h] - m_new)
                l_ref[h] = alpha * l_ref[h] + fin(part(pexp, jnp.sum), jnp.sum)
                acc_ref[h] = alpha * acc_ref[h] + jnp.dot(vt2, pexp.astype(BF16),
                                                          preferred_element_type=F32)
                m_ref[h] = m_new

    stage_logits(0, 0)

    def att_pair(pp, _):
        c = 2 * pp
        stage_logits(c + 1, 1)
        stage_values(c, 0)
        stage_logits(c + 2, 0)
        stage_values(c + 1, 1)
        return 0

    lax.fori_loop(0, nck // 2, att_pair, 0)

    @pl.when(lax.rem(nck, 2) == 1)
    def _last_chunk():
        stage_values(nck - 1, 0)

    lo_rows = lax.broadcasted_iota(I32, (LANES, blk), 0) < HALF
    for p in range(N_PAIRS):
        oa = acc_ref[2 * p] / l_ref[2 * p]
        ob = acc_ref[2 * p + 1] / l_ref[2 * p + 1]
        o_ref[0, :, p * LANES:(p + 1) * LANES] = jnp.where(lo_rows, oa, ob).T.astype(o_ref.dtype)


def _dsa(proj3, vt4, rel_bias):
    B, S, _ = proj3.shape
    blk = DSA_BLK
    n_off = _dsa_n_off(blk)
    n_sel = min(TOPK_MAX, S // 4)
    bucket = jnp.asarray(_dsa_bucket_tiles(blk))
    n_chunks = S // blk
    assert S % blk == 0
    kern = functools.partial(_dsa_kernel, blk=blk, n_chunks=n_chunks, n_sel=n_sel, n_off=n_off)
    return pl.pallas_call(
        kern,
        grid=(B, S // blk),
        in_specs=[
            pl.BlockSpec((n_off, blk, blk), lambda b, i: (0, 0, 0)),
            pl.BlockSpec(memory_space=pltpu.SMEM),
            pl.BlockSpec((1, blk, ATT_W), lambda b, i: (b, i, COL_QA // ATT_W)),
            pl.BlockSpec((1, S, ATT_W), lambda b, i: (b, 0, COL_KA // ATT_W)),
            pl.BlockSpec((1, S // blk, ATT_W, blk), lambda b, i: (b, 0, 0, 0)),
            pl.BlockSpec((1, blk, ATT_W), lambda b, i: (b, i, COL_QI // ATT_W)),
            pl.BlockSpec((1, S, LANES), lambda b, i: (b, 0, COL_KK // LANES)),
            pl.BlockSpec((1, blk, LANES), lambda b, i: (b, i, COL_WI // LANES)),
        ],
        out_specs=pl.BlockSpec((1, blk, ATT_W), lambda b, i: (b, i, 0)),
        out_shape=jax.ShapeDtypeStruct((B, S, ATT_W), BF16),
        scratch_shapes=[
            pltpu.VMEM((S // blk, blk, blk), F32),
            pltpu.VMEM((A_HEADS, n_off, blk, blk), F32),
            pltpu.VMEM((A_HEADS, 1, blk), F32),
            pltpu.VMEM((A_HEADS, 1, blk), F32),
            pltpu.VMEM((A_HEADS, LANES, blk), F32),
            pltpu.VMEM((2, A_HEADS, blk, blk), F32),
            pltpu.VMEM((2, A_HEADS, 1, blk), F32),
        ],
        compiler_params=_params(("arbitrary", "arbitrary")),
    )(bucket, rel_bias, proj3, proj3, vt4, proj3, proj3, proj3)


def _sb_kernel(q_ref, k_ref, v_ref, o_ref, hl_ref, z_ref, *, t):
    i = pl.program_id(2)
    n = i + 1
    lane = lax.broadcasted_iota(I32, (t, LANES), 1)
    lo_half = lane < HALF
    q_m = []
    for pr in range(SB_PAIRS):
        q2 = q_ref[0, :, pr * LANES:(pr + 1) * LANES].astype(F32) * QK_SCALE
        q_m += [jnp.where(lo_half, q2, 0.0).astype(BF16), jnp.where(lo_half, 0.0, q2).astype(BF16)]
    heads = 2 * SB_PAIRS
    r = lax.broadcasted_iota(I32, (t, t), 0)
    cidx = lax.broadcasted_iota(I32, (t, t), 1)
    neg_from = jnp.where(r >= cidx, -1.0, 0.0).astype(BF16)
    diff = cidx - r

    def stage_terms(step, slot, diagonal):
        c0 = pl.multiple_of(jnp.maximum(i - step, 0) * t, t)
        if diagonal:
            keep = diff < 0
        for hh in range(heads):
            pr = hh // 2
            k2 = k_ref[0, pl.ds(c0, t), pr * LANES:(pr + 1) * LANES]
            z = lax.dot_general(q_m[hh], k2, NT_DIMS, preferred_element_type=F32)
            sp = jnp.maximum(z, 0.0) + jnp.log(1.0 + jnp.exp(-jnp.abs(z)))
            if diagonal:
                sp = jnp.where(keep, sp, 0.0)
                z = jnp.where(keep, z, NEG)
            hi = sp.astype(BF16)
            hl_ref[slot, 2 * hh] = hi
            hl_ref[slot, 2 * hh + 1] = (sp - hi.astype(F32)).astype(BF16)
            z_ref[slot, hh] = z

    def stage_apply(step, slot, carry):
        c0 = pl.multiple_of((i - step) * t, t)
        cum4 = jnp.dot(hl_ref[slot].reshape(2 * heads * t, t), neg_from, preferred_element_type=F32)
        out = []
        for hh in range(heads):
            pr = hh // 2
            v2 = v_ref[0, pl.ds(c0, t), pr * LANES:(pr + 1) * LANES]
            car, acc = carry[hh]
            cum = cum4[(2 * hh) * t:(2 * hh + 1) * t] + cum4[(2 * hh + 1) * t:(2 * hh + 2) * t]
            w = jnp.exp(z_ref[slot, hh] + cum + car)
            acc = acc + jnp.dot(w.astype(BF16), v2, preferred_element_type=F32)
            out.append((car + cum[:, 0:1], acc))
        return tuple(out)

    z1 = jnp.zeros((t, 1), F32)
    za = jnp.zeros((t, LANES), F32)
    stage_terms(0, 0, True)

    def pair_body(pp, carry):
        step = 2 * pp
        stage_terms(step + 1, 1, False)
        carry = stage_apply(step, 0, carry)
        stage_terms(step + 2, 0, False)
        return stage_apply(step + 1, 1, carry)

    carry = lax.fori_loop(0, n // 2, pair_body, ((z1, za),) * heads)
    carry = lax.cond(lax.rem(n, 2) == 1, lambda c: stage_apply(n - 1, 0, c), lambda c: c, carry)
    for pr in range(SB_PAIRS):
        o_ref[0, :, pr * LANES:(pr + 1) * LANES] = jnp.where(
            lo_half, carry[2 * pr][1], carry[2 * pr + 1][1]).astype(o_ref.dtype)


def _stick_breaking(proj3):
    B, S, _ = proj3.shape
    t = SB_T
    w = SB_PAIRS * LANES
    qb, kb, vb = COL_QB // w, COL_KB // w, COL_VB // w
    return pl.pallas_call(
        functools.partial(_sb_kernel, t=t),
        grid=(B, N_PAIRS // SB_PAIRS, S // t),
        in_specs=[
            pl.BlockSpec((1, t, w), lambda b, p, i: (b, i, qb + p)),
            pl.BlockSpec((1, S, w), lambda b, p, i: (b, 0, kb + p)),
            pl.BlockSpec((1, S, w), lambda b, p, i: (b, 0, vb + p)),
        ],
        out_specs=pl.BlockSpec((1, t, w), lambda b, p, i: (b, i, p)),
        out_shape=jax.ShapeDtypeStruct((B, S, ATT_W), BF16),
        scratch_shapes=[
            pltpu.VMEM((2, 4 * SB_PAIRS, t, t), BF16),
            pltpu.VMEM((2, 2 * SB_PAIRS, t, t), F32),
        ],
        compiler_params=_params(("arbitrary", "arbitrary", "arbitrary")),
    )(proj3, proj3, proj3)


def _layer_norm(r, g, b):
    mu = jnp.mean(r, axis=-1, keepdims=True)
    d = r - mu
    var = jnp.mean(d * d, axis=-1, keepdims=True)
    return d * lax.rsqrt(var + LN_EPS) * g + b


def _split_bf16(v):
    hi = v.astype(BF16)
    return hi, (v - hi.astype(F32)).astype(BF16)


def _merge_kernel(x_ref, ya_ref, yb_ref, wg_ref, wa_ref, wb_ref, wo_ref, g_ref, b_ref,
                  wr_ref, br_ref, h_ref, hp_ref, e_ref, p_ref, *, alpha, d):
    pa = jnp.dot(ya_ref[...], wa_ref[...], preferred_element_type=F32)
    pb = jnp.dot(yb_ref[...], wb_ref[...], preferred_element_type=F32)
    gates = jnp.dot(x_ref[...].astype(BF16), wg_ref[...], preferred_element_type=F32)
    merged = jax.nn.sigmoid(gates[:, :d]) * pa + jax.nn.sigmoid(gates[:, d:]) * pb
    m = jnp.dot(merged.astype(BF16), wo_ref[...], preferred_element_type=F32)
    h = _layer_norm(alpha * x_ref[...] + m, g_ref[...], b_ref[...])
    h_ref[...] = h
    bits = pltpu.bitcast(h.astype(BF16).astype(F32), jnp.uint32)
    hp_ref[...] = bits[:, :d // 2] | (bits[:, d // 2:] >> 16)

    h_hi, h_lo = _split_bf16(h)
    w_hi, w_lo = _split_bf16(wr_ref[...])
    logit = (lax.dot_general(w_hi, h_hi, NT_DIMS, preferred_element_type=F32)
             + lax.dot_general(w_hi, h_lo, NT_DIMS, preferred_element_type=F32)
             + lax.dot_general(w_lo, h_hi, NT_DIMS, preferred_element_type=F32)) + br_ref[...]
    eid = lax.broadcasted_iota(I32, logit.shape, 0)
    vals, ids = [], []
    for _ in range(TOP_K):
        mx = jnp.max(logit, axis=0, keepdims=True)
        am = jnp.min(jnp.where(logit == mx, eid, N_EXPERTS), axis=0, keepdims=True)
        vals.append(mx)
        ids.append(am)
        logit = jnp.where(eid == am, -jnp.inf, logit)
    ex = [jnp.exp(v - vals[0]) for v in vals]
    den = ex[0] + ex[1] + ex[2] + ex[3]
    for k in range(TOP_K):
        e_ref[k:k + 1, :] = ids[k]
        p_ref[k:k + 1, :] = ex[k] / den


def _merge(x2, ya, yb, wg, wa, wb, wo, g, b, wr_t, br, alpha):
    T, D = x2.shape
    tm = MERGE_TM
    row = lambda i: (i, 0)
    fixed = lambda i: (0, 0)
    return pl.pallas_call(
        functools.partial(_merge_kernel, alpha=alpha, d=D),
        grid=(T // tm,),
        in_specs=[
            pl.BlockSpec((tm, D), row),
            pl.BlockSpec((tm, ATT_W), row),
            pl.BlockSpec((tm, ATT_W), row),
            pl.BlockSpec((D, 2 * D), fixed),
            pl.BlockSpec((ATT_W, D), fixed),
            pl.BlockSpec((ATT_W, D), fixed),
            pl.BlockSpec((D, D), fixed),
            pl.BlockSpec((1, D), fixed),
            pl.BlockSpec((1, D), fixed),
            pl.BlockSpec((N_EXPERTS, D), fixed),
            pl.BlockSpec((N_EXPERTS, 1), fixed),
        ],
        out_specs=[
            pl.BlockSpec((tm, D), row),
            pl.BlockSpec((tm, D // 2), row),
            pl.BlockSpec((TOP_K, tm), lambda i: (0, i)),
            pl.BlockSpec((TOP_K, tm), lambda i: (0, i)),
        ],
        out_shape=[
            jax.ShapeDtypeStruct((T, D), F32),
            jax.ShapeDtypeStruct((T, D // 2), jnp.uint32),
            jax.ShapeDtypeStruct((TOP_K, T), I32),
            jax.ShapeDtypeStruct((TOP_K, T), F32),
        ],
        compiler_params=_params(("arbitrary",)),
    )(x2, ya, yb, wg, wa, wb, wo, g, b, wr_t, br)


def _sc_gather_rows(table, idx):
    n = idx.shape[0]
    d = table.shape[1]
    info = plsc.get_sparse_core_info()
    n_cores, n_sub = info.num_cores, info.num_subcores
    per_w = n // (n_cores * n_sub)
    c = SC_GATHER_BYTES // (d * table.dtype.itemsize)
    n_g = per_w // c
    assert n == per_w * n_cores * n_sub and per_w == n_g * c and n_g % 2 == 0 and n_g >= 2
    mesh = plsc.VectorSubcoreMesh(core_axis_name="c", subcore_axis_name="s")

    @functools.partial(
        pl.kernel, mesh=mesh, out_type=jax.ShapeDtypeStruct((n, d), table.dtype),
        scratch_types=[pltpu.VMEM((per_w,), I32), pltpu.VMEM((2, c, d), table.dtype),
                       pltpu.SemaphoreType.DMA((2,)), pltpu.SemaphoreType.DMA((2,))])
    def gather_kernel(table_hbm, idx_hbm, out_hbm, idx_v, rows_v, gsem, wsem):
        base = (lax.axis_index("s") * n_cores + lax.axis_index("c")) * per_w
        pltpu.sync_copy(idx_hbm.at[pl.ds(base, per_w)], idx_v)

        def gather(g, b):
            return pltpu.make_async_copy(table_hbm.at[idx_v.at[pl.ds(g * c, c)]], rows_v.at[b],
                                         gsem.at[b])

        def write(g, b):
            return pltpu.make_async_copy(rows_v.at[b], out_hbm.at[pl.ds(base + g * c, c)], wsem.at[b])

        gather(0, 0).start()

        @pl.loop(0, n_g, step=2)
        def _ring(g0):
            for b in range(2):
                g = g0 + b

                @pl.when(g + 1 < n_g)
                def _next():
                    @pl.when(g >= 1)
                    def _buffer_free():
                        write(g - 1, 1 - b).wait()
                    gather(g + 1, 1 - b).start()

                gather(g, b).wait()
                write(g, b).start()

        write(n_g - 2, 0).wait()
        write(n_g - 1, 1).wait()

    return gather_kernel(table, idx)


def _moe_kernel(blk_e_ref, nused_ref, x_ref, wgu_ref, bgu_ref, wdn_ref, bdn_ref, o_ref,
                wgu_s, wdn_s, *, f):
    i = pl.program_id(0)
    nused = nused_ref[0]

    @pl.when(i < nused)
    def _compute():
        changed = jnp.logical_or(i == 0, blk_e_ref[i] != blk_e_ref[jnp.maximum(i - 1, 0)])

        @pl.when(changed)
        def _cast_weights():
            wgu_s[...] = wgu_ref[0].astype(BF16)
            wdn_s[...] = wdn_ref[0].astype(BF16)

        xw = x_ref[...]
        x = jnp.concatenate([pltpu.bitcast(xw & jnp.uint32(0xFFFF0000), F32),
                             pltpu.bitcast(xw << 16, F32)], axis=1).astype(BF16)
        hgu = jnp.dot(x, wgu_s[...], preferred_element_type=F32) + bgu_ref[0]
        a = jnp.minimum(hgu[:, :f], SWIGLU_LIMIT)
        u = jnp.clip(hgu[:, f:], -SWIGLU_LIMIT, SWIGLU_LIMIT)
        glu = a * jax.nn.sigmoid(a * SWIGLU_ALPHA)
        o_ref[...] = jnp.dot(((u + 1.0) * glu).astype(BF16), wdn_s[...],
                             preferred_element_type=F32) + bdn_ref[0]

    @pl.when(i >= nused)
    def _unused_block():
        o_ref[...] = jnp.zeros(o_ref.shape, o_ref.dtype)


def _moe_ffn(xs, blk_e, nused, w_gu, b_gu, w_dn, b_dn):
    P = xs.shape[0]
    E, D, F2 = w_gu.shape
    f = F2 // 2
    blk = MOE_BLK
    nb = P // blk
    used_block = lambda i, be, nu: (jnp.minimum(i, nu[0] - 1), 0)
    grid_spec = pltpu.PrefetchScalarGridSpec(
        num_scalar_prefetch=2,
        grid=(nb,),
        in_specs=[
            pl.BlockSpec((blk, D // 2), used_block),
            pl.BlockSpec((1, D, F2), lambda i, be, nu: (be[i], 0, 0)),
            pl.BlockSpec((1, 1, F2), lambda i, be, nu: (be[i], 0, 0)),
            pl.BlockSpec((1, f, D), lambda i, be, nu: (be[i], 0, 0)),
            pl.BlockSpec((1, 1, D), lambda i, be, nu: (be[i], 0, 0)),
        ],
        out_specs=pl.BlockSpec((blk, D), lambda i, be, nu: (i, 0)),
        scratch_shapes=[
            pltpu.VMEM((D, F2), BF16),
            pltpu.VMEM((f, D), BF16),
        ],
    )
    return pl.pallas_call(
        functools.partial(_moe_kernel, f=f),
        grid_spec=grid_spec,
        out_shape=jax.ShapeDtypeStruct((P, D), F32),
        compiler_params=_params(("arbitrary",)),
    )(blk_e, nused, xs, w_gu, b_gu.reshape(E, 1, F2), w_dn, b_dn.reshape(E, 1, D))


def _comb_kernel(h_ref, p_ref, y_ref, g_ref, b_ref, o_ref, *, alpha):
    gate = p_ref[...]
    fsum = ((y_ref[0] * gate[:, 0:1] + y_ref[1] * gate[:, 1:2])
            + (y_ref[2] * gate[:, 2:3] + y_ref[3] * gate[:, 3:4]))
    o_ref[...] = _layer_norm(alpha * h_ref[...] + fsum, g_ref[...], b_ref[...])


def _combine(h, y4, top_p, g, b, alpha):
    T, D = h.shape
    tm = COMB_TM
    return pl.pallas_call(
        functools.partial(_comb_kernel, alpha=alpha),
        grid=(T // tm,),
        in_specs=[
            pl.BlockSpec((tm, D), lambda i: (i, 0)),
            pl.BlockSpec((tm, TOP_K), lambda i: (i, 0)),
            pl.BlockSpec((TOP_K, tm, D), lambda i: (0, i, 0)),
            pl.BlockSpec((1, D), lambda i: (0, 0)),
            pl.BlockSpec((1, D), lambda i: (0, 0)),
        ],
        out_specs=pl.BlockSpec((tm, D), lambda i: (i, 0)),
        out_shape=jax.ShapeDtypeStruct((T, D), F32),
        compiler_params=_params(("arbitrary",)),
    )(h, top_p.T, y4, g, b)


def _route(top_e, blk):
    K, T = top_e.shape
    N = K * T
    flat_e = top_e.reshape(N)
    experts = jnp.arange(N_EXPERTS, dtype=I32)
    order = jnp.argsort(flat_e, stable=True).astype(I32)
    inv = jnp.argsort(order).astype(I32)
    onehot = flat_e[:, None] == experts[None, :]
    counts = jnp.sum(onehot, axis=0, dtype=I32)
    padded = (counts + blk - 1) // blk * blk
    pends = jnp.cumsum(padded)
    offs = jnp.cumsum(counts) - counts
    shift = (pends - padded) - offs
    pos = inv + jnp.sum(jnp.where(onehot, shift[None, :], 0), axis=1, dtype=I32)
    P = N + N_EXPERTS * blk
    nb = P // blk
    blk_start = jnp.arange(nb, dtype=I32) * blk
    blk_e = jnp.minimum(jnp.sum(pends[None, :] <= blk_start[:, None], axis=1, dtype=I32), N_EXPERTS - 1)
    j = (blk_start - shift[blk_e])[:, None] + jnp.arange(blk, dtype=I32)[None, :]
    valid = j < (offs + counts)[blk_e][:, None]
    src = order[jnp.clip(j, 0, N - 1)]
    row_tok = jnp.where(valid, src % T, j % T).reshape(P)
    nused = (pends[-1:] // blk).astype(I32)
    return blk_e, nused, row_tok, pos.reshape(K, T)


def _projection_weights(w_in_l):
    sizes = (ATT_W, ATT_W, ATT_W, IDX_HEADS * IDX_DIM, IDX_DIM, IDX_HEADS, ATT_W, ATT_W, ATT_W)
    offs = np.concatenate([[0], np.cumsum(sizes)])
    qa, ka, va, qi, ki, wi, qb, kb, vb = (w_in_l[:, offs[n]:offs[n + 1]] for n in range(9))
    pad_wi = jnp.zeros((w_in_l.shape[0], LANES - IDX_HEADS), w_in_l.dtype)
    w_att = jnp.concatenate([qa, ka, qi, qb, kb, vb, ki, ki, wi, pad_wi], axis=1).astype(BF16)
    w_va_t = va.T.astype(BF16)
    w_gate = w_in_l[:, offs[9]:].astype(BF16)
    return w_att, w_va_t, w_gate


def kernel(x, w_in, w_branch_a, w_branch_b, w_out, rel_bias, ln1_g, ln1_b, w_router, b_router,
           w_gate_up, b_gate_up, w_down, b_down, ln2_g, ln2_b):
    B, S, D = x.shape
    depth = w_in.shape[0]
    alpha = (2 * depth) ** 0.25
    T = B * S
    h = x.reshape(T, D)
    for l in range(depth):
        w_att, w_va_t, w_gate = _projection_weights(w_in[l])
        proj, vt = _projection(h, w_att, w_va_t, min(T, PROJ_TM), PROJ_TN, DSA_BLK)
        proj = proj.reshape(B, S, ATT_COLS)
        vt = vt.reshape(B, S // DSA_BLK, ATT_W, DSA_BLK)
        ya = _dsa(proj, vt, rel_bias).reshape(T, ATT_W)
        yb = _stick_breaking(proj).reshape(T, ATT_W)
        h1, h1_packed, top_e, top_p = _merge(
            h, ya, yb, w_gate, w_branch_a[l].astype(BF16), w_branch_b[l].astype(BF16),
            w_out[l].astype(BF16), ln1_g[l].reshape(1, D), ln1_b[l].reshape(1, D),
            w_router[l].T, b_router[l].reshape(N_EXPERTS, 1), alpha)
        blk_e, nused, row_tok, pos = _route(top_e, MOE_BLK)
        xs = _sc_gather_rows(h1_packed, row_tok)
        ys = _moe_ffn(xs, blk_e, nused, w_gate_up[l], b_gate_up[l], w_down[l], b_down[l])
        y4 = _sc_gather_rows(ys, pos.reshape(TOP_K * T)).reshape(TOP_K, T, D)
        h = _combine(h1, y4, top_p, ln2_g[l].reshape(1, D), ln2_b[l].reshape(1, D), alpha)
    return h.reshape(B, S, D)
```

```python
import functools
import math

import numpy as np
import jax
import jax.numpy as jnp
from jax import lax
from jax.experimental import pallas as pl
from jax.experimental.pallas import tpu as pltpu
from jax.experimental.pallas import tpu_sc as plsc

F32 = jnp.float32
BF16 = jnp.bfloat16
I32 = jnp.int32

A_HEADS = 8
HEAD_DIM = 64
ATT_W = A_HEADS * HEAD_DIM
IDX_HEADS = 8
IDX_DIM = 64
IDX_SCALE = (IDX_HEADS * IDX_DIM) ** -0.5
TOPK_MAX = 256
N_BUCKETS = 32
MAX_DISTANCE = 128
N_EXPERTS = 32
TOP_K = 4
SWIGLU_LIMIT = 7.0
SWIGLU_ALPHA = 1.702
LN_EPS = 1e-5
QK_SCALE = HEAD_DIM ** -0.5

LANES = 128
SUBLANES = 8
HALF = LANES // 2
N_PAIRS = A_HEADS // 2
VMEM_LIMIT = 56 * 1024 * 1024

DSA_BLK = 256
SB_T = 256
SB_PAIRS = 4
PROJ_TM = 1024
MERGE_TM = 512
MOE_BLK = 512
COMB_TM = 512
SC_GATHER_BYTES = 128 * 1024
REDUCE_CHAINS = 8
BISECT_CAP = 24
NEG = -1e30

COL_QA, COL_KA, COL_QI, COL_QB, COL_KB, COL_VB = (g * ATT_W for g in range(6))
COL_KK = 6 * ATT_W
COL_WI = COL_KK + LANES
ATT_COLS = COL_WI + LANES
PROJ_TN = ATT_COLS // 2

NT_DIMS = (((1,), (1,)), ((), ()))


def _params(sem, vmem=VMEM_LIMIT):
    return pltpu.CompilerParams(dimension_semantics=sem, vmem_limit_bytes=vmem)


def _proj_kernel(x_ref, w_ref, wt_ref, o_ref, ot_ref, xb_ref, *, tt):
    @pl.when(pl.program_id(1) == 0)
    def _row_tile_start():
        xb_ref[...] = x_ref[...].astype(BF16)
        for r in range(ot_ref.shape[0]):
            ot_ref[r] = lax.dot_general(wt_ref[...], xb_ref[r * tt:(r + 1) * tt, :], NT_DIMS,
                                        preferred_element_type=F32).astype(ot_ref.dtype)

    o_ref[...] = jnp.dot(xb_ref[...], w_ref[...], preferred_element_type=F32).astype(o_ref.dtype)


def _projection(x, w, w_t, tm, tn, tt):
    M, K = x.shape
    N = w.shape[1]
    Nt = w_t.shape[0]
    return pl.pallas_call(
        functools.partial(_proj_kernel, tt=tt),
        grid=(M // tm, N // tn),
        in_specs=[pl.BlockSpec((tm, K), lambda i, j: (i, 0)),
                  pl.BlockSpec((K, tn), lambda i, j: (0, j)),
                  pl.BlockSpec((Nt, K), lambda i, j: (0, 0))],
        out_specs=[pl.BlockSpec((tm, tn), lambda i, j: (i, j)),
                   pl.BlockSpec((tm // tt, Nt, tt), lambda i, j: (i, 0, 0))],
        out_shape=[jax.ShapeDtypeStruct((M, N), BF16),
                   jax.ShapeDtypeStruct((M // tt, Nt, tt), BF16)],
        scratch_shapes=[pltpu.VMEM((tm, K), BF16)],
        compiler_params=_params(("arbitrary", "arbitrary")),
    )(x, w, w_t)


def _t5_bucket_np(n):
    n = np.maximum(n, 0)
    max_exact = N_BUCKETS // 2
    nf = np.maximum(n, 1).astype(np.float32)
    large = max_exact + (np.log(nf / max_exact) / math.log(MAX_DISTANCE / max_exact)
                         * (N_BUCKETS - max_exact)).astype(np.int32)
    large = np.minimum(large, N_BUCKETS - 1)
    return np.where(n < max_exact, n, large).astype(np.int32)


def _dsa_n_off(blk):
    return 2 + -(-MAX_DISTANCE // blk)


def _dsa_bucket_tiles(blk):
    n_off = _dsa_n_off(blk)
    j = np.arange(blk)[None, :, None]
    i = np.arange(blk)[None, None, :]
    o = np.arange(n_off)[:, None, None]
    return _t5_bucket_np(i - j + blk * (n_off - 1 - o))


def _dsa_kernel(bucket_ref, relb_ref, q_ref, k_ref, vt_ref, qi_ref, kk_ref, wi_ref, o_ref,
                sc_ref, bias_ref, m_ref, l_ref, acc_ref, st_ref, mx_ref,
                *, blk, n_chunks, n_sel, n_off):
    b = pl.program_id(0)
    i = pl.program_id(1)
    q0 = i * blk
    nck = i + 1
    groups = blk // SUBLANES

    @pl.when(jnp.logical_and(b == 0, i == 0))
    def _build_bias():
        def head_body(h, _):
            for o in range(n_off):
                for rb in range(blk // LANES):
                    for cb in range(blk // LANES):
                        rs = slice(rb * LANES, (rb + 1) * LANES)
                        cs = slice(cb * LANES, (cb + 1) * LANES)
                        bk = bucket_ref[o, rs, cs]

                        def bucket_body(n, acc):
                            return jnp.where(bk == n, relb_ref[n, h], acc)

                        bias_ref[h, o, rs, cs] = lax.fori_loop(
                            0, N_BUCKETS, bucket_body, jnp.zeros((LANES, LANES), F32))
            return 0

        lax.fori_loop(0, A_HEADS, head_body, 0)

    lane = lax.broadcasted_iota(I32, (blk, LANES), 1)
    lo_half = lane < HALF
    krow = lax.broadcasted_iota(I32, (blk, blk), 0)
    qpos = q0 + lax.broadcasted_iota(I32, (1, blk), 1)

    def pair_split(ref, scale):
        out = []
        for p in range(N_PAIRS):
            v = ref[0, :, p * LANES:(p + 1) * LANES].astype(F32)
            if scale != 1.0:
                v = v * scale
            out.append(jnp.where(lo_half, v, 0.0).astype(BF16))
            out.append(jnp.where(lo_half, 0.0, v).astype(BF16))
        return out

    wi_t = wi_ref[0].astype(F32).T
    wrow = [wi_t[h:h + 1, :] * IDX_SCALE for h in range(IDX_HEADS)]
    qi_m = pair_split(qi_ref, 1.0)

    def score_chunk(c, _):
        c0 = pl.multiple_of(c * blk, blk)
        kk = kk_ref[0, pl.ds(c0, blk), :]
        acc = jnp.zeros((blk, blk), F32)
        for h in range(IDX_HEADS):
            s = lax.dot_general(kk, qi_m[h], NT_DIMS, preferred_element_type=F32)
            acc = acc + wrow[h] * jnp.maximum(s, 0.0)
        sc_ref[c] = jnp.where(c0 + krow <= qpos, acc, -jnp.inf)
        return 0

    lax.fori_loop(0, nck, score_chunk, 0)

    kt = jnp.minimum(qpos + 1, n_sel).astype(F32)

    def fold(fn, init):
        def body(c, acc):
            return fn(acc, sc_ref[c])
        return lax.fori_loop(0, nck, body, init)

    def part(x, op):
        y = op(x.reshape(REDUCE_CHAINS, groups // REDUCE_CHAINS, SUBLANES, blk), axis=1)
        return op(y, axis=0)

    def fin(x, op):
        return op(x, axis=0, keepdims=True)

    zeros8 = jnp.zeros((SUBLANES, blk), F32)
    pinf8 = jnp.full((SUBLANES, blk), jnp.inf, F32)

    def count_ge(th):
        return fin(fold(lambda a, s: a + part(jnp.where(s >= th, 1.0, 0.0), jnp.sum), zeros8),
                   jnp.sum)

    mn, mx = fold(lambda a, s: (
        jnp.minimum(a[0], part(jnp.where(s == -jnp.inf, jnp.inf, s), jnp.min)),
        jnp.maximum(a[1], part(s, jnp.max))), (pinf8, -pinf8))
    rmin = fin(mn, jnp.min)
    rmax = fin(mx, jnp.max)

    def bis_cond(st):
        it, lo, hi, clo = st
        return jnp.logical_and(it < BISECT_CAP, jnp.max(jnp.abs(clo - kt)) > 0.0)

    def halve(lo, hi, clo):
        mid = 0.5 * lo + 0.5 * hi
        c = count_ge(mid)
        active = clo != kt
        up = jnp.logical_and(active, c >= kt)
        dn = jnp.logical_and(active, c < kt)
        return jnp.where(up, mid, lo), jnp.where(dn, mid, hi), jnp.where(up, c, clo)

    def bis_body(st):
        it, lo, hi, clo = st
        return (it + 2,) + halve(*halve(lo, hi, clo))

    _, lo, _, _ = lax.while_loop(
        bis_cond, bis_body,
        (jnp.int32(0), rmin, rmax + jnp.maximum(1.0, jnp.abs(rmax) * 2.0 ** -20), (qpos + 1).astype(F32)))

    def stats(lo_):
        a_ = fin(fold(lambda a, s: jnp.minimum(a, part(jnp.where(s >= lo_, s, jnp.inf), jnp.min)),
                      pinf8), jnp.min)
        cg, ct, nx = fold(
            lambda a, s: (a[0] + part(jnp.where(s > a_, 1.0, 0.0), jnp.sum),
                               a[1] + part(jnp.where(s == a_, 1.0, 0.0), jnp.sum),
                               jnp.minimum(a[2], part(jnp.where(s > a_, s, jnp.inf), jnp.min))),
            (zeros8, zeros8, pinf8))
        return a_, fin(cg, jnp.sum), fin(ct, jnp.sum), fin(nx, jnp.min)

    def fin_cond(st):
        return st[0]

    def fin_body(st):
        _, lo_, _, _ = st
        a_, cgt_, nt_, nxt_ = stats(lo_)
        bad = cgt_ >= kt
        return (jnp.max(jnp.where(bad, 1.0, 0.0)) > 0.0, jnp.where(bad, nxt_, a_), cgt_, nt_)

    _, a, cgt, nties = lax.while_loop(fin_cond, fin_body, (jnp.bool_(True), lo, kt, kt))
    need = kt - cgt
    excess = jnp.max(jnp.where(nties > need, 1.0, 0.0)) > 0.0

    def mask_plain():
        def body(c, _):
            sc_ref[c] = jnp.where(sc_ref[c] >= a, 0.0, NEG)
            return 0
        lax.fori_loop(0, nck, body, 0)

    def mask_ties():
        upto = (krow >= lax.broadcasted_iota(I32, (blk, blk), 1)).astype(BF16)

        def body(c, seen):
            s = sc_ref[c]
            tie = s == a
            rank = jnp.dot(upto, jnp.where(tie, 1.0, 0.0).astype(BF16),
                           preferred_element_type=F32) + seen
            sel = jnp.logical_or(s > a, jnp.logical_and(tie, rank <= need))
            sc_ref[c] = jnp.where(sel, 0.0, NEG)
            return rank[blk - 1:blk, :]

        lax.fori_loop(0, nck, body, jnp.zeros((1, blk), F32))

    lax.cond(excess, mask_ties, mask_plain)

    m_ref[...] = jnp.full(m_ref.shape, NEG, F32)
    l_ref[...] = jnp.zeros(l_ref.shape, F32)
    acc_ref[...] = jnp.zeros(acc_ref.shape, F32)
    q_m = pair_split(q_ref, QK_SCALE)

    def stage_logits(c, slot):
        c = jnp.minimum(c, n_chunks - 1)
        c0 = pl.multiple_of(c * blk, blk)
        madd = sc_ref[c]
        o_idx = jnp.clip(c - i + (n_off - 1), 0, n_off - 1)
        for p in range(N_PAIRS):
            k2 = k_ref[0, pl.ds(c0, blk), p * LANES:(p + 1) * LANES]
            for hh in range(2):
                h = 2 * p + hh
                s = lax.dot_general(k2, q_m[h], NT_DIMS, preferred_element_type=F32)
                s = s + bias_ref[h, o_idx] + madd
                st_ref[slot, h] = s
                mx_ref[slot, h] = fin(part(s, jnp.max), jnp.max)

    def stage_values(c, slot):
        for p in range(N_PAIRS):
            vt2 = vt_ref[0, c, p * LANES:(p + 1) * LANES, :]
            for hh in range(2):
                h = 2 * p + hh
                m_old = m_ref[h]
                m_new = jnp.maximum(m_old, mx_ref[slot, h])
                alpha = jnp.exp(m_old - m_new)
                pexp = jnp.exp(st_ref[slot, h] - m_new)
                l_ref[h] = alpha * l_ref[h] + fin(part(pexp, jnp.sum), jnp.sum)
                acc_ref[h] = alpha * acc_ref[h] + jnp.dot(vt2, pexp.astype(BF16),
                                                          preferred_element_type=F32)
                m_ref[h] = m_new

    stage_logits(0, 0)

    def att_pair(pp, _):
        c = 2 * pp
        stage_logits(c + 1, 1)
        stage_values(c, 0)
        stage_logits(c + 2, 0)
        stage_values(c + 1, 1)
        return 0

    lax.fori_loop(0, nck // 2, att_pair, 0)

    @pl.when(lax.rem(nck, 2) == 1)
    def _last_chunk():
        stage_values(nck - 1, 0)

    lo_rows = lax.broadcasted_iota(I32, (LANES, blk), 0) < HALF
    for p in range(N_PAIRS):
        oa = acc_ref[2 * p] / l_ref[2 * p]
        ob = acc_ref[2 * p + 1] / l_ref[2 * p + 1]
        o_ref[0, :, p * LANES:(p + 1) * LANES] = jnp.where(lo_rows, oa, ob).T.astype(o_ref.dtype)


def _dsa(proj3, vt4, rel_bias):
    B, S, _ = proj3.shape
    blk = DSA_BLK
    n_off = _dsa_n_off(blk)
    n_sel = min(TOPK_MAX, S // 4)
    bucket = jnp.asarray(_dsa_bucket_tiles(blk))
    n_chunks = S // blk
    assert S % blk == 0
    kern = functools.partial(_dsa_kernel, blk=blk, n_chunks=n_chunks, n_sel=n_sel, n_off=n_off)
    return pl.pallas_call(
        kern,
        grid=(B, S // blk),
        in_specs=[
            pl.BlockSpec((n_off, blk, blk), lambda b, i: (0, 0, 0)),
            pl.BlockSpec(memory_space=pltpu.SMEM),
            pl.BlockSpec((1, blk, ATT_W), lambda b, i: (b, i, COL_QA // ATT_W)),
            pl.BlockSpec((1, S, ATT_W), lambda b, i: (b, 0, COL_KA // ATT_W)),
            pl.BlockSpec((1, S // blk, ATT_W, blk), lambda b, i: (b, 0, 0, 0)),
            pl.BlockSpec((1, blk, ATT_W), lambda b, i: (b, i, COL_QI // ATT_W)),
            pl.BlockSpec((1, S, LANES), lambda b, i: (b, 0, COL_KK // LANES)),
            pl.BlockSpec((1, blk, LANES), lambda b, i: (b, i, COL_WI // LANES)),
        ],
        out_specs=pl.BlockSpec((1, blk, ATT_W), lambda b, i: (b, i, 0)),
        out_shape=jax.ShapeDtypeStruct((B, S, ATT_W), BF16),
        scratch_shapes=[
            pltpu.VMEM((S // blk, blk, blk), F32),
            pltpu.VMEM((A_HEADS, n_off, blk, blk), F32),
            pltpu.VMEM((A_HEADS, 1, blk), F32),
            pltpu.VMEM((A_HEADS, 1, blk), F32),
            pltpu.VMEM((A_HEADS, LANES, blk), F32),
            pltpu.VMEM((2, A_HEADS, blk, blk), F32),
            pltpu.VMEM((2, A_HEADS, 1, blk), F32),
        ],
        compiler_params=_params(("arbitrary", "arbitrary")),
    )(bucket, rel_bias, proj3, proj3, vt4, proj3, proj3, proj3)


def _sb_kernel(q_ref, k_ref, v_ref, o_ref, hl_ref, z_ref, *, t):
    i = pl.program_id(2)
    n = i + 1
    lane = lax.broadcasted_iota(I32, (t, LANES), 1)
    lo_half = lane < HALF
    q_m = []
    for pr in range(SB_PAIRS):
        q2 = q_ref[0, :, pr * LANES:(pr + 1) * LANES].astype(F32) * QK_SCALE
        q_m += [jnp.where(lo_half, q2, 0.0).astype(BF16), jnp.where(lo_half, 0.0, q2).astype(BF16)]
    heads = 2 * SB_PAIRS
    r = lax.broadcasted_iota(I32, (t, t), 0)
    cidx = lax.broadcasted_iota(I32, (t, t), 1)
    neg_from = jnp.where(r >= cidx, -1.0, 0.0).astype(BF16)
    diff = cidx - r

    def stage_terms(step, slot, diagonal):
        c0 = pl.multiple_of(jnp.maximum(i - step, 0) * t, t)
        if diagonal:
            keep = diff < 0
        for hh in range(heads):
            pr = hh // 2
            k2 = k_ref[0, pl.ds(c0, t), pr * LANES:(pr + 1) * LANES]
            z = lax.dot_general(q_m[hh], k2, NT_DIMS, preferred_element_type=F32)
            sp = jnp.maximum(z, 0.0) + jnp.log(1.0 + jnp.exp(-jnp.abs(z)))
            if diagonal:
                sp = jnp.where(keep, sp, 0.0)
                z = jnp.where(keep, z, NEG)
            hi = sp.astype(BF16)
            hl_ref[slot, 2 * hh] = hi
            hl_ref[slot, 2 * hh + 1] = (sp - hi.astype(F32)).astype(BF16)
            z_ref[slot, hh] = z

    def stage_apply(step, slot, carry):
        c0 = pl.multiple_of((i - step) * t, t)
        cum4 = jnp.dot(hl_ref[slot].reshape(2 * heads * t, t), neg_from, preferred_element_type=F32)
        out = []
        for hh in range(heads):
            pr = hh // 2
            v2 = v_ref[0, pl.ds(c0, t), pr * LANES:(pr + 1) * LANES]
            car, acc = carry[hh]
            cum = cum4[(2 * hh) * t:(2 * hh + 1) * t] + cum4[(2 * hh + 1) * t:(2 * hh + 2) * t]
            w = jnp.exp(z_ref[slot, hh] + cum + car)
            acc = acc + jnp.dot(w.astype(BF16), v2, preferred_element_type=F32)
            out.append((car + cum[:, 0:1], acc))
        return tuple(out)

    z1 = jnp.zeros((t, 1), F32)
    za = jnp.zeros((t, LANES), F32)
    stage_terms(0, 0, True)

    def pair_body(pp, carry):
        step = 2 * pp
        stage_terms(step + 1, 1, False)
        carry = stage_apply(step, 0, carry)
        stage_terms(step + 2, 0, False)
        return stage_apply(step + 1, 1, carry)

    carry = lax.fori_loop(0, n // 2, pair_body, ((z1, za),) * heads)
    carry = lax.cond(lax.rem(n, 2) == 1, lambda c: stage_apply(n - 1, 0, c), lambda c: c, carry)
    for pr in range(SB_PAIRS):
        o_ref[0, :, pr * LANES:(pr + 1) * LANES] = jnp.where(
            lo_half, carry[2 * pr][1], carry[2 * pr + 1][1]).astype(o_ref.dtype)


def _stick_breaking(proj3):
    B, S, _ = proj3.shape
    t = SB_T
    w = SB_PAIRS * LANES
    qb, kb, vb = COL_QB // w, COL_KB // w, COL_VB // w
    return pl.pallas_call(
        functools.partial(_sb_kernel, t=t),
        grid=(B, N_PAIRS // SB_PAIRS, S // t),
        in_specs=[
            pl.BlockSpec((1, t, w), lambda b, p, i: (b, i, qb + p)),
            pl.BlockSpec((1, S, w), lambda b, p, i: (b, 0, kb + p)),
            pl.BlockSpec((1, S, w), lambda b, p, i: (b, 0, vb + p)),
        ],
        out_specs=pl.BlockSpec((1, t, w), lambda b, p, i: (b, i, p)),
        out_shape=jax.ShapeDtypeStruct((B, S, ATT_W), BF16),
        scratch_shapes=[
            pltpu.VMEM((2, 4 * SB_PAIRS, t, t), BF16),
            pltpu.VMEM((2, 2 * SB_PAIRS, t, t), F32),
        ],
        compiler_params=_params(("arbitrary", "arbitrary", "arbitrary")),
    )(proj3, proj3, proj3)


def _layer_norm(r, g, b):
    mu = jnp.mean(r, axis=-1, keepdims=True)
    d = r - mu
    var = jnp.mean(d * d, axis=-1, keepdims=True)
    return d * lax.rsqrt(var + LN_EPS) * g + b


def _split_bf16(v):
    hi = v.astype(BF16)
    return hi, (v - hi.astype(F32)).astype(BF16)


def _merge_kernel(x_ref, ya_ref, yb_ref, wg_ref, wa_ref, wb_ref, wo_ref, g_ref, b_ref,
                  wr_ref, br_ref, h_ref, hp_ref, e_ref, p_ref, *, alpha, d):
    pa = jnp.dot(ya_ref[...], wa_ref[...], preferred_element_type=F32)
    pb = jnp.dot(yb_ref[...], wb_ref[...], preferred_element_type=F32)
    gates = jnp.dot(x_ref[...].astype(BF16), wg_ref[...], preferred_element_type=F32)
    merged = jax.nn.sigmoid(gates[:, :d]) * pa + jax.nn.sigmoid(gates[:, d:]) * pb
    m = jnp.dot(merged.astype(BF16), wo_ref[...], preferred_element_type=F32)
    h = _layer_norm(alpha * x_ref[...] + m, g_ref[...], b_ref[...])
    h_ref[...] = h
    bits = pltpu.bitcast(h.astype(BF16).astype(F32), jnp.uint32)
    hp_ref[...] = bits[:, :d // 2] | (bits[:, d // 2:] >> 16)

    h_hi, h_lo = _split_bf16(h)
    w_hi, w_lo = _split_bf16(wr_ref[...])
    logit = (lax.dot_general(w_hi, h_hi, NT_DIMS, preferred_element_type=F32)
             + lax.dot_general(w_hi, h_lo, NT_DIMS, preferred_element_type=F32)
             + lax.dot_general(w_lo, h_hi, NT_DIMS, preferred_element_type=F32)) + br_ref[...]
    eid = lax.broadcasted_iota(I32, logit.shape, 0)
    vals, ids = [], []
    for _ in range(TOP_K):
        mx = jnp.max(logit, axis=0, keepdims=True)
        am = jnp.min(jnp.where(logit == mx, eid, N_EXPERTS), axis=0, keepdims=True)
        vals.append(mx)
        ids.append(am)
        logit = jnp.where(eid == am, -jnp.inf, logit)
    ex = [jnp.exp(v - vals[0]) for v in vals]
    den = ex[0] + ex[1] + ex[2] + ex[3]
    for k in range(TOP_K):
        e_ref[k:k + 1, :] = ids[k]
        p_ref[k:k + 1, :] = ex[k] / den


def _merge(x2, ya, yb, wg, wa, wb, wo, g, b, wr_t, br, alpha):
    T, D = x2.shape
    tm = MERGE_TM
    row = lambda i: (i, 0)
    fixed = lambda i: (0, 0)
    return pl.pallas_call(
        functools.partial(_merge_kernel, alpha=alpha, d=D),
        grid=(T // tm,),
        in_specs=[
            pl.BlockSpec((tm, D), row),
            pl.BlockSpec((tm, ATT_W), row),
            pl.BlockSpec((tm, ATT_W), row),
            pl.BlockSpec((D, 2 * D), fixed),
            pl.BlockSpec((ATT_W, D), fixed),
            pl.BlockSpec((ATT_W, D), fixed),
            pl.BlockSpec((D, D), fixed),
            pl.BlockSpec((1, D), fixed),
            pl.BlockSpec((1, D), fixed),
            pl.BlockSpec((N_EXPERTS, D), fixed),
            pl.BlockSpec((N_EXPERTS, 1), fixed),
        ],
        out_specs=[
            pl.BlockSpec((tm, D), row),
            pl.BlockSpec((tm, D // 2), row),
            pl.BlockSpec((TOP_K, tm), lambda i: (0, i)),
            pl.BlockSpec((TOP_K, tm), lambda i: (0, i)),
        ],
        out_shape=[
            jax.ShapeDtypeStruct((T, D), F32),
            jax.ShapeDtypeStruct((T, D // 2), jnp.uint32),
            jax.ShapeDtypeStruct((TOP_K, T), I32),
            jax.ShapeDtypeStruct((TOP_K, T), F32),
        ],
        compiler_params=_params(("arbitrary",)),
    )(x2, ya, yb, wg, wa, wb, wo, g, b, wr_t, br)


def _sc_gather_rows(table, idx):
    n = idx.shape[0]
    d = table.shape[1]
    info = plsc.get_sparse_core_info()
    n_cores, n_sub = info.num_cores, info.num_subcores
    per_w = n // (n_cores * n_sub)
    c = SC_GATHER_BYTES // (d * table.dtype.itemsize)
    n_g = per_w // c
    assert n == per_w * n_cores * n_sub and per_w == n_g * c and n_g % 2 == 0 and n_g >= 2
    mesh = plsc.VectorSubcoreMesh(core_axis_name="c", subcore_axis_name="s")

    @functools.partial(
        pl.kernel, mesh=mesh, out_type=jax.ShapeDtypeStruct((n, d), table.dtype),
        scratch_types=[pltpu.VMEM((per_w,), I32), pltpu.VMEM((2, c, d), table.dtype),
                       pltpu.SemaphoreType.DMA((2,)), pltpu.SemaphoreType.DMA((2,))])
    def gather_kernel(table_hbm, idx_hbm, out_hbm, idx_v, rows_v, gsem, wsem):
        base = (lax.axis_index("s") * n_cores + lax.axis_index("c")) * per_w
        pltpu.sync_copy(idx_hbm.at[pl.ds(base, per_w)], idx_v)

        def gather(g, b):
            return pltpu.make_async_copy(table_hbm.at[idx_v.at[pl.ds(g * c, c)]], rows_v.at[b],
                                         gsem.at[b])

        def write(g, b):
            return pltpu.make_async_copy(rows_v.at[b], out_hbm.at[pl.ds(base + g * c, c)], wsem.at[b])

        gather(0, 0).start()

        @pl.loop(0, n_g, step=2)
        def _ring(g0):
            for b in range(2):
                g = g0 + b

                @pl.when(g + 1 < n_g)
                def _next():
                    @pl.when(g >= 1)
                    def _buffer_free():
                        write(g - 1, 1 - b).wait()
                    gather(g + 1, 1 - b).start()

                gather(g, b).wait()
                write(g, b).start()

        write(n_g - 2, 0).wait()
        write(n_g - 1, 1).wait()

    return gather_kernel(table, idx)


def _moe_kernel(blk_e_ref, nused_ref, x_ref, wgu_ref, bgu_ref, wdn_ref, bdn_ref, o_ref,
                wgu_s, wdn_s, *, f):
    i = pl.program_id(0)
    nused = nused_ref[0]

    @pl.when(i < nused)
    def _compute():
        changed = jnp.logical_or(i == 0, blk_e_ref[i] != blk_e_ref[jnp.maximum(i - 1, 0)])

        @pl.when(changed)
        def _cast_weights():
            wgu_s[...] = wgu_ref[0].astype(BF16)
            wdn_s[...] = wdn_ref[0].astype(BF16)

        xw = x_ref[...]
        x = jnp.concatenate([pltpu.bitcast(xw & jnp.uint32(0xFFFF0000), F32),
                             pltpu.bitcast(xw << 16, F32)], axis=1).astype(BF16)
        hgu = jnp.dot(x, wgu_s[...], preferred_element_type=F32) + bgu_ref[0]
        a = jnp.minimum(hgu[:, :f], SWIGLU_LIMIT)
        u = jnp.clip(hgu[:, f:], -SWIGLU_LIMIT, SWIGLU_LIMIT)
        glu = a * jax.nn.sigmoid(a * SWIGLU_ALPHA)
        o_ref[...] = jnp.dot(((u + 1.0) * glu).astype(BF16), wdn_s[...],
                             preferred_element_type=F32) + bdn_ref[0]

    @pl.when(i >= nused)
    def _unused_block():
        o_ref[...] = jnp.zeros(o_ref.shape, o_ref.dtype)


def _moe_ffn(xs, blk_e, nused, w_gu, b_gu, w_dn, b_dn):
    P = xs.shape[0]
    E, D, F2 = w_gu.shape
    f = F2 // 2
    blk = MOE_BLK
    nb = P // blk
    used_block = lambda i, be, nu: (jnp.minimum(i, nu[0] - 1), 0)
    grid_spec = pltpu.PrefetchScalarGridSpec(
        num_scalar_prefetch=2,
        grid=(nb,),
        in_specs=[
            pl.BlockSpec((blk, D // 2), used_block),
            pl.BlockSpec((1, D, F2), lambda i, be, nu: (be[i], 0, 0)),
            pl.BlockSpec((1, 1, F2), lambda i, be, nu: (be[i], 0, 0)),
            pl.BlockSpec((1, f, D), lambda i, be, nu: (be[i], 0, 0)),
            pl.BlockSpec((1, 1, D), lambda i, be, nu: (be[i], 0, 0)),
        ],
        out_specs=pl.BlockSpec((blk, D), lambda i, be, nu: (i, 0)),
        scratch_shapes=[
            pltpu.VMEM((D, F2), BF16),
            pltpu.VMEM((f, D), BF16),
        ],
    )
    return pl.pallas_call(
        functools.partial(_moe_kernel, f=f),
        grid_spec=grid_spec,
        out_shape=jax.ShapeDtypeStruct((P, D), F32),
        compiler_params=_params(("arbitrary",)),
    )(blk_e, nused, xs, w_gu, b_gu.reshape(E, 1, F2), w_dn, b_dn.reshape(E, 1, D))


def _comb_kernel(h_ref, p_ref, y_ref, g_ref, b_ref, o_ref, *, alpha):
    gate = p_ref[...]
    fsum = ((y_ref[0] * gate[:, 0:1] + y_ref[1] * gate[:, 1:2])
            + (y_ref[2] * gate[:, 2:3] + y_ref[3] * gate[:, 3:4]))
    o_ref[...] = _layer_norm(alpha * h_ref[...] + fsum, g_ref[...], b_ref[...])


def _combine(h, y4, top_p, g, b, alpha):
    T, D = h.shape
    tm = COMB_TM
    return pl.pallas_call(
        functools.partial(_comb_kernel, alpha=alpha),
        grid=(T // tm,),
        in_specs=[
            pl.BlockSpec((tm, D), lambda i: (i, 0)),
            pl.BlockSpec((tm, TOP_K), lambda i: (i, 0)),
            pl.BlockSpec((TOP_K, tm, D), lambda i: (0, i, 0)),
            pl.BlockSpec((1, D), lambda i: (0, 0)),
            pl.BlockSpec((1, D), lambda i: (0, 0)),
        ],
        out_specs=pl.BlockSpec((tm, D), lambda i: (i, 0)),
        out_shape=jax.ShapeDtypeStruct((T, D), F32),
        compiler_params=_params(("arbitrary",)),
    )(h, top_p.T, y4, g, b)


def _route(top_e, blk):
    K, T = top_e.shape
    N = K * T
    flat_e = top_e.reshape(N)
    experts = jnp.arange(N_EXPERTS, dtype=I32)
    order = jnp.argsort(flat_e, stable=True).astype(I32)
    inv = jnp.argsort(order).astype(I32)
    onehot = flat_e[:, None] == experts[None, :]
    counts = jnp.sum(onehot, axis=0, dtype=I32)
    padded = (counts + blk - 1) // blk * blk
    pends = jnp.cumsum(padded)
    offs = jnp.cumsum(counts) - counts
    shift = (pends - padded) - offs
    pos = inv + jnp.sum(jnp.where(onehot, shift[None, :], 0), axis=1, dtype=I32)
    P = N + N_EXPERTS * blk
    nb = P // blk
    blk_start = jnp.arange(nb, dtype=I32) * blk
    blk_e = jnp.minimum(jnp.sum(pends[None, :] <= blk_start[:, None], axis=1, dtype=I32), N_EXPERTS - 1)
    j = (blk_start - shift[blk_e])[:, None] + jnp.arange(blk, dtype=I32)[None, :]
    valid = j < (offs + counts)[blk_e][:, None]
    src = order[jnp.clip(j, 0, N - 1)]
    row_tok = jnp.where(valid, src % T, j % T).reshape(P)
    nused = (pends[-1:] // blk).astype(I32)
    return blk_e, nused, row_tok, pos.reshape(K, T)


def _projection_weights(w_in_l):
    sizes = (ATT_W, ATT_W, ATT_W, IDX_HEADS * IDX_DIM, IDX_DIM, IDX_HEADS, ATT_W, ATT_W, ATT_W)
    offs = np.concatenate([[0], np.cumsum(sizes)])
    qa, ka, va, qi, ki, wi, qb, kb, vb = (w_in_l[:, offs[n]:offs[n + 1]] for n in range(9))
    pad_wi = jnp.zeros((w_in_l.shape[0], LANES - IDX_HEADS), w_in_l.dtype)
    w_att = jnp.concatenate([qa, ka, qi, qb, kb, vb, ki, ki, wi, pad_wi], axis=1).astype(BF16)
    w_va_t = va.T.astype(BF16)
    w_gate = w_in_l[:, offs[9]:].astype(BF16)
    return w_att, w_va_t, w_gate


def kernel(x, w_in, w_branch_a, w_branch_b, w_out, rel_bias, ln1_g, ln1_b, w_router, b_router,
           w_gate_up, b_gate_up, w_down, b_down, ln2_g, ln2_b):
    B, S, D = x.shape
    depth = w_in.shape[0]
    alpha = (2 * depth) ** 0.25
    T = B * S
    h = x.reshape(T, D)
    for l in range(depth):
        w_att, w_va_t, w_gate = _projection_weights(w_in[l])
        proj, vt = _projection(h, w_att, w_va_t, min(T, PROJ_TM), PROJ_TN, DSA_BLK)
        proj = proj.reshape(B, S, ATT_COLS)
        vt = vt.reshape(B, S // DSA_BLK, ATT_W, DSA_BLK)
        ya = _dsa(proj, vt, rel_bias).reshape(T, ATT_W)
        yb = _stick_breaking(proj).reshape(T, ATT_W)
        h1, h1_packed, top_e, top_p = _merge(
            h, ya, yb, w_gate, w_branch_a[l].astype(BF16), w_branch_b[l].astype(BF16),
            w_out[l].astype(BF16), ln1_g[l].reshape(1, D), ln1_b[l].reshape(1, D),
            w_router[l].T, b_router[l].reshape(N_EXPERTS, 1), alpha)
        blk_e, nused, row_tok, pos = _route(top_e, MOE_BLK)
        xs = _sc_gather_rows(h1_packed, row_tok)
        ys = _moe_ffn(xs, blk_e, nused, w_gate_up[l], b_gate_up[l], w_down[l], b_down[l])
        y4 = _sc_gather_rows(ys, pos.reshape(TOP_K * T)).reshape(TOP_K, T, D)
        h = _combine(h1, y4, top_p, ln2_g[l].reshape(1, D), ln2_b[l].reshape(1, D), alpha)
    return h.reshape(B, S, D)
```

```python
import functools
import math

import numpy as np
import jax
import jax.numpy as jnp
from jax import lax
from jax.experimental import pallas as pl
from jax.experimental.pallas import tpu as pltpu
from jax.experimental.pallas import tpu_sc as plsc

F32 = jnp.float32
BF16 = jnp.bfloat16
I32 = jnp.int32

A_HEADS = 8
HEAD_DIM = 64
ATT_W = A_HEADS * HEAD_DIM
IDX_HEADS = 8
IDX_DIM = 64
IDX_SCALE = (IDX_HEADS * IDX_DIM) ** -0.5
TOPK_MAX = 256
N_BUCKETS = 32
MAX_DISTANCE = 128
N_EXPERTS = 32
TOP_K = 4
SWIGLU_LIMIT = 7.0
SWIGLU_ALPHA = 1.702
LN_EPS = 1e-5
QK_SCALE = HEAD_DIM ** -0.5

LANES = 128
SUBLANES = 8
HALF = LANES // 2
N_PAIRS = A_HEADS // 2
VMEM_LIMIT = 56 * 1024 * 1024

DSA_BLK = 256
SB_T = 256
SB_PAIRS = 4
PROJ_TM = 1024
MERGE_TM = 512
MOE_BLK = 512
COMB_TM = 512
SC_GATHER_BYTES = 128 * 1024
REDUCE_CHAINS = 8
BISECT_CAP = 24
NEG = -1e30

COL_QA, COL_KA, COL_QI, COL_QB, COL_KB, COL_VB = (g * ATT_W for g in range(6))
COL_KK = 6 * ATT_W
COL_WI = COL_KK + LANES
ATT_COLS = COL_WI + LANES
PROJ_TN = ATT_COLS // 2

NT_DIMS = (((1,), (1,)), ((), ()))


def _params(sem, vmem=VMEM_LIMIT):
    return pltpu.CompilerParams(dimension_semantics=sem, vmem_limit_bytes=vmem)


def _proj_kernel(x_ref, w_ref, wt_ref, o_ref, ot_ref, xb_ref, *, tt):
    @pl.when(pl.program_id(1) == 0)
    def _row_tile_start():
        xb_ref[...] = x_ref[...].astype(BF16)
        for r in range(ot_ref.shape[0]):
            ot_ref[r] = lax.dot_general(wt_ref[...], xb_ref[r * tt:(r + 1) * tt, :], NT_DIMS,
                                        preferred_element_type=F32).astype(ot_ref.dtype)

    o_ref[...] = jnp.dot(xb_ref[...], w_ref[...], preferred_element_type=F32).astype(o_ref.dtype)


def _projection(x, w, w_t, tm, tn, tt):
    M, K = x.shape
    N = w.shape[1]
    Nt = w_t.shape[0]
    return pl.pallas_call(
        functools.partial(_proj_kernel, tt=tt),
        grid=(M // tm, N // tn),
        in_specs=[pl.BlockSpec((tm, K), lambda i, j: (i, 0)),
                  pl.BlockSpec((K, tn), lambda i, j: (0, j)),
                  pl.BlockSpec((Nt, K), lambda i, j: (0, 0))],
        out_specs=[pl.BlockSpec((tm, tn), lambda i, j: (i, j)),
                   pl.BlockSpec((tm // tt, Nt, tt), lambda i, j: (i, 0, 0))],
        out_shape=[jax.ShapeDtypeStruct((M, N), BF16),
                   jax.ShapeDtypeStruct((M // tt, Nt, tt), BF16)],
        scratch_shapes=[pltpu.VMEM((tm, K), BF16)],
        compiler_params=_params(("arbitrary", "arbitrary")),
    )(x, w, w_t)


def _t5_bucket_np(n):
    n = np.maximum(n, 0)
    max_exact = N_BUCKETS // 2
    nf = np.maximum(n, 1).astype(np.float32)
    large = max_exact + (np.log(nf / max_exact) / math.log(MAX_DISTANCE / max_exact)
                         * (N_BUCKETS - max_exact)).astype(np.int32)
    large = np.minimum(large, N_BUCKETS - 1)
    return np.where(n < max_exact, n, large).astype(np.int32)


def _dsa_n_off(blk):
    return 2 + -(-MAX_DISTANCE // blk)


def _dsa_bucket_tiles(blk):
    n_off = _dsa_n_off(blk)
    j = np.arange(blk)[None, :, None]
    i = np.arange(blk)[None, None, :]
    o = np.arange(n_off)[:, None, None]
    return _t5_bucket_np(i - j + blk * (n_off - 1 - o))


def _dsa_kernel(bucket_ref, relb_ref, q_ref, k_ref, vt_ref, qi_ref, kk_ref, wi_ref, o_ref,
                sc_ref, bias_ref, m_ref, l_ref, acc_ref, st_ref, mx_ref,
                *, blk, n_chunks, n_sel, n_off):
    b = pl.program_id(0)
    i = pl.program_id(1)
    q0 = i * blk
    nck = i + 1
    groups = blk // SUBLANES

    @pl.when(jnp.logical_and(b == 0, i == 0))
    def _build_bias():
        def head_body(h, _):
            for o in range(n_off):
                for rb in range(blk // LANES):
                    for cb in range(blk // LANES):
                        rs = slice(rb * LANES, (rb + 1) * LANES)
                        cs = slice(cb * LANES, (cb + 1) * LANES)
                        bk = bucket_ref[o, rs, cs]

                        def bucket_body(n, acc):
                            return jnp.where(bk == n, relb_ref[n, h], acc)

                        bias_ref[h, o, rs, cs] = lax.fori_loop(
                            0, N_BUCKETS, bucket_body, jnp.zeros((LANES, LANES), F32))
            return 0

        lax.fori_loop(0, A_HEADS, head_body, 0)

    lane = lax.broadcasted_iota(I32, (blk, LANES), 1)
    lo_half = lane < HALF
    krow = lax.broadcasted_iota(I32, (blk, blk), 0)
    qpos = q0 + lax.broadcasted_iota(I32, (1, blk), 1)

    def pair_split(ref, scale):
        out = []
        for p in range(N_PAIRS):
            v = ref[0, :, p * LANES:(p + 1) * LANES].astype(F32)
            if scale != 1.0:
                v = v * scale
            out.append(jnp.where(lo_half, v, 0.0).astype(BF16))
            out.append(jnp.where(lo_half, 0.0, v).astype(BF16))
        return out

    wi_t = wi_ref[0].astype(F32).T
    wrow = [wi_t[h:h + 1, :] * IDX_SCALE for h in range(IDX_HEADS)]
    qi_m = pair_split(qi_ref, 1.0)

    def score_chunk(c, _):
        c0 = pl.multiple_of(c * blk, blk)
        kk = kk_ref[0, pl.ds(c0, blk), :]
        acc = jnp.zeros((blk, blk), F32)
        for h in range(IDX_HEADS):
            s = lax.dot_general(kk, qi_m[h], NT_DIMS, preferred_element_type=F32)
            acc = acc + wrow[h] * jnp.maximum(s, 0.0)
        sc_ref[c] = jnp.where(c0 + krow <= qpos, acc, -jnp.inf)
        return 0

    lax.fori_loop(0, nck, score_chunk, 0)

    kt = jnp.minimum(qpos + 1, n_sel).astype(F32)

    def fold(fn, init):
        def body(c, acc):
            return fn(acc, sc_ref[c])
        return lax.fori_loop(0, nck, body, init)

    def part(x, op):
        y = op(x.reshape(REDUCE_CHAINS, groups // REDUCE_CHAINS, SUBLANES, blk), axis=1)
        return op(y, axis=0)

    def fin(x, op):
        return op(x, axis=0, keepdims=True)

    zeros8 = jnp.zeros((SUBLANES, blk), F32)
    pinf8 = jnp.full((SUBLANES, blk), jnp.inf, F32)

    def count_ge(th):
        return fin(fold(lambda a, s: a + part(jnp.where(s >= th, 1.0, 0.0), jnp.sum), zeros8),
                   jnp.sum)

    mn, mx = fold(lambda a, s: (
        jnp.minimum(a[0], part(jnp.where(s == -jnp.inf, jnp.inf, s), jnp.min)),
        jnp.maximum(a[1], part(s, jnp.max))), (pinf8, -pinf8))
    rmin = fin(mn, jnp.min)
    rmax = fin(mx, jnp.max)

    def bis_cond(st):
        it, lo, hi, clo = st
        return jnp.logical_and(it < BISECT_CAP, jnp.max(jnp.abs(clo - kt)) > 0.0)

    def halve(lo, hi, clo):
        mid = 0.5 * lo + 0.5 * hi
        c = count_ge(mid)
        active = clo != kt
        up = jnp.logical_and(active, c >= kt)
        dn = jnp.logical_and(active, c < kt)
        return jnp.where(up, mid, lo), jnp.where(dn, mid, hi), jnp.where(up, c, clo)

    def bis_body(st):
        it, lo, hi, clo = st
        return (it + 2,) + halve(*halve(lo, hi, clo))

    _, lo, _, _ = lax.while_loop(
        bis_cond, bis_body,
        (jnp.int32(0), rmin, rmax + jnp.maximum(1.0, jnp.abs(rmax) * 2.0 ** -20), (qpos + 1).astype(F32)))

    def stats(lo_):
        a_ = fin(fold(lambda a, s: jnp.minimum(a, part(jnp.where(s >= lo_, s, jnp.inf), jnp.min)),
                      pinf8), jnp.min)
        cg, ct, nx = fold(
            lambda a, s: (a[0] + part(jnp.where(s > a_, 1.0, 0.0), jnp.sum),
                               a[1] + part(jnp.where(s == a_, 1.0, 0.0), jnp.sum),
                               jnp.minimum(a[2], part(jnp.where(s > a_, s, jnp.inf), jnp.min))),
            (zeros8, zeros8, pinf8))
        return a_, fin(cg, jnp.sum), fin(ct, jnp.sum), fin(nx, jnp.min)

    def fin_cond(st):
        return st[0]

    def fin_body(st):
        _, lo_, _, _ = st
        a_, cgt_, nt_, nxt_ = stats(lo_)
        bad = cgt_ >= kt
        return (jnp.max(jnp.where(bad, 1.0, 0.0)) > 0.0, jnp.where(bad, nxt_, a_), cgt_, nt_)

    _, a, cgt, nties = lax.while_loop(fin_cond, fin_body, (jnp.bool_(True), lo, kt, kt))
    need = kt - cgt
    excess = jnp.max(jnp.where(nties > need, 1.0, 0.0)) > 0.0

    def mask_plain():
        def body(c, _):
            sc_ref[c] = jnp.where(sc_ref[c] >= a, 0.0, NEG)
            return 0
        lax.fori_loop(0, nck, body, 0)

    def mask_ties():
        upto = (krow >= lax.broadcasted_iota(I32, (blk, blk), 1)).astype(BF16)

        def body(c, seen):
            s = sc_ref[c]
            tie = s == a
            rank = jnp.dot(upto, jnp.where(tie, 1.0, 0.0).astype(BF16),
                           preferred_element_type=F32) + seen
            sel = jnp.logical_or(s > a, jnp.logical_and(tie, rank <= need))
            sc_ref[c] = jnp.where(sel, 0.0, NEG)
            return rank[blk - 1:blk, :]

        lax.fori_loop(0, nck, body, jnp.zeros((1, blk), F32))

    lax.cond(excess, mask_ties, mask_plain)

    m_ref[...] = jnp.full(m_ref.shape, NEG, F32)
    l_ref[...] = jnp.zeros(l_ref.shape, F32)
    acc_ref[...] = jnp.zeros(acc_ref.shape, F32)
    q_m = pair_split(q_ref, QK_SCALE)

    def stage_logits(c, slot):
        c = jnp.minimum(c, n_chunks - 1)
        c0 = pl.multiple_of(c * blk, blk)
        madd = sc_ref[c]
        o_idx = jnp.clip(c - i + (n_off - 1), 0, n_off - 1)
        for p in range(N_PAIRS):
            k2 = k_ref[0, pl.ds(c0, blk), p * LANES:(p + 1) * LANES]
            for hh in range(2):
                h = 2 * p + hh
                s = lax.dot_general(k2, q_m[h], NT_DIMS, preferred_element_type=F32)
                s = s + bias_ref[h, o_idx] + madd
                st_ref[slot, h] = s
                mx_ref[slot, h] = fin(part(s, jnp.max), jnp.max)

    def stage_values(c, slot):
        for p in range(N_PAIRS):
            vt2 = vt_ref[0, c, p * LANES:(p + 1) * LANES, :]
            for hh in range(2):
                h = 2 * p + hh
                m_old = m_ref[h]
                m_new = jnp.maximum(m_old, mx_ref[slot, h])
                alpha = jnp.exp(m_old - m_new)
                pexp = jnp.exp(st_ref[slot, h] - m_new)
                l_ref[h] = alpha * l_ref[h] + fin(part(pexp, jnp.sum), jnp.sum)
                acc_ref[h] = alpha * acc_ref[h] + jnp.dot(vt2, pexp.astype(BF16),
                                                          preferred_element_type=F32)
                m_ref[h] = m_new

    stage_logits(0, 0)

    def att_pair(pp, _):
        c = 2 * pp
        stage_logits(c + 1, 1)
        stage_values(c, 0)
        stage_logits(c + 2, 0)
        stage_values(c + 1, 1)
        return 0

    lax.fori_loop(0, nck // 2, att_pair, 0)

    @pl.when(lax.rem(nck, 2) == 1)
    def _last_chunk():
        stage_values(nck - 1, 0)

    lo_rows = lax.broadcasted_iota(I32, (LANES, blk), 0) < HALF
    for p in range(N_PAIRS):
        oa = acc_ref[2 * p] / l_ref[2 * p]
        ob = acc_ref[2 * p + 1] / l_ref[2 * p + 1]
        o_ref[0, :, p * LANES:(p + 1) * LANES] = jnp.where(lo_rows, oa, ob).T.astype(o_ref.dtype)


def _dsa(proj3, vt4, rel_bias):
    B, S, _ = proj3.shape
    blk = DSA_BLK
    n_off = _dsa_n_off(blk)
    n_sel = min(TOPK_MAX, S // 4)
    bucket = jnp.asarray(_dsa_bucket_tiles(blk))
    n_chunks = S // blk
    assert S % blk == 0
    kern = functools.partial(_dsa_kernel, blk=blk, n_chunks=n_chunks, n_sel=n_sel, n_off=n_off)
    return pl.pallas_call(
        kern,
        grid=(B, S // blk),
        in_specs=[
            pl.BlockSpec((n_off, blk, blk), lambda b, i: (0, 0, 0)),
            pl.BlockSpec(memory_space=pltpu.SMEM),
            pl.BlockSpec((1, blk, ATT_W), lambda b, i: (b, i, COL_QA // ATT_W)),
            pl.BlockSpec((1, S, ATT_W), lambda b, i: (b, 0, COL_KA // ATT_W)),
            pl.BlockSpec((1, S // blk, ATT_W, blk), lambda b, i: (b, 0, 0, 0)),
            pl.BlockSpec((1, blk, ATT_W), lambda b, i: (b, i, COL_QI // ATT_W)),
            pl.BlockSpec((1, S, LANES), lambda b, i: (b, 0, COL_KK // LANES)),
            pl.BlockSpec((1, blk, LANES), lambda b, i: (b, i, COL_WI // LANES)),
        ],
        out_specs=pl.BlockSpec((1, blk, ATT_W), lambda b, i: (b, i, 0)),
        out_shape=jax.ShapeDtypeStruct((B, S, ATT_W), BF16),
        scratch_shapes=[
            pltpu.VMEM((S // blk, blk, blk), F32),
            pltpu.VMEM((A_HEADS, n_off, blk, blk), F32),
            pltpu.VMEM((A_HEADS, 1, blk), F32),
            pltpu.VMEM((A_HEADS, 1, blk), F32),
            pltpu.VMEM((A_HEADS, LANES, blk), F32),
            pltpu.VMEM((2, A_HEADS, blk, blk), F32),
            pltpu.VMEM((2, A_HEADS, 1, blk), F32),
        ],
        compiler_params=_params(("arbitrary", "arbitrary")),
    )(bucket, rel_bias, proj3, proj3, vt4, proj3, proj3, proj3)


def _sb_kernel(q_ref, k_ref, v_ref, o_ref, hl_ref, z_ref, *, t):
    i = pl.program_id(2)
    n = i + 1
    lane = lax.broadcasted_iota(I32, (t, LANES), 1)
    lo_half = lane < HALF
    q_m = []
    for pr in range(SB_PAIRS):
        q2 = q_ref[0, :, pr * LANES:(pr + 1) * LANES].astype(F32) * QK_SCALE
        q_m += [jnp.where(lo_half, q2, 0.0).astype(BF16), jnp.where(lo_half, 0.0, q2).astype(BF16)]
    heads = 2 * SB_PAIRS
    r = lax.broadcasted_iota(I32, (t, t), 0)
    cidx = lax.broadcasted_iota(I32, (t, t), 1)
    neg_from = jnp.where(r >= cidx, -1.0, 0.0).astype(BF16)
    diff = cidx - r

    def stage_terms(step, slot, diagonal):
        c0 = pl.multiple_of(jnp.maximum(i - step, 0) * t, t)
        if diagonal:
            keep = diff < 0
        for hh in range(heads):
            pr = hh // 2
            k2 = k_ref[0, pl.ds(c0, t), pr * LANES:(pr + 1) * LANES]
            z = lax.dot_general(q_m[hh], k2, NT_DIMS, preferred_element_type=F32)
            sp = jnp.maximum(z, 0.0) + jnp.log(1.0 + jnp.exp(-jnp.abs(z)))
            if diagonal:
                sp = jnp.where(keep, sp, 0.0)
                z = jnp.where(keep, z, NEG)
            hi = sp.astype(BF16)
            hl_ref[slot, 2 * hh] = hi
            hl_ref[slot, 2 * hh + 1] = (sp - hi.astype(F32)).astype(BF16)
            z_ref[slot, hh] = z

    def stage_apply(step, slot, carry):
        c0 = pl.multiple_of((i - step) * t, t)
        cum4 = jnp.dot(hl_ref[slot].reshape(2 * heads * t, t), neg_from, preferred_element_type=F32)
        out = []
        for hh in range(heads):
            pr = hh // 2
            v2 = v_ref[0, pl.ds(c0, t), pr * LANES:(pr + 1) * LANES]
            car, acc = carry[hh]
            cum = cum4[(2 * hh) * t:(2 * hh + 1) * t] + cum4[(2 * hh + 1) * t:(2 * hh + 2) * t]
            w = jnp.exp(z_ref[slot, hh] + cum + car)
            acc = acc + jnp.dot(w.astype(BF16), v2, preferred_element_type=F32)
            out.append((car + cum[:, 0:1], acc))
        return tuple(out)

    z1 = jnp.zeros((t, 1), F32)
    za = jnp.zeros((t, LANES), F32)
    stage_terms(0, 0, True)

    def pair_body(pp, carry):
        step = 2 * pp
        stage_terms(step + 1, 1, False)
        carry = stage_apply(step, 0, carry)
        stage_terms(step + 2, 0, False)
        return stage_apply(step + 1, 1, carry)

    carry = lax.fori_loop(0, n // 2, pair_body, ((z1, za),) * heads)
    carry = lax.cond(lax.rem(n, 2) == 1, lambda c: stage_apply(n - 1, 0, c), lambda c: c, carry)
    for pr in range(SB_PAIRS):
        o_ref[0, :, pr * LANES:(pr + 1) * LANES] = jnp.where(
            lo_half, carry[2 * pr][1], carry[2 * pr + 1][1]).astype(o_ref.dtype)


def _stick_breaking(proj3):
    B, S, _ = proj3.shape
    t = SB_T
    w = SB_PAIRS * LANES
    qb, kb, vb = COL_QB // w, COL_KB // w, COL_VB // w
    return pl.pallas_call(
        functools.partial(_sb_kernel, t=t),
        grid=(B, N_PAIRS // SB_PAIRS, S // t),
        in_specs=[
            pl.BlockSpec((1, t, w), lambda b, p, i: (b, i, qb + p)),
            pl.BlockSpec((1, S, w), lambda b, p, i: (b, 0, kb + p)),
            pl.BlockSpec((1, S, w), lambda b, p, i: (b, 0, vb + p)),
        ],
        out_specs=pl.BlockSpec((1, t, w), lambda b, p, i: (b, i, p)),
        out_shape=jax.ShapeDtypeStruct((B, S, ATT_W), BF16),
        scratch_shapes=[
            pltpu.VMEM((2, 4 * SB_PAIRS, t, t), BF16),
            pltpu.VMEM((2, 2 * SB_PAIRS, t, t), F32),
        ],
        compiler_params=_params(("arbitrary", "arbitrary", "arbitrary")),
    )(proj3, proj3, proj3)


def _layer_norm(r, g, b):
    mu = jnp.mean(r, axis=-1, keepdims=True)
    d = r - mu
    var = jnp.mean(d * d, axis=-1, keepdims=True)
    return d * lax.rsqrt(var + LN_EPS) * g + b


def _split_bf16(v):
    hi = v.astype(BF16)
    return hi, (v - hi.astype(F32)).astype(BF16)


def _pack_bf16_pairs(v):
    half = v.shape[1] // 2
    bits = pltpu.bitcast(v.astype(BF16).astype(F32), jnp.uint32)
    return bits[:, :half] | (bits[:, half:] >> 16)


def _unpack_bf16_pairs(w):
    return jnp.concatenate([pltpu.bitcast(w & jnp.uint32(0xFFFF0000), F32),
                            pltpu.bitcast(w << 16, F32)], axis=1)


def _merge_kernel(x_ref, ya_ref, yb_ref, wg_ref, wa_ref, wb_ref, wo_ref, g_ref, b_ref,
                  wr_ref, br_ref, h_ref, hp_ref, e_ref, p_ref, *, alpha, d):
    pa = jnp.dot(ya_ref[...], wa_ref[...], preferred_element_type=F32)
    pb = jnp.dot(yb_ref[...], wb_ref[...], preferred_element_type=F32)
    gates = jnp.dot(x_ref[...].astype(BF16), wg_ref[...], preferred_element_type=F32)
    merged = jax.nn.sigmoid(gates[:, :d]) * pa + jax.nn.sigmoid(gates[:, d:]) * pb
    m = jnp.dot(merged.astype(BF16), wo_ref[...], preferred_element_type=F32)
    h = _layer_norm(alpha * x_ref[...] + m, g_ref[...], b_ref[...])
    h_ref[...] = h
    hp_ref[...] = _pack_bf16_pairs(h)

    h_hi, h_lo = _split_bf16(h)
    w_hi, w_lo = _split_bf16(wr_ref[...])
    logit = (lax.dot_general(w_hi, h_hi, NT_DIMS, preferred_element_type=F32)
             + lax.dot_general(w_hi, h_lo, NT_DIMS, preferred_element_type=F32)
             + lax.dot_general(w_lo, h_hi, NT_DIMS, preferred_element_type=F32)) + br_ref[...]
    eid = lax.broadcasted_iota(I32, logit.shape, 0)
    vals, ids = [], []
    for _ in range(TOP_K):
        mx = jnp.max(logit, axis=0, keepdims=True)
        am = jnp.min(jnp.where(logit == mx, eid, N_EXPERTS), axis=0, keepdims=True)
        vals.append(mx)
        ids.append(am)
        logit = jnp.where(eid == am, -jnp.inf, logit)
    ex = [jnp.exp(v - vals[0]) for v in vals]
    den = ex[0] + ex[1] + ex[2] + ex[3]
    for k in range(TOP_K):
        e_ref[k:k + 1, :] = ids[k]
        p_ref[k:k + 1, :] = ex[k] / den


def _merge(x2, ya, yb, wg, wa, wb, wo, g, b, wr_t, br, alpha):
    T, D = x2.shape
    tm = MERGE_TM
    row = lambda i: (i, 0)
    fixed = lambda i: (0, 0)
    return pl.pallas_call(
        functools.partial(_merge_kernel, alpha=alpha, d=D),
        grid=(T // tm,),
        in_specs=[
            pl.BlockSpec((tm, D), row),
            pl.BlockSpec((tm, ATT_W), row),
            pl.BlockSpec((tm, ATT_W), row),
            pl.BlockSpec((D, 2 * D), fixed),
            pl.BlockSpec((ATT_W, D), fixed),
            pl.BlockSpec((ATT_W, D), fixed),
            pl.BlockSpec((D, D), fixed),
            pl.BlockSpec((1, D), fixed),
            pl.BlockSpec((1, D), fixed),
            pl.BlockSpec((N_EXPERTS, D), fixed),
            pl.BlockSpec((N_EXPERTS, 1), fixed),
        ],
        out_specs=[
            pl.BlockSpec((tm, D), row),
            pl.BlockSpec((tm, D // 2), row),
            pl.BlockSpec((TOP_K, tm), lambda i: (0, i)),
            pl.BlockSpec((TOP_K, tm), lambda i: (0, i)),
        ],
        out_shape=[
            jax.ShapeDtypeStruct((T, D), F32),
            jax.ShapeDtypeStruct((T, D // 2), jnp.uint32),
            jax.ShapeDtypeStruct((TOP_K, T), I32),
            jax.ShapeDtypeStruct((TOP_K, T), F32),
        ],
        compiler_params=_params(("arbitrary",)),
    )(x2, ya, yb, wg, wa, wb, wo, g, b, wr_t, br)


def _sc_gather_rows(table, idx):
    n = idx.shape[0]
    d = table.shape[1]
    info = plsc.get_sparse_core_info()
    n_cores, n_sub = info.num_cores, info.num_subcores
    per_w = n // (n_cores * n_sub)
    c = SC_GATHER_BYTES // (d * table.dtype.itemsize)
    n_g = per_w // c
    assert n == per_w * n_cores * n_sub and per_w == n_g * c and n_g % 2 == 0 and n_g >= 2
    mesh = plsc.VectorSubcoreMesh(core_axis_name="c", subcore_axis_name="s")

    @functools.partial(
        pl.kernel, mesh=mesh, out_type=jax.ShapeDtypeStruct((n, d), table.dtype),
        scratch_types=[pltpu.VMEM((per_w,), I32), pltpu.VMEM((2, c, d), table.dtype),
                       pltpu.SemaphoreType.DMA((2,)), pltpu.SemaphoreType.DMA((2,))])
    def gather_kernel(table_hbm, idx_hbm, out_hbm, idx_v, rows_v, gsem, wsem):
        base = (lax.axis_index("s") * n_cores + lax.axis_index("c")) * per_w
        pltpu.sync_copy(idx_hbm.at[pl.ds(base, per_w)], idx_v)

        def gather(g, b):
            return pltpu.make_async_copy(table_hbm.at[idx_v.at[pl.ds(g * c, c)]], rows_v.at[b],
                                         gsem.at[b])

        def write(g, b):
            return pltpu.make_async_copy(rows_v.at[b], out_hbm.at[pl.ds(base + g * c, c)], wsem.at[b])

        gather(0, 0).start()

        @pl.loop(0, n_g, step=2)
        def _ring(g0):
            for b in range(2):
                g = g0 + b

                @pl.when(g + 1 < n_g)
                def _next():
                    @pl.when(g >= 1)
                    def _buffer_free():
                        write(g - 1, 1 - b).wait()
                    gather(g + 1, 1 - b).start()

                gather(g, b).wait()
                write(g, b).start()

        write(n_g - 2, 0).wait()
        write(n_g - 1, 1).wait()

    return gather_kernel(table, idx)


def _moe_kernel(blk_e_ref, nused_ref, x_ref, wgu_ref, bgu_ref, wdn_ref, bdn_ref, o_ref,
                wgu_s, wdn_s, *, f):
    i = pl.program_id(0)
    nused = nused_ref[0]

    @pl.when(i < nused)
    def _compute():
        changed = jnp.logical_or(i == 0, blk_e_ref[i] != blk_e_ref[jnp.maximum(i - 1, 0)])

        @pl.when(changed)
        def _cast_weights():
            wgu_s[...] = wgu_ref[0].astype(BF16)
            wdn_s[...] = wdn_ref[0].astype(BF16)

        x = _unpack_bf16_pairs(x_ref[...]).astype(BF16)
        hgu = jnp.dot(x, wgu_s[...], preferred_element_type=F32) + bgu_ref[0]
        a = jnp.minimum(hgu[:, :f], SWIGLU_LIMIT)
        u = jnp.clip(hgu[:, f:], -SWIGLU_LIMIT, SWIGLU_LIMIT)
        glu = a * jax.nn.sigmoid(a * SWIGLU_ALPHA)
        y = jnp.dot(((u + 1.0) * glu).astype(BF16), wdn_s[...], preferred_element_type=F32) + bdn_ref[0]
        o_ref[...] = _pack_bf16_pairs(y)

    @pl.when(i >= nused)
    def _unused_block():
        o_ref[...] = jnp.zeros(o_ref.shape, o_ref.dtype)


def _moe_ffn(xs, blk_e, nused, w_gu, b_gu, w_dn, b_dn):
    P = xs.shape[0]
    E, D, F2 = w_gu.shape
    f = F2 // 2
    blk = MOE_BLK
    nb = P // blk
    used_block = lambda i, be, nu: (jnp.minimum(i, nu[0] - 1), 0)
    grid_spec = pltpu.PrefetchScalarGridSpec(
        num_scalar_prefetch=2,
        grid=(nb,),
        in_specs=[
            pl.BlockSpec((blk, D // 2), used_block),
            pl.BlockSpec((1, D, F2), lambda i, be, nu: (be[i], 0, 0)),
            pl.BlockSpec((1, 1, F2), lambda i, be, nu: (be[i], 0, 0)),
            pl.BlockSpec((1, f, D), lambda i, be, nu: (be[i], 0, 0)),
            pl.BlockSpec((1, 1, D), lambda i, be, nu: (be[i], 0, 0)),
        ],
        out_specs=pl.BlockSpec((blk, D // 2), lambda i, be, nu: (i, 0)),
        scratch_shapes=[
            pltpu.VMEM((D, F2), BF16),
            pltpu.VMEM((f, D), BF16),
        ],
    )
    return pl.pallas_call(
        functools.partial(_moe_kernel, f=f),
        grid_spec=grid_spec,
        out_shape=jax.ShapeDtypeStruct((P, D // 2), jnp.uint32),
        compiler_params=_params(("arbitrary",)),
    )(blk_e, nused, xs, w_gu, b_gu.reshape(E, 1, F2), w_dn, b_dn.reshape(E, 1, D))


def _comb_kernel(h_ref, p_ref, y_ref, g_ref, b_ref, o_ref, *, alpha):
    gate = p_ref[...]
    y = [_unpack_bf16_pairs(y_ref[k]) for k in range(TOP_K)]
    fsum = (y[0] * gate[:, 0:1] + y[1] * gate[:, 1:2]) + (y[2] * gate[:, 2:3] + y[3] * gate[:, 3:4])
    o_ref[...] = _layer_norm(alpha * h_ref[...] + fsum, g_ref[...], b_ref[...])


def _combine(h, y4, top_p, g, b, alpha):
    T, D = h.shape
    tm = COMB_TM
    return pl.pallas_call(
        functools.partial(_comb_kernel, alpha=alpha),
        grid=(T // tm,),
        in_specs=[
            pl.BlockSpec((tm, D), lambda i: (i, 0)),
            pl.BlockSpec((tm, TOP_K), lambda i: (i, 0)),
            pl.BlockSpec((TOP_K, tm, D // 2), lambda i: (0, i, 0)),
            pl.BlockSpec((1, D), lambda i: (0, 0)),
            pl.BlockSpec((1, D), lambda i: (0, 0)),
        ],
        out_specs=pl.BlockSpec((tm, D), lambda i: (i, 0)),
        out_shape=jax.ShapeDtypeStruct((T, D), F32),
        compiler_params=_params(("arbitrary",)),
    )(h, top_p.T, y4, g, b)


def _route(top_e, blk):
    K, T = top_e.shape
    N = K * T
    flat_e = top_e.reshape(N)
    experts = jnp.arange(N_EXPERTS, dtype=I32)
    order = jnp.argsort(flat_e, stable=True).astype(I32)
    inv = jnp.argsort(order).astype(I32)
    onehot = flat_e[:, None] == experts[None, :]
    counts = jnp.sum(onehot, axis=0, dtype=I32)
    padded = (counts + blk - 1) // blk * blk
    pends = jnp.cumsum(padded)
    offs = jnp.cumsum(counts) - counts
    shift = (pends - padded) - offs
    pos = inv + jnp.sum(jnp.where(onehot, shift[None, :], 0), axis=1, dtype=I32)
    P = N + N_EXPERTS * blk
    nb = P // blk
    blk_start = jnp.arange(nb, dtype=I32) * blk
    blk_e = jnp.minimum(jnp.sum(pends[None, :] <= blk_start[:, None], axis=1, dtype=I32), N_EXPERTS - 1)
    j = (blk_start - shift[blk_e])[:, None] + jnp.arange(blk, dtype=I32)[None, :]
    valid = j < (offs + counts)[blk_e][:, None]
    src = order[jnp.clip(j, 0, N - 1)]
    row_tok = jnp.where(valid, src % T, j % T).reshape(P)
    nused = (pends[-1:] // blk).astype(I32)
    return blk_e, nused, row_tok, pos.reshape(K, T)


def _projection_weights(w_in_l):
    sizes = (ATT_W, ATT_W, ATT_W, IDX_HEADS * IDX_DIM, IDX_DIM, IDX_HEADS, ATT_W, ATT_W, ATT_W)
    offs = np.concatenate([[0], np.cumsum(sizes)])
    qa, ka, va, qi, ki, wi, qb, kb, vb = (w_in_l[:, offs[n]:offs[n + 1]] for n in range(9))
    pad_wi = jnp.zeros((w_in_l.shape[0], LANES - IDX_HEADS), w_in_l.dtype)
    w_att = jnp.concatenate([qa, ka, qi, qb, kb, vb, ki, ki, wi, pad_wi], axis=1).astype(BF16)
    w_va_t = va.T.astype(BF16)
    w_gate = w_in_l[:, offs[9]:].astype(BF16)
    return w_att, w_va_t, w_gate


def kernel(x, w_in, w_branch_a, w_branch_b, w_out, rel_bias, ln1_g, ln1_b, w_router, b_router,
           w_gate_up, b_gate_up, w_down, b_down, ln2_g, ln2_b):
    B, S, D = x.shape
    depth = w_in.shape[0]
    alpha = (2 * depth) ** 0.25
    T = B * S
    h = x.reshape(T, D)
    for l in range(depth):
        w_att, w_va_t, w_gate = _projection_weights(w_in[l])
        proj, vt = _projection(h, w_att, w_va_t, min(T, PROJ_TM), PROJ_TN, DSA_BLK)
        proj = proj.reshape(B, S, ATT_COLS)
        vt = vt.reshape(B, S // DSA_BLK, ATT_W, DSA_BLK)
        ya = _dsa(proj, vt, rel_bias).reshape(T, ATT_W)
        yb = _stick_breaking(proj).reshape(T, ATT_W)
        h1, h1_packed, top_e, top_p = _merge(
            h, ya, yb, w_gate, w_branch_a[l].astype(BF16), w_branch_b[l].astype(BF16),
            w_out[l].astype(BF16), ln1_g[l].reshape(1, D), ln1_b[l].reshape(1, D),
            w_router[l].T, b_router[l].reshape(N_EXPERTS, 1), alpha)
        blk_e, nused, row_tok, pos = _route(top_e, MOE_BLK)
        xs = _sc_gather_rows(h1_packed, row_tok)
        ys = _moe_ffn(xs, blk_e, nused, w_gate_up[l], b_gate_up[l], w_down[l], b_down[l])
        y4 = _sc_gather_rows(ys, pos.reshape(TOP_K * T)).reshape(TOP_K, T, D // 2)
        h = _combine(h1, y4, top_p, ln2_g[l].reshape(1, D), ln2_b[l].reshape(1, D), alpha)
    return h.reshape(B, S, D)
```

```python
import functools
import math

import numpy as np
import jax
import jax.numpy as jnp
from jax import lax
from jax.experimental import pallas as pl
from jax.experimental.pallas import tpu as pltpu
from jax.experimental.pallas import tpu_sc as plsc

F32 = jnp.float32
BF16 = jnp.bfloat16
I32 = jnp.int32

A_HEADS = 8
HEAD_DIM = 64
ATT_W = A_HEADS * HEAD_DIM
IDX_HEADS = 8
IDX_DIM = 64
IDX_SCALE = (IDX_HEADS * IDX_DIM) ** -0.5
TOPK_MAX = 256
N_BUCKETS = 32
MAX_DISTANCE = 128
N_EXPERTS = 32
TOP_K = 4
SWIGLU_LIMIT = 7.0
SWIGLU_ALPHA = 1.702
LN_EPS = 1e-5
QK_SCALE = HEAD_DIM ** -0.5

LANES = 128
SUBLANES = 8
HALF = LANES // 2
N_PAIRS = A_HEADS // 2
VMEM_LIMIT = 56 * 1024 * 1024

DSA_BLK = 256
SB_T = 256
SB_PAIRS = 4
PROJ_TM = 1024
MERGE_TM = 512
MOE_BLK = 512
COMB_TM = 512
SC_GATHER_BYTES = 128 * 1024
REDUCE_CHAINS = 8
BISECT_CAP = 24
NEG = -1e30

COL_QA, COL_KA, COL_QI, COL_QB, COL_KB, COL_VB = (g * ATT_W for g in range(6))
COL_KK = 6 * ATT_W
COL_WI = COL_KK + LANES
ATT_COLS = COL_WI + LANES
PROJ_TN = ATT_COLS // 2

NT_DIMS = (((1,), (1,)), ((), ()))


def _params(sem, vmem=VMEM_LIMIT):
    return pltpu.CompilerParams(dimension_semantics=sem, vmem_limit_bytes=vmem)


def _proj_kernel(x_ref, w_ref, wt_ref, o_ref, ot_ref, xb_ref, *, tt):
    @pl.when(pl.program_id(1) == 0)
    def _row_tile_start():
        xb_ref[...] = x_ref[...].astype(BF16)
        for r in range(ot_ref.shape[0]):
            ot_ref[r] = lax.dot_general(wt_ref[...], xb_ref[r * tt:(r + 1) * tt, :], NT_DIMS,
                                        preferred_element_type=F32).astype(ot_ref.dtype)

    o_ref[...] = jnp.dot(xb_ref[...], w_ref[...], preferred_element_type=F32).astype(o_ref.dtype)


def _projection(x, w, w_t, tm, tn, tt):
    M, K = x.shape
    N = w.shape[1]
    Nt = w_t.shape[0]
    return pl.pallas_call(
        functools.partial(_proj_kernel, tt=tt),
        grid=(M // tm, N // tn),
        in_specs=[pl.BlockSpec((tm, K), lambda i, j: (i, 0)),
                  pl.BlockSpec((K, tn), lambda i, j: (0, j)),
                  pl.BlockSpec((Nt, K), lambda i, j: (0, 0))],
        out_specs=[pl.BlockSpec((tm, tn), lambda i, j: (i, j)),
                   pl.BlockSpec((tm // tt, Nt, tt), lambda i, j: (i, 0, 0))],
        out_shape=[jax.ShapeDtypeStruct((M, N), BF16),
                   jax.ShapeDtypeStruct((M // tt, Nt, tt), BF16)],
        scratch_shapes=[pltpu.VMEM((tm, K), BF16)],
        compiler_params=_params(("arbitrary", "arbitrary")),
    )(x, w, w_t)


def _t5_bucket_np(n):
    n = np.maximum(n, 0)
    max_exact = N_BUCKETS // 2
    nf = np.maximum(n, 1).astype(np.float32)
    large = max_exact + (np.log(nf / max_exact) / math.log(MAX_DISTANCE / max_exact)
                         * (N_BUCKETS - max_exact)).astype(np.int32)
    large = np.minimum(large, N_BUCKETS - 1)
    return np.where(n < max_exact, n, large).astype(np.int32)


def _dsa_n_off(blk):
    return 2 + -(-MAX_DISTANCE // blk)


def _dsa_bucket_tiles(blk):
    n_off = _dsa_n_off(blk)
    j = np.arange(blk)[None, :, None]
    i = np.arange(blk)[None, None, :]
    o = np.arange(n_off)[:, None, None]
    return _t5_bucket_np(i - j + blk * (n_off - 1 - o))


def _dsa_kernel(bucket_ref, relb_ref, q_ref, k_ref, vt_ref, qi_ref, kk_ref, wi_ref, o_ref,
                sc_ref, bias_ref, m_ref, l_ref, acc_ref, st_ref, mx_ref,
                *, blk, n_chunks, n_sel, n_off):
    b = pl.program_id(0)
    i = pl.program_id(1)
    q0 = i * blk
    nck = i + 1
    groups = blk // SUBLANES

    @pl.when(jnp.logical_and(b == 0, i == 0))
    def _build_bias():
        def head_body(h, _):
            for o in range(n_off):
                for rb in range(blk // LANES):
                    for cb in range(blk // LANES):
                        rs = slice(rb * LANES, (rb + 1) * LANES)
                        cs = slice(cb * LANES, (cb + 1) * LANES)
                        bk = bucket_ref[o, rs, cs]

                        def bucket_body(n, acc):
                            return jnp.where(bk == n, relb_ref[n, h], acc)

                        bias_ref[h, o, rs, cs] = lax.fori_loop(
                            0, N_BUCKETS, bucket_body, jnp.zeros((LANES, LANES), F32))
            return 0

        lax.fori_loop(0, A_HEADS, head_body, 0)

    lane = lax.broadcasted_iota(I32, (blk, LANES), 1)
    lo_half = lane < HALF
    krow = lax.broadcasted_iota(I32, (blk, blk), 0)
    qpos = q0 + lax.broadcasted_iota(I32, (1, blk), 1)

    def pair_split(ref, scale):
        out = []
        for p in range(N_PAIRS):
            v = ref[0, :, p * LANES:(p + 1) * LANES].astype(F32)
            if scale != 1.0:
                v = v * scale
            out.append(jnp.where(lo_half, v, 0.0).astype(BF16))
            out.append(jnp.where(lo_half, 0.0, v).astype(BF16))
        return out

    wi_t = wi_ref[0].astype(F32).T
    wrow = [wi_t[h:h + 1, :] * IDX_SCALE for h in range(IDX_HEADS)]
    qi_m = pair_split(qi_ref, 1.0)

    def score_chunk(c, _):
        c0 = pl.multiple_of(c * blk, blk)
        kk = kk_ref[0, pl.ds(c0, blk), :]
        acc = jnp.zeros((blk, blk), F32)
        for h in range(IDX_HEADS):
            s = lax.dot_general(kk, qi_m[h], NT_DIMS, preferred_element_type=F32)
            acc = acc + wrow[h] * jnp.maximum(s, 0.0)
        sc_ref[c] = jnp.where(c0 + krow <= qpos, acc, -jnp.inf)
        return 0

    lax.fori_loop(0, nck, score_chunk, 0)

    kt = jnp.minimum(qpos + 1, n_sel).astype(F32)

    def fold(fn, init):
        def body(c, acc):
            return fn(acc, sc_ref[c])
        return lax.fori_loop(0, nck, body, init)

    def part(x, op):
        y = op(x.reshape(REDUCE_CHAINS, groups // REDUCE_CHAINS, SUBLANES, blk), axis=1)
        return op(y, axis=0)

    def fin(x, op):
        return op(x, axis=0, keepdims=True)

    zeros8 = jnp.zeros((SUBLANES, blk), F32)
    pinf8 = jnp.full((SUBLANES, blk), jnp.inf, F32)

    def count_ge(th):
        return fin(fold(lambda a, s: a + part(jnp.where(s >= th, 1.0, 0.0), jnp.sum), zeros8),
                   jnp.sum)

    mn, mx = fold(lambda a, s: (
        jnp.minimum(a[0], part(jnp.where(s == -jnp.inf, jnp.inf, s), jnp.min)),
        jnp.maximum(a[1], part(s, jnp.max))), (pinf8, -pinf8))
    rmin = fin(mn, jnp.min)
    rmax = fin(mx, jnp.max)

    def bis_cond(st):
        it, lo, hi, clo = st
        return jnp.logical_and(it < BISECT_CAP, jnp.max(jnp.abs(clo - kt)) > 0.0)

    def halve(lo, hi, clo):
        mid = 0.5 * lo + 0.5 * hi
        c = count_ge(mid)
        active = clo != kt
        up = jnp.logical_and(active, c >= kt)
        dn = jnp.logical_and(active, c < kt)
        return jnp.where(up, mid, lo), jnp.where(dn, mid, hi), jnp.where(up, c, clo)

    def bis_body(st):
        it, lo, hi, clo = st
        return (it + 2,) + halve(*halve(lo, hi, clo))

    _, lo, _, _ = lax.while_loop(
        bis_cond, bis_body,
        (jnp.int32(0), rmin, rmax + jnp.maximum(1.0, jnp.abs(rmax) * 2.0 ** -20), (qpos + 1).astype(F32)))

    def stats(lo_):
        a_ = fin(fold(lambda a, s: jnp.minimum(a, part(jnp.where(s >= lo_, s, jnp.inf), jnp.min)),
                      pinf8), jnp.min)
        cg, ct, nx = fold(
            lambda a, s: (a[0] + part(jnp.where(s > a_, 1.0, 0.0), jnp.sum),
                               a[1] + part(jnp.where(s == a_, 1.0, 0.0), jnp.sum),
                               jnp.minimum(a[2], part(jnp.where(s > a_, s, jnp.inf), jnp.min))),
            (zeros8, zeros8, pinf8))
        return a_, fin(cg, jnp.sum), fin(ct, jnp.sum), fin(nx, jnp.min)

    def fin_cond(st):
        return st[0]

    def fin_body(st):
        _, lo_, _, _ = st
        a_, cgt_, nt_, nxt_ = stats(lo_)
        bad = cgt_ >= kt
        return (jnp.max(jnp.where(bad, 1.0, 0.0)) > 0.0, jnp.where(bad, nxt_, a_), cgt_, nt_)

    _, a, cgt, nties = lax.while_loop(fin_cond, fin_body, (jnp.bool_(True), lo, kt, kt))
    need = kt - cgt
    excess = jnp.max(jnp.where(nties > need, 1.0, 0.0)) > 0.0

    def mask_plain():
        def body(c, _):
            sc_ref[c] = jnp.where(sc_ref[c] >= a, 0.0, NEG)
            return 0
        lax.fori_loop(0, nck, body, 0)

    def mask_ties():
        upto = (krow >= lax.broadcasted_iota(I32, (blk, blk), 1)).astype(BF16)

        def body(c, seen):
            s = sc_ref[c]
            tie = s == a
            rank = jnp.dot(upto, jnp.where(tie, 1.0, 0.0).astype(BF16),
                           preferred_element_type=F32) + seen
            sel = jnp.logical_or(s > a, jnp.logical_and(tie, rank <= need))
            sc_ref[c] = jnp.where(sel, 0.0, NEG)
            return rank[blk - 1:blk, :]

        lax.fori_loop(0, nck, body, jnp.zeros((1, blk), F32))

    lax.cond(excess, mask_ties, mask_plain)

    m_ref[...] = jnp.full(m_ref.shape, NEG, F32)
    l_ref[...] = jnp.zeros(l_ref.shape, F32)
    acc_ref[...] = jnp.zeros(acc_ref.shape, F32)
    q_m = pair_split(q_ref, QK_SCALE)

    def stage_logits(c, slot):
        c = jnp.minimum(c, n_chunks - 1)
        c0 = pl.multiple_of(c * blk, blk)
        madd = sc_ref[c]
        o_idx = jnp.clip(c - i + (n_off - 1), 0, n_off - 1)
        for p in range(N_PAIRS):
            k2 = k_ref[0, pl.ds(c0, blk), p * LANES:(p + 1) * LANES]
            for hh in range(2):
                h = 2 * p + hh
                s = lax.dot_general(k2, q_m[h], NT_DIMS, preferred_element_type=F32)
                s = s + bias_ref[h, o_idx] + madd
                st_ref[slot, h] = s
                mx_ref[slot, h] = fin(part(s, jnp.max), jnp.max)

    def stage_values(c, slot):
        for p in range(N_PAIRS):
            vt2 = vt_ref[0, c, p * LANES:(p + 1) * LANES, :]
            for hh in range(2):
                h = 2 * p + hh
                m_old = m_ref[h]
                m_new = jnp.maximum(m_old, mx_ref[slot, h])
                alpha = jnp.exp(m_old - m_new)
                pexp = jnp.exp(st_ref[slot, h] - m_new)
                l_ref[h] = alpha * l_ref[h] + fin(part(pexp, jnp.sum), jnp.sum)
                acc_ref[h] = alpha * acc_ref[h] + jnp.dot(vt2, pexp.astype(BF16),
                                                          preferred_element_type=F32)
                m_ref[h] = m_new

    stage_logits(0, 0)

    def att_pair(pp, _):
        c = 2 * pp
        stage_logits(c + 1, 1)
        stage_values(c, 0)
        stage_logits(c + 2, 0)
        stage_values(c + 1, 1)
        return 0

    lax.fori_loop(0, nck // 2, att_pair, 0)

    @pl.when(lax.rem(nck, 2) == 1)
    def _last_chunk():
        stage_values(nck - 1, 0)

    lo_rows = lax.broadcasted_iota(I32, (LANES, blk), 0) < HALF
    for p in range(N_PAIRS):
        oa = acc_ref[2 * p] / l_ref[2 * p]
        ob = acc_ref[2 * p + 1] / l_ref[2 * p + 1]
        o_ref[0, :, p * LANES:(p + 1) * LANES] = jnp.where(lo_rows, oa, ob).T.astype(o_ref.dtype)


def _dsa(proj3, vt4, rel_bias):
    B, S, _ = proj3.shape
    blk = DSA_BLK
    n_off = _dsa_n_off(blk)
    n_sel = min(TOPK_MAX, S // 4)
    bucket = jnp.asarray(_dsa_bucket_tiles(blk))
    n_chunks = S // blk
    assert S % blk == 0
    kern = functools.partial(_dsa_kernel, blk=blk, n_chunks=n_chunks, n_sel=n_sel, n_off=n_off)
    return pl.pallas_call(
        kern,
        grid=(B, S // blk),
        in_specs=[
            pl.BlockSpec((n_off, blk, blk), lambda b, i: (0, 0, 0)),
            pl.BlockSpec(memory_space=pltpu.SMEM),
            pl.BlockSpec((1, blk, ATT_W), lambda b, i: (b, i, COL_QA // ATT_W)),
            pl.BlockSpec((1, S, ATT_W), lambda b, i: (b, 0, COL_KA // ATT_W)),
            pl.BlockSpec((1, S // blk, ATT_W, blk), lambda b, i: (b, 0, 0, 0)),
            pl.BlockSpec((1, blk, ATT_W), lambda b, i: (b, i, COL_QI // ATT_W)),
            pl.BlockSpec((1, S, LANES), lambda b, i: (b, 0, COL_KK // LANES)),
            pl.BlockSpec((1, blk, LANES), lambda b, i: (b, i, COL_WI // LANES)),
        ],
        out_specs=pl.BlockSpec((1, blk, ATT_W), lambda b, i: (b, i, 0)),
        out_shape=jax.ShapeDtypeStruct((B, S, ATT_W), BF16),
        scratch_shapes=[
            pltpu.VMEM((S // blk, blk, blk), F32),
            pltpu.VMEM((A_HEADS, n_off, blk, blk), F32),
            pltpu.VMEM((A_HEADS, 1, blk), F32),
            pltpu.VMEM((A_HEADS, 1, blk), F32),
            pltpu.VMEM((A_HEADS, LANES, blk), F32),
            pltpu.VMEM((2, A_HEADS, blk, blk), F32),
            pltpu.VMEM((2, A_HEADS, 1, blk), F32),
        ],
        compiler_params=_params(("arbitrary", "arbitrary")),
    )(bucket, rel_bias, proj3, proj3, vt4, proj3, proj3, proj3)


def _sb_kernel(q_ref, k_ref, v_ref, o_ref, hl_ref, z_ref, *, t):
    i = pl.program_id(2)
    n = i + 1
    lane = lax.broadcasted_iota(I32, (t, LANES), 1)
    lo_half = lane < HALF
    q_m = []
    for pr in range(SB_PAIRS):
        q2 = q_ref[0, :, pr * LANES:(pr + 1) * LANES].astype(F32) * QK_SCALE
        q_m += [jnp.where(lo_half, q2, 0.0).astype(BF16), jnp.where(lo_half, 0.0, q2).astype(BF16)]
    heads = 2 * SB_PAIRS
    r = lax.broadcasted_iota(I32, (t, t), 0)
    cidx = lax.broadcasted_iota(I32, (t, t), 1)
    neg_from = jnp.where(r >= cidx, -1.0, 0.0).astype(BF16)
    neg_from2 = jnp.concatenate([neg_from, neg_from], axis=0)
    diff = cidx - r

    def stage_terms(step, slot, diagonal):
        c0 = pl.multiple_of(jnp.maximum(i - step, 0) * t, t)
        if diagonal:
            keep = diff < 0
        for hh in range(heads):
            pr = hh // 2
            k2 = k_ref[0, pl.ds(c0, t), pr * LANES:(pr + 1) * LANES]
            z = lax.dot_general(q_m[hh], k2, NT_DIMS, preferred_element_type=F32)
            sp = jnp.maximum(z, 0.0) + jnp.log(1.0 + jnp.exp(-jnp.abs(z)))
            if diagonal:
                sp = jnp.where(keep, sp, 0.0)
                z = jnp.where(keep, z, NEG)
            hi = sp.astype(BF16)
            hl_ref[slot, hh, :, :t] = hi
            hl_ref[slot, hh, :, t:] = (sp - hi.astype(F32)).astype(BF16)
            z_ref[slot, hh] = z

    def stage_apply(step, slot, carry):
        c0 = pl.multiple_of((i - step) * t, t)
        cum_all = jnp.dot(hl_ref[slot].reshape(heads * t, 2 * t), neg_from2, preferred_element_type=F32)
        out = []
        for hh in range(heads):
            pr = hh // 2
            v2 = v_ref[0, pl.ds(c0, t), pr * LANES:(pr + 1) * LANES]
            car, acc = carry[hh]
            cum = cum_all[hh * t:(hh + 1) * t]
            w = jnp.exp(z_ref[slot, hh] + cum + car)
            acc = acc + jnp.dot(w.astype(BF16), v2, preferred_element_type=F32)
            out.append((car + cum[:, 0:1], acc))
        return tuple(out)

    z1 = jnp.zeros((t, 1), F32)
    za = jnp.zeros((t, LANES), F32)
    stage_terms(0, 0, True)

    def pair_body(pp, carry):
        step = 2 * pp
        stage_terms(step + 1, 1, False)
        carry = stage_apply(step, 0, carry)
        stage_terms(step + 2, 0, False)
        return stage_apply(step + 1, 1, carry)

    carry = lax.fori_loop(0, n // 2, pair_body, ((z1, za),) * heads)
    carry = lax.cond(lax.rem(n, 2) == 1, lambda c: stage_apply(n - 1, 0, c), lambda c: c, carry)
    for pr in range(SB_PAIRS):
        o_ref[0, :, pr * LANES:(pr + 1) * LANES] = jnp.where(
            lo_half, carry[2 * pr][1], carry[2 * pr + 1][1]).astype(o_ref.dtype)


def _stick_breaking(proj3):
    B, S, _ = proj3.shape
    t = SB_T
    w = SB_PAIRS * LANES
    qb, kb, vb = COL_QB // w, COL_KB // w, COL_VB // w
    return pl.pallas_call(
        functools.partial(_sb_kernel, t=t),
        grid=(B, N_PAIRS // SB_PAIRS, S // t),
        in_specs=[
            pl.BlockSpec((1, t, w), lambda b, p, i: (b, i, qb + p)),
            pl.BlockSpec((1, S, w), lambda b, p, i: (b, 0, kb + p)),
            pl.BlockSpec((1, S, w), lambda b, p, i: (b, 0, vb + p)),
        ],
        out_specs=pl.BlockSpec((1, t, w), lambda b, p, i: (b, i, p)),
        out_shape=jax.ShapeDtypeStruct((B, S, ATT_W), BF16),
        scratch_shapes=[
            pltpu.VMEM((2, 2 * SB_PAIRS, t, 2 * t), BF16),
            pltpu.VMEM((2, 2 * SB_PAIRS, t, t), F32),
        ],
        compiler_params=_params(("arbitrary", "arbitrary", "arbitrary")),
    )(proj3, proj3, proj3)


def _layer_norm(r, g, b):
    mu = jnp.mean(r, axis=-1, keepdims=True)
    d = r - mu
    var = jnp.mean(d * d, axis=-1, keepdims=True)
    return d * lax.rsqrt(var + LN_EPS) * g + b


def _split_bf16(v):
    hi = v.astype(BF16)
    return hi, (v - hi.astype(F32)).astype(BF16)


def _pack_bf16_pairs(v):
    half = v.shape[1] // 2
    bits = pltpu.bitcast(v.astype(BF16).astype(F32), jnp.uint32)
    return bits[:, :half] | (bits[:, half:] >> 16)


def _unpack_bf16_pairs(w):
    return jnp.concatenate([pltpu.bitcast(w & jnp.uint32(0xFFFF0000), F32),
                            pltpu.bitcast(w << 16, F32)], axis=1)


def _merge_kernel(x_ref, ya_ref, yb_ref, wg_ref, wa_ref, wb_ref, wo_ref, g_ref, b_ref,
                  wr_ref, br_ref, h_ref, hp_ref, e_ref, p_ref, *, alpha, d):
    pa = jnp.dot(ya_ref[...], wa_ref[...], preferred_element_type=F32)
    pb = jnp.dot(yb_ref[...], wb_ref[...], preferred_element_type=F32)
    gates = jnp.dot(x_ref[...].astype(BF16), wg_ref[...], preferred_element_type=F32)
    merged = jax.nn.sigmoid(gates[:, :d]) * pa + jax.nn.sigmoid(gates[:, d:]) * pb
    m = jnp.dot(merged.astype(BF16), wo_ref[...], preferred_element_type=F32)
    h = _layer_norm(alpha * x_ref[...] + m, g_ref[...], b_ref[...])
    h_ref[...] = h
    hp_ref[...] = _pack_bf16_pairs(h)

    h_hi, h_lo = _split_bf16(h)
    w_hi, w_lo = _split_bf16(wr_ref[...])
    logit = (lax.dot_general(w_hi, h_hi, NT_DIMS, preferred_element_type=F32)
             + lax.dot_general(w_hi, h_lo, NT_DIMS, preferred_element_type=F32)
             + lax.dot_general(w_lo, h_hi, NT_DIMS, preferred_element_type=F32)) + br_ref[...]
    eid = lax.broadcasted_iota(I32, logit.shape, 0)
    vals, ids = [], []
    for _ in range(TOP_K):
        mx = jnp.max(logit, axis=0, keepdims=True)
        am = jnp.min(jnp.where(logit == mx, eid, N_EXPERTS), axis=0, keepdims=True)
        vals.append(mx)
        ids.append(am)
        logit = jnp.where(eid == am, -jnp.inf, logit)
    ex = [jnp.exp(v - vals[0]) for v in vals]
    den = ex[0] + ex[1] + ex[2] + ex[3]
    for k in range(TOP_K):
        e_ref[k:k + 1, :] = ids[k]
        p_ref[k:k + 1, :] = ex[k] / den


def _merge(x2, ya, yb, wg, wa, wb, wo, g, b, wr_t, br, alpha):
    T, D = x2.shape
    tm = MERGE_TM
    row = lambda i: (i, 0)
    fixed = lambda i: (0, 0)
    return pl.pallas_call(
        functools.partial(_merge_kernel, alpha=alpha, d=D),
        grid=(T // tm,),
        in_specs=[
            pl.BlockSpec((tm, D), row),
            pl.BlockSpec((tm, ATT_W), row),
            pl.BlockSpec((tm, ATT_W), row),
            pl.BlockSpec((D, 2 * D), fixed),
            pl.BlockSpec((ATT_W, D), fixed),
            pl.BlockSpec((ATT_W, D), fixed),
            pl.BlockSpec((D, D), fixed),
            pl.BlockSpec((1, D), fixed),
            pl.BlockSpec((1, D), fixed),
            pl.BlockSpec((N_EXPERTS, D), fixed),
            pl.BlockSpec((N_EXPERTS, 1), fixed),
        ],
        out_specs=[
            pl.BlockSpec((tm, D), row),
            pl.BlockSpec((tm, D // 2), row),
            pl.BlockSpec((TOP_K, tm), lambda i: (0, i)),
            pl.BlockSpec((TOP_K, tm), lambda i: (0, i)),
        ],
        out_shape=[
            jax.ShapeDtypeStruct((T, D), F32),
            jax.ShapeDtypeStruct((T, D // 2), jnp.uint32),
            jax.ShapeDtypeStruct((TOP_K, T), I32),
            jax.ShapeDtypeStruct((TOP_K, T), F32),
        ],
        compiler_params=_params(("arbitrary",)),
    )(x2, ya, yb, wg, wa, wb, wo, g, b, wr_t, br)


def _sc_gather_rows(table, idx):
    n = idx.shape[0]
    d = table.shape[1]
    info = plsc.get_sparse_core_info()
    n_cores, n_sub = info.num_cores, info.num_subcores
    per_w = n // (n_cores * n_sub)
    c = SC_GATHER_BYTES // (d * table.dtype.itemsize)
    n_g = per_w // c
    assert n == per_w * n_cores * n_sub and per_w == n_g * c and n_g % 2 == 0 and n_g >= 2
    mesh = plsc.VectorSubcoreMesh(core_axis_name="c", subcore_axis_name="s")

    @functools.partial(
        pl.kernel, mesh=mesh, out_type=jax.ShapeDtypeStruct((n, d), table.dtype),
        scratch_types=[pltpu.VMEM((per_w,), I32), pltpu.VMEM((2, c, d), table.dtype),
                       pltpu.SemaphoreType.DMA((2,)), pltpu.SemaphoreType.DMA((2,))])
    def gather_kernel(table_hbm, idx_hbm, out_hbm, idx_v, rows_v, gsem, wsem):
        base = (lax.axis_index("s") * n_cores + lax.axis_index("c")) * per_w
        pltpu.sync_copy(idx_hbm.at[pl.ds(base, per_w)], idx_v)

        def gather(g, b):
            return pltpu.make_async_copy(table_hbm.at[idx_v.at[pl.ds(g * c, c)]], rows_v.at[b],
                                         gsem.at[b])

        def write(g, b):
            return pltpu.make_async_copy(rows_v.at[b], out_hbm.at[pl.ds(base + g * c, c)], wsem.at[b])

        gather(0, 0).start()

        @pl.loop(0, n_g, step=2)
        def _ring(g0):
            for b in range(2):
                g = g0 + b

                @pl.when(g + 1 < n_g)
                def _next():
                    @pl.when(g >= 1)
                    def _buffer_free():
                        write(g - 1, 1 - b).wait()
                    gather(g + 1, 1 - b).start()

                gather(g, b).wait()
                write(g, b).start()

        write(n_g - 2, 0).wait()
        write(n_g - 1, 1).wait()

    return gather_kernel(table, idx)


def _moe_kernel(blk_e_ref, nused_ref, x_ref, wgu_ref, bgu_ref, wdn_ref, bdn_ref, o_ref,
                wgu_s, wdn_s, *, f):
    i = pl.program_id(0)
    nused = nused_ref[0]

    @pl.when(i < nused)
    def _compute():
        changed = jnp.logical_or(i == 0, blk_e_ref[i] != blk_e_ref[jnp.maximum(i - 1, 0)])

        @pl.when(changed)
        def _cast_weights():
            wgu_s[...] = wgu_ref[0].astype(BF16)
            wdn_s[...] = wdn_ref[0].astype(BF16)

        x = _unpack_bf16_pairs(x_ref[...]).astype(BF16)
        hgu = jnp.dot(x, wgu_s[...], preferred_element_type=F32) + bgu_ref[0]
        a = jnp.minimum(hgu[:, :f], SWIGLU_LIMIT)
        u = jnp.clip(hgu[:, f:], -SWIGLU_LIMIT, SWIGLU_LIMIT)
        glu = a * jax.nn.sigmoid(a * SWIGLU_ALPHA)
        y = jnp.dot(((u + 1.0) * glu).astype(BF16), wdn_s[...], preferred_element_type=F32) + bdn_ref[0]
        o_ref[...] = _pack_bf16_pairs(y)

    @pl.when(i >= nused)
    def _unused_block():
        o_ref[...] = jnp.zeros(o_ref.shape, o_ref.dtype)


def _moe_ffn(xs, blk_e, nused, w_gu, b_gu, w_dn, b_dn):
    P = xs.shape[0]
    E, D, F2 = w_gu.shape
    f = F2 // 2
    blk = MOE_BLK
    nb = P // blk
    used_block = lambda i, be, nu: (jnp.minimum(i, nu[0] - 1), 0)
    grid_spec = pltpu.PrefetchScalarGridSpec(
        num_scalar_prefetch=2,
        grid=(nb,),
        in_specs=[
            pl.BlockSpec((blk, D // 2), used_block),
            pl.BlockSpec((1, D, F2), lambda i, be, nu: (be[i], 0, 0)),
            pl.BlockSpec((1, 1, F2), lambda i, be, nu: (be[i], 0, 0)),
            pl.BlockSpec((1, f, D), lambda i, be, nu: (be[i], 0, 0)),
            pl.BlockSpec((1, 1, D), lambda i, be, nu: (be[i], 0, 0)),
        ],
        out_specs=pl.BlockSpec((blk, D // 2), lambda i, be, nu: (i, 0)),
        scratch_shapes=[
            pltpu.VMEM((D, F2), BF16),
            pltpu.VMEM((f, D), BF16),
        ],
    )
    return pl.pallas_call(
        functools.partial(_moe_kernel, f=f),
        grid_spec=grid_spec,
        out_shape=jax.ShapeDtypeStruct((P, D // 2), jnp.uint32),
        compiler_params=_params(("arbitrary",)),
    )(blk_e, nused, xs, w_gu, b_gu.reshape(E, 1, F2), w_dn, b_dn.reshape(E, 1, D))


def _comb_kernel(h_ref, p_ref, y_ref, g_ref, b_ref, o_ref, *, alpha):
    gate = p_ref[...]
    y = [_unpack_bf16_pairs(y_ref[k]) for k in range(TOP_K)]
    fsum = (y[0] * gate[:, 0:1] + y[1] * gate[:, 1:2]) + (y[2] * gate[:, 2:3] + y[3] * gate[:, 3:4])
    o_ref[...] = _layer_norm(alpha * h_ref[...] + fsum, g_ref[...], b_ref[...])


def _combine(h, y4, top_p, g, b, alpha):
    T, D = h.shape
    tm = COMB_TM
    return pl.pallas_call(
        functools.partial(_comb_kernel, alpha=alpha),
        grid=(T // tm,),
        in_specs=[
            pl.BlockSpec((tm, D), lambda i: (i, 0)),
            pl.BlockSpec((tm, TOP_K), lambda i: (i, 0)),
            pl.BlockSpec((TOP_K, tm, D // 2), lambda i: (0, i, 0)),
            pl.BlockSpec((1, D), lambda i: (0, 0)),
            pl.BlockSpec((1, D), lambda i: (0, 0)),
        ],
        out_specs=pl.BlockSpec((tm, D), lambda i: (i, 0)),
        out_shape=jax.ShapeDtypeStruct((T, D), F32),
        compiler_params=_params(("arbitrary",)),
    )(h, top_p.T, y4, g, b)


def _route(top_e, blk):
    K, T = top_e.shape
    N = K * T
    flat_e = top_e.reshape(N)
    experts = jnp.arange(N_EXPERTS, dtype=I32)
    order = jnp.argsort(flat_e, stable=True).astype(I32)
    inv = jnp.argsort(order).astype(I32)
    onehot = flat_e[:, None] == experts[None, :]
    counts = jnp.sum(onehot, axis=0, dtype=I32)
    padded = (counts + blk - 1) // blk * blk
    pends = jnp.cumsum(padded)
    offs = jnp.cumsum(counts) - counts
    shift = (pends - padded) - offs
    pos = inv + jnp.sum(jnp.where(onehot, shift[None, :], 0), axis=1, dtype=I32)
    P = N + N_EXPERTS * blk
    nb = P // blk
    blk_start = jnp.arange(nb, dtype=I32) * blk
    blk_e = jnp.minimum(jnp.sum(pends[None, :] <= blk_start[:, None], axis=1, dtype=I32), N_EXPERTS - 1)
    j = (blk_start - shift[blk_e])[:, None] + jnp.arange(blk, dtype=I32)[None, :]
    valid = j < (offs + counts)[blk_e][:, None]
    src = order[jnp.clip(j, 0, N - 1)]
    row_tok = jnp.where(valid, src % T, j % T).reshape(P)
    nused = (pends[-1:] // blk).astype(I32)
    return blk_e, nused, row_tok, pos.reshape(K, T)


def _projection_weights(w_in_l):
    sizes = (ATT_W, ATT_W, ATT_W, IDX_HEADS * IDX_DIM, IDX_DIM, IDX_HEADS, ATT_W, ATT_W, ATT_W)
    offs = np.concatenate([[0], np.cumsum(sizes)])
    qa, ka, va, qi, ki, wi, qb, kb, vb = (w_in_l[:, offs[n]:offs[n + 1]] for n in range(9))
    pad_wi = jnp.zeros((w_in_l.shape[0], LANES - IDX_HEADS), w_in_l.dtype)
    w_att = jnp.concatenate([qa, ka, qi, qb, kb, vb, ki, ki, wi, pad_wi], axis=1).astype(BF16)
    w_va_t = va.T.astype(BF16)
    w_gate = w_in_l[:, offs[9]:].astype(BF16)
    return w_att, w_va_t, w_gate


def kernel(x, w_in, w_branch_a, w_branch_b, w_out, rel_bias, ln1_g, ln1_b, w_router, b_router,
           w_gate_up, b_gate_up, w_down, b_down, ln2_g, ln2_b):
    B, S, D = x.shape
    depth = w_in.shape[0]
    alpha = (2 * depth) ** 0.25
    T = B * S
    h = x.reshape(T, D)
    for l in range(depth):
        w_att, w_va_t, w_gate = _projection_weights(w_in[l])
        proj, vt = _projection(h, w_att, w_va_t, min(T, PROJ_TM), PROJ_TN, DSA_BLK)
        proj = proj.reshape(B, S, ATT_COLS)
        vt = vt.reshape(B, S // DSA_BLK, ATT_W, DSA_BLK)
        ya = _dsa(proj, vt, rel_bias).reshape(T, ATT_W)
        yb = _stick_breaking(proj).reshape(T, ATT_W)
        h1, h1_packed, top_e, top_p = _merge(
            h, ya, yb, w_gate, w_branch_a[l].astype(BF16), w_branch_b[l].astype(BF16),
            w_out[l].astype(BF16), ln1_g[l].reshape(1, D), ln1_b[l].reshape(1, D),
            w_router[l].T, b_router[l].reshape(N_EXPERTS, 1), alpha)
        blk_e, nused, row_tok, pos = _route(top_e, MOE_BLK)
        xs = _sc_gather_rows(h1_packed, row_tok)
        ys = _moe_ffn(xs, blk_e, nused, w_gate_up[l], b_gate_up[l], w_down[l], b_down[l])
        y4 = _sc_gather_rows(ys, pos.reshape(TOP_K * T)).reshape(TOP_K, T, D // 2)
        h = _combine(h1, y4, top_p, ln2_g[l].reshape(1, D), ln2_b[l].reshape(1, D), alpha)
    return h.reshape(B, S, D)
```

```python
import functools
import math

import numpy as np
import jax
import jax.numpy as jnp
from jax import lax
from jax.experimental import pallas as pl
from jax.experimental.pallas import tpu as pltpu
from jax.experimental.pallas import tpu_sc as plsc

F32 = jnp.float32
BF16 = jnp.bfloat16
I32 = jnp.int32

A_HEADS = 8
HEAD_DIM = 64
ATT_W = A_HEADS * HEAD_DIM
IDX_HEADS = 8
IDX_DIM = 64
IDX_SCALE = (IDX_HEADS * IDX_DIM) ** -0.5
TOPK_MAX = 256
N_BUCKETS = 32
MAX_DISTANCE = 128
N_EXPERTS = 32
TOP_K = 4
SWIGLU_LIMIT = 7.0
SWIGLU_ALPHA = 1.702
LN_EPS = 1e-5
QK_SCALE = HEAD_DIM ** -0.5

LANES = 128
SUBLANES = 8
HALF = LANES // 2
N_PAIRS = A_HEADS // 2
VMEM_LIMIT = 56 * 1024 * 1024

DSA_BLK = 256
SB_T = 256
SB_PAIRS = 4
PROJ_TM = 1024
MERGE_TM = 512
MOE_BLK = 512
COMB_TM = 1024
SC_GATHER_BYTES = 128 * 1024
REDUCE_CHAINS = 8
BISECT_CAP = 22
NEG = -1e30

COL_QA, COL_KA, COL_QI, COL_QB, COL_KB, COL_VB = (g * ATT_W for g in range(6))
COL_KK = 6 * ATT_W
COL_WI = COL_KK + LANES
ATT_COLS = COL_WI + LANES
PROJ_TN = ATT_COLS // 2

NT_DIMS = (((1,), (1,)), ((), ()))


def _params(sem, vmem=VMEM_LIMIT):
    return pltpu.CompilerParams(dimension_semantics=sem, vmem_limit_bytes=vmem)


def _proj_kernel(x_ref, w_ref, wt_ref, o_ref, ot_ref, xb_ref, *, tt):
    @pl.when(pl.program_id(1) == 0)
    def _row_tile_start():
        xb_ref[...] = x_ref[...].astype(BF16)
        for r in range(ot_ref.shape[0]):
            ot_ref[r] = lax.dot_general(wt_ref[...], xb_ref[r * tt:(r + 1) * tt, :], NT_DIMS,
                                        preferred_element_type=F32).astype(ot_ref.dtype)

    o_ref[...] = jnp.dot(xb_ref[...], w_ref[...], preferred_element_type=F32).astype(o_ref.dtype)


def _projection(x, w, w_t, tm, tn, tt):
    M, K = x.shape
    N = w.shape[1]
    Nt = w_t.shape[0]
    return pl.pallas_call(
        functools.partial(_proj_kernel, tt=tt),
        grid=(M // tm, N // tn),
        in_specs=[pl.BlockSpec((tm, K), lambda i, j: (i, 0)),
                  pl.BlockSpec((K, tn), lambda i, j: (0, j)),
                  pl.BlockSpec((Nt, K), lambda i, j: (0, 0))],
        out_specs=[pl.BlockSpec((tm, tn), lambda i, j: (i, j)),
                   pl.BlockSpec((tm // tt, Nt, tt), lambda i, j: (i, 0, 0))],
        out_shape=[jax.ShapeDtypeStruct((M, N), BF16),
                   jax.ShapeDtypeStruct((M // tt, Nt, tt), BF16)],
        scratch_shapes=[pltpu.VMEM((tm, K), BF16)],
        compiler_params=_params(("arbitrary", "arbitrary")),
    )(x, w, w_t)


def _t5_bucket_np(n):
    n = np.maximum(n, 0)
    max_exact = N_BUCKETS // 2
    nf = np.maximum(n, 1).astype(np.float32)
    large = max_exact + (np.log(nf / max_exact) / math.log(MAX_DISTANCE / max_exact)
                         * (N_BUCKETS - max_exact)).astype(np.int32)
    large = np.minimum(large, N_BUCKETS - 1)
    return np.where(n < max_exact, n, large).astype(np.int32)


def _dsa_n_off(blk):
    return 2 + -(-MAX_DISTANCE // blk)


def _dsa_bucket_tiles(blk):
    n_off = _dsa_n_off(blk)
    j = np.arange(blk)[None, :, None]
    i = np.arange(blk)[None, None, :]
    o = np.arange(n_off)[:, None, None]
    return _t5_bucket_np(i - j + blk * (n_off - 1 - o))


def _dsa_kernel(bucket_ref, relb_ref, q_ref, k_ref, vt_ref, qi_ref, kk_ref, wi_ref, o_ref,
                sc_ref, bias_ref, m_ref, l_ref, acc_ref, st_ref, mx_ref,
                *, blk, n_chunks, n_sel, n_off):
    b = pl.program_id(0)
    i = pl.program_id(1)
    q0 = i * blk
    nck = i + 1
    groups = blk // SUBLANES

    @pl.when(jnp.logical_and(b == 0, i == 0))
    def _build_bias():
        def head_body(h, _):
            for o in range(n_off):
                for rb in range(blk // LANES):
                    for cb in range(blk // LANES):
                        rs = slice(rb * LANES, (rb + 1) * LANES)
                        cs = slice(cb * LANES, (cb + 1) * LANES)
                        bk = bucket_ref[o, rs, cs]

                        def bucket_body(n, acc):
                            return jnp.where(bk == n, relb_ref[n, h], acc)

                        bias_ref[h, o, rs, cs] = lax.fori_loop(
                            0, N_BUCKETS, bucket_body, jnp.zeros((LANES, LANES), F32))
            return 0

        lax.fori_loop(0, A_HEADS, head_body, 0)

    lane = lax.broadcasted_iota(I32, (blk, LANES), 1)
    lo_half = lane < HALF
    krow = lax.broadcasted_iota(I32, (blk, blk), 0)
    qpos = q0 + lax.broadcasted_iota(I32, (1, blk), 1)

    def pair_split(ref, scale):
        out = []
        for p in range(N_PAIRS):
            v = ref[0, :, p * LANES:(p + 1) * LANES].astype(F32)
            if scale != 1.0:
                v = v * scale
            out.append(jnp.where(lo_half, v, 0.0).astype(BF16))
            out.append(jnp.where(lo_half, 0.0, v).astype(BF16))
        return out

    wi_t = wi_ref[0].astype(F32).T
    wrow = [wi_t[h:h + 1, :] * IDX_SCALE for h in range(IDX_HEADS)]
    qi_m = pair_split(qi_ref, 1.0)

    def score_chunk(c, _):
        c0 = pl.multiple_of(c * blk, blk)
        kk = kk_ref[0, pl.ds(c0, blk), :]
        acc = jnp.zeros((blk, blk), F32)
        for h in range(IDX_HEADS):
            s = lax.dot_general(kk, qi_m[h], NT_DIMS, preferred_element_type=F32)
            acc = acc + wrow[h] * jnp.maximum(s, 0.0)
        sc_ref[c] = jnp.where(c0 + krow <= qpos, acc, -jnp.inf)
        return 0

    lax.fori_loop(0, nck, score_chunk, 0)

    kt = jnp.minimum(qpos + 1, n_sel).astype(F32)

    def fold(fn, init):
        def body(c, acc):
            return fn(acc, sc_ref[c])
        return lax.fori_loop(0, nck, body, init)

    def part(x, op):
        y = op(x.reshape(REDUCE_CHAINS, groups // REDUCE_CHAINS, SUBLANES, blk), axis=1)
        return op(y, axis=0)

    def fin(x, op):
        return op(x, axis=0, keepdims=True)

    zeros8 = jnp.zeros((SUBLANES, blk), F32)
    pinf8 = jnp.full((SUBLANES, blk), jnp.inf, F32)

    def count_ge(th):
        return fin(fold(lambda a, s: a + part(jnp.where(s >= th, 1.0, 0.0), jnp.sum), zeros8),
                   jnp.sum)

    mn, mx = fold(lambda a, s: (
        jnp.minimum(a[0], part(jnp.where(s == -jnp.inf, jnp.inf, s), jnp.min)),
        jnp.maximum(a[1], part(s, jnp.max))), (pinf8, -pinf8))
    rmin = fin(mn, jnp.min)
    rmax = fin(mx, jnp.max)

    def bis_cond(st):
        it, lo, hi, clo = st
        return jnp.logical_and(it < BISECT_CAP, jnp.max(jnp.abs(clo - kt)) > 0.0)

    def halve(lo, hi, clo):
        mid = 0.5 * lo + 0.5 * hi
        c = count_ge(mid)
        active = clo != kt
        up = jnp.logical_and(active, c >= kt)
        dn = jnp.logical_and(active, c < kt)
        return jnp.where(up, mid, lo), jnp.where(dn, mid, hi), jnp.where(up, c, clo)

    def bis_body(st):
        it, lo, hi, clo = st
        return (it + 2,) + halve(*halve(lo, hi, clo))

    _, lo, _, _ = lax.while_loop(
        bis_cond, bis_body,
        (jnp.int32(0), rmin, rmax + jnp.maximum(1.0, jnp.abs(rmax) * 2.0 ** -20), (qpos + 1).astype(F32)))

    def stats(lo_):
        a_ = fin(fold(lambda a, s: jnp.minimum(a, part(jnp.where(s >= lo_, s, jnp.inf), jnp.min)),
                      pinf8), jnp.min)
        cg, ct, nx = fold(
            lambda a, s: (a[0] + part(jnp.where(s > a_, 1.0, 0.0), jnp.sum),
                               a[1] + part(jnp.where(s == a_, 1.0, 0.0), jnp.sum),
                               jnp.minimum(a[2], part(jnp.where(s > a_, s, jnp.inf), jnp.min))),
            (zeros8, zeros8, pinf8))
        return a_, fin(cg, jnp.sum), fin(ct, jnp.sum), fin(nx, jnp.min)

    def fin_cond(st):
        return st[0]

    def fin_body(st):
        _, lo_, _, _ = st
        a_, cgt_, nt_, nxt_ = stats(lo_)
        bad = cgt_ >= kt
        return (jnp.max(jnp.where(bad, 1.0, 0.0)) > 0.0, jnp.where(bad, nxt_, a_), cgt_, nt_)

    _, a, cgt, nties = lax.while_loop(fin_cond, fin_body, (jnp.bool_(True), lo, kt, kt))
    need = kt - cgt
    excess = jnp.max(jnp.where(nties > need, 1.0, 0.0)) > 0.0

    def mask_plain():
        def body(c, _):
            sc_ref[c] = jnp.where(sc_ref[c] >= a, 0.0, NEG)
            return 0
        lax.fori_loop(0, nck, body, 0)

    def mask_ties():
        upto = (krow >= lax.broadcasted_iota(I32, (blk, blk), 1)).astype(BF16)

        def body(c, seen):
            s = sc_ref[c]
            tie = s == a
            rank = jnp.dot(upto, jnp.where(tie, 1.0, 0.0).astype(BF16),
                           preferred_element_type=F32) + seen
            sel = jnp.logical_or(s > a, jnp.logical_and(tie, rank <= need))
            sc_ref[c] = jnp.where(sel, 0.0, NEG)
            return rank[blk - 1:blk, :]

        lax.fori_loop(0, nck, body, jnp.zeros((1, blk), F32))

    lax.cond(excess, mask_ties, mask_plain)

    m_ref[...] = jnp.full(m_ref.shape, NEG, F32)
    l_ref[...] = jnp.zeros(l_ref.shape, F32)
    acc_ref[...] = jnp.zeros(acc_ref.shape, F32)
    q_m = pair_split(q_ref, QK_SCALE)

    def stage_logits(c, slot):
        c = jnp.minimum(c, n_chunks - 1)
        c0 = pl.multiple_of(c * blk, blk)
        madd = sc_ref[c]
        o_idx = jnp.clip(c - i + (n_off - 1), 0, n_off - 1)
        for p in range(N_PAIRS):
            k2 = k_ref[0, pl.ds(c0, blk), p * LANES:(p + 1) * LANES]
            for hh in range(2):
                h = 2 * p + hh
                s = lax.dot_general(k2, q_m[h], NT_DIMS, preferred_element_type=F32)
                s = s + bias_ref[h, o_idx] + madd
                st_ref[slot, h] = s
                mx_ref[slot, h] = fin(part(s, jnp.max), jnp.max)

    def stage_values(c, slot):
        for p in range(N_PAIRS):
            vt2 = vt_ref[0, c, p * LANES:(p + 1) * LANES, :]
            for hh in range(2):
                h = 2 * p + hh
                m_old = m_ref[h]
                m_new = jnp.maximum(m_old, mx_ref[slot, h])
                alpha = jnp.exp(m_old - m_new)
                pexp = jnp.exp(st_ref[slot, h] - m_new)
                l_ref[h] = alpha * l_ref[h] + fin(part(pexp, jnp.sum), jnp.sum)
                acc_ref[h] = alpha * acc_ref[h] + jnp.dot(vt2, pexp.astype(BF16),
                                                          preferred_element_type=F32)
                m_ref[h] = m_new

    stage_logits(0, 0)

    def att_pair(pp, _):
        c = 2 * pp
        stage_logits(c + 1, 1)
        stage_values(c, 0)
        stage_logits(c + 2, 0)
        stage_values(c + 1, 1)
        return 0

    lax.fori_loop(0, nck // 2, att_pair, 0)

    @pl.when(lax.rem(nck, 2) == 1)
    def _last_chunk():
        stage_values(nck - 1, 0)

    lo_rows = lax.broadcasted_iota(I32, (LANES, blk), 0) < HALF
    for p in range(N_PAIRS):
        oa = acc_ref[2 * p] / l_ref[2 * p]
        ob = acc_ref[2 * p + 1] / l_ref[2 * p + 1]
        o_ref[0, :, p * LANES:(p + 1) * LANES] = jnp.where(lo_rows, oa, ob).T.astype(o_ref.dtype)


def _dsa(proj3, vt4, rel_bias):
    B, S, _ = proj3.shape
    blk = DSA_BLK
    n_off = _dsa_n_off(blk)
    n_sel = min(TOPK_MAX, S // 4)
    bucket = jnp.asarray(_dsa_bucket_tiles(blk))
    n_chunks = S // blk
    assert S % blk == 0
    kern = functools.partial(_dsa_kernel, blk=blk, n_chunks=n_chunks, n_sel=n_sel, n_off=n_off)
    return pl.pallas_call(
        kern,
        grid=(B, S // blk),
        in_specs=[
            pl.BlockSpec((n_off, blk, blk), lambda b, i: (0, 0, 0)),
            pl.BlockSpec(memory_space=pltpu.SMEM),
            pl.BlockSpec((1, blk, ATT_W), lambda b, i: (b, i, COL_QA // ATT_W)),
            pl.BlockSpec((1, S, ATT_W), lambda b, i: (b, 0, COL_KA // ATT_W)),
            pl.BlockSpec((1, S // blk, ATT_W, blk), lambda b, i: (b, 0, 0, 0)),
            pl.BlockSpec((1, blk, ATT_W), lambda b, i: (b, i, COL_QI // ATT_W)),
            pl.BlockSpec((1, S, LANES), lambda b, i: (b, 0, COL_KK // LANES)),
            pl.BlockSpec((1, blk, LANES), lambda b, i: (b, i, COL_WI // LANES)),
        ],
        out_specs=pl.BlockSpec((1, blk, ATT_W), lambda b, i: (b, i, 0)),
        out_shape=jax.ShapeDtypeStruct((B, S, ATT_W), BF16),
        scratch_shapes=[
            pltpu.VMEM((S // blk, blk, blk), F32),
            pltpu.VMEM((A_HEADS, n_off, blk, blk), F32),
            pltpu.VMEM((A_HEADS, 1, blk), F32),
            pltpu.VMEM((A_HEADS, 1, blk), F32),
            pltpu.VMEM((A_HEADS, LANES, blk), F32),
            pltpu.VMEM((2, A_HEADS, blk, blk), F32),
            pltpu.VMEM((2, A_HEADS, 1, blk), F32),
        ],
        compiler_params=_params(("arbitrary", "arbitrary")),
    )(bucket, rel_bias, proj3, proj3, vt4, proj3, proj3, proj3)


def _sb_kernel(q_ref, k_ref, v_ref, o_ref, hl_ref, z_ref, *, t):
    i = pl.program_id(2)
    n = i + 1
    lane = lax.broadcasted_iota(I32, (t, LANES), 1)
    lo_half = lane < HALF
    q_m = []
    for pr in range(SB_PAIRS):
        q2 = q_ref[0, :, pr * LANES:(pr + 1) * LANES].astype(F32) * QK_SCALE
        q_m += [jnp.where(lo_half, q2, 0.0).astype(BF16), jnp.where(lo_half, 0.0, q2).astype(BF16)]
    heads = 2 * SB_PAIRS
    r = lax.broadcasted_iota(I32, (t, t), 0)
    cidx = lax.broadcasted_iota(I32, (t, t), 1)
    neg_from = jnp.where(r >= cidx, -1.0, 0.0).astype(BF16)
    neg_from2 = jnp.concatenate([neg_from, neg_from], axis=0)
    diff = cidx - r

    def stage_terms(step, slot, diagonal):
        c0 = pl.multiple_of(jnp.maximum(i - step, 0) * t, t)
        if diagonal:
            keep = diff < 0
        for hh in range(heads):
            pr = hh // 2
            k2 = k_ref[0, pl.ds(c0, t), pr * LANES:(pr + 1) * LANES]
            z = lax.dot_general(q_m[hh], k2, NT_DIMS, preferred_element_type=F32)
            sp = jnp.maximum(z, 0.0) + jnp.log(1.0 + jnp.exp(-jnp.abs(z)))
            if diagonal:
                sp = jnp.where(keep, sp, 0.0)
                z = jnp.where(keep, z, NEG)
            hi = sp.astype(BF16)
            hl_ref[slot, hh, :, :t] = hi
            hl_ref[slot, hh, :, t:] = (sp - hi.astype(F32)).astype(BF16)
            z_ref[slot, hh] = z

    def stage_apply(step, slot, carry):
        c0 = pl.multiple_of((i - step) * t, t)
        cum_all = jnp.dot(hl_ref[slot].reshape(heads * t, 2 * t), neg_from2, preferred_element_type=F32)
        out = []
        for hh in range(heads):
            pr = hh // 2
            v2 = v_ref[0, pl.ds(c0, t), pr * LANES:(pr + 1) * LANES]
            car, acc = carry[hh]
            cum = cum_all[hh * t:(hh + 1) * t]
            w = jnp.exp(z_ref[slot, hh] + cum + car)
            acc = acc + jnp.dot(w.astype(BF16), v2, preferred_element_type=F32)
            out.append((car + cum[:, 0:1], acc))
        return tuple(out)

    z1 = jnp.zeros((t, 1), F32)
    za = jnp.zeros((t, LANES), F32)
    stage_terms(0, 0, True)

    def pair_body(pp, carry):
        step = 2 * pp
        stage_terms(step + 1, 1, False)
        carry = stage_apply(step, 0, carry)
        stage_terms(step + 2, 0, False)
        return stage_apply(step + 1, 1, carry)

    carry = lax.fori_loop(0, n // 2, pair_body, ((z1, za),) * heads)
    carry = lax.cond(lax.rem(n, 2) == 1, lambda c: stage_apply(n - 1, 0, c), lambda c: c, carry)
    for pr in range(SB_PAIRS):
        o_ref[0, :, pr * LANES:(pr + 1) * LANES] = jnp.where(
            lo_half, carry[2 * pr][1], carry[2 * pr + 1][1]).astype(o_ref.dtype)


def _stick_breaking(proj3):
    B, S, _ = proj3.shape
    t = SB_T
    w = SB_PAIRS * LANES
    qb, kb, vb = COL_QB // w, COL_KB // w, COL_VB // w
    return pl.pallas_call(
        functools.partial(_sb_kernel, t=t),
        grid=(B, N_PAIRS // SB_PAIRS, S // t),
        in_specs=[
            pl.BlockSpec((1, t, w), lambda b, p, i: (b, i, qb + p)),
            pl.BlockSpec((1, S, w), lambda b, p, i: (b, 0, kb + p)),
            pl.BlockSpec((1, S, w), lambda b, p, i: (b, 0, vb + p)),
        ],
        out_specs=pl.BlockSpec((1, t, w), lambda b, p, i: (b, i, p)),
        out_shape=jax.ShapeDtypeStruct((B, S, ATT_W), BF16),
        scratch_shapes=[
            pltpu.VMEM((2, 2 * SB_PAIRS, t, 2 * t), BF16),
            pltpu.VMEM((2, 2 * SB_PAIRS, t, t), F32),
        ],
        compiler_params=_params(("arbitrary", "arbitrary", "arbitrary")),
    )(proj3, proj3, proj3)


def _layer_norm(r, g, b):
    mu = jnp.mean(r, axis=-1, keepdims=True)
    d = r - mu
    var = jnp.mean(d * d, axis=-1, keepdims=True)
    return d * lax.rsqrt(var + LN_EPS) * g + b


def _split_bf16(v):
    hi = v.astype(BF16)
    return hi, (v - hi.astype(F32)).astype(BF16)


def _pack_bf16_pairs(v):
    half = v.shape[1] // 2
    bits = pltpu.bitcast(v.astype(BF16).astype(F32), jnp.uint32)
    return bits[:, :half] | (bits[:, half:] >> 16)


def _unpack_bf16_pairs(w):
    return jnp.concatenate([pltpu.bitcast(w & jnp.uint32(0xFFFF0000), F32),
                            pltpu.bitcast(w << 16, F32)], axis=1)


def _merge_kernel(x_ref, ya_ref, yb_ref, wg_ref, wa_ref, wb_ref, wo_ref, g_ref, b_ref,
                  wr_ref, br_ref, h_ref, hp_ref, e_ref, p_ref, *, alpha, d):
    pa = jnp.dot(ya_ref[...], wa_ref[...], preferred_element_type=F32)
    pb = jnp.dot(yb_ref[...], wb_ref[...], preferred_element_type=F32)
    gates = jnp.dot(x_ref[...].astype(BF16), wg_ref[...], preferred_element_type=F32)
    merged = jax.nn.sigmoid(gates[:, :d]) * pa + jax.nn.sigmoid(gates[:, d:]) * pb
    m = jnp.dot(merged.astype(BF16), wo_ref[...], preferred_element_type=F32)
    h = _layer_norm(alpha * x_ref[...] + m, g_ref[...], b_ref[...])
    h_ref[...] = h
    hp_ref[...] = _pack_bf16_pairs(h)

    h_hi, h_lo = _split_bf16(h)
    w_hi, w_lo = _split_bf16(wr_ref[...])
    logit = (lax.dot_general(w_hi, h_hi, NT_DIMS, preferred_element_type=F32)
             + lax.dot_general(w_hi, h_lo, NT_DIMS, preferred_element_type=F32)
             + lax.dot_general(w_lo, h_hi, NT_DIMS, preferred_element_type=F32)) + br_ref[...]
    eid = lax.broadcasted_iota(I32, logit.shape, 0)
    vals, ids = [], []
    for _ in range(TOP_K):
        mx = jnp.max(logit, axis=0, keepdims=True)
        am = jnp.min(jnp.where(logit == mx, eid, N_EXPERTS), axis=0, keepdims=True)
        vals.append(mx)
        ids.append(am)
        logit = jnp.where(eid == am, -jnp.inf, logit)
    ex = [jnp.exp(v - vals[0]) for v in vals]
    den = ex[0] + ex[1] + ex[2] + ex[3]
    for k in range(TOP_K):
        e_ref[k:k + 1, :] = ids[k]
        p_ref[k:k + 1, :] = ex[k] / den


def _merge(x2, ya, yb, wg, wa, wb, wo, g, b, wr_t, br, alpha):
    T, D = x2.shape
    tm = MERGE_TM
    row = lambda i: (i, 0)
    fixed = lambda i: (0, 0)
    return pl.pallas_call(
        functools.partial(_merge_kernel, alpha=alpha, d=D),
        grid=(T // tm,),
        in_specs=[
            pl.BlockSpec((tm, D), row),
            pl.BlockSpec((tm, ATT_W), row),
            pl.BlockSpec((tm, ATT_W), row),
            pl.BlockSpec((D, 2 * D), fixed),
            pl.BlockSpec((ATT_W, D), fixed),
            pl.BlockSpec((ATT_W, D), fixed),
            pl.BlockSpec((D, D), fixed),
            pl.BlockSpec((1, D), fixed),
            pl.BlockSpec((1, D), fixed),
            pl.BlockSpec((N_EXPERTS, D), fixed),
            pl.BlockSpec((N_EXPERTS, 1), fixed),
        ],
        out_specs=[
            pl.BlockSpec((tm, D), row),
            pl.BlockSpec((tm, D // 2), row),
            pl.BlockSpec((TOP_K, tm), lambda i: (0, i)),
            pl.BlockSpec((TOP_K, tm), lambda i: (0, i)),
        ],
        out_shape=[
            jax.ShapeDtypeStruct((T, D), F32),
            jax.ShapeDtypeStruct((T, D // 2), jnp.uint32),
            jax.ShapeDtypeStruct((TOP_K, T), I32),
            jax.ShapeDtypeStruct((TOP_K, T), F32),
        ],
        compiler_params=_params(("arbitrary",)),
    )(x2, ya, yb, wg, wa, wb, wo, g, b, wr_t, br)


def _sc_gather_rows(table, idx):
    n = idx.shape[0]
    d = table.shape[1]
    info = plsc.get_sparse_core_info()
    n_cores, n_sub = info.num_cores, info.num_subcores
    per_w = n // (n_cores * n_sub)
    c = SC_GATHER_BYTES // (d * table.dtype.itemsize)
    n_g = per_w // c
    assert n == per_w * n_cores * n_sub and per_w == n_g * c and n_g % 2 == 0 and n_g >= 2
    mesh = plsc.VectorSubcoreMesh(core_axis_name="c", subcore_axis_name="s")

    @functools.partial(
        pl.kernel, mesh=mesh, out_type=jax.ShapeDtypeStruct((n, d), table.dtype),
        scratch_types=[pltpu.VMEM((per_w,), I32), pltpu.VMEM((2, c, d), table.dtype),
                       pltpu.SemaphoreType.DMA((2,)), pltpu.SemaphoreType.DMA((2,))])
    def gather_kernel(table_hbm, idx_hbm, out_hbm, idx_v, rows_v, gsem, wsem):
        base = (lax.axis_index("s") * n_cores + lax.axis_index("c")) * per_w
        pltpu.sync_copy(idx_hbm.at[pl.ds(base, per_w)], idx_v)

        def gather(g, b):
            return pltpu.make_async_copy(table_hbm.at[idx_v.at[pl.ds(g * c, c)]], rows_v.at[b],
                                         gsem.at[b])

        def write(g, b):
            return pltpu.make_async_copy(rows_v.at[b], out_hbm.at[pl.ds(base + g * c, c)], wsem.at[b])

        gather(0, 0).start()

        @pl.loop(0, n_g, step=2)
        def _ring(g0):
            for b in range(2):
                g = g0 + b

                @pl.when(g + 1 < n_g)
                def _next():
                    @pl.when(g >= 1)
                    def _buffer_free():
                        write(g - 1, 1 - b).wait()
                    gather(g + 1, 1 - b).start()

                gather(g, b).wait()
                write(g, b).start()

        write(n_g - 2, 0).wait()
        write(n_g - 1, 1).wait()

    return gather_kernel(table, idx)


def _moe_kernel(blk_e_ref, nused_ref, x_ref, wgu_ref, bgu_ref, wdn_ref, bdn_ref, o_ref,
                wgu_s, wdn_s, *, f):
    i = pl.program_id(0)
    nused = nused_ref[0]

    @pl.when(i < nused)
    def _compute():
        changed = jnp.logical_or(i == 0, blk_e_ref[i] != blk_e_ref[jnp.maximum(i - 1, 0)])

        @pl.when(changed)
        def _cast_weights():
            wgu_s[...] = wgu_ref[0].astype(BF16)
            wdn_s[...] = wdn_ref[0].astype(BF16)

        x = _unpack_bf16_pairs(x_ref[...]).astype(BF16)
        hgu = jnp.dot(x, wgu_s[...], preferred_element_type=F32) + bgu_ref[0]
        a = jnp.minimum(hgu[:, :f], SWIGLU_LIMIT)
        u = jnp.clip(hgu[:, f:], -SWIGLU_LIMIT, SWIGLU_LIMIT)
        glu = a * jax.nn.sigmoid(a * SWIGLU_ALPHA)
        y = jnp.dot(((u + 1.0) * glu).astype(BF16), wdn_s[...], preferred_element_type=F32) + bdn_ref[0]
        o_ref[...] = _pack_bf16_pairs(y)

    @pl.when(i >= nused)
    def _unused_block():
        o_ref[...] = jnp.zeros(o_ref.shape, o_ref.dtype)


def _moe_ffn(xs, blk_e, nused, w_gu, b_gu, w_dn, b_dn):
    P = xs.shape[0]
    E, D, F2 = w_gu.shape
    f = F2 // 2
    blk = MOE_BLK
    nb = P // blk
    used_block = lambda i, be, nu: (jnp.minimum(i, nu[0] - 1), 0)
    grid_spec = pltpu.PrefetchScalarGridSpec(
        num_scalar_prefetch=2,
        grid=(nb,),
        in_specs=[
            pl.BlockSpec((blk, D // 2), used_block),
            pl.BlockSpec((1, D, F2), lambda i, be, nu: (be[i], 0, 0)),
            pl.BlockSpec((1, 1, F2), lambda i, be, nu: (be[i], 0, 0)),
            pl.BlockSpec((1, f, D), lambda i, be, nu: (be[i], 0, 0)),
            pl.BlockSpec((1, 1, D), lambda i, be, nu: (be[i], 0, 0)),
        ],
        out_specs=pl.BlockSpec((blk, D // 2), lambda i, be, nu: (i, 0)),
        scratch_shapes=[
            pltpu.VMEM((D, F2), BF16),
            pltpu.VMEM((f, D), BF16),
        ],
    )
    return pl.pallas_call(
        functools.partial(_moe_kernel, f=f),
        grid_spec=grid_spec,
        out_shape=jax.ShapeDtypeStruct((P, D // 2), jnp.uint32),
        compiler_params=_params(("arbitrary",)),
    )(blk_e, nused, xs, w_gu, b_gu.reshape(E, 1, F2), w_dn, b_dn.reshape(E, 1, D))


def _comb_kernel(h_ref, p_ref, y_ref, g_ref, b_ref, o_ref, *, alpha):
    gate = p_ref[...]
    y = [_unpack_bf16_pairs(y_ref[k]) for k in range(TOP_K)]
    fsum = (y[0] * gate[:, 0:1] + y[1] * gate[:, 1:2]) + (y[2] * gate[:, 2:3] + y[3] * gate[:, 3:4])
    o_ref[...] = _layer_norm(alpha * h_ref[...] + fsum, g_ref[...], b_ref[...])


def _combine(h, y4, top_p, g, b, alpha):
    T, D = h.shape
    tm = COMB_TM
    return pl.pallas_call(
        functools.partial(_comb_kernel, alpha=alpha),
        grid=(T // tm,),
        in_specs=[
            pl.BlockSpec((tm, D), lambda i: (i, 0)),
            pl.BlockSpec((tm, TOP_K), lambda i: (i, 0)),
            pl.BlockSpec((TOP_K, tm, D // 2), lambda i: (0, i, 0)),
            pl.BlockSpec((1, D), lambda i: (0, 0)),
            pl.BlockSpec((1, D), lambda i: (0, 0)),
        ],
        out_specs=pl.BlockSpec((tm, D), lambda i: (i, 0)),
        out_shape=jax.ShapeDtypeStruct((T, D), F32),
        compiler_params=_params(("arbitrary",)),
    )(h, top_p.T, y4, g, b)


def _route(top_e, blk):
    K, T = top_e.shape
    N = K * T
    flat_e = top_e.reshape(N)
    experts = jnp.arange(N_EXPERTS, dtype=I32)
    order = jnp.argsort(flat_e, stable=True).astype(I32)
    inv = jnp.argsort(order).astype(I32)
    onehot = flat_e[:, None] == experts[None, :]
    counts = jnp.sum(onehot, axis=0, dtype=I32)
    padded = (counts + blk - 1) // blk * blk
    pends = jnp.cumsum(padded)
    offs = jnp.cumsum(counts) - counts
    shift = (pends - padded) - offs
    pos = inv + jnp.sum(jnp.where(onehot, shift[None, :], 0), axis=1, dtype=I32)
    P = N + N_EXPERTS * blk
    nb = P // blk
    blk_start = jnp.arange(nb, dtype=I32) * blk
    blk_e = jnp.minimum(jnp.sum(pends[None, :] <= blk_start[:, None], axis=1, dtype=I32), N_EXPERTS - 1)
    j = (blk_start - shift[blk_e])[:, None] + jnp.arange(blk, dtype=I32)[None, :]
    valid = j < (offs + counts)[blk_e][:, None]
    src = order[jnp.clip(j, 0, N - 1)]
    row_tok = jnp.where(valid, src % T, j % T).reshape(P)
    nused = (pends[-1:] // blk).astype(I32)
    return blk_e, nused, row_tok, pos.reshape(K, T)


def _projection_weights(w_in_l):
    sizes = (ATT_W, ATT_W, ATT_W, IDX_HEADS * IDX_DIM, IDX_DIM, IDX_HEADS, ATT_W, ATT_W, ATT_W)
    offs = np.concatenate([[0], np.cumsum(sizes)])
    qa, ka, va, qi, ki, wi, qb, kb, vb = (w_in_l[:, offs[n]:offs[n + 1]] for n in range(9))
    pad_wi = jnp.zeros((w_in_l.shape[0], LANES - IDX_HEADS), w_in_l.dtype)
    w_att = jnp.concatenate([qa, ka, qi, qb, kb, vb, ki, ki, wi, pad_wi], axis=1).astype(BF16)
    w_va_t = va.T.astype(BF16)
    w_gate = w_in_l[:, offs[9]:].astype(BF16)
    return w_att, w_va_t, w_gate


def kernel(x, w_in, w_branch_a, w_branch_b, w_out, rel_bias, ln1_g, ln1_b, w_router, b_router,
           w_gate_up, b_gate_up, w_down, b_down, ln2_g, ln2_b):
    B, S, D = x.shape
    depth = w_in.shape[0]
    alpha = (2 * depth) ** 0.25
    T = B * S
    h = x.reshape(T, D)
    for l in range(depth):
        w_att, w_va_t, w_gate = _projection_weights(w_in[l])
        proj, vt = _projection(h, w_att, w_va_t, min(T, PROJ_TM), PROJ_TN, DSA_BLK)
        proj = proj.reshape(B, S, ATT_COLS)
        vt = vt.reshape(B, S // DSA_BLK, ATT_W, DSA_BLK)
        ya = _dsa(proj, vt, rel_bias).reshape(T, ATT_W)
        yb = _stick_breaking(proj).reshape(T, ATT_W)
        h1, h1_packed, top_e, top_p = _merge(
            h, ya, yb, w_gate, w_branch_a[l].astype(BF16), w_branch_b[l].astype(BF16),
            w_out[l].astype(BF16), ln1_g[l].reshape(1, D), ln1_b[l].reshape(1, D),
            w_router[l].T, b_router[l].reshape(N_EXPERTS, 1), alpha)
        blk_e, nused, row_tok, pos = _route(top_e, MOE_BLK)
        xs = _sc_gather_rows(h1_packed, row_tok)
        ys = _moe_ffn(xs, blk_e, nused, w_gate_up[l], b_gate_up[l], w_down[l], b_down[l])
        y4 = _sc_gather_rows(ys, pos.reshape(TOP_K * T)).reshape(TOP_K, T, D // 2)
        h = _combine(h1, y4, top_p, ln2_g[l].reshape(1, D), ln2_b[l].reshape(1, D), alpha)
    return h.reshape(B, S, D)
```

```python
import functools
import math

import numpy as np
import jax
import jax.numpy as jnp
from jax import lax
from jax.experimental import pallas as pl
from jax.experimental.pallas import tpu as pltpu
from jax.experimental.pallas import tpu_sc as plsc

F32 = jnp.float32
BF16 = jnp.bfloat16
I32 = jnp.int32

A_HEADS = 8
HEAD_DIM = 64
ATT_W = A_HEADS * HEAD_DIM
IDX_HEADS = 8
IDX_DIM = 64
IDX_SCALE = (IDX_HEADS * IDX_DIM) ** -0.5
TOPK_MAX = 256
N_BUCKETS = 32
MAX_DISTANCE = 128
N_EXPERTS = 32
TOP_K = 4
SWIGLU_LIMIT = 7.0
SWIGLU_ALPHA = 1.702
LN_EPS = 1e-5
QK_SCALE = HEAD_DIM ** -0.5

LANES = 128
SUBLANES = 8
HALF = LANES // 2
N_PAIRS = A_HEADS // 2
VMEM_LIMIT = 56 * 1024 * 1024

DSA_BLK = 256
SB_T = 256
SB_PAIRS = 4
PROJ_TM = 1024
MERGE_TM = 512
MOE_BLK = 512
COMB_TM = 512
SC_GATHER_BYTES = 128 * 1024
REDUCE_CHAINS = 8
BISECT_CAP = 24
BISECT_FREE = 16
NEG = -1e30

COL_QA, COL_KA, COL_QI, COL_QB, COL_KB, COL_VB = (g * ATT_W for g in range(6))
COL_KK = 6 * ATT_W
COL_WI = COL_KK + LANES
ATT_COLS = COL_WI + LANES
PROJ_TN = ATT_COLS // 2

NT_DIMS = (((1,), (1,)), ((), ()))


def _params(sem, vmem=VMEM_LIMIT):
    return pltpu.CompilerParams(dimension_semantics=sem, vmem_limit_bytes=vmem)


def _proj_kernel(x_ref, w_ref, wt_ref, o_ref, ot_ref, xb_ref, *, tt):
    @pl.when(pl.program_id(1) == 0)
    def _row_tile_start():
        xb_ref[...] = x_ref[...].astype(BF16)
        for r in range(ot_ref.shape[0]):
            ot_ref[r] = lax.dot_general(wt_ref[...], xb_ref[r * tt:(r + 1) * tt, :], NT_DIMS,
                                        preferred_element_type=F32).astype(ot_ref.dtype)

    o_ref[...] = jnp.dot(xb_ref[...], w_ref[...], preferred_element_type=F32).astype(o_ref.dtype)


def _projection(x, w, w_t, tm, tn, tt):
    M, K = x.shape
    N = w.shape[1]
    Nt = w_t.shape[0]
    return pl.pallas_call(
        functools.partial(_proj_kernel, tt=tt),
        grid=(M // tm, N // tn),
        in_specs=[pl.BlockSpec((tm, K), lambda i, j: (i, 0)),
                  pl.BlockSpec((K, tn), lambda i, j: (0, j)),
                  pl.BlockSpec((Nt, K), lambda i, j: (0, 0))],
        out_specs=[pl.BlockSpec((tm, tn), lambda i, j: (i, j)),
                   pl.BlockSpec((tm // tt, Nt, tt), lambda i, j: (i, 0, 0))],
        out_shape=[jax.ShapeDtypeStruct((M, N), BF16),
                   jax.ShapeDtypeStruct((M // tt, Nt, tt), BF16)],
        scratch_shapes=[pltpu.VMEM((tm, K), BF16)],
        compiler_params=_params(("arbitrary", "arbitrary")),
    )(x, w, w_t)


def _t5_bucket_np(n):
    n = np.maximum(n, 0)
    max_exact = N_BUCKETS // 2
    nf = np.maximum(n, 1).astype(np.float32)
    large = max_exact + (np.log(nf / max_exact) / math.log(MAX_DISTANCE / max_exact)
                         * (N_BUCKETS - max_exact)).astype(np.int32)
    large = np.minimum(large, N_BUCKETS - 1)
    return np.where(n < max_exact, n, large).astype(np.int32)


def _dsa_n_off(blk):
    return 2 + -(-MAX_DISTANCE // blk)


def _dsa_bucket_tiles(blk):
    n_off = _dsa_n_off(blk)
    j = np.arange(blk)[None, :, None]
    i = np.arange(blk)[None, None, :]
    o = np.arange(n_off)[:, None, None]
    return _t5_bucket_np(i - j + blk * (n_off - 1 - o))


def _dsa_kernel(bucket_ref, relb_ref, q_ref, k_ref, vt_ref, qi_ref, kk_ref, wi_ref, o_ref,
                sc_ref, bias_ref, m_ref, l_ref, acc_ref, st_ref, mx_ref,
                *, blk, n_chunks, n_sel, n_off):
    b = pl.program_id(0)
    i = pl.program_id(1)
    q0 = i * blk
    nck = i + 1
    groups = blk // SUBLANES

    @pl.when(jnp.logical_and(b == 0, i == 0))
    def _build_bias():
        def head_body(h, _):
            for o in range(n_off):
                for rb in range(blk // LANES):
                    for cb in range(blk // LANES):
                        rs = slice(rb * LANES, (rb + 1) * LANES)
                        cs = slice(cb * LANES, (cb + 1) * LANES)
                        bk = bucket_ref[o, rs, cs]

                        def bucket_body(n, acc):
                            return jnp.where(bk == n, relb_ref[n, h], acc)

                        bias_ref[h, o, rs, cs] = lax.fori_loop(
                            0, N_BUCKETS, bucket_body, jnp.zeros((LANES, LANES), F32))
            return 0

        lax.fori_loop(0, A_HEADS, head_body, 0)

    lane = lax.broadcasted_iota(I32, (blk, LANES), 1)
    lo_half = lane < HALF
    krow = lax.broadcasted_iota(I32, (blk, blk), 0)
    qpos = q0 + lax.broadcasted_iota(I32, (1, blk), 1)

    def pair_split(ref, scale):
        out = []
        for p in range(N_PAIRS):
            v = ref[0, :, p * LANES:(p + 1) * LANES].astype(F32)
            if scale != 1.0:
                v = v * scale
            out.append(jnp.where(lo_half, v, 0.0).astype(BF16))
            out.append(jnp.where(lo_half, 0.0, v).astype(BF16))
        return out

    wi_t = wi_ref[0].astype(F32).T
    wrow = [wi_t[h:h + 1, :] * IDX_SCALE for h in range(IDX_HEADS)]
    qi_m = pair_split(qi_ref, 1.0)

    def score_chunk(c, _):
        c0 = pl.multiple_of(c * blk, blk)
        kk = kk_ref[0, pl.ds(c0, blk), :]
        acc = jnp.zeros((blk, blk), F32)
        for h in range(IDX_HEADS):
            s = lax.dot_general(kk, qi_m[h], NT_DIMS, preferred_element_type=F32)
            acc = acc + wrow[h] * jnp.maximum(s, 0.0)
        sc_ref[c] = jnp.where(c0 + krow <= qpos, acc, -jnp.inf)
        return 0

    lax.fori_loop(0, nck, score_chunk, 0)

    kt = jnp.minimum(qpos + 1, n_sel).astype(F32)

    def fold(fn, init):
        def body(c, acc):
            return fn(acc, sc_ref[c])
        return lax.fori_loop(0, nck, body, init)

    def part(x, op):
        y = op(x.reshape(REDUCE_CHAINS, groups // REDUCE_CHAINS, SUBLANES, blk), axis=1)
        return op(y, axis=0)

    def fin(x, op):
        return op(x, axis=0, keepdims=True)

    zeros8 = jnp.zeros((SUBLANES, blk), F32)
    pinf8 = jnp.full((SUBLANES, blk), jnp.inf, F32)

    def count_ge(th):
        return fin(fold(lambda a, s: a + part(jnp.where(s >= th, 1.0, 0.0), jnp.sum), zeros8),
                   jnp.sum)

    mn, mx = fold(lambda a, s: (
        jnp.minimum(a[0], part(jnp.where(s == -jnp.inf, jnp.inf, s), jnp.min)),
        jnp.maximum(a[1], part(s, jnp.max))), (pinf8, -pinf8))
    rmin = fin(mn, jnp.min)
    rmax = fin(mx, jnp.max)

    def bis_cond(st):
        it, lo, hi, clo = st
        return jnp.logical_and(it < BISECT_CAP, jnp.max(jnp.abs(clo - kt)) > 0.0)

    def halve(lo, hi, clo):
        mid = 0.5 * lo + 0.5 * hi
        c = count_ge(mid)
        active = clo != kt
        up = jnp.logical_and(active, c >= kt)
        dn = jnp.logical_and(active, c < kt)
        return jnp.where(up, mid, lo), jnp.where(dn, mid, hi), jnp.where(up, c, clo)

    def bis_body(st):
        it, lo, hi, clo = st
        return (it + 2,) + halve(*halve(lo, hi, clo))

    start = (rmin, rmax + jnp.maximum(1.0, jnp.abs(rmax) * 2.0 ** -20), (qpos + 1).astype(F32))
    start = lax.fori_loop(0, BISECT_FREE, lambda _, st: halve(*st), start)
    _, lo, _, _ = lax.while_loop(bis_cond, bis_body, (jnp.int32(BISECT_FREE),) + start)

    def stats(lo_):
        a_ = fin(fold(lambda a, s: jnp.minimum(a, part(jnp.where(s >= lo_, s, jnp.inf), jnp.min)),
                      pinf8), jnp.min)
        cg, ct, nx = fold(
            lambda a, s: (a[0] + part(jnp.where(s > a_, 1.0, 0.0), jnp.sum),
                               a[1] + part(jnp.where(s == a_, 1.0, 0.0), jnp.sum),
                               jnp.minimum(a[2], part(jnp.where(s > a_, s, jnp.inf), jnp.min))),
            (zeros8, zeros8, pinf8))
        return a_, fin(cg, jnp.sum), fin(ct, jnp.sum), fin(nx, jnp.min)

    def fin_cond(st):
        return st[0]

    def fin_body(st):
        _, lo_, _, _ = st
        a_, cgt_, nt_, nxt_ = stats(lo_)
        bad = cgt_ >= kt
        return (jnp.max(jnp.where(bad, 1.0, 0.0)) > 0.0, jnp.where(bad, nxt_, a_), cgt_, nt_)

    _, a, cgt, nties = lax.while_loop(fin_cond, fin_body, (jnp.bool_(True), lo, kt, kt))
    need = kt - cgt
    excess = jnp.max(jnp.where(nties > need, 1.0, 0.0)) > 0.0

    def mask_plain():
        def body(c, _):
            sc_ref[c] = jnp.where(sc_ref[c] >= a, 0.0, NEG)
            return 0
        lax.fori_loop(0, nck, body, 0)

    def mask_ties():
        upto = (krow >= lax.broadcasted_iota(I32, (blk, blk), 1)).astype(BF16)

        def body(c, seen):
            s = sc_ref[c]
            tie = s == a
            rank = jnp.dot(upto, jnp.where(tie, 1.0, 0.0).astype(BF16),
                           preferred_element_type=F32) + seen
            sel = jnp.logical_or(s > a, jnp.logical_and(tie, rank <= need))
            sc_ref[c] = jnp.where(sel, 0.0, NEG)
            return rank[blk - 1:blk, :]

        lax.fori_loop(0, nck, body, jnp.zeros((1, blk), F32))

    lax.cond(excess, mask_ties, mask_plain)

    m_ref[...] = jnp.full(m_ref.shape, NEG, F32)
    l_ref[...] = jnp.zeros(l_ref.shape, F32)
    acc_ref[...] = jnp.zeros(acc_ref.shape, F32)
    q_m = pair_split(q_ref, QK_SCALE)

    def stage_logits(c, slot):
        c = jnp.minimum(c, n_chunks - 1)
        c0 = pl.multiple_of(c * blk, blk)
        madd = sc_ref[c]
        o_idx = jnp.clip(c - i + (n_off - 1), 0, n_off - 1)
        for p in range(N_PAIRS):
            k2 = k_ref[0, pl.ds(c0, blk), p * LANES:(p + 1) * LANES]
            for hh in range(2):
                h = 2 * p + hh
                s = lax.dot_general(k2, q_m[h], NT_DIMS, preferred_element_type=F32)
                s = s + bias_ref[h, o_idx] + madd
                st_ref[slot, h] = s
                mx_ref[slot, h] = fin(part(s, jnp.max), jnp.max)

    def stage_values(c, slot):
        for p in range(N_PAIRS):
            vt2 = vt_ref[0, c, p * LANES:(p + 1) * LANES, :]
            for hh in range(2):
                h = 2 * p + hh
                m_old = m_ref[h]
                m_new = jnp.maximum(m_old, mx_ref[slot, h])
                alpha = jnp.exp(m_old - m_new)
                pexp = jnp.exp(st_ref[slot, h] - m_new)
                l_ref[h] = alpha * l_ref[h] + fin(part(pexp, jnp.sum), jnp.sum)
                acc_ref[h] = alpha * acc_ref[h] + jnp.dot(vt2, pexp.astype(BF16),
                                                          preferred_element_type=F32)
                m_ref[h] = m_new

    stage_logits(0, 0)

    def att_pair(pp, _):
        c = 2 * pp
        stage_logits(c + 1, 1)
        stage_values(c, 0)
        stage_logits(c + 2, 0)
        stage_values(c + 1, 1)
        return 0

    lax.fori_loop(0, nck // 2, att_pair, 0)

    @pl.when(lax.rem(nck, 2) == 1)
    def _last_chunk():
        stage_values(nck - 1, 0)

    lo_rows = lax.broadcasted_iota(I32, (LANES, blk), 0) < HALF
    for p in range(N_PAIRS):
        oa = acc_ref[2 * p] / l_ref[2 * p]
        ob = acc_ref[2 * p + 1] / l_ref[2 * p + 1]
        o_ref[0, :, p * LANES:(p + 1) * LANES] = jnp.where(lo_rows, oa, ob).T.astype(o_ref.dtype)


def _dsa(proj3, vt4, rel_bias):
    B, S, _ = proj3.shape
    blk = DSA_BLK
    n_off = _dsa_n_off(blk)
    n_sel = min(TOPK_MAX, S // 4)
    bucket = jnp.asarray(_dsa_bucket_tiles(blk))
    n_chunks = S // blk
    assert S % blk == 0
    kern = functools.partial(_dsa_kernel, blk=blk, n_chunks=n_chunks, n_sel=n_sel, n_off=n_off)
    return pl.pallas_call(
        kern,
        grid=(B, S // blk),
        in_specs=[
            pl.BlockSpec((n_off, blk, blk), lambda b, i: (0, 0, 0)),
            pl.BlockSpec(memory_space=pltpu.SMEM),
            pl.BlockSpec((1, blk, ATT_W), lambda b, i: (b, i, COL_QA // ATT_W)),
            pl.BlockSpec((1, S, ATT_W), lambda b, i: (b, 0, COL_KA // ATT_W)),
            pl.BlockSpec((1, S // blk, ATT_W, blk), lambda b, i: (b, 0, 0, 0)),
            pl.BlockSpec((1, blk, ATT_W), lambda b, i: (b, i, COL_QI // ATT_W)),
            pl.BlockSpec((1, S, LANES), lambda b, i: (b, 0, COL_KK // LANES)),
            pl.BlockSpec((1, blk, LANES), lambda b, i: (b, i, COL_WI // LANES)),
        ],
        out_specs=pl.BlockSpec((1, blk, ATT_W), lambda b, i: (b, i, 0)),
        out_shape=jax.ShapeDtypeStruct((B, S, ATT_W), BF16),
        scratch_shapes=[
            pltpu.VMEM((S // blk, blk, blk), F32),
            pltpu.VMEM((A_HEADS, n_off, blk, blk), F32),
            pltpu.VMEM((A_HEADS, 1, blk), F32),
            pltpu.VMEM((A_HEADS, 1, blk), F32),
            pltpu.VMEM((A_HEADS, LANES, blk), F32),
            pltpu.VMEM((2, A_HEADS, blk, blk), F32),
            pltpu.VMEM((2, A_HEADS, 1, blk), F32),
        ],
        compiler_params=_params(("arbitrary", "arbitrary")),
    )(bucket, rel_bias, proj3, proj3, vt4, proj3, proj3, proj3)


def _sb_kernel(q_ref, k_ref, v_ref, o_ref, hl_ref, z_ref, *, t):
    i = pl.program_id(2)
    n = i + 1
    lane = lax.broadcasted_iota(I32, (t, LANES), 1)
    lo_half = lane < HALF
    q_m = []
    for pr in range(SB_PAIRS):
        q2 = q_ref[0, :, pr * LANES:(pr + 1) * LANES].astype(F32) * QK_SCALE
        q_m += [jnp.where(lo_half, q2, 0.0).astype(BF16), jnp.where(lo_half, 0.0, q2).astype(BF16)]
    heads = 2 * SB_PAIRS
    r = lax.broadcasted_iota(I32, (t, t), 0)
    cidx = lax.broadcasted_iota(I32, (t, t), 1)
    neg_from = jnp.where(r >= cidx, -1.0, 0.0).astype(BF16)
    neg_from2 = jnp.concatenate([neg_from, neg_from], axis=0)
    diff = cidx - r

    def stage_terms(step, slot, diagonal):
        c0 = pl.multiple_of(jnp.maximum(i - step, 0) * t, t)
        if diagonal:
            keep = diff < 0
        for hh in range(heads):
            pr = hh // 2
            k2 = k_ref[0, pl.ds(c0, t), pr * LANES:(pr + 1) * LANES]
            z = lax.dot_general(q_m[hh], k2, NT_DIMS, preferred_element_type=F32)
            sp = jnp.maximum(z, 0.0) + jnp.log(1.0 + jnp.exp(-jnp.abs(z)))
            if diagonal:
                sp = jnp.where(keep, sp, 0.0)
                z = jnp.where(keep, z, NEG)
            hi = sp.astype(BF16)
            hl_ref[slot, hh, :, :t] = hi
            hl_ref[slot, hh, :, t:] = (sp - hi.astype(F32)).astype(BF16)
            z_ref[slot, hh] = z

    def stage_apply(step, slot, carry):
        c0 = pl.multiple_of((i - step) * t, t)
        cum_all = jnp.dot(hl_ref[slot].reshape(heads * t, 2 * t), neg_from2, preferred_element_type=F32)
        out = []
        for hh in range(heads):
            pr = hh // 2
            v2 = v_ref[0, pl.ds(c0, t), pr * LANES:(pr + 1) * LANES]
            car, acc = carry[hh]
            cum = cum_all[hh * t:(hh + 1) * t]
            w = jnp.exp(z_ref[slot, hh] + cum + car)
            acc = acc + jnp.dot(w.astype(BF16), v2, preferred_element_type=F32)
            out.append((car + cum[:, 0:1], acc))
        return tuple(out)

    z1 = jnp.zeros((t, 1), F32)
    za = jnp.zeros((t, LANES), F32)
    stage_terms(0, 0, True)

    def pair_body(pp, carry):
        step = 2 * pp
        stage_terms(step + 1, 1, False)
        carry = stage_apply(step, 0, carry)
        stage_terms(step + 2, 0, False)
        return stage_apply(step + 1, 1, carry)

    carry = lax.fori_loop(0, n // 2, pair_body, ((z1, za),) * heads)
    carry = lax.cond(lax.rem(n, 2) == 1, lambda c: stage_apply(n - 1, 0, c), lambda c: c, carry)
    for pr in range(SB_PAIRS):
        o_ref[0, :, pr * LANES:(pr + 1) * LANES] = jnp.where(
            lo_half, carry[2 * pr][1], carry[2 * pr + 1][1]).astype(o_ref.dtype)


def _stick_breaking(proj3):
    B, S, _ = proj3.shape
    t = SB_T
    w = SB_PAIRS * LANES
    qb, kb, vb = COL_QB // w, COL_KB // w, COL_VB // w
    return pl.pallas_call(
        functools.partial(_sb_kernel, t=t),
        grid=(B, N_PAIRS // SB_PAIRS, S // t),
        in_specs=[
            pl.BlockSpec((1, t, w), lambda b, p, i: (b, i, qb + p)),
            pl.BlockSpec((1, S, w), lambda b, p, i: (b, 0, kb + p)),
            pl.BlockSpec((1, S, w), lambda b, p, i: (b, 0, vb + p)),
        ],
        out_specs=pl.BlockSpec((1, t, w), lambda b, p, i: (b, i, p)),
        out_shape=jax.ShapeDtypeStruct((B, S, ATT_W), BF16),
        scratch_shapes=[
            pltpu.VMEM((2, 2 * SB_PAIRS, t, 2 * t), BF16),
            pltpu.VMEM((2, 2 * SB_PAIRS, t, t), F32),
        ],
        compiler_params=_params(("arbitrary", "arbitrary", "arbitrary")),
    )(proj3, proj3, proj3)


def _layer_norm(r, g, b):
    mu = jnp.mean(r, axis=-1, keepdims=True)
    d = r - mu
    var = jnp.mean(d * d, axis=-1, keepdims=True)
    return d * lax.rsqrt(var + LN_EPS) * g + b


def _split_bf16(v):
    hi = v.astype(BF16)
    return hi, (v - hi.astype(F32)).astype(BF16)


def _pack_bf16_pairs(v):
    half = v.shape[1] // 2
    bits = pltpu.bitcast(v.astype(BF16).astype(F32), jnp.uint32)
    return bits[:, :half] | (bits[:, half:] >> 16)


def _unpack_bf16_pairs(w):
    return jnp.concatenate([pltpu.bitcast(w & jnp.uint32(0xFFFF0000), F32),
                            pltpu.bitcast(w << 16, F32)], axis=1)


def _merge_kernel(x_ref, ya_ref, yb_ref, wg_ref, wa_ref, wb_ref, wo_ref, g_ref, b_ref,
                  wr_ref, br_ref, h_ref, hp_ref, e_ref, p_ref, *, alpha, d):
    pa = jnp.dot(ya_ref[...], wa_ref[...], preferred_element_type=F32)
    pb = jnp.dot(yb_ref[...], wb_ref[...], preferred_element_type=F32)
    gates = jnp.dot(x_ref[...].astype(BF16), wg_ref[...], preferred_element_type=F32)
    merged = jax.nn.sigmoid(gates[:, :d]) * pa + jax.nn.sigmoid(gates[:, d:]) * pb
    m = jnp.dot(merged.astype(BF16), wo_ref[...], preferred_element_type=F32)
    h = _layer_norm(alpha * x_ref[...] + m, g_ref[...], b_ref[...])
    h_ref[...] = h
    hp_ref[...] = _pack_bf16_pairs(h)

    h_hi, h_lo = _split_bf16(h)
    w_hi, w_lo = _split_bf16(wr_ref[...])
    logit = (lax.dot_general(w_hi, h_hi, NT_DIMS, preferred_element_type=F32)
             + lax.dot_general(w_hi, h_lo, NT_DIMS, preferred_element_type=F32)
             + lax.dot_general(w_lo, h_hi, NT_DIMS, preferred_element_type=F32)) + br_ref[...]
    eid = lax.broadcasted_iota(I32, logit.shape, 0)
    vals, ids = [], []
    for _ in range(TOP_K):
        mx = jnp.max(logit, axis=0, keepdims=True)
        am = jnp.min(jnp.where(logit == mx, eid, N_EXPERTS), axis=0, keepdims=True)
        vals.append(mx)
        ids.append(am)
        logit = jnp.where(eid == am, -jnp.inf, logit)
    ex = [jnp.exp(v - vals[0]) for v in vals]
    den = ex[0] + ex[1] + ex[2] + ex[3]
    for k in range(TOP_K):
        e_ref[k:k + 1, :] = ids[k]
        p_ref[k:k + 1, :] = ex[k] / den


def _merge(x2, ya, yb, wg, wa, wb, wo, g, b, wr_t, br, alpha):
    T, D = x2.shape
    tm = MERGE_TM
    row = lambda i: (i, 0)
    fixed = lambda i: (0, 0)
    return pl.pallas_call(
        functools.partial(_merge_kernel, alpha=alpha, d=D),
        grid=(T // tm,),
        in_specs=[
            pl.BlockSpec((tm, D), row),
            pl.BlockSpec((tm, ATT_W), row),
            pl.BlockSpec((tm, ATT_W), row),
            pl.BlockSpec((D, 2 * D), fixed),
            pl.BlockSpec((ATT_W, D), fixed),
            pl.BlockSpec((ATT_W, D), fixed),
            pl.BlockSpec((D, D), fixed),
            pl.BlockSpec((1, D), fixed),
            pl.BlockSpec((1, D), fixed),
            pl.BlockSpec((N_EXPERTS, D), fixed),
            pl.BlockSpec((N_EXPERTS, 1), fixed),
        ],
        out_specs=[
            pl.BlockSpec((tm, D), row),
            pl.BlockSpec((tm, D // 2), row),
            pl.BlockSpec((TOP_K, tm), lambda i: (0, i)),
            pl.BlockSpec((TOP_K, tm), lambda i: (0, i)),
        ],
        out_shape=[
            jax.ShapeDtypeStruct((T, D), F32),
            jax.ShapeDtypeStruct((T, D // 2), jnp.uint32),
            jax.ShapeDtypeStruct((TOP_K, T), I32),
            jax.ShapeDtypeStruct((TOP_K, T), F32),
        ],
        compiler_params=_params(("arbitrary",)),
    )(x2, ya, yb, wg, wa, wb, wo, g, b, wr_t, br)


def _sc_gather_rows(table, idx):
    n = idx.shape[0]
    d = table.shape[1]
    info = plsc.get_sparse_core_info()
    n_cores, n_sub = info.num_cores, info.num_subcores
    per_w = n // (n_cores * n_sub)
    c = SC_GATHER_BYTES // (d * table.dtype.itemsize)
    n_g = per_w // c
    assert n == per_w * n_cores * n_sub and per_w == n_g * c and n_g % 2 == 0 and n_g >= 2
    mesh = plsc.VectorSubcoreMesh(core_axis_name="c", subcore_axis_name="s")

    @functools.partial(
        pl.kernel, mesh=mesh, out_type=jax.ShapeDtypeStruct((n, d), table.dtype),
        scratch_types=[pltpu.VMEM((per_w,), I32), pltpu.VMEM((2, c, d), table.dtype),
                       pltpu.SemaphoreType.DMA((2,)), pltpu.SemaphoreType.DMA((2,))])
    def gather_kernel(table_hbm, idx_hbm, out_hbm, idx_v, rows_v, gsem, wsem):
        base = (lax.axis_index("s") * n_cores + lax.axis_index("c")) * per_w
        pltpu.sync_copy(idx_hbm.at[pl.ds(base, per_w)], idx_v)

        def gather(g, b):
            return pltpu.make_async_copy(table_hbm.at[idx_v.at[pl.ds(g * c, c)]], rows_v.at[b],
                                         gsem.at[b])

        def write(g, b):
            return pltpu.make_async_copy(rows_v.at[b], out_hbm.at[pl.ds(base + g * c, c)], wsem.at[b])

        gather(0, 0).start()

        @pl.loop(0, n_g, step=2)
        def _ring(g0):
            for b in range(2):
                g = g0 + b

                @pl.when(g + 1 < n_g)
                def _next():
                    @pl.when(g >= 1)
                    def _buffer_free():
                        write(g - 1, 1 - b).wait()
                    gather(g + 1, 1 - b).start()

                gather(g, b).wait()
                write(g, b).start()

        write(n_g - 2, 0).wait()
        write(n_g - 1, 1).wait()

    return gather_kernel(table, idx)


def _moe_kernel(blk_e_ref, nused_ref, x_ref, wgu_ref, bgu_ref, wdn_ref, bdn_ref, o_ref,
                wgu_s, wdn_s, *, f):
    i = pl.program_id(0)
    nused = nused_ref[0]

    @pl.when(i < nused)
    def _compute():
        changed = jnp.logical_or(i == 0, blk_e_ref[i] != blk_e_ref[jnp.maximum(i - 1, 0)])

        @pl.when(changed)
        def _cast_weights():
            wgu_s[...] = wgu_ref[0].astype(BF16)
            wdn_s[...] = wdn_ref[0].astype(BF16)

        x = _unpack_bf16_pairs(x_ref[...]).astype(BF16)
        hgu = jnp.dot(x, wgu_s[...], preferred_element_type=F32) + bgu_ref[0]
        a = jnp.minimum(hgu[:, :f], SWIGLU_LIMIT)
        u = jnp.clip(hgu[:, f:], -SWIGLU_LIMIT, SWIGLU_LIMIT)
        glu = a * jax.nn.sigmoid(a * SWIGLU_ALPHA)
        y = jnp.dot(((u + 1.0) * glu).astype(BF16), wdn_s[...], preferred_element_type=F32) + bdn_ref[0]
        o_ref[...] = _pack_bf16_pairs(y)

    @pl.when(i >= nused)
    def _unused_block():
        o_ref[...] = jnp.zeros(o_ref.shape, o_ref.dtype)


def _moe_ffn(xs, blk_e, nused, w_gu, b_gu, w_dn, b_dn):
    P = xs.shape[0]
    E, D, F2 = w_gu.shape
    f = F2 // 2
    blk = MOE_BLK
    nb = P // blk
    used_block = lambda i, be, nu: (jnp.minimum(i, nu[0] - 1), 0)
    grid_spec = pltpu.PrefetchScalarGridSpec(
        num_scalar_prefetch=2,
        grid=(nb,),
        in_specs=[
            pl.BlockSpec((blk, D // 2), used_block),
            pl.BlockSpec((1, D, F2), lambda i, be, nu: (be[i], 0, 0)),
            pl.BlockSpec((1, 1, F2), lambda i, be, nu: (be[i], 0, 0)),
            pl.BlockSpec((1, f, D), lambda i, be, nu: (be[i], 0, 0)),
            pl.BlockSpec((1, 1, D), lambda i, be, nu: (be[i], 0, 0)),
        ],
        out_specs=pl.BlockSpec((blk, D // 2), lambda i, be, nu: (i, 0)),
        scratch_shapes=[
            pltpu.VMEM((D, F2), BF16),
            pltpu.VMEM((f, D), BF16),
        ],
    )
    return pl.pallas_call(
        functools.partial(_moe_kernel, f=f),
        grid_spec=grid_spec,
        out_shape=jax.ShapeDtypeStruct((P, D // 2), jnp.uint32),
        compiler_params=_params(("arbitrary",)),
    )(blk_e, nused, xs, w_gu, b_gu.reshape(E, 1, F2), w_dn, b_dn.reshape(E, 1, D))


def _comb_kernel(h_ref, p_ref, y_ref, g_ref, b_ref, o_ref, *, alpha):
    gate = p_ref[...]
    y = [_unpack_bf16_pairs(y_ref[k]) for k in range(TOP_K)]
    fsum = (y[0] * gate[:, 0:1] + y[1] * gate[:, 1:2]) + (y[2] * gate[:, 2:3] + y[3] * gate[:, 3:4])
    o_ref[...] = _layer_norm(alpha * h_ref[...] + fsum, g_ref[...], b_ref[...])


def _combine(h, y4, top_p, g, b, alpha):
    T, D = h.shape
    tm = COMB_TM
    return pl.pallas_call(
        functools.partial(_comb_kernel, alpha=alpha),
        grid=(T // tm,),
        in_specs=[
            pl.BlockSpec((tm, D), lambda i: (i, 0)),
            pl.BlockSpec((tm, TOP_K), lambda i: (i, 0)),
            pl.BlockSpec((TOP_K, tm, D // 2), lambda i: (0, i, 0)),
            pl.BlockSpec((1, D), lambda i: (0, 0)),
            pl.BlockSpec((1, D), lambda i: (0, 0)),
        ],
        out_specs=pl.BlockSpec((tm, D), lambda i: (i, 0)),
        out_shape=jax.ShapeDtypeStruct((T, D), F32),
        compiler_params=_params(("arbitrary",)),
    )(h, top_p.T, y4, g, b)


def _route(top_e, blk):
    K, T = top_e.shape
    N = K * T
    flat_e = top_e.reshape(N)
    experts = jnp.arange(N_EXPERTS, dtype=I32)
    order = jnp.argsort(flat_e, stable=True).astype(I32)
    inv = jnp.argsort(order).astype(I32)
    onehot = flat_e[:, None] == experts[None, :]
    counts = jnp.sum(onehot, axis=0, dtype=I32)
    padded = (counts + blk - 1) // blk * blk
    pends = jnp.cumsum(padded)
    offs = jnp.cumsum(counts) - counts
    shift = (pends - padded) - offs
    pos = inv + jnp.sum(jnp.where(onehot, shift[None, :], 0), axis=1, dtype=I32)
    P = N + N_EXPERTS * blk
    nb = P // blk
    blk_start = jnp.arange(nb, dtype=I32) * blk
    blk_e = jnp.minimum(jnp.sum(pends[None, :] <= blk_start[:, None], axis=1, dtype=I32), N_EXPERTS - 1)
    j = (blk_start - shift[blk_e])[:, None] + jnp.arange(blk, dtype=I32)[None, :]
    valid = j < (offs + counts)[blk_e][:, None]
    src = order[jnp.clip(j, 0, N - 1)]
    row_tok = jnp.where(valid, src % T, j % T).reshape(P)
    nused = (pends[-1:] // blk).astype(I32)
    return blk_e, nused, row_tok, pos.reshape(K, T)


def _projection_weights(w_in_l):
    sizes = (ATT_W, ATT_W, ATT_W, IDX_HEADS * IDX_DIM, IDX_DIM, IDX_HEADS, ATT_W, ATT_W, ATT_W)
    offs = np.concatenate([[0], np.cumsum(sizes)])
    qa, ka, va, qi, ki, wi, qb, kb, vb = (w_in_l[:, offs[n]:offs[n + 1]] for n in range(9))
    pad_wi = jnp.zeros((w_in_l.shape[0], LANES - IDX_HEADS), w_in_l.dtype)
    w_att = jnp.concatenate([qa, ka, qi, qb, kb, vb, ki, ki, wi, pad_wi], axis=1).astype(BF16)
    w_va_t = va.T.astype(BF16)
    w_gate = w_in_l[:, offs[9]:].astype(BF16)
    return w_att, w_va_t, w_gate


def kernel(x, w_in, w_branch_a, w_branch_b, w_out, rel_bias, ln1_g, ln1_b, w_router, b_router,
           w_gate_up, b_gate_up, w_down, b_down, ln2_g, ln2_b):
    B, S, D = x.shape
    depth = w_in.shape[0]
    alpha = (2 * depth) ** 0.25
    T = B * S
    h = x.reshape(T, D)
    for l in range(depth):
        w_att, w_va_t, w_gate = _projection_weights(w_in[l])
        proj, vt = _projection(h, w_att, w_va_t, min(T, PROJ_TM), PROJ_TN, DSA_BLK)
        proj = proj.reshape(B, S, ATT_COLS)
        vt = vt.reshape(B, S // DSA_BLK, ATT_W, DSA_BLK)
        ya = _dsa(proj, vt, rel_bias).reshape(T, ATT_W)
        yb = _stick_breaking(proj).reshape(T, ATT_W)
        h1, h1_packed, top_e, top_p = _merge(
            h, ya, yb, w_gate, w_branch_a[l].astype(BF16), w_branch_b[l].astype(BF16),
            w_out[l].astype(BF16), ln1_g[l].reshape(1, D), ln1_b[l].reshape(1, D),
            w_router[l].T, b_router[l].reshape(N_EXPERTS, 1), alpha)
        blk_e, nused, row_tok, pos = _route(top_e, MOE_BLK)
        xs = _sc_gather_rows(h1_packed, row_tok)
        ys = _moe_ffn(xs, blk_e, nused, w_gate_up[l], b_gate_up[l], w_down[l], b_down[l])
        y4 = _sc_gather_rows(ys, pos.reshape(TOP_K * T)).reshape(TOP_K, T, D // 2)
        h = _combine(h1, y4, top_p, ln2_g[l].reshape(1, D), ln2_b[l].reshape(1, D), alpha)
    return h.reshape(B, S, D)
```

```python
import functools
import math

import numpy as np
import jax
import jax.numpy as jnp
from jax import lax
from jax.experimental import pallas as pl
from jax.experimental.pallas import tpu as pltpu
from jax.experimental.pallas import tpu_sc as plsc

F32 = jnp.float32
BF16 = jnp.bfloat16
I32 = jnp.int32

A_HEADS = 8
HEAD_DIM = 64
ATT_W = A_HEADS * HEAD_DIM
IDX_HEADS = 8
IDX_DIM = 64
IDX_SCALE = (IDX_HEADS * IDX_DIM) ** -0.5
TOPK_MAX = 256
N_BUCKETS = 32
MAX_DISTANCE = 128
N_EXPERTS = 32
TOP_K = 4
SWIGLU_LIMIT = 7.0
SWIGLU_ALPHA = 1.702
LN_EPS = 1e-5
QK_SCALE = HEAD_DIM ** -0.5

LANES = 128
SUBLANES = 8
HALF = LANES // 2
N_PAIRS = A_HEADS // 2
VMEM_LIMIT = 56 * 1024 * 1024

DSA_BLK = 256
SB_T = 256
SB_PAIRS = 4
PROJ_TM = 1024
MERGE_TM = 512
MOE_BLK = 512
COMB_TM = 512
SC_GATHER_BYTES = 128 * 1024
REDUCE_CHAINS = 8
BISECT_CAP = 24
BISECT_FREE = 20
NEG = -1e30

COL_QA, COL_KA, COL_QI, COL_QB, COL_KB, COL_VB = (g * ATT_W for g in range(6))
COL_KK = 6 * ATT_W
COL_WI = COL_KK + LANES
ATT_COLS = COL_WI + LANES
PROJ_TN = ATT_COLS // 2

NT_DIMS = (((1,), (1,)), ((), ()))


def _params(sem, vmem=VMEM_LIMIT):
    return pltpu.CompilerParams(dimension_semantics=sem, vmem_limit_bytes=vmem)


def _proj_kernel(x_ref, w_ref, wt_ref, o_ref, ot_ref, xb_ref, *, tt):
    @pl.when(pl.program_id(1) == 0)
    def _row_tile_start():
        xb_ref[...] = x_ref[...].astype(BF16)
        for r in range(ot_ref.shape[0]):
            ot_ref[r] = lax.dot_general(wt_ref[...], xb_ref[r * tt:(r + 1) * tt, :], NT_DIMS,
                                        preferred_element_type=F32).astype(ot_ref.dtype)

    o_ref[...] = jnp.dot(xb_ref[...], w_ref[...], preferred_element_type=F32).astype(o_ref.dtype)


def _projection(x, w, w_t, tm, tn, tt):
    M, K = x.shape
    N = w.shape[1]
    Nt = w_t.shape[0]
    return pl.pallas_call(
        functools.partial(_proj_kernel, tt=tt),
        grid=(M // tm, N // tn),
        in_specs=[pl.BlockSpec((tm, K), lambda i, j: (i, 0)),
                  pl.BlockSpec((K, tn), lambda i, j: (0, j)),
                  pl.BlockSpec((Nt, K), lambda i, j: (0, 0))],
        out_specs=[pl.BlockSpec((tm, tn), lambda i, j: (i, j)),
                   pl.BlockSpec((tm // tt, Nt, tt), lambda i, j: (i, 0, 0))],
        out_shape=[jax.ShapeDtypeStruct((M, N), BF16),
                   jax.ShapeDtypeStruct((M // tt, Nt, tt), BF16)],
        scratch_shapes=[pltpu.VMEM((tm, K), BF16)],
        compiler_params=_params(("arbitrary", "arbitrary")),
    )(x, w, w_t)


def _t5_bucket_np(n):
    n = np.maximum(n, 0)
    max_exact = N_BUCKETS // 2
    nf = np.maximum(n, 1).astype(np.float32)
    large = max_exact + (np.log(nf / max_exact) / math.log(MAX_DISTANCE / max_exact)
                         * (N_BUCKETS - max_exact)).astype(np.int32)
    large = np.minimum(large, N_BUCKETS - 1)
    return np.where(n < max_exact, n, large).astype(np.int32)


def _dsa_n_off(blk):
    return 2 + -(-MAX_DISTANCE // blk)


def _dsa_bucket_tiles(blk):
    n_off = _dsa_n_off(blk)
    j = np.arange(blk)[None, :, None]
    i = np.arange(blk)[None, None, :]
    o = np.arange(n_off)[:, None, None]
    return _t5_bucket_np(i - j + blk * (n_off - 1 - o))


def _dsa_kernel(bucket_ref, relb_ref, q_ref, k_ref, vt_ref, qi_ref, kk_ref, wi_ref, o_ref,
                sc_ref, bias_ref, m_ref, l_ref, acc_ref, st_ref, mx_ref,
                *, blk, n_chunks, n_sel, n_off):
    b = pl.program_id(0)
    i = pl.program_id(1)
    q0 = i * blk
    nck = i + 1
    groups = blk // SUBLANES

    @pl.when(jnp.logical_and(b == 0, i == 0))
    def _build_bias():
        def head_body(h, _):
            for o in range(n_off):
                for rb in range(blk // LANES):
                    for cb in range(blk // LANES):
                        rs = slice(rb * LANES, (rb + 1) * LANES)
                        cs = slice(cb * LANES, (cb + 1) * LANES)
                        bk = bucket_ref[o, rs, cs]

                        def bucket_body(n, acc):
                            return jnp.where(bk == n, relb_ref[n, h], acc)

                        bias_ref[h, o, rs, cs] = lax.fori_loop(
                            0, N_BUCKETS, bucket_body, jnp.zeros((LANES, LANES), F32))
            return 0

        lax.fori_loop(0, A_HEADS, head_body, 0)

    lane = lax.broadcasted_iota(I32, (blk, LANES), 1)
    lo_half = lane < HALF
    krow = lax.broadcasted_iota(I32, (blk, blk), 0)
    qpos = q0 + lax.broadcasted_iota(I32, (1, blk), 1)

    def pair_split(ref, scale):
        out = []
        for p in range(N_PAIRS):
            v = ref[0, :, p * LANES:(p + 1) * LANES].astype(F32)
            if scale != 1.0:
                v = v * scale
            out.append(jnp.where(lo_half, v, 0.0).astype(BF16))
            out.append(jnp.where(lo_half, 0.0, v).astype(BF16))
        return out

    wi_t = wi_ref[0].astype(F32).T
    wrow = [wi_t[h:h + 1, :] * IDX_SCALE for h in range(IDX_HEADS)]
    qi_m = pair_split(qi_ref, 1.0)

    def score_chunk(c, _):
        c0 = pl.multiple_of(c * blk, blk)
        kk = kk_ref[0, pl.ds(c0, blk), :]
        acc = jnp.zeros((blk, blk), F32)
        for h in range(IDX_HEADS):
            s = lax.dot_general(kk, qi_m[h], NT_DIMS, preferred_element_type=F32)
            acc = acc + wrow[h] * jnp.maximum(s, 0.0)
        sc_ref[c] = jnp.where(c0 + krow <= qpos, acc, -jnp.inf)
        return 0

    lax.fori_loop(0, nck, score_chunk, 0)

    kt = jnp.minimum(qpos + 1, n_sel).astype(F32)

    def fold(fn, init):
        def body(c, acc):
            return fn(acc, sc_ref[c])
        return lax.fori_loop(0, nck, body, init)

    def part(x, op):
        y = op(x.reshape(REDUCE_CHAINS, groups // REDUCE_CHAINS, SUBLANES, blk), axis=1)
        return op(y, axis=0)

    def fin(x, op):
        return op(x, axis=0, keepdims=True)

    zeros8 = jnp.zeros((SUBLANES, blk), F32)
    pinf8 = jnp.full((SUBLANES, blk), jnp.inf, F32)

    def count_ge(th):
        return fin(fold(lambda a, s: a + part(jnp.where(s >= th, 1.0, 0.0), jnp.sum), zeros8),
                   jnp.sum)

    mn, mx = fold(lambda a, s: (
        jnp.minimum(a[0], part(jnp.where(s == -jnp.inf, jnp.inf, s), jnp.min)),
        jnp.maximum(a[1], part(s, jnp.max))), (pinf8, -pinf8))
    rmin = fin(mn, jnp.min)
    rmax = fin(mx, jnp.max)

    def bis_cond(st):
        it, lo, hi, clo = st
        return jnp.logical_and(it < BISECT_CAP, jnp.max(jnp.abs(clo - kt)) > 0.0)

    def halve(lo, hi, clo):
        mid = 0.5 * lo + 0.5 * hi
        c = count_ge(mid)
        active = clo != kt
        up = jnp.logical_and(active, c >= kt)
        dn = jnp.logical_and(active, c < kt)
        return jnp.where(up, mid, lo), jnp.where(dn, mid, hi), jnp.where(up, c, clo)

    def bis_body(st):
        it, lo, hi, clo = st
        return (it + 2,) + halve(*halve(lo, hi, clo))

    start = (rmin, rmax + jnp.maximum(1.0, jnp.abs(rmax) * 2.0 ** -20), (qpos + 1).astype(F32))
    start = lax.fori_loop(0, BISECT_FREE, lambda _, st: halve(*st), start)
    _, lo, _, _ = lax.while_loop(bis_cond, bis_body, (jnp.int32(BISECT_FREE),) + start)

    def stats(lo_):
        a_ = fin(fold(lambda a, s: jnp.minimum(a, part(jnp.where(s >= lo_, s, jnp.inf), jnp.min)),
                      pinf8), jnp.min)
        cg, ct, nx = fold(
            lambda a, s: (a[0] + part(jnp.where(s > a_, 1.0, 0.0), jnp.sum),
                               a[1] + part(jnp.where(s == a_, 1.0, 0.0), jnp.sum),
                               jnp.minimum(a[2], part(jnp.where(s > a_, s, jnp.inf), jnp.min))),
            (zeros8, zeros8, pinf8))
        return a_, fin(cg, jnp.sum), fin(ct, jnp.sum), fin(nx, jnp.min)

    def fin_cond(st):
        return st[0]

    def fin_body(st):
        _, lo_, _, _ = st
        a_, cgt_, nt_, nxt_ = stats(lo_)
        bad = cgt_ >= kt
        return (jnp.max(jnp.where(bad, 1.0, 0.0)) > 0.0, jnp.where(bad, nxt_, a_), cgt_, nt_)

    _, a, cgt, nties = lax.while_loop(fin_cond, fin_body, (jnp.bool_(True), lo, kt, kt))
    need = kt - cgt
    excess = jnp.max(jnp.where(nties > need, 1.0, 0.0)) > 0.0

    def mask_plain():
        def body(c, _):
            sc_ref[c] = jnp.where(sc_ref[c] >= a, 0.0, NEG)
            return 0
        lax.fori_loop(0, nck, body, 0)

    def mask_ties():
        upto = (krow >= lax.broadcasted_iota(I32, (blk, blk), 1)).astype(BF16)

        def body(c, seen):
            s = sc_ref[c]
            tie = s == a
            rank = jnp.dot(upto, jnp.where(tie, 1.0, 0.0).astype(BF16),
                           preferred_element_type=F32) + seen
            sel = jnp.logical_or(s > a, jnp.logical_and(tie, rank <= need))
            sc_ref[c] = jnp.where(sel, 0.0, NEG)
            return rank[blk - 1:blk, :]

        lax.fori_loop(0, nck, body, jnp.zeros((1, blk), F32))

    lax.cond(excess, mask_ties, mask_plain)

    m_ref[...] = jnp.full(m_ref.shape, NEG, F32)
    l_ref[...] = jnp.zeros(l_ref.shape, F32)
    acc_ref[...] = jnp.zeros(acc_ref.shape, F32)
    q_m = pair_split(q_ref, QK_SCALE)

    def stage_logits(c, slot):
        c = jnp.minimum(c, n_chunks - 1)
        c0 = pl.multiple_of(c * blk, blk)
        madd = sc_ref[c]
        o_idx = jnp.clip(c - i + (n_off - 1), 0, n_off - 1)
        for p in range(N_PAIRS):
            k2 = k_ref[0, pl.ds(c0, blk), p * LANES:(p + 1) * LANES]
            for hh in range(2):
                h = 2 * p + hh
                s = lax.dot_general(k2, q_m[h], NT_DIMS, preferred_element_type=F32)
                s = s + bias_ref[h, o_idx] + madd
                st_ref[slot, h] = s
                mx_ref[slot, h] = fin(part(s, jnp.max), jnp.max)

    def stage_values(c, slot):
        for p in range(N_PAIRS):
            vt2 = vt_ref[0, c, p * LANES:(p + 1) * LANES, :]
            for hh in range(2):
                h = 2 * p + hh
                m_old = m_ref[h]
                m_new = jnp.maximum(m_old, mx_ref[slot, h])
                alpha = jnp.exp(m_old - m_new)
                pexp = jnp.exp(st_ref[slot, h] - m_new)
                l_ref[h] = alpha * l_ref[h] + fin(part(pexp, jnp.sum), jnp.sum)
                acc_ref[h] = alpha * acc_ref[h] + jnp.dot(vt2, pexp.astype(BF16),
                                                          preferred_element_type=F32)
                m_ref[h] = m_new

    stage_logits(0, 0)

    def att_pair(pp, _):
        c = 2 * pp
        stage_logits(c + 1, 1)
        stage_values(c, 0)
        stage_logits(c + 2, 0)
        stage_values(c + 1, 1)
        return 0

    lax.fori_loop(0, nck // 2, att_pair, 0)

    @pl.when(lax.rem(nck, 2) == 1)
    def _last_chunk():
        stage_values(nck - 1, 0)

    lo_rows = lax.broadcasted_iota(I32, (LANES, blk), 0) < HALF
    for p in range(N_PAIRS):
        oa = acc_ref[2 * p] / l_ref[2 * p]
        ob = acc_ref[2 * p + 1] / l_ref[2 * p + 1]
        o_ref[0, :, p * LANES:(p + 1) * LANES] = jnp.where(lo_rows, oa, ob).T.astype(o_ref.dtype)


def _dsa(proj3, vt4, rel_bias):
    B, S, _ = proj3.shape
    blk = DSA_BLK
    n_off = _dsa_n_off(blk)
    n_sel = min(TOPK_MAX, S // 4)
    bucket = jnp.asarray(_dsa_bucket_tiles(blk))
    n_chunks = S // blk
    assert S % blk == 0
    kern = functools.partial(_dsa_kernel, blk=blk, n_chunks=n_chunks, n_sel=n_sel, n_off=n_off)
    return pl.pallas_call(
        kern,
        grid=(B, S // blk),
        in_specs=[
            pl.BlockSpec((n_off, blk, blk), lambda b, i: (0, 0, 0)),
            pl.BlockSpec(memory_space=pltpu.SMEM),
            pl.BlockSpec((1, blk, ATT_W), lambda b, i: (b, i, COL_QA // ATT_W)),
            pl.BlockSpec((1, S, ATT_W), lambda b, i: (b, 0, COL_KA // ATT_W)),
            pl.BlockSpec((1, S // blk, ATT_W, blk), lambda b, i: (b, 0, 0, 0)),
            pl.BlockSpec((1, blk, ATT_W), lambda b, i: (b, i, COL_QI // ATT_W)),
            pl.BlockSpec((1, S, LANES), lambda b, i: (b, 0, COL_KK // LANES)),
            pl.BlockSpec((1, blk, LANES), lambda b, i: (b, i, COL_WI // LANES)),
        ],
        out_specs=pl.BlockSpec((1, blk, ATT_W), lambda b, i: (b, i, 0)),
        out_shape=jax.ShapeDtypeStruct((B, S, ATT_W), BF16),
        scratch_shapes=[
            pltpu.VMEM((S // blk, blk, blk), F32),
            pltpu.VMEM((A_HEADS, n_off, blk, blk), F32),
            pltpu.VMEM((A_HEADS, 1, blk), F32),
            pltpu.VMEM((A_HEADS, 1, blk), F32),
            pltpu.VMEM((A_HEADS, LANES, blk), F32),
            pltpu.VMEM((2, A_HEADS, blk, blk), F32),
            pltpu.VMEM((2, A_HEADS, 1, blk), F32),
        ],
        compiler_params=_params(("arbitrary", "arbitrary")),
    )(bucket, rel_bias, proj3, proj3, vt4, proj3, proj3, proj3)


def _sb_kernel(q_ref, k_ref, v_ref, o_ref, hl_ref, z_ref, *, t):
    i = pl.program_id(2)
    n = i + 1
    lane = lax.broadcasted_iota(I32, (t, LANES), 1)
    lo_half = lane < HALF
    q_m = []
    for pr in range(SB_PAIRS):
        q2 = q_ref[0, :, pr * LANES:(pr + 1) * LANES].astype(F32) * QK_SCALE
        q_m += [jnp.where(lo_half, q2, 0.0).astype(BF16), jnp.where(lo_half, 0.0, q2).astype(BF16)]
    heads = 2 * SB_PAIRS
    r = lax.broadcasted_iota(I32, (t, t), 0)
    cidx = lax.broadcasted_iota(I32, (t, t), 1)
    neg_from = jnp.where(r >= cidx, -1.0, 0.0).astype(BF16)
    neg_from2 = jnp.concatenate([neg_from, neg_from], axis=0)
    diff = cidx - r

    def stage_terms(step, slot, diagonal):
        c0 = pl.multiple_of(jnp.maximum(i - step, 0) * t, t)
        if diagonal:
            keep = diff < 0
        for hh in range(heads):
            pr = hh // 2
            k2 = k_ref[0, pl.ds(c0, t), pr * LANES:(pr + 1) * LANES]
            z = lax.dot_general(q_m[hh], k2, NT_DIMS, preferred_element_type=F32)
            sp = jnp.maximum(z, 0.0) + jnp.log(1.0 + jnp.exp(-jnp.abs(z)))
            if diagonal:
                sp = jnp.where(keep, sp, 0.0)
                z = jnp.where(keep, z, NEG)
            hi = sp.astype(BF16)
            hl_ref[slot, hh, :, :t] = hi
            hl_ref[slot, hh, :, t:] = (sp - hi.astype(F32)).astype(BF16)
            z_ref[slot, hh] = z

    def stage_apply(step, slot, carry):
        c0 = pl.multiple_of((i - step) * t, t)
        cum_all = jnp.dot(hl_ref[slot].reshape(heads * t, 2 * t), neg_from2, preferred_element_type=F32)
        out = []
        for hh in range(heads):
            pr = hh // 2
            v2 = v_ref[0, pl.ds(c0, t), pr * LANES:(pr + 1) * LANES]
            car, acc = carry[hh]
            cum = cum_all[hh * t:(hh + 1) * t]
            w = jnp.exp(z_ref[slot, hh] + cum + car)
            acc = acc + jnp.dot(w.astype(BF16), v2, preferred_element_type=F32)
            out.append((car + cum[:, 0:1], acc))
        return tuple(out)

    z1 = jnp.zeros((t, 1), F32)
    za = jnp.zeros((t, LANES), F32)
    stage_terms(0, 0, True)

    def pair_body(pp, carry):
        step = 2 * pp
        stage_terms(step + 1, 1, False)
        carry = stage_apply(step, 0, carry)
        stage_terms(step + 2, 0, False)
        return stage_apply(step + 1, 1, carry)

    carry = lax.fori_loop(0, n // 2, pair_body, ((z1, za),) * heads)
    carry = lax.cond(lax.rem(n, 2) == 1, lambda c: stage_apply(n - 1, 0, c), lambda c: c, carry)
    for pr in range(SB_PAIRS):
        o_ref[0, :, pr * LANES:(pr + 1) * LANES] = jnp.where(
            lo_half, carry[2 * pr][1], carry[2 * pr + 1][1]).astype(o_ref.dtype)


def _stick_breaking(proj3):
    B, S, _ = proj3.shape
    t = SB_T
    w = SB_PAIRS * LANES
    qb, kb, vb = COL_QB // w, COL_KB // w, COL_VB // w
    return pl.pallas_call(
        functools.partial(_sb_kernel, t=t),
        grid=(B, N_PAIRS // SB_PAIRS, S // t),
        in_specs=[
            pl.BlockSpec((1, t, w), lambda b, p, i: (b, i, qb + p)),
            pl.BlockSpec((1, S, w), lambda b, p, i: (b, 0, kb + p)),
            pl.BlockSpec((1, S, w), lambda b, p, i: (b, 0, vb + p)),
        ],
        out_specs=pl.BlockSpec((1, t, w), lambda b, p, i: (b, i, p)),
        out_shape=jax.ShapeDtypeStruct((B, S, ATT_W), BF16),
        scratch_shapes=[
            pltpu.VMEM((2, 2 * SB_PAIRS, t, 2 * t), BF16),
            pltpu.VMEM((2, 2 * SB_PAIRS, t, t), F32),
        ],
        compiler_params=_params(("arbitrary", "arbitrary", "arbitrary")),
    )(proj3, proj3, proj3)


def _layer_norm(r, g, b):
    mu = jnp.mean(r, axis=-1, keepdims=True)
    d = r - mu
    var = jnp.mean(d * d, axis=-1, keepdims=True)
    return d * lax.rsqrt(var + LN_EPS) * g + b


def _split_bf16(v):
    hi = v.astype(BF16)
    return hi, (v - hi.astype(F32)).astype(BF16)


def _pack_bf16_pairs(v):
    half = v.shape[1] // 2
    bits = pltpu.bitcast(v.astype(BF16).astype(F32), jnp.uint32)
    return bits[:, :half] | (bits[:, half:] >> 16)


def _unpack_bf16_pairs(w):
    return jnp.concatenate([pltpu.bitcast(w & jnp.uint32(0xFFFF0000), F32),
                            pltpu.bitcast(w << 16, F32)], axis=1)


def _merge_kernel(x_ref, ya_ref, yb_ref, wg_ref, wa_ref, wb_ref, wo_ref, g_ref, b_ref,
                  wr_ref, br_ref, h_ref, hp_ref, e_ref, p_ref, *, alpha, d):
    pa = jnp.dot(ya_ref[...], wa_ref[...], preferred_element_type=F32)
    pb = jnp.dot(yb_ref[...], wb_ref[...], preferred_element_type=F32)
    gates = jnp.dot(x_ref[...].astype(BF16), wg_ref[...], preferred_element_type=F32)
    merged = jax.nn.sigmoid(gates[:, :d]) * pa + jax.nn.sigmoid(gates[:, d:]) * pb
    m = jnp.dot(merged.astype(BF16), wo_ref[...], preferred_element_type=F32)
    h = _layer_norm(alpha * x_ref[...] + m, g_ref[...], b_ref[...])
    h_ref[...] = h
    hp_ref[...] = _pack_bf16_pairs(h)

    h_hi, h_lo = _split_bf16(h)
    w_hi, w_lo = _split_bf16(wr_ref[...])
    logit = (lax.dot_general(w_hi, h_hi, NT_DIMS, preferred_element_type=F32)
             + lax.dot_general(w_hi, h_lo, NT_DIMS, preferred_element_type=F32)
             + lax.dot_general(w_lo, h_hi, NT_DIMS, preferred_element_type=F32)) + br_ref[...]
    eid = lax.broadcasted_iota(I32, logit.shape, 0)
    vals, ids = [], []
    for _ in range(TOP_K):
        mx = jnp.max(logit, axis=0, keepdims=True)
        am = jnp.min(jnp.where(logit == mx, eid, N_EXPERTS), axis=0, keepdims=True)
        vals.append(mx)
        ids.append(am)
        logit = jnp.where(eid == am, -jnp.inf, logit)
    ex = [jnp.exp(v - vals[0]) for v in vals]
    den = ex[0] + ex[1] + ex[2] + ex[3]
    for k in range(TOP_K):
        e_ref[k:k + 1, :] = ids[k]
        p_ref[k:k + 1, :] = ex[k] / den


def _merge(x2, ya, yb, wg, wa, wb, wo, g, b, wr_t, br, alpha):
    T, D = x2.shape
    tm = MERGE_TM
    row = lambda i: (i, 0)
    fixed = lambda i: (0, 0)
    return pl.pallas_call(
        functools.partial(_merge_kernel, alpha=alpha, d=D),
        grid=(T // tm,),
        in_specs=[
            pl.BlockSpec((tm, D), row),
            pl.BlockSpec((tm, ATT_W), row),
            pl.BlockSpec((tm, ATT_W), row),
            pl.BlockSpec((D, 2 * D), fixed),
            pl.BlockSpec((ATT_W, D), fixed),
            pl.BlockSpec((ATT_W, D), fixed),
            pl.BlockSpec((D, D), fixed),
            pl.BlockSpec((1, D), fixed),
            pl.BlockSpec((1, D), fixed),
            pl.BlockSpec((N_EXPERTS, D), fixed),
            pl.BlockSpec((N_EXPERTS, 1), fixed),
        ],
        out_specs=[
            pl.BlockSpec((tm, D), row),
            pl.BlockSpec((tm, D // 2), row),
            pl.BlockSpec((TOP_K, tm), lambda i: (0, i)),
            pl.BlockSpec((TOP_K, tm), lambda i: (0, i)),
        ],
        out_shape=[
            jax.ShapeDtypeStruct((T, D), F32),
            jax.ShapeDtypeStruct((T, D // 2), jnp.uint32),
            jax.ShapeDtypeStruct((TOP_K, T), I32),
            jax.ShapeDtypeStruct((TOP_K, T), F32),
        ],
        compiler_params=_params(("arbitrary",)),
    )(x2, ya, yb, wg, wa, wb, wo, g, b, wr_t, br)


def _sc_gather_rows(table, idx):
    n = idx.shape[0]
    d = table.shape[1]
    info = plsc.get_sparse_core_info()
    n_cores, n_sub = info.num_cores, info.num_subcores
    per_w = n // (n_cores * n_sub)
    c = SC_GATHER_BYTES // (d * table.dtype.itemsize)
    n_g = per_w // c
    assert n == per_w * n_cores * n_sub and per_w == n_g * c and n_g % 2 == 0 and n_g >= 2
    mesh = plsc.VectorSubcoreMesh(core_axis_name="c", subcore_axis_name="s")

    @functools.partial(
        pl.kernel, mesh=mesh, out_type=jax.ShapeDtypeStruct((n, d), table.dtype),
        scratch_types=[pltpu.VMEM((per_w,), I32), pltpu.VMEM((2, c, d), table.dtype),
                       pltpu.SemaphoreType.DMA((2,)), pltpu.SemaphoreType.DMA((2,))])
    def gather_kernel(table_hbm, idx_hbm, out_hbm, idx_v, rows_v, gsem, wsem):
        base = (lax.axis_index("s") * n_cores + lax.axis_index("c")) * per_w
        pltpu.sync_copy(idx_hbm.at[pl.ds(base, per_w)], idx_v)

        def gather(g, b):
            return pltpu.make_async_copy(table_hbm.at[idx_v.at[pl.ds(g * c, c)]], rows_v.at[b],
                                         gsem.at[b])

        def write(g, b):
            return pltpu.make_async_copy(rows_v.at[b], out_hbm.at[pl.ds(base + g * c, c)], wsem.at[b])

        gather(0, 0).start()

        @pl.loop(0, n_g, step=2)
        def _ring(g0):
            for b in range(2):
                g = g0 + b

                @pl.when(g + 1 < n_g)
                def _next():
                    @pl.when(g >= 1)
                    def _buffer_free():
                        write(g - 1, 1 - b).wait()
                    gather(g + 1, 1 - b).start()

                gather(g, b).wait()
                write(g, b).start()

        write(n_g - 2, 0).wait()
        write(n_g - 1, 1).wait()

    return gather_kernel(table, idx)


def _moe_kernel(blk_e_ref, nused_ref, x_ref, wgu_ref, bgu_ref, wdn_ref, bdn_ref, o_ref,
                wgu_s, wdn_s, *, f):
    i = pl.program_id(0)
    nused = nused_ref[0]

    @pl.when(i < nused)
    def _compute():
        changed = jnp.logical_or(i == 0, blk_e_ref[i] != blk_e_ref[jnp.maximum(i - 1, 0)])

        @pl.when(changed)
        def _cast_weights():
            wgu_s[...] = wgu_ref[0].astype(BF16)
            wdn_s[...] = wdn_ref[0].astype(BF16)

        x = _unpack_bf16_pairs(x_ref[...]).astype(BF16)
        hgu = jnp.dot(x, wgu_s[...], preferred_element_type=F32) + bgu_ref[0]
        a = jnp.minimum(hgu[:, :f], SWIGLU_LIMIT)
        u = jnp.clip(hgu[:, f:], -SWIGLU_LIMIT, SWIGLU_LIMIT)
        glu = a * jax.nn.sigmoid(a * SWIGLU_ALPHA)
        y = jnp.dot(((u + 1.0) * glu).astype(BF16), wdn_s[...], preferred_element_type=F32) + bdn_ref[0]
        o_ref[...] = _pack_bf16_pairs(y)

    @pl.when(i >= nused)
    def _unused_block():
        o_ref[...] = jnp.zeros(o_ref.shape, o_ref.dtype)


def _moe_ffn(xs, blk_e, nused, w_gu, b_gu, w_dn, b_dn):
    P = xs.shape[0]
    E, D, F2 = w_gu.shape
    f = F2 // 2
    blk = MOE_BLK
    nb = P // blk
    used_block = lambda i, be, nu: (jnp.minimum(i, nu[0] - 1), 0)
    grid_spec = pltpu.PrefetchScalarGridSpec(
        num_scalar_prefetch=2,
        grid=(nb,),
        in_specs=[
            pl.BlockSpec((blk, D // 2), used_block),
            pl.BlockSpec((1, D, F2), lambda i, be, nu: (be[i], 0, 0)),
            pl.BlockSpec((1, 1, F2), lambda i, be, nu: (be[i], 0, 0)),
            pl.BlockSpec((1, f, D), lambda i, be, nu: (be[i], 0, 0)),
            pl.BlockSpec((1, 1, D), lambda i, be, nu: (be[i], 0, 0)),
        ],
        out_specs=pl.BlockSpec((blk, D // 2), lambda i, be, nu: (i, 0)),
        scratch_shapes=[
            pltpu.VMEM((D, F2), BF16),
            pltpu.VMEM((f, D), BF16),
        ],
    )
    return pl.pallas_call(
        functools.partial(_moe_kernel, f=f),
        grid_spec=grid_spec,
        out_shape=jax.ShapeDtypeStruct((P, D // 2), jnp.uint32),
        compiler_params=_params(("arbitrary",)),
    )(blk_e, nused, xs, w_gu, b_gu.reshape(E, 1, F2), w_dn, b_dn.reshape(E, 1, D))


def _comb_kernel(h_ref, p_ref, y_ref, g_ref, b_ref, o_ref, *, alpha):
    gate = p_ref[...]
    y = [_unpack_bf16_pairs(y_ref[k]) for k in range(TOP_K)]
    fsum = (y[0] * gate[:, 0:1] + y[1] * gate[:, 1:2]) + (y[2] * gate[:, 2:3] + y[3] * gate[:, 3:4])
    o_ref[...] = _layer_norm(alpha * h_ref[...] + fsum, g_ref[...], b_ref[...])


def _combine(h, y4, top_p, g, b, alpha):
    T, D = h.shape
    tm = COMB_TM
    return pl.pallas_call(
        functools.partial(_comb_kernel, alpha=alpha),
        grid=(T // tm,),
        in_specs=[
            pl.BlockSpec((tm, D), lambda i: (i, 0)),
            pl.BlockSpec((tm, TOP_K), lambda i: (i, 0)),
            pl.BlockSpec((TOP_K, tm, D // 2), lambda i: (0, i, 0)),
            pl.BlockSpec((1, D), lambda i: (0, 0)),
            pl.BlockSpec((1, D), lambda i: (0, 0)),
        ],
        out_specs=pl.BlockSpec((tm, D), lambda i: (i, 0)),
        out_shape=jax.ShapeDtypeStruct((T, D), F32),
        compiler_params=_params(("arbitrary",)),
    )(h, top_p.T, y4, g, b)


def _route(top_e, blk):
    K, T = top_e.shape
    N = K * T
    flat_e = top_e.reshape(N)
    experts = jnp.arange(N_EXPERTS, dtype=I32)
    order = jnp.argsort(flat_e, stable=True).astype(I32)
    inv = jnp.argsort(order).astype(I32)
    onehot = flat_e[:, None] == experts[None, :]
    counts = jnp.sum(onehot, axis=0, dtype=I32)
    padded = (counts + blk - 1) // blk * blk
    pends = jnp.cumsum(padded)
    offs = jnp.cumsum(counts) - counts
    shift = (pends - padded) - offs
    pos = inv + jnp.sum(jnp.where(onehot, shift[None, :], 0), axis=1, dtype=I32)
    P = N + N_EXPERTS * blk
    nb = P // blk
    blk_start = jnp.arange(nb, dtype=I32) * blk
    blk_e = jnp.minimum(jnp.sum(pends[None, :] <= blk_start[:, None], axis=1, dtype=I32), N_EXPERTS - 1)
    j = (blk_start - shift[blk_e])[:, None] + jnp.arange(blk, dtype=I32)[None, :]
    valid = j < (offs + counts)[blk_e][:, None]
    src = order[jnp.clip(j, 0, N - 1)]
    row_tok = jnp.where(valid, src % T, j % T).reshape(P)
    nused = (pends[-1:] // blk).astype(I32)
    return blk_e, nused, row_tok, pos.reshape(K, T)


def _projection_weights(w_in_l):
    sizes = (ATT_W, ATT_W, ATT_W, IDX_HEADS * IDX_DIM, IDX_DIM, IDX_HEADS, ATT_W, ATT_W, ATT_W)
    offs = np.concatenate([[0], np.cumsum(sizes)])
    qa, ka, va, qi, ki, wi, qb, kb, vb = (w_in_l[:, offs[n]:offs[n + 1]] for n in range(9))
    pad_wi = jnp.zeros((w_in_l.shape[0], LANES - IDX_HEADS), w_in_l.dtype)
    w_att = jnp.concatenate([qa, ka, qi, qb, kb, vb, ki, ki, wi, pad_wi], axis=1).astype(BF16)
    w_va_t = va.T.astype(BF16)
    w_gate = w_in_l[:, offs[9]:].astype(BF16)
    return w_att, w_va_t, w_gate


def kernel(x, w_in, w_branch_a, w_branch_b, w_out, rel_bias, ln1_g, ln1_b, w_router, b_router,
           w_gate_up, b_gate_up, w_down, b_down, ln2_g, ln2_b):
    B, S, D = x.shape
    depth = w_in.shape[0]
    alpha = (2 * depth) ** 0.25
    T = B * S
    h = x.reshape(T, D)
    for l in range(depth):
        w_att, w_va_t, w_gate = _projection_weights(w_in[l])
        proj, vt = _projection(h, w_att, w_va_t, min(T, PROJ_TM), PROJ_TN, DSA_BLK)
        proj = proj.reshape(B, S, ATT_COLS)
        vt = vt.reshape(B, S // DSA_BLK, ATT_W, DSA_BLK)
        ya = _dsa(proj, vt, rel_bias).reshape(T, ATT_W)
        yb = _stick_breaking(proj).reshape(T, ATT_W)
        h1, h1_packed, top_e, top_p = _merge(
            h, ya, yb, w_gate, w_branch_a[l].astype(BF16), w_branch_b[l].astype(BF16),
            w_out[l].astype(BF16), ln1_g[l].reshape(1, D), ln1_b[l].reshape(1, D),
            w_router[l].T, b_router[l].reshape(N_EXPERTS, 1), alpha)
        blk_e, nused, row_tok, pos = _route(top_e, MOE_BLK)
        xs = _sc_gather_rows(h1_packed, row_tok)
        ys = _moe_ffn(xs, blk_e, nused, w_gate_up[l], b_gate_up[l], w_down[l], b_down[l])
        y4 = _sc_gather_rows(ys, pos.reshape(TOP_K * T)).reshape(TOP_K, T, D // 2)
        h = _combine(h1, y4, top_p, ln2_g[l].reshape(1, D), ln2_b[l].reshape(1, D), alpha)
    return h.reshape(B, S, D)
```

```python
import functools
import math

import numpy as np
import jax
import jax.numpy as jnp
from jax import lax
from jax.experimental import pallas as pl
from jax.experimental.pallas import tpu as pltpu
from jax.experimental.pallas import tpu_sc as plsc

F32 = jnp.float32
BF16 = jnp.bfloat16
I32 = jnp.int32

A_HEADS = 8
HEAD_DIM = 64
ATT_W = A_HEADS * HEAD_DIM
IDX_HEADS = 8
IDX_DIM = 64
IDX_SCALE = (IDX_HEADS * IDX_DIM) ** -0.5
TOPK_MAX = 256
N_BUCKETS = 32
MAX_DISTANCE = 128
N_EXPERTS = 32
TOP_K = 4
SWIGLU_LIMIT = 7.0
SWIGLU_ALPHA = 1.702
LN_EPS = 1e-5
QK_SCALE = HEAD_DIM ** -0.5

LANES = 128
SUBLANES = 8
HALF = LANES // 2
N_PAIRS = A_HEADS // 2
VMEM_LIMIT = 56 * 1024 * 1024

DSA_BLK = 256
SB_T = 256
SB_PAIRS = 4
PROJ_TM = 1024
MERGE_TM = 512
MOE_BLK = 512
COMB_TM = 512
SC_GATHER_BYTES = 128 * 1024
REDUCE_CHAINS = 8
BISECT_CAP = 24
BISECT_FREE = 20
NEG = -1e30

COL_QA, COL_KA, COL_QI, COL_QB, COL_KB, COL_VB = (g * ATT_W for g in range(6))
COL_KK = 6 * ATT_W
COL_WI = COL_KK + LANES
ATT_COLS = COL_WI + LANES
PROJ_TN = ATT_COLS // 2

NT_DIMS = (((1,), (1,)), ((), ()))


def _params(sem, vmem=VMEM_LIMIT):
    return pltpu.CompilerParams(dimension_semantics=sem, vmem_limit_bytes=vmem)


def _proj_kernel(x_ref, w_ref, wt_ref, o_ref, ot_ref, xb_ref, *, tt):
    @pl.when(pl.program_id(1) == 0)
    def _row_tile_start():
        xb_ref[...] = x_ref[...].astype(BF16)
        for r in range(ot_ref.shape[0]):
            ot_ref[r] = lax.dot_general(wt_ref[...], xb_ref[r * tt:(r + 1) * tt, :], NT_DIMS,
                                        preferred_element_type=F32).astype(ot_ref.dtype)

    o_ref[...] = jnp.dot(xb_ref[...], w_ref[...], preferred_element_type=F32).astype(o_ref.dtype)


def _projection(x, w, w_t, tm, tn, tt):
    M, K = x.shape
    N = w.shape[1]
    Nt = w_t.shape[0]
    return pl.pallas_call(
        functools.partial(_proj_kernel, tt=tt),
        grid=(M // tm, N // tn),
        in_specs=[pl.BlockSpec((tm, K), lambda i, j: (i, 0)),
                  pl.BlockSpec((K, tn), lambda i, j: (0, j)),
                  pl.BlockSpec((Nt, K), lambda i, j: (0, 0))],
        out_specs=[pl.BlockSpec((tm, tn), lambda i, j: (i, j)),
                   pl.BlockSpec((tm // tt, Nt, tt), lambda i, j: (i, 0, 0))],
        out_shape=[jax.ShapeDtypeStruct((M, N), BF16),
                   jax.ShapeDtypeStruct((M // tt, Nt, tt), BF16)],
        scratch_shapes=[pltpu.VMEM((tm, K), BF16)],
        compiler_params=_params(("arbitrary", "arbitrary")),
    )(x, w, w_t)


def _t5_bucket_np(n):
    n = np.maximum(n, 0)
    max_exact = N_BUCKETS // 2
    nf = np.maximum(n, 1).astype(np.float32)
    large = max_exact + (np.log(nf / max_exact) / math.log(MAX_DISTANCE / max_exact)
                         * (N_BUCKETS - max_exact)).astype(np.int32)
    large = np.minimum(large, N_BUCKETS - 1)
    return np.where(n < max_exact, n, large).astype(np.int32)


def _dsa_n_off(blk):
    return 2 + -(-MAX_DISTANCE // blk)


def _dsa_bucket_tiles(blk):
    n_off = _dsa_n_off(blk)
    j = np.arange(blk)[None, :, None]
    i = np.arange(blk)[None, None, :]
    o = np.arange(n_off)[:, None, None]
    return _t5_bucket_np(i - j + blk * (n_off - 1 - o))


def _dsa_kernel(bucket_ref, relb_ref, q_ref, k_ref, vt_ref, qi_ref, kk_ref, wi_ref, o_ref,
                sc_ref, bias_ref, m_ref, l_ref, acc_ref, st_ref, mx_ref, red_ref,
                *, blk, n_chunks, n_sel, n_off):
    b = pl.program_id(0)
    i = pl.program_id(1)
    q0 = i * blk
    nck = i + 1
    groups = blk // SUBLANES

    @pl.when(jnp.logical_and(b == 0, i == 0))
    def _build_bias():
        def head_body(h, _):
            for o in range(n_off):
                for rb in range(blk // LANES):
                    for cb in range(blk // LANES):
                        rs = slice(rb * LANES, (rb + 1) * LANES)
                        cs = slice(cb * LANES, (cb + 1) * LANES)
                        bk = bucket_ref[o, rs, cs]

                        def bucket_body(n, acc):
                            return jnp.where(bk == n, relb_ref[n, h], acc)

                        bias_ref[h, o, rs, cs] = lax.fori_loop(
                            0, N_BUCKETS, bucket_body, jnp.zeros((LANES, LANES), F32))
            return 0

        lax.fori_loop(0, A_HEADS, head_body, 0)

    lane = lax.broadcasted_iota(I32, (blk, LANES), 1)
    lo_half = lane < HALF
    krow = lax.broadcasted_iota(I32, (blk, blk), 0)
    qpos = q0 + lax.broadcasted_iota(I32, (1, blk), 1)

    def pair_split(ref, scale):
        out = []
        for p in range(N_PAIRS):
            v = ref[0, :, p * LANES:(p + 1) * LANES].astype(F32)
            if scale != 1.0:
                v = v * scale
            out.append(jnp.where(lo_half, v, 0.0).astype(BF16))
            out.append(jnp.where(lo_half, 0.0, v).astype(BF16))
        return out

    wi_t = wi_ref[0].astype(F32).T
    wrow = [wi_t[h:h + 1, :] * IDX_SCALE for h in range(IDX_HEADS)]
    qi_m = pair_split(qi_ref, 1.0)

    def score_chunk(c, _):
        c0 = pl.multiple_of(c * blk, blk)
        kk = kk_ref[0, pl.ds(c0, blk), :]
        acc = jnp.zeros((blk, blk), F32)
        for h in range(IDX_HEADS):
            s = lax.dot_general(kk, qi_m[h], NT_DIMS, preferred_element_type=F32)
            acc = acc + wrow[h] * jnp.maximum(s, 0.0)
        sc_ref[c] = jnp.where(c0 + krow <= qpos, acc, -jnp.inf)
        return 0

    lax.fori_loop(0, nck, score_chunk, 0)

    kt = jnp.minimum(qpos + 1, n_sel).astype(F32)

    def fold(fn, init):
        def body(c, acc):
            return fn(acc, sc_ref[c])
        return lax.fori_loop(0, nck, body, init)

    def part(x, op):
        y = op(x.reshape(REDUCE_CHAINS, groups // REDUCE_CHAINS, SUBLANES, blk), axis=1)
        return op(y, axis=0)

    def fin(x, op):
        return op(x, axis=0, keepdims=True)

    zeros8 = jnp.zeros((SUBLANES, blk), F32)
    pinf8 = jnp.full((SUBLANES, blk), jnp.inf, F32)

    def count_ge(th):
        red_ref[...] = fold(lambda a, s: a + part(jnp.where(s >= th, 1.0, 0.0), jnp.sum), zeros8)
        r = [red_ref[j:j + 1, :] for j in range(SUBLANES)]
        return ((r[0] + r[1]) + (r[2] + r[3])) + ((r[4] + r[5]) + (r[6] + r[7]))

    mn, mx = fold(lambda a, s: (
        jnp.minimum(a[0], part(jnp.where(s == -jnp.inf, jnp.inf, s), jnp.min)),
        jnp.maximum(a[1], part(s, jnp.max))), (pinf8, -pinf8))
    rmin = fin(mn, jnp.min)
    rmax = fin(mx, jnp.max)

    def bis_cond(st):
        it, lo, hi, clo = st
        return jnp.logical_and(it < BISECT_CAP, jnp.max(jnp.abs(clo - kt)) > 0.0)

    def halve(lo, hi, clo):
        mid = 0.5 * lo + 0.5 * hi
        c = count_ge(mid)
        active = clo != kt
        up = jnp.logical_and(active, c >= kt)
        dn = jnp.logical_and(active, c < kt)
        return jnp.where(up, mid, lo), jnp.where(dn, mid, hi), jnp.where(up, c, clo)

    def bis_body(st):
        it, lo, hi, clo = st
        return (it + 2,) + halve(*halve(lo, hi, clo))

    start = (rmin, rmax + jnp.maximum(1.0, jnp.abs(rmax) * 2.0 ** -20), (qpos + 1).astype(F32))
    start = lax.fori_loop(0, BISECT_FREE, lambda _, st: halve(*st), start)
    _, lo, _, _ = lax.while_loop(bis_cond, bis_body, (jnp.int32(BISECT_FREE),) + start)

    def stats(lo_):
        a_ = fin(fold(lambda a, s: jnp.minimum(a, part(jnp.where(s >= lo_, s, jnp.inf), jnp.min)),
                      pinf8), jnp.min)
        cg, ct, nx = fold(
            lambda a, s: (a[0] + part(jnp.where(s > a_, 1.0, 0.0), jnp.sum),
                               a[1] + part(jnp.where(s == a_, 1.0, 0.0), jnp.sum),
                               jnp.minimum(a[2], part(jnp.where(s > a_, s, jnp.inf), jnp.min))),
            (zeros8, zeros8, pinf8))
        return a_, fin(cg, jnp.sum), fin(ct, jnp.sum), fin(nx, jnp.min)

    def fin_cond(st):
        return st[0]

    def fin_body(st):
        _, lo_, _, _ = st
        a_, cgt_, nt_, nxt_ = stats(lo_)
        bad = cgt_ >= kt
        return (jnp.max(jnp.where(bad, 1.0, 0.0)) > 0.0, jnp.where(bad, nxt_, a_), cgt_, nt_)

    _, a, cgt, nties = lax.while_loop(fin_cond, fin_body, (jnp.bool_(True), lo, kt, kt))
    need = kt - cgt
    excess = jnp.max(jnp.where(nties > need, 1.0, 0.0)) > 0.0

    def mask_plain():
        def body(c, _):
            sc_ref[c] = jnp.where(sc_ref[c] >= a, 0.0, NEG)
            return 0
        lax.fori_loop(0, nck, body, 0)

    def mask_ties():
        upto = (krow >= lax.broadcasted_iota(I32, (blk, blk), 1)).astype(BF16)

        def body(c, seen):
            s = sc_ref[c]
            tie = s == a
            rank = jnp.dot(upto, jnp.where(tie, 1.0, 0.0).astype(BF16),
                           preferred_element_type=F32) + seen
            sel = jnp.logical_or(s > a, jnp.logical_and(tie, rank <= need))
            sc_ref[c] = jnp.where(sel, 0.0, NEG)
            return rank[blk - 1:blk, :]

        lax.fori_loop(0, nck, body, jnp.zeros((1, blk), F32))

    lax.cond(excess, mask_ties, mask_plain)

    m_ref[...] = jnp.full(m_ref.shape, NEG, F32)
    l_ref[...] = jnp.zeros(l_ref.shape, F32)
    acc_ref[...] = jnp.zeros(acc_ref.shape, F32)
    q_m = pair_split(q_ref, QK_SCALE)

    def stage_logits(c, slot):
        c = jnp.minimum(c, n_chunks - 1)
        c0 = pl.multiple_of(c * blk, blk)
        madd = sc_ref[c]
        o_idx = jnp.clip(c - i + (n_off - 1), 0, n_off - 1)
        for p in range(N_PAIRS):
            k2 = k_ref[0, pl.ds(c0, blk), p * LANES:(p + 1) * LANES]
            for hh in range(2):
                h = 2 * p + hh
                s = lax.dot_general(k2, q_m[h], NT_DIMS, preferred_element_type=F32)
                s = s + bias_ref[h, o_idx] + madd
                st_ref[slot, h] = s
                mx_ref[slot, h] = fin(part(s, jnp.max), jnp.max)

    def stage_values(c, slot):
        for p in range(N_PAIRS):
            vt2 = vt_ref[0, c, p * LANES:(p + 1) * LANES, :]
            for hh in range(2):
                h = 2 * p + hh
                m_old = m_ref[h]
                m_new = jnp.maximum(m_old, mx_ref[slot, h])
                alpha = jnp.exp(m_old - m_new)
                pexp = jnp.exp(st_ref[slot, h] - m_new)
                l_ref[h] = alpha * l_ref[h] + fin(part(pexp, jnp.sum), jnp.sum)
                acc_ref[h] = alpha * acc_ref[h] + jnp.dot(vt2, pexp.astype(BF16),
                                                          preferred_element_type=F32)
                m_ref[h] = m_new

    stage_logits(0, 0)

    def att_pair(pp, _):
        c = 2 * pp
        stage_logits(c + 1, 1)
        stage_values(c, 0)
        stage_logits(c + 2, 0)
        stage_values(c + 1, 1)
        return 0

    lax.fori_loop(0, nck // 2, att_pair, 0)

    @pl.when(lax.rem(nck, 2) == 1)
    def _last_chunk():
        stage_values(nck - 1, 0)

    lo_rows = lax.broadcasted_iota(I32, (LANES, blk), 0) < HALF
    for p in range(N_PAIRS):
        oa = acc_ref[2 * p] / l_ref[2 * p]
        ob = acc_ref[2 * p + 1] / l_ref[2 * p + 1]
        o_ref[0, :, p * LANES:(p + 1) * LANES] = jnp.where(lo_rows, oa, ob).T.astype(o_ref.dtype)


def _dsa(proj3, vt4, rel_bias):
    B, S, _ = proj3.shape
    blk = DSA_BLK
    n_off = _dsa_n_off(blk)
    n_sel = min(TOPK_MAX, S // 4)
    bucket = jnp.asarray(_dsa_bucket_tiles(blk))
    n_chunks = S // blk
    assert S % blk == 0
    kern = functools.partial(_dsa_kernel, blk=blk, n_chunks=n_chunks, n_sel=n_sel, n_off=n_off)
    return pl.pallas_call(
        kern,
        grid=(B, S // blk),
        in_specs=[
            pl.BlockSpec((n_off, blk, blk), lambda b, i: (0, 0, 0)),
            pl.BlockSpec(memory_space=pltpu.SMEM),
            pl.BlockSpec((1, blk, ATT_W), lambda b, i: (b, i, COL_QA // ATT_W)),
            pl.BlockSpec((1, S, ATT_W), lambda b, i: (b, 0, COL_KA // ATT_W)),
            pl.BlockSpec((1, S // blk, ATT_W, blk), lambda b, i: (b, 0, 0, 0)),
            pl.BlockSpec((1, blk, ATT_W), lambda b, i: (b, i, COL_QI // ATT_W)),
            pl.BlockSpec((1, S, LANES), lambda b, i: (b, 0, COL_KK // LANES)),
            pl.BlockSpec((1, blk, LANES), lambda b, i: (b, i, COL_WI // LANES)),
        ],
        out_specs=pl.BlockSpec((1, blk, ATT_W), lambda b, i: (b, i, 0)),
        out_shape=jax.ShapeDtypeStruct((B, S, ATT_W), BF16),
        scratch_shapes=[
            pltpu.VMEM((S // blk, blk, blk), F32),
            pltpu.VMEM((A_HEADS, n_off, blk, blk), F32),
            pltpu.VMEM((A_HEADS, 1, blk), F32),
            pltpu.VMEM((A_HEADS, 1, blk), F32),
            pltpu.VMEM((A_HEADS, LANES, blk), F32),
            pltpu.VMEM((2, A_HEADS, blk, blk), F32),
            pltpu.VMEM((2, A_HEADS, 1, blk), F32),
            pltpu.VMEM((SUBLANES, blk), F32),
        ],
        compiler_params=_params(("arbitrary", "arbitrary")),
    )(bucket, rel_bias, proj3, proj3, vt4, proj3, proj3, proj3)


def _sb_kernel(q_ref, k_ref, v_ref, o_ref, hl_ref, z_ref, *, t):
    i = pl.program_id(2)
    n = i + 1
    lane = lax.broadcasted_iota(I32, (t, LANES), 1)
    lo_half = lane < HALF
    q_m = []
    for pr in range(SB_PAIRS):
        q2 = q_ref[0, :, pr * LANES:(pr + 1) * LANES].astype(F32) * QK_SCALE
        q_m += [jnp.where(lo_half, q2, 0.0).astype(BF16), jnp.where(lo_half, 0.0, q2).astype(BF16)]
    heads = 2 * SB_PAIRS
    r = lax.broadcasted_iota(I32, (t, t), 0)
    cidx = lax.broadcasted_iota(I32, (t, t), 1)
    neg_from = jnp.where(r >= cidx, -1.0, 0.0).astype(BF16)
    neg_from2 = jnp.concatenate([neg_from, neg_from], axis=0)
    diff = cidx - r

    def stage_terms(step, slot, diagonal):
        c0 = pl.multiple_of(jnp.maximum(i - step, 0) * t, t)
        if diagonal:
            keep = diff < 0
        for hh in range(heads):
            pr = hh // 2
            k2 = k_ref[0, pl.ds(c0, t), pr * LANES:(pr + 1) * LANES]
            z = lax.dot_general(q_m[hh], k2, NT_DIMS, preferred_element_type=F32)
            sp = jnp.maximum(z, 0.0) + jnp.log(1.0 + jnp.exp(-jnp.abs(z)))
            if diagonal:
                sp = jnp.where(keep, sp, 0.0)
                z = jnp.where(keep, z, NEG)
            hi = sp.astype(BF16)
            hl_ref[slot, hh, :, :t] = hi
            hl_ref[slot, hh, :, t:] = (sp - hi.astype(F32)).astype(BF16)
            z_ref[slot, hh] = z

    def stage_apply(step, slot, carry):
        c0 = pl.multiple_of((i - step) * t, t)
        cum_all = jnp.dot(hl_ref[slot].reshape(heads * t, 2 * t), neg_from2, preferred_element_type=F32)
        out = []
        for hh in range(heads):
            pr = hh // 2
            v2 = v_ref[0, pl.ds(c0, t), pr * LANES:(pr + 1) * LANES]
            car, acc = carry[hh]
            cum = cum_all[hh * t:(hh + 1) * t]
            w = jnp.exp(z_ref[slot, hh] + cum + car)
            acc = acc + jnp.dot(w.astype(BF16), v2, preferred_element_type=F32)
            out.append((car + cum[:, 0:1], acc))
        return tuple(out)

    z1 = jnp.zeros((t, 1), F32)
    za = jnp.zeros((t, LANES), F32)
    stage_terms(0, 0, True)

    def pair_body(pp, carry):
        step = 2 * pp
        stage_terms(step + 1, 1, False)
        carry = stage_apply(step, 0, carry)
        stage_terms(step + 2, 0, False)
        return stage_apply(step + 1, 1, carry)

    carry = lax.fori_loop(0, n // 2, pair_body, ((z1, za),) * heads)
    carry = lax.cond(lax.rem(n, 2) == 1, lambda c: stage_apply(n - 1, 0, c), lambda c: c, carry)
    for pr in range(SB_PAIRS):
        o_ref[0, :, pr * LANES:(pr + 1) * LANES] = jnp.where(
            lo_half, carry[2 * pr][1], carry[2 * pr + 1][1]).astype(o_ref.dtype)


def _stick_breaking(proj3):
    B, S, _ = proj3.shape
    t = SB_T
    w = SB_PAIRS * LANES
    qb, kb, vb = COL_QB // w, COL_KB // w, COL_VB // w
    return pl.pallas_call(
        functools.partial(_sb_kernel, t=t),
        grid=(B, N_PAIRS // SB_PAIRS, S // t),
        in_specs=[
            pl.BlockSpec((1, t, w), lambda b, p, i: (b, i, qb + p)),
            pl.BlockSpec((1, S, w), lambda b, p, i: (b, 0, kb + p)),
            pl.BlockSpec((1, S, w), lambda b, p, i: (b, 0, vb + p)),
        ],
        out_specs=pl.BlockSpec((1, t, w), lambda b, p, i: (b, i, p)),
        out_shape=jax.ShapeDtypeStruct((B, S, ATT_W), BF16),
        scratch_shapes=[
            pltpu.VMEM((2, 2 * SB_PAIRS, t, 2 * t), BF16),
            pltpu.VMEM((2, 2 * SB_PAIRS, t, t), F32),
        ],
        compiler_params=_params(("arbitrary", "arbitrary", "arbitrary")),
    )(proj3, proj3, proj3)


def _layer_norm(r, g, b):
    mu = jnp.mean(r, axis=-1, keepdims=True)
    d = r - mu
    var = jnp.mean(d * d, axis=-1, keepdims=True)
    return d * lax.rsqrt(var + LN_EPS) * g + b


def _split_bf16(v):
    hi = v.astype(BF16)
    return hi, (v - hi.astype(F32)).astype(BF16)


def _pack_bf16_pairs(v):
    half = v.shape[1] // 2
    bits = pltpu.bitcast(v.astype(BF16).astype(F32), jnp.uint32)
    return bits[:, :half] | (bits[:, half:] >> 16)


def _unpack_bf16_pairs(w):
    return jnp.concatenate([pltpu.bitcast(w & jnp.uint32(0xFFFF0000), F32),
                            pltpu.bitcast(w << 16, F32)], axis=1)


def _merge_kernel(x_ref, ya_ref, yb_ref, wg_ref, wa_ref, wb_ref, wo_ref, g_ref, b_ref,
                  wr_ref, br_ref, h_ref, hp_ref, e_ref, p_ref, *, alpha, d):
    pa = jnp.dot(ya_ref[...], wa_ref[...], preferred_element_type=F32)
    pb = jnp.dot(yb_ref[...], wb_ref[...], preferred_element_type=F32)
    gates = jnp.dot(x_ref[...].astype(BF16), wg_ref[...], preferred_element_type=F32)
    merged = jax.nn.sigmoid(gates[:, :d]) * pa + jax.nn.sigmoid(gates[:, d:]) * pb
    m = jnp.dot(merged.astype(BF16), wo_ref[...], preferred_element_type=F32)
    h = _layer_norm(alpha * x_ref[...] + m, g_ref[...], b_ref[...])
    h_ref[...] = h
    hp_ref[...] = _pack_bf16_pairs(h)

    h_hi, h_lo = _split_bf16(h)
    w_hi, w_lo = _split_bf16(wr_ref[...])
    logit = (lax.dot_general(w_hi, h_hi, NT_DIMS, preferred_element_type=F32)
             + lax.dot_general(w_hi, h_lo, NT_DIMS, preferred_element_type=F32)
             + lax.dot_general(w_lo, h_hi, NT_DIMS, preferred_element_type=F32)) + br_ref[...]
    eid = lax.broadcasted_iota(I32, logit.shape, 0)
    vals, ids = [], []
    for _ in range(TOP_K):
        mx = jnp.max(logit, axis=0, keepdims=True)
        am = jnp.min(jnp.where(logit == mx, eid, N_EXPERTS), axis=0, keepdims=True)
        vals.append(mx)
        ids.append(am)
        logit = jnp.where(eid == am, -jnp.inf, logit)
    ex = [jnp.exp(v - vals[0]) for v in vals]
    den = ex[0] + ex[1] + ex[2] + ex[3]
    for k in range(TOP_K):
        e_ref[k:k + 1, :] = ids[k]
        p_ref[k:k + 1, :] = ex[k] / den


def _merge(x2, ya, yb, wg, wa, wb, wo, g, b, wr_t, br, alpha):
    T, D = x2.shape
    tm = MERGE_TM
    row = lambda i: (i, 0)
    fixed = lambda i: (0, 0)
    return pl.pallas_call(
        functools.partial(_merge_kernel, alpha=alpha, d=D),
        grid=(T // tm,),
        in_specs=[
            pl.BlockSpec((tm, D), row),
            pl.BlockSpec((tm, ATT_W), row),
            pl.BlockSpec((tm, ATT_W), row),
            pl.BlockSpec((D, 2 * D), fixed),
            pl.BlockSpec((ATT_W, D), fixed),
            pl.BlockSpec((ATT_W, D), fixed),
            pl.BlockSpec((D, D), fixed),
            pl.BlockSpec((1, D), fixed),
            pl.BlockSpec((1, D), fixed),
            pl.BlockSpec((N_EXPERTS, D), fixed),
            pl.BlockSpec((N_EXPERTS, 1), fixed),
        ],
        out_specs=[
            pl.BlockSpec((tm, D), row),
            pl.BlockSpec((tm, D // 2), row),
            pl.BlockSpec((TOP_K, tm), lambda i: (0, i)),
            pl.BlockSpec((TOP_K, tm), lambda i: (0, i)),
        ],
        out_shape=[
            jax.ShapeDtypeStruct((T, D), F32),
            jax.ShapeDtypeStruct((T, D // 2), jnp.uint32),
            jax.ShapeDtypeStruct((TOP_K, T), I32),
            jax.ShapeDtypeStruct((TOP_K, T), F32),
        ],
        compiler_params=_params(("arbitrary",)),
    )(x2, ya, yb, wg, wa, wb, wo, g, b, wr_t, br)


def _sc_gather_rows(table, idx):
    n = idx.shape[0]
    d = table.shape[1]
    info = plsc.get_sparse_core_info()
    n_cores, n_sub = info.num_cores, info.num_subcores
    per_w = n // (n_cores * n_sub)
    c = SC_GATHER_BYTES // (d * table.dtype.itemsize)
    n_g = per_w // c
    assert n == per_w * n_cores * n_sub and per_w == n_g * c and n_g % 2 == 0 and n_g >= 2
    mesh = plsc.VectorSubcoreMesh(core_axis_name="c", subcore_axis_name="s")

    @functools.partial(
        pl.kernel, mesh=mesh, out_type=jax.ShapeDtypeStruct((n, d), table.dtype),
        scratch_types=[pltpu.VMEM((per_w,), I32), pltpu.VMEM((2, c, d), table.dtype),
                       pltpu.SemaphoreType.DMA((2,)), pltpu.SemaphoreType.DMA((2,))])
    def gather_kernel(table_hbm, idx_hbm, out_hbm, idx_v, rows_v, gsem, wsem):
        base = (lax.axis_index("s") * n_cores + lax.axis_index("c")) * per_w
        pltpu.sync_copy(idx_hbm.at[pl.ds(base, per_w)], idx_v)

        def gather(g, b):
            return pltpu.make_async_copy(table_hbm.at[idx_v.at[pl.ds(g * c, c)]], rows_v.at[b],
                                         gsem.at[b])

        def write(g, b):
            return pltpu.make_async_copy(rows_v.at[b], out_hbm.at[pl.ds(base + g * c, c)], wsem.at[b])

        gather(0, 0).start()

        @pl.loop(0, n_g, step=2)
        def _ring(g0):
            for b in range(2):
                g = g0 + b

                @pl.when(g + 1 < n_g)
                def _next():
                    @pl.when(g >= 1)
                    def _buffer_free():
                        write(g - 1, 1 - b).wait()
                    gather(g + 1, 1 - b).start()

                gather(g, b).wait()
                write(g, b).start()

        write(n_g - 2, 0).wait()
        write(n_g - 1, 1).wait()

    return gather_kernel(table, idx)


def _moe_kernel(blk_e_ref, nused_ref, x_ref, wgu_ref, bgu_ref, wdn_ref, bdn_ref, o_ref,
                wgu_s, wdn_s, *, f):
    i = pl.program_id(0)
    nused = nused_ref[0]

    @pl.when(i < nused)
    def _compute():
        changed = jnp.logical_or(i == 0, blk_e_ref[i] != blk_e_ref[jnp.maximum(i - 1, 0)])

        @pl.when(changed)
        def _cast_weights():
            wgu_s[...] = wgu_ref[0].astype(BF16)
            wdn_s[...] = wdn_ref[0].astype(BF16)

        x = _unpack_bf16_pairs(x_ref[...]).astype(BF16)
        hgu = jnp.dot(x, wgu_s[...], preferred_element_type=F32) + bgu_ref[0]
        a = jnp.minimum(hgu[:, :f], SWIGLU_LIMIT)
        u = jnp.clip(hgu[:, f:], -SWIGLU_LIMIT, SWIGLU_LIMIT)
        glu = a * jax.nn.sigmoid(a * SWIGLU_ALPHA)
        y = jnp.dot(((u + 1.0) * glu).astype(BF16), wdn_s[...], preferred_element_type=F32) + bdn_ref[0]
        o_ref[...] = _pack_bf16_pairs(y)

    @pl.when(i >= nused)
    def _unused_block():
        o_ref[...] = jnp.zeros(o_ref.shape, o_ref.dtype)


def _moe_ffn(xs, blk_e, nused, w_gu, b_gu, w_dn, b_dn):
    P = xs.shape[0]
    E, D, F2 = w_gu.shape
    f = F2 // 2
    blk = MOE_BLK
    nb = P // blk
    used_block = lambda i, be, nu: (jnp.minimum(i, nu[0] - 1), 0)
    grid_spec = pltpu.PrefetchScalarGridSpec(
        num_scalar_prefetch=2,
        grid=(nb,),
        in_specs=[
            pl.BlockSpec((blk, D // 2), used_block),
            pl.BlockSpec((1, D, F2), lambda i, be, nu: (be[i], 0, 0)),
            pl.BlockSpec((1, 1, F2), lambda i, be, nu: (be[i], 0, 0)),
            pl.BlockSpec((1, f, D), lambda i, be, nu: (be[i], 0, 0)),
            pl.BlockSpec((1, 1, D), lambda i, be, nu: (be[i], 0, 0)),
        ],
        out_specs=pl.BlockSpec((blk, D // 2), lambda i, be, nu: (i, 0)),
        scratch_shapes=[
            pltpu.VMEM((D, F2), BF16),
            pltpu.VMEM((f, D), BF16),
        ],
    )
    return pl.pallas_call(
        functools.partial(_moe_kernel, f=f),
        grid_spec=grid_spec,
        out_shape=jax.ShapeDtypeStruct((P, D // 2), jnp.uint32),
        compiler_params=_params(("arbitrary",)),
    )(blk_e, nused, xs, w_gu, b_gu.reshape(E, 1, F2), w_dn, b_dn.reshape(E, 1, D))


def _comb_kernel(h_ref, p_ref, y_ref, g_ref, b_ref, o_ref, *, alpha):
    gate = p_ref[...]
    y = [_unpack_bf16_pairs(y_ref[k]) for k in range(TOP_K)]
    fsum = (y[0] * gate[:, 0:1] + y[1] * gate[:, 1:2]) + (y[2] * gate[:, 2:3] + y[3] * gate[:, 3:4])
    o_ref[...] = _layer_norm(alpha * h_ref[...] + fsum, g_ref[...], b_ref[...])


def _combine(h, y4, top_p, g, b, alpha):
    T, D = h.shape
    tm = COMB_TM
    return pl.pallas_call(
        functools.partial(_comb_kernel, alpha=alpha),
        grid=(T // tm,),
        in_specs=[
            pl.BlockSpec((tm, D), lambda i: (i, 0)),
            pl.BlockSpec((tm, TOP_K), lambda i: (i, 0)),
            pl.BlockSpec((TOP_K, tm, D // 2), lambda i: (0, i, 0)),
            pl.BlockSpec((1, D), lambda i: (0, 0)),
            pl.BlockSpec((1, D), lambda i: (0, 0)),
        ],
        out_specs=pl.BlockSpec((tm, D), lambda i: (i, 0)),
        out_shape=jax.ShapeDtypeStruct((T, D), F32),
        compiler_params=_params(("arbitrary",)),
    )(h, top_p.T, y4, g, b)


def _route(top_e, blk):
    K, T = top_e.shape
    N = K * T
    flat_e = top_e.reshape(N)
    experts = jnp.arange(N_EXPERTS, dtype=I32)
    order = jnp.argsort(flat_e, stable=True).astype(I32)
    inv = jnp.argsort(order).astype(I32)
    onehot = flat_e[:, None] == experts[None, :]
    counts = jnp.sum(onehot, axis=0, dtype=I32)
    padded = (counts + blk - 1) // blk * blk
    pends = jnp.cumsum(padded)
    offs = jnp.cumsum(counts) - counts
    shift = (pends - padded) - offs
    pos = inv + jnp.sum(jnp.where(onehot, shift[None, :], 0), axis=1, dtype=I32)
    P = N + N_EXPERTS * blk
    nb = P // blk
    blk_start = jnp.arange(nb, dtype=I32) * blk
    blk_e = jnp.minimum(jnp.sum(pends[None, :] <= blk_start[:, None], axis=1, dtype=I32), N_EXPERTS - 1)
    j = (blk_start - shift[blk_e])[:, None] + jnp.arange(blk, dtype=I32)[None, :]
    valid = j < (offs + counts)[blk_e][:, None]
    src = order[jnp.clip(j, 0, N - 1)]
    row_tok = jnp.where(valid, src % T, j % T).reshape(P)
    nused = (pends[-1:] // blk).astype(I32)
    return blk_e, nused, row_tok, pos.reshape(K, T)


def _projection_weights(w_in_l):
    sizes = (ATT_W, ATT_W, ATT_W, IDX_HEADS * IDX_DIM, IDX_DIM, IDX_HEADS, ATT_W, ATT_W, ATT_W)
    offs = np.concatenate([[0], np.cumsum(sizes)])
    qa, ka, va, qi, ki, wi, qb, kb, vb = (w_in_l[:, offs[n]:offs[n + 1]] for n in range(9))
    pad_wi = jnp.zeros((w_in_l.shape[0], LANES - IDX_HEADS), w_in_l.dtype)
    w_att = jnp.concatenate([qa, ka, qi, qb, kb, vb, ki, ki, wi, pad_wi], axis=1).astype(BF16)
    w_va_t = va.T.astype(BF16)
    w_gate = w_in_l[:, offs[9]:].astype(BF16)
    return w_att, w_va_t, w_gate


def kernel(x, w_in, w_branch_a, w_branch_b, w_out, rel_bias, ln1_g, ln1_b, w_router, b_router,
           w_gate_up, b_gate_up, w_down, b_down, ln2_g, ln2_b):
    B, S, D = x.shape
    depth = w_in.shape[0]
    alpha = (2 * depth) ** 0.25
    T = B * S
    h = x.reshape(T, D)
    for l in range(depth):
        w_att, w_va_t, w_gate = _projection_weights(w_in[l])
        proj, vt = _projection(h, w_att, w_va_t, min(T, PROJ_TM), PROJ_TN, DSA_BLK)
        proj = proj.reshape(B, S, ATT_COLS)
        vt = vt.reshape(B, S // DSA_BLK, ATT_W, DSA_BLK)
        ya = _dsa(proj, vt, rel_bias).reshape(T, ATT_W)
        yb = _stick_breaking(proj).reshape(T, ATT_W)
        h1, h1_packed, top_e, top_p = _merge(
            h, ya, yb, w_gate, w_branch_a[l].astype(BF16), w_branch_b[l].astype(BF16),
            w_out[l].astype(BF16), ln1_g[l].reshape(1, D), ln1_b[l].reshape(1, D),
            w_router[l].T, b_router[l].reshape(N_EXPERTS, 1), alpha)
        blk_e, nused, row_tok, pos = _route(top_e, MOE_BLK)
        xs = _sc_gather_rows(h1_packed, row_tok)
        ys = _moe_ffn(xs, blk_e, nused, w_gate_up[l], b_gate_up[l], w_down[l], b_down[l])
        y4 = _sc_gather_rows(ys, pos.reshape(TOP_K * T)).reshape(TOP_K, T, D // 2)
        h = _combine(h1, y4, top_p, ln2_g[l].reshape(1, D), ln2_b[l].reshape(1, D), alpha)
    return h.reshape(B, S, D)
```

```python
import functools
import math

import numpy as np
import jax
import jax.numpy as jnp
from jax import lax
from jax.experimental import pallas as pl
from jax.experimental.pallas import tpu as pltpu
from jax.experimental.pallas import tpu_sc as plsc

F32 = jnp.float32
BF16 = jnp.bfloat16
I32 = jnp.int32

A_HEADS = 8
HEAD_DIM = 64
ATT_W = A_HEADS * HEAD_DIM
IDX_HEADS = 8
IDX_DIM = 64
IDX_SCALE = (IDX_HEADS * IDX_DIM) ** -0.5
TOPK_MAX = 256
N_BUCKETS = 32
MAX_DISTANCE = 128
N_EXPERTS = 32
TOP_K = 4
SWIGLU_LIMIT = 7.0
SWIGLU_ALPHA = 1.702
LN_EPS = 1e-5
QK_SCALE = HEAD_DIM ** -0.5

LANES = 128
SUBLANES = 8
HALF = LANES // 2
N_PAIRS = A_HEADS // 2
VMEM_LIMIT = 56 * 1024 * 1024

DSA_BLK = 256
SB_T = 256
SB_PAIRS = 4
PROJ_TM = 1024
MERGE_TM = 512
MOE_BLK = 512
COMB_TM = 512
SC_GATHER_BYTES = 128 * 1024
REDUCE_CHAINS = 8
BISECT_CAP = 24
BISECT_FREE = 20
NEG = -1e30

COL_QA, COL_KA, COL_QI, COL_QB, COL_KB, COL_VB = (g * ATT_W for g in range(6))
COL_KK = 6 * ATT_W
COL_WI = COL_KK + LANES
ATT_COLS = COL_WI + LANES
PROJ_TN = ATT_COLS // 2

NT_DIMS = (((1,), (1,)), ((), ()))


def _params(sem, vmem=VMEM_LIMIT):
    return pltpu.CompilerParams(dimension_semantics=sem, vmem_limit_bytes=vmem)


def _proj_kernel(x_ref, w_ref, wt_ref, o_ref, ot_ref, xb_ref, *, tt):
    @pl.when(pl.program_id(1) == 0)
    def _row_tile_start():
        xb_ref[...] = x_ref[...].astype(BF16)
        for r in range(ot_ref.shape[0]):
            ot_ref[r] = lax.dot_general(wt_ref[...], xb_ref[r * tt:(r + 1) * tt, :], NT_DIMS,
                                        preferred_element_type=F32).astype(ot_ref.dtype)

    o_ref[...] = jnp.dot(xb_ref[...], w_ref[...], preferred_element_type=F32).astype(o_ref.dtype)


def _projection(x, w, w_t, tm, tn, tt):
    M, K = x.shape
    N = w.shape[1]
    Nt = w_t.shape[0]
    return pl.pallas_call(
        functools.partial(_proj_kernel, tt=tt),
        grid=(M // tm, N // tn),
        in_specs=[pl.BlockSpec((tm, K), lambda i, j: (i, 0)),
                  pl.BlockSpec((K, tn), lambda i, j: (0, j)),
                  pl.BlockSpec((Nt, K), lambda i, j: (0, 0))],
        out_specs=[pl.BlockSpec((tm, tn), lambda i, j: (i, j)),
                   pl.BlockSpec((tm // tt, Nt, tt), lambda i, j: (i, 0, 0))],
        out_shape=[jax.ShapeDtypeStruct((M, N), BF16),
                   jax.ShapeDtypeStruct((M // tt, Nt, tt), BF16)],
        scratch_shapes=[pltpu.VMEM((tm, K), BF16)],
        compiler_params=_params(("arbitrary", "arbitrary")),
    )(x, w, w_t)


def _t5_bucket_np(n):
    n = np.maximum(n, 0)
    max_exact = N_BUCKETS // 2
    nf = np.maximum(n, 1).astype(np.float32)
    large = max_exact + (np.log(nf / max_exact) / math.log(MAX_DISTANCE / max_exact)
                         * (N_BUCKETS - max_exact)).astype(np.int32)
    large = np.minimum(large, N_BUCKETS - 1)
    return np.where(n < max_exact, n, large).astype(np.int32)


def _dsa_n_off(blk):
    return 2 + -(-MAX_DISTANCE // blk)


def _dsa_bucket_tiles(blk):
    n_off = _dsa_n_off(blk)
    j = np.arange(blk)[None, :, None]
    i = np.arange(blk)[None, None, :]
    o = np.arange(n_off)[:, None, None]
    return _t5_bucket_np(i - j + blk * (n_off - 1 - o))


def _dsa_kernel(bucket_ref, relb_ref, q_ref, k_ref, vt_ref, qi_ref, kk_ref, wi_ref, o_ref,
                sc_ref, bias_ref, m_ref, l_ref, acc_ref, st_ref, mx_ref,
                *, blk, n_chunks, n_sel, n_off):
    b = pl.program_id(0)
    i = pl.program_id(1)
    q0 = i * blk
    nck = i + 1
    groups = blk // SUBLANES

    @pl.when(jnp.logical_and(b == 0, i == 0))
    def _build_bias():
        def head_body(h, _):
            for o in range(n_off):
                for rb in range(blk // LANES):
                    for cb in range(blk // LANES):
                        rs = slice(rb * LANES, (rb + 1) * LANES)
                        cs = slice(cb * LANES, (cb + 1) * LANES)
                        bk = bucket_ref[o, rs, cs]

                        def bucket_body(n, acc):
                            return jnp.where(bk == n, relb_ref[n, h], acc)

                        bias_ref[h, o, rs, cs] = lax.fori_loop(
                            0, N_BUCKETS, bucket_body, jnp.zeros((LANES, LANES), F32))
            return 0

        lax.fori_loop(0, A_HEADS, head_body, 0)

    lane = lax.broadcasted_iota(I32, (blk, LANES), 1)
    lo_half = lane < HALF
    krow = lax.broadcasted_iota(I32, (blk, blk), 0)
    qpos = q0 + lax.broadcasted_iota(I32, (1, blk), 1)

    def pair_split(ref, scale):
        out = []
        for p in range(N_PAIRS):
            v = ref[0, :, p * LANES:(p + 1) * LANES].astype(F32)
            if scale != 1.0:
                v = v * scale
            out.append(jnp.where(lo_half, v, 0.0).astype(BF16))
            out.append(jnp.where(lo_half, 0.0, v).astype(BF16))
        return out

    wi_t = wi_ref[0].astype(F32).T
    wrow = [wi_t[h:h + 1, :] * IDX_SCALE for h in range(IDX_HEADS)]
    qi_m = pair_split(qi_ref, 1.0)

    def score_chunk(c, _):
        c0 = pl.multiple_of(c * blk, blk)
        kk = kk_ref[0, pl.ds(c0, blk), :]
        acc = jnp.zeros((blk, blk), F32)
        for h in range(IDX_HEADS):
            s = lax.dot_general(kk, qi_m[h], NT_DIMS, preferred_element_type=F32)
            acc = acc + wrow[h] * jnp.maximum(s, 0.0)
        sc_ref[c] = jnp.where(c0 + krow <= qpos, acc, -jnp.inf)
        return 0

    lax.fori_loop(0, nck, score_chunk, 0)

    kt = jnp.minimum(qpos + 1, n_sel).astype(F32)

    def fold(fn, init):
        def body(c, acc):
            return fn(acc, sc_ref[c])
        return lax.fori_loop(0, nck, body, init)

    def part(x, op):
        y = op(x.reshape(REDUCE_CHAINS, groups // REDUCE_CHAINS, SUBLANES, blk), axis=1)
        return op(y, axis=0)

    def fin(x, op):
        return op(x, axis=0, keepdims=True)

    zeros8 = jnp.zeros((SUBLANES, blk), F32)
    pinf8 = jnp.full((SUBLANES, blk), jnp.inf, F32)

    def count_ge(th):
        return fin(fold(lambda a, s: a + part(jnp.where(s >= th, 1.0, 0.0), jnp.sum), zeros8),
                   jnp.sum)

    mn, mx = fold(lambda a, s: (
        jnp.minimum(a[0], part(jnp.where(s == -jnp.inf, jnp.inf, s), jnp.min)),
        jnp.maximum(a[1], part(s, jnp.max))), (pinf8, -pinf8))
    rmin = fin(mn, jnp.min)
    rmax = fin(mx, jnp.max)

    def bis_cond(st):
        it, lo, hi, clo = st
        return jnp.logical_and(it < BISECT_CAP, jnp.max(jnp.abs(clo - kt)) > 0.0)

    def halve(lo, hi, clo):
        mid = 0.5 * lo + 0.5 * hi
        c = count_ge(mid)
        active = clo != kt
        up = jnp.logical_and(active, c >= kt)
        dn = jnp.logical_and(active, c < kt)
        return jnp.where(up, mid, lo), jnp.where(dn, mid, hi), jnp.where(up, c, clo)

    def bis_body(st):
        it, lo, hi, clo = st
        return (it + 2,) + halve(*halve(lo, hi, clo))

    start = (rmin, rmax + jnp.maximum(1.0, jnp.abs(rmax) * 2.0 ** -20), (qpos + 1).astype(F32))
    start = lax.fori_loop(0, BISECT_FREE, lambda _, st: halve(*st), start)
    _, lo, _, clo = lax.while_loop(bis_cond, bis_body, (jnp.int32(BISECT_FREE),) + start)
    open_rows = jnp.max(jnp.abs(clo - kt)) > 0.0

    def stats(lo_):
        a_ = fin(fold(lambda a, s: jnp.minimum(a, part(jnp.where(s >= lo_, s, jnp.inf), jnp.min)),
                      pinf8), jnp.min)
        cg, ct, nx = fold(
            lambda a, s: (a[0] + part(jnp.where(s > a_, 1.0, 0.0), jnp.sum),
                               a[1] + part(jnp.where(s == a_, 1.0, 0.0), jnp.sum),
                               jnp.minimum(a[2], part(jnp.where(s > a_, s, jnp.inf), jnp.min))),
            (zeros8, zeros8, pinf8))
        return a_, fin(cg, jnp.sum), fin(ct, jnp.sum), fin(nx, jnp.min)

    def fin_cond(st):
        return st[0]

    def fin_body(st):
        _, lo_, _, _ = st
        a_, cgt_, nt_, nxt_ = stats(lo_)
        bad = cgt_ >= kt
        return (jnp.max(jnp.where(bad, 1.0, 0.0)) > 0.0, jnp.where(bad, nxt_, a_), cgt_, nt_)

    def exact_finish():
        _, a_, cgt, nties = lax.while_loop(fin_cond, fin_body, (jnp.bool_(True), lo, kt, kt))
        need_ = kt - cgt
        return a_, need_, jnp.max(jnp.where(nties > need_, 1.0, 0.0)) > 0.0

    a, need, excess = lax.cond(open_rows, exact_finish, lambda: (lo, kt, jnp.bool_(False)))

    def mask_plain():
        def body(c, _):
            sc_ref[c] = jnp.where(sc_ref[c] >= a, 0.0, NEG)
            return 0
        lax.fori_loop(0, nck, body, 0)

    def mask_ties():
        upto = (krow >= lax.broadcasted_iota(I32, (blk, blk), 1)).astype(BF16)

        def body(c, seen):
            s = sc_ref[c]
            tie = s == a
            rank = jnp.dot(upto, jnp.where(tie, 1.0, 0.0).astype(BF16),
                           preferred_element_type=F32) + seen
            sel = jnp.logical_or(s > a, jnp.logical_and(tie, rank <= need))
            sc_ref[c] = jnp.where(sel, 0.0, NEG)
            return rank[blk - 1:blk, :]

        lax.fori_loop(0, nck, body, jnp.zeros((1, blk), F32))

    lax.cond(excess, mask_ties, mask_plain)

    m_ref[...] = jnp.full(m_ref.shape, NEG, F32)
    l_ref[...] = jnp.zeros(l_ref.shape, F32)
    acc_ref[...] = jnp.zeros(acc_ref.shape, F32)
    q_m = pair_split(q_ref, QK_SCALE)

    def stage_logits(c, slot):
        c = jnp.minimum(c, n_chunks - 1)
        c0 = pl.multiple_of(c * blk, blk)
        madd = sc_ref[c]
        o_idx = jnp.clip(c - i + (n_off - 1), 0, n_off - 1)
        for p in range(N_PAIRS):
            k2 = k_ref[0, pl.ds(c0, blk), p * LANES:(p + 1) * LANES]
            for hh in range(2):
                h = 2 * p + hh
                s = lax.dot_general(k2, q_m[h], NT_DIMS, preferred_element_type=F32)
                s = s + bias_ref[h, o_idx] + madd
                st_ref[slot, h] = s
                mx_ref[slot, h] = fin(part(s, jnp.max), jnp.max)

    def stage_values(c, slot):
        for p in range(N_PAIRS):
            vt2 = vt_ref[0, c, p * LANES:(p + 1) * LANES, :]
            for hh in range(2):
                h = 2 * p + hh
                m_old = m_ref[h]
                m_new = jnp.maximum(m_old, mx_ref[slot, h])
                alpha = jnp.exp(m_old - m_new)
                pexp = jnp.exp(st_ref[slot, h] - m_new)
                l_ref[h] = alpha * l_ref[h] + fin(part(pexp, jnp.sum), jnp.sum)
                acc_ref[h] = alpha * acc_ref[h] + jnp.dot(vt2, pexp.astype(BF16),
                                                          preferred_element_type=F32)
                m_ref[h] = m_new

    stage_logits(0, 0)

    def att_pair(pp, _):
        c = 2 * pp
        stage_logits(c + 1, 1)
        stage_values(c, 0)
        stage_logits(c + 2, 0)
        stage_values(c + 1, 1)
        return 0

    lax.fori_loop(0, nck // 2, att_pair, 0)

    @pl.when(lax.rem(nck, 2) == 1)
    def _last_chunk():
        stage_values(nck - 1, 0)

    lo_rows = lax.broadcasted_iota(I32, (LANES, blk), 0) < HALF
    for p in range(N_PAIRS):
        oa = acc_ref[2 * p] / l_ref[2 * p]
        ob = acc_ref[2 * p + 1] / l_ref[2 * p + 1]
        o_ref[0, :, p * LANES:(p + 1) * LANES] = jnp.where(lo_rows, oa, ob).T.astype(o_ref.dtype)


def _dsa(proj3, vt4, rel_bias):
    B, S, _ = proj3.shape
    blk = DSA_BLK
    n_off = _dsa_n_off(blk)
    n_sel = min(TOPK_MAX, S // 4)
    bucket = jnp.asarray(_dsa_bucket_tiles(blk))
    n_chunks = S // blk
    assert S % blk == 0
    kern = functools.partial(_dsa_kernel, blk=blk, n_chunks=n_chunks, n_sel=n_sel, n_off=n_off)
    return pl.pallas_call(
        kern,
        grid=(B, S // blk),
        in_specs=[
            pl.BlockSpec((n_off, blk, blk), lambda b, i: (0, 0, 0)),
            pl.BlockSpec(memory_space=pltpu.SMEM),
            pl.BlockSpec((1, blk, ATT_W), lambda b, i: (b, i, COL_QA // ATT_W)),
            pl.BlockSpec((1, S, ATT_W), lambda b, i: (b, 0, COL_KA // ATT_W)),
            pl.BlockSpec((1, S // blk, ATT_W, blk), lambda b, i: (b, 0, 0, 0)),
            pl.BlockSpec((1, blk, ATT_W), lambda b, i: (b, i, COL_QI // ATT_W)),
            pl.BlockSpec((1, S, LANES), lambda b, i: (b, 0, COL_KK // LANES)),
            pl.BlockSpec((1, blk, LANES), lambda b, i: (b, i, COL_WI // LANES)),
        ],
        out_specs=pl.BlockSpec((1, blk, ATT_W), lambda b, i: (b, i, 0)),
        out_shape=jax.ShapeDtypeStruct((B, S, ATT_W), BF16),
        scratch_shapes=[
            pltpu.VMEM((S // blk, blk, blk), F32),
            pltpu.VMEM((A_HEADS, n_off, blk, blk), F32),
            pltpu.VMEM((A_HEADS, 1, blk), F32),
            pltpu.VMEM((A_HEADS, 1, blk), F32),
            pltpu.VMEM((A_HEADS, LANES, blk), F32),
            pltpu.VMEM((2, A_HEADS, blk, blk), F32),
            pltpu.VMEM((2, A_HEADS, 1, blk), F32),
        ],
        compiler_params=_params(("arbitrary", "arbitrary")),
    )(bucket, rel_bias, proj3, proj3, vt4, proj3, proj3, proj3)


def _sb_kernel(q_ref, k_ref, v_ref, o_ref, hl_ref, z_ref, *, t):
    i = pl.program_id(2)
    n = i + 1
    lane = lax.broadcasted_iota(I32, (t, LANES), 1)
    lo_half = lane < HALF
    q_m = []
    for pr in range(SB_PAIRS):
        q2 = q_ref[0, :, pr * LANES:(pr + 1) * LANES].astype(F32) * QK_SCALE
        q_m += [jnp.where(lo_half, q2, 0.0).astype(BF16), jnp.where(lo_half, 0.0, q2).astype(BF16)]
    heads = 2 * SB_PAIRS
    r = lax.broadcasted_iota(I32, (t, t), 0)
    cidx = lax.broadcasted_iota(I32, (t, t), 1)
    neg_from = jnp.where(r >= cidx, -1.0, 0.0).astype(BF16)
    neg_from2 = jnp.concatenate([neg_from, neg_from], axis=0)
    diff = cidx - r

    def stage_terms(step, slot, diagonal):
        c0 = pl.multiple_of(jnp.maximum(i - step, 0) * t, t)
        if diagonal:
            keep = diff < 0
        for hh in range(heads):
            pr = hh // 2
            k2 = k_ref[0, pl.ds(c0, t), pr * LANES:(pr + 1) * LANES]
            z = lax.dot_general(q_m[hh], k2, NT_DIMS, preferred_element_type=F32)
            sp = jnp.maximum(z, 0.0) + jnp.log(1.0 + jnp.exp(-jnp.abs(z)))
            if diagonal:
                sp = jnp.where(keep, sp, 0.0)
                z = jnp.where(keep, z, NEG)
            hi = sp.astype(BF16)
            hl_ref[slot, hh, :, :t] = hi
            hl_ref[slot, hh, :, t:] = (sp - hi.astype(F32)).astype(BF16)
            z_ref[slot, hh] = z

    def stage_apply(step, slot, carry):
        c0 = pl.multiple_of((i - step) * t, t)
        cum_all = jnp.dot(hl_ref[slot].reshape(heads * t, 2 * t), neg_from2, preferred_element_type=F32)
        out = []
        for hh in range(heads):
            pr = hh // 2
            v2 = v_ref[0, pl.ds(c0, t), pr * LANES:(pr + 1) * LANES]
            car, acc = carry[hh]
            cum = cum_all[hh * t:(hh + 1) * t]
            w = jnp.exp(z_ref[slot, hh] + cum + car)
            acc = acc + jnp.dot(w.astype(BF16), v2, preferred_element_type=F32)
            out.append((car + cum[:, 0:1], acc))
        return tuple(out)

    z1 = jnp.zeros((t, 1), F32)
    za = jnp.zeros((t, LANES), F32)
    stage_terms(0, 0, True)

    def pair_body(pp, carry):
        step = 2 * pp
        stage_terms(step + 1, 1, False)
        carry = stage_apply(step, 0, carry)
        stage_terms(step + 2, 0, False)
        return stage_apply(step + 1, 1, carry)

    carry = lax.fori_loop(0, n // 2, pair_body, ((z1, za),) * heads)
    carry = lax.cond(lax.rem(n, 2) == 1, lambda c: stage_apply(n - 1, 0, c), lambda c: c, carry)
    for pr in range(SB_PAIRS):
        o_ref[0, :, pr * LANES:(pr + 1) * LANES] = jnp.where(
            lo_half, carry[2 * pr][1], carry[2 * pr + 1][1]).astype(o_ref.dtype)


def _stick_breaking(proj3):
    B, S, _ = proj3.shape
    t = SB_T
    w = SB_PAIRS * LANES
    qb, kb, vb = COL_QB // w, COL_KB // w, COL_VB // w
    return pl.pallas_call(
        functools.partial(_sb_kernel, t=t),
        grid=(B, N_PAIRS // SB_PAIRS, S // t),
        in_specs=[
            pl.BlockSpec((1, t, w), lambda b, p, i: (b, i, qb + p)),
            pl.BlockSpec((1, S, w), lambda b, p, i: (b, 0, kb + p)),
            pl.BlockSpec((1, S, w), lambda b, p, i: (b, 0, vb + p)),
        ],
        out_specs=pl.BlockSpec((1, t, w), lambda b, p, i: (b, i, p)),
        out_shape=jax.ShapeDtypeStruct((B, S, ATT_W), BF16),
        scratch_shapes=[
            pltpu.VMEM((2, 2 * SB_PAIRS, t, 2 * t), BF16),
            pltpu.VMEM((2, 2 * SB_PAIRS, t, t), F32),
        ],
        compiler_params=_params(("arbitrary", "arbitrary", "arbitrary")),
    )(proj3, proj3, proj3)


def _layer_norm(r, g, b):
    mu = jnp.mean(r, axis=-1, keepdims=True)
    d = r - mu
    var = jnp.mean(d * d, axis=-1, keepdims=True)
    return d * lax.rsqrt(var + LN_EPS) * g + b


def _split_bf16(v):
    hi = v.astype(BF16)
    return hi, (v - hi.astype(F32)).astype(BF16)


def _pack_bf16_pairs(v):
    half = v.shape[1] // 2
    bits = pltpu.bitcast(v.astype(BF16).astype(F32), jnp.uint32)
    return bits[:, :half] | (bits[:, half:] >> 16)


def _unpack_bf16_pairs(w):
    return jnp.concatenate([pltpu.bitcast(w & jnp.uint32(0xFFFF0000), F32),
                            pltpu.bitcast(w << 16, F32)], axis=1)


def _merge_kernel(x_ref, ya_ref, yb_ref, wg_ref, wa_ref, wb_ref, wo_ref, g_ref, b_ref,
                  wr_ref, br_ref, h_ref, hp_ref, e_ref, p_ref, *, alpha, d):
    pa = jnp.dot(ya_ref[...], wa_ref[...], preferred_element_type=F32)
    pb = jnp.dot(yb_ref[...], wb_ref[...], preferred_element_type=F32)
    gates = jnp.dot(x_ref[...].astype(BF16), wg_ref[...], preferred_element_type=F32)
    merged = jax.nn.sigmoid(gates[:, :d]) * pa + jax.nn.sigmoid(gates[:, d:]) * pb
    m = jnp.dot(merged.astype(BF16), wo_ref[...], preferred_element_type=F32)
    h = _layer_norm(alpha * x_ref[...] + m, g_ref[...], b_ref[...])
    h_ref[...] = h
    hp_ref[...] = _pack_bf16_pairs(h)

    h_hi, h_lo = _split_bf16(h)
    w_hi, w_lo = _split_bf16(wr_ref[...])
    logit = (lax.dot_general(w_hi, h_hi, NT_DIMS, preferred_element_type=F32)
             + lax.dot_general(w_hi, h_lo, NT_DIMS, preferred_element_type=F32)
             + lax.dot_general(w_lo, h_hi, NT_DIMS, preferred_element_type=F32)) + br_ref[...]
    eid = lax.broadcasted_iota(I32, logit.shape, 0)
    vals, ids = [], []
    for _ in range(TOP_K):
        mx = jnp.max(logit, axis=0, keepdims=True)
        am = jnp.min(jnp.where(logit == mx, eid, N_EXPERTS), axis=0, keepdims=True)
        vals.append(mx)
        ids.append(am)
        logit = jnp.where(eid == am, -jnp.inf, logit)
    ex = [jnp.exp(v - vals[0]) for v in vals]
    den = ex[0] + ex[1] + ex[2] + ex[3]
    for k in range(TOP_K):
        e_ref[k:k + 1, :] = ids[k]
        p_ref[k:k + 1, :] = ex[k] / den


def _merge(x2, ya, yb, wg, wa, wb, wo, g, b, wr_t, br, alpha):
    T, D = x2.shape
    tm = MERGE_TM
    row = lambda i: (i, 0)
    fixed = lambda i: (0, 0)
    return pl.pallas_call(
        functools.partial(_merge_kernel, alpha=alpha, d=D),
        grid=(T // tm,),
        in_specs=[
            pl.BlockSpec((tm, D), row),
            pl.BlockSpec((tm, ATT_W), row),
            pl.BlockSpec((tm, ATT_W), row),
            pl.BlockSpec((D, 2 * D), fixed),
            pl.BlockSpec((ATT_W, D), fixed),
            pl.BlockSpec((ATT_W, D), fixed),
            pl.BlockSpec((D, D), fixed),
            pl.BlockSpec((1, D), fixed),
            pl.BlockSpec((1, D), fixed),
            pl.BlockSpec((N_EXPERTS, D), fixed),
            pl.BlockSpec((N_EXPERTS, 1), fixed),
        ],
        out_specs=[
            pl.BlockSpec((tm, D), row),
            pl.BlockSpec((tm, D // 2), row),
            pl.BlockSpec((TOP_K, tm), lambda i: (0, i)),
            pl.BlockSpec((TOP_K, tm), lambda i: (0, i)),
        ],
        out_shape=[
            jax.ShapeDtypeStruct((T, D), F32),
            jax.ShapeDtypeStruct((T, D // 2), jnp.uint32),
            jax.ShapeDtypeStruct((TOP_K, T), I32),
            jax.ShapeDtypeStruct((TOP_K, T), F32),
        ],
        compiler_params=_params(("arbitrary",)),
    )(x2, ya, yb, wg, wa, wb, wo, g, b, wr_t, br)


def _sc_gather_rows(table, idx):
    n = idx.shape[0]
    d = table.shape[1]
    info = plsc.get_sparse_core_info()
    n_cores, n_sub = info.num_cores, info.num_subcores
    per_w = n // (n_cores * n_sub)
    c = SC_GATHER_BYTES // (d * table.dtype.itemsize)
    n_g = per_w // c
    assert n == per_w * n_cores * n_sub and per_w == n_g * c and n_g % 2 == 0 and n_g >= 2
    mesh = plsc.VectorSubcoreMesh(core_axis_name="c", subcore_axis_name="s")

    @functools.partial(
        pl.kernel, mesh=mesh, out_type=jax.ShapeDtypeStruct((n, d), table.dtype),
        scratch_types=[pltpu.VMEM((per_w,), I32), pltpu.VMEM((2, c, d), table.dtype),
                       pltpu.SemaphoreType.DMA((2,)), pltpu.SemaphoreType.DMA((2,))])
    def gather_kernel(table_hbm, idx_hbm, out_hbm, idx_v, rows_v, gsem, wsem):
        base = (lax.axis_index("s") * n_cores + lax.axis_index("c")) * per_w
        pltpu.sync_copy(idx_hbm.at[pl.ds(base, per_w)], idx_v)

        def gather(g, b):
            return pltpu.make_async_copy(table_hbm.at[idx_v.at[pl.ds(g * c, c)]], rows_v.at[b],
                                         gsem.at[b])

        def write(g, b):
            return pltpu.make_async_copy(rows_v.at[b], out_hbm.at[pl.ds(base + g * c, c)], wsem.at[b])

        gather(0, 0).start()

        @pl.loop(0, n_g, step=2)
        def _ring(g0):
            for b in range(2):
                g = g0 + b

                @pl.when(g + 1 < n_g)
                def _next():
                    @pl.when(g >= 1)
                    def _buffer_free():
                        write(g - 1, 1 - b).wait()
                    gather(g + 1, 1 - b).start()

                gather(g, b).wait()
                write(g, b).start()

        write(n_g - 2, 0).wait()
        write(n_g - 1, 1).wait()

    return gather_kernel(table, idx)


def _moe_kernel(blk_e_ref, nused_ref, x_ref, wgu_ref, bgu_ref, wdn_ref, bdn_ref, o_ref,
                wgu_s, wdn_s, *, f):
    i = pl.program_id(0)
    nused = nused_ref[0]

    @pl.when(i < nused)
    def _compute():
        changed = jnp.logical_or(i == 0, blk_e_ref[i] != blk_e_ref[jnp.maximum(i - 1, 0)])

        @pl.when(changed)
        def _cast_weights():
            wgu_s[...] = wgu_ref[0].astype(BF16)
            wdn_s[...] = wdn_ref[0].astype(BF16)

        x = _unpack_bf16_pairs(x_ref[...]).astype(BF16)
        hgu = jnp.dot(x, wgu_s[...], preferred_element_type=F32) + bgu_ref[0]
        a = jnp.minimum(hgu[:, :f], SWIGLU_LIMIT)
        u = jnp.clip(hgu[:, f:], -SWIGLU_LIMIT, SWIGLU_LIMIT)
        glu = a * jax.nn.sigmoid(a * SWIGLU_ALPHA)
        y = jnp.dot(((u + 1.0) * glu).astype(BF16), wdn_s[...], preferred_element_type=F32) + bdn_ref[0]
        o_ref[...] = _pack_bf16_pairs(y)

    @pl.when(i >= nused)
    def _unused_block():
        o_ref[...] = jnp.zeros(o_ref.shape, o_ref.dtype)


def _moe_ffn(xs, blk_e, nused, w_gu, b_gu, w_dn, b_dn):
    P = xs.shape[0]
    E, D, F2 = w_gu.shape
    f = F2 // 2
    blk = MOE_BLK
    nb = P // blk
    used_block = lambda i, be, nu: (jnp.minimum(i, nu[0] - 1), 0)
    grid_spec = pltpu.PrefetchScalarGridSpec(
        num_scalar_prefetch=2,
        grid=(nb,),
        in_specs=[
            pl.BlockSpec((blk, D // 2), used_block),
            pl.BlockSpec((1, D, F2), lambda i, be, nu: (be[i], 0, 0)),
            pl.BlockSpec((1, 1, F2), lambda i, be, nu: (be[i], 0, 0)),
            pl.BlockSpec((1, f, D), lambda i, be, nu: (be[i], 0, 0)),
            pl.BlockSpec((1, 1, D), lambda i, be, nu: (be[i], 0, 0)),
        ],
        out_specs=pl.BlockSpec((blk, D // 2), lambda i, be, nu: (i, 0)),
        scratch_shapes=[
            pltpu.VMEM((D, F2), BF16),
            pltpu.VMEM((f, D), BF16),
        ],
    )
    return pl.pallas_call(
        functools.partial(_moe_kernel, f=f),
        grid_spec=grid_spec,
        out_shape=jax.ShapeDtypeStruct((P, D // 2), jnp.uint32),
        compiler_params=_params(("arbitrary",)),
    )(blk_e, nused, xs, w_gu, b_gu.reshape(E, 1, F2), w_dn, b_dn.reshape(E, 1, D))


def _comb_kernel(h_ref, p_ref, y_ref, g_ref, b_ref, o_ref, *, alpha):
    gate = p_ref[...]
    y = [_unpack_bf16_pairs(y_ref[k]) for k in range(TOP_K)]
    fsum = (y[0] * gate[:, 0:1] + y[1] * gate[:, 1:2]) + (y[2] * gate[:, 2:3] + y[3] * gate[:, 3:4])
    o_ref[...] = _layer_norm(alpha * h_ref[...] + fsum, g_ref[...], b_ref[...])


def _combine(h, y4, top_p, g, b, alpha):
    T, D = h.shape
    tm = COMB_TM
    return pl.pallas_call(
        functools.partial(_comb_kernel, alpha=alpha),
        grid=(T // tm,),
        in_specs=[
            pl.BlockSpec((tm, D), lambda i: (i, 0)),
            pl.BlockSpec((tm, TOP_K), lambda i: (i, 0)),
            pl.BlockSpec((TOP_K, tm, D // 2), lambda i: (0, i, 0)),
            pl.BlockSpec((1, D), lambda i: (0, 0)),
            pl.BlockSpec((1, D), lambda i: (0, 0)),
        ],
        out_specs=pl.BlockSpec((tm, D), lambda i: (i, 0)),
        out_shape=jax.ShapeDtypeStruct((T, D), F32),
        compiler_params=_params(("arbitrary",)),
    )(h, top_p.T, y4, g, b)


def _route(top_e, blk):
    K, T = top_e.shape
    N = K * T
    flat_e = top_e.reshape(N)
    experts = jnp.arange(N_EXPERTS, dtype=I32)
    order = jnp.argsort(flat_e, stable=True).astype(I32)
    inv = jnp.argsort(order).astype(I32)
    onehot = flat_e[:, None] == experts[None, :]
    counts = jnp.sum(onehot, axis=0, dtype=I32)
    padded = (counts + blk - 1) // blk * blk
    pends = jnp.cumsum(padded)
    offs = jnp.cumsum(counts) - counts
    shift = (pends - padded) - offs
    pos = inv + jnp.sum(jnp.where(onehot, shift[None, :], 0), axis=1, dtype=I32)
    P = N + N_EXPERTS * blk
    nb = P // blk
    blk_start = jnp.arange(nb, dtype=I32) * blk
    blk_e = jnp.minimum(jnp.sum(pends[None, :] <= blk_start[:, None], axis=1, dtype=I32), N_EXPERTS - 1)
    j = (blk_start - shift[blk_e])[:, None] + jnp.arange(blk, dtype=I32)[None, :]
    valid = j < (offs + counts)[blk_e][:, None]
    src = order[jnp.clip(j, 0, N - 1)]
    row_tok = jnp.where(valid, src % T, j % T).reshape(P)
    nused = (pends[-1:] // blk).astype(I32)
    return blk_e, nused, row_tok, pos.reshape(K, T)


def _projection_weights(w_in_l):
    sizes = (ATT_W, ATT_W, ATT_W, IDX_HEADS * IDX_DIM, IDX_DIM, IDX_HEADS, ATT_W, ATT_W, ATT_W)
    offs = np.concatenate([[0], np.cumsum(sizes)])
    qa, ka, va, qi, ki, wi, qb, kb, vb = (w_in_l[:, offs[n]:offs[n + 1]] for n in range(9))
    pad_wi = jnp.zeros((w_in_l.shape[0], LANES - IDX_HEADS), w_in_l.dtype)
    w_att = jnp.concatenate([qa, ka, qi, qb, kb, vb, ki, ki, wi, pad_wi], axis=1).astype(BF16)
    w_va_t = va.T.astype(BF16)
    w_gate = w_in_l[:, offs[9]:].astype(BF16)
    return w_att, w_va_t, w_gate


def kernel(x, w_in, w_branch_a, w_branch_b, w_out, rel_bias, ln1_g, ln1_b, w_router, b_router,
           w_gate_up, b_gate_up, w_down, b_down, ln2_g, ln2_b):
    B, S, D = x.shape
    depth = w_in.shape[0]
    alpha = (2 * depth) ** 0.25
    T = B * S
    h = x.reshape(T, D)
    for l in range(depth):
        w_att, w_va_t, w_gate = _projection_weights(w_in[l])
        proj, vt = _projection(h, w_att, w_va_t, min(T, PROJ_TM), PROJ_TN, DSA_BLK)
        proj = proj.reshape(B, S, ATT_COLS)
        vt = vt.reshape(B, S // DSA_BLK, ATT_W, DSA_BLK)
        ya = _dsa(proj, vt, rel_bias).reshape(T, ATT_W)
        yb = _stick_breaking(proj).reshape(T, ATT_W)
        h1, h1_packed, top_e, top_p = _merge(
            h, ya, yb, w_gate, w_branch_a[l].astype(BF16), w_branch_b[l].astype(BF16),
            w_out[l].astype(BF16), ln1_g[l].reshape(1, D), ln1_b[l].reshape(1, D),
            w_router[l].T, b_router[l].reshape(N_EXPERTS, 1), alpha)
        blk_e, nused, row_tok, pos = _route(top_e, MOE_BLK)
        xs = _sc_gather_rows(h1_packed, row_tok)
        ys = _moe_ffn(xs, blk_e, nused, w_gate_up[l], b_gate_up[l], w_down[l], b_down[l])
        y4 = _sc_gather_rows(ys, pos.reshape(TOP_K * T)).reshape(TOP_K, T, D // 2)
        h = _combine(h1, y4, top_p, ln2_g[l].reshape(1, D), ln2_b[l].reshape(1, D), alpha)
    return h.reshape(B, S, D)
```

```python
import functools
import math

import numpy as np
import jax
import jax.numpy as jnp
from jax import lax
from jax.experimental import pallas as pl
from jax.experimental.pallas import tpu as pltpu
from jax.experimental.pallas import tpu_sc as plsc

F32 = jnp.float32
BF16 = jnp.bfloat16
I32 = jnp.int32

A_HEADS = 8
HEAD_DIM = 64
ATT_W = A_HEADS * HEAD_DIM
IDX_HEADS = 8
IDX_DIM = 64
IDX_SCALE = (IDX_HEADS * IDX_DIM) ** -0.5
TOPK_MAX = 256
N_BUCKETS = 32
MAX_DISTANCE = 128
N_EXPERTS = 32
TOP_K = 4
SWIGLU_LIMIT = 7.0
SWIGLU_ALPHA = 1.702
LN_EPS = 1e-5
QK_SCALE = HEAD_DIM ** -0.5

LANES = 128
SUBLANES = 8
HALF = LANES // 2
N_PAIRS = A_HEADS // 2
VMEM_LIMIT = 56 * 1024 * 1024

DSA_BLK = 256
SB_T = 256
SB_PAIRS = 4
PROJ_TM = 1024
MERGE_TM = 512
MOE_BLK = 512
COMB_TM = 512
SC_GATHER_BYTES = 128 * 1024
REDUCE_CHAINS = 8
BISECT_CAP = 24
BISECT_FREE = 20
NEG = -1e30

COL_QA, COL_KA, COL_QI, COL_QB, COL_KB, COL_VB = (g * ATT_W for g in range(6))
COL_KK = 6 * ATT_W
COL_WI = COL_KK + LANES
ATT_COLS = COL_WI + LANES
PROJ_TN = ATT_COLS // 2

NT_DIMS = (((1,), (1,)), ((), ()))


def _params(sem, vmem=VMEM_LIMIT):
    return pltpu.CompilerParams(dimension_semantics=sem, vmem_limit_bytes=vmem)


def _proj_kernel(x_ref, w_ref, wt_ref, o_ref, ot_ref, xb_ref, *, tt):
    @pl.when(pl.program_id(1) == 0)
    def _row_tile_start():
        xb_ref[...] = x_ref[...].astype(BF16)
        for r in range(ot_ref.shape[0]):
            ot_ref[r] = lax.dot_general(wt_ref[...], xb_ref[r * tt:(r + 1) * tt, :], NT_DIMS,
                                        preferred_element_type=F32).astype(ot_ref.dtype)

    o_ref[...] = jnp.dot(xb_ref[...], w_ref[...], preferred_element_type=F32).astype(o_ref.dtype)


def _projection(x, w, w_t, tm, tn, tt):
    M, K = x.shape
    N = w.shape[1]
    Nt = w_t.shape[0]
    return pl.pallas_call(
        functools.partial(_proj_kernel, tt=tt),
        grid=(M // tm, N // tn),
        in_specs=[pl.BlockSpec((tm, K), lambda i, j: (i, 0)),
                  pl.BlockSpec((K, tn), lambda i, j: (0, j)),
                  pl.BlockSpec((Nt, K), lambda i, j: (0, 0))],
        out_specs=[pl.BlockSpec((tm, tn), lambda i, j: (i, j)),
                   pl.BlockSpec((tm // tt, Nt, tt), lambda i, j: (i, 0, 0))],
        out_shape=[jax.ShapeDtypeStruct((M, N), BF16),
                   jax.ShapeDtypeStruct((M // tt, Nt, tt), BF16)],
        scratch_shapes=[pltpu.VMEM((tm, K), BF16)],
        compiler_params=_params(("arbitrary", "arbitrary")),
    )(x, w, w_t)


def _t5_bucket_np(n):
    n = np.maximum(n, 0)
    max_exact = N_BUCKETS // 2
    nf = np.maximum(n, 1).astype(np.float32)
    large = max_exact + (np.log(nf / max_exact) / math.log(MAX_DISTANCE / max_exact)
                         * (N_BUCKETS - max_exact)).astype(np.int32)
    large = np.minimum(large, N_BUCKETS - 1)
    return np.where(n < max_exact, n, large).astype(np.int32)


def _dsa_n_off(blk):
    return 2 + -(-MAX_DISTANCE // blk)


def _dsa_bucket_tiles(blk):
    n_off = _dsa_n_off(blk)
    j = np.arange(blk)[None, :, None]
    i = np.arange(blk)[None, None, :]
    o = np.arange(n_off)[:, None, None]
    return _t5_bucket_np(i - j + blk * (n_off - 1 - o))


def _dsa_kernel(bucket_ref, relb_ref, q_ref, k_ref, vt_ref, qi_ref, kk_ref, wi_ref, o_ref,
                sc_ref, bias_ref, m_ref, l_ref, acc_ref, st_ref, mx_ref,
                *, blk, n_chunks, n_sel, n_off):
    b = pl.program_id(0)
    i = pl.program_id(1)
    q0 = i * blk
    nck = i + 1
    groups = blk // SUBLANES

    @pl.when(jnp.logical_and(b == 0, i == 0))
    def _build_bias():
        def head_body(h, _):
            for o in range(n_off):
                for rb in range(blk // LANES):
                    for cb in range(blk // LANES):
                        rs = slice(rb * LANES, (rb + 1) * LANES)
                        cs = slice(cb * LANES, (cb + 1) * LANES)
                        bk = bucket_ref[o, rs, cs]

                        def bucket_body(n, acc):
                            return jnp.where(bk == n, relb_ref[n, h], acc)

                        bias_ref[h, o, rs, cs] = lax.fori_loop(
                            0, N_BUCKETS, bucket_body, jnp.zeros((LANES, LANES), F32))
            return 0

        lax.fori_loop(0, A_HEADS, head_body, 0)

    lane = lax.broadcasted_iota(I32, (blk, LANES), 1)
    lo_half = lane < HALF
    krow = lax.broadcasted_iota(I32, (blk, blk), 0)
    qpos = q0 + lax.broadcasted_iota(I32, (1, blk), 1)

    def pair_split(ref, scale):
        out = []
        for p in range(N_PAIRS):
            v = ref[0, :, p * LANES:(p + 1) * LANES].astype(F32)
            if scale != 1.0:
                v = v * scale
            out.append(jnp.where(lo_half, v, 0.0).astype(BF16))
            out.append(jnp.where(lo_half, 0.0, v).astype(BF16))
        return out

    wi_t = wi_ref[0].astype(F32).T
    wrow = [wi_t[h:h + 1, :] * IDX_SCALE for h in range(IDX_HEADS)]
    qi_m = pair_split(qi_ref, 1.0)

    def part(x, op):
        y = op(x.reshape(REDUCE_CHAINS, groups // REDUCE_CHAINS, SUBLANES, blk), axis=1)
        return op(y, axis=0)

    def fin(x, op):
        return op(x, axis=0, keepdims=True)

    zeros8 = jnp.zeros((SUBLANES, blk), F32)
    pinf8 = jnp.full((SUBLANES, blk), jnp.inf, F32)

    def score_chunk(c, mn_mx):
        c0 = pl.multiple_of(c * blk, blk)
        kk = kk_ref[0, pl.ds(c0, blk), :]
        acc = jnp.zeros((blk, blk), F32)
        for h in range(IDX_HEADS):
            s = lax.dot_general(kk, qi_m[h], NT_DIMS, preferred_element_type=F32)
            acc = acc + wrow[h] * jnp.maximum(s, 0.0)
        causal = c0 + krow <= qpos
        sc_ref[c] = jnp.where(causal, acc, -jnp.inf)
        return (jnp.minimum(mn_mx[0], part(jnp.where(causal, acc, jnp.inf), jnp.min)),
                jnp.maximum(mn_mx[1], part(jnp.where(causal, acc, -jnp.inf), jnp.max)))

    mn, mx = lax.fori_loop(0, nck, score_chunk, (pinf8, -pinf8))
    rmin = fin(mn, jnp.min)
    rmax = fin(mx, jnp.max)

    kt = jnp.minimum(qpos + 1, n_sel).astype(F32)

    def fold(fn, init):
        def body(c, acc):
            return fn(acc, sc_ref[c])
        return lax.fori_loop(0, nck, body, init)

    def count_ge(th):
        return fin(fold(lambda a, s: a + part(jnp.where(s >= th, 1.0, 0.0), jnp.sum), zeros8),
                   jnp.sum)

    def bis_cond(st):
        it, lo, hi, clo = st
        return jnp.logical_and(it < BISECT_CAP, jnp.max(jnp.abs(clo - kt)) > 0.0)

    def halve(lo, hi, clo):
        mid = 0.5 * lo + 0.5 * hi
        c = count_ge(mid)
        active = clo != kt
        up = jnp.logical_and(active, c >= kt)
        dn = jnp.logical_and(active, c < kt)
        return jnp.where(up, mid, lo), jnp.where(dn, mid, hi), jnp.where(up, c, clo)

    def bis_body(st):
        it, lo, hi, clo = st
        return (it + 2,) + halve(*halve(lo, hi, clo))

    start = (rmin, rmax + jnp.maximum(1.0, jnp.abs(rmax) * 2.0 ** -20), (qpos + 1).astype(F32))
    start = lax.fori_loop(0, BISECT_FREE, lambda _, st: halve(*st), start)
    _, lo, _, clo = lax.while_loop(bis_cond, bis_body, (jnp.int32(BISECT_FREE),) + start)
    open_rows = jnp.max(jnp.abs(clo - kt)) > 0.0

    def stats(lo_):
        a_ = fin(fold(lambda a, s: jnp.minimum(a, part(jnp.where(s >= lo_, s, jnp.inf), jnp.min)),
                      pinf8), jnp.min)
        cg, ct, nx = fold(
            lambda a, s: (a[0] + part(jnp.where(s > a_, 1.0, 0.0), jnp.sum),
                               a[1] + part(jnp.where(s == a_, 1.0, 0.0), jnp.sum),
                               jnp.minimum(a[2], part(jnp.where(s > a_, s, jnp.inf), jnp.min))),
            (zeros8, zeros8, pinf8))
        return a_, fin(cg, jnp.sum), fin(ct, jnp.sum), fin(nx, jnp.min)

    def fin_cond(st):
        return st[0]

    def fin_body(st):
        _, lo_, _, _ = st
        a_, cgt_, nt_, nxt_ = stats(lo_)
        bad = cgt_ >= kt
        return (jnp.max(jnp.where(bad, 1.0, 0.0)) > 0.0, jnp.where(bad, nxt_, a_), cgt_, nt_)

    def exact_finish():
        _, a_, cgt, nties = lax.while_loop(fin_cond, fin_body, (jnp.bool_(True), lo, kt, kt))
        need_ = kt - cgt
        return a_, need_, jnp.max(jnp.where(nties > need_, 1.0, 0.0)) > 0.0

    a, need, excess = lax.cond(open_rows, exact_finish, lambda: (lo, kt, jnp.bool_(False)))

    def mask_plain():
        def body(c, _):
            sc_ref[c] = jnp.where(sc_ref[c] >= a, 0.0, NEG)
            return 0
        lax.fori_loop(0, nck, body, 0)

    def mask_ties():
        upto = (krow >= lax.broadcasted_iota(I32, (blk, blk), 1)).astype(BF16)

        def body(c, seen):
            s = sc_ref[c]
            tie = s == a
            rank = jnp.dot(upto, jnp.where(tie, 1.0, 0.0).astype(BF16),
                           preferred_element_type=F32) + seen
            sel = jnp.logical_or(s > a, jnp.logical_and(tie, rank <= need))
            sc_ref[c] = jnp.where(sel, 0.0, NEG)
            return rank[blk - 1:blk, :]

        lax.fori_loop(0, nck, body, jnp.zeros((1, blk), F32))

    lax.cond(excess, mask_ties, mask_plain)

    m_ref[...] = jnp.full(m_ref.shape, NEG, F32)
    l_ref[...] = jnp.zeros(l_ref.shape, F32)
    acc_ref[...] = jnp.zeros(acc_ref.shape, F32)
    q_m = pair_split(q_ref, QK_SCALE)

    def stage_logits(c, slot):
        c = jnp.minimum(c, n_chunks - 1)
        c0 = pl.multiple_of(c * blk, blk)
        madd = sc_ref[c]
        o_idx = jnp.clip(c - i + (n_off - 1), 0, n_off - 1)
        for p in range(N_PAIRS):
            k2 = k_ref[0, pl.ds(c0, blk), p * LANES:(p + 1) * LANES]
            for hh in range(2):
                h = 2 * p + hh
                s = lax.dot_general(k2, q_m[h], NT_DIMS, preferred_element_type=F32)
                s = s + bias_ref[h, o_idx] + madd
                st_ref[slot, h] = s
                mx_ref[slot, h] = fin(part(s, jnp.max), jnp.max)

    def stage_values(c, slot):
        for p in range(N_PAIRS):
            vt2 = vt_ref[0, c, p * LANES:(p + 1) * LANES, :]
            for hh in range(2):
                h = 2 * p + hh
                m_old = m_ref[h]
                m_new = jnp.maximum(m_old, mx_ref[slot, h])
                alpha = jnp.exp(m_old - m_new)
                pexp = jnp.exp(st_ref[slot, h] - m_new)
                l_ref[h] = alpha * l_ref[h] + fin(part(pexp, jnp.sum), jnp.sum)
                acc_ref[h] = alpha * acc_ref[h] + jnp.dot(vt2, pexp.astype(BF16),
                                                          preferred_element_type=F32)
                m_ref[h] = m_new

    stage_logits(0, 0)

    def att_pair(pp, _):
        c = 2 * pp
        stage_logits(c + 1, 1)
        stage_values(c, 0)
        stage_logits(c + 2, 0)
        stage_values(c + 1, 1)
        return 0

    lax.fori_loop(0, nck // 2, att_pair, 0)

    @pl.when(lax.rem(nck, 2) == 1)
    def _last_chunk():
        stage_values(nck - 1, 0)

    lo_rows = lax.broadcasted_iota(I32, (LANES, blk), 0) < HALF
    for p in range(N_PAIRS):
        oa = acc_ref[2 * p] / l_ref[2 * p]
        ob = acc_ref[2 * p + 1] / l_ref[2 * p + 1]
        o_ref[0, :, p * LANES:(p + 1) * LANES] = jnp.where(lo_rows, oa, ob).T.astype(o_ref.dtype)


def _dsa(proj3, vt4, rel_bias):
    B, S, _ = proj3.shape
    blk = DSA_BLK
    n_off = _dsa_n_off(blk)
    n_sel = min(TOPK_MAX, S // 4)
    bucket = jnp.asarray(_dsa_bucket_tiles(blk))
    n_chunks = S // blk
    assert S % blk == 0
    kern = functools.partial(_dsa_kernel, blk=blk, n_chunks=n_chunks, n_sel=n_sel, n_off=n_off)
    return pl.pallas_call(
        kern,
        grid=(B, S // blk),
        in_specs=[
            pl.BlockSpec((n_off, blk, blk), lambda b, i: (0, 0, 0)),
            pl.BlockSpec(memory_space=pltpu.SMEM),
            pl.BlockSpec((1, blk, ATT_W), lambda b, i: (b, i, COL_QA // ATT_W)),
            pl.BlockSpec((1, S, ATT_W), lambda b, i: (b, 0, COL_KA // ATT_W)),
            pl.BlockSpec((1, S // blk, ATT_W, blk), lambda b, i: (b, 0, 0, 0)),
            pl.BlockSpec((1, blk, ATT_W), lambda b, i: (b, i, COL_QI // ATT_W)),
            pl.BlockSpec((1, S, LANES), lambda b, i: (b, 0, COL_KK // LANES)),
            pl.BlockSpec((1, blk, LANES), lambda b, i: (b, i, COL_WI // LANES)),
        ],
        out_specs=pl.BlockSpec((1, blk, ATT_W), lambda b, i: (b, i, 0)),
        out_shape=jax.ShapeDtypeStruct((B, S, ATT_W), BF16),
        scratch_shapes=[
            pltpu.VMEM((S // blk, blk, blk), F32),
            pltpu.VMEM((A_HEADS, n_off, blk, blk), F32),
            pltpu.VMEM((A_HEADS, 1, blk), F32),
            pltpu.VMEM((A_HEADS, 1, blk), F32),
            pltpu.VMEM((A_HEADS, LANES, blk), F32),
            pltpu.VMEM((2, A_HEADS, blk, blk), F32),
            pltpu.VMEM((2, A_HEADS, 1, blk), F32),
        ],
        compiler_params=_params(("arbitrary", "arbitrary")),
    )(bucket, rel_bias, proj3, proj3, vt4, proj3, proj3, proj3)


def _sb_kernel(q_ref, k_ref, v_ref, o_ref, hl_ref, z_ref, *, t):
    i = pl.program_id(2)
    n = i + 1
    lane = lax.broadcasted_iota(I32, (t, LANES), 1)
    lo_half = lane < HALF
    q_m = []
    for pr in range(SB_PAIRS):
        q2 = q_ref[0, :, pr * LANES:(pr + 1) * LANES].astype(F32) * QK_SCALE
        q_m += [jnp.where(lo_half, q2, 0.0).astype(BF16), jnp.where(lo_half, 0.0, q2).astype(BF16)]
    heads = 2 * SB_PAIRS
    r = lax.broadcasted_iota(I32, (t, t), 0)
    cidx = lax.broadcasted_iota(I32, (t, t), 1)
    neg_from = jnp.where(r >= cidx, -1.0, 0.0).astype(BF16)
    neg_from2 = jnp.concatenate([neg_from, neg_from], axis=0)
    diff = cidx - r

    def stage_terms(step, slot, diagonal):
        c0 = pl.multiple_of(jnp.maximum(i - step, 0) * t, t)
        if diagonal:
            keep = diff < 0
        for hh in range(heads):
            pr = hh // 2
            k2 = k_ref[0, pl.ds(c0, t), pr * LANES:(pr + 1) * LANES]
            z = lax.dot_general(q_m[hh], k2, NT_DIMS, preferred_element_type=F32)
            sp = jnp.maximum(z, 0.0) + jnp.log(1.0 + jnp.exp(-jnp.abs(z)))
            if diagonal:
                sp = jnp.where(keep, sp, 0.0)
                z = jnp.where(keep, z, NEG)
            hi = sp.astype(BF16)
            hl_ref[slot, hh, :, :t] = hi
            hl_ref[slot, hh, :, t:] = (sp - hi.astype(F32)).astype(BF16)
            z_ref[slot, hh] = z

    def stage_apply(step, slot, carry):
        c0 = pl.multiple_of((i - step) * t, t)
        cum_all = jnp.dot(hl_ref[slot].reshape(heads * t, 2 * t), neg_from2, preferred_element_type=F32)
        out = []
        for hh in range(heads):
            pr = hh // 2
            v2 = v_ref[0, pl.ds(c0, t), pr * LANES:(pr + 1) * LANES]
            car, acc = carry[hh]
            cum = cum_all[hh * t:(hh + 1) * t]
            w = jnp.exp(z_ref[slot, hh] + cum + car)
            acc = acc + jnp.dot(w.astype(BF16), v2, preferred_element_type=F32)
            out.append((car + cum[:, 0:1], acc))
        return tuple(out)

    z1 = jnp.zeros((t, 1), F32)
    za = jnp.zeros((t, LANES), F32)
    stage_terms(0, 0, True)

    def pair_body(pp, carry):
        step = 2 * pp
        stage_terms(step + 1, 1, False)
        carry = stage_apply(step, 0, carry)
        stage_terms(step + 2, 0, False)
        return stage_apply(step + 1, 1, carry)

    carry = lax.fori_loop(0, n // 2, pair_body, ((z1, za),) * heads)
    carry = lax.cond(lax.rem(n, 2) == 1, lambda c: stage_apply(n - 1, 0, c), lambda c: c, carry)
    for pr in range(SB_PAIRS):
        o_ref[0, :, pr * LANES:(pr + 1) * LANES] = jnp.where(
            lo_half, carry[2 * pr][1], carry[2 * pr + 1][1]).astype(o_ref.dtype)


def _stick_breaking(proj3):
    B, S, _ = proj3.shape
    t = SB_T
    w = SB_PAIRS * LANES
    qb, kb, vb = COL_QB // w, COL_KB // w, COL_VB // w
    return pl.pallas_call(
        functools.partial(_sb_kernel, t=t),
        grid=(B, N_PAIRS // SB_PAIRS, S // t),
        in_specs=[
            pl.BlockSpec((1, t, w), lambda b, p, i: (b, i, qb + p)),
            pl.BlockSpec((1, S, w), lambda b, p, i: (b, 0, kb + p)),
            pl.BlockSpec((1, S, w), lambda b, p, i: (b, 0, vb + p)),
        ],
        out_specs=pl.BlockSpec((1, t, w), lambda b, p, i: (b, i, p)),
        out_shape=jax.ShapeDtypeStruct((B, S, ATT_W), BF16),
        scratch_shapes=[
            pltpu.VMEM((2, 2 * SB_PAIRS, t, 2 * t), BF16),
            pltpu.VMEM((2, 2 * SB_PAIRS, t, t), F32),
        ],
        compiler_params=_params(("arbitrary", "arbitrary", "arbitrary")),
    )(proj3, proj3, proj3)


def _layer_norm(r, g, b):
    mu = jnp.mean(r, axis=-1, keepdims=True)
    d = r - mu
    var = jnp.mean(d * d, axis=-1, keepdims=True)
    return d * lax.rsqrt(var + LN_EPS) * g + b


def _split_bf16(v):
    hi = v.astype(BF16)
    return hi, (v - hi.astype(F32)).astype(BF16)


def _pack_bf16_pairs(v):
    half = v.shape[1] // 2
    bits = pltpu.bitcast(v.astype(BF16).astype(F32), jnp.uint32)
    return bits[:, :half] | (bits[:, half:] >> 16)


def _unpack_bf16_pairs(w):
    return jnp.concatenate([pltpu.bitcast(w & jnp.uint32(0xFFFF0000), F32),
                            pltpu.bitcast(w << 16, F32)], axis=1)


def _merge_kernel(x_ref, ya_ref, yb_ref, wg_ref, wa_ref, wb_ref, wo_ref, g_ref, b_ref,
                  wr_ref, br_ref, h_ref, hp_ref, e_ref, p_ref, *, alpha, d):
    pa = jnp.dot(ya_ref[...], wa_ref[...], preferred_element_type=F32)
    pb = jnp.dot(yb_ref[...], wb_ref[...], preferred_element_type=F32)
    gates = jnp.dot(x_ref[...].astype(BF16), wg_ref[...], preferred_element_type=F32)
    merged = jax.nn.sigmoid(gates[:, :d]) * pa + jax.nn.sigmoid(gates[:, d:]) * pb
    m = jnp.dot(merged.astype(BF16), wo_ref[...], preferred_element_type=F32)
    h = _layer_norm(alpha * x_ref[...] + m, g_ref[...], b_ref[...])
    h_ref[...] = h
    hp_ref[...] = _pack_bf16_pairs(h)

    h_hi, h_lo = _split_bf16(h)
    w_hi, w_lo = _split_bf16(wr_ref[...])
    logit = (lax.dot_general(w_hi, h_hi, NT_DIMS, preferred_element_type=F32)
             + lax.dot_general(w_hi, h_lo, NT_DIMS, preferred_element_type=F32)
             + lax.dot_general(w_lo, h_hi, NT_DIMS, preferred_element_type=F32)) + br_ref[...]
    eid = lax.broadcasted_iota(I32, logit.shape, 0)
    vals, ids = [], []
    for _ in range(TOP_K):
        mx = jnp.max(logit, axis=0, keepdims=True)
        am = jnp.min(jnp.where(logit == mx, eid, N_EXPERTS), axis=0, keepdims=True)
        vals.append(mx)
        ids.append(am)
        logit = jnp.where(eid == am, -jnp.inf, logit)
    ex = [jnp.exp(v - vals[0]) for v in vals]
    den = ex[0] + ex[1] + ex[2] + ex[3]
    for k in range(TOP_K):
        e_ref[k:k + 1, :] = ids[k]
        p_ref[k:k + 1, :] = ex[k] / den


def _merge(x2, ya, yb, wg, wa, wb, wo, g, b, wr_t, br, alpha):
    T, D = x2.shape
    tm = MERGE_TM
    row = lambda i: (i, 0)
    fixed = lambda i: (0, 0)
    return pl.pallas_call(
        functools.partial(_merge_kernel, alpha=alpha, d=D),
        grid=(T // tm,),
        in_specs=[
            pl.BlockSpec((tm, D), row),
            pl.BlockSpec((tm, ATT_W), row),
            pl.BlockSpec((tm, ATT_W), row),
            pl.BlockSpec((D, 2 * D), fixed),
            pl.BlockSpec((ATT_W, D), fixed),
            pl.BlockSpec((ATT_W, D), fixed),
            pl.BlockSpec((D, D), fixed),
            pl.BlockSpec((1, D), fixed),
            pl.BlockSpec((1, D), fixed),
            pl.BlockSpec((N_EXPERTS, D), fixed),
            pl.BlockSpec((N_EXPERTS, 1), fixed),
        ],
        out_specs=[
            pl.BlockSpec((tm, D), row),
            pl.BlockSpec((tm, D // 2), row),
            pl.BlockSpec((TOP_K, tm), lambda i: (0, i)),
            pl.BlockSpec((TOP_K, tm), lambda i: (0, i)),
        ],
        out_shape=[
            jax.ShapeDtypeStruct((T, D), F32),
            jax.ShapeDtypeStruct((T, D // 2), jnp.uint32),
            jax.ShapeDtypeStruct((TOP_K, T), I32),
            jax.ShapeDtypeStruct((TOP_K, T), F32),
        ],
        compiler_params=_params(("arbitrary",)),
    )(x2, ya, yb, wg, wa, wb, wo, g, b, wr_t, br)


def _sc_gather_rows(table, idx):
    n = idx.shape[0]
    d = table.shape[1]
    info = plsc.get_sparse_core_info()
    n_cores, n_sub = info.num_cores, info.num_subcores
    per_w = n // (n_cores * n_sub)
    c = SC_GATHER_BYTES // (d * table.dtype.itemsize)
    n_g = per_w // c
    assert n == per_w * n_cores * n_sub and per_w == n_g * c and n_g % 2 == 0 and n_g >= 2
    mesh = plsc.VectorSubcoreMesh(core_axis_name="c", subcore_axis_name="s")

    @functools.partial(
        pl.kernel, mesh=mesh, out_type=jax.ShapeDtypeStruct((n, d), table.dtype),
        scratch_types=[pltpu.VMEM((per_w,), I32), pltpu.VMEM((2, c, d), table.dtype),
                       pltpu.SemaphoreType.DMA((2,)), pltpu.SemaphoreType.DMA((2,))])
    def gather_kernel(table_hbm, idx_hbm, out_hbm, idx_v, rows_v, gsem, wsem):
        base = (lax.axis_index("s") * n_cores + lax.axis_index("c")) * per_w
        pltpu.sync_copy(idx_hbm.at[pl.ds(base, per_w)], idx_v)

        def gather(g, b):
            return pltpu.make_async_copy(table_hbm.at[idx_v.at[pl.ds(g * c, c)]], rows_v.at[b],
                                         gsem.at[b])

        def write(g, b):
            return pltpu.make_async_copy(rows_v.at[b], out_hbm.at[pl.ds(base + g * c, c)], wsem.at[b])

        gather(0, 0).start()

        @pl.loop(0, n_g, step=2)
        def _ring(g0):
            for b in range(2):
                g = g0 + b

                @pl.when(g + 1 < n_g)
                def _next():
                    @pl.when(g >= 1)
                    def _buffer_free():
                        write(g - 1, 1 - b).wait()
                    gather(g + 1, 1 - b).start()

                gather(g, b).wait()
                write(g, b).start()

        write(n_g - 2, 0).wait()
        write(n_g - 1, 1).wait()

    return gather_kernel(table, idx)


def _moe_kernel(blk_e_ref, nused_ref, x_ref, wgu_ref, bgu_ref, wdn_ref, bdn_ref, o_ref,
                wgu_s, wdn_s, *, f):
    i = pl.program_id(0)
    nused = nused_ref[0]

    @pl.when(i < nused)
    def _compute():
        changed = jnp.logical_or(i == 0, blk_e_ref[i] != blk_e_ref[jnp.maximum(i - 1, 0)])

        @pl.when(changed)
        def _cast_weights():
            wgu_s[...] = wgu_ref[0].astype(BF16)
            wdn_s[...] = wdn_ref[0].astype(BF16)

        x = _unpack_bf16_pairs(x_ref[...]).astype(BF16)
        hgu = jnp.dot(x, wgu_s[...], preferred_element_type=F32) + bgu_ref[0]
        a = jnp.minimum(hgu[:, :f], SWIGLU_LIMIT)
        u = jnp.clip(hgu[:, f:], -SWIGLU_LIMIT, SWIGLU_LIMIT)
        glu = a * jax.nn.sigmoid(a * SWIGLU_ALPHA)
        y = jnp.dot(((u + 1.0) * glu).astype(BF16), wdn_s[...], preferred_element_type=F32) + bdn_ref[0]
        o_ref[...] = _pack_bf16_pairs(y)

    @pl.when(i >= nused)
    def _unused_block():
        o_ref[...] = jnp.zeros(o_ref.shape, o_ref.dtype)


def _moe_ffn(xs, blk_e, nused, w_gu, b_gu, w_dn, b_dn):
    P = xs.shape[0]
    E, D, F2 = w_gu.shape
    f = F2 // 2
    blk = MOE_BLK
    nb = P // blk
    used_block = lambda i, be, nu: (jnp.minimum(i, nu[0] - 1), 0)
    grid_spec = pltpu.PrefetchScalarGridSpec(
        num_scalar_prefetch=2,
        grid=(nb,),
        in_specs=[
            pl.BlockSpec((blk, D // 2), used_block),
            pl.BlockSpec((1, D, F2), lambda i, be, nu: (be[i], 0, 0)),
            pl.BlockSpec((1, 1, F2), lambda i, be, nu: (be[i], 0, 0)),
            pl.BlockSpec((1, f, D), lambda i, be, nu: (be[i], 0, 0)),
            pl.BlockSpec((1, 1, D), lambda i, be, nu: (be[i], 0, 0)),
        ],
        out_specs=pl.BlockSpec((blk, D // 2), lambda i, be, nu: (i, 0)),
        scratch_shapes=[
            pltpu.VMEM((D, F2), BF16),
            pltpu.VMEM((f, D), BF16),
        ],
    )
    return pl.pallas_call(
        functools.partial(_moe_kernel, f=f),
        grid_spec=grid_spec,
        out_shape=jax.ShapeDtypeStruct((P, D // 2), jnp.uint32),
        compiler_params=_params(("arbitrary",)),
    )(blk_e, nused, xs, w_gu, b_gu.reshape(E, 1, F2), w_dn, b_dn.reshape(E, 1, D))


def _comb_kernel(h_ref, p_ref, y_ref, g_ref, b_ref, o_ref, *, alpha):
    gate = p_ref[...]
    y = [_unpack_bf16_pairs(y_ref[k]) for k in range(TOP_K)]
    fsum = (y[0] * gate[:, 0:1] + y[1] * gate[:, 1:2]) + (y[2] * gate[:, 2:3] + y[3] * gate[:, 3:4])
    o_ref[...] = _layer_norm(alpha * h_ref[...] + fsum, g_ref[...], b_ref[...])


def _combine(h, y4, top_p, g, b, alpha):
    T, D = h.shape
    tm = COMB_TM
    return pl.pallas_call(
        functools.partial(_comb_kernel, alpha=alpha),
        grid=(T // tm,),
        in_specs=[
            pl.BlockSpec((tm, D), lambda i: (i, 0)),
            pl.BlockSpec((tm, TOP_K), lambda i: (i, 0)),
            pl.BlockSpec((TOP_K, tm, D // 2), lambda i: (0, i, 0)),
            pl.BlockSpec((1, D), lambda i: (0, 0)),
            pl.BlockSpec((1, D), lambda i: (0, 0)),
        ],
        out_specs=pl.BlockSpec((tm, D), lambda i: (i, 0)),
        out_shape=jax.ShapeDtypeStruct((T, D), F32),
        compiler_params=_params(("arbitrary",)),
    )(h, top_p.T, y4, g, b)


def _route(top_e, blk):
    K, T = top_e.shape
    N = K * T
    flat_e = top_e.reshape(N)
    experts = jnp.arange(N_EXPERTS, dtype=I32)
    order = jnp.argsort(flat_e, stable=True).astype(I32)
    inv = jnp.argsort(order).astype(I32)
    onehot = flat_e[:, None] == experts[None, :]
    counts = jnp.sum(onehot, axis=0, dtype=I32)
    padded = (counts + blk - 1) // blk * blk
    pends = jnp.cumsum(padded)
    offs = jnp.cumsum(counts) - counts
    shift = (pends - padded) - offs
    pos = inv + jnp.sum(jnp.where(onehot, shift[None, :], 0), axis=1, dtype=I32)
    P = N + N_EXPERTS * blk
    nb = P // blk
    blk_start = jnp.arange(nb, dtype=I32) * blk
    blk_e = jnp.minimum(jnp.sum(pends[None, :] <= blk_start[:, None], axis=1, dtype=I32), N_EXPERTS - 1)
    j = (blk_start - shift[blk_e])[:, None] + jnp.arange(blk, dtype=I32)[None, :]
    valid = j < (offs + counts)[blk_e][:, None]
    src = order[jnp.clip(j, 0, N - 1)]
    row_tok = jnp.where(valid, src % T, j % T).reshape(P)
    nused = (pends[-1:] // blk).astype(I32)
    return blk_e, nused, row_tok, pos.reshape(K, T)


def _projection_weights(w_in_l):
    sizes = (ATT_W, ATT_W, ATT_W, IDX_HEADS * IDX_DIM, IDX_DIM, IDX_HEADS, ATT_W, ATT_W, ATT_W)
    offs = np.concatenate([[0], np.cumsum(sizes)])
    qa, ka, va, qi, ki, wi, qb, kb, vb = (w_in_l[:, offs[n]:offs[n + 1]] for n in range(9))
    pad_wi = jnp.zeros((w_in_l.shape[0], LANES - IDX_HEADS), w_in_l.dtype)
    w_att = jnp.concatenate([qa, ka, qi, qb, kb, vb, ki, ki, wi, pad_wi], axis=1).astype(BF16)
    w_va_t = va.T.astype(BF16)
    w_gate = w_in_l[:, offs[9]:].astype(BF16)
    return w_att, w_va_t, w_gate


def kernel(x, w_in, w_branch_a, w_branch_b, w_out, rel_bias, ln1_g, ln1_b, w_router, b_router,
           w_gate_up, b_gate_up, w_down, b_down, ln2_g, ln2_b):
    B, S, D = x.shape
    depth = w_in.shape[0]
    alpha = (2 * depth) ** 0.25
    T = B * S
    h = x.reshape(T, D)
    for l in range(depth):
        w_att, w_va_t, w_gate = _projection_weights(w_in[l])
        proj, vt = _projection(h, w_att, w_va_t, min(T, PROJ_TM), PROJ_TN, DSA_BLK)
        proj = proj.reshape(B, S, ATT_COLS)
        vt = vt.reshape(B, S // DSA_BLK, ATT_W, DSA_BLK)
        ya = _dsa(proj, vt, rel_bias).reshape(T, ATT_W)
        yb = _stick_breaking(proj).reshape(T, ATT_W)
        h1, h1_packed, top_e, top_p = _merge(
            h, ya, yb, w_gate, w_branch_a[l].astype(BF16), w_branch_b[l].astype(BF16),
            w_out[l].astype(BF16), ln1_g[l].reshape(1, D), ln1_b[l].reshape(1, D),
            w_router[l].T, b_router[l].reshape(N_EXPERTS, 1), alpha)
        blk_e, nused, row_tok, pos = _route(top_e, MOE_BLK)
        xs = _sc_gather_rows(h1_packed, row_tok)
        ys = _moe_ffn(xs, blk_e, nused, w_gate_up[l], b_gate_up[l], w_down[l], b_down[l])
        y4 = _sc_gather_rows(ys, pos.reshape(TOP_K * T)).reshape(TOP_K, T, D // 2)
        h = _combine(h1, y4, top_p, ln2_g[l].reshape(1, D), ln2_b[l].reshape(1, D), alpha)
    return h.reshape(B, S, D)
```

```python
import functools
import math

import numpy as np
import jax
import jax.numpy as jnp
from jax import lax
from jax.experimental import pallas as pl
from jax.experimental.pallas import tpu as pltpu
from jax.experimental.pallas import tpu_sc as plsc

F32 = jnp.float32
BF16 = jnp.bfloat16
I32 = jnp.int32

A_HEADS = 8
HEAD_DIM = 64
ATT_W = A_HEADS * HEAD_DIM
IDX_HEADS = 8
IDX_DIM = 64
IDX_SCALE = (IDX_HEADS * IDX_DIM) ** -0.5
TOPK_MAX = 256
N_BUCKETS = 32
MAX_DISTANCE = 128
N_EXPERTS = 32
TOP_K = 4
SWIGLU_LIMIT = 7.0
SWIGLU_ALPHA = 1.702
LN_EPS = 1e-5
QK_SCALE = HEAD_DIM ** -0.5

LANES = 128
SUBLANES = 8
HALF = LANES // 2
N_PAIRS = A_HEADS // 2
VMEM_LIMIT = 56 * 1024 * 1024

DSA_BLK = 256
SB_T = 256
SB_PAIRS = 4
PROJ_TM = 1024
MERGE_TM = 512
MOE_BLK = 512
COMB_TM = 512
SC_GATHER_BYTES = 128 * 1024
REDUCE_CHAINS = 8
BISECT_CAP = 24
BISECT_FREE = 20
NEG = -1e30

COL_QA, COL_KA, COL_QI, COL_QB, COL_KB, COL_VB = (g * ATT_W for g in range(6))
COL_KK = 6 * ATT_W
COL_WI = COL_KK + LANES
ATT_COLS = COL_WI + LANES
PROJ_TN = ATT_COLS // 2

NT_DIMS = (((1,), (1,)), ((), ()))


def _params(sem, vmem=VMEM_LIMIT):
    return pltpu.CompilerParams(dimension_semantics=sem, vmem_limit_bytes=vmem)


def _proj_kernel(x_ref, w_ref, wt_ref, o_ref, ot_ref, xb_ref, *, tt):
    @pl.when(pl.program_id(1) == 0)
    def _row_tile_start():
        xb_ref[...] = x_ref[...].astype(BF16)
        for r in range(ot_ref.shape[0]):
            ot_ref[r] = lax.dot_general(wt_ref[...], xb_ref[r * tt:(r + 1) * tt, :], NT_DIMS,
                                        preferred_element_type=F32).astype(ot_ref.dtype)

    o_ref[...] = jnp.dot(xb_ref[...], w_ref[...], preferred_element_type=F32).astype(o_ref.dtype)


def _projection(x, w, w_t, tm, tn, tt):
    M, K = x.shape
    N = w.shape[1]
    Nt = w_t.shape[0]
    return pl.pallas_call(
        functools.partial(_proj_kernel, tt=tt),
        grid=(M // tm, N // tn),
        in_specs=[pl.BlockSpec((tm, K), lambda i, j: (i, 0)),
                  pl.BlockSpec((K, tn), lambda i, j: (0, j)),
                  pl.BlockSpec((Nt, K), lambda i, j: (0, 0))],
        out_specs=[pl.BlockSpec((tm, tn), lambda i, j: (i, j)),
                   pl.BlockSpec((tm // tt, Nt, tt), lambda i, j: (i, 0, 0))],
        out_shape=[jax.ShapeDtypeStruct((M, N), BF16),
                   jax.ShapeDtypeStruct((M // tt, Nt, tt), BF16)],
        scratch_shapes=[pltpu.VMEM((tm, K), BF16)],
        compiler_params=_params(("arbitrary", "arbitrary")),
    )(x, w, w_t)


def _t5_bucket_np(n):
    n = np.maximum(n, 0)
    max_exact = N_BUCKETS // 2
    nf = np.maximum(n, 1).astype(np.float32)
    large = max_exact + (np.log(nf / max_exact) / math.log(MAX_DISTANCE / max_exact)
                         * (N_BUCKETS - max_exact)).astype(np.int32)
    large = np.minimum(large, N_BUCKETS - 1)
    return np.where(n < max_exact, n, large).astype(np.int32)


def _dsa_n_off(blk):
    return 2 + -(-MAX_DISTANCE // blk)


def _dsa_bucket_tiles(blk):
    n_off = _dsa_n_off(blk)
    j = np.arange(blk)[None, :, None]
    i = np.arange(blk)[None, None, :]
    o = np.arange(n_off)[:, None, None]
    return _t5_bucket_np(i - j + blk * (n_off - 1 - o))


def _dsa_kernel(bucket_ref, relb_ref, q_ref, k_ref, vt_ref, qi_ref, kk_ref, wi_ref, o_ref,
                sc_ref, bias_ref, m_ref, l_ref, acc_ref, st_ref, mx_ref,
                *, blk, n_chunks, n_sel, n_off):
    b = pl.program_id(0)
    i = pl.program_id(1)
    q0 = i * blk
    nck = i + 1
    groups = blk // SUBLANES

    @pl.when(jnp.logical_and(b == 0, i == 0))
    def _build_bias():
        def head_body(h, _):
            for o in range(n_off):
                for rb in range(blk // LANES):
                    for cb in range(blk // LANES):
                        rs = slice(rb * LANES, (rb + 1) * LANES)
                        cs = slice(cb * LANES, (cb + 1) * LANES)
                        bk = bucket_ref[o, rs, cs]

                        def bucket_body(n, acc):
                            return jnp.where(bk == n, relb_ref[n, h], acc)

                        bias_ref[h, o, rs, cs] = lax.fori_loop(
                            0, N_BUCKETS, bucket_body, jnp.zeros((LANES, LANES), F32))
            return 0

        lax.fori_loop(0, A_HEADS, head_body, 0)

    lane = lax.broadcasted_iota(I32, (blk, LANES), 1)
    lo_half = lane < HALF
    krow = lax.broadcasted_iota(I32, (blk, blk), 0)
    qpos = q0 + lax.broadcasted_iota(I32, (1, blk), 1)

    def pair_split(ref, scale):
        out = []
        for p in range(N_PAIRS):
            v = ref[0, :, p * LANES:(p + 1) * LANES].astype(F32)
            if scale != 1.0:
                v = v * scale
            out.append(jnp.where(lo_half, v, 0.0).astype(BF16))
            out.append(jnp.where(lo_half, 0.0, v).astype(BF16))
        return out

    wi_t = wi_ref[0].astype(F32).T
    wrow = [wi_t[h:h + 1, :] * IDX_SCALE for h in range(IDX_HEADS)]
    qi_m = pair_split(qi_ref, 1.0)

    def part(x, op):
        y = op(x.reshape(REDUCE_CHAINS, groups // REDUCE_CHAINS, SUBLANES, blk), axis=1)
        return op(y, axis=0)

    def fin(x, op):
        return op(x, axis=0, keepdims=True)

    zeros8 = jnp.zeros((SUBLANES, blk), F32)
    pinf8 = jnp.full((SUBLANES, blk), jnp.inf, F32)

    def score_chunk(c, mn_mx):
        c0 = pl.multiple_of(c * blk, blk)
        kk = kk_ref[0, pl.ds(c0, blk), :]
        acc = jnp.zeros((blk, blk), F32)
        for h in range(IDX_HEADS):
            s = lax.dot_general(kk, qi_m[h], NT_DIMS, preferred_element_type=F32)
            acc = acc + wrow[h] * jnp.maximum(s, 0.0)
        causal = c0 + krow <= qpos
        sc_ref[c] = jnp.where(causal, acc, -jnp.inf)
        return (jnp.minimum(mn_mx[0], part(jnp.where(causal, acc, jnp.inf), jnp.min)),
                jnp.maximum(mn_mx[1], part(jnp.where(causal, acc, -jnp.inf), jnp.max)))

    mn, mx = lax.fori_loop(0, nck, score_chunk, (pinf8, -pinf8))
    rmin = fin(mn, jnp.min)
    rmax = fin(mx, jnp.max)

    kt = jnp.minimum(qpos + 1, n_sel).astype(F32)

    def fold(fn, init):
        def body(c, acc):
            return fn(acc, sc_ref[c])
        return lax.fori_loop(0, nck, body, init)

    def count_ge(th):
        return fin(fold(lambda a, s: a + part(jnp.where(s >= th, 1.0, 0.0), jnp.sum), zeros8),
                   jnp.sum)

    def bis_cond(st):
        it, lo, hi, clo = st
        return jnp.logical_and(it < BISECT_CAP, jnp.max(jnp.abs(clo - kt)) > 0.0)

    def halve(lo, hi, clo):
        mid = 0.5 * lo + 0.5 * hi
        c = count_ge(mid)
        active = clo != kt
        up = jnp.logical_and(active, c >= kt)
        dn = jnp.logical_and(active, c < kt)
        return jnp.where(up, mid, lo), jnp.where(dn, mid, hi), jnp.where(up, c, clo)

    def bis_body(st):
        it, lo, hi, clo = st
        return (it + 2,) + halve(*halve(lo, hi, clo))

    start = (rmin, rmax + jnp.maximum(1.0, jnp.abs(rmax) * 2.0 ** -20), (qpos + 1).astype(F32))
    start = lax.fori_loop(0, BISECT_FREE, lambda _, st: halve(*st), start)
    _, lo, _, clo = lax.while_loop(bis_cond, bis_body, (jnp.int32(BISECT_FREE),) + start)
    open_rows = jnp.max(jnp.abs(clo - kt)) > 0.0

    def stats(lo_):
        a_ = fin(fold(lambda a, s: jnp.minimum(a, part(jnp.where(s >= lo_, s, jnp.inf), jnp.min)),
                      pinf8), jnp.min)
        cg, ct, nx = fold(
            lambda a, s: (a[0] + part(jnp.where(s > a_, 1.0, 0.0), jnp.sum),
                               a[1] + part(jnp.where(s == a_, 1.0, 0.0), jnp.sum),
                               jnp.minimum(a[2], part(jnp.where(s > a_, s, jnp.inf), jnp.min))),
            (zeros8, zeros8, pinf8))
        return a_, fin(cg, jnp.sum), fin(ct, jnp.sum), fin(nx, jnp.min)

    def fin_cond(st):
        return st[0]

    def fin_body(st):
        _, lo_, _, _ = st
        a_, cgt_, nt_, nxt_ = stats(lo_)
        bad = cgt_ >= kt
        return (jnp.max(jnp.where(bad, 1.0, 0.0)) > 0.0, jnp.where(bad, nxt_, a_), cgt_, nt_)

    def exact_finish():
        _, a_, cgt, nties = lax.while_loop(fin_cond, fin_body, (jnp.bool_(True), lo, kt, kt))
        need_ = kt - cgt
        return a_, need_, jnp.max(jnp.where(nties > need_, 1.0, 0.0)) > 0.0

    a, need, excess = lax.cond(open_rows, exact_finish, lambda: (lo, kt, jnp.bool_(False)))

    def mask_plain():
        def body(c, _):
            sc_ref[c] = jnp.where(sc_ref[c] >= a, 0.0, NEG)
            return 0
        lax.fori_loop(0, nck, body, 0)

    def mask_ties():
        upto = (krow >= lax.broadcasted_iota(I32, (blk, blk), 1)).astype(BF16)

        def body(c, seen):
            s = sc_ref[c]
            tie = s == a
            rank = jnp.dot(upto, jnp.where(tie, 1.0, 0.0).astype(BF16),
                           preferred_element_type=F32) + seen
            sel = jnp.logical_or(s > a, jnp.logical_and(tie, rank <= need))
            sc_ref[c] = jnp.where(sel, 0.0, NEG)
            return rank[blk - 1:blk, :]

        lax.fori_loop(0, nck, body, jnp.zeros((1, blk), F32))

    lax.cond(excess, mask_ties, mask_plain)

    m_ref[...] = jnp.full(m_ref.shape, NEG, F32)
    l_ref[...] = jnp.zeros(l_ref.shape, F32)
    acc_ref[...] = jnp.zeros(acc_ref.shape, F32)
    q_m = pair_split(q_ref, QK_SCALE)

    def stage_logits(c, slot):
        c = jnp.minimum(c, n_chunks - 1)
        c0 = pl.multiple_of(c * blk, blk)
        madd = sc_ref[c]
        o_idx = jnp.clip(c - i + (n_off - 1), 0, n_off - 1)
        for p in range(N_PAIRS):
            k2 = k_ref[0, pl.ds(c0, blk), p * LANES:(p + 1) * LANES]
            for hh in range(2):
                h = 2 * p + hh
                s = lax.dot_general(k2, q_m[h], NT_DIMS, preferred_element_type=F32)
                s = s + bias_ref[h, o_idx] + madd
                st_ref[slot, h] = s
                mx_ref[slot, h] = fin(part(s, jnp.max), jnp.max)

    def stage_values(c, slot):
        for p in range(N_PAIRS):
            vt2 = vt_ref[0, c, p * LANES:(p + 1) * LANES, :]
            for hh in range(2):
                h = 2 * p + hh
                m_old = m_ref[h]
                m_new = jnp.maximum(m_old, mx_ref[slot, h])
                alpha = jnp.exp(m_old - m_new)
                pexp = jnp.exp(st_ref[slot, h] - m_new)
                l_ref[h] = alpha * l_ref[h] + fin(part(pexp, jnp.sum), jnp.sum)
                acc_ref[h] = alpha * acc_ref[h] + jnp.dot(vt2, pexp.astype(BF16),
                                                          preferred_element_type=F32)
                m_ref[h] = m_new

    stage_logits(0, 0)

    def att_pair(pp, _):
        c = 2 * pp
        stage_logits(c + 1, 1)
        stage_values(c, 0)
        stage_logits(c + 2, 0)
        stage_values(c + 1, 1)
        return 0

    lax.fori_loop(0, nck // 2, att_pair, 0)

    @pl.when(lax.rem(nck, 2) == 1)
    def _last_chunk():
        stage_values(nck - 1, 0)

    lo_rows = lax.broadcasted_iota(I32, (LANES, blk), 0) < HALF
    for p in range(N_PAIRS):
        oa = acc_ref[2 * p] / l_ref[2 * p]
        ob = acc_ref[2 * p + 1] / l_ref[2 * p + 1]
        o_ref[0, :, p * LANES:(p + 1) * LANES] = jnp.where(lo_rows, oa, ob).T.astype(o_ref.dtype)


def _dsa(proj3, vt4, rel_bias):
    B, S, _ = proj3.shape
    blk = DSA_BLK
    n_off = _dsa_n_off(blk)
    n_sel = min(TOPK_MAX, S // 4)
    bucket = jnp.asarray(_dsa_bucket_tiles(blk))
    n_chunks = S // blk
    assert S % blk == 0
    kern = functools.partial(_dsa_kernel, blk=blk, n_chunks=n_chunks, n_sel=n_sel, n_off=n_off)
    return pl.pallas_call(
        kern,
        grid=(B, S // blk),
        in_specs=[
            pl.BlockSpec((n_off, blk, blk), lambda b, i: (0, 0, 0)),
            pl.BlockSpec(memory_space=pltpu.SMEM),
            pl.BlockSpec((1, blk, ATT_W), lambda b, i: (b, i, COL_QA // ATT_W)),
            pl.BlockSpec((1, S, ATT_W), lambda b, i: (b, 0, COL_KA // ATT_W)),
            pl.BlockSpec((1, S // blk, ATT_W, blk), lambda b, i: (b, 0, 0, 0)),
            pl.BlockSpec((1, blk, ATT_W), lambda b, i: (b, i, COL_QI // ATT_W)),
            pl.BlockSpec((1, S, LANES), lambda b, i: (b, 0, COL_KK // LANES)),
            pl.BlockSpec((1, blk, LANES), lambda b, i: (b, i, COL_WI // LANES)),
        ],
        out_specs=pl.BlockSpec((1, blk, ATT_W), lambda b, i: (b, i, 0)),
        out_shape=jax.ShapeDtypeStruct((B, S, ATT_W), BF16),
        scratch_shapes=[
            pltpu.VMEM((S // blk, blk, blk), F32),
            pltpu.VMEM((A_HEADS, n_off, blk, blk), F32),
            pltpu.VMEM((A_HEADS, 1, blk), F32),
            pltpu.VMEM((A_HEADS, 1, blk), F32),
            pltpu.VMEM((A_HEADS, LANES, blk), F32),
            pltpu.VMEM((2, A_HEADS, blk, blk), F32),
            pltpu.VMEM((2, A_HEADS, 1, blk), F32),
        ],
        compiler_params=_params(("arbitrary", "arbitrary")),
    )(bucket, rel_bias, proj3, proj3, vt4, proj3, proj3, proj3)


def _sb_kernel(q_ref, k_ref, v_ref, o_ref, hl_ref, z_ref, *, t):
    i = pl.program_id(2)
    n = i + 1
    lane = lax.broadcasted_iota(I32, (t, LANES), 1)
    lo_half = lane < HALF
    q_m = []
    for pr in range(SB_PAIRS):
        q2 = q_ref[0, :, pr * LANES:(pr + 1) * LANES].astype(F32) * QK_SCALE
        q_m += [jnp.where(lo_half, q2, 0.0).astype(BF16), jnp.where(lo_half, 0.0, q2).astype(BF16)]
    heads = 2 * SB_PAIRS
    r = lax.broadcasted_iota(I32, (t, t), 0)
    cidx = lax.broadcasted_iota(I32, (t, t), 1)
    neg_from = jnp.where(r >= cidx, -1.0, 0.0).astype(BF16)
    neg_from2 = jnp.concatenate([neg_from, neg_from], axis=0)
    diff = cidx - r

    def stage_terms(step, slot, diagonal):
        c0 = pl.multiple_of(jnp.maximum(i - step, 0) * t, t)
        if diagonal:
            keep = diff < 0
        for hh in range(heads):
            pr = hh // 2
            k2 = k_ref[0, pl.ds(c0, t), pr * LANES:(pr + 1) * LANES]
            z = lax.dot_general(q_m[hh], k2, NT_DIMS, preferred_element_type=F32)
            sp = jnp.maximum(z, 0.0) + jnp.log(1.0 + jnp.exp(-jnp.abs(z)))
            if diagonal:
                sp = jnp.where(keep, sp, 0.0)
                z = jnp.where(keep, z, NEG)
            hi = sp.astype(BF16)
            hl_ref[slot, hh, :, :t] = hi
            hl_ref[slot, hh, :, t:] = (sp - hi.astype(F32)).astype(BF16)
            z_ref[slot, hh] = z

    def stage_apply(step, slot, carry):
        c0 = pl.multiple_of((i - step) * t, t)
        cum_all = jnp.dot(hl_ref[slot].reshape(heads * t, 2 * t), neg_from2, preferred_element_type=F32)
        out = []
        for hh in range(heads):
            pr = hh // 2
            v2 = v_ref[0, pl.ds(c0, t), pr * LANES:(pr + 1) * LANES]
            car, acc = carry[hh]
            cum = cum_all[hh * t:(hh + 1) * t]
            w = jnp.exp(z_ref[slot, hh] + cum + car)
            acc = acc + jnp.dot(w.astype(BF16), v2, preferred_element_type=F32)
            out.append((car + cum[:, 0:1], acc))
        return tuple(out)

    z1 = jnp.zeros((t, 1), F32)
    za = jnp.zeros((t, LANES), F32)
    stage_terms(0, 0, True)

    def pair_body(pp, carry):
        step = 2 * pp
        stage_terms(step + 1, 1, False)
        carry = stage_apply(step, 0, carry)
        stage_terms(step + 2, 0, False)
        return stage_apply(step + 1, 1, carry)

    carry = lax.fori_loop(0, n // 2, pair_body, ((z1, za),) * heads)
    carry = lax.cond(lax.rem(n, 2) == 1, lambda c: stage_apply(n - 1, 0, c), lambda c: c, carry)
    for pr in range(SB_PAIRS):
        o_ref[0, :, pr * LANES:(pr + 1) * LANES] = jnp.where(
            lo_half, carry[2 * pr][1], carry[2 * pr + 1][1]).astype(o_ref.dtype)


def _stick_breaking(proj3):
    B, S, _ = proj3.shape
    t = SB_T
    w = SB_PAIRS * LANES
    qb, kb, vb = COL_QB // w, COL_KB // w, COL_VB // w
    return pl.pallas_call(
        functools.partial(_sb_kernel, t=t),
        grid=(B, N_PAIRS // SB_PAIRS, S // t),
        in_specs=[
            pl.BlockSpec((1, t, w), lambda b, p, i: (b, i, qb + p)),
            pl.BlockSpec((1, S, w), lambda b, p, i: (b, 0, kb + p)),
            pl.BlockSpec((1, S, w), lambda b, p, i: (b, 0, vb + p)),
        ],
        out_specs=pl.BlockSpec((1, t, w), lambda b, p, i: (b, i, p)),
        out_shape=jax.ShapeDtypeStruct((B, S, ATT_W), BF16),
        scratch_shapes=[
            pltpu.VMEM((2, 2 * SB_PAIRS, t, 2 * t), BF16),
            pltpu.VMEM((2, 2 * SB_PAIRS, t, t), F32),
        ],
        compiler_params=_params(("arbitrary", "arbitrary", "arbitrary")),
    )(proj3, proj3, proj3)


def _layer_norm(r, g, b):
    mu = jnp.mean(r, axis=-1, keepdims=True)
    d = r - mu
    var = jnp.mean(d * d, axis=-1, keepdims=True)
    return d * lax.rsqrt(var + LN_EPS) * g + b


def _split_bf16(v):
    hi = v.astype(BF16)
    return hi, (v - hi.astype(F32)).astype(BF16)


def _pack_bf16_pairs(v):
    half = v.shape[1] // 2
    bits = pltpu.bitcast(v.astype(BF16).astype(F32), jnp.uint32)
    return bits[:, :half] | (bits[:, half:] >> 16)


def _unpack_bf16_pairs(w):
    return jnp.concatenate([pltpu.bitcast(w & jnp.uint32(0xFFFF0000), F32),
                            pltpu.bitcast(w << 16, F32)], axis=1)


def _merge_kernel(x_ref, ya_ref, yb_ref, wg_ref, wa_ref, wb_ref, wo_ref, g_ref, b_ref,
                  wr_ref, br_ref, h_ref, hp_ref, e_ref, p_ref, *, alpha, d):
    pa = jnp.dot(ya_ref[...], wa_ref[...], preferred_element_type=F32)
    pb = jnp.dot(yb_ref[...], wb_ref[...], preferred_element_type=F32)
    gates = jnp.dot(x_ref[...].astype(BF16), wg_ref[...], preferred_element_type=F32)
    merged = jax.nn.sigmoid(gates[:, :d]) * pa + jax.nn.sigmoid(gates[:, d:]) * pb
    m = jnp.dot(merged.astype(BF16), wo_ref[...], preferred_element_type=F32)
    h = _layer_norm(alpha * x_ref[...] + m, g_ref[...], b_ref[...])
    h_ref[...] = h
    hp_ref[...] = _pack_bf16_pairs(h)

    h_hi, h_lo = _split_bf16(h)
    w_hi, w_lo = _split_bf16(wr_ref[...])
    logit = (lax.dot_general(w_hi, h_hi, NT_DIMS, preferred_element_type=F32)
             + lax.dot_general(w_hi, h_lo, NT_DIMS, preferred_element_type=F32)
             + lax.dot_general(w_lo, h_hi, NT_DIMS, preferred_element_type=F32)) + br_ref[...]
    eid = lax.broadcasted_iota(I32, logit.shape, 0)
    vals, ids = [], []
    for _ in range(TOP_K):
        mx = jnp.max(logit, axis=0, keepdims=True)
        am = jnp.min(jnp.where(logit == mx, eid, N_EXPERTS), axis=0, keepdims=True)
        vals.append(mx)
        ids.append(am)
        logit = jnp.where(eid == am, -jnp.inf, logit)
    ex = [jnp.exp(v - vals[0]) for v in vals]
    den = ex[0] + ex[1] + ex[2] + ex[3]
    for k in range(TOP_K):
        e_ref[k:k + 1, :] = ids[k]
        p_ref[k:k + 1, :] = ex[k] / den


def _merge(x2, ya, yb, wg, wa, wb, wo, g, b, wr_t, br, alpha):
    T, D = x2.shape
    tm = MERGE_TM
    row = lambda i: (i, 0)
    fixed = lambda i: (0, 0)
    return pl.pallas_call(
        functools.partial(_merge_kernel, alpha=alpha, d=D),
        grid=(T // tm,),
        in_specs=[
            pl.BlockSpec((tm, D), row),
            pl.BlockSpec((tm, ATT_W), row),
            pl.BlockSpec((tm, ATT_W), row),
            pl.BlockSpec((D, 2 * D), fixed),
            pl.BlockSpec((ATT_W, D), fixed),
            pl.BlockSpec((ATT_W, D), fixed),
            pl.BlockSpec((D, D), fixed),
            pl.BlockSpec((1, D), fixed),
            pl.BlockSpec((1, D), fixed),
            pl.BlockSpec((N_EXPERTS, D), fixed),
            pl.BlockSpec((N_EXPERTS, 1), fixed),
        ],
        out_specs=[
            pl.BlockSpec((tm, D), row),
            pl.BlockSpec((tm, D // 2), row),
            pl.BlockSpec((TOP_K, tm), lambda i: (0, i)),
            pl.BlockSpec((TOP_K, tm), lambda i: (0, i)),
        ],
        out_shape=[
            jax.ShapeDtypeStruct((T, D), F32),
            jax.ShapeDtypeStruct((T, D // 2), jnp.uint32),
            jax.ShapeDtypeStruct((TOP_K, T), I32),
            jax.ShapeDtypeStruct((TOP_K, T), F32),
        ],
        compiler_params=_params(("arbitrary",)),
    )(x2, ya, yb, wg, wa, wb, wo, g, b, wr_t, br)


def _sc_gather_rows(table, idx):
    n = idx.shape[0]
    d = table.shape[1]
    info = plsc.get_sparse_core_info()
    n_cores, n_sub = info.num_cores, info.num_subcores
    per_w = n // (n_cores * n_sub)
    c = SC_GATHER_BYTES // (d * table.dtype.itemsize)
    n_g = per_w // c
    assert n == per_w * n_cores * n_sub and per_w == n_g * c and n_g % 2 == 0 and n_g >= 2
    mesh = plsc.VectorSubcoreMesh(core_axis_name="c", subcore_axis_name="s")

    @functools.partial(
        pl.kernel, mesh=mesh, out_type=jax.ShapeDtypeStruct((n, d), table.dtype),
        scratch_types=[pltpu.VMEM((per_w,), I32), pltpu.VMEM((2, c, d), table.dtype),
                       pltpu.SemaphoreType.DMA((2,)), pltpu.SemaphoreType.DMA((2,))])
    def gather_kernel(table_hbm, idx_hbm, out_hbm, idx_v, rows_v, gsem, wsem):
        base = (lax.axis_index("s") * n_cores + lax.axis_index("c")) * per_w
        pltpu.sync_copy(idx_hbm.at[pl.ds(base, per_w)], idx_v)

        def gather(g, b):
            return pltpu.make_async_copy(table_hbm.at[idx_v.at[pl.ds(g * c, c)]], rows_v.at[b],
                                         gsem.at[b])

        def write(g, b):
            return pltpu.make_async_copy(rows_v.at[b], out_hbm.at[pl.ds(base + g * c, c)], wsem.at[b])

        gather(0, 0).start()

        @pl.loop(0, n_g, step=2)
        def _ring(g0):
            for b in range(2):
                g = g0 + b

                @pl.when(g + 1 < n_g)
                def _next():
                    @pl.when(g >= 1)
                    def _buffer_free():
                        write(g - 1, 1 - b).wait()
                    gather(g + 1, 1 - b).start()

                gather(g, b).wait()
                write(g, b).start()

        write(n_g - 2, 0).wait()
        write(n_g - 1, 1).wait()

    return gather_kernel(table, idx)


def _moe_kernel(blk_e_ref, nused_ref, x_ref, wgu_ref, bgu_ref, wdn_ref, bdn_ref, o_ref,
                wgu_s, wdn_s, *, f):
    i = pl.program_id(0)
    nused = nused_ref[0]

    @pl.when(i < nused)
    def _compute():
        changed = jnp.logical_or(i == 0, blk_e_ref[i] != blk_e_ref[jnp.maximum(i - 1, 0)])

        @pl.when(changed)
        def _cast_weights():
            wgu_s[...] = wgu_ref[0].astype(BF16)
            wdn_s[...] = wdn_ref[0].astype(BF16)

        x = _unpack_bf16_pairs(x_ref[...]).astype(BF16)
        hgu = jnp.dot(x, wgu_s[...], preferred_element_type=F32) + bgu_ref[0]
        a = jnp.minimum(hgu[:, :f], SWIGLU_LIMIT)
        u = jnp.clip(hgu[:, f:], -SWIGLU_LIMIT, SWIGLU_LIMIT)
        glu = a * jax.nn.sigmoid(a * SWIGLU_ALPHA)
        y = jnp.dot(((u + 1.0) * glu).astype(BF16), wdn_s[...], preferred_element_type=F32) + bdn_ref[0]
        o_ref[...] = _pack_bf16_pairs(y)

    @pl.when(i >= nused)
    def _unused_block():
        o_ref[...] = jnp.zeros(o_ref.shape, o_ref.dtype)


def _moe_ffn(xs, blk_e, nused, w_gu, b_gu, w_dn, b_dn):
    P = xs.shape[0]
    E, D, F2 = w_gu.shape
    f = F2 // 2
    blk = MOE_BLK
    nb = P // blk
    used_block = lambda i, be, nu: (jnp.minimum(i, nu[0] - 1), 0)
    grid_spec = pltpu.PrefetchScalarGridSpec(
        num_scalar_prefetch=2,
        grid=(nb,),
        in_specs=[
            pl.BlockSpec((blk, D // 2), used_block),
            pl.BlockSpec((1, D, F2), lambda i, be, nu: (be[i], 0, 0)),
            pl.BlockSpec((1, 1, F2), lambda i, be, nu: (be[i], 0, 0)),
            pl.BlockSpec((1, f, D), lambda i, be, nu: (be[i], 0, 0)),
            pl.BlockSpec((1, 1, D), lambda i, be, nu: (be[i], 0, 0)),
        ],
        out_specs=pl.BlockSpec((blk, D // 2), lambda i, be, nu: (i, 0)),
        scratch_shapes=[
            pltpu.VMEM((D, F2), BF16),
            pltpu.VMEM((f, D), BF16),
        ],
    )
    return pl.pallas_call(
        functools.partial(_moe_kernel, f=f),
        grid_spec=grid_spec,
        out_shape=jax.ShapeDtypeStruct((P, D // 2), jnp.uint32),
        compiler_params=_params(("arbitrary",)),
    )(blk_e, nused, xs, w_gu, b_gu.reshape(E, 1, F2), w_dn, b_dn.reshape(E, 1, D))


def _comb_kernel(h_ref, p_ref, y_ref, g_ref, b_ref, o_ref, *, alpha):
    gate = p_ref[...]
    y = [_unpack_bf16_pairs(y_ref[k]) for k in range(TOP_K)]
    fsum = (y[0] * gate[:, 0:1] + y[1] * gate[:, 1:2]) + (y[2] * gate[:, 2:3] + y[3] * gate[:, 3:4])
    o_ref[...] = _layer_norm(alpha * h_ref[...] + fsum, g_ref[...], b_ref[...])


def _combine(h, y4, top_p, g, b, alpha):
    T, D = h.shape
    tm = COMB_TM
    return pl.pallas_call(
        functools.partial(_comb_kernel, alpha=alpha),
        grid=(T // tm,),
        in_specs=[
            pl.BlockSpec((tm, D), lambda i: (i, 0)),
            pl.BlockSpec((tm, TOP_K), lambda i: (i, 0)),
            pl.BlockSpec((TOP_K, tm, D // 2), lambda i: (0, i, 0)),
            pl.BlockSpec((1, D), lambda i: (0, 0)),
            pl.BlockSpec((1, D), lambda i: (0, 0)),
        ],
        out_specs=pl.BlockSpec((tm, D), lambda i: (i, 0)),
        out_shape=jax.ShapeDtypeStruct((T, D), F32),
        compiler_params=_params(("arbitrary",)),
    )(h, top_p.T, y4, g, b)


def _route(top_e, blk):
    K, T = top_e.shape
    N = K * T
    flat_e = top_e.reshape(N)
    experts = jnp.arange(N_EXPERTS, dtype=I32)
    assert N & (N - 1) == 0 and N_EXPERTS * N < 2 ** 31
    iota = jnp.arange(N, dtype=jnp.uint32)
    order_u = jnp.sort(flat_e.astype(jnp.uint32) * N + iota) & (N - 1)
    order = order_u.astype(I32)
    inv = (jnp.sort(order_u * N + iota) & (N - 1)).astype(I32)
    onehot = flat_e[:, None] == experts[None, :]
    counts = jnp.sum(onehot, axis=0, dtype=I32)
    padded = (counts + blk - 1) // blk * blk
    pends = jnp.cumsum(padded)
    offs = jnp.cumsum(counts) - counts
    shift = (pends - padded) - offs
    pos = inv + jnp.sum(jnp.where(onehot, shift[None, :], 0), axis=1, dtype=I32)
    P = N + N_EXPERTS * blk
    nb = P // blk
    blk_start = jnp.arange(nb, dtype=I32) * blk
    blk_e = jnp.minimum(jnp.sum(pends[None, :] <= blk_start[:, None], axis=1, dtype=I32), N_EXPERTS - 1)
    j = (blk_start - shift[blk_e])[:, None] + jnp.arange(blk, dtype=I32)[None, :]
    valid = j < (offs + counts)[blk_e][:, None]
    src = order[jnp.clip(j, 0, N - 1)]
    row_tok = jnp.where(valid, src % T, j % T).reshape(P)
    nused = (pends[-1:] // blk).astype(I32)
    return blk_e, nused, row_tok, pos.reshape(K, T)


def _projection_weights(w_in_l):
    sizes = (ATT_W, ATT_W, ATT_W, IDX_HEADS * IDX_DIM, IDX_DIM, IDX_HEADS, ATT_W, ATT_W, ATT_W)
    offs = np.concatenate([[0], np.cumsum(sizes)])
    qa, ka, va, qi, ki, wi, qb, kb, vb = (w_in_l[:, offs[n]:offs[n + 1]] for n in range(9))
    pad_wi = jnp.zeros((w_in_l.shape[0], LANES - IDX_HEADS), w_in_l.dtype)
    w_att = jnp.concatenate([qa, ka, qi, qb, kb, vb, ki, ki, wi, pad_wi], axis=1).astype(BF16)
    w_va_t = va.T.astype(BF16)
    w_gate = w_in_l[:, offs[9]:].astype(BF16)
    return w_att, w_va_t, w_gate


def kernel(x, w_in, w_branch_a, w_branch_b, w_out, rel_bias, ln1_g, ln1_b, w_router, b_router,
           w_gate_up, b_gate_up, w_down, b_down, ln2_g, ln2_b):
    B, S, D = x.shape
    depth = w_in.shape[0]
    alpha = (2 * depth) ** 0.25
    T = B * S
    h = x.reshape(T, D)
    for l in range(depth):
        w_att, w_va_t, w_gate = _projection_weights(w_in[l])
        proj, vt = _projection(h, w_att, w_va_t, min(T, PROJ_TM), PROJ_TN, DSA_BLK)
        proj = proj.reshape(B, S, ATT_COLS)
        vt = vt.reshape(B, S // DSA_BLK, ATT_W, DSA_BLK)
        ya = _dsa(proj, vt, rel_bias).reshape(T, ATT_W)
        yb = _stick_breaking(proj).reshape(T, ATT_W)
        h1, h1_packed, top_e, top_p = _merge(
            h, ya, yb, w_gate, w_branch_a[l].astype(BF16), w_branch_b[l].astype(BF16),
            w_out[l].astype(BF16), ln1_g[l].reshape(1, D), ln1_b[l].reshape(1, D),
            w_router[l].T, b_router[l].reshape(N_EXPERTS, 1), alpha)
        blk_e, nused, row_tok, pos = _route(top_e, MOE_BLK)
        xs = _sc_gather_rows(h1_packed, row_tok)
        ys = _moe_ffn(xs, blk_e, nused, w_gate_up[l], b_gate_up[l], w_down[l], b_down[l])
        y4 = _sc_gather_rows(ys, pos.reshape(TOP_K * T)).reshape(TOP_K, T, D // 2)
        h = _combine(h1, y4, top_p, ln2_g[l].reshape(1, D), ln2_b[l].reshape(1, D), alpha)
    return h.reshape(B, S, D)
```

```python
import functools
import math

import numpy as np
import jax
import jax.numpy as jnp
from jax import lax
from jax.experimental import pallas as pl
from jax.experimental.pallas import tpu as pltpu
from jax.experimental.pallas import tpu_sc as plsc

F32 = jnp.float32
BF16 = jnp.bfloat16
I32 = jnp.int32

A_HEADS = 8
HEAD_DIM = 64
ATT_W = A_HEADS * HEAD_DIM
IDX_HEADS = 8
IDX_DIM = 64
IDX_SCALE = (IDX_HEADS * IDX_DIM) ** -0.5
TOPK_MAX = 256
N_BUCKETS = 32
MAX_DISTANCE = 128
N_EXPERTS = 32
TOP_K = 4
SWIGLU_LIMIT = 7.0
SWIGLU_ALPHA = 1.702
LN_EPS = 1e-5
QK_SCALE = HEAD_DIM ** -0.5

LANES = 128
SUBLANES = 8
HALF = LANES // 2
N_PAIRS = A_HEADS // 2
VMEM_LIMIT = 56 * 1024 * 1024

DSA_BLK = 256
SB_T = 256
SB_PAIRS = 4
PROJ_TM = 1024
MERGE_TM = 512
MOE_BLK = 512
COMB_TM = 512
SC_GATHER_BYTES = 128 * 1024
REDUCE_CHAINS = 8
BISECT_CAP = 24
BISECT_FREE = 20
NEG = -1e30

COL_QA, COL_KA, COL_QI, COL_QB, COL_KB, COL_VB = (g * ATT_W for g in range(6))
COL_KK = 6 * ATT_W
COL_WI = COL_KK + LANES
ATT_COLS = COL_WI + LANES
PROJ_TN = ATT_COLS // 2

NT_DIMS = (((1,), (1,)), ((), ()))


def _params(sem, vmem=VMEM_LIMIT):
    return pltpu.CompilerParams(dimension_semantics=sem, vmem_limit_bytes=vmem)


def _proj_kernel(x_ref, w_ref, wt_ref, o_ref, ot_ref, xb_ref, *, tt):
    @pl.when(pl.program_id(1) == 0)
    def _row_tile_start():
        xb_ref[...] = x_ref[...].astype(BF16)
        for r in range(ot_ref.shape[0]):
            ot_ref[r] = lax.dot_general(wt_ref[...], xb_ref[r * tt:(r + 1) * tt, :], NT_DIMS,
                                        preferred_element_type=F32).astype(ot_ref.dtype)

    o_ref[...] = jnp.dot(xb_ref[...], w_ref[...], preferred_element_type=F32).astype(o_ref.dtype)


def _projection(x, w, w_t, tm, tn, tt):
    M, K = x.shape
    N = w.shape[1]
    Nt = w_t.shape[0]
    return pl.pallas_call(
        functools.partial(_proj_kernel, tt=tt),
        grid=(M // tm, N // tn),
        in_specs=[pl.BlockSpec((tm, K), lambda i, j: (i, 0)),
                  pl.BlockSpec((K, tn), lambda i, j: (0, j)),
                  pl.BlockSpec((Nt, K), lambda i, j: (0, 0))],
        out_specs=[pl.BlockSpec((tm, tn), lambda i, j: (i, j)),
                   pl.BlockSpec((tm // tt, Nt, tt), lambda i, j: (i, 0, 0))],
        out_shape=[jax.ShapeDtypeStruct((M, N), BF16),
                   jax.ShapeDtypeStruct((M // tt, Nt, tt), BF16)],
        scratch_shapes=[pltpu.VMEM((tm, K), BF16)],
        compiler_params=_params(("arbitrary", "arbitrary")),
    )(x, w, w_t)


def _t5_bucket_np(n):
    n = np.maximum(n, 0)
    max_exact = N_BUCKETS // 2
    nf = np.maximum(n, 1).astype(np.float32)
    large = max_exact + (np.log(nf / max_exact) / math.log(MAX_DISTANCE / max_exact)
                         * (N_BUCKETS - max_exact)).astype(np.int32)
    large = np.minimum(large, N_BUCKETS - 1)
    return np.where(n < max_exact, n, large).astype(np.int32)


def _dsa_n_off(blk):
    return 2 + -(-MAX_DISTANCE // blk)


def _dsa_bucket_tiles(blk):
    n_off = _dsa_n_off(blk)
    j = np.arange(blk)[None, :, None]
    i = np.arange(blk)[None, None, :]
    o = np.arange(n_off)[:, None, None]
    return _t5_bucket_np(i - j + blk * (n_off - 1 - o))


def _dsa_kernel(bucket_ref, relb_ref, q_ref, k_ref, vt_ref, qi_ref, kk_ref, wi_ref, o_ref,
                sc_ref, bias_ref, m_ref, l_ref, acc_ref, st_ref, mx_ref,
                *, blk, n_chunks, n_sel, n_off):
    b = pl.program_id(0)
    i = pl.program_id(1)
    q0 = i * blk
    nck = i + 1
    groups = blk // SUBLANES

    @pl.when(jnp.logical_and(b == 0, i == 0))
    def _build_bias():
        def head_body(h, _):
            for o in range(n_off):
                for rb in range(blk // LANES):
                    for cb in range(blk // LANES):
                        rs = slice(rb * LANES, (rb + 1) * LANES)
                        cs = slice(cb * LANES, (cb + 1) * LANES)
                        bk = bucket_ref[o, rs, cs]

                        def bucket_body(n, acc):
                            return jnp.where(bk == n, relb_ref[n, h], acc)

                        bias_ref[h, o, rs, cs] = lax.fori_loop(
                            0, N_BUCKETS, bucket_body, jnp.zeros((LANES, LANES), F32))
            return 0

        lax.fori_loop(0, A_HEADS, head_body, 0)

    lane = lax.broadcasted_iota(I32, (blk, LANES), 1)
    lo_half = lane < HALF
    krow = lax.broadcasted_iota(I32, (blk, blk), 0)
    qpos = q0 + lax.broadcasted_iota(I32, (1, blk), 1)

    def pair_split(ref, scale):
        out = []
        for p in range(N_PAIRS):
            v = ref[0, :, p * LANES:(p + 1) * LANES].astype(F32)
            if scale != 1.0:
                v = v * scale
            out.append(jnp.where(lo_half, v, 0.0).astype(BF16))
            out.append(jnp.where(lo_half, 0.0, v).astype(BF16))
        return out

    wi_t = wi_ref[0].astype(F32).T
    wrow = [wi_t[h:h + 1, :] * IDX_SCALE for h in range(IDX_HEADS)]
    qi_m = pair_split(qi_ref, 1.0)

    def part(x, op):
        y = op(x.reshape(REDUCE_CHAINS, groups // REDUCE_CHAINS, SUBLANES, blk), axis=1)
        return op(y, axis=0)

    def fin(x, op):
        return op(x, axis=0, keepdims=True)

    zeros8 = jnp.zeros((SUBLANES, blk), F32)
    pinf8 = jnp.full((SUBLANES, blk), jnp.inf, F32)

    def score_chunk(c, mn_mx):
        c0 = pl.multiple_of(c * blk, blk)
        kk = kk_ref[0, pl.ds(c0, blk), :]
        acc = jnp.zeros((blk, blk), F32)
        for h in range(IDX_HEADS):
            s = lax.dot_general(kk, qi_m[h], NT_DIMS, preferred_element_type=F32)
            acc = acc + wrow[h] * jnp.maximum(s, 0.0)
        causal = c0 + krow <= qpos
        sc_ref[c] = jnp.where(causal, acc, -jnp.inf)
        return (jnp.minimum(mn_mx[0], part(jnp.where(causal, acc, jnp.inf), jnp.min)),
                jnp.maximum(mn_mx[1], part(jnp.where(causal, acc, -jnp.inf), jnp.max)))

    mn, mx = lax.fori_loop(0, nck, score_chunk, (pinf8, -pinf8))
    rmin = fin(mn, jnp.min)
    rmax = fin(mx, jnp.max)

    kt = jnp.minimum(qpos + 1, n_sel).astype(F32)

    def fold(fn, init):
        def body(c, acc):
            return fn(acc, sc_ref[c])
        return lax.fori_loop(0, nck, body, init)

    def count_ge(th):
        return fin(fold(lambda a, s: a + part(jnp.where(s >= th, 1.0, 0.0), jnp.sum), zeros8),
                   jnp.sum)

    def bis_cond(st):
        it, lo, hi, clo = st
        return jnp.logical_and(it < BISECT_CAP, jnp.max(jnp.abs(clo - kt)) > 0.0)

    def halve(lo, hi, clo):
        mid = 0.5 * lo + 0.5 * hi
        c = count_ge(mid)
        active = clo != kt
        up = jnp.logical_and(active, c >= kt)
        dn = jnp.logical_and(active, c < kt)
        return jnp.where(up, mid, lo), jnp.where(dn, mid, hi), jnp.where(up, c, clo)

    def bis_body(st):
        it, lo, hi, clo = st
        return (it + 2,) + halve(*halve(lo, hi, clo))

    start = (rmin, rmax + jnp.maximum(1.0, jnp.abs(rmax) * 2.0 ** -20), (qpos + 1).astype(F32))
    start = lax.fori_loop(0, BISECT_FREE, lambda _, st: halve(*st), start)
    _, lo, _, clo = lax.while_loop(bis_cond, bis_body, (jnp.int32(BISECT_FREE),) + start)
    open_rows = jnp.max(jnp.abs(clo - kt)) > 0.0

    def stats(lo_):
        a_ = fin(fold(lambda a, s: jnp.minimum(a, part(jnp.where(s >= lo_, s, jnp.inf), jnp.min)),
                      pinf8), jnp.min)
        cg, ct, nx = fold(
            lambda a, s: (a[0] + part(jnp.where(s > a_, 1.0, 0.0), jnp.sum),
                               a[1] + part(jnp.where(s == a_, 1.0, 0.0), jnp.sum),
                               jnp.minimum(a[2], part(jnp.where(s > a_, s, jnp.inf), jnp.min))),
            (zeros8, zeros8, pinf8))
        return a_, fin(cg, jnp.sum), fin(ct, jnp.sum), fin(nx, jnp.min)

    def fin_cond(st):
        return st[0]

    def fin_body(st):
        _, lo_, _, _ = st
        a_, cgt_, nt_, nxt_ = stats(lo_)
        bad = cgt_ >= kt
        return (jnp.max(jnp.where(bad, 1.0, 0.0)) > 0.0, jnp.where(bad, nxt_, a_), cgt_, nt_)

    def exact_finish():
        _, a_, cgt, nties = lax.while_loop(fin_cond, fin_body, (jnp.bool_(True), lo, kt, kt))
        need_ = kt - cgt
        return a_, need_, jnp.max(jnp.where(nties > need_, 1.0, 0.0)) > 0.0

    a, need, excess = lax.cond(open_rows, exact_finish, lambda: (lo, kt, jnp.bool_(False)))

    def mask_plain():
        def body(c, _):
            sc_ref[c] = jnp.where(sc_ref[c] >= a, 0.0, NEG)
            return 0
        lax.fori_loop(0, nck, body, 0)

    def mask_ties():
        upto = (krow >= lax.broadcasted_iota(I32, (blk, blk), 1)).astype(BF16)

        def body(c, seen):
            s = sc_ref[c]
            tie = s == a
            rank = jnp.dot(upto, jnp.where(tie, 1.0, 0.0).astype(BF16),
                           preferred_element_type=F32) + seen
            sel = jnp.logical_or(s > a, jnp.logical_and(tie, rank <= need))
            sc_ref[c] = jnp.where(sel, 0.0, NEG)
            return rank[blk - 1:blk, :]

        lax.fori_loop(0, nck, body, jnp.zeros((1, blk), F32))

    lax.cond(excess, mask_ties, mask_plain)

    m_ref[...] = jnp.full(m_ref.shape, NEG, F32)
    l_ref[...] = jnp.zeros(l_ref.shape, F32)
    acc_ref[...] = jnp.zeros(acc_ref.shape, F32)
    q_m = pair_split(q_ref, QK_SCALE)

    def stage_logits(c, slot):
        c = jnp.minimum(c, n_chunks - 1)
        c0 = pl.multiple_of(c * blk, blk)
        madd = sc_ref[c]
        o_idx = jnp.clip(c - i + (n_off - 1), 0, n_off - 1)
        for p in range(N_PAIRS):
            k2 = k_ref[0, pl.ds(c0, blk), p * LANES:(p + 1) * LANES]
            for hh in range(2):
                h = 2 * p + hh
                s = lax.dot_general(k2, q_m[h], NT_DIMS, preferred_element_type=F32)
                s = s + bias_ref[h, o_idx] + madd
                st_ref[slot, h] = s
                mx_ref[slot, h] = fin(part(s, jnp.max), jnp.max)

    def stage_values(c, slot):
        for p in range(N_PAIRS):
            vt2 = vt_ref[0, c, p * LANES:(p + 1) * LANES, :]
            for hh in range(2):
                h = 2 * p + hh
                m_old = m_ref[h]
                m_new = jnp.maximum(m_old, mx_ref[slot, h])
                alpha = jnp.exp(m_old - m_new)
                pexp = jnp.exp(st_ref[slot, h] - m_new)
                l_ref[h] = alpha * l_ref[h] + fin(part(pexp, jnp.sum), jnp.sum)
                acc_ref[h] = alpha * acc_ref[h] + jnp.dot(vt2, pexp.astype(BF16),
                                                          preferred_element_type=F32)
                m_ref[h] = m_new

    stage_logits(0, 0)

    def att_pair(pp, _):
        c = 2 * pp
        stage_logits(c + 1, 1)
        stage_values(c, 0)
        stage_logits(c + 2, 0)
        stage_values(c + 1, 1)
        return 0

    lax.fori_loop(0, nck // 2, att_pair, 0)

    @pl.when(lax.rem(nck, 2) == 1)
    def _last_chunk():
        stage_values(nck - 1, 0)

    lo_rows = lax.broadcasted_iota(I32, (LANES, blk), 0) < HALF
    for p in range(N_PAIRS):
        oa = acc_ref[2 * p] / l_ref[2 * p]
        ob = acc_ref[2 * p + 1] / l_ref[2 * p + 1]
        o_ref[0, :, p * LANES:(p + 1) * LANES] = jnp.where(lo_rows, oa, ob).T.astype(o_ref.dtype)


def _dsa(proj3, vt4, rel_bias):
    B, S, _ = proj3.shape
    blk = DSA_BLK
    n_off = _dsa_n_off(blk)
    n_sel = min(TOPK_MAX, S // 4)
    bucket = jnp.asarray(_dsa_bucket_tiles(blk))
    n_chunks = S // blk
    assert S % blk == 0
    kern = functools.partial(_dsa_kernel, blk=blk, n_chunks=n_chunks, n_sel=n_sel, n_off=n_off)
    return pl.pallas_call(
        kern,
        grid=(B, S // blk),
        in_specs=[
            pl.BlockSpec((n_off, blk, blk), lambda b, i: (0, 0, 0)),
            pl.BlockSpec(memory_space=pltpu.SMEM),
            pl.BlockSpec((1, blk, ATT_W), lambda b, i: (b, i, COL_QA // ATT_W)),
            pl.BlockSpec((1, S, ATT_W), lambda b, i: (b, 0, COL_KA // ATT_W)),
            pl.BlockSpec((1, S // blk, ATT_W, blk), lambda b, i: (b, 0, 0, 0)),
            pl.BlockSpec((1, blk, ATT_W), lambda b, i: (b, i, COL_QI // ATT_W)),
            pl.BlockSpec((1, S, LANES), lambda b, i: (b, 0, COL_KK // LANES)),
            pl.BlockSpec((1, blk, LANES), lambda b, i: (b, i, COL_WI // LANES)),
        ],
        out_specs=pl.BlockSpec((1, blk, ATT_W), lambda b, i: (b, i, 0)),
        out_shape=jax.ShapeDtypeStruct((B, S, ATT_W), BF16),
        scratch_shapes=[
            pltpu.VMEM((S // blk, blk, blk), F32),
            pltpu.VMEM((A_HEADS, n_off, blk, blk), F32),
            pltpu.VMEM((A_HEADS, 1, blk), F32),
            pltpu.VMEM((A_HEADS, 1, blk), F32),
            pltpu.VMEM((A_HEADS, LANES, blk), F32),
            pltpu.VMEM((2, A_HEADS, blk, blk), F32),
            pltpu.VMEM((2, A_HEADS, 1, blk), F32),
        ],
        compiler_params=_params(("arbitrary", "arbitrary")),
    )(bucket, rel_bias, proj3, proj3, vt4, proj3, proj3, proj3)


def _sb_kernel(q_ref, k_ref, v_ref, o_ref, hl_ref, z_ref, *, t):
    i = pl.program_id(2)
    n = i + 1
    lane = lax.broadcasted_iota(I32, (t, LANES), 1)
    lo_half = lane < HALF
    q_m = []
    for pr in range(SB_PAIRS):
        q2 = q_ref[0, :, pr * LANES:(pr + 1) * LANES].astype(F32) * QK_SCALE
        q_m += [jnp.where(lo_half, q2, 0.0).astype(BF16), jnp.where(lo_half, 0.0, q2).astype(BF16)]
    heads = 2 * SB_PAIRS
    r = lax.broadcasted_iota(I32, (t, t), 0)
    cidx = lax.broadcasted_iota(I32, (t, t), 1)
    neg_from = jnp.where(r >= cidx, -1.0, 0.0).astype(BF16)
    neg_from2 = jnp.concatenate([neg_from, neg_from], axis=0)
    diff = cidx - r

    def stage_terms(step, slot, diagonal):
        c0 = pl.multiple_of(jnp.maximum(i - step, 0) * t, t)
        if diagonal:
            keep = diff < 0
        for hh in range(heads):
            pr = hh // 2
            k2 = k_ref[0, pl.ds(c0, t), pr * LANES:(pr + 1) * LANES]
            z = lax.dot_general(q_m[hh], k2, NT_DIMS, preferred_element_type=F32)
            sp = jnp.maximum(z, 0.0) + jnp.log(1.0 + jnp.exp(-jnp.abs(z)))
            if diagonal:
                sp = jnp.where(keep, sp, 0.0)
                z = jnp.where(keep, z, NEG)
            hi = sp.astype(BF16)
            hl_ref[slot, hh, :, :t] = hi
            hl_ref[slot, hh, :, t:] = (sp - hi.astype(F32)).astype(BF16)
            z_ref[slot, hh] = z

    def stage_apply(step, slot, carry):
        c0 = pl.multiple_of((i - step) * t, t)
        cum_all = jnp.dot(hl_ref[slot].reshape(heads * t, 2 * t), neg_from2, preferred_element_type=F32)
        out = []
        for hh in range(heads):
            pr = hh // 2
            v2 = v_ref[0, pl.ds(c0, t), pr * LANES:(pr + 1) * LANES]
            car, acc = carry[hh]
            cum = cum_all[hh * t:(hh + 1) * t]
            w = jnp.exp(z_ref[slot, hh] + cum + car)
            acc = acc + jnp.dot(w.astype(BF16), v2, preferred_element_type=F32)
            out.append((car + cum[:, 0:1], acc))
        return tuple(out)

    z1 = jnp.zeros((t, 1), F32)
    za = jnp.zeros((t, LANES), F32)
    stage_terms(0, 0, True)

    def pair_body(pp, carry):
        step = 2 * pp
        stage_terms(step + 1, 1, False)
        carry = stage_apply(step, 0, carry)
        stage_terms(step + 2, 0, False)
        return stage_apply(step + 1, 1, carry)

    carry = lax.fori_loop(0, (n - 1) // 2, pair_body, ((z1, za),) * heads)

    def last_two(c):
        stage_terms(n - 1, 1, False)
        return stage_apply(n - 1, 1, stage_apply(n - 2, 0, c))

    carry = lax.cond(lax.rem(n, 2) == 1, lambda c: stage_apply(n - 1, 0, c), last_two, carry)
    for pr in range(SB_PAIRS):
        o_ref[0, :, pr * LANES:(pr + 1) * LANES] = jnp.where(
            lo_half, carry[2 * pr][1], carry[2 * pr + 1][1]).astype(o_ref.dtype)


def _stick_breaking(proj3):
    B, S, _ = proj3.shape
    t = SB_T
    w = SB_PAIRS * LANES
    qb, kb, vb = COL_QB // w, COL_KB // w, COL_VB // w
    return pl.pallas_call(
        functools.partial(_sb_kernel, t=t),
        grid=(B, N_PAIRS // SB_PAIRS, S // t),
        in_specs=[
            pl.BlockSpec((1, t, w), lambda b, p, i: (b, i, qb + p)),
            pl.BlockSpec((1, S, w), lambda b, p, i: (b, 0, kb + p)),
            pl.BlockSpec((1, S, w), lambda b, p, i: (b, 0, vb + p)),
        ],
        out_specs=pl.BlockSpec((1, t, w), lambda b, p, i: (b, i, p)),
        out_shape=jax.ShapeDtypeStruct((B, S, ATT_W), BF16),
        scratch_shapes=[
            pltpu.VMEM((2, 2 * SB_PAIRS, t, 2 * t), BF16),
            pltpu.VMEM((2, 2 * SB_PAIRS, t, t), F32),
        ],
        compiler_params=_params(("arbitrary", "arbitrary", "arbitrary")),
    )(proj3, proj3, proj3)


def _layer_norm(r, g, b):
    mu = jnp.mean(r, axis=-1, keepdims=True)
    d = r - mu
    var = jnp.mean(d * d, axis=-1, keepdims=True)
    return d * lax.rsqrt(var + LN_EPS) * g + b


def _split_bf16(v):
    hi = v.astype(BF16)
    return hi, (v - hi.astype(F32)).astype(BF16)


def _pack_bf16_pairs(v):
    half = v.shape[1] // 2
    bits = pltpu.bitcast(v.astype(BF16).astype(F32), jnp.uint32)
    return bits[:, :half] | (bits[:, half:] >> 16)


def _unpack_bf16_pairs(w):
    return jnp.concatenate([pltpu.bitcast(w & jnp.uint32(0xFFFF0000), F32),
                            pltpu.bitcast(w << 16, F32)], axis=1)


def _merge_kernel(x_ref, ya_ref, yb_ref, wg_ref, wa_ref, wb_ref, wo_ref, g_ref, b_ref,
                  wr_ref, br_ref, h_ref, hp_ref, e_ref, p_ref, *, alpha, d):
    pa = jnp.dot(ya_ref[...], wa_ref[...], preferred_element_type=F32)
    pb = jnp.dot(yb_ref[...], wb_ref[...], preferred_element_type=F32)
    gates = jnp.dot(x_ref[...].astype(BF16), wg_ref[...], preferred_element_type=F32)
    merged = jax.nn.sigmoid(gates[:, :d]) * pa + jax.nn.sigmoid(gates[:, d:]) * pb
    m = jnp.dot(merged.astype(BF16), wo_ref[...], preferred_element_type=F32)
    h = _layer_norm(alpha * x_ref[...] + m, g_ref[...], b_ref[...])
    h_ref[...] = h
    hp_ref[...] = _pack_bf16_pairs(h)

    h_hi, h_lo = _split_bf16(h)
    w_hi, w_lo = _split_bf16(wr_ref[...])
    logit = (lax.dot_general(w_hi, h_hi, NT_DIMS, preferred_element_type=F32)
             + lax.dot_general(w_hi, h_lo, NT_DIMS, preferred_element_type=F32)
             + lax.dot_general(w_lo, h_hi, NT_DIMS, preferred_element_type=F32)) + br_ref[...]
    eid = lax.broadcasted_iota(I32, logit.shape, 0)
    vals, ids = [], []
    for _ in range(TOP_K):
        mx = jnp.max(logit, axis=0, keepdims=True)
        am = jnp.min(jnp.where(logit == mx, eid, N_EXPERTS), axis=0, keepdims=True)
        vals.append(mx)
        ids.append(am)
        logit = jnp.where(eid == am, -jnp.inf, logit)
    ex = [jnp.exp(v - vals[0]) for v in vals]
    den = ex[0] + ex[1] + ex[2] + ex[3]
    for k in range(TOP_K):
        e_ref[k:k + 1, :] = ids[k]
        p_ref[k:k + 1, :] = ex[k] / den


def _merge(x2, ya, yb, wg, wa, wb, wo, g, b, wr_t, br, alpha):
    T, D = x2.shape
    tm = MERGE_TM
    row = lambda i: (i, 0)
    fixed = lambda i: (0, 0)
    return pl.pallas_call(
        functools.partial(_merge_kernel, alpha=alpha, d=D),
        grid=(T // tm,),
        in_specs=[
            pl.BlockSpec((tm, D), row),
            pl.BlockSpec((tm, ATT_W), row),
            pl.BlockSpec((tm, ATT_W), row),
            pl.BlockSpec((D, 2 * D), fixed),
            pl.BlockSpec((ATT_W, D), fixed),
            pl.BlockSpec((ATT_W, D), fixed),
            pl.BlockSpec((D, D), fixed),
            pl.BlockSpec((1, D), fixed),
            pl.BlockSpec((1, D), fixed),
            pl.BlockSpec((N_EXPERTS, D), fixed),
            pl.BlockSpec((N_EXPERTS, 1), fixed),
        ],
        out_specs=[
            pl.BlockSpec((tm, D), row),
            pl.BlockSpec((tm, D // 2), row),
            pl.BlockSpec((TOP_K, tm), lambda i: (0, i)),
            pl.BlockSpec((TOP_K, tm), lambda i: (0, i)),
        ],
        out_shape=[
            jax.ShapeDtypeStruct((T, D), F32),
            jax.ShapeDtypeStruct((T, D // 2), jnp.uint32),
            jax.ShapeDtypeStruct((TOP_K, T), I32),
            jax.ShapeDtypeStruct((TOP_K, T), F32),
        ],
        compiler_params=_params(("arbitrary",)),
    )(x2, ya, yb, wg, wa, wb, wo, g, b, wr_t, br)


def _sc_gather_rows(table, idx):
    n = idx.shape[0]
    d = table.shape[1]
    info = plsc.get_sparse_core_info()
    n_cores, n_sub = info.num_cores, info.num_subcores
    per_w = n // (n_cores * n_sub)
    c = SC_GATHER_BYTES // (d * table.dtype.itemsize)
    n_g = per_w // c
    assert n == per_w * n_cores * n_sub and per_w == n_g * c and n_g % 2 == 0 and n_g >= 2
    mesh = plsc.VectorSubcoreMesh(core_axis_name="c", subcore_axis_name="s")

    @functools.partial(
        pl.kernel, mesh=mesh, out_type=jax.ShapeDtypeStruct((n, d), table.dtype),
        scratch_types=[pltpu.VMEM((per_w,), I32), pltpu.VMEM((2, c, d), table.dtype),
                       pltpu.SemaphoreType.DMA((2,)), pltpu.SemaphoreType.DMA((2,))])
    def gather_kernel(table_hbm, idx_hbm, out_hbm, idx_v, rows_v, gsem, wsem):
        base = (lax.axis_index("s") * n_cores + lax.axis_index("c")) * per_w
        pltpu.sync_copy(idx_hbm.at[pl.ds(base, per_w)], idx_v)

        def gather(g, b):
            return pltpu.make_async_copy(table_hbm.at[idx_v.at[pl.ds(g * c, c)]], rows_v.at[b],
                                         gsem.at[b])

        def write(g, b):
            return pltpu.make_async_copy(rows_v.at[b], out_hbm.at[pl.ds(base + g * c, c)], wsem.at[b])

        gather(0, 0).start()

        @pl.loop(0, n_g, step=2)
        def _ring(g0):
            for b in range(2):
                g = g0 + b

                @pl.when(g + 1 < n_g)
                def _next():
                    @pl.when(g >= 1)
                    def _buffer_free():
                        write(g - 1, 1 - b).wait()
                    gather(g + 1, 1 - b).start()

                gather(g, b).wait()
                write(g, b).start()

        write(n_g - 2, 0).wait()
        write(n_g - 1, 1).wait()

    return gather_kernel(table, idx)


def _moe_kernel(blk_e_ref, nused_ref, x_ref, wgu_ref, bgu_ref, wdn_ref, bdn_ref, o_ref,
                wgu_s, wdn_s, *, f):
    i = pl.program_id(0)
    nused = nused_ref[0]

    @pl.when(i < nused)
    def _compute():
        changed = jnp.logical_or(i == 0, blk_e_ref[i] != blk_e_ref[jnp.maximum(i - 1, 0)])

        @pl.when(changed)
        def _cast_weights():
            wgu_s[...] = wgu_ref[0].astype(BF16)
            wdn_s[...] = wdn_ref[0].astype(BF16)

        x = _unpack_bf16_pairs(x_ref[...]).astype(BF16)
        hgu = jnp.dot(x, wgu_s[...], preferred_element_type=F32) + bgu_ref[0]
        a = jnp.minimum(hgu[:, :f], SWIGLU_LIMIT)
        u = jnp.clip(hgu[:, f:], -SWIGLU_LIMIT, SWIGLU_LIMIT)
        glu = a * jax.nn.sigmoid(a * SWIGLU_ALPHA)
        y = jnp.dot(((u + 1.0) * glu).astype(BF16), wdn_s[...], preferred_element_type=F32) + bdn_ref[0]
        o_ref[...] = _pack_bf16_pairs(y)

    @pl.when(i >= nused)
    def _unused_block():
        o_ref[...] = jnp.zeros(o_ref.shape, o_ref.dtype)


def _moe_ffn(xs, blk_e, nused, w_gu, b_gu, w_dn, b_dn):
    P = xs.shape[0]
    E, D, F2 = w_gu.shape
    f = F2 // 2
    blk = MOE_BLK
    nb = P // blk
    used_block = lambda i, be, nu: (jnp.minimum(i, nu[0] - 1), 0)
    grid_spec = pltpu.PrefetchScalarGridSpec(
        num_scalar_prefetch=2,
        grid=(nb,),
        in_specs=[
            pl.BlockSpec((blk, D // 2), used_block),
            pl.BlockSpec((1, D, F2), lambda i, be, nu: (be[i], 0, 0)),
            pl.BlockSpec((1, 1, F2), lambda i, be, nu: (be[i], 0, 0)),
            pl.BlockSpec((1, f, D), lambda i, be, nu: (be[i], 0, 0)),
            pl.BlockSpec((1, 1, D), lambda i, be, nu: (be[i], 0, 0)),
        ],
        out_specs=pl.BlockSpec((blk, D // 2), lambda i, be, nu: (i, 0)),
        scratch_shapes=[
            pltpu.VMEM((D, F2), BF16),
            pltpu.VMEM((f, D), BF16),
        ],
    )
    return pl.pallas_call(
        functools.partial(_moe_kernel, f=f),
        grid_spec=grid_spec,
        out_shape=jax.ShapeDtypeStruct((P, D // 2), jnp.uint32),
        compiler_params=_params(("arbitrary",)),
    )(blk_e, nused, xs, w_gu, b_gu.reshape(E, 1, F2), w_dn, b_dn.reshape(E, 1, D))


def _comb_kernel(h_ref, p_ref, y_ref, g_ref, b_ref, o_ref, *, alpha):
    gate = p_ref[...]
    y = [_unpack_bf16_pairs(y_ref[k]) for k in range(TOP_K)]
    fsum = (y[0] * gate[:, 0:1] + y[1] * gate[:, 1:2]) + (y[2] * gate[:, 2:3] + y[3] * gate[:, 3:4])
    o_ref[...] = _layer_norm(alpha * h_ref[...] + fsum, g_ref[...], b_ref[...])


def _combine(h, y4, top_p, g, b, alpha):
    T, D = h.shape
    tm = COMB_TM
    return pl.pallas_call(
        functools.partial(_comb_kernel, alpha=alpha),
        grid=(T // tm,),
        in_specs=[
            pl.BlockSpec((tm, D), lambda i: (i, 0)),
            pl.BlockSpec((tm, TOP_K), lambda i: (i, 0)),
            pl.BlockSpec((TOP_K, tm, D // 2), lambda i: (0, i, 0)),
            pl.BlockSpec((1, D), lambda i: (0, 0)),
            pl.BlockSpec((1, D), lambda i: (0, 0)),
        ],
        out_specs=pl.BlockSpec((tm, D), lambda i: (i, 0)),
        out_shape=jax.ShapeDtypeStruct((T, D), F32),
        compiler_params=_params(("arbitrary",)),
    )(h, top_p.T, y4, g, b)


def _route(top_e, blk):
    K, T = top_e.shape
    N = K * T
    flat_e = top_e.reshape(N)
    experts = jnp.arange(N_EXPERTS, dtype=I32)
    order = jnp.argsort(flat_e, stable=True).astype(I32)
    inv = jnp.argsort(order).astype(I32)
    onehot = flat_e[:, None] == experts[None, :]
    counts = jnp.sum(onehot, axis=0, dtype=I32)
    padded = (counts + blk - 1) // blk * blk
    pends = jnp.cumsum(padded)
    offs = jnp.cumsum(counts) - counts
    shift = (pends - padded) - offs
    pos = inv + jnp.sum(jnp.where(onehot, shift[None, :], 0), axis=1, dtype=I32)
    P = N + N_EXPERTS * blk
    nb = P // blk
    blk_start = jnp.arange(nb, dtype=I32) * blk
    blk_e = jnp.minimum(jnp.sum(pends[None, :] <= blk_start[:, None], axis=1, dtype=I32), N_EXPERTS - 1)
    j = (blk_start - shift[blk_e])[:, None] + jnp.arange(blk, dtype=I32)[None, :]
    valid = j < (offs + counts)[blk_e][:, None]
    src = order[jnp.clip(j, 0, N - 1)]
    row_tok = jnp.where(valid, src % T, j % T).reshape(P)
    nused = (pends[-1:] // blk).astype(I32)
    return blk_e, nused, row_tok, pos.reshape(K, T)


def _projection_weights(w_in_l):
    sizes = (ATT_W, ATT_W, ATT_W, IDX_HEADS * IDX_DIM, IDX_DIM, IDX_HEADS, ATT_W, ATT_W, ATT_W)
    offs = np.concatenate([[0], np.cumsum(sizes)])
    qa, ka, va, qi, ki, wi, qb, kb, vb = (w_in_l[:, offs[n]:offs[n + 1]] for n in range(9))
    pad_wi = jnp.zeros((w_in_l.shape[0], LANES - IDX_HEADS), w_in_l.dtype)
    w_att = jnp.concatenate([qa, ka, qi, qb, kb, vb, ki, ki, wi, pad_wi], axis=1).astype(BF16)
    w_va_t = va.T.astype(BF16)
    w_gate = w_in_l[:, offs[9]:].astype(BF16)
    return w_att, w_va_t, w_gate


def kernel(x, w_in, w_branch_a, w_branch_b, w_out, rel_bias, ln1_g, ln1_b, w_router, b_router,
           w_gate_up, b_gate_up, w_down, b_down, ln2_g, ln2_b):
    B, S, D = x.shape
    depth = w_in.shape[0]
    alpha = (2 * depth) ** 0.25
    T = B * S
    h = x.reshape(T, D)
    for l in range(depth):
        w_att, w_va_t, w_gate = _projection_weights(w_in[l])
        proj, vt = _projection(h, w_att, w_va_t, min(T, PROJ_TM), PROJ_TN, DSA_BLK)
        proj = proj.reshape(B, S, ATT_COLS)
        vt = vt.reshape(B, S // DSA_BLK, ATT_W, DSA_BLK)
        ya = _dsa(proj, vt, rel_bias).reshape(T, ATT_W)
        yb = _stick_breaking(proj).reshape(T, ATT_W)
        h1, h1_packed, top_e, top_p = _merge(
            h, ya, yb, w_gate, w_branch_a[l].astype(BF16), w_branch_b[l].astype(BF16),
            w_out[l].astype(BF16), ln1_g[l].reshape(1, D), ln1_b[l].reshape(1, D),
            w_router[l].T, b_router[l].reshape(N_EXPERTS, 1), alpha)
        blk_e, nused, row_tok, pos = _route(top_e, MOE_BLK)
        xs = _sc_gather_rows(h1_packed, row_tok)
        ys = _moe_ffn(xs, blk_e, nused, w_gate_up[l], b_gate_up[l], w_down[l], b_down[l])
        y4 = _sc_gather_rows(ys, pos.reshape(TOP_K * T)).reshape(TOP_K, T, D // 2)
        h = _combine(h1, y4, top_p, ln2_g[l].reshape(1, D), ln2_b[l].reshape(1, D), alpha)
    return h.reshape(B, S, D)
```

```python
import functools
import math

import numpy as np
import jax
import jax.numpy as jnp
from jax import lax
from jax.experimental import pallas as pl
from jax.experimental.pallas import tpu as pltpu
from jax.experimental.pallas import tpu_sc as plsc

F32 = jnp.float32
BF16 = jnp.bfloat16
I32 = jnp.int32

A_HEADS = 8
HEAD_DIM = 64
ATT_W = A_HEADS * HEAD_DIM
IDX_HEADS = 8
IDX_DIM = 64
IDX_SCALE = (IDX_HEADS * IDX_DIM) ** -0.5
TOPK_MAX = 256
N_BUCKETS = 32
MAX_DISTANCE = 128
N_EXPERTS = 32
TOP_K = 4
SWIGLU_LIMIT = 7.0
SWIGLU_ALPHA = 1.702
LN_EPS = 1e-5
QK_SCALE = HEAD_DIM ** -0.5

LANES = 128
SUBLANES = 8
HALF = LANES // 2
N_PAIRS = A_HEADS // 2
VMEM_LIMIT = 56 * 1024 * 1024

DSA_BLK = 256
SB_T = 256
SB_PAIRS = 4
PROJ_TM = 1024
MERGE_TM = 512
MOE_BLK = 512
COMB_TM = 512
SC_GATHER_BYTES = 128 * 1024
REDUCE_CHAINS = 8
BISECT_CAP = 24
BISECT_FREE = 20
NEG = -1e30

COL_QA, COL_KA, COL_QI, COL_QB, COL_KB, COL_VB = (g * ATT_W for g in range(6))
COL_KK = 6 * ATT_W
COL_WI = COL_KK + LANES
ATT_COLS = COL_WI + LANES
PROJ_TN = ATT_COLS // 2

NT_DIMS = (((1,), (1,)), ((), ()))


def _params(sem, vmem=VMEM_LIMIT):
    return pltpu.CompilerParams(dimension_semantics=sem, vmem_limit_bytes=vmem)


def _proj_kernel(x_ref, w_ref, wt_ref, o_ref, ot_ref, xb_ref, *, tt):
    @pl.when(pl.program_id(1) == 0)
    def _row_tile_start():
        xb_ref[...] = x_ref[...].astype(BF16)
        for r in range(ot_ref.shape[0]):
            ot_ref[r] = lax.dot_general(wt_ref[...], xb_ref[r * tt:(r + 1) * tt, :], NT_DIMS,
                                        preferred_element_type=F32).astype(ot_ref.dtype)

    o_ref[...] = jnp.dot(xb_ref[...], w_ref[...], preferred_element_type=F32).astype(o_ref.dtype)


def _projection(x, w, w_t, tm, tn, tt):
    M, K = x.shape
    N = w.shape[1]
    Nt = w_t.shape[0]
    return pl.pallas_call(
        functools.partial(_proj_kernel, tt=tt),
        grid=(M // tm, N // tn),
        in_specs=[pl.BlockSpec((tm, K), lambda i, j: (i, 0)),
                  pl.BlockSpec((K, tn), lambda i, j: (0, j)),
                  pl.BlockSpec((Nt, K), lambda i, j: (0, 0))],
        out_specs=[pl.BlockSpec((tm, tn), lambda i, j: (i, j)),
                   pl.BlockSpec((tm // tt, Nt, tt), lambda i, j: (i, 0, 0))],
        out_shape=[jax.ShapeDtypeStruct((M, N), BF16),
                   jax.ShapeDtypeStruct((M // tt, Nt, tt), BF16)],
        scratch_shapes=[pltpu.VMEM((tm, K), BF16)],
        compiler_params=_params(("arbitrary", "arbitrary")),
    )(x, w, w_t)


def _t5_bucket_np(n):
    n = np.maximum(n, 0)
    max_exact = N_BUCKETS // 2
    nf = np.maximum(n, 1).astype(np.float32)
    large = max_exact + (np.log(nf / max_exact) / math.log(MAX_DISTANCE / max_exact)
                         * (N_BUCKETS - max_exact)).astype(np.int32)
    large = np.minimum(large, N_BUCKETS - 1)
    return np.where(n < max_exact, n, large).astype(np.int32)


def _dsa_n_off(blk):
    return 2 + -(-MAX_DISTANCE // blk)


def _dsa_bucket_tiles(blk):
    n_off = _dsa_n_off(blk)
    j = np.arange(blk)[None, :, None]
    i = np.arange(blk)[None, None, :]
    o = np.arange(n_off)[:, None, None]
    return _t5_bucket_np(i - j + blk * (n_off - 1 - o))


def _dsa_kernel(bucket_ref, relb_ref, q_ref, k_ref, vt_ref, qi_ref, kk_ref, wi_ref, o_ref,
                sc_ref, bias_ref, m_ref, l_ref, acc_ref, st_ref, mx_ref,
                *, blk, n_chunks, n_sel, n_off):
    b = pl.program_id(0)
    i = pl.program_id(1)
    q0 = i * blk
    nck = i + 1
    groups = blk // SUBLANES

    @pl.when(jnp.logical_and(b == 0, i == 0))
    def _build_bias():
        def head_body(h, _):
            for o in range(n_off):
                for rb in range(blk // LANES):
                    for cb in range(blk // LANES):
                        rs = slice(rb * LANES, (rb + 1) * LANES)
                        cs = slice(cb * LANES, (cb + 1) * LANES)
                        bk = bucket_ref[o, rs, cs]

                        def bucket_body(n, acc):
                            return jnp.where(bk == n, relb_ref[n, h], acc)

                        bias_ref[h, o, rs, cs] = lax.fori_loop(
                            0, N_BUCKETS, bucket_body, jnp.zeros((LANES, LANES), F32))
            return 0

        lax.fori_loop(0, A_HEADS, head_body, 0)

    lane = lax.broadcasted_iota(I32, (blk, LANES), 1)
    lo_half = lane < HALF
    krow = lax.broadcasted_iota(I32, (blk, blk), 0)
    qpos = q0 + lax.broadcasted_iota(I32, (1, blk), 1)

    def pair_split(ref, scale):
        out = []
        for p in range(N_PAIRS):
            v = ref[0, :, p * LANES:(p + 1) * LANES].astype(F32)
            if scale != 1.0:
                v = v * scale
            out.append(jnp.where(lo_half, v, 0.0).astype(BF16))
            out.append(jnp.where(lo_half, 0.0, v).astype(BF16))
        return out

    wi_t = wi_ref[0].astype(F32).T
    wrow = [wi_t[h:h + 1, :] * IDX_SCALE for h in range(IDX_HEADS)]
    qi_m = pair_split(qi_ref, 1.0)

    def part(x, op):
        y = op(x.reshape(REDUCE_CHAINS, groups // REDUCE_CHAINS, SUBLANES, blk), axis=1)
        return op(y, axis=0)

    def fin(x, op):
        return op(x, axis=0, keepdims=True)

    zeros8 = jnp.zeros((SUBLANES, blk), F32)
    pinf8 = jnp.full((SUBLANES, blk), jnp.inf, F32)

    def score_chunk(c, mn_mx):
        c0 = pl.multiple_of(c * blk, blk)
        kk = kk_ref[0, pl.ds(c0, blk), :]
        acc = jnp.zeros((blk, blk), F32)
        for h in range(IDX_HEADS):
            s = lax.dot_general(kk, qi_m[h], NT_DIMS, preferred_element_type=F32)
            acc = acc + wrow[h] * jnp.maximum(s, 0.0)
        causal = c0 + krow <= qpos
        sc_ref[c] = jnp.where(causal, acc, -jnp.inf)
        return (jnp.minimum(mn_mx[0], part(jnp.where(causal, acc, jnp.inf), jnp.min)),
                jnp.maximum(mn_mx[1], part(jnp.where(causal, acc, -jnp.inf), jnp.max)))

    mn, mx = lax.fori_loop(0, nck, score_chunk, (pinf8, -pinf8))
    rmin = fin(mn, jnp.min)
    rmax = fin(mx, jnp.max)

    kt = jnp.minimum(qpos + 1, n_sel).astype(F32)

    def fold(fn, init):
        def body(c, acc):
            return fn(acc, sc_ref[c])
        return lax.fori_loop(0, nck, body, init)

    def count_ge(th):
        return fin(fold(lambda a, s: a + part(jnp.where(s >= th, 1.0, 0.0), jnp.sum), zeros8),
                   jnp.sum)

    def bis_cond(st):
        it, lo, hi, clo = st
        return jnp.logical_and(it < BISECT_CAP, jnp.max(jnp.abs(clo - kt)) > 0.0)

    def halve(lo, hi, clo):
        mid = 0.5 * lo + 0.5 * hi
        c = count_ge(mid)
        active = clo != kt
        up = jnp.logical_and(active, c >= kt)
        dn = jnp.logical_and(active, c < kt)
        return jnp.where(up, mid, lo), jnp.where(dn, mid, hi), jnp.where(up, c, clo)

    def bis_body(st):
        it, lo, hi, clo = st
        return (it + 2,) + halve(*halve(lo, hi, clo))

    start = (rmin, rmax + jnp.maximum(1.0, jnp.abs(rmax) * 2.0 ** -20), (qpos + 1).astype(F32))
    start = lax.fori_loop(0, BISECT_FREE, lambda _, st: halve(*st), start)
    _, lo, _, clo = lax.while_loop(bis_cond, bis_body, (jnp.int32(BISECT_FREE),) + start)
    open_rows = jnp.max(jnp.abs(clo - kt)) > 0.0

    def stats(lo_):
        a_ = fin(fold(lambda a, s: jnp.minimum(a, part(jnp.where(s >= lo_, s, jnp.inf), jnp.min)),
                      pinf8), jnp.min)
        cg, ct, nx = fold(
            lambda a, s: (a[0] + part(jnp.where(s > a_, 1.0, 0.0), jnp.sum),
                               a[1] + part(jnp.where(s == a_, 1.0, 0.0), jnp.sum),
                               jnp.minimum(a[2], part(jnp.where(s > a_, s, jnp.inf), jnp.min))),
            (zeros8, zeros8, pinf8))
        return a_, fin(cg, jnp.sum), fin(ct, jnp.sum), fin(nx, jnp.min)

    def fin_cond(st):
        return st[0]

    def fin_body(st):
        _, lo_, _, _ = st
        a_, cgt_, nt_, nxt_ = stats(lo_)
        bad = cgt_ >= kt
        return (jnp.max(jnp.where(bad, 1.0, 0.0)) > 0.0, jnp.where(bad, nxt_, a_), cgt_, nt_)

    def exact_finish():
        _, a_, cgt, nties = lax.while_loop(fin_cond, fin_body, (jnp.bool_(True), lo, kt, kt))
        need_ = kt - cgt
        return a_, need_, jnp.max(jnp.where(nties > need_, 1.0, 0.0)) > 0.0

    a, need, excess = lax.cond(open_rows, exact_finish, lambda: (lo, kt, jnp.bool_(False)))

    def mask_plain():
        def body(c, _):
            sc_ref[c] = jnp.where(sc_ref[c] >= a, 0.0, NEG)
            return 0
        lax.fori_loop(0, nck, body, 0)

    def mask_ties():
        upto = (krow >= lax.broadcasted_iota(I32, (blk, blk), 1)).astype(BF16)

        def body(c, seen):
            s = sc_ref[c]
            tie = s == a
            rank = jnp.dot(upto, jnp.where(tie, 1.0, 0.0).astype(BF16),
                           preferred_element_type=F32) + seen
            sel = jnp.logical_or(s > a, jnp.logical_and(tie, rank <= need))
            sc_ref[c] = jnp.where(sel, 0.0, NEG)
            return rank[blk - 1:blk, :]

        lax.fori_loop(0, nck, body, jnp.zeros((1, blk), F32))

    lax.cond(excess, mask_ties, mask_plain)

    m_ref[...] = jnp.full(m_ref.shape, NEG, F32)
    l_ref[...] = jnp.zeros(l_ref.shape, F32)
    acc_ref[...] = jnp.zeros(acc_ref.shape, F32)
    q_m = pair_split(q_ref, QK_SCALE)

    def stage_logits(c, slot):
        c = jnp.minimum(c, n_chunks - 1)
        c0 = pl.multiple_of(c * blk, blk)
        madd = sc_ref[c]
        o_idx = jnp.clip(c - i + (n_off - 1), 0, n_off - 1)
        for p in range(N_PAIRS):
            k2 = k_ref[0, pl.ds(c0, blk), p * LANES:(p + 1) * LANES]
            for hh in range(2):
                h = 2 * p + hh
                s = lax.dot_general(k2, q_m[h], NT_DIMS, preferred_element_type=F32)
                s = s + bias_ref[h, o_idx] + madd
                st_ref[slot, h] = s
                mx_ref[slot, h] = fin(part(s, jnp.max), jnp.max)

    def stage_values(c, slot):
        for p in range(N_PAIRS):
            vt2 = vt_ref[0, c, p * LANES:(p + 1) * LANES, :]
            for hh in range(2):
                h = 2 * p + hh
                m_old = m_ref[h]
                m_new = jnp.maximum(m_old, mx_ref[slot, h])
                alpha = jnp.exp(m_old - m_new)
                pexp = jnp.exp(st_ref[slot, h] - m_new)
                l_ref[h] = alpha * l_ref[h] + fin(part(pexp, jnp.sum), jnp.sum)
                acc_ref[h] = alpha * acc_ref[h] + jnp.dot(vt2, pexp.astype(BF16),
                                                          preferred_element_type=F32)
                m_ref[h] = m_new

    stage_logits(0, 0)

    def att_pair(pp, _):
        c = 2 * pp
        stage_logits(c + 1, 1)
        stage_values(c, 0)
        stage_logits(c + 2, 0)
        stage_values(c + 1, 1)
        return 0

    lax.fori_loop(0, (nck - 1) // 2, att_pair, 0)

    @pl.when(lax.rem(nck, 2) == 1)
    def _last_chunk():
        stage_values(nck - 1, 0)

    @pl.when(lax.rem(nck, 2) == 0)
    def _last_two_chunks():
        stage_logits(nck - 1, 1)
        stage_values(nck - 2, 0)
        stage_values(nck - 1, 1)

    lo_rows = lax.broadcasted_iota(I32, (LANES, blk), 0) < HALF
    for p in range(N_PAIRS):
        oa = acc_ref[2 * p] / l_ref[2 * p]
        ob = acc_ref[2 * p + 1] / l_ref[2 * p + 1]
        o_ref[0, :, p * LANES:(p + 1) * LANES] = jnp.where(lo_rows, oa, ob).T.astype(o_ref.dtype)


def _dsa(proj3, vt4, rel_bias):
    B, S, _ = proj3.shape
    blk = DSA_BLK
    n_off = _dsa_n_off(blk)
    n_sel = min(TOPK_MAX, S // 4)
    bucket = jnp.asarray(_dsa_bucket_tiles(blk))
    n_chunks = S // blk
    assert S % blk == 0
    kern = functools.partial(_dsa_kernel, blk=blk, n_chunks=n_chunks, n_sel=n_sel, n_off=n_off)
    return pl.pallas_call(
        kern,
        grid=(B, S // blk),
        in_specs=[
            pl.BlockSpec((n_off, blk, blk), lambda b, i: (0, 0, 0)),
            pl.BlockSpec(memory_space=pltpu.SMEM),
            pl.BlockSpec((1, blk, ATT_W), lambda b, i: (b, i, COL_QA // ATT_W)),
            pl.BlockSpec((1, S, ATT_W), lambda b, i: (b, 0, COL_KA // ATT_W)),
            pl.BlockSpec((1, S // blk, ATT_W, blk), lambda b, i: (b, 0, 0, 0)),
            pl.BlockSpec((1, blk, ATT_W), lambda b, i: (b, i, COL_QI // ATT_W)),
            pl.BlockSpec((1, S, LANES), lambda b, i: (b, 0, COL_KK // LANES)),
            pl.BlockSpec((1, blk, LANES), lambda b, i: (b, i, COL_WI // LANES)),
        ],
        out_specs=pl.BlockSpec((1, blk, ATT_W), lambda b, i: (b, i, 0)),
        out_shape=jax.ShapeDtypeStruct((B, S, ATT_W), BF16),
        scratch_shapes=[
            pltpu.VMEM((S // blk, blk, blk), F32),
            pltpu.VMEM((A_HEADS, n_off, blk, blk), F32),
            pltpu.VMEM((A_HEADS, 1, blk), F32),
            pltpu.VMEM((A_HEADS, 1, blk), F32),
            pltpu.VMEM((A_HEADS, LANES, blk), F32),
            pltpu.VMEM((2, A_HEADS, blk, blk), F32),
            pltpu.VMEM((2, A_HEADS, 1, blk), F32),
        ],
        compiler_params=_params(("arbitrary", "arbitrary")),
    )(bucket, rel_bias, proj3, proj3, vt4, proj3, proj3, proj3)


def _sb_kernel(q_ref, k_ref, v_ref, o_ref, hl_ref, z_ref, *, t):
    i = pl.program_id(2)
    n = i + 1
    lane = lax.broadcasted_iota(I32, (t, LANES), 1)
    lo_half = lane < HALF
    q_m = []
    for pr in range(SB_PAIRS):
        q2 = q_ref[0, :, pr * LANES:(pr + 1) * LANES].astype(F32) * QK_SCALE
        q_m += [jnp.where(lo_half, q2, 0.0).astype(BF16), jnp.where(lo_half, 0.0, q2).astype(BF16)]
    heads = 2 * SB_PAIRS
    r = lax.broadcasted_iota(I32, (t, t), 0)
    cidx = lax.broadcasted_iota(I32, (t, t), 1)
    neg_from = jnp.where(r >= cidx, -1.0, 0.0).astype(BF16)
    neg_from2 = jnp.concatenate([neg_from, neg_from], axis=0)
    diff = cidx - r

    def stage_terms(step, slot, diagonal):
        c0 = pl.multiple_of(jnp.maximum(i - step, 0) * t, t)
        if diagonal:
            keep = diff < 0
        for hh in range(heads):
            pr = hh // 2
            k2 = k_ref[0, pl.ds(c0, t), pr * LANES:(pr + 1) * LANES]
            z = lax.dot_general(q_m[hh], k2, NT_DIMS, preferred_element_type=F32)
            sp = jnp.maximum(z, 0.0) + jnp.log(1.0 + jnp.exp(-jnp.abs(z)))
            if diagonal:
                sp = jnp.where(keep, sp, 0.0)
                z = jnp.where(keep, z, NEG)
            hi = sp.astype(BF16)
            hl_ref[slot, hh, :, :t] = hi
            hl_ref[slot, hh, :, t:] = (sp - hi.astype(F32)).astype(BF16)
            z_ref[slot, hh] = z

    def stage_apply(step, slot, carry):
        c0 = pl.multiple_of((i - step) * t, t)
        cum_all = jnp.dot(hl_ref[slot].reshape(heads * t, 2 * t), neg_from2, preferred_element_type=F32)
        out = []
        for hh in range(heads):
            pr = hh // 2
            v2 = v_ref[0, pl.ds(c0, t), pr * LANES:(pr + 1) * LANES]
            car, acc = carry[hh]
            cum = cum_all[hh * t:(hh + 1) * t]
            w = jnp.exp(z_ref[slot, hh] + cum + car)
            acc = acc + jnp.dot(w.astype(BF16), v2, preferred_element_type=F32)
            out.append((car + cum[:, 0:1], acc))
        return tuple(out)

    z1 = jnp.zeros((t, 1), F32)
    za = jnp.zeros((t, LANES), F32)
    stage_terms(0, 0, True)

    def pair_body(pp, carry):
        step = 2 * pp
        stage_terms(step + 1, 1, False)
        carry = stage_apply(step, 0, carry)
        stage_terms(step + 2, 0, False)
        return stage_apply(step + 1, 1, carry)

    carry = lax.fori_loop(0, (n - 1) // 2, pair_body, ((z1, za),) * heads)

    def last_two(c):
        stage_terms(n - 1, 1, False)
        return stage_apply(n - 1, 1, stage_apply(n - 2, 0, c))

    carry = lax.cond(lax.rem(n, 2) == 1, lambda c: stage_apply(n - 1, 0, c), last_two, carry)
    for pr in range(SB_PAIRS):
        o_ref[0, :, pr * LANES:(pr + 1) * LANES] = jnp.where(
            lo_half, carry[2 * pr][1], carry[2 * pr + 1][1]).astype(o_ref.dtype)


def _stick_breaking(proj3):
    B, S, _ = proj3.shape
    t = SB_T
    w = SB_PAIRS * LANES
    qb, kb, vb = COL_QB // w, COL_KB // w, COL_VB // w
    return pl.pallas_call(
        functools.partial(_sb_kernel, t=t),
        grid=(B, N_PAIRS // SB_PAIRS, S // t),
        in_specs=[
            pl.BlockSpec((1, t, w), lambda b, p, i: (b, i, qb + p)),
            pl.BlockSpec((1, S, w), lambda b, p, i: (b, 0, kb + p)),
            pl.BlockSpec((1, S, w), lambda b, p, i: (b, 0, vb + p)),
        ],
        out_specs=pl.BlockSpec((1, t, w), lambda b, p, i: (b, i, p)),
        out_shape=jax.ShapeDtypeStruct((B, S, ATT_W), BF16),
        scratch_shapes=[
            pltpu.VMEM((2, 2 * SB_PAIRS, t, 2 * t), BF16),
            pltpu.VMEM((2, 2 * SB_PAIRS, t, t), F32),
        ],
        compiler_params=_params(("arbitrary", "arbitrary", "arbitrary")),
    )(proj3, proj3, proj3)


def _layer_norm(r, g, b):
    mu = jnp.mean(r, axis=-1, keepdims=True)
    d = r - mu
    var = jnp.mean(d * d, axis=-1, keepdims=True)
    return d * lax.rsqrt(var + LN_EPS) * g + b


def _split_bf16(v):
    hi = v.astype(BF16)
    return hi, (v - hi.astype(F32)).astype(BF16)


def _pack_bf16_pairs(v):
    half = v.shape[1] // 2
    bits = pltpu.bitcast(v.astype(BF16).astype(F32), jnp.uint32)
    return bits[:, :half] | (bits[:, half:] >> 16)


def _unpack_bf16_pairs(w):
    return jnp.concatenate([pltpu.bitcast(w & jnp.uint32(0xFFFF0000), F32),
                            pltpu.bitcast(w << 16, F32)], axis=1)


def _merge_kernel(x_ref, ya_ref, yb_ref, wg_ref, wa_ref, wb_ref, wo_ref, g_ref, b_ref,
                  wr_ref, br_ref, h_ref, hp_ref, e_ref, p_ref, *, alpha, d):
    pa = jnp.dot(ya_ref[...], wa_ref[...], preferred_element_type=F32)
    pb = jnp.dot(yb_ref[...], wb_ref[...], preferred_element_type=F32)
    gates = jnp.dot(x_ref[...].astype(BF16), wg_ref[...], preferred_element_type=F32)
    merged = jax.nn.sigmoid(gates[:, :d]) * pa + jax.nn.sigmoid(gates[:, d:]) * pb
    m = jnp.dot(merged.astype(BF16), wo_ref[...], preferred_element_type=F32)
    h = _layer_norm(alpha * x_ref[...] + m, g_ref[...], b_ref[...])
    h_ref[...] = h
    hp_ref[...] = _pack_bf16_pairs(h)

    h_hi, h_lo = _split_bf16(h)
    w_hi, w_lo = _split_bf16(wr_ref[...])
    logit = (lax.dot_general(w_hi, h_hi, NT_DIMS, preferred_element_type=F32)
             + lax.dot_general(w_hi, h_lo, NT_DIMS, preferred_element_type=F32)
             + lax.dot_general(w_lo, h_hi, NT_DIMS, preferred_element_type=F32)) + br_ref[...]
    eid = lax.broadcasted_iota(I32, logit.shape, 0)
    vals, ids = [], []
    for _ in range(TOP_K):
        mx = jnp.max(logit, axis=0, keepdims=True)
        am = jnp.min(jnp.where(logit == mx, eid, N_EXPERTS), axis=0, keepdims=True)
        vals.append(mx)
        ids.append(am)
        logit = jnp.where(eid == am, -jnp.inf, logit)
    ex = [jnp.exp(v - vals[0]) for v in vals]
    den = ex[0] + ex[1] + ex[2] + ex[3]
    for k in range(TOP_K):
        e_ref[k:k + 1, :] = ids[k]
        p_ref[k:k + 1, :] = ex[k] / den


def _merge(x2, ya, yb, wg, wa, wb, wo, g, b, wr_t, br, alpha):
    T, D = x2.shape
    tm = MERGE_TM
    row = lambda i: (i, 0)
    fixed = lambda i: (0, 0)
    return pl.pallas_call(
        functools.partial(_merge_kernel, alpha=alpha, d=D),
        grid=(T // tm,),
        in_specs=[
            pl.BlockSpec((tm, D), row),
            pl.BlockSpec((tm, ATT_W), row),
            pl.BlockSpec((tm, ATT_W), row),
            pl.BlockSpec((D, 2 * D), fixed),
            pl.BlockSpec((ATT_W, D), fixed),
            pl.BlockSpec((ATT_W, D), fixed),
            pl.BlockSpec((D, D), fixed),
            pl.BlockSpec((1, D), fixed),
            pl.BlockSpec((1, D), fixed),
            pl.BlockSpec((N_EXPERTS, D), fixed),
            pl.BlockSpec((N_EXPERTS, 1), fixed),
        ],
        out_specs=[
            pl.BlockSpec((tm, D), row),
            pl.BlockSpec((tm, D // 2), row),
            pl.BlockSpec((TOP_K, tm), lambda i: (0, i)),
            pl.BlockSpec((TOP_K, tm), lambda i: (0, i)),
        ],
        out_shape=[
            jax.ShapeDtypeStruct((T, D), F32),
            jax.ShapeDtypeStruct((T, D // 2), jnp.uint32),
            jax.ShapeDtypeStruct((TOP_K, T), I32),
            jax.ShapeDtypeStruct((TOP_K, T), F32),
        ],
        compiler_params=_params(("arbitrary",)),
    )(x2, ya, yb, wg, wa, wb, wo, g, b, wr_t, br)


def _sc_gather_rows(table, idx):
    n = idx.shape[0]
    d = table.shape[1]
    info = plsc.get_sparse_core_info()
    n_cores, n_sub = info.num_cores, info.num_subcores
    per_w = n // (n_cores * n_sub)
    c = SC_GATHER_BYTES // (d * table.dtype.itemsize)
    n_g = per_w // c
    assert n == per_w * n_cores * n_sub and per_w == n_g * c and n_g % 2 == 0 and n_g >= 2
    mesh = plsc.VectorSubcoreMesh(core_axis_name="c", subcore_axis_name="s")

    @functools.partial(
        pl.kernel, mesh=mesh, out_type=jax.ShapeDtypeStruct((n, d), table.dtype),
        scratch_types=[pltpu.VMEM((per_w,), I32), pltpu.VMEM((2, c, d), table.dtype),
                       pltpu.SemaphoreType.DMA((2,)), pltpu.SemaphoreType.DMA((2,))])
    def gather_kernel(table_hbm, idx_hbm, out_hbm, idx_v, rows_v, gsem, wsem):
        base = (lax.axis_index("s") * n_cores + lax.axis_index("c")) * per_w
        pltpu.sync_copy(idx_hbm.at[pl.ds(base, per_w)], idx_v)

        def gather(g, b):
            return pltpu.make_async_copy(table_hbm.at[idx_v.at[pl.ds(g * c, c)]], rows_v.at[b],
                                         gsem.at[b])

        def write(g, b):
            return pltpu.make_async_copy(rows_v.at[b], out_hbm.at[pl.ds(base + g * c, c)], wsem.at[b])

        gather(0, 0).start()

        @pl.loop(0, n_g, step=2)
        def _ring(g0):
            for b in range(2):
                g = g0 + b

                @pl.when(g + 1 < n_g)
                def _next():
                    @pl.when(g >= 1)
                    def _buffer_free():
                        write(g - 1, 1 - b).wait()
                    gather(g + 1, 1 - b).start()

                gather(g, b).wait()
                write(g, b).start()

        write(n_g - 2, 0).wait()
        write(n_g - 1, 1).wait()

    return gather_kernel(table, idx)


def _moe_kernel(blk_e_ref, nused_ref, x_ref, wgu_ref, bgu_ref, wdn_ref, bdn_ref, o_ref,
                wgu_s, wdn_s, *, f):
    i = pl.program_id(0)
    nused = nused_ref[0]

    @pl.when(i < nused)
    def _compute():
        changed = jnp.logical_or(i == 0, blk_e_ref[i] != blk_e_ref[jnp.maximum(i - 1, 0)])

        @pl.when(changed)
        def _cast_weights():
            wgu_s[...] = wgu_ref[0].astype(BF16)
            wdn_s[...] = wdn_ref[0].astype(BF16)

        x = _unpack_bf16_pairs(x_ref[...]).astype(BF16)
        hgu = jnp.dot(x, wgu_s[...], preferred_element_type=F32) + bgu_ref[0]
        a = jnp.minimum(hgu[:, :f], SWIGLU_LIMIT)
        u = jnp.clip(hgu[:, f:], -SWIGLU_LIMIT, SWIGLU_LIMIT)
        glu = a * jax.nn.sigmoid(a * SWIGLU_ALPHA)
        y = jnp.dot(((u + 1.0) * glu).astype(BF16), wdn_s[...], preferred_element_type=F32) + bdn_ref[0]
        o_ref[...] = _pack_bf16_pairs(y)

    @pl.when(i >= nused)
    def _unused_block():
        o_ref[...] = jnp.zeros(o_ref.shape, o_ref.dtype)


def _moe_ffn(xs, blk_e, nused, w_gu, b_gu, w_dn, b_dn):
    P = xs.shape[0]
    E, D, F2 = w_gu.shape
    f = F2 // 2
    blk = MOE_BLK
    nb = P // blk
    used_block = lambda i, be, nu: (jnp.minimum(i, nu[0] - 1), 0)
    grid_spec = pltpu.PrefetchScalarGridSpec(
        num_scalar_prefetch=2,
        grid=(nb,),
        in_specs=[
            pl.BlockSpec((blk, D // 2), used_block),
            pl.BlockSpec((1, D, F2), lambda i, be, nu: (be[i], 0, 0)),
            pl.BlockSpec((1, 1, F2), lambda i, be, nu: (be[i], 0, 0)),
            pl.BlockSpec((1, f, D), lambda i, be, nu: (be[i], 0, 0)),
            pl.BlockSpec((1, 1, D), lambda i, be, nu: (be[i], 0, 0)),
        ],
        out_specs=pl.BlockSpec((blk, D // 2), lambda i, be, nu: (i, 0)),
        scratch_shapes=[
            pltpu.VMEM((D, F2), BF16),
            pltpu.VMEM((f, D), BF16),
        ],
    )
    return pl.pallas_call(
        functools.partial(_moe_kernel, f=f),
        grid_spec=grid_spec,
        out_shape=jax.ShapeDtypeStruct((P, D // 2), jnp.uint32),
        compiler_params=_params(("arbitrary",)),
    )(blk_e, nused, xs, w_gu, b_gu.reshape(E, 1, F2), w_dn, b_dn.reshape(E, 1, D))


def _comb_kernel(h_ref, p_ref, y_ref, g_ref, b_ref, o_ref, *, alpha):
    gate = p_ref[...]
    y = [_unpack_bf16_pairs(y_ref[k]) for k in range(TOP_K)]
    fsum = (y[0] * gate[:, 0:1] + y[1] * gate[:, 1:2]) + (y[2] * gate[:, 2:3] + y[3] * gate[:, 3:4])
    o_ref[...] = _layer_norm(alpha * h_ref[...] + fsum, g_ref[...], b_ref[...])


def _combine(h, y4, top_p, g, b, alpha):
    T, D = h.shape
    tm = COMB_TM
    return pl.pallas_call(
        functools.partial(_comb_kernel, alpha=alpha),
        grid=(T // tm,),
        in_specs=[
            pl.BlockSpec((tm, D), lambda i: (i, 0)),
            pl.BlockSpec((tm, TOP_K), lambda i: (i, 0)),
            pl.BlockSpec((TOP_K, tm, D // 2), lambda i: (0, i, 0)),
            pl.BlockSpec((1, D), lambda i: (0, 0)),
            pl.BlockSpec((1, D), lambda i: (0, 0)),
        ],
        out_specs=pl.BlockSpec((tm, D), lambda i: (i, 0)),
        out_shape=jax.ShapeDtypeStruct((T, D), F32),
        compiler_params=_params(("arbitrary",)),
    )(h, top_p.T, y4, g, b)


def _route(top_e, blk):
    K, T = top_e.shape
    N = K * T
    flat_e = top_e.reshape(N)
    experts = jnp.arange(N_EXPERTS, dtype=I32)
    order = jnp.argsort(flat_e, stable=True).astype(I32)
    inv = jnp.argsort(order).astype(I32)
    onehot = flat_e[:, None] == experts[None, :]
    counts = jnp.sum(onehot, axis=0, dtype=I32)
    padded = (counts + blk - 1) // blk * blk
    pends = jnp.cumsum(padded)
    offs = jnp.cumsum(counts) - counts
    shift = (pends - padded) - offs
    pos = inv + jnp.sum(jnp.where(onehot, shift[None, :], 0), axis=1, dtype=I32)
    P = N + N_EXPERTS * blk
    nb = P // blk
    blk_start = jnp.arange(nb, dtype=I32) * blk
    blk_e = jnp.minimum(jnp.sum(pends[None, :] <= blk_start[:, None], axis=1, dtype=I32), N_EXPERTS - 1)
    j = (blk_start - shift[blk_e])[:, None] + jnp.arange(blk, dtype=I32)[None, :]
    valid = j < (offs + counts)[blk_e][:, None]
    src = order[jnp.clip(j, 0, N - 1)]
    row_tok = jnp.where(valid, src % T, j % T).reshape(P)
    nused = (pends[-1:] // blk).astype(I32)
    return blk_e, nused, row_tok, pos.reshape(K, T)


def _projection_weights(w_in_l):
    sizes = (ATT_W, ATT_W, ATT_W, IDX_HEADS * IDX_DIM, IDX_DIM, IDX_HEADS, ATT_W, ATT_W, ATT_W)
    offs = np.concatenate([[0], np.cumsum(sizes)])
    qa, ka, va, qi, ki, wi, qb, kb, vb = (w_in_l[:, offs[n]:offs[n + 1]] for n in range(9))
    pad_wi = jnp.zeros((w_in_l.shape[0], LANES - IDX_HEADS), w_in_l.dtype)
    w_att = jnp.concatenate([qa, ka, qi, qb, kb, vb, ki, ki, wi, pad_wi], axis=1).astype(BF16)
    w_va_t = va.T.astype(BF16)
    w_gate = w_in_l[:, offs[9]:].astype(BF16)
    return w_att, w_va_t, w_gate


def kernel(x, w_in, w_branch_a, w_branch_b, w_out, rel_bias, ln1_g, ln1_b, w_router, b_router,
           w_gate_up, b_gate_up, w_down, b_down, ln2_g, ln2_b):
    B, S, D = x.shape
    depth = w_in.shape[0]
    alpha = (2 * depth) ** 0.25
    T = B * S
    h = x.reshape(T, D)
    for l in range(depth):
        w_att, w_va_t, w_gate = _projection_weights(w_in[l])
        proj, vt = _projection(h, w_att, w_va_t, min(T, PROJ_TM), PROJ_TN, DSA_BLK)
        proj = proj.reshape(B, S, ATT_COLS)
        vt = vt.reshape(B, S // DSA_BLK, ATT_W, DSA_BLK)
        ya = _dsa(proj, vt, rel_bias).reshape(T, ATT_W)
        yb = _stick_breaking(proj).reshape(T, ATT_W)
        h1, h1_packed, top_e, top_p = _merge(
            h, ya, yb, w_gate, w_branch_a[l].astype(BF16), w_branch_b[l].astype(BF16),
            w_out[l].astype(BF16), ln1_g[l].reshape(1, D), ln1_b[l].reshape(1, D),
            w_router[l].T, b_router[l].reshape(N_EXPERTS, 1), alpha)
        blk_e, nused, row_tok, pos = _route(top_e, MOE_BLK)
        xs = _sc_gather_rows(h1_packed, row_tok)
        ys = _moe_ffn(xs, blk_e, nused, w_gate_up[l], b_gate_up[l], w_down[l], b_down[l])
        y4 = _sc_gather_rows(ys, pos.reshape(TOP_K * T)).reshape(TOP_K, T, D // 2)
        h = _combine(h1, y4, top_p, ln2_g[l].reshape(1, D), ln2_b[l].reshape(1, D), alpha)
    return h.reshape(B, S, D)
```

```python
import functools
import math

import numpy as np
import jax
import jax.numpy as jnp
from jax import lax
from jax.experimental import pallas as pl
from jax.experimental.pallas import tpu as pltpu
from jax.experimental.pallas import tpu_sc as plsc

F32 = jnp.float32
BF16 = jnp.bfloat16
I32 = jnp.int32

A_HEADS = 8
HEAD_DIM = 64
ATT_W = A_HEADS * HEAD_DIM
IDX_HEADS = 8
IDX_DIM = 64
IDX_SCALE = (IDX_HEADS * IDX_DIM) ** -0.5
TOPK_MAX = 256
N_BUCKETS = 32
MAX_DISTANCE = 128
N_EXPERTS = 32
TOP_K = 4
SWIGLU_LIMIT = 7.0
SWIGLU_ALPHA = 1.702
LN_EPS = 1e-5
QK_SCALE = HEAD_DIM ** -0.5

LANES = 128
SUBLANES = 8
HALF = LANES // 2
N_PAIRS = A_HEADS // 2
VMEM_LIMIT = 56 * 1024 * 1024

DSA_BLK = 256
SB_T = 256
SB_PAIRS = 4
PROJ_TM = 1024
MERGE_TM = 512
MOE_BLK = 512
COMB_TM = 512
SC_GATHER_BYTES = 128 * 1024
REDUCE_CHAINS = 8
BISECT_CAP = 24
BISECT_FREE = 20
NEG = -1e30

COL_QA, COL_KA, COL_QI, COL_QB, COL_KB, COL_VB = (g * ATT_W for g in range(6))
COL_KK = 6 * ATT_W
COL_WI = COL_KK + LANES
ATT_COLS = COL_WI + LANES
PROJ_TN = ATT_COLS // 2

NT_DIMS = (((1,), (1,)), ((), ()))


def _params(sem, vmem=VMEM_LIMIT):
    return pltpu.CompilerParams(dimension_semantics=sem, vmem_limit_bytes=vmem)


def _proj_kernel(x_ref, w_ref, wt_ref, o_ref, ot_ref, xb_ref, *, tt):
    @pl.when(pl.program_id(1) == 0)
    def _row_tile_start():
        xb_ref[...] = x_ref[...].astype(BF16)
        for r in range(ot_ref.shape[0]):
            ot_ref[r] = lax.dot_general(wt_ref[...], xb_ref[r * tt:(r + 1) * tt, :], NT_DIMS,
                                        preferred_element_type=F32).astype(ot_ref.dtype)

    o_ref[...] = jnp.dot(xb_ref[...], w_ref[...], preferred_element_type=F32).astype(o_ref.dtype)


def _projection(x, w, w_t, tm, tn, tt):
    M, K = x.shape
    N = w.shape[1]
    Nt = w_t.shape[0]
    return pl.pallas_call(
        functools.partial(_proj_kernel, tt=tt),
        grid=(M // tm, N // tn),
        in_specs=[pl.BlockSpec((tm, K), lambda i, j: (i, 0)),
                  pl.BlockSpec((K, tn), lambda i, j: (0, j)),
                  pl.BlockSpec((Nt, K), lambda i, j: (0, 0))],
        out_specs=[pl.BlockSpec((tm, tn), lambda i, j: (i, j)),
                   pl.BlockSpec((tm // tt, Nt, tt), lambda i, j: (i, 0, 0))],
        out_shape=[jax.ShapeDtypeStruct((M, N), BF16),
                   jax.ShapeDtypeStruct((M // tt, Nt, tt), BF16)],
        scratch_shapes=[pltpu.VMEM((tm, K), BF16)],
        compiler_params=_params(("arbitrary", "arbitrary")),
    )(x, w, w_t)


def _t5_bucket_np(n):
    n = np.maximum(n, 0)
    max_exact = N_BUCKETS // 2
    nf = np.maximum(n, 1).astype(np.float32)
    large = max_exact + (np.log(nf / max_exact) / math.log(MAX_DISTANCE / max_exact)
                         * (N_BUCKETS - max_exact)).astype(np.int32)
    large = np.minimum(large, N_BUCKETS - 1)
    return np.where(n < max_exact, n, large).astype(np.int32)


def _dsa_n_off(blk):
    return 2 + -(-MAX_DISTANCE // blk)


def _dsa_bucket_tiles(blk):
    n_off = _dsa_n_off(blk)
    j = np.arange(blk)[None, :, None]
    i = np.arange(blk)[None, None, :]
    o = np.arange(n_off)[:, None, None]
    return _t5_bucket_np(i - j + blk * (n_off - 1 - o))


def _dsa_kernel(bucket_ref, relb_ref, q_ref, k_ref, vt_ref, qi_ref, kk_ref, wi_ref, o_ref,
                sc_ref, bias_ref, m_ref, l_ref, acc_ref, st_ref, mx_ref,
                *, blk, n_sel, n_off):
    b = pl.program_id(0)
    i = pl.program_id(1)
    q0 = i * blk
    nck = i + 1
    groups = blk // SUBLANES

    @pl.when(jnp.logical_and(b == 0, i == 0))
    def _build_bias():
        def head_body(h, _):
            for o in range(n_off):
                for rb in range(blk // LANES):
                    for cb in range(blk // LANES):
                        rs = slice(rb * LANES, (rb + 1) * LANES)
                        cs = slice(cb * LANES, (cb + 1) * LANES)
                        bk = bucket_ref[o, rs, cs]

                        def bucket_body(n, acc):
                            return jnp.where(bk == n, relb_ref[n, h], acc)

                        bias_ref[h, o, rs, cs] = lax.fori_loop(
                            0, N_BUCKETS, bucket_body, jnp.zeros((LANES, LANES), F32))
            return 0

        lax.fori_loop(0, A_HEADS, head_body, 0)

    lane = lax.broadcasted_iota(I32, (blk, LANES), 1)
    lo_half = lane < HALF
    krow = lax.broadcasted_iota(I32, (blk, blk), 0)
    qpos = q0 + lax.broadcasted_iota(I32, (1, blk), 1)

    def pair_split(ref, scale):
        out = []
        for p in range(N_PAIRS):
            v = ref[0, :, p * LANES:(p + 1) * LANES].astype(F32)
            if scale != 1.0:
                v = v * scale
            out.append(jnp.where(lo_half, v, 0.0).astype(BF16))
            out.append(jnp.where(lo_half, 0.0, v).astype(BF16))
        return out

    wi_t = wi_ref[0].astype(F32).T
    wrow = [wi_t[h:h + 1, :] * IDX_SCALE for h in range(IDX_HEADS)]
    qi_m = pair_split(qi_ref, 1.0)

    def part(x, op):
        y = op(x.reshape(REDUCE_CHAINS, groups // REDUCE_CHAINS, SUBLANES, blk), axis=1)
        return op(y, axis=0)

    def fin(x, op):
        return op(x, axis=0, keepdims=True)

    zeros8 = jnp.zeros((SUBLANES, blk), F32)
    pinf8 = jnp.full((SUBLANES, blk), jnp.inf, F32)

    def score_chunk(c, mn_mx):
        c0 = pl.multiple_of(c * blk, blk)
        kk = kk_ref[0, pl.ds(c0, blk), :]
        acc = jnp.zeros((blk, blk), F32)
        for h in range(IDX_HEADS):
            s = lax.dot_general(kk, qi_m[h], NT_DIMS, preferred_element_type=F32)
            acc = acc + wrow[h] * jnp.maximum(s, 0.0)
        causal = c0 + krow <= qpos
        sc_ref[c] = jnp.where(causal, acc, -jnp.inf)
        return (jnp.minimum(mn_mx[0], part(jnp.where(causal, acc, jnp.inf), jnp.min)),
                jnp.maximum(mn_mx[1], part(jnp.where(causal, acc, -jnp.inf), jnp.max)))

    mn, mx = lax.fori_loop(0, nck, score_chunk, (pinf8, -pinf8))
    rmin = fin(mn, jnp.min)
    rmax = fin(mx, jnp.max)

    kt = jnp.minimum(qpos + 1, n_sel).astype(F32)

    def fold(fn, init):
        def body(c, acc):
            return fn(acc, sc_ref[c])
        return lax.fori_loop(0, nck, body, init)

    def count_ge(th):
        return fin(fold(lambda a, s: a + part(jnp.where(s >= th, 1.0, 0.0), jnp.sum), zeros8),
                   jnp.sum)

    def bis_cond(st):
        it, lo, hi, clo = st
        return jnp.logical_and(it < BISECT_CAP, jnp.max(jnp.abs(clo - kt)) > 0.0)

    def halve(lo, hi, clo):
        mid = 0.5 * lo + 0.5 * hi
        c = count_ge(mid)
        active = clo != kt
        up = jnp.logical_and(active, c >= kt)
        dn = jnp.logical_and(active, c < kt)
        return jnp.where(up, mid, lo), jnp.where(dn, mid, hi), jnp.where(up, c, clo)

    def bis_body(st):
        it, lo, hi, clo = st
        return (it + 2,) + halve(*halve(lo, hi, clo))

    start = (rmin, rmax + jnp.maximum(1.0, jnp.abs(rmax) * 2.0 ** -20), (qpos + 1).astype(F32))
    start = lax.fori_loop(0, BISECT_FREE, lambda _, st: halve(*st), start)
    _, lo, _, clo = lax.while_loop(bis_cond, bis_body, (jnp.int32(BISECT_FREE),) + start)
    open_rows = jnp.max(jnp.abs(clo - kt)) > 0.0

    def stats(lo_):
        a_ = fin(fold(lambda a, s: jnp.minimum(a, part(jnp.where(s >= lo_, s, jnp.inf), jnp.min)),
                      pinf8), jnp.min)
        cg, ct, nx = fold(
            lambda a, s: (a[0] + part(jnp.where(s > a_, 1.0, 0.0), jnp.sum),
                               a[1] + part(jnp.where(s == a_, 1.0, 0.0), jnp.sum),
                               jnp.minimum(a[2], part(jnp.where(s > a_, s, jnp.inf), jnp.min))),
            (zeros8, zeros8, pinf8))
        return a_, fin(cg, jnp.sum), fin(ct, jnp.sum), fin(nx, jnp.min)

    def fin_cond(st):
        return st[0]

    def fin_body(st):
        _, lo_, _, _ = st
        a_, cgt_, nt_, nxt_ = stats(lo_)
        bad = cgt_ >= kt
        return (jnp.max(jnp.where(bad, 1.0, 0.0)) > 0.0, jnp.where(bad, nxt_, a_), cgt_, nt_)

    def exact_finish():
        _, a_, cgt, nties = lax.while_loop(fin_cond, fin_body, (jnp.bool_(True), lo, kt, kt))
        need_ = kt - cgt
        return a_, need_, jnp.max(jnp.where(nties > need_, 1.0, 0.0)) > 0.0

    a, need, excess = lax.cond(open_rows, exact_finish, lambda: (lo, kt, jnp.bool_(False)))

    def mask_plain():
        def body(c, _):
            sc_ref[c] = jnp.where(sc_ref[c] >= a, 0.0, NEG)
            return 0
        lax.fori_loop(0, nck, body, 0)

    def mask_ties():
        upto = (krow >= lax.broadcasted_iota(I32, (blk, blk), 1)).astype(BF16)

        def body(c, seen):
            s = sc_ref[c]
            tie = s == a
            rank = jnp.dot(upto, jnp.where(tie, 1.0, 0.0).astype(BF16),
                           preferred_element_type=F32) + seen
            sel = jnp.logical_or(s > a, jnp.logical_and(tie, rank <= need))
            sc_ref[c] = jnp.where(sel, 0.0, NEG)
            return rank[blk - 1:blk, :]

        lax.fori_loop(0, nck, body, jnp.zeros((1, blk), F32))

    lax.cond(excess, mask_ties, mask_plain)

    m_ref[...] = jnp.full(m_ref.shape, NEG, F32)
    l_ref[...] = jnp.zeros(l_ref.shape, F32)
    acc_ref[...] = jnp.zeros(acc_ref.shape, F32)
    q_m = pair_split(q_ref, QK_SCALE)

    def stage_logits(c, slot):
        c0 = pl.multiple_of(c * blk, blk)
        madd = sc_ref[c]
        o_idx = jnp.maximum(c - i + (n_off - 1), 0)
        for p in range(N_PAIRS):
            k2 = k_ref[0, pl.ds(c0, blk), p * LANES:(p + 1) * LANES]
            for hh in range(2):
                h = 2 * p + hh
                s = lax.dot_general(k2, q_m[h], NT_DIMS, preferred_element_type=F32)
                s = s + bias_ref[h, o_idx] + madd
                st_ref[slot, h] = s
                mx_ref[slot, h] = fin(part(s, jnp.max), jnp.max)

    def stage_values(c, slot):
        for p in range(N_PAIRS):
            vt2 = vt_ref[0, c, p * LANES:(p + 1) * LANES, :]
            for hh in range(2):
                h = 2 * p + hh
                m_old = m_ref[h]
                m_new = jnp.maximum(m_old, mx_ref[slot, h])
                alpha = jnp.exp(m_old - m_new)
                pexp = jnp.exp(st_ref[slot, h] - m_new)
                l_ref[h] = alpha * l_ref[h] + fin(part(pexp, jnp.sum), jnp.sum)
                acc_ref[h] = alpha * acc_ref[h] + jnp.dot(vt2, pexp.astype(BF16),
                                                          preferred_element_type=F32)
                m_ref[h] = m_new

    stage_logits(0, 0)

    def att_pair(pp, _):
        c = 2 * pp
        stage_logits(c + 1, 1)
        stage_values(c, 0)
        stage_logits(c + 2, 0)
        stage_values(c + 1, 1)
        return 0

    lax.fori_loop(0, (nck - 1) // 2, att_pair, 0)

    @pl.when(lax.rem(nck, 2) == 1)
    def _last_chunk():
        stage_values(nck - 1, 0)

    @pl.when(lax.rem(nck, 2) == 0)
    def _last_two_chunks():
        stage_logits(nck - 1, 1)
        stage_values(nck - 2, 0)
        stage_values(nck - 1, 1)

    lo_rows = lax.broadcasted_iota(I32, (LANES, blk), 0) < HALF
    for p in range(N_PAIRS):
        oa = acc_ref[2 * p] / l_ref[2 * p]
        ob = acc_ref[2 * p + 1] / l_ref[2 * p + 1]
        o_ref[0, :, p * LANES:(p + 1) * LANES] = jnp.where(lo_rows, oa, ob).T.astype(o_ref.dtype)


def _dsa(proj3, vt4, rel_bias):
    B, S, _ = proj3.shape
    blk = DSA_BLK
    n_off = _dsa_n_off(blk)
    n_sel = min(TOPK_MAX, S // 4)
    bucket = jnp.asarray(_dsa_bucket_tiles(blk))
    assert S % blk == 0
    kern = functools.partial(_dsa_kernel, blk=blk, n_sel=n_sel, n_off=n_off)
    return pl.pallas_call(
        kern,
        grid=(B, S // blk),
        in_specs=[
            pl.BlockSpec((n_off, blk, blk), lambda b, i: (0, 0, 0)),
            pl.BlockSpec(memory_space=pltpu.SMEM),
            pl.BlockSpec((1, blk, ATT_W), lambda b, i: (b, i, COL_QA // ATT_W)),
            pl.BlockSpec((1, S, ATT_W), lambda b, i: (b, 0, COL_KA // ATT_W)),
            pl.BlockSpec((1, S // blk, ATT_W, blk), lambda b, i: (b, 0, 0, 0)),
            pl.BlockSpec((1, blk, ATT_W), lambda b, i: (b, i, COL_QI // ATT_W)),
            pl.BlockSpec((1, S, LANES), lambda b, i: (b, 0, COL_KK // LANES)),
            pl.BlockSpec((1, blk, LANES), lambda b, i: (b, i, COL_WI // LANES)),
        ],
        out_specs=pl.BlockSpec((1, blk, ATT_W), lambda b, i: (b, i, 0)),
        out_shape=jax.ShapeDtypeStruct((B, S, ATT_W), BF16),
        scratch_shapes=[
            pltpu.VMEM((S // blk, blk, blk), F32),
            pltpu.VMEM((A_HEADS, n_off, blk, blk), F32),
            pltpu.VMEM((A_HEADS, 1, blk), F32),
            pltpu.VMEM((A_HEADS, 1, blk), F32),
            pltpu.VMEM((A_HEADS, LANES, blk), F32),
            pltpu.VMEM((2, A_HEADS, blk, blk), F32),
            pltpu.VMEM((2, A_HEADS, 1, blk), F32),
        ],
        compiler_params=_params(("arbitrary", "arbitrary")),
    )(bucket, rel_bias, proj3, proj3, vt4, proj3, proj3, proj3)


def _sb_kernel(q_ref, k_ref, v_ref, o_ref, hl_ref, z_ref, *, t):
    i = pl.program_id(2)
    n = i + 1
    lane = lax.broadcasted_iota(I32, (t, LANES), 1)
    lo_half = lane < HALF
    q_m = []
    for pr in range(SB_PAIRS):
        q2 = q_ref[0, :, pr * LANES:(pr + 1) * LANES].astype(F32) * QK_SCALE
        q_m += [jnp.where(lo_half, q2, 0.0).astype(BF16), jnp.where(lo_half, 0.0, q2).astype(BF16)]
    heads = 2 * SB_PAIRS
    r = lax.broadcasted_iota(I32, (t, t), 0)
    cidx = lax.broadcasted_iota(I32, (t, t), 1)
    neg_from = jnp.where(r >= cidx, -1.0, 0.0).astype(BF16)
    neg_from2 = jnp.concatenate([neg_from, neg_from], axis=0)
    diff = cidx - r

    def stage_terms(step, slot, diagonal):
        c0 = pl.multiple_of((i - step) * t, t)
        if diagonal:
            keep = diff < 0
        for hh in range(heads):
            pr = hh // 2
            k2 = k_ref[0, pl.ds(c0, t), pr * LANES:(pr + 1) * LANES]
            z = lax.dot_general(q_m[hh], k2, NT_DIMS, preferred_element_type=F32)
            sp = jnp.maximum(z, 0.0) + jnp.log(1.0 + jnp.exp(-jnp.abs(z)))
            if diagonal:
                sp = jnp.where(keep, sp, 0.0)
                z = jnp.where(keep, z, NEG)
            hi = sp.astype(BF16)
            hl_ref[slot, hh, :, :t] = hi
            hl_ref[slot, hh, :, t:] = (sp - hi.astype(F32)).astype(BF16)
            z_ref[slot, hh] = z

    def stage_apply(step, slot, carry):
        c0 = pl.multiple_of((i - step) * t, t)
        cum_all = jnp.dot(hl_ref[slot].reshape(heads * t, 2 * t), neg_from2, preferred_element_type=F32)
        out = []
        for hh in range(heads):
            pr = hh // 2
            v2 = v_ref[0, pl.ds(c0, t), pr * LANES:(pr + 1) * LANES]
            car, acc = carry[hh]
            cum = cum_all[hh * t:(hh + 1) * t]
            w = jnp.exp(z_ref[slot, hh] + cum + car)
            acc = acc + jnp.dot(w.astype(BF16), v2, preferred_element_type=F32)
            out.append((car + cum[:, 0:1], acc))
        return tuple(out)

    z1 = jnp.zeros((t, 1), F32)
    za = jnp.zeros((t, LANES), F32)
    stage_terms(0, 0, True)

    def pair_body(pp, carry):
        step = 2 * pp
        stage_terms(step + 1, 1, False)
        carry = stage_apply(step, 0, carry)
        stage_terms(step + 2, 0, False)
        return stage_apply(step + 1, 1, carry)

    carry = lax.fori_loop(0, (n - 1) // 2, pair_body, ((z1, za),) * heads)

    def last_two(c):
        stage_terms(n - 1, 1, False)
        return stage_apply(n - 1, 1, stage_apply(n - 2, 0, c))

    carry = lax.cond(lax.rem(n, 2) == 1, lambda c: stage_apply(n - 1, 0, c), last_two, carry)
    for pr in range(SB_PAIRS):
        o_ref[0, :, pr * LANES:(pr + 1) * LANES] = jnp.where(
            lo_half, carry[2 * pr][1], carry[2 * pr + 1][1]).astype(o_ref.dtype)


def _stick_breaking(proj3):
    B, S, _ = proj3.shape
    t = SB_T
    w = SB_PAIRS * LANES
    qb, kb, vb = COL_QB // w, COL_KB // w, COL_VB // w
    return pl.pallas_call(
        functools.partial(_sb_kernel, t=t),
        grid=(B, N_PAIRS // SB_PAIRS, S // t),
        in_specs=[
            pl.BlockSpec((1, t, w), lambda b, p, i: (b, i, qb + p)),
            pl.BlockSpec((1, S, w), lambda b, p, i: (b, 0, kb + p)),
            pl.BlockSpec((1, S, w), lambda b, p, i: (b, 0, vb + p)),
        ],
        out_specs=pl.BlockSpec((1, t, w), lambda b, p, i: (b, i, p)),
        out_shape=jax.ShapeDtypeStruct((B, S, ATT_W), BF16),
        scratch_shapes=[
            pltpu.VMEM((2, 2 * SB_PAIRS, t, 2 * t), BF16),
            pltpu.VMEM((2, 2 * SB_PAIRS, t, t), F32),
        ],
        compiler_params=_params(("arbitrary", "arbitrary", "arbitrary")),
    )(proj3, proj3, proj3)


def _layer_norm(r, g, b):
    mu = jnp.mean(r, axis=-1, keepdims=True)
    d = r - mu
    var = jnp.mean(d * d, axis=-1, keepdims=True)
    return d * lax.rsqrt(var + LN_EPS) * g + b


def _split_bf16(v):
    hi = v.astype(BF16)
    return hi, (v - hi.astype(F32)).astype(BF16)


def _pack_bf16_pairs(v):
    half = v.shape[1] // 2
    bits = pltpu.bitcast(v.astype(BF16).astype(F32), jnp.uint32)
    return bits[:, :half] | (bits[:, half:] >> 16)


def _unpack_bf16_pairs(w):
    return jnp.concatenate([pltpu.bitcast(w & jnp.uint32(0xFFFF0000), F32),
                            pltpu.bitcast(w << 16, F32)], axis=1)


def _merge_kernel(x_ref, ya_ref, yb_ref, wg_ref, wa_ref, wb_ref, wo_ref, g_ref, b_ref,
                  wr_ref, br_ref, h_ref, hp_ref, e_ref, p_ref, *, alpha, d):
    pa = jnp.dot(ya_ref[...], wa_ref[...], preferred_element_type=F32)
    pb = jnp.dot(yb_ref[...], wb_ref[...], preferred_element_type=F32)
    gates = jnp.dot(x_ref[...].astype(BF16), wg_ref[...], preferred_element_type=F32)
    merged = jax.nn.sigmoid(gates[:, :d]) * pa + jax.nn.sigmoid(gates[:, d:]) * pb
    m = jnp.dot(merged.astype(BF16), wo_ref[...], preferred_element_type=F32)
    h = _layer_norm(alpha * x_ref[...] + m, g_ref[...], b_ref[...])
    h_ref[...] = h
    hp_ref[...] = _pack_bf16_pairs(h)

    h_hi, h_lo = _split_bf16(h)
    w_hi, w_lo = _split_bf16(wr_ref[...])
    logit = (lax.dot_general(w_hi, h_hi, NT_DIMS, preferred_element_type=F32)
             + lax.dot_general(w_hi, h_lo, NT_DIMS, preferred_element_type=F32)
             + lax.dot_general(w_lo, h_hi, NT_DIMS, preferred_element_type=F32)) + br_ref[...]
    eid = lax.broadcasted_iota(I32, logit.shape, 0)
    vals, ids = [], []
    for _ in range(TOP_K):
        mx = jnp.max(logit, axis=0, keepdims=True)
        am = jnp.min(jnp.where(logit == mx, eid, N_EXPERTS), axis=0, keepdims=True)
        vals.append(mx)
        ids.append(am)
        logit = jnp.where(eid == am, -jnp.inf, logit)
    ex = [jnp.exp(v - vals[0]) for v in vals]
    den = ex[0] + ex[1] + ex[2] + ex[3]
    for k in range(TOP_K):
        e_ref[k:k + 1, :] = ids[k]
        p_ref[k:k + 1, :] = ex[k] / den


def _merge(x2, ya, yb, wg, wa, wb, wo, g, b, wr_t, br, alpha):
    T, D = x2.shape
    tm = MERGE_TM
    row = lambda i: (i, 0)
    fixed = lambda i: (0, 0)
    return pl.pallas_call(
        functools.partial(_merge_kernel, alpha=alpha, d=D),
        grid=(T // tm,),
        in_specs=[
            pl.BlockSpec((tm, D), row),
            pl.BlockSpec((tm, ATT_W), row),
            pl.BlockSpec((tm, ATT_W), row),
            pl.BlockSpec((D, 2 * D), fixed),
            pl.BlockSpec((ATT_W, D), fixed),
            pl.BlockSpec((ATT_W, D), fixed),
            pl.BlockSpec((D, D), fixed),
            pl.BlockSpec((1, D), fixed),
            pl.BlockSpec((1, D), fixed),
            pl.BlockSpec((N_EXPERTS, D), fixed),
            pl.BlockSpec((N_EXPERTS, 1), fixed),
        ],
        out_specs=[
            pl.BlockSpec((tm, D), row),
            pl.BlockSpec((tm, D // 2), row),
            pl.BlockSpec((TOP_K, tm), lambda i: (0, i)),
            pl.BlockSpec((TOP_K, tm), lambda i: (0, i)),
        ],
        out_shape=[
            jax.ShapeDtypeStruct((T, D), F32),
            jax.ShapeDtypeStruct((T, D // 2), jnp.uint32),
            jax.ShapeDtypeStruct((TOP_K, T), I32),
            jax.ShapeDtypeStruct((TOP_K, T), F32),
        ],
        compiler_params=_params(("arbitrary",)),
    )(x2, ya, yb, wg, wa, wb, wo, g, b, wr_t, br)


def _sc_gather_rows(table, idx):
    n = idx.shape[0]
    d = table.shape[1]
    info = plsc.get_sparse_core_info()
    n_cores, n_sub = info.num_cores, info.num_subcores
    per_w = n // (n_cores * n_sub)
    c = SC_GATHER_BYTES // (d * table.dtype.itemsize)
    n_g = per_w // c
    assert n == per_w * n_cores * n_sub and per_w == n_g * c and n_g % 2 == 0 and n_g >= 2
    mesh = plsc.VectorSubcoreMesh(core_axis_name="c", subcore_axis_name="s")

    @functools.partial(
        pl.kernel, mesh=mesh, out_type=jax.ShapeDtypeStruct((n, d), table.dtype),
        scratch_types=[pltpu.VMEM((per_w,), I32), pltpu.VMEM((2, c, d), table.dtype),
                       pltpu.SemaphoreType.DMA((2,)), pltpu.SemaphoreType.DMA((2,))])
    def gather_kernel(table_hbm, idx_hbm, out_hbm, idx_v, rows_v, gsem, wsem):
        base = (lax.axis_index("s") * n_cores + lax.axis_index("c")) * per_w
        pltpu.sync_copy(idx_hbm.at[pl.ds(base, per_w)], idx_v)

        def gather(g, b):
            return pltpu.make_async_copy(table_hbm.at[idx_v.at[pl.ds(g * c, c)]], rows_v.at[b],
                                         gsem.at[b])

        def write(g, b):
            return pltpu.make_async_copy(rows_v.at[b], out_hbm.at[pl.ds(base + g * c, c)], wsem.at[b])

        gather(0, 0).start()

        @pl.loop(0, n_g, step=2)
        def _ring(g0):
            for b in range(2):
                g = g0 + b

                @pl.when(g + 1 < n_g)
                def _next():
                    @pl.when(g >= 1)
                    def _buffer_free():
                        write(g - 1, 1 - b).wait()
                    gather(g + 1, 1 - b).start()

                gather(g, b).wait()
                write(g, b).start()

        write(n_g - 2, 0).wait()
        write(n_g - 1, 1).wait()

    return gather_kernel(table, idx)


def _moe_kernel(blk_e_ref, nused_ref, x_ref, wgu_ref, bgu_ref, wdn_ref, bdn_ref, o_ref,
                wgu_s, wdn_s, *, f):
    i = pl.program_id(0)
    nused = nused_ref[0]

    @pl.when(i < nused)
    def _compute():
        changed = jnp.logical_or(i == 0, blk_e_ref[i] != blk_e_ref[jnp.maximum(i - 1, 0)])

        @pl.when(changed)
        def _cast_weights():
            wgu_s[...] = wgu_ref[0].astype(BF16)
            wdn_s[...] = wdn_ref[0].astype(BF16)

        x = _unpack_bf16_pairs(x_ref[...]).astype(BF16)
        hgu = jnp.dot(x, wgu_s[...], preferred_element_type=F32) + bgu_ref[0]
        a = jnp.minimum(hgu[:, :f], SWIGLU_LIMIT)
        u = jnp.clip(hgu[:, f:], -SWIGLU_LIMIT, SWIGLU_LIMIT)
        glu = a * jax.nn.sigmoid(a * SWIGLU_ALPHA)
        y = jnp.dot(((u + 1.0) * glu).astype(BF16), wdn_s[...], preferred_element_type=F32) + bdn_ref[0]
        o_ref[...] = _pack_bf16_pairs(y)

    @pl.when(i >= nused)
    def _unused_block():
        o_ref[...] = jnp.zeros(o_ref.shape, o_ref.dtype)


def _moe_ffn(xs, blk_e, nused, w_gu, b_gu, w_dn, b_dn):
    P = xs.shape[0]
    E, D, F2 = w_gu.shape
    f = F2 // 2
    blk = MOE_BLK
    nb = P // blk
    used_block = lambda i, be, nu: (jnp.minimum(i, nu[0] - 1), 0)
    grid_spec = pltpu.PrefetchScalarGridSpec(
        num_scalar_prefetch=2,
        grid=(nb,),
        in_specs=[
            pl.BlockSpec((blk, D // 2), used_block),
            pl.BlockSpec((1, D, F2), lambda i, be, nu: (be[i], 0, 0)),
            pl.BlockSpec((1, 1, F2), lambda i, be, nu: (be[i], 0, 0)),
            pl.BlockSpec((1, f, D), lambda i, be, nu: (be[i], 0, 0)),
            pl.BlockSpec((1, 1, D), lambda i, be, nu: (be[i], 0, 0)),
        ],
        out_specs=pl.BlockSpec((blk, D // 2), lambda i, be, nu: (i, 0)),
        scratch_shapes=[
            pltpu.VMEM((D, F2), BF16),
            pltpu.VMEM((f, D), BF16),
        ],
    )
    return pl.pallas_call(
        functools.partial(_moe_kernel, f=f),
        grid_spec=grid_spec,
        out_shape=jax.ShapeDtypeStruct((P, D // 2), jnp.uint32),
        compiler_params=_params(("arbitrary",)),
    )(blk_e, nused, xs, w_gu, b_gu.reshape(E, 1, F2), w_dn, b_dn.reshape(E, 1, D))


def _comb_kernel(h_ref, p_ref, y_ref, g_ref, b_ref, o_ref, *, alpha):
    gate = p_ref[...]
    y = [_unpack_bf16_pairs(y_ref[k]) for k in range(TOP_K)]
    fsum = (y[0] * gate[:, 0:1] + y[1] * gate[:, 1:2]) + (y[2] * gate[:, 2:3] + y[3] * gate[:, 3:4])
    o_ref[...] = _layer_norm(alpha * h_ref[...] + fsum, g_ref[...], b_ref[...])


def _combine(h, y4, top_p, g, b, alpha):
    T, D = h.shape
    tm = COMB_TM
    return pl.pallas_call(
        functools.partial(_comb_kernel, alpha=alpha),
        grid=(T // tm,),
        in_specs=[
            pl.BlockSpec((tm, D), lambda i: (i, 0)),
            pl.BlockSpec((tm, TOP_K), lambda i: (i, 0)),
            pl.BlockSpec((TOP_K, tm, D // 2), lambda i: (0, i, 0)),
            pl.BlockSpec((1, D), lambda i: (0, 0)),
            pl.BlockSpec((1, D), lambda i: (0, 0)),
        ],
        out_specs=pl.BlockSpec((tm, D), lambda i: (i, 0)),
        out_shape=jax.ShapeDtypeStruct((T, D), F32),
        compiler_params=_params(("arbitrary",)),
    )(h, top_p.T, y4, g, b)


def _route(top_e, blk):
    K, T = top_e.shape
    N = K * T
    flat_e = top_e.reshape(N)
    experts = jnp.arange(N_EXPERTS, dtype=I32)
    order = jnp.argsort(flat_e, stable=True).astype(I32)
    inv = jnp.argsort(order).astype(I32)
    onehot = flat_e[:, None] == experts[None, :]
    counts = jnp.sum(onehot, axis=0, dtype=I32)
    padded = (counts + blk - 1) // blk * blk
    pends = jnp.cumsum(padded)
    offs = jnp.cumsum(counts) - counts
    shift = (pends - padded) - offs
    pos = inv + jnp.sum(jnp.where(onehot, shift[None, :], 0), axis=1, dtype=I32)
    P = N + N_EXPERTS * blk
    nb = P // blk
    blk_start = jnp.arange(nb, dtype=I32) * blk
    blk_e = jnp.minimum(jnp.sum(pends[None, :] <= blk_start[:, None], axis=1, dtype=I32), N_EXPERTS - 1)
    j = (blk_start - shift[blk_e])[:, None] + jnp.arange(blk, dtype=I32)[None, :]
    valid = j < (offs + counts)[blk_e][:, None]
    src = order[jnp.clip(j, 0, N - 1)]
    row_tok = jnp.where(valid, src % T, j % T).reshape(P)
    nused = (pends[-1:] // blk).astype(I32)
    return blk_e, nused, row_tok, pos.reshape(K, T)


def _projection_weights(w_in_l):
    sizes = (ATT_W, ATT_W, ATT_W, IDX_HEADS * IDX_DIM, IDX_DIM, IDX_HEADS, ATT_W, ATT_W, ATT_W)
    offs = np.concatenate([[0], np.cumsum(sizes)])
    qa, ka, va, qi, ki, wi, qb, kb, vb = (w_in_l[:, offs[n]:offs[n + 1]] for n in range(9))
    pad_wi = jnp.zeros((w_in_l.shape[0], LANES - IDX_HEADS), w_in_l.dtype)
    w_att = jnp.concatenate([qa, ka, qi, qb, kb, vb, ki, ki, wi, pad_wi], axis=1).astype(BF16)
    w_va_t = va.T.astype(BF16)
    w_gate = w_in_l[:, offs[9]:].astype(BF16)
    return w_att, w_va_t, w_gate


def kernel(x, w_in, w_branch_a, w_branch_b, w_out, rel_bias, ln1_g, ln1_b, w_router, b_router,
           w_gate_up, b_gate_up, w_down, b_down, ln2_g, ln2_b):
    B, S, D = x.shape
    depth = w_in.shape[0]
    alpha = (2 * depth) ** 0.25
    T = B * S
    h = x.reshape(T, D)
    for l in range(depth):
        w_att, w_va_t, w_gate = _projection_weights(w_in[l])
        proj, vt = _projection(h, w_att, w_va_t, min(T, PROJ_TM), PROJ_TN, DSA_BLK)
        proj = proj.reshape(B, S, ATT_COLS)
        vt = vt.reshape(B, S // DSA_BLK, ATT_W, DSA_BLK)
        ya = _dsa(proj, vt, rel_bias).reshape(T, ATT_W)
        yb = _stick_breaking(proj).reshape(T, ATT_W)
        h1, h1_packed, top_e, top_p = _merge(
            h, ya, yb, w_gate, w_branch_a[l].astype(BF16), w_branch_b[l].astype(BF16),
            w_out[l].astype(BF16), ln1_g[l].reshape(1, D), ln1_b[l].reshape(1, D),
            w_router[l].T, b_router[l].reshape(N_EXPERTS, 1), alpha)
        blk_e, nused, row_tok, pos = _route(top_e, MOE_BLK)
        xs = _sc_gather_rows(h1_packed, row_tok)
        ys = _moe_ffn(xs, blk_e, nused, w_gate_up[l], b_gate_up[l], w_down[l], b_down[l])
        y4 = _sc_gather_rows(ys, pos.reshape(TOP_K * T)).reshape(TOP_K, T, D // 2)
        h = _combine(h1, y4, top_p, ln2_g[l].reshape(1, D), ln2_b[l].reshape(1, D), alpha)
    return h.reshape(B, S, D)
```

```python
import functools
import math

import numpy as np
import jax
import jax.numpy as jnp
from jax import lax
from jax.experimental import pallas as pl
from jax.experimental.pallas import tpu as pltpu
from jax.experimental.pallas import tpu_sc as plsc

F32 = jnp.float32
BF16 = jnp.bfloat16
I32 = jnp.int32

A_HEADS = 8
HEAD_DIM = 64
ATT_W = A_HEADS * HEAD_DIM
IDX_HEADS = 8
IDX_DIM = 64
IDX_SCALE = (IDX_HEADS * IDX_DIM) ** -0.5
TOPK_MAX = 256
N_BUCKETS = 32
MAX_DISTANCE = 128
N_EXPERTS = 32
TOP_K = 4
SWIGLU_LIMIT = 7.0
SWIGLU_ALPHA = 1.702
LN_EPS = 1e-5
QK_SCALE = HEAD_DIM ** -0.5

LANES = 128
SUBLANES = 8
HALF = LANES // 2
N_PAIRS = A_HEADS // 2
VMEM_LIMIT = 56 * 1024 * 1024

DSA_BLK = 256
SB_T = 256
SB_PAIRS = 4
PROJ_TM = 1024
MERGE_TM = 512
MOE_BLK = 512
FFN_HIDDEN_SLABS = 2
COMB_TM = 512
SC_GATHER_BYTES = 128 * 1024
REDUCE_CHAINS = 8
BISECT_CAP = 24
BISECT_FREE = 20
NEG = -1e30

COL_QA, COL_KA, COL_QI, COL_QB, COL_KB, COL_VB = (g * ATT_W for g in range(6))
COL_KK = 6 * ATT_W
COL_WI = COL_KK + LANES
ATT_COLS = COL_WI + LANES
PROJ_TN = ATT_COLS // 2

NT_DIMS = (((1,), (1,)), ((), ()))


def _params(sem, vmem=VMEM_LIMIT):
    return pltpu.CompilerParams(dimension_semantics=sem, vmem_limit_bytes=vmem)


def _proj_kernel(x_ref, w_ref, wt_ref, o_ref, ot_ref, xb_ref, *, tt):
    @pl.when(pl.program_id(1) == 0)
    def _row_tile_start():
        xb_ref[...] = x_ref[...].astype(BF16)
        for r in range(ot_ref.shape[0]):
            ot_ref[r] = lax.dot_general(wt_ref[...], xb_ref[r * tt:(r + 1) * tt, :], NT_DIMS,
                                        preferred_element_type=F32).astype(ot_ref.dtype)

    o_ref[...] = jnp.dot(xb_ref[...], w_ref[...], preferred_element_type=F32).astype(o_ref.dtype)


def _projection(x, w, w_t, tm, tn, tt):
    M, K = x.shape
    N = w.shape[1]
    Nt = w_t.shape[0]
    return pl.pallas_call(
        functools.partial(_proj_kernel, tt=tt),
        grid=(M // tm, N // tn),
        in_specs=[pl.BlockSpec((tm, K), lambda i, j: (i, 0)),
                  pl.BlockSpec((K, tn), lambda i, j: (0, j)),
                  pl.BlockSpec((Nt, K), lambda i, j: (0, 0))],
        out_specs=[pl.BlockSpec((tm, tn), lambda i, j: (i, j)),
                   pl.BlockSpec((tm // tt, Nt, tt), lambda i, j: (i, 0, 0))],
        out_shape=[jax.ShapeDtypeStruct((M, N), BF16),
                   jax.ShapeDtypeStruct((M // tt, Nt, tt), BF16)],
        scratch_shapes=[pltpu.VMEM((tm, K), BF16)],
        compiler_params=_params(("arbitrary", "arbitrary")),
    )(x, w, w_t)


def _t5_bucket_np(n):
    n = np.maximum(n, 0)
    max_exact = N_BUCKETS // 2
    nf = np.maximum(n, 1).astype(np.float32)
    large = max_exact + (np.log(nf / max_exact) / math.log(MAX_DISTANCE / max_exact)
                         * (N_BUCKETS - max_exact)).astype(np.int32)
    large = np.minimum(large, N_BUCKETS - 1)
    return np.where(n < max_exact, n, large).astype(np.int32)


def _dsa_n_off(blk):
    return 2 + -(-MAX_DISTANCE // blk)


def _dsa_bucket_tiles(blk):
    n_off = _dsa_n_off(blk)
    j = np.arange(blk)[None, :, None]
    i = np.arange(blk)[None, None, :]
    o = np.arange(n_off)[:, None, None]
    return _t5_bucket_np(i - j + blk * (n_off - 1 - o))


def _dsa_kernel(bucket_ref, relb_ref, q_ref, k_ref, vt_ref, qi_ref, kk_ref, wi_ref, o_ref,
                sc_ref, bias_ref, m_ref, l_ref, acc_ref, st_ref, mx_ref,
                *, blk, n_sel, n_off):
    b = pl.program_id(0)
    i = pl.program_id(1)
    q0 = i * blk
    nck = i + 1
    groups = blk // SUBLANES

    @pl.when(jnp.logical_and(b == 0, i == 0))
    def _build_bias():
        def head_body(h, _):
            for o in range(n_off):
                for rb in range(blk // LANES):
                    for cb in range(blk // LANES):
                        rs = slice(rb * LANES, (rb + 1) * LANES)
                        cs = slice(cb * LANES, (cb + 1) * LANES)
                        bk = bucket_ref[o, rs, cs]

                        def bucket_body(n, acc):
                            return jnp.where(bk == n, relb_ref[n, h], acc)

                        bias_ref[h, o, rs, cs] = lax.fori_loop(
                            0, N_BUCKETS, bucket_body, jnp.zeros((LANES, LANES), F32))
            return 0

        lax.fori_loop(0, A_HEADS, head_body, 0)

    lane = lax.broadcasted_iota(I32, (blk, LANES), 1)
    lo_half = lane < HALF
    krow = lax.broadcasted_iota(I32, (blk, blk), 0)
    qpos = q0 + lax.broadcasted_iota(I32, (1, blk), 1)

    def pair_split(ref, scale):
        out = []
        for p in range(N_PAIRS):
            v = ref[0, :, p * LANES:(p + 1) * LANES].astype(F32)
            if scale != 1.0:
                v = v * scale
            out.append(jnp.where(lo_half, v, 0.0).astype(BF16))
            out.append(jnp.where(lo_half, 0.0, v).astype(BF16))
        return out

    wi_t = wi_ref[0].astype(F32).T
    wrow = [wi_t[h:h + 1, :] * IDX_SCALE for h in range(IDX_HEADS)]
    qi_m = pair_split(qi_ref, 1.0)

    def part(x, op):
        y = op(x.reshape(REDUCE_CHAINS, groups // REDUCE_CHAINS, SUBLANES, blk), axis=1)
        return op(y, axis=0)

    def fin(x, op):
        return op(x, axis=0, keepdims=True)

    zeros8 = jnp.zeros((SUBLANES, blk), F32)
    pinf8 = jnp.full((SUBLANES, blk), jnp.inf, F32)

    def score_chunk(c, mn_mx):
        c0 = pl.multiple_of(c * blk, blk)
        kk = kk_ref[0, pl.ds(c0, blk), :]
        acc = jnp.zeros((blk, blk), F32)
        for h in range(IDX_HEADS):
            s = lax.dot_general(kk, qi_m[h], NT_DIMS, preferred_element_type=F32)
            acc = acc + wrow[h] * jnp.maximum(s, 0.0)
        causal = c0 + krow <= qpos
        sc_ref[c] = jnp.where(causal, acc, -jnp.inf)
        return (jnp.minimum(mn_mx[0], part(jnp.where(causal, acc, jnp.inf), jnp.min)),
                jnp.maximum(mn_mx[1], part(jnp.where(causal, acc, -jnp.inf), jnp.max)))

    mn, mx = lax.fori_loop(0, nck, score_chunk, (pinf8, -pinf8))
    rmin = fin(mn, jnp.min)
    rmax = fin(mx, jnp.max)

    kt = jnp.minimum(qpos + 1, n_sel).astype(F32)

    def fold(fn, init):
        def body(c, acc):
            return fn(acc, sc_ref[c])
        return lax.fori_loop(0, nck, body, init)

    def count_ge(th):
        return fin(fold(lambda a, s: a + part(jnp.where(s >= th, 1.0, 0.0), jnp.sum), zeros8),
                   jnp.sum)

    def bis_cond(st):
        it, lo, hi, clo = st
        return jnp.logical_and(it < BISECT_CAP, jnp.max(jnp.abs(clo - kt)) > 0.0)

    def halve(lo, hi, clo):
        mid = 0.5 * lo + 0.5 * hi
        c = count_ge(mid)
        active = clo != kt
        up = jnp.logical_and(active, c >= kt)
        dn = jnp.logical_and(active, c < kt)
        return jnp.where(up, mid, lo), jnp.where(dn, mid, hi), jnp.where(up, c, clo)

    def bis_body(st):
        it, lo, hi, clo = st
        return (it + 2,) + halve(*halve(lo, hi, clo))

    start = (rmin, rmax + jnp.maximum(1.0, jnp.abs(rmax) * 2.0 ** -20), (qpos + 1).astype(F32))
    start = lax.fori_loop(0, BISECT_FREE, lambda _, st: halve(*st), start)
    _, lo, _, clo = lax.while_loop(bis_cond, bis_body, (jnp.int32(BISECT_FREE),) + start)
    open_rows = jnp.max(jnp.abs(clo - kt)) > 0.0

    def stats(lo_):
        a_ = fin(fold(lambda a, s: jnp.minimum(a, part(jnp.where(s >= lo_, s, jnp.inf), jnp.min)),
                      pinf8), jnp.min)
        cg, ct, nx = fold(
            lambda a, s: (a[0] + part(jnp.where(s > a_, 1.0, 0.0), jnp.sum),
                               a[1] + part(jnp.where(s == a_, 1.0, 0.0), jnp.sum),
                               jnp.minimum(a[2], part(jnp.where(s > a_, s, jnp.inf), jnp.min))),
            (zeros8, zeros8, pinf8))
        return a_, fin(cg, jnp.sum), fin(ct, jnp.sum), fin(nx, jnp.min)

    def fin_cond(st):
        return st[0]

    def fin_body(st):
        _, lo_, _, _ = st
        a_, cgt_, nt_, nxt_ = stats(lo_)
        bad = cgt_ >= kt
        return (jnp.max(jnp.where(bad, 1.0, 0.0)) > 0.0, jnp.where(bad, nxt_, a_), cgt_, nt_)

    def exact_finish():
        _, a_, cgt, nties = lax.while_loop(fin_cond, fin_body, (jnp.bool_(True), lo, kt, kt))
        need_ = kt - cgt
        return a_, need_, jnp.max(jnp.where(nties > need_, 1.0, 0.0)) > 0.0

    a, need, excess = lax.cond(open_rows, exact_finish, lambda: (lo, kt, jnp.bool_(False)))

    def mask_plain():
        def body(c, _):
            sc_ref[c] = jnp.where(sc_ref[c] >= a, 0.0, NEG)
            return 0
        lax.fori_loop(0, nck, body, 0)

    def mask_ties():
        upto = (krow >= lax.broadcasted_iota(I32, (blk, blk), 1)).astype(BF16)

        def body(c, seen):
            s = sc_ref[c]
            tie = s == a
            rank = jnp.dot(upto, jnp.where(tie, 1.0, 0.0).astype(BF16),
                           preferred_element_type=F32) + seen
            sel = jnp.logical_or(s > a, jnp.logical_and(tie, rank <= need))
            sc_ref[c] = jnp.where(sel, 0.0, NEG)
            return rank[blk - 1:blk, :]

        lax.fori_loop(0, nck, body, jnp.zeros((1, blk), F32))

    lax.cond(excess, mask_ties, mask_plain)

    m_ref[...] = jnp.full(m_ref.shape, NEG, F32)
    l_ref[...] = jnp.zeros(l_ref.shape, F32)
    acc_ref[...] = jnp.zeros(acc_ref.shape, F32)
    q_m = pair_split(q_ref, QK_SCALE)

    def stage_logits(c, slot):
        c0 = pl.multiple_of(c * blk, blk)
        madd = sc_ref[c]
        o_idx = jnp.maximum(c - i + (n_off - 1), 0)
        for p in range(N_PAIRS):
            k2 = k_ref[0, pl.ds(c0, blk), p * LANES:(p + 1) * LANES]
            for hh in range(2):
                h = 2 * p + hh
                s = lax.dot_general(k2, q_m[h], NT_DIMS, preferred_element_type=F32)
                s = s + bias_ref[h, o_idx] + madd
                st_ref[slot, h] = s
                mx_ref[slot, h] = fin(part(s, jnp.max), jnp.max)

    def stage_values(c, slot):
        for p in range(N_PAIRS):
            vt2 = vt_ref[0, c, p * LANES:(p + 1) * LANES, :]
            for hh in range(2):
                h = 2 * p + hh
                m_old = m_ref[h]
                m_new = jnp.maximum(m_old, mx_ref[slot, h])
                alpha = jnp.exp(m_old - m_new)
                pexp = jnp.exp(st_ref[slot, h] - m_new)
                l_ref[h] = alpha * l_ref[h] + fin(part(pexp, jnp.sum), jnp.sum)
                acc_ref[h] = alpha * acc_ref[h] + jnp.dot(vt2, pexp.astype(BF16),
                                                          preferred_element_type=F32)
                m_ref[h] = m_new

    stage_logits(0, 0)

    def att_pair(pp, _):
        c = 2 * pp
        stage_logits(c + 1, 1)
        stage_values(c, 0)
        stage_logits(c + 2, 0)
        stage_values(c + 1, 1)
        return 0

    lax.fori_loop(0, (nck - 1) // 2, att_pair, 0)

    @pl.when(lax.rem(nck, 2) == 1)
    def _last_chunk():
        stage_values(nck - 1, 0)

    @pl.when(lax.rem(nck, 2) == 0)
    def _last_two_chunks():
        stage_logits(nck - 1, 1)
        stage_values(nck - 2, 0)
        stage_values(nck - 1, 1)

    lo_rows = lax.broadcasted_iota(I32, (LANES, blk), 0) < HALF
    for p in range(N_PAIRS):
        oa = acc_ref[2 * p] / l_ref[2 * p]
        ob = acc_ref[2 * p + 1] / l_ref[2 * p + 1]
        o_ref[0, :, p * LANES:(p + 1) * LANES] = jnp.where(lo_rows, oa, ob).T.astype(o_ref.dtype)


def _dsa(proj3, vt4, rel_bias):
    B, S, _ = proj3.shape
    blk = DSA_BLK
    n_off = _dsa_n_off(blk)
    n_sel = min(TOPK_MAX, S // 4)
    bucket = jnp.asarray(_dsa_bucket_tiles(blk))
    assert S % blk == 0
    kern = functools.partial(_dsa_kernel, blk=blk, n_sel=n_sel, n_off=n_off)
    return pl.pallas_call(
        kern,
        grid=(B, S // blk),
        in_specs=[
            pl.BlockSpec((n_off, blk, blk), lambda b, i: (0, 0, 0)),
            pl.BlockSpec(memory_space=pltpu.SMEM),
            pl.BlockSpec((1, blk, ATT_W), lambda b, i: (b, i, COL_QA // ATT_W)),
            pl.BlockSpec((1, S, ATT_W), lambda b, i: (b, 0, COL_KA // ATT_W)),
            pl.BlockSpec((1, S // blk, ATT_W, blk), lambda b, i: (b, 0, 0, 0)),
            pl.BlockSpec((1, blk, ATT_W), lambda b, i: (b, i, COL_QI // ATT_W)),
            pl.BlockSpec((1, S, LANES), lambda b, i: (b, 0, COL_KK // LANES)),
            pl.BlockSpec((1, blk, LANES), lambda b, i: (b, i, COL_WI // LANES)),
        ],
        out_specs=pl.BlockSpec((1, blk, ATT_W), lambda b, i: (b, i, 0)),
        out_shape=jax.ShapeDtypeStruct((B, S, ATT_W), BF16),
        scratch_shapes=[
            pltpu.VMEM((S // blk, blk, blk), F32),
            pltpu.VMEM((A_HEADS, n_off, blk, blk), F32),
            pltpu.VMEM((A_HEADS, 1, blk), F32),
            pltpu.VMEM((A_HEADS, 1, blk), F32),
            pltpu.VMEM((A_HEADS, LANES, blk), F32),
            pltpu.VMEM((2, A_HEADS, blk, blk), F32),
            pltpu.VMEM((2, A_HEADS, 1, blk), F32),
        ],
        compiler_params=_params(("arbitrary", "arbitrary")),
    )(bucket, rel_bias, proj3, proj3, vt4, proj3, proj3, proj3)


def _sb_kernel(q_ref, k_ref, v_ref, o_ref, hl_ref, z_ref, *, t):
    i = pl.program_id(2)
    n = i + 1
    lane = lax.broadcasted_iota(I32, (t, LANES), 1)
    lo_half = lane < HALF
    q_m = []
    for pr in range(SB_PAIRS):
        q2 = q_ref[0, :, pr * LANES:(pr + 1) * LANES].astype(F32) * QK_SCALE
        q_m += [jnp.where(lo_half, q2, 0.0).astype(BF16), jnp.where(lo_half, 0.0, q2).astype(BF16)]
    heads = 2 * SB_PAIRS
    r = lax.broadcasted_iota(I32, (t, t), 0)
    cidx = lax.broadcasted_iota(I32, (t, t), 1)
    neg_from = jnp.where(r >= cidx, -1.0, 0.0).astype(BF16)
    neg_from2 = jnp.concatenate([neg_from, neg_from], axis=0)
    diff = cidx - r

    def stage_terms(step, slot, diagonal):
        c0 = pl.multiple_of((i - step) * t, t)
        if diagonal:
            keep = diff < 0
        for hh in range(heads):
            pr = hh // 2
            k2 = k_ref[0, pl.ds(c0, t), pr * LANES:(pr + 1) * LANES]
            z = lax.dot_general(q_m[hh], k2, NT_DIMS, preferred_element_type=F32)
            sp = jnp.maximum(z, 0.0) + jnp.log(1.0 + jnp.exp(-jnp.abs(z)))
            if diagonal:
                sp = jnp.where(keep, sp, 0.0)
                z = jnp.where(keep, z, NEG)
            hi = sp.astype(BF16)
            hl_ref[slot, hh, :, :t] = hi
            hl_ref[slot, hh, :, t:] = (sp - hi.astype(F32)).astype(BF16)
            z_ref[slot, hh] = z

    def stage_apply(step, slot, carry):
        c0 = pl.multiple_of((i - step) * t, t)
        cum_all = jnp.dot(hl_ref[slot].reshape(heads * t, 2 * t), neg_from2, preferred_element_type=F32)
        out = []
        for hh in range(heads):
            pr = hh // 2
            v2 = v_ref[0, pl.ds(c0, t), pr * LANES:(pr + 1) * LANES]
            car, acc = carry[hh]
            cum = cum_all[hh * t:(hh + 1) * t]
            w = jnp.exp(z_ref[slot, hh] + cum + car)
            acc = acc + jnp.dot(w.astype(BF16), v2, preferred_element_type=F32)
            out.append((car + cum[:, 0:1], acc))
        return tuple(out)

    z1 = jnp.zeros((t, 1), F32)
    za = jnp.zeros((t, LANES), F32)
    stage_terms(0, 0, True)

    def pair_body(pp, carry):
        step = 2 * pp
        stage_terms(step + 1, 1, False)
        carry = stage_apply(step, 0, carry)
        stage_terms(step + 2, 0, False)
        return stage_apply(step + 1, 1, carry)

    carry = lax.fori_loop(0, (n - 1) // 2, pair_body, ((z1, za),) * heads)

    def last_two(c):
        stage_terms(n - 1, 1, False)
        return stage_apply(n - 1, 1, stage_apply(n - 2, 0, c))

    carry = lax.cond(lax.rem(n, 2) == 1, lambda c: stage_apply(n - 1, 0, c), last_two, carry)
    for pr in range(SB_PAIRS):
        o_ref[0, :, pr * LANES:(pr + 1) * LANES] = jnp.where(
            lo_half, carry[2 * pr][1], carry[2 * pr + 1][1]).astype(o_ref.dtype)


def _stick_breaking(proj3):
    B, S, _ = proj3.shape
    t = SB_T
    w = SB_PAIRS * LANES
    qb, kb, vb = COL_QB // w, COL_KB // w, COL_VB // w
    return pl.pallas_call(
        functools.partial(_sb_kernel, t=t),
        grid=(B, N_PAIRS // SB_PAIRS, S // t),
        in_specs=[
            pl.BlockSpec((1, t, w), lambda b, p, i: (b, i, qb + p)),
            pl.BlockSpec((1, S, w), lambda b, p, i: (b, 0, kb + p)),
            pl.BlockSpec((1, S, w), lambda b, p, i: (b, 0, vb + p)),
        ],
        out_specs=pl.BlockSpec((1, t, w), lambda b, p, i: (b, i, p)),
        out_shape=jax.ShapeDtypeStruct((B, S, ATT_W), BF16),
        scratch_shapes=[
            pltpu.VMEM((2, 2 * SB_PAIRS, t, 2 * t), BF16),
            pltpu.VMEM((2, 2 * SB_PAIRS, t, t), F32),
        ],
        compiler_params=_params(("arbitrary", "arbitrary", "arbitrary")),
    )(proj3, proj3, proj3)


def _layer_norm(r, g, b):
    mu = jnp.mean(r, axis=-1, keepdims=True)
    d = r - mu
    var = jnp.mean(d * d, axis=-1, keepdims=True)
    return d * lax.rsqrt(var + LN_EPS) * g + b


def _split_bf16(v):
    hi = v.astype(BF16)
    return hi, (v - hi.astype(F32)).astype(BF16)


def _pack_bf16_pairs(v):
    half = v.shape[1] // 2
    bits = pltpu.bitcast(v.astype(BF16).astype(F32), jnp.uint32)
    return bits[:, :half] | (bits[:, half:] >> 16)


def _unpack_bf16_pairs(w):
    return jnp.concatenate([pltpu.bitcast(w & jnp.uint32(0xFFFF0000), F32),
                            pltpu.bitcast(w << 16, F32)], axis=1)


def _merge_kernel(x_ref, ya_ref, yb_ref, wg_ref, wa_ref, wb_ref, wo_ref, g_ref, b_ref,
                  wr_ref, br_ref, h_ref, hp_ref, e_ref, p_ref, *, alpha, d):
    pa = jnp.dot(ya_ref[...], wa_ref[...], preferred_element_type=F32)
    pb = jnp.dot(yb_ref[...], wb_ref[...], preferred_element_type=F32)
    gates = jnp.dot(x_ref[...].astype(BF16), wg_ref[...], preferred_element_type=F32)
    merged = jax.nn.sigmoid(gates[:, :d]) * pa + jax.nn.sigmoid(gates[:, d:]) * pb
    m = jnp.dot(merged.astype(BF16), wo_ref[...], preferred_element_type=F32)
    h = _layer_norm(alpha * x_ref[...] + m, g_ref[...], b_ref[...])
    h_ref[...] = h
    hp_ref[...] = _pack_bf16_pairs(h)

    h_hi, h_lo = _split_bf16(h)
    w_hi, w_lo = _split_bf16(wr_ref[...])
    logit = (lax.dot_general(w_hi, h_hi, NT_DIMS, preferred_element_type=F32)
             + lax.dot_general(w_hi, h_lo, NT_DIMS, preferred_element_type=F32)
             + lax.dot_general(w_lo, h_hi, NT_DIMS, preferred_element_type=F32)) + br_ref[...]
    eid = lax.broadcasted_iota(I32, logit.shape, 0)
    vals, ids = [], []
    for _ in range(TOP_K):
        mx = jnp.max(logit, axis=0, keepdims=True)
        am = jnp.min(jnp.where(logit == mx, eid, N_EXPERTS), axis=0, keepdims=True)
        vals.append(mx)
        ids.append(am)
        logit = jnp.where(eid == am, -jnp.inf, logit)
    ex = [jnp.exp(v - vals[0]) for v in vals]
    den = ex[0] + ex[1] + ex[2] + ex[3]
    for k in range(TOP_K):
        e_ref[k:k + 1, :] = ids[k]
        p_ref[k:k + 1, :] = ex[k] / den


def _merge(x2, ya, yb, wg, wa, wb, wo, g, b, wr_t, br, alpha):
    T, D = x2.shape
    tm = MERGE_TM
    row = lambda i: (i, 0)
    fixed = lambda i: (0, 0)
    return pl.pallas_call(
        functools.partial(_merge_kernel, alpha=alpha, d=D),
        grid=(T // tm,),
        in_specs=[
            pl.BlockSpec((tm, D), row),
            pl.BlockSpec((tm, ATT_W), row),
            pl.BlockSpec((tm, ATT_W), row),
            pl.BlockSpec((D, 2 * D), fixed),
            pl.BlockSpec((ATT_W, D), fixed),
            pl.BlockSpec((ATT_W, D), fixed),
            pl.BlockSpec((D, D), fixed),
            pl.BlockSpec((1, D), fixed),
            pl.BlockSpec((1, D), fixed),
            pl.BlockSpec((N_EXPERTS, D), fixed),
            pl.BlockSpec((N_EXPERTS, 1), fixed),
        ],
        out_specs=[
            pl.BlockSpec((tm, D), row),
            pl.BlockSpec((tm, D // 2), row),
            pl.BlockSpec((TOP_K, tm), lambda i: (0, i)),
            pl.BlockSpec((TOP_K, tm), lambda i: (0, i)),
        ],
        out_shape=[
            jax.ShapeDtypeStruct((T, D), F32),
            jax.ShapeDtypeStruct((T, D // 2), jnp.uint32),
            jax.ShapeDtypeStruct((TOP_K, T), I32),
            jax.ShapeDtypeStruct((TOP_K, T), F32),
        ],
        compiler_params=_params(("arbitrary",)),
    )(x2, ya, yb, wg, wa, wb, wo, g, b, wr_t, br)


def _sc_gather_rows(table, idx):
    n = idx.shape[0]
    d = table.shape[1]
    info = plsc.get_sparse_core_info()
    n_cores, n_sub = info.num_cores, info.num_subcores
    per_w = n // (n_cores * n_sub)
    c = SC_GATHER_BYTES // (d * table.dtype.itemsize)
    n_g = per_w // c
    assert n == per_w * n_cores * n_sub and per_w == n_g * c and n_g % 2 == 0 and n_g >= 2
    mesh = plsc.VectorSubcoreMesh(core_axis_name="c", subcore_axis_name="s")

    @functools.partial(
        pl.kernel, mesh=mesh, out_type=jax.ShapeDtypeStruct((n, d), table.dtype),
        scratch_types=[pltpu.VMEM((per_w,), I32), pltpu.VMEM((2, c, d), table.dtype),
                       pltpu.SemaphoreType.DMA((2,)), pltpu.SemaphoreType.DMA((2,))])
    def gather_kernel(table_hbm, idx_hbm, out_hbm, idx_v, rows_v, gsem, wsem):
        base = (lax.axis_index("s") * n_cores + lax.axis_index("c")) * per_w
        pltpu.sync_copy(idx_hbm.at[pl.ds(base, per_w)], idx_v)

        def gather(g, b):
            return pltpu.make_async_copy(table_hbm.at[idx_v.at[pl.ds(g * c, c)]], rows_v.at[b],
                                         gsem.at[b])

        def write(g, b):
            return pltpu.make_async_copy(rows_v.at[b], out_hbm.at[pl.ds(base + g * c, c)], wsem.at[b])

        gather(0, 0).start()

        @pl.loop(0, n_g, step=2)
        def _ring(g0):
            for b in range(2):
                g = g0 + b

                @pl.when(g + 1 < n_g)
                def _next():
                    @pl.when(g >= 1)
                    def _buffer_free():
                        write(g - 1, 1 - b).wait()
                    gather(g + 1, 1 - b).start()

                gather(g, b).wait()
                write(g, b).start()

        write(n_g - 2, 0).wait()
        write(n_g - 1, 1).wait()

    return gather_kernel(table, idx)


def _moe_kernel(blk_e_ref, nused_ref, x_ref, wgu_ref, bgu_ref, wdn_ref, bdn_ref, o_ref,
                wgu_s, wdn_s, *, f):
    i = pl.program_id(0)
    nused = nused_ref[0]

    @pl.when(i < nused)
    def _compute():
        changed = jnp.logical_or(i == 0, blk_e_ref[i] != blk_e_ref[jnp.maximum(i - 1, 0)])

        @pl.when(changed)
        def _cast_weights():
            wgu_s[...] = wgu_ref[0].astype(BF16)
            wdn_s[...] = wdn_ref[0].astype(BF16)

        x = _unpack_bf16_pairs(x_ref[...]).astype(BF16)
        fs = f // FFN_HIDDEN_SLABS
        y = bdn_ref[0]
        for j in range(FFN_HIDDEN_SLABS):
            a = jnp.dot(x, wgu_s[:, j * fs:(j + 1) * fs], preferred_element_type=F32) \
                + bgu_ref[0][:, j * fs:(j + 1) * fs]
            u = jnp.dot(x, wgu_s[:, f + j * fs:f + (j + 1) * fs], preferred_element_type=F32) \
                + bgu_ref[0][:, f + j * fs:f + (j + 1) * fs]
            a = jnp.minimum(a, SWIGLU_LIMIT)
            u = jnp.clip(u, -SWIGLU_LIMIT, SWIGLU_LIMIT)
            glu = a * jax.nn.sigmoid(a * SWIGLU_ALPHA)
            y = y + jnp.dot(((u + 1.0) * glu).astype(BF16), wdn_s[j * fs:(j + 1) * fs, :],
                            preferred_element_type=F32)
        o_ref[...] = _pack_bf16_pairs(y)

    @pl.when(i >= nused)
    def _unused_block():
        o_ref[...] = jnp.zeros(o_ref.shape, o_ref.dtype)


def _moe_ffn(xs, blk_e, nused, w_gu, b_gu, w_dn, b_dn):
    P = xs.shape[0]
    E, D, F2 = w_gu.shape
    f = F2 // 2
    blk = MOE_BLK
    nb = P // blk
    used_block = lambda i, be, nu: (jnp.minimum(i, nu[0] - 1), 0)
    grid_spec = pltpu.PrefetchScalarGridSpec(
        num_scalar_prefetch=2,
        grid=(nb,),
        in_specs=[
            pl.BlockSpec((blk, D // 2), used_block),
            pl.BlockSpec((1, D, F2), lambda i, be, nu: (be[i], 0, 0)),
            pl.BlockSpec((1, 1, F2), lambda i, be, nu: (be[i], 0, 0)),
            pl.BlockSpec((1, f, D), lambda i, be, nu: (be[i], 0, 0)),
            pl.BlockSpec((1, 1, D), lambda i, be, nu: (be[i], 0, 0)),
        ],
        out_specs=pl.BlockSpec((blk, D // 2), lambda i, be, nu: (i, 0)),
        scratch_shapes=[
            pltpu.VMEM((D, F2), BF16),
            pltpu.VMEM((f, D), BF16),
        ],
    )
    return pl.pallas_call(
        functools.partial(_moe_kernel, f=f),
        grid_spec=grid_spec,
        out_shape=jax.ShapeDtypeStruct((P, D // 2), jnp.uint32),
        compiler_params=_params(("arbitrary",)),
    )(blk_e, nused, xs, w_gu, b_gu.reshape(E, 1, F2), w_dn, b_dn.reshape(E, 1, D))


def _comb_kernel(h_ref, p_ref, y_ref, g_ref, b_ref, o_ref, *, alpha):
    gate = p_ref[...]
    y = [_unpack_bf16_pairs(y_ref[k]) for k in range(TOP_K)]
    fsum = (y[0] * gate[:, 0:1] + y[1] * gate[:, 1:2]) + (y[2] * gate[:, 2:3] + y[3] * gate[:, 3:4])
    o_ref[...] = _layer_norm(alpha * h_ref[...] + fsum, g_ref[...], b_ref[...])


def _combine(h, y4, top_p, g, b, alpha):
    T, D = h.shape
    tm = COMB_TM
    return pl.pallas_call(
        functools.partial(_comb_kernel, alpha=alpha),
        grid=(T // tm,),
        in_specs=[
            pl.BlockSpec((tm, D), lambda i: (i, 0)),
            pl.BlockSpec((tm, TOP_K), lambda i: (i, 0)),
            pl.BlockSpec((TOP_K, tm, D // 2), lambda i: (0, i, 0)),
            pl.BlockSpec((1, D), lambda i: (0, 0)),
            pl.BlockSpec((1, D), lambda i: (0, 0)),
        ],
        out_specs=pl.BlockSpec((tm, D), lambda i: (i, 0)),
        out_shape=jax.ShapeDtypeStruct((T, D), F32),
        compiler_params=_params(("arbitrary",)),
    )(h, top_p.T, y4, g, b)


def _route(top_e, blk):
    K, T = top_e.shape
    N = K * T
    flat_e = top_e.reshape(N)
    experts = jnp.arange(N_EXPERTS, dtype=I32)
    order = jnp.argsort(flat_e, stable=True).astype(I32)
    inv = jnp.argsort(order).astype(I32)
    onehot = flat_e[:, None] == experts[None, :]
    counts = jnp.sum(onehot, axis=0, dtype=I32)
    padded = (counts + blk - 1) // blk * blk
    pends = jnp.cumsum(padded)
    offs = jnp.cumsum(counts) - counts
    shift = (pends - padded) - offs
    pos = inv + jnp.sum(jnp.where(onehot, shift[None, :], 0), axis=1, dtype=I32)
    P = N + N_EXPERTS * blk
    nb = P // blk
    blk_start = jnp.arange(nb, dtype=I32) * blk
    blk_e = jnp.minimum(jnp.sum(pends[None, :] <= blk_start[:, None], axis=1, dtype=I32), N_EXPERTS - 1)
    j = (blk_start - shift[blk_e])[:, None] + jnp.arange(blk, dtype=I32)[None, :]
    valid = j < (offs + counts)[blk_e][:, None]
    src = order[jnp.clip(j, 0, N - 1)]
    row_tok = jnp.where(valid, src % T, j % T).reshape(P)
    nused = (pends[-1:] // blk).astype(I32)
    return blk_e, nused, row_tok, pos.reshape(K, T)


def _projection_weights(w_in_l):
    sizes = (ATT_W, ATT_W, ATT_W, IDX_HEADS * IDX_DIM, IDX_DIM, IDX_HEADS, ATT_W, ATT_W, ATT_W)
    offs = np.concatenate([[0], np.cumsum(sizes)])
    qa, ka, va, qi, ki, wi, qb, kb, vb = (w_in_l[:, offs[n]:offs[n + 1]] for n in range(9))
    pad_wi = jnp.zeros((w_in_l.shape[0], LANES - IDX_HEADS), w_in_l.dtype)
    w_att = jnp.concatenate([qa, ka, qi, qb, kb, vb, ki, ki, wi, pad_wi], axis=1).astype(BF16)
    w_va_t = va.T.astype(BF16)
    w_gate = w_in_l[:, offs[9]:].astype(BF16)
    return w_att, w_va_t, w_gate


def kernel(x, w_in, w_branch_a, w_branch_b, w_out, rel_bias, ln1_g, ln1_b, w_router, b_router,
           w_gate_up, b_gate_up, w_down, b_down, ln2_g, ln2_b):
    B, S, D = x.shape
    depth = w_in.shape[0]
    alpha = (2 * depth) ** 0.25
    T = B * S
    h = x.reshape(T, D)
    for l in range(depth):
        w_att, w_va_t, w_gate = _projection_weights(w_in[l])
        proj, vt = _projection(h, w_att, w_va_t, min(T, PROJ_TM), PROJ_TN, DSA_BLK)
        proj = proj.reshape(B, S, ATT_COLS)
        vt = vt.reshape(B, S // DSA_BLK, ATT_W, DSA_BLK)
        ya = _dsa(proj, vt, rel_bias).reshape(T, ATT_W)
        yb = _stick_breaking(proj).reshape(T, ATT_W)
        h1, h1_packed, top_e, top_p = _merge(
            h, ya, yb, w_gate, w_branch_a[l].astype(BF16), w_branch_b[l].astype(BF16),
            w_out[l].astype(BF16), ln1_g[l].reshape(1, D), ln1_b[l].reshape(1, D),
            w_router[l].T, b_router[l].reshape(N_EXPERTS, 1), alpha)
        blk_e, nused, row_tok, pos = _route(top_e, MOE_BLK)
        xs = _sc_gather_rows(h1_packed, row_tok)
        ys = _moe_ffn(xs, blk_e, nused, w_gate_up[l], b_gate_up[l], w_down[l], b_down[l])
        y4 = _sc_gather_rows(ys, pos.reshape(TOP_K * T)).reshape(TOP_K, T, D // 2)
        h = _combine(h1, y4, top_p, ln2_g[l].reshape(1, D), ln2_b[l].reshape(1, D), alpha)
    return h.reshape(B, S, D)
```
